```python
import math
import jax, jax.numpy as jnp
from jax import lax
import numpy as np


D_MODEL = 1024
BATCH = 16
SEQ = 2048
DEPTH = 2

N_META = 16
MLA_HEADS = 8
MLA_NOPE = 64
MLA_ROPE = 32
MLA_V = 64
MLA_Q_RANK = 256
MLA_KV_RANK = 256
ROPE_THETA = 10000.0
Q_BLOCK = 128
MASK_VALUE = -1e9
S5_WIDTH = 512
S5_GROUP = 16
S5_GROUPS = S5_WIDTH // S5_GROUP
S5_STATE = 64
S5_DT_MIN = 1e-3
S5_DT_MAX = 1e-1
HG_HEADS = 4
HG_KEY = 128
HG_VAL = 128
HG_CHUNK = 64
HG_QK = HG_HEADS * HG_KEY
HG_VW = HG_HEADS * HG_VAL
HG_F_MIN = 1e-6
N_BRANCH = 3
D_FF = -(-(8 * D_MODEL) // (3 * 256)) * 256
ALPHA = (2 * DEPTH) ** 0.25
BETA = (8 * DEPTH) ** -0.25
MLA_IN = MLA_Q_RANK + MLA_KV_RANK + MLA_ROPE
S5_IN = S5_WIDTH
HG_IN = 2 * HG_QK + 2 * HG_VW
GATE_IN = N_BRANCH * D_MODEL
D_IN = MLA_IN + S5_IN + HG_IN + GATE_IN
SPLIT_IN = [MLA_IN, MLA_IN + S5_IN, MLA_IN + S5_IN + HG_IN]

kernel_name = 'hybrid_mla_s5_hgrn2_deepnorm_meta'


def layer_norm(x, g, b, eps=1e-5):
    xf = x.astype(jnp.float32)
    mu = jnp.mean(xf, axis=-1, keepdims=True)
    var = jnp.mean(jnp.square(xf - mu), axis=-1, keepdims=True)
    return ((xf - mu) * lax.rsqrt(var + eps) * g.astype(jnp.float32) + b.astype(jnp.float32)).astype(x.dtype)


def rms_norm(x, g, eps=1e-6):
    xf = x.astype(jnp.float32)
    return (xf * lax.rsqrt(jnp.mean(jnp.square(xf), axis=-1, keepdims=True) + eps) * g.astype(jnp.float32)).astype(x.dtype)


def rope_tables(pos):
    inv = ROPE_THETA ** (-(jnp.arange(0, MLA_ROPE, 2, dtype=jnp.float32) / MLA_ROPE))
    ang = pos.astype(jnp.float32)[..., None] * inv
    return jnp.cos(ang), jnp.sin(ang)


def apply_rope(x, cos, sin):
    x1, x2 = jnp.split(x.astype(jnp.float32), 2, axis=-1)
    return jnp.concatenate([x1 * cos - x2 * sin, x1 * sin + x2 * cos], axis=-1).astype(x.dtype)


def mla_mixer(z, cos, sin, q_norm, w_uq, kv_norm, w_ukv):
    B_, L_, _ = z.shape
    c_q, c_kv, k_r = jnp.split(z, [MLA_Q_RANK, MLA_Q_RANK + MLA_KV_RANK], axis=-1)
    q = (rms_norm(c_q, q_norm) @ w_uq).reshape(B_, L_, MLA_HEADS, MLA_NOPE + MLA_ROPE)
    q_nope, q_rope = jnp.split(q, [MLA_NOPE], axis=-1)
    q_rope = apply_rope(q_rope, cos[:, :, None, :], sin[:, :, None, :])
    kv = (rms_norm(c_kv, kv_norm) @ w_ukv).reshape(B_, L_, MLA_HEADS, MLA_NOPE + MLA_V)
    k_nope, v = jnp.split(kv, [MLA_NOPE], axis=-1)
    k_rope = apply_rope(k_r, cos, sin)
    scale = (MLA_NOPE + MLA_ROPE) ** -0.5
    key_idx = jnp.arange(L_)

    def attend(qn, qr, q_start):
        s = (jnp.einsum('bqhd,bkhd->bhqk', qn, k_nope) + jnp.einsum('bqhr,bkr->bhqk', qr, k_rope)).astype(jnp.float32) * scale
        q_idx = q_start + jnp.arange(qn.shape[1])
        s = jnp.where(key_idx[None, :] <= q_idx[:, None], s, MASK_VALUE)
        p = jax.nn.softmax(s, axis=-1).astype(v.dtype)
        return jnp.einsum('bhqk,bkhd->bqhd', p, v)

    o_meta = attend(q_nope[:, :N_META], q_rope[:, :N_META], 0).reshape(B_, N_META, MLA_HEADS * MLA_V)
    n_blk = (L_ - N_META) // Q_BLOCK

    def blocks(t):
        return t[:, N_META:].reshape((B_, n_blk, Q_BLOCK) + t.shape[2:]).swapaxes(0, 1)

    starts = N_META + Q_BLOCK * jnp.arange(n_blk, dtype=jnp.int32)
    o_real = lax.map(lambda a: attend(a[0], a[1], a[2]), (blocks(q_nope), blocks(q_rope), starts))
    o_real = o_real.swapaxes(0, 1).reshape(B_, L_ - N_META, MLA_HEADS * MLA_V)
    return jnp.concatenate([o_meta, o_real], axis=1)


def s5_mixer(u, lam_re, lam_im, log_dt, b_re, b_im, c_re, c_im, d_skip, w_glu):
    f32 = jnp.float32
    B_, L_, _ = u.shape
    lr = jnp.minimum(lam_re.astype(f32), -1e-4)
    li = lam_im.astype(f32)
    dt = jnp.exp(log_dt.astype(f32))[:, None]
    mag = jnp.exp(lr * dt)
    ab_r = mag * jnp.cos(li * dt)
    ab_i = mag * jnp.sin(li * dt)
    den = lr * lr + li * li
    nr = ab_r - 1.0
    coef_r = ((nr * lr + ab_i * li) / den)[..., None]
    coef_i = ((ab_i * lr - nr * li) / den)[..., None]
    bb_r = coef_r * b_re.astype(f32) - coef_i * b_im.astype(f32)
    bb_i = coef_r * b_im.astype(f32) + coef_i * b_re.astype(f32)
    uf = u.astype(f32)
    ug = uf.reshape(B_, L_, S5_GROUPS, S5_GROUP)
    bu_r = jnp.einsum('blgc,gnc->blgn', ug, bb_r)
    bu_i = jnp.einsum('blgc,gnc->blgn', ug, bb_i)
    a_r = jnp.broadcast_to(ab_r[None, None], (1, L_, S5_GROUPS, S5_STATE))
    a_i = jnp.broadcast_to(ab_i[None, None], (1, L_, S5_GROUPS, S5_STATE))

    def combine(e1, e2):
        a1r, a1i, b1r, b1i = e1
        a2r, a2i, b2r, b2i = e2
        return (a2r * a1r - a2i * a1i, a2r * a1i + a2i * a1r,
                a2r * b1r - a2i * b1i + b2r, a2r * b1i + a2i * b1r + b2i)

    _, _, x_r, x_i = lax.associative_scan(combine, (a_r, a_i, bu_r, bu_i), axis=1)
    y = jnp.einsum('blgn,gcn->blgc', x_r, c_re.astype(f32)) - jnp.einsum('blgn,gcn->blgc', x_i, c_im.astype(f32))
    y = y.reshape(B_, L_, S5_WIDTH) + d_skip.astype(f32) * uf
    y = jax.nn.gelu(y)
    y = y * jax.nn.sigmoid(y @ w_glu.astype(f32))
    return y.astype(u.dtype)


def hgrn2_chunk(state, q, k, v, log_f):
    cum = jnp.cumsum(log_f, axis=2)
    n = q.shape[2]
    causal = jnp.tril(jnp.ones((n, n), dtype=bool))[None, None, :, :, None]
    rel = cum[:, :, :, None, :] - cum[:, :, None, :, :]
    decay = jnp.where(causal, jnp.exp(jnp.minimum(rel, 0.0)), 0.0)
    scores = jnp.einsum('bhtk,bhsk,bhtsk->bhts', q, k, decay)
    out = jnp.einsum('bhts,bhsv->bhtv', scores, v) + jnp.einsum('bhtk,bhkv->bhtv', q * jnp.exp(cum), state)
    last = cum[:, :, -1:, :]
    new_state = jnp.exp(last[:, :, 0, :, None]) * state + jnp.einsum('bhsk,bhsv->bhkv', k * jnp.exp(last - cum), v)
    return new_state, out


def hgrn2_mixer(z, lb, out_norm):
    f32 = jnp.float32
    B_, L_, _ = z.shape
    q, zf, v, g = jnp.split(z.astype(f32), [HG_QK, 2 * HG_QK, 2 * HG_QK + HG_VW], axis=-1)
    lb = lb.astype(f32)
    f = lb + (1.0 - lb) * jax.nn.sigmoid(zf)
    log_f = jnp.log(jnp.maximum(f, HG_F_MIN))
    k = (1.0 - lb) * jax.nn.sigmoid(-zf)

    def heads(t, d):
        return t.reshape(B_, L_, HG_HEADS, d).transpose(0, 2, 1, 3)

    q, k, log_f, v = heads(q, HG_KEY), heads(k, HG_KEY), heads(log_f, HG_KEY), heads(v, HG_VAL)
    s0 = jnp.zeros((B_, HG_HEADS, HG_KEY, HG_VAL), f32)
    s_meta, o_meta = hgrn2_chunk(s0, q[:, :, :N_META], k[:, :, :N_META], v[:, :, :N_META], log_f[:, :, :N_META])
    n_chunks = (L_ - N_META) // HG_CHUNK

    def to_chunks(t):
        return t[:, :, N_META:].reshape(B_, HG_HEADS, n_chunks, HG_CHUNK, t.shape[-1]).transpose(2, 0, 1, 3, 4)

    def step(s, xs):
        return hgrn2_chunk(s, xs[0], xs[1], xs[2], xs[3])

    _, o_real = lax.scan(step, s_meta, (to_chunks(q), to_chunks(k), to_chunks(v), to_chunks(log_f)))
    o_real = o_real.transpose(1, 2, 0, 3, 4).reshape(B_, HG_HEADS, L_ - N_META, HG_VAL)
    o = jnp.concatenate([o_meta, o_real], axis=2).transpose(0, 2, 1, 3)
    o = rms_norm(o, out_norm.reshape(HG_HEADS, HG_VAL))
    return (o.reshape(B_, L_, HG_VW) * jax.nn.silu(g)).astype(z.dtype)


def setup_inputs(seed: int = 0) -> dict:
    key = jax.random.key(seed)
    ks = iter(jax.random.split(key, 40))
    f32 = jnp.float32

    def nrm(shape, scale):
        return scale * jax.random.normal(next(ks), shape, f32)

    def gain(shape):
        return 1.0 + nrm(shape, 0.02)

    L = DEPTH
    x = nrm((BATCH, SEQ, D_MODEL), 1.0)
    positions = jnp.broadcast_to(jnp.arange(SEQ, dtype=jnp.int32)[None], (BATCH, SEQ))
    meta_tokens = nrm((N_META, D_MODEL), 1.0)
    ln_in_g = gain((D_MODEL,))
    ln_in_b = nrm((D_MODEL,), 0.02)
    w_in = nrm((L, D_MODEL, D_IN), D_MODEL ** -0.5)
    mla_q_norm = gain((L, MLA_Q_RANK))
    mla_w_uq = nrm((L, MLA_Q_RANK, MLA_HEADS * (MLA_NOPE + MLA_ROPE)), MLA_Q_RANK ** -0.5)
    mla_kv_norm = gain((L, MLA_KV_RANK))
    mla_w_ukv = nrm((L, MLA_KV_RANK, MLA_HEADS * (MLA_NOPE + MLA_V)), MLA_KV_RANK ** -0.5)
    s5_lam_re = -0.5 + nrm((L, S5_GROUPS, S5_STATE), 0.01)
    s5_lam_im = jnp.pi * jnp.arange(S5_STATE, dtype=f32)[None, None, :] + nrm((L, S5_GROUPS, S5_STATE), 0.01)
    s5_log_dt = jax.random.uniform(next(ks), (L, S5_GROUPS), f32, math.log(S5_DT_MIN), math.log(S5_DT_MAX))
    s5_b_re = nrm((L, S5_GROUPS, S5_STATE, S5_GROUP), (2 * S5_GROUP) ** -0.5)
    s5_b_im = nrm((L, S5_GROUPS, S5_STATE, S5_GROUP), (2 * S5_GROUP) ** -0.5)
    s5_c_re = nrm((L, S5_GROUPS, S5_GROUP, S5_STATE), S5_STATE ** -0.5)
    s5_c_im = nrm((L, S5_GROUPS, S5_GROUP, S5_STATE), S5_STATE ** -0.5)
    s5_d = nrm((L, S5_WIDTH), 1.0)
    s5_w_glu = nrm((L, S5_WIDTH, S5_WIDTH), S5_WIDTH ** -0.5)
    hg_lb_logits = nrm((L, HG_QK), 0.1)
    hg_out_norm = gain((L, HG_VW))
    w_br_mla = nrm((L, MLA_HEADS * MLA_V, D_MODEL), BETA * (MLA_HEADS * MLA_V) ** -0.5)
    w_br_s5 = nrm((L, S5_WIDTH, D_MODEL), BETA * S5_WIDTH ** -0.5)
    w_br_hg = nrm((L, HG_VW, D_MODEL), BETA * HG_VW ** -0.5)
    w_out = nrm((L, D_MODEL, D_MODEL), BETA * D_MODEL ** -0.5)
    ln1_g = gain((L, D_MODEL))
    ln1_b = nrm((L, D_MODEL), 0.02)
    w_ffn_gate = nrm((L, D_MODEL, D_FF), D_MODEL ** -0.5)
    w_ffn_up = nrm((L, D_MODEL, D_FF), D_MODEL ** -0.5)
    w_ffn_down = nrm((L, D_FF, D_MODEL), BETA * D_FF ** -0.5)
    ln2_g = gain((L, D_MODEL))
    ln2_b = nrm((L, D_MODEL), 0.02)
    return {'x': x, 'positions': positions, 'meta_tokens': meta_tokens,
            'ln_in_g': ln_in_g, 'ln_in_b': ln_in_b, 'w_in': w_in,
            'mla_q_norm': mla_q_norm, 'mla_w_uq': mla_w_uq, 'mla_kv_norm': mla_kv_norm, 'mla_w_ukv': mla_w_ukv,
            's5_lam_re': s5_lam_re, 's5_lam_im': s5_lam_im, 's5_log_dt': s5_log_dt,
            's5_b_re': s5_b_re, 's5_b_im': s5_b_im, 's5_c_re': s5_c_re, 's5_c_im': s5_c_im,
            's5_d': s5_d, 's5_w_glu': s5_w_glu,
            'hg_lb_logits': hg_lb_logits, 'hg_out_norm': hg_out_norm,
            'w_br_mla': w_br_mla, 'w_br_s5': w_br_s5, 'w_br_hg': w_br_hg, 'w_out': w_out,
            'ln1_g': ln1_g, 'ln1_b': ln1_b,
            'w_ffn_gate': w_ffn_gate, 'w_ffn_up': w_ffn_up, 'w_ffn_down': w_ffn_down,
            'ln2_g': ln2_g, 'ln2_b': ln2_b}


def reference(x, positions, meta_tokens, ln_in_g, ln_in_b, w_in,
              mla_q_norm, mla_w_uq, mla_kv_norm, mla_w_ukv,
              s5_lam_re, s5_lam_im, s5_log_dt, s5_b_re, s5_b_im, s5_c_re, s5_c_im, s5_d, s5_w_glu,
              hg_lb_logits, hg_out_norm,
              w_br_mla, w_br_s5, w_br_hg, w_out, ln1_g, ln1_b,
              w_ffn_gate, w_ffn_up, w_ffn_down, ln2_g, ln2_b):
    B_ = x.shape[0]
    meta = jnp.broadcast_to(meta_tokens.astype(x.dtype)[None], (B_, N_META, D_MODEL))
    h = layer_norm(jnp.concatenate([meta, x], axis=1), ln_in_g, ln_in_b)
    meta_pos = jnp.broadcast_to(jnp.arange(N_META, dtype=jnp.int32)[None], (B_, N_META))
    pos = jnp.concatenate([meta_pos, positions.astype(jnp.int32) + N_META], axis=1)
    cos, sin = rope_tables(pos)
    p_lb = jax.nn.softmax(hg_lb_logits.astype(jnp.float32), axis=0)
    lower_bounds = jnp.cumsum(p_lb, axis=0) - p_lb[0]
    for l in range(DEPTH):
        z = h @ w_in[l]
        z_mla, z_s5, z_hg, z_gate = jnp.split(z, SPLIT_IN, axis=-1)
        y_mla = mla_mixer(z_mla, cos, sin, mla_q_norm[l], mla_w_uq[l], mla_kv_norm[l], mla_w_ukv[l]) @ w_br_mla[l]
        y_s5 = s5_mixer(z_s5, s5_lam_re[l], s5_lam_im[l], s5_log_dt[l], s5_b_re[l], s5_b_im[l],
                        s5_c_re[l], s5_c_im[l], s5_d[l], s5_w_glu[l]) @ w_br_s5[l]
        y_hg = hgrn2_mixer(z_hg, lower_bounds[l], hg_out_norm[l]) @ w_br_hg[l]
        g_mla, g_s5, g_hg = jnp.split(jax.nn.sigmoid(z_gate), N_BRANCH, axis=-1)
        mixed = (g_mla * y_mla + g_s5 * y_s5 + g_hg * y_hg) @ w_out[l]
        h = layer_norm(ALPHA * h + mixed, ln1_g[l], ln1_b[l])
        ffn = (jax.nn.silu(h @ w_ffn_gate[l]) * (h @ w_ffn_up[l])) @ w_ffn_down[l]
        h = layer_norm(ALPHA * h + ffn, ln2_g[l], ln2_b[l])
    return h[:, N_META:]
```

```python
import functools
import math

import jax
import jax.numpy as jnp
import numpy as np
from jax import lax
from jax.experimental import pallas as pl
from jax.experimental.pallas import tpu as pltpu

F32 = jnp.float32
BF16 = jnp.bfloat16

N_META = 16
MLA_HEADS = 8
MLA_NOPE = 64
MLA_ROPE = 32
MLA_V = 64
MLA_Q_RANK = 256
MLA_KV_RANK = 256
ROPE_THETA = 10000.0
MASK_VALUE = -1e9
HEAD_PAD = 128
S5_WIDTH = 512
S5_GROUP = 16
S5_GROUPS = S5_WIDTH // S5_GROUP
S5_STATE = 64
S5_SLAB = 128
S5_SLAB_STATE = (S5_SLAB // S5_GROUP) * S5_STATE
HG_HEADS = 4
HG_KEY = 128
HG_VAL = 128
HG_QK = HG_HEADS * HG_KEY
HG_VW = HG_HEADS * HG_VAL
HG_F_MIN = 1e-6
HG_CHUNK = 128
HG_LEVELS = (128, 64, 32, 16)
HG_BOTTOM = 8
N_BRANCH = 3
T_BLK = 16
ATT_TQ = 256
VMEM_LIMIT = 56 * 1024 * 1024
LANES = 128

ZA_W = 1024
ZHG_W = 2048
ZG_W = 3072
ZKR_W = 128
Z_W = ZA_W + ZHG_W + ZG_W + ZKR_W


def _cparams(n_grid, sem="parallel"):
    return pltpu.CompilerParams(dimension_semantics=(sem,) * n_grid, vmem_limit_bytes=VMEM_LIMIT)


def _const_spec(shape):
    nd = len(shape)
    return pl.BlockSpec(shape, lambda *_: (0,) * nd, pipeline_mode=pl.Buffered(1))


def _dot(a, b):
    return jnp.dot(a, b, preferred_element_type=F32)


def _dot_nt(a, b):
    return lax.dot_general(a, b, (((1,), (1,)), ((), ())), preferred_element_type=F32)


def _dot_tn(a, b):
    return lax.dot_general(a, b, (((0,), (0,)), ((), ())), preferred_element_type=F32)


def _sigmoid(x):
    return 1.0 / (1.0 + jnp.exp(-x))


def _layer_norm(x, g, b, eps=1e-5):
    mu = jnp.mean(x, axis=-1, keepdims=True)
    xc = x - mu
    var = jnp.mean(xc * xc, axis=-1, keepdims=True)
    return xc * lax.rsqrt(var + eps) * g + b


def _rms_norm(x, g, eps=1e-6):
    return x * lax.rsqrt(jnp.mean(x * x, axis=-1, keepdims=True) + eps) * g


def _ln_in_kernel(x_ref, meta_ref, g_ref, b_ref, o_ref, *, nb, d):
    i = pl.program_id(0)

    @pl.when(i == 0)
    def _():
        y = _layer_norm(meta_ref[...], g_ref[...], b_ref[...])
        for b in range(nb):
            o_ref[:, b * d:(b + 1) * d] = y

    @pl.when(i > 0)
    def _():
        for b in range(nb):
            o_ref[:, b * d:(b + 1) * d] = _layer_norm(x_ref[b], g_ref[...], b_ref[...])


def _ln_in(x, meta, g, b):
    nb, s, d = x.shape
    l = s + N_META
    return pl.pallas_call(
        functools.partial(_ln_in_kernel, nb=nb, d=d),
        out_shape=jax.ShapeDtypeStruct((l, nb * d), F32),
        grid=(l // T_BLK,),
        in_specs=[
            pl.BlockSpec((nb, T_BLK, d), lambda i: (0, jnp.maximum(i - 1, 0), 0)),
            _const_spec((N_META, d)),
            _const_spec((1, d)),
            _const_spec((1, d)),
        ],
        out_specs=pl.BlockSpec((T_BLK, nb * d), lambda i: (i, 0)),
        compiler_params=_cparams(1),
        name="ln_in",
    )(x, meta, g, b)


def _proj_in_kernel(h_ref, w_ref, za_ref, zhg_ref, zg_ref, zkr_ref):
    x = h_ref[...].astype(BF16)
    za_ref[...] = _dot(x, w_ref[:, 0:ZA_W]).astype(BF16)
    o = ZA_W
    for c in range(0, ZHG_W, 1024):
        zhg_ref[:, c:c + 1024] = _dot(x, w_ref[:, o + c:o + c + 1024]).astype(BF16)
    o += ZHG_W
    for c in range(0, ZG_W, 1024):
        zg_ref[:, c:c + 1024] = _dot(x, w_ref[:, o + c:o + c + 1024]).astype(BF16)
    o += ZG_W
    zkr_ref[...] = _dot(x, w_ref[:, o:o + ZKR_W]).astype(BF16)


def _proj_in(h, w, rb):
    r, d = h.shape
    row = lambda width: pl.BlockSpec((rb, width), lambda i: (i, 0))
    return pl.pallas_call(
        _proj_in_kernel,
        out_shape=[jax.ShapeDtypeStruct((r, wd), BF16) for wd in (ZA_W, ZHG_W, ZG_W, ZKR_W)],
        grid=(r // rb,),
        in_specs=[row(d), _const_spec((d, Z_W))],
        out_specs=[row(ZA_W), row(ZHG_W), row(ZG_W), row(ZKR_W)],
        compiler_params=_cparams(1),
        name="proj_in",
    )(h, w)


def _mla_prep_kernel(cq_ref, ckv_ref, kr_ref, pos_ref, inv_ref, qn_ref, kvn_ref,
                     wq_ref, wqr_ref, wk_ref, wv_ref, e_ref, er_ref,
                     q_ref, k_ref, v_ref):
    scale = (MLA_NOPE + MLA_ROPE) ** -0.5
    ang = pos_ref[...].astype(F32) * inv_ref[...]
    cos = jnp.cos(ang)
    sin = jnp.sin(ang)
    cqn = _rms_norm(cq_ref[...].astype(F32), qn_ref[...]).astype(BF16)
    ckvn = _rms_norm(ckv_ref[...].astype(F32), kvn_ref[...]).astype(BF16)
    kr = kr_ref[...]
    cos_q = cos * scale
    sin_q = sin * scale
    for h in range(MLA_HEADS):
        cs = slice(h * HEAD_PAD, (h + 1) * HEAD_PAD)
        qf = _dot(cqn, wq_ref[:, cs])
        qr = _dot(cqn, wqr_ref[:, cs])
        q_ref[:, cs] = (qf * cos_q + qr * sin_q).astype(BF16)
        kf = _dot(ckvn, wk_ref[:, cs]) + _dot(kr, e_ref[:, cs])
        krot = _dot(kr, er_ref[:, cs])
        k_ref[:, cs] = (kf * cos + krot * sin).astype(BF16)
    v_ref[...] = _dot(ckvn, wv_ref[...]).astype(BF16)


def _mla_prep(za, zkr, pos, inv128, qn, kvn, wq, wqr, wk, wv, e, er, rb):
    r = za.shape[0]
    hw = MLA_HEADS * HEAD_PAD
    vw = MLA_HEADS * MLA_V
    return pl.pallas_call(
        _mla_prep_kernel,
        out_shape=[jax.ShapeDtypeStruct((r, hw), BF16), jax.ShapeDtypeStruct((r, hw), BF16),
                   jax.ShapeDtypeStruct((r, vw), BF16)],
        grid=(r // rb,),
        in_specs=[
            pl.BlockSpec((rb, MLA_Q_RANK), lambda i: (i, 2)),
            pl.BlockSpec((rb, MLA_KV_RANK), lambda i: (i, 3)),
            pl.BlockSpec((rb, ZKR_W), lambda i: (i, 0)),
            pl.BlockSpec((rb, 1), lambda i: (i, 0)),
            _const_spec((1, HEAD_PAD)),
            _const_spec((1, MLA_Q_RANK)),
            _const_spec((1, MLA_KV_RANK)),
            _const_spec((MLA_Q_RANK, hw)),
            _const_spec((MLA_Q_RANK, hw)),
            _const_spec((MLA_KV_RANK, hw)),
            _const_spec((MLA_KV_RANK, vw)),
            _const_spec((ZKR_W, hw)),
            _const_spec((ZKR_W, hw)),
        ],
        out_specs=[pl.BlockSpec((rb, hw), lambda i: (i, 0)), pl.BlockSpec((rb, hw), lambda i: (i, 0)),
                   pl.BlockSpec((rb, vw), lambda i: (i, 0))],
        compiler_params=_cparams(1),
        name="mla_prep",
    )(za, za, zkr, pos, inv128, qn, kvn, wq, wqr, wk, wv, e, er)


def _attn_kernel(q_ref, k_ref, v_ref, o_ref, *, n_heads, tq, nq):
    def causal(s):
        r = lax.broadcasted_iota(jnp.int32, s.shape, 0)
        c = lax.broadcasted_iota(jnp.int32, s.shape, 1)
        return jnp.where(c <= r, s, MASK_VALUE)

    for hh in range(n_heads):
        qs = slice(hh * HEAD_PAD, (hh + 1) * HEAD_PAD)
        vs = slice(hh * MLA_V, (hh + 1) * MLA_V)
        k0 = k_ref[0:N_META, qs]
        v0 = v_ref[0:N_META, vs]
        s = causal(_dot_nt(q_ref[0:N_META, qs], k0))
        p = jnp.exp(s - jnp.max(s, axis=-1, keepdims=True))
        o = _dot(p.astype(BF16), v0) / jnp.sum(p, axis=-1, keepdims=True)
        o_ref[0:N_META, vs] = o.astype(BF16)

        def q_block(i, carry):
            r0 = pl.multiple_of(N_META + i * tq, 16)
            q = q_ref[pl.ds(r0, tq), qs]
            s0 = _dot_nt(q, k0)
            m = jnp.max(s0, axis=-1, keepdims=True)
            p0 = jnp.exp(s0 - m)
            l = jnp.sum(p0, axis=-1, keepdims=True)
            acc = _dot(p0.astype(BF16), v0)

            def update(m, l, acc, s, vb):
                mn = jnp.maximum(m, jnp.max(s, axis=-1, keepdims=True))
                a = jnp.exp(m - mn)
                p = jnp.exp(s - mn)
                l = a * l + jnp.sum(p, axis=-1, keepdims=True)
                acc = a * acc + _dot(p.astype(BF16), vb)
                return mn, l, acc

            def kv_block(j, c):
                c0 = pl.multiple_of(N_META + j * tq, 16)
                s = _dot_nt(q, k_ref[pl.ds(c0, tq), qs])
                return update(*c, s, v_ref[pl.ds(c0, tq), vs])

            m, l, acc = lax.fori_loop(0, i, kv_block, (m, l, acc))
            s = causal(_dot_nt(q, k_ref[pl.ds(r0, tq), qs]))
            m, l, acc = update(m, l, acc, s, v_ref[pl.ds(r0, tq), vs])
            o_ref[pl.ds(r0, tq), vs] = (acc / l).astype(BF16)
            return carry

        lax.fori_loop(0, nq, q_block, 0)


def _attention(q, k, v, nb, l):
    heads_per_step = 4
    n_hp = MLA_HEADS // heads_per_step
    qw = heads_per_step * HEAD_PAD
    vw = heads_per_step * MLA_V
    s = l - N_META
    tq = min(ATT_TQ, s)
    q2 = q.reshape(l, nb * MLA_HEADS * HEAD_PAD)
    k2 = k.reshape(l, nb * MLA_HEADS * HEAD_PAD)
    v2 = v.reshape(l, nb * MLA_HEADS * MLA_V)
    o = pl.pallas_call(
        functools.partial(_attn_kernel, n_heads=heads_per_step, tq=tq, nq=s // tq),
        out_shape=jax.ShapeDtypeStruct((l, nb * MLA_HEADS * MLA_V), BF16),
        grid=(nb, n_hp),
        in_specs=[
            pl.BlockSpec((l, qw), lambda b, p: (0, b * n_hp + p)),
            pl.BlockSpec((l, qw), lambda b, p: (0, b * n_hp + p)),
            pl.BlockSpec((l, vw), lambda b, p: (0, b * n_hp + p)),
        ],
        out_specs=pl.BlockSpec((l, vw), lambda b, p: (0, b * n_hp + p)),
        compiler_params=_cparams(2),
        name="mla_attn",
    )(q2, k2, v2)
    return o.reshape(l * nb, MLA_HEADS * MLA_V)


def _s5_kernel(u_ref, bm_ref, cm_ref, ar_ref, ai_ref, d_ref, wg_ref, o_ref,
               xr_ref, xi_ref, buf_ref, y_ref, *, nb):
    n_slab = S5_WIDTH // S5_SLAB
    ns = S5_SLAB_STATE

    @pl.when(pl.program_id(0) == 0)
    def _():
        xr_ref[...] = jnp.zeros_like(xr_ref)
        xi_ref[...] = jnp.zeros_like(xi_ref)

    for j in range(n_slab):
        uj = u_ref[:, j * S5_SLAB:(j + 1) * S5_SLAB]
        buf_ref[...] = _dot(uj, bm_ref[j])
        ar = jnp.broadcast_to(ar_ref[j], (nb, ns))
        ai = jnp.broadcast_to(ai_ref[j], (nb, ns))
        xr = xr_ref[j]
        xi = xi_ref[j]
        for t in range(T_BLK):
            rows = slice(t * nb, (t + 1) * nb)
            nr = ar * xr - ai * xi + buf_ref[rows, 0:ns]
            ni = ar * xi + ai * xr + buf_ref[rows, ns:2 * ns]
            buf_ref[rows, 0:ns] = nr
            buf_ref[rows, ns:2 * ns] = ni
            xr, xi = nr, ni
        xr_ref[j] = xr
        xi_ref[j] = xi
        y_ref[:, j * S5_SLAB:(j + 1) * S5_SLAB] = _dot(buf_ref[...].astype(BF16), cm_ref[j])

    y = y_ref[...] + d_ref[...] * u_ref[...].astype(F32)
    y = 0.5 * y * (1.0 + jnp.tanh(math.sqrt(2.0 / math.pi) * (y + 0.044715 * (y * y * y))))
    gate = _sigmoid(_dot(y.astype(BF16), wg_ref[...]))
    o_ref[...] = (y * gate).astype(BF16)


def _s5(za, bm, cm, ar, ai, d, wg, nb):
    r = za.shape[0]
    rb = T_BLK * nb
    n_slab = S5_WIDTH // S5_SLAB
    return pl.pallas_call(
        functools.partial(_s5_kernel, nb=nb),
        out_shape=jax.ShapeDtypeStruct((r, S5_WIDTH), BF16),
        grid=(r // rb,),
        in_specs=[
            pl.BlockSpec((rb, S5_WIDTH), lambda i: (i, 0)),
            _const_spec((n_slab, S5_SLAB, 2 * S5_SLAB_STATE)),
            _const_spec((n_slab, 2 * S5_SLAB_STATE, S5_SLAB)),
            _const_spec((n_slab, 1, S5_SLAB_STATE)),
            _const_spec((n_slab, 1, S5_SLAB_STATE)),
            _const_spec((1, S5_WIDTH)),
            _const_spec((S5_WIDTH, S5_WIDTH)),
        ],
        out_specs=pl.BlockSpec((rb, S5_WIDTH), lambda i: (i, 0)),
        scratch_shapes=[
            pltpu.VMEM((n_slab, nb, S5_SLAB_STATE), F32),
            pltpu.VMEM((n_slab, nb, S5_SLAB_STATE), F32),
            pltpu.VMEM((rb, 2 * S5_SLAB_STATE), F32),
            pltpu.VMEM((rb, S5_WIDTH), F32),
        ],
        compiler_params=_cparams(1, "arbitrary"),
        name="s5_scan",
    )(za, bm, cm, ar, ai, d, wg)


def _block_row_bcast(x, m, row):
    t, c = x.shape
    if m == t:
        return jnp.broadcast_to(x[row:row + 1, :], x.shape)
    x3 = x.reshape(t // m, m, c)
    return jnp.broadcast_to(x3[:, row:row + 1, :], x3.shape).reshape(t, c)


def _hgrn_chunk(st, r0, t, cs, q_ref, f_ref, v_ref, g_ref, lb, onorm, tri_ref, lvl_ref, o_ref):
    rows = pl.ds(r0, t)
    q = q_ref[rows, cs].astype(F32)
    zf = f_ref[rows, cs].astype(F32)
    v = v_ref[rows, cs]
    g = g_ref[rows, cs].astype(F32)
    e = jnp.exp(-jnp.abs(zf))
    rcp = 1.0 / (1.0 + e)
    pos = zf >= 0.0
    sig_p = jnp.where(pos, rcp, e * rcp)
    sig_n = jnp.where(pos, e * rcp, rcp)
    f = lb + (1.0 - lb) * sig_p
    log_f = jnp.log(jnp.maximum(f, HG_F_MIN))
    k = (1.0 - lb) * sig_n
    hi = log_f.astype(BF16)
    lo = (log_f - hi.astype(F32)).astype(BF16)
    tri = tri_ref[0:t, 0:t]
    cum = _dot(tri, hi) + _dot(tri, lo)
    lvl = lvl_ref[0:t, 0:t]
    r_idx = lax.broadcasted_iota(jnp.int32, (t, HG_KEY), 0)
    c8 = _block_row_bcast(cum, HG_BOTTOM, HG_BOTTOM // 2 - 1)
    qe = (q * jnp.exp(cum - c8)).astype(BF16)
    ke = (k * jnp.exp(c8 - cum)).astype(BF16)
    n_lvl = len(HG_LEVELS)
    scores = jnp.where(lvl == n_lvl, _dot_nt(qe, ke), 0.0)
    for li, m in enumerate(HG_LEVELS):
        if m > t:
            continue
        half = m // 2
        cmid = _block_row_bcast(cum, m, half - 1)
        upper = (r_idx & (m - 1)) >= half
        ex = jnp.exp(jnp.where(upper, cum - cmid, cmid - cum))
        qe = jnp.where(upper, q * ex, 0.0).astype(BF16)
        ke = jnp.where(upper, 0.0, k * ex).astype(BF16)
        scores = scores + jnp.where(lvl == li, _dot_nt(qe, ke), 0.0)
    o = _dot(scores.astype(BF16), v)
    o = o + _dot_nt((q * jnp.exp(cum)).astype(BF16), st.astype(BF16))
    last = cum[t - 1:t, :]
    kd = (k * jnp.exp(last - cum)).astype(BF16)
    st = st * jnp.exp(last) + _dot_tn(v, kd)
    o = _rms_norm(o, onorm)
    o_ref[rows, cs] = (o * (g * _sigmoid(g))).astype(BF16)
    return st


def _hgrn_kernel(q_ref, f_ref, v_ref, g_ref, lb_ref, on_ref, tri_ref, lvl_ref, o_ref, *, n_chunks):
    for h in range(HG_HEADS):
        cs = slice(h * HG_KEY, (h + 1) * HG_KEY)
        lb = lb_ref[:, cs]
        onorm = on_ref[:, cs]
        chunk = functools.partial(_hgrn_chunk, cs=cs, q_ref=q_ref, f_ref=f_ref, v_ref=v_ref, g_ref=g_ref,
                                  lb=lb, onorm=onorm, tri_ref=tri_ref, lvl_ref=lvl_ref, o_ref=o_ref)
        st = chunk(jnp.zeros((HG_VAL, HG_KEY), F32), 0, N_META)
        lax.fori_loop(0, n_chunks,
                      lambda i, s: chunk(s, pl.multiple_of(N_META + i * HG_CHUNK, 16), HG_CHUNK), st)


def _hgrn(zhg, lb, onorm, tri, lvl, nb, l):
    z2 = zhg.reshape(l, nb * ZHG_W)
    n_col = ZHG_W // HG_QK
    col = lambda j: pl.BlockSpec((l, HG_QK), lambda b: (0, b * n_col + j))
    o = pl.pallas_call(
        functools.partial(_hgrn_kernel, n_chunks=(l - N_META) // HG_CHUNK),
        out_shape=jax.ShapeDtypeStruct((l, nb * HG_VW), BF16),
        grid=(nb,),
        in_specs=[col(0), col(1), col(2), col(3),
                  _const_spec((1, HG_QK)), _const_spec((1, HG_VW)),
                  _const_spec((HG_CHUNK, HG_CHUNK)), _const_spec((HG_CHUNK, HG_CHUNK))],
        out_specs=pl.BlockSpec((l, HG_VW), lambda b: (0, b)),
        compiler_params=_cparams(1),
        name="hgrn2",
    )(z2, z2, z2, z2, lb, onorm, tri, lvl)
    return o.reshape(l * nb, HG_VW)


def _hgrn_level_matrix():
    r = np.arange(HG_CHUNK)[:, None]
    c = np.arange(HG_CHUNK)[None, :]
    lvl = np.zeros((HG_CHUNK, HG_CHUNK), np.int32)
    for li, m in enumerate(HG_LEVELS):
        lvl = np.where(r // m == c // m, li, lvl)
    lvl = np.where(r // HG_BOTTOM == c // HG_BOTTOM, len(HG_LEVELS), lvl)
    return np.where(c <= r, lvl, -1).astype(np.int32)


def _merge_kernel(om_ref, os_ref, oh_ref, gm_ref, gs_ref, gh_ref, h_ref,
                  wm_ref, ws_ref, wh_ref, wo_ref, g_ref, b_ref, o_ref, *, alpha):
    mixed = _sigmoid(gm_ref[...].astype(F32)) * _dot(om_ref[...], wm_ref[...])
    mixed += _sigmoid(gs_ref[...].astype(F32)) * _dot(os_ref[...], ws_ref[...])
    mixed += _sigmoid(gh_ref[...].astype(F32)) * _dot(oh_ref[...], wh_ref[...])
    r = alpha * h_ref[...] + _dot(mixed.astype(BF16), wo_ref[...])
    o_ref[...] = _layer_norm(r, g_ref[...], b_ref[...])


def _merge(om, os_, oh, zg, h, wm, ws, wh, wo, g, b, rb, alpha, skip):
    r, d = h.shape
    bw = om.shape[1]
    row = lambda width, j=0: pl.BlockSpec((rb, width), lambda i: (i + skip, j))
    return pl.pallas_call(
        functools.partial(_merge_kernel, alpha=alpha),
        out_shape=jax.ShapeDtypeStruct((r - skip * rb, d), F32),
        grid=(r // rb - skip,),
        in_specs=[row(bw), row(bw), row(bw), row(d, 0), row(d, 1), row(d, 2), row(d),
                  _const_spec((bw, d)), _const_spec((bw, d)), _const_spec((bw, d)), _const_spec((d, d)),
                  _const_spec((1, d)), _const_spec((1, d))],
        out_specs=pl.BlockSpec((rb, d), lambda i: (i, 0)),
        compiler_params=_cparams(1),
        name="merge",
    )(om, os_, oh, zg, zg, zg, h, wm, ws, wh, wo, g, b)


def _ffn_body(h_ref, wg_ref, wu_ref, wd_ref, g_ref, b_ref, alpha, fc):
    h = h_ref[...]
    hb = h.astype(BF16)
    acc = alpha * h
    for c in range(0, wg_ref.shape[1], fc):
        a = _dot(hb, wg_ref[:, c:c + fc])
        u = _dot(hb, wu_ref[:, c:c + fc])
        acc += _dot((a * _sigmoid(a) * u).astype(BF16), wd_ref[c:c + fc, :])
    return _layer_norm(acc, g_ref[...], b_ref[...])


def _ffn_kernel(h_ref, wg_ref, wu_ref, wd_ref, g_ref, b_ref, o_ref, *, alpha, fc):
    o_ref[...] = _ffn_body(h_ref, wg_ref, wu_ref, wd_ref, g_ref, b_ref, alpha, fc)


def _ffn_final_kernel(h_ref, wg_ref, wu_ref, wd_ref, g_ref, b_ref, o_ref, y_ref, *, alpha, fc, nb):
    y = _ffn_body(h_ref, wg_ref, wu_ref, wd_ref, g_ref, b_ref, alpha, fc)
    n_lane = y_ref.shape[2]
    for c in range(y_ref.shape[0]):
        y_ref[c] = y[:, c * n_lane:(c + 1) * n_lane]
    for b in range(nb):
        for c in range(y_ref.shape[0]):
            o_ref[b, :, c * n_lane:(c + 1) * n_lane] = y_ref[c, pl.ds(b, T_BLK, stride=nb), :]


def _ffn_chunk(dff):
    return dff // 2 if dff % 256 == 0 else dff


def _ffn(h, wg, wu, wd, g, b, rb, alpha):
    r, d = h.shape
    dff = wg.shape[1]
    row = pl.BlockSpec((rb, d), lambda i: (i, 0))
    return pl.pallas_call(
        functools.partial(_ffn_kernel, alpha=alpha, fc=_ffn_chunk(dff)),
        out_shape=jax.ShapeDtypeStruct((r, d), F32),
        grid=(r // rb,),
        in_specs=[row, _const_spec((d, dff)), _const_spec((d, dff)), _const_spec((dff, d)),
                  _const_spec((1, d)), _const_spec((1, d))],
        out_specs=row,
        compiler_params=_cparams(1),
        name="ffn",
    )(h, wg, wu, wd, g, b)


def _ffn_final(h, wg, wu, wd, g, b, nb, alpha):
    r, d = h.shape
    dff = wg.shape[1]
    rb = T_BLK * nb
    s = r // nb
    return pl.pallas_call(
        functools.partial(_ffn_final_kernel, alpha=alpha, fc=_ffn_chunk(dff), nb=nb),
        out_shape=jax.ShapeDtypeStruct((nb, s, d), F32),
        grid=(s // T_BLK,),
        in_specs=[pl.BlockSpec((rb, d), lambda i: (i, 0)),
                  _const_spec((d, dff)), _const_spec((d, dff)), _const_spec((dff, d)),
                  _const_spec((1, d)), _const_spec((1, d))],
        out_specs=pl.BlockSpec((nb, T_BLK, d), lambda i: (0, i, 0)),
        scratch_shapes=[pltpu.VMEM((d // LANES, rb, LANES), F32)],
        compiler_params=_cparams(1),
        name="ffn_final",
    )(h, wg, wu, wd, g, b)


def _permute_w_in(w):
    mla_in = MLA_Q_RANK + MLA_KV_RANK + MLA_ROPE
    s5_0, hg_0 = mla_in, mla_in + S5_WIDTH
    g_0 = hg_0 + 2 * HG_QK + 2 * HG_VW
    pad = jnp.zeros((w.shape[0], ZKR_W - MLA_ROPE), w.dtype)
    return jnp.concatenate([w[:, s5_0:hg_0], w[:, 0:MLA_Q_RANK + MLA_KV_RANK], w[:, hg_0:g_0], w[:, g_0:],
                            w[:, MLA_Q_RANK + MLA_KV_RANK:mla_in], pad], axis=1).astype(BF16)


def _rot_half(x):
    x1, x2 = jnp.split(x, 2, axis=-1)
    return jnp.concatenate([-x2, x1], axis=-1)


def _mla_weights(w_uq, w_ukv):
    rq, rkv = w_uq.shape[0], w_ukv.shape[0]
    zpad = HEAD_PAD - MLA_NOPE - MLA_ROPE
    wq = w_uq.reshape(rq, MLA_HEADS, MLA_NOPE + MLA_ROPE)
    q_nope, q_rope = wq[..., :MLA_NOPE], wq[..., MLA_NOPE:]
    zq = jnp.zeros((rq, MLA_HEADS, zpad), w_uq.dtype)
    wq_p = jnp.concatenate([q_nope, q_rope, zq], axis=-1).reshape(rq, -1)
    wq_r = jnp.concatenate([jnp.zeros_like(q_nope), _rot_half(q_rope), zq], axis=-1).reshape(rq, -1)
    wkv = w_ukv.reshape(rkv, MLA_HEADS, MLA_NOPE + MLA_V)
    zk = jnp.zeros((rkv, MLA_HEADS, HEAD_PAD - MLA_NOPE), w_ukv.dtype)
    wk_p = jnp.concatenate([wkv[..., :MLA_NOPE], zk], axis=-1).reshape(rkv, -1)
    wv = wkv[..., MLA_NOPE:].reshape(rkv, -1)
    eye = jnp.eye(ZKR_W, MLA_ROPE, dtype=F32)
    place = jnp.concatenate([jnp.zeros((ZKR_W, MLA_NOPE), F32), eye, jnp.zeros((ZKR_W, zpad), F32)], axis=-1)
    place_r = jnp.concatenate([jnp.zeros((ZKR_W, MLA_NOPE), F32), _rot_half(eye),
                               jnp.zeros((ZKR_W, zpad), F32)], axis=-1)
    e = jnp.tile(place, (1, MLA_HEADS))
    er = jnp.tile(place_r, (1, MLA_HEADS))
    return [a.astype(BF16) for a in (wq_p, wq_r, wk_p, wv, e, er)]


def _rope_inv():
    inv = ROPE_THETA ** (-(jnp.arange(0, MLA_ROPE, 2, dtype=F32) / MLA_ROPE))
    z = jnp.zeros((HEAD_PAD - MLA_NOPE - MLA_ROPE,), F32)
    return jnp.concatenate([jnp.zeros((MLA_NOPE,), F32), inv, inv, z])[None, :]


def _s5_params(lam_re, lam_im, log_dt, b_re, b_im, c_re, c_im):
    lr = jnp.minimum(lam_re.astype(F32), -1e-4)
    li = lam_im.astype(F32)
    dt = jnp.exp(log_dt.astype(F32))[:, None]
    mag = jnp.exp(lr * dt)
    ab_r = mag * jnp.cos(li * dt)
    ab_i = mag * jnp.sin(li * dt)
    den = lr * lr + li * li
    nr = ab_r - 1.0
    coef_r = ((nr * lr + ab_i * li) / den)[..., None]
    coef_i = ((ab_i * lr - nr * li) / den)[..., None]
    bb_r = coef_r * b_re.astype(F32) - coef_i * b_im.astype(F32)
    bb_i = coef_r * b_im.astype(F32) + coef_i * b_re.astype(F32)
    n_slab = S5_WIDTH // S5_SLAB
    gps = S5_SLAB // S5_GROUP
    eye = jnp.eye(gps, dtype=F32)

    def in_mat(bb):
        b4 = bb.reshape(n_slab, gps, S5_STATE, S5_GROUP)
        return jnp.einsum('jgnc,gh->jgchn', b4, eye).reshape(n_slab, S5_SLAB, S5_SLAB_STATE)

    def out_mat(cc):
        c4 = cc.astype(F32).reshape(n_slab, gps, S5_GROUP, S5_STATE)
        return jnp.einsum('jgcn,gh->jgnhc', c4, eye).reshape(n_slab, S5_SLAB_STATE, S5_SLAB)

    bm = jnp.concatenate([in_mat(bb_r), in_mat(bb_i)], axis=2).astype(BF16)
    cm = jnp.concatenate([out_mat(c_re), -out_mat(c_im)], axis=1).astype(BF16)
    ar = ab_r.reshape(n_slab, 1, S5_SLAB_STATE)
    ai = ab_i.reshape(n_slab, 1, S5_SLAB_STATE)
    return bm, cm, ar, ai


def kernel(x, positions, meta_tokens, ln_in_g, ln_in_b, w_in, mla_q_norm, mla_w_uq, mla_kv_norm, mla_w_ukv,
           s5_lam_re, s5_lam_im, s5_log_dt, s5_b_re, s5_b_im, s5_c_re, s5_c_im, s5_d, s5_w_glu,
           hg_lb_logits, hg_out_norm, w_br_mla, w_br_s5, w_br_hg, w_out, ln1_g, ln1_b,
           w_ffn_gate, w_ffn_up, w_ffn_down, ln2_g, ln2_b):
    nb, s, d = x.shape
    depth = w_in.shape[0]
    l = s + N_META
    rb = T_BLK * nb
    alpha = (2 * depth) ** 0.25
    row2 = lambda a: a.astype(F32)[None, :]

    h = _ln_in(x, meta_tokens.astype(x.dtype), row2(ln_in_g), row2(ln_in_b)).reshape(l * nb, d)

    meta_pos = jnp.broadcast_to(jnp.arange(N_META, dtype=jnp.int32)[:, None], (N_META, nb))
    pos = jnp.concatenate([meta_pos, positions.astype(jnp.int32).T + N_META], axis=0).reshape(l * nb, 1)
    inv128 = _rope_inv()
    p_lb = jax.nn.softmax(hg_lb_logits.astype(F32), axis=0)
    lower_bounds = jnp.cumsum(p_lb, axis=0) - p_lb[0]
    tri = jnp.asarray(np.tril(np.ones((HG_CHUNK, HG_CHUNK), np.float32)), BF16)
    lvl = jnp.asarray(_hgrn_level_matrix())

    out = None
    for li in range(depth):
        za, zhg, zg, zkr = _proj_in(h, _permute_w_in(w_in[li]), rb)
        q, k, v = _mla_prep(za, zkr, pos, inv128, row2(mla_q_norm[li]), row2(mla_kv_norm[li]),
                            *_mla_weights(mla_w_uq[li], mla_w_ukv[li]), rb)
        o_mla = _attention(q, k, v, nb, l)
        bm, cm, ar, ai = _s5_params(s5_lam_re[li], s5_lam_im[li], s5_log_dt[li], s5_b_re[li], s5_b_im[li],
                                    s5_c_re[li], s5_c_im[li])
        y_s5 = _s5(za, bm, cm, ar, ai, row2(s5_d[li]), s5_w_glu[li].astype(BF16), nb)
        o_hg = _hgrn(zhg, lower_bounds[li][None, :], row2(hg_out_norm[li]), tri, lvl, nb, l)
        last = li == depth - 1
        skip = 1 if last else 0
        h1 = _merge(o_mla, y_s5, o_hg, zg, h, w_br_mla[li].astype(BF16), w_br_s5[li].astype(BF16),
                    w_br_hg[li].astype(BF16), w_out[li].astype(BF16), row2(ln1_g[li]), row2(ln1_b[li]),
                    rb, alpha, skip)
        ffn_w = (w_ffn_gate[li].astype(BF16), w_ffn_up[li].astype(BF16), w_ffn_down[li].astype(BF16),
                 row2(ln2_g[li]), row2(ln2_b[li]))
        if last:
            out = _ffn_final(h1, *ffn_w, nb, alpha)
        else:
            h = _ffn(h1, *ffn_w, rb, alpha)
    return out
```

```python
import functools
import math

import jax
import jax.numpy as jnp
import numpy as np
from jax import lax
from jax.experimental import pallas as pl
from jax.experimental.pallas import tpu as pltpu

F32 = jnp.float32
BF16 = jnp.bfloat16

N_META = 16
MLA_HEADS = 8
MLA_NOPE = 64
MLA_ROPE = 32
MLA_V = 64
MLA_Q_RANK = 256
MLA_KV_RANK = 256
ROPE_THETA = 10000.0
MASK_VALUE = -1e9
HEAD_PAD = 128
S5_WIDTH = 512
S5_GROUP = 16
S5_GROUPS = S5_WIDTH // S5_GROUP
S5_STATE = 64
S5_SLAB = 128
S5_SLAB_STATE = (S5_SLAB // S5_GROUP) * S5_STATE
HG_HEADS = 4
HG_KEY = 128
HG_VAL = 128
HG_QK = HG_HEADS * HG_KEY
HG_VW = HG_HEADS * HG_VAL
HG_F_MIN = 1e-6
HG_CHUNK = 128
HG_LEVELS = (128, 64, 32, 16)
HG_BOTTOM = 8
N_BRANCH = 3
T_BLK = 16
ATT_TQ = 256
ATT_HEADS_PER_STEP = 4
VMEM_LIMIT = 56 * 1024 * 1024

ZA_W = 1024
ZHG_W = 2048
ZG_W = 3072
ZKR_W = 128
Z_W = ZA_W + ZHG_W + ZG_W + ZKR_W


def _cparams(n_grid, sem="parallel"):
    return pltpu.CompilerParams(dimension_semantics=(sem,) * n_grid, vmem_limit_bytes=VMEM_LIMIT)


def _const_spec(shape):
    nd = len(shape)
    return pl.BlockSpec(shape, lambda *_: (0,) * nd, pipeline_mode=pl.Buffered(1))


def _seq_spec(nblk, width, col):
    return pl.BlockSpec((nblk, None, T_BLK, width), lambda b, *g: (0, b, 0, col(b, *g)))


def _dot(a, b):
    return jnp.dot(a, b, preferred_element_type=F32)


def _dot_nt(a, b):
    return lax.dot_general(a, b, (((1,), (1,)), ((), ())), preferred_element_type=F32)


def _dot_tn(a, b):
    return lax.dot_general(a, b, (((0,), (0,)), ((), ())), preferred_element_type=F32)


def _sigmoid(x):
    return 1.0 / (1.0 + jnp.exp(-x))


def _layer_norm(x, g, b, eps=1e-5):
    mu = jnp.mean(x, axis=-1, keepdims=True)
    xc = x - mu
    var = jnp.mean(xc * xc, axis=-1, keepdims=True)
    return xc * lax.rsqrt(var + eps) * g + b


def _rms_norm(x, g, eps=1e-6):
    return x * lax.rsqrt(jnp.mean(x * x, axis=-1, keepdims=True) + eps) * g


def _load_rows(ref, blk0, n_blk, cs):
    x = ref[pl.ds(blk0, n_blk), :, cs]
    return x.reshape(n_blk * T_BLK, x.shape[-1])


def _store_rows(ref, blk0, n_blk, cs, x):
    ref[pl.ds(blk0, n_blk), :, cs] = x.reshape(n_blk, T_BLK, x.shape[-1])


def _ln_in_kernel(x_ref, meta_ref, g_ref, b_ref, o_ref, *, nb):
    i = pl.program_id(0)

    @pl.when(i == 0)
    def _():
        y = _layer_norm(meta_ref[...], g_ref[...], b_ref[...])
        for b in range(nb):
            o_ref[0, b] = y

    @pl.when(i > 0)
    def _():
        o_ref[0] = _layer_norm(x_ref[...], g_ref[...], b_ref[...])


def _ln_in(x, meta, g, b):
    nb, s, d = x.shape
    nblk = (s + N_META) // T_BLK
    return pl.pallas_call(
        functools.partial(_ln_in_kernel, nb=nb),
        out_shape=jax.ShapeDtypeStruct((nblk, nb, T_BLK, d), F32),
        grid=(nblk,),
        in_specs=[
            pl.BlockSpec((nb, T_BLK, d), lambda i: (0, jnp.maximum(i - 1, 0), 0)),
            _const_spec((N_META, d)),
            _const_spec((1, d)),
            _const_spec((1, d)),
        ],
        out_specs=pl.BlockSpec((1, nb, T_BLK, d), lambda i: (i, 0, 0, 0)),
        compiler_params=_cparams(1),
        name="ln_in",
    )(x, meta, g, b)


def _proj_in_kernel(h_ref, w_ref, za_ref, zhg_ref, zg_ref, zkr_ref):
    x = h_ref[...].astype(BF16)
    za_ref[...] = _dot(x, w_ref[:, 0:ZA_W]).astype(BF16)
    o = ZA_W
    for c in range(0, ZHG_W, 1024):
        zhg_ref[:, c:c + 1024] = _dot(x, w_ref[:, o + c:o + c + 1024]).astype(BF16)
    o += ZHG_W
    for c in range(0, ZG_W, 1024):
        zg_ref[:, c:c + 1024] = _dot(x, w_ref[:, o + c:o + c + 1024]).astype(BF16)
    o += ZG_W
    zkr_ref[...] = _dot(x, w_ref[:, o:o + ZKR_W]).astype(BF16)


def _proj_in(h, w, rb):
    r, d = h.shape
    row = lambda width: pl.BlockSpec((rb, width), lambda i: (i, 0))
    return pl.pallas_call(
        _proj_in_kernel,
        out_shape=[jax.ShapeDtypeStruct((r, wd), BF16) for wd in (ZA_W, ZHG_W, ZG_W, ZKR_W)],
        grid=(r // rb,),
        in_specs=[row(d), _const_spec((d, Z_W))],
        out_specs=[row(ZA_W), row(ZHG_W), row(ZG_W), row(ZKR_W)],
        compiler_params=_cparams(1),
        name="proj_in",
    )(h, w)


def _mla_prep_kernel(cq_ref, ckv_ref, kr_ref, pos_ref, inv_ref, qn_ref, kvn_ref,
                     wq_ref, wqr_ref, wk_ref, wv_ref, e_ref, er_ref,
                     q_ref, k_ref, v_ref):
    scale = (MLA_NOPE + MLA_ROPE) ** -0.5
    ang = pos_ref[...].astype(F32) * inv_ref[...]
    cos = jnp.cos(ang)
    sin = jnp.sin(ang)
    cqn = _rms_norm(cq_ref[...].astype(F32), qn_ref[...]).astype(BF16)
    ckvn = _rms_norm(ckv_ref[...].astype(F32), kvn_ref[...]).astype(BF16)
    kr = kr_ref[...]
    cos_q = cos * scale
    sin_q = sin * scale
    for h in range(MLA_HEADS):
        cs = slice(h * HEAD_PAD, (h + 1) * HEAD_PAD)
        qf = _dot(cqn, wq_ref[:, cs])
        qr = _dot(cqn, wqr_ref[:, cs])
        q_ref[:, cs] = (qf * cos_q + qr * sin_q).astype(BF16)
        kf = _dot(ckvn, wk_ref[:, cs]) + _dot(kr, e_ref[:, cs])
        krot = _dot(kr, er_ref[:, cs])
        k_ref[:, cs] = (kf * cos + krot * sin).astype(BF16)
    v_ref[...] = _dot(ckvn, wv_ref[...]).astype(BF16)


def _mla_prep(za, zkr, pos, inv128, qn, kvn, wq, wqr, wk, wv, e, er, rb):
    r = za.shape[0]
    hw = MLA_HEADS * HEAD_PAD
    vw = MLA_HEADS * MLA_V
    return pl.pallas_call(
        _mla_prep_kernel,
        out_shape=[jax.ShapeDtypeStruct((r, hw), BF16), jax.ShapeDtypeStruct((r, hw), BF16),
                   jax.ShapeDtypeStruct((r, vw), BF16)],
        grid=(r // rb,),
        in_specs=[
            pl.BlockSpec((rb, MLA_Q_RANK), lambda i: (i, 2)),
            pl.BlockSpec((rb, MLA_KV_RANK), lambda i: (i, 3)),
            pl.BlockSpec((rb, ZKR_W), lambda i: (i, 0)),
            pl.BlockSpec((rb, 1), lambda i: (i, 0)),
            _const_spec((1, HEAD_PAD)),
            _const_spec((1, MLA_Q_RANK)),
            _const_spec((1, MLA_KV_RANK)),
            _const_spec((MLA_Q_RANK, hw)),
            _const_spec((MLA_Q_RANK, hw)),
            _const_spec((MLA_KV_RANK, hw)),
            _const_spec((MLA_KV_RANK, vw)),
            _const_spec((ZKR_W, hw)),
            _const_spec((ZKR_W, hw)),
        ],
        out_specs=[pl.BlockSpec((rb, hw), lambda i: (i, 0)), pl.BlockSpec((rb, hw), lambda i: (i, 0)),
                   pl.BlockSpec((rb, vw), lambda i: (i, 0))],
        compiler_params=_cparams(1),
        name="mla_prep",
    )(za, za, zkr, pos, inv128, qn, kvn, wq, wqr, wk, wv, e, er)


def _attn_kernel(q_ref, k_ref, v_ref, o_ref, vt_ref, *, n_heads, tq, nq):
    bpq = tq // T_BLK
    vw = n_heads * MLA_V

    for j in range(nq):
        vb = _load_rows(v_ref, 1 + j * bpq, bpq, slice(0, vw))
        vt_ref[j] = vb.astype(F32).T.astype(BF16)
    v0t = v_ref[0].astype(F32).T.astype(BF16)

    def col_softmax_update(m, l, acc, st, vt):
        mn = jnp.maximum(m, jnp.max(st, axis=0, keepdims=True))
        a = jnp.exp(m - mn)
        p = jnp.exp(st - mn)
        l = a * l + jnp.sum(p, axis=0, keepdims=True)
        acc = a * acc + _dot(vt, p.astype(BF16))
        return mn, l, acc

    for hp in range(n_heads // 2):
        heads = (2 * hp, 2 * hp + 1)
        qs = [slice(h * HEAD_PAD, (h + 1) * HEAD_PAD) for h in heads]
        vs = [slice(h * MLA_V, (h + 1) * MLA_V) for h in heads]
        k0 = [k_ref[0, :, c] for c in qs]

        outs = []
        for n in range(2):
            st = _dot_nt(k0[n], q_ref[0, :, qs[n]])
            r = lax.broadcasted_iota(jnp.int32, st.shape, 0)
            c = lax.broadcasted_iota(jnp.int32, st.shape, 1)
            st = jnp.where(r <= c, st, MASK_VALUE)
            p = jnp.exp(st - jnp.max(st, axis=0, keepdims=True))
            outs.append(_dot(v0t[vs[n], :], p.astype(BF16)) / jnp.sum(p, axis=0, keepdims=True))
        o_ref[0, :, 2 * hp * MLA_V:(2 * hp + 2) * MLA_V] = jnp.concatenate(outs, axis=0).T.astype(BF16)

        def q_block(i, carry):
            blk0 = 1 + i * bpq
            qt = [_load_rows(q_ref, blk0, bpq, c).astype(F32).T.astype(BF16) for c in qs]
            state = []
            for n in range(2):
                s0 = _dot(k0[n], qt[n])
                m = jnp.max(s0, axis=0, keepdims=True)
                p0 = jnp.exp(s0 - m)
                state += [m, jnp.sum(p0, axis=0, keepdims=True), _dot(v0t[vs[n], :], p0.astype(BF16))]

            def scores(j, n):
                return _dot(_load_rows(k_ref, 1 + j * bpq, bpq, qs[n]), qt[n])

            s_first = tuple(scores(0, n) for n in range(2))

            def kv_block(j, carry):
                st6, s_cur = carry
                s_next = tuple(scores(j + 1, n) for n in range(2))
                out = []
                for n in range(2):
                    out += col_softmax_update(*st6[3 * n:3 * n + 3], s_cur[n], vt_ref[j, vs[n], :])
                return tuple(out), s_next

            state, s_last = lax.fori_loop(0, i, kv_block, (tuple(state), s_first))
            outs = []
            for n in range(2):
                st = s_last[n]
                r = lax.broadcasted_iota(jnp.int32, st.shape, 0)
                c = lax.broadcasted_iota(jnp.int32, st.shape, 1)
                st = jnp.where(r <= c, st, MASK_VALUE)
                m, l, acc = col_softmax_update(*state[3 * n:3 * n + 3], st, vt_ref[i, vs[n], :])
                outs.append(acc / l)
            o = jnp.concatenate(outs, axis=0).T.astype(BF16)
            _store_rows(o_ref, blk0, bpq, slice(2 * hp * MLA_V, (2 * hp + 2) * MLA_V), o)
            return carry

        lax.fori_loop(0, nq, q_block, 0)


def _attention(q, k, v, nb, nblk):
    hps = ATT_HEADS_PER_STEP
    n_hp = MLA_HEADS // hps
    qw = hps * HEAD_PAD
    vw = hps * MLA_V
    s = (nblk - 1) * T_BLK
    tq = min(ATT_TQ, s)
    nq = s // tq
    col = lambda b, p: p
    o = pl.pallas_call(
        functools.partial(_attn_kernel, n_heads=hps, tq=tq, nq=nq),
        out_shape=jax.ShapeDtypeStruct((nblk, nb, T_BLK, MLA_HEADS * MLA_V), BF16),
        grid=(nb, n_hp),
        in_specs=[_seq_spec(nblk, qw, col), _seq_spec(nblk, qw, col), _seq_spec(nblk, vw, col)],
        out_specs=_seq_spec(nblk, vw, col),
        scratch_shapes=[pltpu.VMEM((nq, vw, tq), BF16)],
        compiler_params=_cparams(2),
        name="mla_attn",
    )(q.reshape(nblk, nb, T_BLK, -1), k.reshape(nblk, nb, T_BLK, -1), v.reshape(nblk, nb, T_BLK, -1))
    return o.reshape(nblk * nb * T_BLK, MLA_HEADS * MLA_V)


def _s5_kernel(u_ref, perm_ref, bm_ref, cm_ref, ar_ref, ai_ref, d_ref, wg_ref, o_ref,
               xr_ref, xi_ref, buf_ref, y_ref, *, nb):
    n_slab = S5_WIDTH // S5_SLAB
    ns = S5_SLAB_STATE

    @pl.when(pl.program_id(0) == 0)
    def _():
        xr_ref[...] = jnp.zeros_like(xr_ref)
        xi_ref[...] = jnp.zeros_like(xi_ref)

    u = _dot(perm_ref[...], u_ref[...]).astype(BF16)
    for j in range(n_slab):
        buf_ref[...] = _dot(u[:, j * S5_SLAB:(j + 1) * S5_SLAB], bm_ref[j])
        ar = jnp.broadcast_to(ar_ref[j], (nb, ns))
        ai = jnp.broadcast_to(ai_ref[j], (nb, ns))
        xr = xr_ref[j]
        xi = xi_ref[j]
        for t in range(T_BLK):
            rows = slice(t * nb, (t + 1) * nb)
            nr = ar * xr - ai * xi + buf_ref[rows, 0:ns]
            ni = ar * xi + ai * xr + buf_ref[rows, ns:2 * ns]
            buf_ref[rows, 0:ns] = nr
            buf_ref[rows, ns:2 * ns] = ni
            xr, xi = nr, ni
        xr_ref[j] = xr
        xi_ref[j] = xi
        y_ref[:, j * S5_SLAB:(j + 1) * S5_SLAB] = _dot(buf_ref[...].astype(BF16), cm_ref[j])

    y = y_ref[...] + d_ref[...] * u.astype(F32)
    y = 0.5 * y * (1.0 + jnp.tanh(math.sqrt(2.0 / math.pi) * (y + 0.044715 * (y * y * y))))
    gate = _sigmoid(_dot(y.astype(BF16), wg_ref[...]))
    out = (y * gate).astype(BF16)
    o_ref[...] = _dot_tn(perm_ref[...], out).astype(BF16)


def _s5(za, perm, bm, cm, ar, ai, d, wg, nb):
    r = za.shape[0]
    rb = T_BLK * nb
    n_slab = S5_WIDTH // S5_SLAB
    return pl.pallas_call(
        functools.partial(_s5_kernel, nb=nb),
        out_shape=jax.ShapeDtypeStruct((r, S5_WIDTH), BF16),
        grid=(r // rb,),
        in_specs=[
            pl.BlockSpec((rb, S5_WIDTH), lambda i: (i, 0)),
            _const_spec((rb, rb)),
            _const_spec((n_slab, S5_SLAB, 2 * S5_SLAB_STATE)),
            _const_spec((n_slab, 2 * S5_SLAB_STATE, S5_SLAB)),
            _const_spec((n_slab, 1, S5_SLAB_STATE)),
            _const_spec((n_slab, 1, S5_SLAB_STATE)),
            _const_spec((1, S5_WIDTH)),
            _const_spec((S5_WIDTH, S5_WIDTH)),
        ],
        out_specs=pl.BlockSpec((rb, S5_WIDTH), lambda i: (i, 0)),
        scratch_shapes=[
            pltpu.VMEM((n_slab, nb, S5_SLAB_STATE), F32),
            pltpu.VMEM((n_slab, nb, S5_SLAB_STATE), F32),
            pltpu.VMEM((rb, 2 * S5_SLAB_STATE), F32),
            pltpu.VMEM((rb, S5_WIDTH), F32),
        ],
        compiler_params=_cparams(1, "arbitrary"),
        name="s5_scan",
    )(za, perm, bm, cm, ar, ai, d, wg)


def _time_major_perm(nb):
    p = np.zeros((T_BLK * nb, T_BLK * nb), np.float32)
    for b in range(nb):
        for t in range(T_BLK):
            p[t * nb + b, b * T_BLK + t] = 1.0
    return p


def _block_row_bcast(x, m, row):
    t, c = x.shape
    if m == t:
        return jnp.broadcast_to(x[row:row + 1, :], x.shape)
    x3 = x.reshape(t // m, m, c)
    return jnp.broadcast_to(x3[:, row:row + 1, :], x3.shape).reshape(t, c)


def _hgrn_chunk(st, blk0, n_blk, cs, q_ref, f_ref, v_ref, g_ref, lb, onorm, tri_ref, lvl_ref, o_ref):
    t = n_blk * T_BLK
    q = _load_rows(q_ref, blk0, n_blk, cs).astype(F32)
    zf = _load_rows(f_ref, blk0, n_blk, cs).astype(F32)
    v = _load_rows(v_ref, blk0, n_blk, cs)
    g = _load_rows(g_ref, blk0, n_blk, cs).astype(F32)
    e = jnp.exp(-jnp.abs(zf))
    rcp = 1.0 / (1.0 + e)
    pos = zf >= 0.0
    sig_p = jnp.where(pos, rcp, e * rcp)
    sig_n = jnp.where(pos, e * rcp, rcp)
    f = lb + (1.0 - lb) * sig_p
    log_f = jnp.log(jnp.maximum(f, HG_F_MIN))
    k = (1.0 - lb) * sig_n
    hi = log_f.astype(BF16)
    lo = (log_f - hi.astype(F32)).astype(BF16)
    tri = tri_ref[0:t, 0:t]
    cum = _dot(tri, hi) + _dot(tri, lo)
    lvl = lvl_ref[0:t, 0:t]
    r_idx = lax.broadcasted_iota(jnp.int32, (t, HG_KEY), 0)
    c8 = _block_row_bcast(cum, HG_BOTTOM, HG_BOTTOM // 2 - 1)
    qe = (q * jnp.exp(cum - c8)).astype(BF16)
    ke = (k * jnp.exp(c8 - cum)).astype(BF16)
    n_lvl = len(HG_LEVELS)
    scores = jnp.where(lvl == n_lvl, _dot_nt(qe, ke), 0.0)
    for li, m in enumerate(HG_LEVELS):
        if m > t:
            continue
        half = m // 2
        cmid = _block_row_bcast(cum, m, half - 1)
        upper = (r_idx & (m - 1)) >= half
        ex = jnp.exp(jnp.where(upper, cum - cmid, cmid - cum))
        qe = jnp.where(upper, q * ex, 0.0).astype(BF16)
        ke = jnp.where(upper, 0.0, k * ex).astype(BF16)
        scores = scores + jnp.where(lvl == li, _dot_nt(qe, ke), 0.0)
    o = _dot(scores.astype(BF16), v)
    o = o + _dot_nt((q * jnp.exp(cum)).astype(BF16), st.astype(BF16))
    last = cum[t - 1:t, :]
    kd = (k * jnp.exp(last - cum)).astype(BF16)
    st = st * jnp.exp(last) + _dot_tn(v, kd)
    o = _rms_norm(o, onorm)
    _store_rows(o_ref, blk0, n_blk, cs, (o * (g * _sigmoid(g))).astype(BF16))
    return st


def _hgrn_kernel(q_ref, f_ref, v_ref, g_ref, lb_ref, on_ref, tri_ref, lvl_ref, o_ref, *, n_chunks):
    bpc = HG_CHUNK // T_BLK
    for h in range(HG_HEADS):
        cs = slice(h * HG_KEY, (h + 1) * HG_KEY)
        chunk = functools.partial(_hgrn_chunk, cs=cs, q_ref=q_ref, f_ref=f_ref, v_ref=v_ref, g_ref=g_ref,
                                  lb=lb_ref[:, cs], onorm=on_ref[:, cs], tri_ref=tri_ref, lvl_ref=lvl_ref,
                                  o_ref=o_ref)
        st = chunk(jnp.zeros((HG_VAL, HG_KEY), F32), 0, 1)
        lax.fori_loop(0, n_chunks, lambda i, s: chunk(s, 1 + i * bpc, bpc), st)


def _hgrn(zhg, lb, onorm, tri, lvl, nb, nblk):
    z4 = zhg.reshape(nblk, nb, T_BLK, ZHG_W)
    col = lambda j: _seq_spec(nblk, HG_QK, lambda b: j)
    o = pl.pallas_call(
        functools.partial(_hgrn_kernel, n_chunks=(nblk - 1) * T_BLK // HG_CHUNK),
        out_shape=jax.ShapeDtypeStruct((nblk, nb, T_BLK, HG_VW), BF16),
        grid=(nb,),
        in_specs=[col(0), col(1), col(2), col(3),
                  _const_spec((1, HG_QK)), _const_spec((1, HG_VW)),
                  _const_spec((HG_CHUNK, HG_CHUNK)), _const_spec((HG_CHUNK, HG_CHUNK))],
        out_specs=_seq_spec(nblk, HG_VW, lambda b: 0),
        compiler_params=_cparams(1),
        name="hgrn2",
    )(z4, z4, z4, z4, lb, onorm, tri, lvl)
    return o.reshape(nblk * nb * T_BLK, HG_VW)


def _hgrn_level_matrix():
    r = np.arange(HG_CHUNK)[:, None]
    c = np.arange(HG_CHUNK)[None, :]
    lvl = np.zeros((HG_CHUNK, HG_CHUNK), np.int32)
    for li, m in enumerate(HG_LEVELS):
        lvl = np.where(r // m == c // m, li, lvl)
    lvl = np.where(r // HG_BOTTOM == c // HG_BOTTOM, len(HG_LEVELS), lvl)
    return np.where(c <= r, lvl, -1).astype(np.int32)


def _merge_kernel(om_ref, os_ref, oh_ref, gm_ref, gs_ref, gh_ref, h_ref,
                  wm_ref, ws_ref, wh_ref, wo_ref, g_ref, b_ref, o_ref, *, alpha):
    mixed = _sigmoid(gm_ref[...].astype(F32)) * _dot(om_ref[...], wm_ref[...])
    mixed += _sigmoid(gs_ref[...].astype(F32)) * _dot(os_ref[...], ws_ref[...])
    mixed += _sigmoid(gh_ref[...].astype(F32)) * _dot(oh_ref[...], wh_ref[...])
    r = alpha * h_ref[...] + _dot(mixed.astype(BF16), wo_ref[...])
    o_ref[...] = _layer_norm(r, g_ref[...], b_ref[...])


def _merge(om, os_, oh, zg, h, wm, ws, wh, wo, g, b, rb, alpha, skip):
    r, d = h.shape
    bw = om.shape[1]
    row = lambda width, j=0: pl.BlockSpec((rb, width), lambda i: (i + skip, j))
    return pl.pallas_call(
        functools.partial(_merge_kernel, alpha=alpha),
        out_shape=jax.ShapeDtypeStruct((r - skip * rb, d), F32),
        grid=(r // rb - skip,),
        in_specs=[row(bw), row(bw), row(bw), row(d, 0), row(d, 1), row(d, 2), row(d),
                  _const_spec((bw, d)), _const_spec((bw, d)), _const_spec((bw, d)), _const_spec((d, d)),
                  _const_spec((1, d)), _const_spec((1, d))],
        out_specs=pl.BlockSpec((rb, d), lambda i: (i, 0)),
        compiler_params=_cparams(1),
        name="merge",
    )(om, os_, oh, zg, zg, zg, h, wm, ws, wh, wo, g, b)


def _ffn_body(h_ref, wg_ref, wu_ref, wd_ref, g_ref, b_ref, alpha, fc):
    h = h_ref[...]
    hb = h.astype(BF16)
    acc = alpha * h
    for c in range(0, wg_ref.shape[1], fc):
        a = _dot(hb, wg_ref[:, c:c + fc])
        u = _dot(hb, wu_ref[:, c:c + fc])
        acc += _dot((a * _sigmoid(a) * u).astype(BF16), wd_ref[c:c + fc, :])
    return _layer_norm(acc, g_ref[...], b_ref[...])


def _ffn_kernel(h_ref, wg_ref, wu_ref, wd_ref, g_ref, b_ref, o_ref, *, alpha, fc):
    o_ref[...] = _ffn_body(h_ref, wg_ref, wu_ref, wd_ref, g_ref, b_ref, alpha, fc)


def _ffn_final_kernel(h_ref, wg_ref, wu_ref, wd_ref, g_ref, b_ref, o_ref, *, alpha, fc):
    y = _ffn_body(h_ref, wg_ref, wu_ref, wd_ref, g_ref, b_ref, alpha, fc)
    o_ref[...] = y.reshape(o_ref.shape)


def _ffn_chunk(dff):
    return dff // 2 if dff % 256 == 0 else dff


def _ffn(h, wg, wu, wd, g, b, rb, alpha):
    r, d = h.shape
    dff = wg.shape[1]
    row = pl.BlockSpec((rb, d), lambda i: (i, 0))
    return pl.pallas_call(
        functools.partial(_ffn_kernel, alpha=alpha, fc=_ffn_chunk(dff)),
        out_shape=jax.ShapeDtypeStruct((r, d), F32),
        grid=(r // rb,),
        in_specs=[row, _const_spec((d, dff)), _const_spec((d, dff)), _const_spec((dff, d)),
                  _const_spec((1, d)), _const_spec((1, d))],
        out_specs=row,
        compiler_params=_cparams(1),
        name="ffn",
    )(h, wg, wu, wd, g, b)


def _ffn_final(h, wg, wu, wd, g, b, nb, alpha):
    r, d = h.shape
    dff = wg.shape[1]
    rb = T_BLK * nb
    s = r // nb
    return pl.pallas_call(
        functools.partial(_ffn_final_kernel, alpha=alpha, fc=_ffn_chunk(dff)),
        out_shape=jax.ShapeDtypeStruct((nb, s, d), F32),
        grid=(s // T_BLK,),
        in_specs=[pl.BlockSpec((rb, d), lambda i: (i, 0)),
                  _const_spec((d, dff)), _const_spec((d, dff)), _const_spec((dff, d)),
                  _const_spec((1, d)), _const_spec((1, d))],
        out_specs=pl.BlockSpec((nb, T_BLK, d), lambda i: (0, i, 0)),
        compiler_params=_cparams(1),
        name="ffn_final",
    )(h, wg, wu, wd, g, b)


def _permute_w_in(w):
    mla_in = MLA_Q_RANK + MLA_KV_RANK + MLA_ROPE
    s5_0, hg_0 = mla_in, mla_in + S5_WIDTH
    g_0 = hg_0 + 2 * HG_QK + 2 * HG_VW
    pad = jnp.zeros((w.shape[0], ZKR_W - MLA_ROPE), w.dtype)
    return jnp.concatenate([w[:, s5_0:hg_0], w[:, 0:MLA_Q_RANK + MLA_KV_RANK], w[:, hg_0:g_0], w[:, g_0:],
                            w[:, MLA_Q_RANK + MLA_KV_RANK:mla_in], pad], axis=1).astype(BF16)


def _rot_half(x):
    x1, x2 = jnp.split(x, 2, axis=-1)
    return jnp.concatenate([-x2, x1], axis=-1)


def _mla_weights(w_uq, w_ukv):
    rq, rkv = w_uq.shape[0], w_ukv.shape[0]
    zpad = HEAD_PAD - MLA_NOPE - MLA_ROPE
    wq = w_uq.reshape(rq, MLA_HEADS, MLA_NOPE + MLA_ROPE)
    q_nope, q_rope = wq[..., :MLA_NOPE], wq[..., MLA_NOPE:]
    zq = jnp.zeros((rq, MLA_HEADS, zpad), w_uq.dtype)
    wq_p = jnp.concatenate([q_nope, q_rope, zq], axis=-1).reshape(rq, -1)
    wq_r = jnp.concatenate([jnp.zeros_like(q_nope), _rot_half(q_rope), zq], axis=-1).reshape(rq, -1)
    wkv = w_ukv.reshape(rkv, MLA_HEADS, MLA_NOPE + MLA_V)
    zk = jnp.zeros((rkv, MLA_HEADS, HEAD_PAD - MLA_NOPE), w_ukv.dtype)
    wk_p = jnp.concatenate([wkv[..., :MLA_NOPE], zk], axis=-1).reshape(rkv, -1)
    wv = wkv[..., MLA_NOPE:].reshape(rkv, -1)
    eye = jnp.eye(ZKR_W, MLA_ROPE, dtype=F32)
    place = jnp.concatenate([jnp.zeros((ZKR_W, MLA_NOPE), F32), eye, jnp.zeros((ZKR_W, zpad), F32)], axis=-1)
    place_r = jnp.concatenate([jnp.zeros((ZKR_W, MLA_NOPE), F32), _rot_half(eye),
                               jnp.zeros((ZKR_W, zpad), F32)], axis=-1)
    e = jnp.tile(place, (1, MLA_HEADS))
    er = jnp.tile(place_r, (1, MLA_HEADS))
    return [a.astype(BF16) for a in (wq_p, wq_r, wk_p, wv, e, er)]


def _rope_inv():
    inv = ROPE_THETA ** (-(jnp.arange(0, MLA_ROPE, 2, dtype=F32) / MLA_ROPE))
    z = jnp.zeros((HEAD_PAD - MLA_NOPE - MLA_ROPE,), F32)
    return jnp.concatenate([jnp.zeros((MLA_NOPE,), F32), inv, inv, z])[None, :]


def _s5_params(lam_re, lam_im, log_dt, b_re, b_im, c_re, c_im):
    lr = jnp.minimum(lam_re.astype(F32), -1e-4)
    li = lam_im.astype(F32)
    dt = jnp.exp(log_dt.astype(F32))[:, None]
    mag = jnp.exp(lr * dt)
    ab_r = mag * jnp.cos(li * dt)
    ab_i = mag * jnp.sin(li * dt)
    den = lr * lr + li * li
    nr = ab_r - 1.0
    coef_r = ((nr * lr + ab_i * li) / den)[..., None]
    coef_i = ((ab_i * lr - nr * li) / den)[..., None]
    bb_r = coef_r * b_re.astype(F32) - coef_i * b_im.astype(F32)
    bb_i = coef_r * b_im.astype(F32) + coef_i * b_re.astype(F32)
    n_slab = S5_WIDTH // S5_SLAB
    gps = S5_SLAB // S5_GROUP
    eye = jnp.eye(gps, dtype=F32)

    def in_mat(bb):
        b4 = bb.reshape(n_slab, gps, S5_STATE, S5_GROUP)
        return jnp.einsum('jgnc,gh->jgchn', b4, eye).reshape(n_slab, S5_SLAB, S5_SLAB_STATE)

    def out_mat(cc):
        c4 = cc.astype(F32).reshape(n_slab, gps, S5_GROUP, S5_STATE)
        return jnp.einsum('jgcn,gh->jgnhc', c4, eye).reshape(n_slab, S5_SLAB_STATE, S5_SLAB)

    bm = jnp.concatenate([in_mat(bb_r), in_mat(bb_i)], axis=2).astype(BF16)
    cm = jnp.concatenate([out_mat(c_re), -out_mat(c_im)], axis=1).astype(BF16)
    ar = ab_r.reshape(n_slab, 1, S5_SLAB_STATE)
    ai = ab_i.reshape(n_slab, 1, S5_SLAB_STATE)
    return bm, cm, ar, ai


def kernel(x, positions, meta_tokens, ln_in_g, ln_in_b, w_in, mla_q_norm, mla_w_uq, mla_kv_norm, mla_w_ukv,
           s5_lam_re, s5_lam_im, s5_log_dt, s5_b_re, s5_b_im, s5_c_re, s5_c_im, s5_d, s5_w_glu,
           hg_lb_logits, hg_out_norm, w_br_mla, w_br_s5, w_br_hg, w_out, ln1_g, ln1_b,
           w_ffn_gate, w_ffn_up, w_ffn_down, ln2_g, ln2_b):
    nb, s, d = x.shape
    depth = w_in.shape[0]
    nblk = (s + N_META) // T_BLK
    rb = T_BLK * nb
    n_rows = nblk * rb
    alpha = (2 * depth) ** 0.25
    row2 = lambda a: a.astype(F32)[None, :]

    h = _ln_in(x, meta_tokens.astype(x.dtype), row2(ln_in_g), row2(ln_in_b)).reshape(n_rows, d)

    meta_pos = jnp.broadcast_to(jnp.arange(N_META, dtype=jnp.int32)[None, :], (nb, N_META))
    pos = jnp.concatenate([meta_pos, positions.astype(jnp.int32) + N_META], axis=1)
    pos = pos.reshape(nb, nblk, T_BLK).transpose(1, 0, 2).reshape(n_rows, 1)
    inv128 = _rope_inv()
    p_lb = jax.nn.softmax(hg_lb_logits.astype(F32), axis=0)
    lower_bounds = jnp.cumsum(p_lb, axis=0) - p_lb[0]
    tri = jnp.asarray(np.tril(np.ones((HG_CHUNK, HG_CHUNK), np.float32)), BF16)
    lvl = jnp.asarray(_hgrn_level_matrix())
    perm = jnp.asarray(_time_major_perm(nb), BF16)

    out = None
    for li in range(depth):
        za, zhg, zg, zkr = _proj_in(h, _permute_w_in(w_in[li]), rb)
        q, k, v = _mla_prep(za, zkr, pos, inv128, row2(mla_q_norm[li]), row2(mla_kv_norm[li]),
                            *_mla_weights(mla_w_uq[li], mla_w_ukv[li]), rb)
        o_mla = _attention(q, k, v, nb, nblk)
        bm, cm, ar, ai = _s5_params(s5_lam_re[li], s5_lam_im[li], s5_log_dt[li], s5_b_re[li], s5_b_im[li],
                                    s5_c_re[li], s5_c_im[li])
        y_s5 = _s5(za, perm, bm, cm, ar, ai, row2(s5_d[li]), s5_w_glu[li].astype(BF16), nb)
        o_hg = _hgrn(zhg, lower_bounds[li][None, :], row2(hg_out_norm[li]), tri, lvl, nb, nblk)
        last = li == depth - 1
        skip = 1 if last else 0
        h1 = _merge(o_mla, y_s5, o_hg, zg, h, w_br_mla[li].astype(BF16), w_br_s5[li].astype(BF16),
                    w_br_hg[li].astype(BF16), w_out[li].astype(BF16), row2(ln1_g[li]), row2(ln1_b[li]),
                    rb, alpha, skip)
        ffn_w = (w_ffn_gate[li].astype(BF16), w_ffn_up[li].astype(BF16), w_ffn_down[li].astype(BF16),
                 row2(ln2_g[li]), row2(ln2_b[li]))
        if last:
            out = _ffn_final(h1, *ffn_w, nb, alpha)
        else:
            h = _ffn(h1, *ffn_w, rb, alpha)
    return out
```

```python
import functools
import math

import jax
import jax.numpy as jnp
import numpy as np
from jax import lax
from jax.experimental import pallas as pl
from jax.experimental.pallas import tpu as pltpu

F32 = jnp.float32
BF16 = jnp.bfloat16

N_META = 16
MLA_HEADS = 8
MLA_NOPE = 64
MLA_ROPE = 32
MLA_V = 64
MLA_Q_RANK = 256
MLA_KV_RANK = 256
ROPE_THETA = 10000.0
MASK_VALUE = -1e9
LOG2_E = math.log2(math.e)
HEAD_PAD = 128
S5_WIDTH = 512
S5_GROUP = 16
S5_GROUPS = S5_WIDTH // S5_GROUP
S5_STATE = 64
S5_SLAB = 128
S5_SLAB_STATE = (S5_SLAB // S5_GROUP) * S5_STATE
HG_HEADS = 4
HG_KEY = 128
HG_VAL = 128
HG_QK = HG_HEADS * HG_KEY
HG_VW = HG_HEADS * HG_VAL
HG_F_MIN = 1e-6
HG_CHUNK = 128
HG_LEVELS = (128, 64, 32, 16)
HG_BOTTOM = 8
N_BRANCH = 3
T_BLK = 16
ATT_TQ = 256
ATT_HEADS_PER_STEP = 4
ATT_HEADS_PER_BODY = 4
VMEM_LIMIT = 56 * 1024 * 1024

ZA_W = 1024
ZHG_W = 2048
ZG_W = 3072
ZKR_W = 128
Z_W = ZA_W + ZHG_W + ZG_W + ZKR_W


def _cparams(n_grid, sem="parallel"):
    return pltpu.CompilerParams(dimension_semantics=(sem,) * n_grid, vmem_limit_bytes=VMEM_LIMIT)


def _const_spec(shape):
    nd = len(shape)
    return pl.BlockSpec(shape, lambda *_: (0,) * nd, pipeline_mode=pl.Buffered(1))


def _seq_spec(nblk, width, col):
    return pl.BlockSpec((nblk, None, T_BLK, width), lambda b, *g: (0, b, 0, col(b, *g)))


def _dot(a, b):
    return jnp.dot(a, b, preferred_element_type=F32)


def _dot_nt(a, b):
    return lax.dot_general(a, b, (((1,), (1,)), ((), ())), preferred_element_type=F32)


def _dot_tn(a, b):
    return lax.dot_general(a, b, (((0,), (0,)), ((), ())), preferred_element_type=F32)


def _sigmoid(x):
    return 1.0 / (1.0 + jnp.exp(-x))


def _layer_norm(x, g, b, eps=1e-5):
    mu = jnp.mean(x, axis=-1, keepdims=True)
    xc = x - mu
    var = jnp.mean(xc * xc, axis=-1, keepdims=True)
    return xc * lax.rsqrt(var + eps) * g + b


def _rms_norm(x, g, eps=1e-6):
    return x * lax.rsqrt(jnp.mean(x * x, axis=-1, keepdims=True) + eps) * g


def _load_rows(ref, blk0, n_blk, cs):
    x = ref[pl.ds(blk0, n_blk), :, cs]
    return x.reshape(n_blk * T_BLK, x.shape[-1])


def _store_rows(ref, blk0, n_blk, cs, x):
    ref[pl.ds(blk0, n_blk), :, cs] = x.reshape(n_blk, T_BLK, x.shape[-1])


def _ln_in_kernel(x_ref, meta_ref, g_ref, b_ref, o_ref, *, nb):
    i = pl.program_id(0)

    @pl.when(i == 0)
    def _():
        y = _layer_norm(meta_ref[...], g_ref[...], b_ref[...])
        for b in range(nb):
            o_ref[0, b] = y

    @pl.when(i > 0)
    def _():
        o_ref[0] = _layer_norm(x_ref[...], g_ref[...], b_ref[...])


def _ln_in(x, meta, g, b):
    nb, s, d = x.shape
    nblk = (s + N_META) // T_BLK
    return pl.pallas_call(
        functools.partial(_ln_in_kernel, nb=nb),
        out_shape=jax.ShapeDtypeStruct((nblk, nb, T_BLK, d), F32),
        grid=(nblk,),
        in_specs=[
            pl.BlockSpec((nb, T_BLK, d), lambda i: (0, jnp.maximum(i - 1, 0), 0)),
            _const_spec((N_META, d)),
            _const_spec((1, d)),
            _const_spec((1, d)),
        ],
        out_specs=pl.BlockSpec((1, nb, T_BLK, d), lambda i: (i, 0, 0, 0)),
        compiler_params=_cparams(1),
        name="ln_in",
    )(x, meta, g, b)


def _proj_in_kernel(h_ref, w_ref, za_ref, zhg_ref, zg_ref, zkr_ref):
    x = h_ref[...].astype(BF16)
    za_ref[...] = _dot(x, w_ref[:, 0:ZA_W]).astype(BF16)
    o = ZA_W
    for c in range(0, ZHG_W, 1024):
        zhg_ref[:, c:c + 1024] = _dot(x, w_ref[:, o + c:o + c + 1024]).astype(BF16)
    o += ZHG_W
    for c in range(0, ZG_W, 1024):
        zg_ref[:, c:c + 1024] = _dot(x, w_ref[:, o + c:o + c + 1024]).astype(BF16)
    o += ZG_W
    zkr_ref[...] = _dot(x, w_ref[:, o:o + ZKR_W]).astype(BF16)


def _proj_in(h, w, rb):
    r, d = h.shape
    row = lambda width: pl.BlockSpec((rb, width), lambda i: (i, 0))
    return pl.pallas_call(
        _proj_in_kernel,
        out_shape=[jax.ShapeDtypeStruct((r, wd), BF16) for wd in (ZA_W, ZHG_W, ZG_W, ZKR_W)],
        grid=(r // rb,),
        in_specs=[row(d), _const_spec((d, Z_W))],
        out_specs=[row(ZA_W), row(ZHG_W), row(ZG_W), row(ZKR_W)],
        compiler_params=_cparams(1),
        name="proj_in",
    )(h, w)


def _mla_prep_kernel(cq_ref, ckv_ref, kr_ref, pos_ref, inv_ref, qn_ref, kvn_ref,
                     wq_ref, wqr_ref, wk_ref, wv_ref, e_ref, er_ref,
                     q_ref, k_ref, v_ref):
    scale = (MLA_NOPE + MLA_ROPE) ** -0.5 * LOG2_E
    ang = pos_ref[...].astype(F32) * inv_ref[...]
    cos = jnp.cos(ang)
    sin = jnp.sin(ang)
    cqn = _rms_norm(cq_ref[...].astype(F32), qn_ref[...]).astype(BF16)
    ckvn = _rms_norm(ckv_ref[...].astype(F32), kvn_ref[...]).astype(BF16)
    kr = kr_ref[...]
    cos_q = cos * scale
    sin_q = sin * scale
    for h in range(MLA_HEADS):
        cs = slice(h * HEAD_PAD, (h + 1) * HEAD_PAD)
        qf = _dot(cqn, wq_ref[:, cs])
        qr = _dot(cqn, wqr_ref[:, cs])
        q_ref[:, cs] = (qf * cos_q + qr * sin_q).astype(BF16)
        kf = _dot(ckvn, wk_ref[:, cs]) + _dot(kr, e_ref[:, cs])
        krot = _dot(kr, er_ref[:, cs])
        k_ref[:, cs] = (kf * cos + krot * sin).astype(BF16)
    v_ref[...] = _dot(ckvn, wv_ref[...]).astype(BF16)


def _mla_prep(za, zkr, pos, inv128, qn, kvn, wq, wqr, wk, wv, e, er, rb):
    r = za.shape[0]
    hw = MLA_HEADS * HEAD_PAD
    vw = MLA_HEADS * MLA_V
    return pl.pallas_call(
        _mla_prep_kernel,
        out_shape=[jax.ShapeDtypeStruct((r, hw), BF16), jax.ShapeDtypeStruct((r, hw), BF16),
                   jax.ShapeDtypeStruct((r, vw), BF16)],
        grid=(r // rb,),
        in_specs=[
            pl.BlockSpec((rb, MLA_Q_RANK), lambda i: (i, 2)),
            pl.BlockSpec((rb, MLA_KV_RANK), lambda i: (i, 3)),
            pl.BlockSpec((rb, ZKR_W), lambda i: (i, 0)),
            pl.BlockSpec((rb, 1), lambda i: (i, 0)),
            _const_spec((1, HEAD_PAD)),
            _const_spec((1, MLA_Q_RANK)),
            _const_spec((1, MLA_KV_RANK)),
            _const_spec((MLA_Q_RANK, hw)),
            _const_spec((MLA_Q_RANK, hw)),
            _const_spec((MLA_KV_RANK, hw)),
            _const_spec((MLA_KV_RANK, vw)),
            _const_spec((ZKR_W, hw)),
            _const_spec((ZKR_W, hw)),
        ],
        out_specs=[pl.BlockSpec((rb, hw), lambda i: (i, 0)), pl.BlockSpec((rb, hw), lambda i: (i, 0)),
                   pl.BlockSpec((rb, vw), lambda i: (i, 0))],
        compiler_params=_cparams(1),
        name="mla_prep",
    )(za, za, zkr, pos, inv128, qn, kvn, wq, wqr, wk, wv, e, er)


def _attn_kernel(q_ref, k_ref, v_ref, o_ref, vt_ref, *, n_heads, hb, tq, nq):
    bpq = tq // T_BLK
    vw = n_heads * MLA_V

    for j in range(nq):
        vb = _load_rows(v_ref, 1 + j * bpq, bpq, slice(0, vw))
        vt_ref[j] = vb.astype(F32).T.astype(BF16)
    v0t = v_ref[0].astype(F32).T.astype(BF16)

    def col_softmax_update(m, l, acc, st, vt):
        mn = jnp.maximum(m, jnp.max(st, axis=0, keepdims=True))
        a = jnp.exp2(m - mn)
        p = jnp.exp2(st - mn)
        l = a * l + jnp.sum(p, axis=0, keepdims=True)
        acc = a * acc + _dot(vt, p.astype(BF16))
        return mn, l, acc

    def causal(st, limit):
        r = lax.broadcasted_iota(jnp.int32, st.shape, 0)
        c = lax.broadcasted_iota(jnp.int32, st.shape, 1)
        return jnp.where(r - c <= limit, st, MASK_VALUE * LOG2_E)

    for g0 in range(0, n_heads, hb):
        heads = range(g0, g0 + hb)
        qs = [slice(h * HEAD_PAD, (h + 1) * HEAD_PAD) for h in heads]
        vs = [slice(h * MLA_V, (h + 1) * MLA_V) for h in heads]
        os_ = slice(g0 * MLA_V, (g0 + hb) * MLA_V)
        k0 = [k_ref[0, :, c] for c in qs]

        outs = []
        for n in range(hb):
            st = causal(_dot_nt(k0[n], q_ref[0, :, qs[n]]), 0)
            p = jnp.exp2(st - jnp.max(st, axis=0, keepdims=True))
            outs.append(_dot(v0t[vs[n], :], p.astype(BF16)) / jnp.sum(p, axis=0, keepdims=True))
        o_ref[0, :, os_] = jnp.concatenate(outs, axis=0).T.astype(BF16)

        def q_block(i, carry):
            blk0 = 1 + i * bpq
            qt = [_load_rows(q_ref, blk0, bpq, c).astype(F32).T.astype(BF16) for c in qs]
            s_meta = [_dot(k0[n], qt[n]) for n in range(hb)]

            def scores(j, n):
                return _dot(_load_rows(k_ref, 1 + j * bpq, bpq, qs[n]), qt[n])

            state = []
            for n in range(hb):
                state += [jnp.full((1, tq), -1e30, F32), jnp.zeros((1, tq), F32), jnp.zeros((MLA_V, tq), F32)]
            s_first = tuple(scores(0, n) for n in range(hb))

            def kv_block(j, carry):
                st, s_cur = carry
                s_next = tuple(scores(j + 1, n) for n in range(hb))
                out = []
                for n in range(hb):
                    out += col_softmax_update(*st[3 * n:3 * n + 3], s_cur[n], vt_ref[j, vs[n], :])
                return tuple(out), s_next

            state, s_diag = lax.fori_loop(0, i, kv_block, (tuple(state), s_first))
            outs = []
            for n in range(hb):
                st3 = col_softmax_update(*state[3 * n:3 * n + 3], causal(s_diag[n], 0), vt_ref[i, vs[n], :])
                m, l, acc = col_softmax_update(*st3, s_meta[n], v0t[vs[n], :])
                outs.append(acc / l)
            o = jnp.concatenate(outs, axis=0).T.astype(BF16)
            _store_rows(o_ref, blk0, bpq, os_, o)
            return carry

        lax.fori_loop(0, nq, q_block, 0)


def _attention(q, k, v, nb, nblk):
    hps = ATT_HEADS_PER_STEP
    n_hp = MLA_HEADS // hps
    qw = hps * HEAD_PAD
    vw = hps * MLA_V
    s = (nblk - 1) * T_BLK
    tq = min(ATT_TQ, s)
    nq = s // tq
    col = lambda b, p: p
    o = pl.pallas_call(
        functools.partial(_attn_kernel, n_heads=hps, hb=ATT_HEADS_PER_BODY, tq=tq, nq=nq),
        out_shape=jax.ShapeDtypeStruct((nblk, nb, T_BLK, MLA_HEADS * MLA_V), BF16),
        grid=(nb, n_hp),
        in_specs=[_seq_spec(nblk, qw, col), _seq_spec(nblk, qw, col), _seq_spec(nblk, vw, col)],
        out_specs=_seq_spec(nblk, vw, col),
        scratch_shapes=[pltpu.VMEM((nq, vw, tq), BF16)],
        compiler_params=_cparams(2),
        name="mla_attn",
    )(q.reshape(nblk, nb, T_BLK, -1), k.reshape(nblk, nb, T_BLK, -1), v.reshape(nblk, nb, T_BLK, -1))
    return o.reshape(nblk * nb * T_BLK, MLA_HEADS * MLA_V)


def _s5_kernel(u_ref, perm_ref, bm_ref, cm_ref, ar_ref, ai_ref, d_ref, wg_ref, o_ref,
               xr_ref, xi_ref, buf_ref, y_ref, *, nb):
    n_slab = S5_WIDTH // S5_SLAB
    ns = S5_SLAB_STATE

    @pl.when(pl.program_id(0) == 0)
    def _():
        xr_ref[...] = jnp.zeros_like(xr_ref)
        xi_ref[...] = jnp.zeros_like(xi_ref)

    u = _dot(perm_ref[...], u_ref[...]).astype(BF16)
    for j in range(n_slab):
        buf_ref[...] = _dot(u[:, j * S5_SLAB:(j + 1) * S5_SLAB], bm_ref[j])
        ar = jnp.broadcast_to(ar_ref[j], (nb, ns))
        ai = jnp.broadcast_to(ai_ref[j], (nb, ns))
        xr = xr_ref[j]
        xi = xi_ref[j]
        for t in range(T_BLK):
            rows = slice(t * nb, (t + 1) * nb)
            nr = ar * xr - ai * xi + buf_ref[rows, 0:ns]
            ni = ar * xi + ai * xr + buf_ref[rows, ns:2 * ns]
            buf_ref[rows, 0:ns] = nr
            buf_ref[rows, ns:2 * ns] = ni
            xr, xi = nr, ni
        xr_ref[j] = xr
        xi_ref[j] = xi
        y_ref[:, j * S5_SLAB:(j + 1) * S5_SLAB] = _dot(buf_ref[...].astype(BF16), cm_ref[j])

    y = y_ref[...] + d_ref[...] * u.astype(F32)
    y = 0.5 * y * (1.0 + jnp.tanh(math.sqrt(2.0 / math.pi) * (y + 0.044715 * (y * y * y))))
    gate = _sigmoid(_dot(y.astype(BF16), wg_ref[...]))
    out = (y * gate).astype(BF16)
    o_ref[...] = _dot_tn(perm_ref[...], out).astype(BF16)


def _s5(za, perm, bm, cm, ar, ai, d, wg, nb):
    r = za.shape[0]
    rb = T_BLK * nb
    n_slab = S5_WIDTH // S5_SLAB
    return pl.pallas_call(
        functools.partial(_s5_kernel, nb=nb),
        out_shape=jax.ShapeDtypeStruct((r, S5_WIDTH), BF16),
        grid=(r // rb,),
        in_specs=[
            pl.BlockSpec((rb, S5_WIDTH), lambda i: (i, 0)),
            _const_spec((rb, rb)),
            _const_spec((n_slab, S5_SLAB, 2 * S5_SLAB_STATE)),
            _const_spec((n_slab, 2 * S5_SLAB_STATE, S5_SLAB)),
            _const_spec((n_slab, 1, S5_SLAB_STATE)),
            _const_spec((n_slab, 1, S5_SLAB_STATE)),
            _const_spec((1, S5_WIDTH)),
            _const_spec((S5_WIDTH, S5_WIDTH)),
        ],
        out_specs=pl.BlockSpec((rb, S5_WIDTH), lambda i: (i, 0)),
        scratch_shapes=[
            pltpu.VMEM((n_slab, nb, S5_SLAB_STATE), F32),
            pltpu.VMEM((n_slab, nb, S5_SLAB_STATE), F32),
            pltpu.VMEM((rb, 2 * S5_SLAB_STATE), F32),
            pltpu.VMEM((rb, S5_WIDTH), F32),
        ],
        compiler_params=_cparams(1, "arbitrary"),
        name="s5_scan",
    )(za, perm, bm, cm, ar, ai, d, wg)


def _time_major_perm(nb):
    p = np.zeros((T_BLK * nb, T_BLK * nb), np.float32)
    for b in range(nb):
        for t in range(T_BLK):
            p[t * nb + b, b * T_BLK + t] = 1.0
    return p


def _block_row_bcast(x, m, row):
    t, c = x.shape
    if m == t:
        return jnp.broadcast_to(x[row:row + 1, :], x.shape)
    x3 = x.reshape(t // m, m, c)
    return jnp.broadcast_to(x3[:, row:row + 1, :], x3.shape).reshape(t, c)


def _hgrn_chunk(blk0, n_blk, q_ref, f_ref, v_ref, g_ref, lb_ref, on_ref, tri_ref, lvl_ref, o_ref, st_ref):
    t = n_blk * T_BLK
    cs = slice(0, HG_QK)
    heads = [slice(h * HG_KEY, (h + 1) * HG_KEY) for h in range(HG_HEADS)]
    lb = lb_ref[...]
    q = _load_rows(q_ref, blk0, n_blk, cs).astype(F32)
    zf = _load_rows(f_ref, blk0, n_blk, cs).astype(F32)
    v = _load_rows(v_ref, blk0, n_blk, cs)
    g = _load_rows(g_ref, blk0, n_blk, cs).astype(F32)
    e = jnp.exp(-jnp.abs(zf))
    rcp = 1.0 / (1.0 + e)
    pos = zf >= 0.0
    sig_p = jnp.where(pos, rcp, e * rcp)
    sig_n = jnp.where(pos, e * rcp, rcp)
    f = lb + (1.0 - lb) * sig_p
    log_f = jnp.log(jnp.maximum(f, HG_F_MIN))
    k = (1.0 - lb) * sig_n
    hi = log_f.astype(BF16)
    lo = (log_f - hi.astype(F32)).astype(BF16)
    tri = tri_ref[0:t, 0:t]
    cum = _dot(tri, hi) + _dot(tri, lo)
    lvl = lvl_ref[0:t, 0:t]
    r_idx = lax.broadcasted_iota(jnp.int32, (t, HG_QK), 0)
    c8 = _block_row_bcast(cum, HG_BOTTOM, HG_BOTTOM // 2 - 1)
    qe = (q * jnp.exp(cum - c8)).astype(BF16)
    ke = (k * jnp.exp(c8 - cum)).astype(BF16)
    n_lvl = len(HG_LEVELS)
    scores = [jnp.where(lvl == n_lvl, _dot_nt(qe[:, hs], ke[:, hs]), 0.0) for hs in heads]
    for li, m in enumerate(HG_LEVELS):
        if m > t:
            continue
        half = m // 2
        cmid = _block_row_bcast(cum, m, half - 1)
        upper = (r_idx & (m - 1)) >= half
        ex = jnp.exp(jnp.where(upper, cum - cmid, cmid - cum))
        qe = jnp.where(upper, q * ex, 0.0).astype(BF16)
        ke = jnp.where(upper, 0.0, k * ex).astype(BF16)
        scores = [s + jnp.where(lvl == li, _dot_nt(qe[:, hs], ke[:, hs]), 0.0) for s, hs in zip(scores, heads)]
    qd = (q * jnp.exp(cum)).astype(BF16)
    last = cum[t - 1:t, :]
    kd = (k * jnp.exp(last - cum)).astype(BF16)
    dec = jnp.exp(last)
    outs = []
    for h, hs in enumerate(heads):
        st = st_ref[h]
        o = _dot(scores[h].astype(BF16), v[:, hs]) + _dot_nt(qd[:, hs], st.astype(BF16))
        st_ref[h] = st * dec[:, hs] + _dot_tn(v[:, hs], kd[:, hs])
        outs.append(o * lax.rsqrt(jnp.mean(o * o, axis=-1, keepdims=True) + 1e-6))
    o = jnp.concatenate(outs, axis=1) * on_ref[...]
    _store_rows(o_ref, blk0, n_blk, cs, (o * (g * _sigmoid(g))).astype(BF16))


def _hgrn_kernel(q_ref, f_ref, v_ref, g_ref, lb_ref, on_ref, tri_ref, lvl_ref, o_ref, st_ref, *, n_chunks):
    bpc = HG_CHUNK // T_BLK
    chunk = functools.partial(_hgrn_chunk, q_ref=q_ref, f_ref=f_ref, v_ref=v_ref, g_ref=g_ref, lb_ref=lb_ref,
                              on_ref=on_ref, tri_ref=tri_ref, lvl_ref=lvl_ref, o_ref=o_ref, st_ref=st_ref)
    st_ref[...] = jnp.zeros_like(st_ref)
    chunk(0, 1)

    def body(i, carry):
        chunk(1 + i * bpc, bpc)
        return carry

    lax.fori_loop(0, n_chunks, body, 0)


def _hgrn(zhg, lb, onorm, tri, lvl, nb, nblk):
    z4 = zhg.reshape(nblk, nb, T_BLK, ZHG_W)
    col = lambda j: _seq_spec(nblk, HG_QK, lambda b: j)
    o = pl.pallas_call(
        functools.partial(_hgrn_kernel, n_chunks=(nblk - 1) * T_BLK // HG_CHUNK),
        out_shape=jax.ShapeDtypeStruct((nblk, nb, T_BLK, HG_VW), BF16),
        grid=(nb,),
        in_specs=[col(0), col(1), col(2), col(3),
                  _const_spec((1, HG_QK)), _const_spec((1, HG_VW)),
                  _const_spec((HG_CHUNK, HG_CHUNK)), _const_spec((HG_CHUNK, HG_CHUNK))],
        out_specs=_seq_spec(nblk, HG_VW, lambda b: 0),
        scratch_shapes=[pltpu.VMEM((HG_HEADS, HG_VAL, HG_KEY), F32)],
        compiler_params=_cparams(1),
        name="hgrn2",
    )(z4, z4, z4, z4, lb, onorm, tri, lvl)
    return o.reshape(nblk * nb * T_BLK, HG_VW)


def _hgrn_level_matrix():
    r = np.arange(HG_CHUNK)[:, None]
    c = np.arange(HG_CHUNK)[None, :]
    lvl = np.zeros((HG_CHUNK, HG_CHUNK), np.int32)
    for li, m in enumerate(HG_LEVELS):
        lvl = np.where(r // m == c // m, li, lvl)
    lvl = np.where(r // HG_BOTTOM == c // HG_BOTTOM, len(HG_LEVELS), lvl)
    return np.where(c <= r, lvl, -1).astype(np.int32)


def _merge_kernel(om_ref, os_ref, oh_ref, gm_ref, gs_ref, gh_ref, h_ref,
                  wm_ref, ws_ref, wh_ref, wo_ref, g_ref, b_ref, o_ref, *, alpha):
    mixed = _sigmoid(gm_ref[...].astype(F32)) * _dot(om_ref[...], wm_ref[...])
    mixed += _sigmoid(gs_ref[...].astype(F32)) * _dot(os_ref[...], ws_ref[...])
    mixed += _sigmoid(gh_ref[...].astype(F32)) * _dot(oh_ref[...], wh_ref[...])
    r = alpha * h_ref[...] + _dot(mixed.astype(BF16), wo_ref[...])
    o_ref[...] = _layer_norm(r, g_ref[...], b_ref[...])


def _merge(om, os_, oh, zg, h, wm, ws, wh, wo, g, b, rb, alpha, skip):
    r, d = h.shape
    bw = om.shape[1]
    row = lambda width, j=0: pl.BlockSpec((rb, width), lambda i: (i + skip, j))
    return pl.pallas_call(
        functools.partial(_merge_kernel, alpha=alpha),
        out_shape=jax.ShapeDtypeStruct((r - skip * rb, d), F32),
        grid=(r // rb - skip,),
        in_specs=[row(bw), row(bw), row(bw), row(d, 0), row(d, 1), row(d, 2), row(d),
                  _const_spec((bw, d)), _const_spec((bw, d)), _const_spec((bw, d)), _const_spec((d, d)),
                  _const_spec((1, d)), _const_spec((1, d))],
        out_specs=pl.BlockSpec((rb, d), lambda i: (i, 0)),
        compiler_params=_cparams(1),
        name="merge",
    )(om, os_, oh, zg, zg, zg, h, wm, ws, wh, wo, g, b)


def _ffn_body(h_ref, wg_ref, wu_ref, wd_ref, g_ref, b_ref, alpha, fc):
    h = h_ref[...]
    hb = h.astype(BF16)
    acc = alpha * h
    for c in range(0, wg_ref.shape[1], fc):
        a = _dot(hb, wg_ref[:, c:c + fc])
        u = _dot(hb, wu_ref[:, c:c + fc])
        acc += _dot((a * _sigmoid(a) * u).astype(BF16), wd_ref[c:c + fc, :])
    return _layer_norm(acc, g_ref[...], b_ref[...])


def _ffn_kernel(h_ref, wg_ref, wu_ref, wd_ref, g_ref, b_ref, o_ref, *, alpha, fc):
    o_ref[...] = _ffn_body(h_ref, wg_ref, wu_ref, wd_ref, g_ref, b_ref, alpha, fc)


def _ffn_final_kernel(h_ref, wg_ref, wu_ref, wd_ref, g_ref, b_ref, o_ref, *, alpha, fc):
    y = _ffn_body(h_ref, wg_ref, wu_ref, wd_ref, g_ref, b_ref, alpha, fc)
    o_ref[...] = y.reshape(o_ref.shape)


def _ffn_chunk(dff):
    return dff // 2 if dff % 256 == 0 else dff


def _ffn(h, wg, wu, wd, g, b, rb, alpha):
    r, d = h.shape
    dff = wg.shape[1]
    row = pl.BlockSpec((rb, d), lambda i: (i, 0))
    return pl.pallas_call(
        functools.partial(_ffn_kernel, alpha=alpha, fc=_ffn_chunk(dff)),
        out_shape=jax.ShapeDtypeStruct((r, d), F32),
        grid=(r // rb,),
        in_specs=[row, _const_spec((d, dff)), _const_spec((d, dff)), _const_spec((dff, d)),
                  _const_spec((1, d)), _const_spec((1, d))],
        out_specs=row,
        compiler_params=_cparams(1),
        name="ffn",
    )(h, wg, wu, wd, g, b)


def _ffn_final(h, wg, wu, wd, g, b, nb, alpha):
    r, d = h.shape
    dff = wg.shape[1]
    rb = T_BLK * nb
    s = r // nb
    return pl.pallas_call(
        functools.partial(_ffn_final_kernel, alpha=alpha, fc=_ffn_chunk(dff)),
        out_shape=jax.ShapeDtypeStruct((nb, s, d), F32),
        grid=(s // T_BLK,),
        in_specs=[pl.BlockSpec((rb, d), lambda i: (i, 0)),
                  _const_spec((d, dff)), _const_spec((d, dff)), _const_spec((dff, d)),
                  _const_spec((1, d)), _const_spec((1, d))],
        out_specs=pl.BlockSpec((nb, T_BLK, d), lambda i: (0, i, 0)),
        compiler_params=_cparams(1),
        name="ffn_final",
    )(h, wg, wu, wd, g, b)


def _permute_w_in(w):
    mla_in = MLA_Q_RANK + MLA_KV_RANK + MLA_ROPE
    s5_0, hg_0 = mla_in, mla_in + S5_WIDTH
    g_0 = hg_0 + 2 * HG_QK + 2 * HG_VW
    pad = jnp.zeros((w.shape[0], ZKR_W - MLA_ROPE), w.dtype)
    return jnp.concatenate([w[:, s5_0:hg_0], w[:, 0:MLA_Q_RANK + MLA_KV_RANK], w[:, hg_0:g_0], w[:, g_0:],
                            w[:, MLA_Q_RANK + MLA_KV_RANK:mla_in], pad], axis=1).astype(BF16)


def _rot_half(x):
    x1, x2 = jnp.split(x, 2, axis=-1)
    return jnp.concatenate([-x2, x1], axis=-1)


def _mla_weights(w_uq, w_ukv):
    rq, rkv = w_uq.shape[0], w_ukv.shape[0]
    zpad = HEAD_PAD - MLA_NOPE - MLA_ROPE
    wq = w_uq.reshape(rq, MLA_HEADS, MLA_NOPE + MLA_ROPE)
    q_nope, q_rope = wq[..., :MLA_NOPE], wq[..., MLA_NOPE:]
    zq = jnp.zeros((rq, MLA_HEADS, zpad), w_uq.dtype)
    wq_p = jnp.concatenate([q_nope, q_rope, zq], axis=-1).reshape(rq, -1)
    wq_r = jnp.concatenate([jnp.zeros_like(q_nope), _rot_half(q_rope), zq], axis=-1).reshape(rq, -1)
    wkv = w_ukv.reshape(rkv, MLA_HEADS, MLA_NOPE + MLA_V)
    zk = jnp.zeros((rkv, MLA_HEADS, HEAD_PAD - MLA_NOPE), w_ukv.dtype)
    wk_p = jnp.concatenate([wkv[..., :MLA_NOPE], zk], axis=-1).reshape(rkv, -1)
    wv = wkv[..., MLA_NOPE:].reshape(rkv, -1)
    eye = jnp.eye(ZKR_W, MLA_ROPE, dtype=F32)
    place = jnp.concatenate([jnp.zeros((ZKR_W, MLA_NOPE), F32), eye, jnp.zeros((ZKR_W, zpad), F32)], axis=-1)
    place_r = jnp.concatenate([jnp.zeros((ZKR_W, MLA_NOPE), F32), _rot_half(eye),
                               jnp.zeros((ZKR_W, zpad), F32)], axis=-1)
    e = jnp.tile(place, (1, MLA_HEADS))
    er = jnp.tile(place_r, (1, MLA_HEADS))
    return [a.astype(BF16) for a in (wq_p, wq_r, wk_p, wv, e, er)]


def _rope_inv():
    inv = ROPE_THETA ** (-(jnp.arange(0, MLA_ROPE, 2, dtype=F32) / MLA_ROPE))
    z = jnp.zeros((HEAD_PAD - MLA_NOPE - MLA_ROPE,), F32)
    return jnp.concatenate([jnp.zeros((MLA_NOPE,), F32), inv, inv, z])[None, :]


def _s5_params(lam_re, lam_im, log_dt, b_re, b_im, c_re, c_im):
    lr = jnp.minimum(lam_re.astype(F32), -1e-4)
    li = lam_im.astype(F32)
    dt = jnp.exp(log_dt.astype(F32))[:, None]
    mag = jnp.exp(lr * dt)
    ab_r = mag * jnp.cos(li * dt)
    ab_i = mag * jnp.sin(li * dt)
    den = lr * lr + li * li
    nr = ab_r - 1.0
    coef_r = ((nr * lr + ab_i * li) / den)[..., None]
    coef_i = ((ab_i * lr - nr * li) / den)[..., None]
    bb_r = coef_r * b_re.astype(F32) - coef_i * b_im.astype(F32)
    bb_i = coef_r * b_im.astype(F32) + coef_i * b_re.astype(F32)
    n_slab = S5_WIDTH // S5_SLAB
    gps = S5_SLAB // S5_GROUP
    eye = jnp.eye(gps, dtype=F32)

    def in_mat(bb):
        b4 = bb.reshape(n_slab, gps, S5_STATE, S5_GROUP)
        return jnp.einsum('jgnc,gh->jgchn', b4, eye).reshape(n_slab, S5_SLAB, S5_SLAB_STATE)

    def out_mat(cc):
        c4 = cc.astype(F32).reshape(n_slab, gps, S5_GROUP, S5_STATE)
        return jnp.einsum('jgcn,gh->jgnhc', c4, eye).reshape(n_slab, S5_SLAB_STATE, S5_SLAB)

    bm = jnp.concatenate([in_mat(bb_r), in_mat(bb_i)], axis=2).astype(BF16)
    cm = jnp.concatenate([out_mat(c_re), -out_mat(c_im)], axis=1).astype(BF16)
    ar = ab_r.reshape(n_slab, 1, S5_SLAB_STATE)
    ai = ab_i.reshape(n_slab, 1, S5_SLAB_STATE)
    return bm, cm, ar, ai


def kernel(x, positions, meta_tokens, ln_in_g, ln_in_b, w_in, mla_q_norm, mla_w_uq, mla_kv_norm, mla_w_ukv,
           s5_lam_re, s5_lam_im, s5_log_dt, s5_b_re, s5_b_im, s5_c_re, s5_c_im, s5_d, s5_w_glu,
           hg_lb_logits, hg_out_norm, w_br_mla, w_br_s5, w_br_hg, w_out, ln1_g, ln1_b,
           w_ffn_gate, w_ffn_up, w_ffn_down, ln2_g, ln2_b):
    nb, s, d = x.shape
    depth = w_in.shape[0]
    nblk = (s + N_META) // T_BLK
    rb = T_BLK * nb
    n_rows = nblk * rb
    alpha = (2 * depth) ** 0.25
    row2 = lambda a: a.astype(F32)[None, :]

    h = _ln_in(x, meta_tokens.astype(x.dtype), row2(ln_in_g), row2(ln_in_b)).reshape(n_rows, d)

    meta_pos = jnp.broadcast_to(jnp.arange(N_META, dtype=jnp.int32)[None, :], (nb, N_META))
    pos = jnp.concatenate([meta_pos, positions.astype(jnp.int32) + N_META], axis=1)
    pos = pos.reshape(nb, nblk, T_BLK).transpose(1, 0, 2).reshape(n_rows, 1)
    inv128 = _rope_inv()
    p_lb = jax.nn.softmax(hg_lb_logits.astype(F32), axis=0)
    lower_bounds = jnp.cumsum(p_lb, axis=0) - p_lb[0]
    tri = jnp.asarray(np.tril(np.ones((HG_CHUNK, HG_CHUNK), np.float32)), BF16)
    lvl = jnp.asarray(_hgrn_level_matrix())
    perm = jnp.asarray(_time_major_perm(nb), BF16)

    out = None
    for li in range(depth):
        za, zhg, zg, zkr = _proj_in(h, _permute_w_in(w_in[li]), rb)
        q, k, v = _mla_prep(za, zkr, pos, inv128, row2(mla_q_norm[li]), row2(mla_kv_norm[li]),
                            *_mla_weights(mla_w_uq[li], mla_w_ukv[li]), rb)
        o_mla = _attention(q, k, v, nb, nblk)
        bm, cm, ar, ai = _s5_params(s5_lam_re[li], s5_lam_im[li], s5_log_dt[li], s5_b_re[li], s5_b_im[li],
                                    s5_c_re[li], s5_c_im[li])
        y_s5 = _s5(za, perm, bm, cm, ar, ai, row2(s5_d[li]), s5_w_glu[li].astype(BF16), nb)
        o_hg = _hgrn(zhg, lower_bounds[li][None, :], row2(hg_out_norm[li]), tri, lvl, nb, nblk)
        last = li == depth - 1
        skip = 1 if last else 0
        h1 = _merge(o_mla, y_s5, o_hg, zg, h, w_br_mla[li].astype(BF16), w_br_s5[li].astype(BF16),
                    w_br_hg[li].astype(BF16), w_out[li].astype(BF16), row2(ln1_g[li]), row2(ln1_b[li]),
                    rb, alpha, skip)
        ffn_w = (w_ffn_gate[li].astype(BF16), w_ffn_up[li].astype(BF16), w_ffn_down[li].astype(BF16),
                 row2(ln2_g[li]), row2(ln2_b[li]))
        if last:
            out = _ffn_final(h1, *ffn_w, nb, alpha)
        else:
            h = _ffn(h1, *ffn_w, rb, alpha)
    return out
```

```python
import functools
import math

import jax
import jax.numpy as jnp
import numpy as np
from jax import lax
from jax.experimental import pallas as pl
from jax.experimental.pallas import tpu as pltpu

F32 = jnp.float32
BF16 = jnp.bfloat16

N_META = 16
MLA_HEADS = 8
MLA_NOPE = 64
MLA_ROPE = 32
MLA_V = 64
MLA_Q_RANK = 256
MLA_KV_RANK = 256
ROPE_THETA = 10000.0
MASK_VALUE = -1e9
LOG2_E = math.log2(math.e)
HEAD_PAD = 128
S5_WIDTH = 512
S5_GROUP = 16
S5_GROUPS = S5_WIDTH // S5_GROUP
S5_STATE = 64
S5_SLAB = 128
S5_SLAB_STATE = (S5_SLAB // S5_GROUP) * S5_STATE
HG_HEADS = 4
HG_KEY = 128
HG_VAL = 128
HG_QK = HG_HEADS * HG_KEY
HG_VW = HG_HEADS * HG_VAL
HG_F_MIN = 1e-6
HG_CHUNK = 128
HG_LEVELS = (128, 64, 32, 16)
HG_BOTTOM = 8
N_BRANCH = 3
T_BLK = 16
ATT_TQ = 256
ATT_HEADS_PER_STEP = 4
ATT_HEADS_PER_BODY = 4
VMEM_LIMIT = 56 * 1024 * 1024

ZA_W = 1024
ZHG_W = 2048
ZG_W = 3072
ZKR_W = 128
Z_W = ZA_W + ZHG_W + ZG_W + ZKR_W


def _cparams(n_grid, sem="parallel"):
    return pltpu.CompilerParams(dimension_semantics=(sem,) * n_grid, vmem_limit_bytes=VMEM_LIMIT)


def _const_spec(shape):
    nd = len(shape)
    return pl.BlockSpec(shape, lambda *_: (0,) * nd, pipeline_mode=pl.Buffered(1))


def _seq_spec(nblk, width, col):
    return pl.BlockSpec((nblk, None, T_BLK, width), lambda b, *g: (0, b, 0, col(b, *g)))


def _dot(a, b):
    return jnp.dot(a, b, preferred_element_type=F32)


def _dot_nt(a, b):
    return lax.dot_general(a, b, (((1,), (1,)), ((), ())), preferred_element_type=F32)


def _dot_tn(a, b):
    return lax.dot_general(a, b, (((0,), (0,)), ((), ())), preferred_element_type=F32)


def _sigmoid(x):
    return 1.0 / (1.0 + jnp.exp(-x))


def _layer_norm(x, g, b, eps=1e-5):
    mu = jnp.mean(x, axis=-1, keepdims=True)
    xc = x - mu
    var = jnp.mean(xc * xc, axis=-1, keepdims=True)
    return xc * lax.rsqrt(var + eps) * g + b


def _rms_norm(x, g, eps=1e-6):
    return x * lax.rsqrt(jnp.mean(x * x, axis=-1, keepdims=True) + eps) * g


def _load_rows(ref, blk0, n_blk, cs):
    x = ref[pl.ds(blk0, n_blk), :, cs]
    return x.reshape(n_blk * T_BLK, x.shape[-1])


def _store_rows(ref, blk0, n_blk, cs, x):
    ref[pl.ds(blk0, n_blk), :, cs] = x.reshape(n_blk, T_BLK, x.shape[-1])


def _ln_in_kernel(x_ref, meta_ref, g_ref, b_ref, o_ref, *, nb):
    i = pl.program_id(0)

    @pl.when(i == 0)
    def _():
        y = _layer_norm(meta_ref[...], g_ref[...], b_ref[...])
        for b in range(nb):
            o_ref[0, b] = y

    @pl.when(i > 0)
    def _():
        o_ref[0] = _layer_norm(x_ref[...], g_ref[...], b_ref[...])


def _ln_in(x, meta, g, b):
    nb, s, d = x.shape
    nblk = (s + N_META) // T_BLK
    return pl.pallas_call(
        functools.partial(_ln_in_kernel, nb=nb),
        out_shape=jax.ShapeDtypeStruct((nblk, nb, T_BLK, d), F32),
        grid=(nblk,),
        in_specs=[
            pl.BlockSpec((nb, T_BLK, d), lambda i: (0, jnp.maximum(i - 1, 0), 0)),
            _const_spec((N_META, d)),
            _const_spec((1, d)),
            _const_spec((1, d)),
        ],
        out_specs=pl.BlockSpec((1, nb, T_BLK, d), lambda i: (i, 0, 0, 0)),
        compiler_params=_cparams(1),
        name="ln_in",
    )(x, meta, g, b)


def _mla_prep(cq, ckv, kr, pos_ref, inv_ref, qn_ref, kvn_ref, wq_ref, wqr_ref, wk_ref, wv_ref, e_ref, er_ref,
              q_ref, k_ref, v_ref):
    scale = (MLA_NOPE + MLA_ROPE) ** -0.5 * LOG2_E
    ang = pos_ref[...].astype(F32) * inv_ref[...]
    cos = jnp.cos(ang)
    sin = jnp.sin(ang)
    cqn = _rms_norm(cq, qn_ref[...]).astype(BF16)
    ckvn = _rms_norm(ckv, kvn_ref[...]).astype(BF16)
    kr = kr.astype(BF16)
    cos_q = cos * scale
    sin_q = sin * scale
    for h in range(MLA_HEADS):
        cs = slice(h * HEAD_PAD, (h + 1) * HEAD_PAD)
        qf = _dot(cqn, wq_ref[:, cs])
        qr = _dot(cqn, wqr_ref[:, cs])
        q_ref[:, cs] = (qf * cos_q + qr * sin_q).astype(BF16)
        kf = _dot(ckvn, wk_ref[:, cs]) + _dot(kr, e_ref[:, cs])
        krot = _dot(kr, er_ref[:, cs])
        k_ref[:, cs] = (kf * cos + krot * sin).astype(BF16)
    v_ref[...] = _dot(ckvn, wv_ref[...]).astype(BF16)


def _s5_block(u, perm_ref, bm_ref, cm_ref, ar_ref, ai_ref, d_ref, wg_ref, o_ref,
              xr_ref, xi_ref, buf_ref, y_ref, nb):
    n_slab = S5_WIDTH // S5_SLAB
    ns = S5_SLAB_STATE
    u = _dot(perm_ref[...], u).astype(BF16)
    for j in range(n_slab):
        buf_ref[...] = _dot(u[:, j * S5_SLAB:(j + 1) * S5_SLAB], bm_ref[j])
        ar = jnp.broadcast_to(ar_ref[j], (nb, ns))
        ai = jnp.broadcast_to(ai_ref[j], (nb, ns))
        xr = xr_ref[j]
        xi = xi_ref[j]
        for t in range(T_BLK):
            rows = slice(t * nb, (t + 1) * nb)
            nr = ar * xr - ai * xi + buf_ref[rows, 0:ns]
            ni = ar * xi + ai * xr + buf_ref[rows, ns:2 * ns]
            buf_ref[rows, 0:ns] = nr
            buf_ref[rows, ns:2 * ns] = ni
            xr, xi = nr, ni
        xr_ref[j] = xr
        xi_ref[j] = xi
        y_ref[:, j * S5_SLAB:(j + 1) * S5_SLAB] = _dot(buf_ref[...].astype(BF16), cm_ref[j])

    y = y_ref[...] + d_ref[...] * u.astype(F32)
    y = 0.5 * y * (1.0 + jnp.tanh(math.sqrt(2.0 / math.pi) * (y + 0.044715 * (y * y * y))))
    gate = _sigmoid(_dot(y.astype(BF16), wg_ref[...]))
    out = (y * gate).astype(BF16)
    o_ref[...] = _dot_tn(perm_ref[...], out).astype(BF16)


def _front_kernel(h_ref, w_ref, pos_ref, inv_ref, qn_ref, kvn_ref, wq_ref, wqr_ref, wk_ref, wv_ref, e_ref, er_ref,
                  perm_ref, bm_ref, cm_ref, ar_ref, ai_ref, d_ref, wglu_ref,
                  zhg_ref, zg_ref, q_ref, k_ref, v_ref, ys5_ref,
                  xr_ref, xi_ref, buf_ref, y_ref, *, nb):
    @pl.when(pl.program_id(0) == 0)
    def _():
        xr_ref[...] = jnp.zeros_like(xr_ref)
        xi_ref[...] = jnp.zeros_like(xi_ref)

    x = h_ref[...].astype(BF16)
    za = _dot(x, w_ref[:, 0:ZA_W])
    o = ZA_W
    for c in range(0, ZHG_W, 1024):
        zhg_ref[:, c:c + 1024] = _dot(x, w_ref[:, o + c:o + c + 1024]).astype(BF16)
    o += ZHG_W
    for c in range(0, ZG_W, 1024):
        zg_ref[:, c:c + 1024] = _dot(x, w_ref[:, o + c:o + c + 1024]).astype(BF16)
    o += ZG_W
    zkr = _dot(x, w_ref[:, o:o + ZKR_W])
    c0 = S5_WIDTH
    _mla_prep(za[:, c0:c0 + MLA_Q_RANK], za[:, c0 + MLA_Q_RANK:c0 + MLA_Q_RANK + MLA_KV_RANK], zkr,
              pos_ref, inv_ref, qn_ref, kvn_ref, wq_ref, wqr_ref, wk_ref, wv_ref, e_ref, er_ref,
              q_ref, k_ref, v_ref)
    _s5_block(za[:, 0:S5_WIDTH].astype(BF16), perm_ref, bm_ref, cm_ref, ar_ref, ai_ref, d_ref, wglu_ref, ys5_ref,
              xr_ref, xi_ref, buf_ref, y_ref, nb)


def _front(h, w, pos, inv128, qn, kvn, wq, wqr, wk, wv, e, er, perm, bm, cm, ar, ai, dskip, wglu, nb):
    r, d = h.shape
    rb = T_BLK * nb
    hw = MLA_HEADS * HEAD_PAD
    vw = MLA_HEADS * MLA_V
    n_slab = S5_WIDTH // S5_SLAB
    row = lambda width: pl.BlockSpec((rb, width), lambda i: (i, 0))
    widths = (ZHG_W, ZG_W, hw, hw, vw, S5_WIDTH)
    return pl.pallas_call(
        functools.partial(_front_kernel, nb=nb),
        out_shape=[jax.ShapeDtypeStruct((r, wd), BF16) for wd in widths],
        grid=(r // rb,),
        in_specs=[
            row(d), _const_spec((d, Z_W)), row(1),
            _const_spec((1, HEAD_PAD)), _const_spec((1, MLA_Q_RANK)), _const_spec((1, MLA_KV_RANK)),
            _const_spec((MLA_Q_RANK, hw)), _const_spec((MLA_Q_RANK, hw)), _const_spec((MLA_KV_RANK, hw)),
            _const_spec((MLA_KV_RANK, vw)), _const_spec((ZKR_W, hw)), _const_spec((ZKR_W, hw)),
            _const_spec((rb, rb)),
            _const_spec((n_slab, S5_SLAB, 2 * S5_SLAB_STATE)),
            _const_spec((n_slab, 2 * S5_SLAB_STATE, S5_SLAB)),
            _const_spec((n_slab, 1, S5_SLAB_STATE)),
            _const_spec((n_slab, 1, S5_SLAB_STATE)),
            _const_spec((1, S5_WIDTH)),
            _const_spec((S5_WIDTH, S5_WIDTH)),
        ],
        out_specs=[row(wd) for wd in widths],
        scratch_shapes=[
            pltpu.VMEM((n_slab, nb, S5_SLAB_STATE), F32),
            pltpu.VMEM((n_slab, nb, S5_SLAB_STATE), F32),
            pltpu.VMEM((rb, 2 * S5_SLAB_STATE), F32),
            pltpu.VMEM((rb, S5_WIDTH), F32),
        ],
        compiler_params=_cparams(1, "arbitrary"),
        name="front",
    )(h, w, pos, inv128, qn, kvn, wq, wqr, wk, wv, e, er, perm, bm, cm, ar, ai, dskip, wglu)


def _attn_kernel(q_ref, k_ref, v_ref, o_ref, vt_ref, *, n_heads, hb, tq, nq):
    bpq = tq // T_BLK
    vw = n_heads * MLA_V

    for j in range(nq):
        vb = _load_rows(v_ref, 1 + j * bpq, bpq, slice(0, vw))
        vt_ref[j] = vb.astype(F32).T.astype(BF16)
    v0t = v_ref[0].astype(F32).T.astype(BF16)

    def col_softmax_update(m, l, acc, st, vt):
        mn = jnp.maximum(m, jnp.max(st, axis=0, keepdims=True))
        a = jnp.exp2(m - mn)
        p = jnp.exp2(st - mn)
        l = a * l + jnp.sum(p, axis=0, keepdims=True)
        acc = a * acc + _dot(vt, p.astype(BF16))
        return mn, l, acc

    def causal(st, limit):
        r = lax.broadcasted_iota(jnp.int32, st.shape, 0)
        c = lax.broadcasted_iota(jnp.int32, st.shape, 1)
        return jnp.where(r - c <= limit, st, MASK_VALUE * LOG2_E)

    for g0 in range(0, n_heads, hb):
        heads = range(g0, g0 + hb)
        qs = [slice(h * HEAD_PAD, (h + 1) * HEAD_PAD) for h in heads]
        vs = [slice(h * MLA_V, (h + 1) * MLA_V) for h in heads]
        os_ = slice(g0 * MLA_V, (g0 + hb) * MLA_V)
        k0 = [k_ref[0, :, c] for c in qs]

        outs = []
        for n in range(hb):
            st = causal(_dot_nt(k0[n], q_ref[0, :, qs[n]]), 0)
            p = jnp.exp2(st - jnp.max(st, axis=0, keepdims=True))
            outs.append(_dot(v0t[vs[n], :], p.astype(BF16)) / jnp.sum(p, axis=0, keepdims=True))
        o_ref[0, :, os_] = jnp.concatenate(outs, axis=0).T.astype(BF16)

        def q_block(i, carry):
            blk0 = 1 + i * bpq
            qt = [_load_rows(q_ref, blk0, bpq, c).astype(F32).T.astype(BF16) for c in qs]
            s_meta = [_dot(k0[n], qt[n]) for n in range(hb)]

            def scores(j, n):
                return _dot(_load_rows(k_ref, 1 + j * bpq, bpq, qs[n]), qt[n])

            state = []
            for n in range(hb):
                state += [jnp.full((1, tq), -1e30, F32), jnp.zeros((1, tq), F32), jnp.zeros((MLA_V, tq), F32)]
            s_first = tuple(scores(0, n) for n in range(hb))

            def kv_block(j, carry):
                st, s_cur = carry
                s_next = tuple(scores(j + 1, n) for n in range(hb))
                out = []
                for n in range(hb):
                    out += col_softmax_update(*st[3 * n:3 * n + 3], s_cur[n], vt_ref[j, vs[n], :])
                return tuple(out), s_next

            state, s_diag = lax.fori_loop(0, i, kv_block, (tuple(state), s_first))
            outs = []
            for n in range(hb):
                st3 = col_softmax_update(*state[3 * n:3 * n + 3], causal(s_diag[n], 0), vt_ref[i, vs[n], :])
                m, l, acc = col_softmax_update(*st3, s_meta[n], v0t[vs[n], :])
                outs.append(acc / l)
            o = jnp.concatenate(outs, axis=0).T.astype(BF16)
            _store_rows(o_ref, blk0, bpq, os_, o)
            return carry

        lax.fori_loop(0, nq, q_block, 0)


def _attention(q, k, v, nb, nblk):
    hps = ATT_HEADS_PER_STEP
    n_hp = MLA_HEADS // hps
    qw = hps * HEAD_PAD
    vw = hps * MLA_V
    s = (nblk - 1) * T_BLK
    tq = min(ATT_TQ, s)
    nq = s // tq
    col = lambda b, p: p
    o = pl.pallas_call(
        functools.partial(_attn_kernel, n_heads=hps, hb=ATT_HEADS_PER_BODY, tq=tq, nq=nq),
        out_shape=jax.ShapeDtypeStruct((nblk, nb, T_BLK, MLA_HEADS * MLA_V), BF16),
        grid=(nb, n_hp),
        in_specs=[_seq_spec(nblk, qw, col), _seq_spec(nblk, qw, col), _seq_spec(nblk, vw, col)],
        out_specs=_seq_spec(nblk, vw, col),
        scratch_shapes=[pltpu.VMEM((nq, vw, tq), BF16)],
        compiler_params=_cparams(2),
        name="mla_attn",
    )(q.reshape(nblk, nb, T_BLK, -1), k.reshape(nblk, nb, T_BLK, -1), v.reshape(nblk, nb, T_BLK, -1))
    return o.reshape(nblk * nb * T_BLK, MLA_HEADS * MLA_V)


def _time_major_perm(nb):
    p = np.zeros((T_BLK * nb, T_BLK * nb), np.float32)
    for b in range(nb):
        for t in range(T_BLK):
            p[t * nb + b, b * T_BLK + t] = 1.0
    return p


def _block_row_bcast(x, m, row):
    t, c = x.shape
    if m == t:
        return jnp.broadcast_to(x[row:row + 1, :], x.shape)
    x3 = x.reshape(t // m, m, c)
    return jnp.broadcast_to(x3[:, row:row + 1, :], x3.shape).reshape(t, c)


def _hgrn_chunk(blk0, n_blk, q_ref, f_ref, v_ref, g_ref, lb_ref, on_ref, tri_ref, lvl_ref, o_ref, st_ref):
    t = n_blk * T_BLK
    cs = slice(0, HG_QK)
    heads = [slice(h * HG_KEY, (h + 1) * HG_KEY) for h in range(HG_HEADS)]
    lb = lb_ref[...]
    q = _load_rows(q_ref, blk0, n_blk, cs).astype(F32)
    zf = _load_rows(f_ref, blk0, n_blk, cs).astype(F32)
    v = _load_rows(v_ref, blk0, n_blk, cs)
    g = _load_rows(g_ref, blk0, n_blk, cs).astype(F32)
    e = jnp.exp(-jnp.abs(zf))
    rcp = 1.0 / (1.0 + e)
    pos = zf >= 0.0
    sig_p = jnp.where(pos, rcp, e * rcp)
    sig_n = jnp.where(pos, e * rcp, rcp)
    f = lb + (1.0 - lb) * sig_p
    log_f = jnp.log(jnp.maximum(f, HG_F_MIN))
    k = (1.0 - lb) * sig_n
    hi = log_f.astype(BF16)
    lo = (log_f - hi.astype(F32)).astype(BF16)
    tri = tri_ref[0:t, 0:t]
    cum = _dot(tri, hi) + _dot(tri, lo)
    lvl = lvl_ref[0:t, 0:t]
    r_idx = lax.broadcasted_iota(jnp.int32, (t, HG_QK), 0)
    c8 = _block_row_bcast(cum, HG_BOTTOM, HG_BOTTOM // 2 - 1)
    qe = (q * jnp.exp(cum - c8)).astype(BF16)
    ke = (k * jnp.exp(c8 - cum)).astype(BF16)
    n_lvl = len(HG_LEVELS)
    scores = [jnp.where(lvl == n_lvl, _dot_nt(qe[:, hs], ke[:, hs]), 0.0) for hs in heads]
    for li, m in enumerate(HG_LEVELS):
        if m > t:
            continue
        half = m // 2
        cmid = _block_row_bcast(cum, m, half - 1)
        upper = (r_idx & (m - 1)) >= half
        ex = jnp.exp(jnp.where(upper, cum - cmid, cmid - cum))
        qe = jnp.where(upper, q * ex, 0.0).astype(BF16)
        ke = jnp.where(upper, 0.0, k * ex).astype(BF16)
        scores = [s + jnp.where(lvl == li, _dot_nt(qe[:, hs], ke[:, hs]), 0.0) for s, hs in zip(scores, heads)]
    qd = (q * jnp.exp(cum)).astype(BF16)
    last = cum[t - 1:t, :]
    kd = (k * jnp.exp(last - cum)).astype(BF16)
    dec = jnp.exp(last)
    outs = []
    for h, hs in enumerate(heads):
        st = st_ref[h]
        o = _dot(scores[h].astype(BF16), v[:, hs]) + _dot_nt(qd[:, hs], st.astype(BF16))
        st_ref[h] = st * dec[:, hs] + _dot_tn(v[:, hs], kd[:, hs])
        outs.append(o * lax.rsqrt(jnp.mean(o * o, axis=-1, keepdims=True) + 1e-6))
    o = jnp.concatenate(outs, axis=1) * on_ref[...]
    _store_rows(o_ref, blk0, n_blk, cs, (o * (g * _sigmoid(g))).astype(BF16))


def _hgrn_kernel(q_ref, f_ref, v_ref, g_ref, lb_ref, on_ref, tri_ref, lvl_ref, o_ref, st_ref, *, n_chunks):
    bpc = HG_CHUNK // T_BLK
    chunk = functools.partial(_hgrn_chunk, q_ref=q_ref, f_ref=f_ref, v_ref=v_ref, g_ref=g_ref, lb_ref=lb_ref,
                              on_ref=on_ref, tri_ref=tri_ref, lvl_ref=lvl_ref, o_ref=o_ref, st_ref=st_ref)
    st_ref[...] = jnp.zeros_like(st_ref)
    chunk(0, 1)

    def body(i, carry):
        chunk(1 + i * bpc, bpc)
        return carry

    lax.fori_loop(0, n_chunks, body, 0)


def _hgrn(zhg, lb, onorm, tri, lvl, nb, nblk):
    z4 = zhg.reshape(nblk, nb, T_BLK, ZHG_W)
    col = lambda j: _seq_spec(nblk, HG_QK, lambda b: j)
    o = pl.pallas_call(
        functools.partial(_hgrn_kernel, n_chunks=(nblk - 1) * T_BLK // HG_CHUNK),
        out_shape=jax.ShapeDtypeStruct((nblk, nb, T_BLK, HG_VW), BF16),
        grid=(nb,),
        in_specs=[col(0), col(1), col(2), col(3),
                  _const_spec((1, HG_QK)), _const_spec((1, HG_VW)),
                  _const_spec((HG_CHUNK, HG_CHUNK)), _const_spec((HG_CHUNK, HG_CHUNK))],
        out_specs=_seq_spec(nblk, HG_VW, lambda b: 0),
        scratch_shapes=[pltpu.VMEM((HG_HEADS, HG_VAL, HG_KEY), F32)],
        compiler_params=_cparams(1),
        name="hgrn2",
    )(z4, z4, z4, z4, lb, onorm, tri, lvl)
    return o.reshape(nblk * nb * T_BLK, HG_VW)


def _hgrn_level_matrix():
    r = np.arange(HG_CHUNK)[:, None]
    c = np.arange(HG_CHUNK)[None, :]
    lvl = np.zeros((HG_CHUNK, HG_CHUNK), np.int32)
    for li, m in enumerate(HG_LEVELS):
        lvl = np.where(r // m == c // m, li, lvl)
    lvl = np.where(r // HG_BOTTOM == c // HG_BOTTOM, len(HG_LEVELS), lvl)
    return np.where(c <= r, lvl, -1).astype(np.int32)


def _back_kernel(om_ref, os_ref, oh_ref, gm_ref, gs_ref, gh_ref, h_ref,
                 wm_ref, ws_ref, wh_ref, wo_ref, g1_ref, b1_ref,
                 wg_ref, wu_ref, wd_ref, g2_ref, b2_ref, o_ref, *, alpha):
    mixed = _sigmoid(gm_ref[...].astype(F32)) * _dot(om_ref[...], wm_ref[...])
    mixed += _sigmoid(gs_ref[...].astype(F32)) * _dot(os_ref[...], ws_ref[...])
    mixed += _sigmoid(gh_ref[...].astype(F32)) * _dot(oh_ref[...], wh_ref[...])
    r = alpha * h_ref[...] + _dot(mixed.astype(BF16), wo_ref[...])
    h1 = _layer_norm(r, g1_ref[...], b1_ref[...])
    hb = h1.astype(BF16)
    a = _dot(hb, wg_ref[...])
    u = _dot(hb, wu_ref[...])
    r2 = alpha * h1 + _dot((a * _sigmoid(a) * u).astype(BF16), wd_ref[...])
    y = _layer_norm(r2, g2_ref[...], b2_ref[...])
    o_ref[...] = y.reshape(o_ref.shape)


def _back(om, os_, oh, zg, h, wm, ws, wh, wo, g1, b1, wg, wu, wd, g2, b2, nb, alpha, final):
    r, d = h.shape
    bw = om.shape[1]
    dff = wg.shape[1]
    rb = T_BLK * nb
    skip = 1 if final else 0
    n_blk = r // rb - skip
    row = lambda width, j=0: pl.BlockSpec((rb, width), lambda i: (i + skip, j))
    if final:
        out_shape = jax.ShapeDtypeStruct((nb, n_blk * T_BLK, d), F32)
        out_spec = pl.BlockSpec((nb, T_BLK, d), lambda i: (0, i, 0))
    else:
        out_shape = jax.ShapeDtypeStruct((r, d), F32)
        out_spec = pl.BlockSpec((rb, d), lambda i: (i, 0))
    return pl.pallas_call(
        functools.partial(_back_kernel, alpha=alpha),
        out_shape=out_shape,
        grid=(n_blk,),
        in_specs=[row(bw), row(bw), row(bw), row(d, 0), row(d, 1), row(d, 2), row(d),
                  _const_spec((bw, d)), _const_spec((bw, d)), _const_spec((bw, d)), _const_spec((d, d)),
                  _const_spec((1, d)), _const_spec((1, d)),
                  _const_spec((d, dff)), _const_spec((d, dff)), _const_spec((dff, d)),
                  _const_spec((1, d)), _const_spec((1, d))],
        out_specs=out_spec,
        compiler_params=_cparams(1),
        name="merge_ffn",
    )(om, os_, oh, zg, zg, zg, h, wm, ws, wh, wo, g1, b1, wg, wu, wd, g2, b2)


def _permute_w_in(w):
    mla_in = MLA_Q_RANK + MLA_KV_RANK + MLA_ROPE
    s5_0, hg_0 = mla_in, mla_in + S5_WIDTH
    g_0 = hg_0 + 2 * HG_QK + 2 * HG_VW
    pad = jnp.zeros((w.shape[0], ZKR_W - MLA_ROPE), w.dtype)
    return jnp.concatenate([w[:, s5_0:hg_0], w[:, 0:MLA_Q_RANK + MLA_KV_RANK], w[:, hg_0:g_0], w[:, g_0:],
                            w[:, MLA_Q_RANK + MLA_KV_RANK:mla_in], pad], axis=1).astype(BF16)


def _rot_half(x):
    x1, x2 = jnp.split(x, 2, axis=-1)
    return jnp.concatenate([-x2, x1], axis=-1)


def _mla_weights(w_uq, w_ukv):
    rq, rkv = w_uq.shape[0], w_ukv.shape[0]
    zpad = HEAD_PAD - MLA_NOPE - MLA_ROPE
    wq = w_uq.reshape(rq, MLA_HEADS, MLA_NOPE + MLA_ROPE)
    q_nope, q_rope = wq[..., :MLA_NOPE], wq[..., MLA_NOPE:]
    zq = jnp.zeros((rq, MLA_HEADS, zpad), w_uq.dtype)
    wq_p = jnp.concatenate([q_nope, q_rope, zq], axis=-1).reshape(rq, -1)
    wq_r = jnp.concatenate([jnp.zeros_like(q_nope), _rot_half(q_rope), zq], axis=-1).reshape(rq, -1)
    wkv = w_ukv.reshape(rkv, MLA_HEADS, MLA_NOPE + MLA_V)
    zk = jnp.zeros((rkv, MLA_HEADS, HEAD_PAD - MLA_NOPE), w_ukv.dtype)
    wk_p = jnp.concatenate([wkv[..., :MLA_NOPE], zk], axis=-1).reshape(rkv, -1)
    wv = wkv[..., MLA_NOPE:].reshape(rkv, -1)
    eye = jnp.eye(ZKR_W, MLA_ROPE, dtype=F32)
    place = jnp.concatenate([jnp.zeros((ZKR_W, MLA_NOPE), F32), eye, jnp.zeros((ZKR_W, zpad), F32)], axis=-1)
    place_r = jnp.concatenate([jnp.zeros((ZKR_W, MLA_NOPE), F32), _rot_half(eye),
                               jnp.zeros((ZKR_W, zpad), F32)], axis=-1)
    e = jnp.tile(place, (1, MLA_HEADS))
    er = jnp.tile(place_r, (1, MLA_HEADS))
    return [a.astype(BF16) for a in (wq_p, wq_r, wk_p, wv, e, er)]


def _rope_inv():
    inv = ROPE_THETA ** (-(jnp.arange(0, MLA_ROPE, 2, dtype=F32) / MLA_ROPE))
    z = jnp.zeros((HEAD_PAD - MLA_NOPE - MLA_ROPE,), F32)
    return jnp.concatenate([jnp.zeros((MLA_NOPE,), F32), inv, inv, z])[None, :]


def _s5_params(lam_re, lam_im, log_dt, b_re, b_im, c_re, c_im):
    lr = jnp.minimum(lam_re.astype(F32), -1e-4)
    li = lam_im.astype(F32)
    dt = jnp.exp(log_dt.astype(F32))[:, None]
    mag = jnp.exp(lr * dt)
    ab_r = mag * jnp.cos(li * dt)
    ab_i = mag * jnp.sin(li * dt)
    den = lr * lr + li * li
    nr = ab_r - 1.0
    coef_r = ((nr * lr + ab_i * li) / den)[..., None]
    coef_i = ((ab_i * lr - nr * li) / den)[..., None]
    bb_r = coef_r * b_re.astype(F32) - coef_i * b_im.astype(F32)
    bb_i = coef_r * b_im.astype(F32) + coef_i * b_re.astype(F32)
    n_slab = S5_WIDTH // S5_SLAB
    gps = S5_SLAB // S5_GROUP
    eye = jnp.eye(gps, dtype=F32)

    def in_mat(bb):
        b4 = bb.reshape(n_slab, gps, S5_STATE, S5_GROUP)
        return jnp.einsum('jgnc,gh->jgchn', b4, eye).reshape(n_slab, S5_SLAB, S5_SLAB_STATE)

    def out_mat(cc):
        c4 = cc.astype(F32).reshape(n_slab, gps, S5_GROUP, S5_STATE)
        return jnp.einsum('jgcn,gh->jgnhc', c4, eye).reshape(n_slab, S5_SLAB_STATE, S5_SLAB)

    bm = jnp.concatenate([in_mat(bb_r), in_mat(bb_i)], axis=2).astype(BF16)
    cm = jnp.concatenate([out_mat(c_re), -out_mat(c_im)], axis=1).astype(BF16)
    ar = ab_r.reshape(n_slab, 1, S5_SLAB_STATE)
    ai = ab_i.reshape(n_slab, 1, S5_SLAB_STATE)
    return bm, cm, ar, ai


def kernel(x, positions, meta_tokens, ln_in_g, ln_in_b, w_in, mla_q_norm, mla_w_uq, mla_kv_norm, mla_w_ukv,
           s5_lam_re, s5_lam_im, s5_log_dt, s5_b_re, s5_b_im, s5_c_re, s5_c_im, s5_d, s5_w_glu,
           hg_lb_logits, hg_out_norm, w_br_mla, w_br_s5, w_br_hg, w_out, ln1_g, ln1_b,
           w_ffn_gate, w_ffn_up, w_ffn_down, ln2_g, ln2_b):
    nb, s, d = x.shape
    depth = w_in.shape[0]
    nblk = (s + N_META) // T_BLK
    rb = T_BLK * nb
    n_rows = nblk * rb
    alpha = (2 * depth) ** 0.25
    row2 = lambda a: a.astype(F32)[None, :]

    h = _ln_in(x, meta_tokens.astype(x.dtype), row2(ln_in_g), row2(ln_in_b)).reshape(n_rows, d)

    meta_pos = jnp.broadcast_to(jnp.arange(N_META, dtype=jnp.int32)[None, :], (nb, N_META))
    pos = jnp.concatenate([meta_pos, positions.astype(jnp.int32) + N_META], axis=1)
    pos = pos.reshape(nb, nblk, T_BLK).transpose(1, 0, 2).reshape(n_rows, 1)
    inv128 = _rope_inv()
    p_lb = jax.nn.softmax(hg_lb_logits.astype(F32), axis=0)
    lower_bounds = jnp.cumsum(p_lb, axis=0) - p_lb[0]
    tri = jnp.asarray(np.tril(np.ones((HG_CHUNK, HG_CHUNK), np.float32)), BF16)
    lvl = jnp.asarray(_hgrn_level_matrix())
    perm = jnp.asarray(_time_major_perm(nb), BF16)

    for li in range(depth):
        s5_mats = _s5_params(s5_lam_re[li], s5_lam_im[li], s5_log_dt[li], s5_b_re[li], s5_b_im[li],
                             s5_c_re[li], s5_c_im[li])
        zhg, zg, q, k, v, y_s5 = _front(
            h, _permute_w_in(w_in[li]), pos, inv128, row2(mla_q_norm[li]), row2(mla_kv_norm[li]),
            *_mla_weights(mla_w_uq[li], mla_w_ukv[li]), perm, *s5_mats, row2(s5_d[li]),
            s5_w_glu[li].astype(BF16), nb)
        o_mla = _attention(q, k, v, nb, nblk)
        o_hg = _hgrn(zhg, lower_bounds[li][None, :], row2(hg_out_norm[li]), tri, lvl, nb, nblk)
        h = _back(o_mla, y_s5, o_hg, zg, h, w_br_mla[li].astype(BF16), w_br_s5[li].astype(BF16),
                  w_br_hg[li].astype(BF16), w_out[li].astype(BF16), row2(ln1_g[li]), row2(ln1_b[li]),
                  w_ffn_gate[li].astype(BF16), w_ffn_up[li].astype(BF16), w_ffn_down[li].astype(BF16),
                  row2(ln2_g[li]), row2(ln2_b[li]), nb, alpha, final=li == depth - 1)
    return h
```

```python
import functools
import math

import jax
import jax.numpy as jnp
import numpy as np
from jax import lax
from jax.experimental import pallas as pl
from jax.experimental.pallas import tpu as pltpu

F32 = jnp.float32
BF16 = jnp.bfloat16

N_META = 16
MLA_HEADS = 8
MLA_NOPE = 64
MLA_ROPE = 32
MLA_V = 64
MLA_Q_RANK = 256
MLA_KV_RANK = 256
ROPE_THETA = 10000.0
MASK_VALUE = -1e9
LOG2_E = math.log2(math.e)
HEAD_PAD = 128
S5_WIDTH = 512
S5_GROUP = 16
S5_GROUPS = S5_WIDTH // S5_GROUP
S5_STATE = 64
S5_SLAB = 128
S5_SLAB_STATE = (S5_SLAB // S5_GROUP) * S5_STATE
HG_HEADS = 4
HG_KEY = 128
HG_VAL = 128
HG_QK = HG_HEADS * HG_KEY
HG_VW = HG_HEADS * HG_VAL
HG_F_MIN = 1e-6
HG_CHUNK = 128
HG_LEVELS = (128, 64, 32, 16)
HG_BOTTOM = 8
N_BRANCH = 3
T_BLK = 16
ATT_TQ = 256
ATT_HEADS_PER_STEP = 4
ATT_HEADS_PER_BODY = 4
VMEM_LIMIT = 56 * 1024 * 1024

ZA_W = 1024
ZHG_W = 2048
ZG_W = 3072
ZKR_W = 128
Z_W = ZA_W + ZHG_W + ZG_W + ZKR_W
WIDE_CHUNK = 512
FILL_BEFORE_MLA = 3


def _cparams(n_grid, sem="parallel"):
    return pltpu.CompilerParams(dimension_semantics=(sem,) * n_grid, vmem_limit_bytes=VMEM_LIMIT)


def _const_spec(shape):
    nd = len(shape)
    return pl.BlockSpec(shape, lambda *_: (0,) * nd, pipeline_mode=pl.Buffered(1))


def _seq_spec(nblk, width, col):
    return pl.BlockSpec((nblk, None, T_BLK, width), lambda b, *g: (0, b, 0, col(b, *g)))


def _dot(a, b):
    return jnp.dot(a, b, preferred_element_type=F32)


def _dot_nt(a, b):
    return lax.dot_general(a, b, (((1,), (1,)), ((), ())), preferred_element_type=F32)


def _dot_tn(a, b):
    return lax.dot_general(a, b, (((0,), (0,)), ((), ())), preferred_element_type=F32)


def _sigmoid(x):
    return 1.0 / (1.0 + jnp.exp(-x))


def _layer_norm(x, g, b, eps=1e-5):
    mu = jnp.mean(x, axis=-1, keepdims=True)
    xc = x - mu
    var = jnp.mean(xc * xc, axis=-1, keepdims=True)
    return xc * lax.rsqrt(var + eps) * g + b


def _rms_norm(x, g, eps=1e-6):
    return x * lax.rsqrt(jnp.mean(x * x, axis=-1, keepdims=True) + eps) * g


def _load_rows(ref, blk0, n_blk, cs):
    x = ref[pl.ds(blk0, n_blk), :, cs]
    return x.reshape(n_blk * T_BLK, x.shape[-1])


def _store_rows(ref, blk0, n_blk, cs, x):
    ref[pl.ds(blk0, n_blk), :, cs] = x.reshape(n_blk, T_BLK, x.shape[-1])


def _ln_in_kernel(x_ref, meta_ref, g_ref, b_ref, o_ref, *, nb):
    i = pl.program_id(0)

    @pl.when(i == 0)
    def _():
        y = _layer_norm(meta_ref[...], g_ref[...], b_ref[...])
        for b in range(nb):
            o_ref[0, b] = y

    @pl.when(i > 0)
    def _():
        o_ref[0] = _layer_norm(x_ref[...], g_ref[...], b_ref[...])


def _ln_in(x, meta, g, b):
    nb, s, d = x.shape
    nblk = (s + N_META) // T_BLK
    return pl.pallas_call(
        functools.partial(_ln_in_kernel, nb=nb),
        out_shape=jax.ShapeDtypeStruct((nblk, nb, T_BLK, d), F32),
        grid=(nblk,),
        in_specs=[
            pl.BlockSpec((nb, T_BLK, d), lambda i: (0, jnp.maximum(i - 1, 0), 0)),
            _const_spec((N_META, d)),
            _const_spec((1, d)),
            _const_spec((1, d)),
        ],
        out_specs=pl.BlockSpec((1, nb, T_BLK, d), lambda i: (i, 0, 0, 0)),
        compiler_params=_cparams(1),
        name="ln_in",
    )(x, meta, g, b)


def _rope_tables(pos_ref, inv_ref):
    ang = pos_ref[...].astype(F32) * inv_ref[...]
    return jnp.cos(ang), jnp.sin(ang)


def _mla_prep(cq, ckv, kr, cos, sin, qn_ref, kvn_ref, wq_ref, wqr_ref, wk_ref, wv_ref, e_ref, er_ref,
              q_ref, k_ref, v_ref):
    scale = (MLA_NOPE + MLA_ROPE) ** -0.5 * LOG2_E
    cqn = _rms_norm(cq, qn_ref[...]).astype(BF16)
    ckvn = _rms_norm(ckv, kvn_ref[...]).astype(BF16)
    kr = kr.astype(BF16)
    cos_q = cos * scale
    sin_q = sin * scale
    for h in range(MLA_HEADS):
        cs = slice(h * HEAD_PAD, (h + 1) * HEAD_PAD)
        qf = _dot(cqn, wq_ref[:, cs])
        qr = _dot(cqn, wqr_ref[:, cs])
        q_ref[:, cs] = (qf * cos_q + qr * sin_q).astype(BF16)
        kf = _dot(ckvn, wk_ref[:, cs]) + _dot(kr, e_ref[:, cs])
        krot = _dot(kr, er_ref[:, cs])
        k_ref[:, cs] = (kf * cos + krot * sin).astype(BF16)
    v_ref[...] = _dot(ckvn, wv_ref[...]).astype(BF16)


def _s5_block(u, perm_ref, bm_ref, cm_ref, ar_ref, ai_ref, d_ref, wg_ref, o_ref,
              xr_ref, xi_ref, buf_ref, y_ref, nb, fillers):
    n_slab = S5_WIDTH // S5_SLAB
    ns = S5_SLAB_STATE
    u = _dot(perm_ref[...], u).astype(BF16)
    for j in range(n_slab):
        buf_ref[...] = _dot(u[:, j * S5_SLAB:(j + 1) * S5_SLAB], bm_ref[j])
        if fillers:
            fillers.pop(0)()
        ar = jnp.broadcast_to(ar_ref[j], (nb, ns))
        ai = jnp.broadcast_to(ai_ref[j], (nb, ns))
        xr = xr_ref[j]
        xi = xi_ref[j]
        for t in range(T_BLK):
            rows = slice(t * nb, (t + 1) * nb)
            nr = ar * xr - ai * xi + buf_ref[rows, 0:ns]
            ni = ar * xi + ai * xr + buf_ref[rows, ns:2 * ns]
            buf_ref[rows, 0:ns] = nr
            buf_ref[rows, ns:2 * ns] = ni
            xr, xi = nr, ni
        xr_ref[j] = xr
        xi_ref[j] = xi
        y_ref[:, j * S5_SLAB:(j + 1) * S5_SLAB] = _dot(buf_ref[...].astype(BF16), cm_ref[j])

    while fillers:
        fillers.pop(0)()
    y = y_ref[...] + d_ref[...] * u.astype(F32)
    y = 0.5 * y * (1.0 + jnp.tanh(math.sqrt(2.0 / math.pi) * (y + 0.044715 * (y * y * y))))
    gate = _sigmoid(_dot(y.astype(BF16), wg_ref[...]))
    out = (y * gate).astype(BF16)
    o_ref[...] = _dot_tn(perm_ref[...], out).astype(BF16)


def _front_kernel(h_ref, w_ref, pos_ref, inv_ref, qn_ref, kvn_ref, wq_ref, wqr_ref, wk_ref, wv_ref, e_ref, er_ref,
                  perm_ref, bm_ref, cm_ref, ar_ref, ai_ref, d_ref, wglu_ref,
                  zhg_ref, zg_ref, q_ref, k_ref, v_ref, ys5_ref,
                  xr_ref, xi_ref, buf_ref, y_ref, *, nb):
    @pl.when(pl.program_id(0) == 0)
    def _():
        xr_ref[...] = jnp.zeros_like(xr_ref)
        xi_ref[...] = jnp.zeros_like(xi_ref)

    cos, sin = _rope_tables(pos_ref, inv_ref)
    x = h_ref[...].astype(BF16)
    za = _dot(x, w_ref[:, 0:ZA_W])
    zkr = _dot(x, w_ref[:, ZA_W + ZHG_W + ZG_W:Z_W])

    def wide_chunk(o_ref, c, off):
        def run():
            o_ref[:, c:c + WIDE_CHUNK] = _dot(x, w_ref[:, off + c:off + c + WIDE_CHUNK]).astype(BF16)
        return run

    fillers = [wide_chunk(zhg_ref, c, ZA_W) for c in range(0, ZHG_W, WIDE_CHUNK)]
    fillers += [wide_chunk(zg_ref, c, ZA_W + ZHG_W) for c in range(0, ZG_W, WIDE_CHUNK)]
    for _ in range(FILL_BEFORE_MLA):
        fillers.pop(0)()
    c0 = S5_WIDTH
    _mla_prep(za[:, c0:c0 + MLA_Q_RANK], za[:, c0 + MLA_Q_RANK:c0 + MLA_Q_RANK + MLA_KV_RANK], zkr,
              cos, sin, qn_ref, kvn_ref, wq_ref, wqr_ref, wk_ref, wv_ref, e_ref, er_ref,
              q_ref, k_ref, v_ref)
    _s5_block(za[:, 0:S5_WIDTH].astype(BF16), perm_ref, bm_ref, cm_ref, ar_ref, ai_ref, d_ref, wglu_ref, ys5_ref,
              xr_ref, xi_ref, buf_ref, y_ref, nb, fillers)


def _front(h, w, pos, inv128, qn, kvn, wq, wqr, wk, wv, e, er, perm, bm, cm, ar, ai, dskip, wglu, nb):
    r, d = h.shape
    rb = T_BLK * nb
    hw = MLA_HEADS * HEAD_PAD
    vw = MLA_HEADS * MLA_V
    n_slab = S5_WIDTH // S5_SLAB
    row = lambda width: pl.BlockSpec((rb, width), lambda i: (i, 0))
    widths = (ZHG_W, ZG_W, hw, hw, vw, S5_WIDTH)
    return pl.pallas_call(
        functools.partial(_front_kernel, nb=nb),
        out_shape=[jax.ShapeDtypeStruct((r, wd), BF16) for wd in widths],
        grid=(r // rb,),
        in_specs=[
            row(d), _const_spec((d, Z_W)), row(1),
            _const_spec((1, HEAD_PAD)), _const_spec((1, MLA_Q_RANK)), _const_spec((1, MLA_KV_RANK)),
            _const_spec((MLA_Q_RANK, hw)), _const_spec((MLA_Q_RANK, hw)), _const_spec((MLA_KV_RANK, hw)),
            _const_spec((MLA_KV_RANK, vw)), _const_spec((ZKR_W, hw)), _const_spec((ZKR_W, hw)),
            _const_spec((rb, rb)),
            _const_spec((n_slab, S5_SLAB, 2 * S5_SLAB_STATE)),
            _const_spec((n_slab, 2 * S5_SLAB_STATE, S5_SLAB)),
            _const_spec((n_slab, 1, S5_SLAB_STATE)),
            _const_spec((n_slab, 1, S5_SLAB_STATE)),
            _const_spec((1, S5_WIDTH)),
            _const_spec((S5_WIDTH, S5_WIDTH)),
        ],
        out_specs=[row(wd) for wd in widths],
        scratch_shapes=[
            pltpu.VMEM((n_slab, nb, S5_SLAB_STATE), F32),
            pltpu.VMEM((n_slab, nb, S5_SLAB_STATE), F32),
            pltpu.VMEM((rb, 2 * S5_SLAB_STATE), F32),
            pltpu.VMEM((rb, S5_WIDTH), F32),
        ],
        compiler_params=_cparams(1, "arbitrary"),
        name="front",
    )(h, w, pos, inv128, qn, kvn, wq, wqr, wk, wv, e, er, perm, bm, cm, ar, ai, dskip, wglu)


def _attn_kernel(q_ref, k_ref, v_ref, o_ref, vt_ref, *, n_heads, hb, tq, nq):
    bpq = tq // T_BLK
    vw = n_heads * MLA_V

    for j in range(nq):
        vb = _load_rows(v_ref, 1 + j * bpq, bpq, slice(0, vw))
        vt_ref[j] = vb.astype(F32).T.astype(BF16)
    v0t = v_ref[0].astype(F32).T.astype(BF16)

    def col_softmax_update(m, l, acc, st, vt):
        mn = jnp.maximum(m, jnp.max(st, axis=0, keepdims=True))
        a = jnp.exp2(m - mn)
        p = jnp.exp2(st - mn)
        l = a * l + jnp.sum(p, axis=0, keepdims=True)
        acc = a * acc + _dot(vt, p.astype(BF16))
        return mn, l, acc

    def causal(st, limit):
        r = lax.broadcasted_iota(jnp.int32, st.shape, 0)
        c = lax.broadcasted_iota(jnp.int32, st.shape, 1)
        return jnp.where(r - c <= limit, st, MASK_VALUE * LOG2_E)

    for g0 in range(0, n_heads, hb):
        heads = range(g0, g0 + hb)
        qs = [slice(h * HEAD_PAD, (h + 1) * HEAD_PAD) for h in heads]
        vs = [slice(h * MLA_V, (h + 1) * MLA_V) for h in heads]
        os_ = slice(g0 * MLA_V, (g0 + hb) * MLA_V)
        k0 = [k_ref[0, :, c] for c in qs]

        outs = []
        for n in range(hb):
            st = causal(_dot_nt(k0[n], q_ref[0, :, qs[n]]), 0)
            p = jnp.exp2(st - jnp.max(st, axis=0, keepdims=True))
            outs.append(_dot(v0t[vs[n], :], p.astype(BF16)) / jnp.sum(p, axis=0, keepdims=True))
        o_ref[0, :, os_] = jnp.concatenate(outs, axis=0).T.astype(BF16)

        def q_block(i, carry):
            blk0 = 1 + i * bpq
            qt = [_load_rows(q_ref, blk0, bpq, c).astype(F32).T.astype(BF16) for c in qs]
            s_meta = [_dot(k0[n], qt[n]) for n in range(hb)]

            def scores(j, n):
                return _dot(_load_rows(k_ref, 1 + j * bpq, bpq, qs[n]), qt[n])

            state = []
            for n in range(hb):
                state += [jnp.full((1, tq), -1e30, F32), jnp.zeros((1, tq), F32), jnp.zeros((MLA_V, tq), F32)]
            s_first = tuple(scores(0, n) for n in range(hb))

            def kv_block(j, carry):
                st, s_cur = carry
                s_next = tuple(scores(j + 1, n) for n in range(hb))
                out = []
                for n in range(hb):
                    out += col_softmax_update(*st[3 * n:3 * n + 3], s_cur[n], vt_ref[j, vs[n], :])
                return tuple(out), s_next

            state, s_diag = lax.fori_loop(0, i, kv_block, (tuple(state), s_first))
            outs = []
            for n in range(hb):
                st3 = col_softmax_update(*state[3 * n:3 * n + 3], causal(s_diag[n], 0), vt_ref[i, vs[n], :])
                m, l, acc = col_softmax_update(*st3, s_meta[n], v0t[vs[n], :])
                outs.append(acc / l)
            o = jnp.concatenate(outs, axis=0).T.astype(BF16)
            _store_rows(o_ref, blk0, bpq, os_, o)
            return carry

        lax.fori_loop(0, nq, q_block, 0)


def _attention(q, k, v, nb, nblk):
    hps = ATT_HEADS_PER_STEP
    n_hp = MLA_HEADS // hps
    qw = hps * HEAD_PAD
    vw = hps * MLA_V
    s = (nblk - 1) * T_BLK
    tq = min(ATT_TQ, s)
    nq = s // tq
    col = lambda b, p: p
    o = pl.pallas_call(
        functools.partial(_attn_kernel, n_heads=hps, hb=ATT_HEADS_PER_BODY, tq=tq, nq=nq),
        out_shape=jax.ShapeDtypeStruct((nblk, nb, T_BLK, MLA_HEADS * MLA_V), BF16),
        grid=(nb, n_hp),
        in_specs=[_seq_spec(nblk, qw, col), _seq_spec(nblk, qw, col), _seq_spec(nblk, vw, col)],
        out_specs=_seq_spec(nblk, vw, col),
        scratch_shapes=[pltpu.VMEM((nq, vw, tq), BF16)],
        compiler_params=_cparams(2),
        name="mla_attn",
    )(q.reshape(nblk, nb, T_BLK, -1), k.reshape(nblk, nb, T_BLK, -1), v.reshape(nblk, nb, T_BLK, -1))
    return o.reshape(nblk * nb * T_BLK, MLA_HEADS * MLA_V)


def _time_major_perm(nb):
    p = np.zeros((T_BLK * nb, T_BLK * nb), np.float32)
    for b in range(nb):
        for t in range(T_BLK):
            p[t * nb + b, b * T_BLK + t] = 1.0
    return p


def _block_row_bcast(x, m, row):
    t, c = x.shape
    if m == t:
        return jnp.broadcast_to(x[row:row + 1, :], x.shape)
    x3 = x.reshape(t // m, m, c)
    return jnp.broadcast_to(x3[:, row:row + 1, :], x3.shape).reshape(t, c)


def _hgrn_chunk(blk0, n_blk, q_ref, f_ref, v_ref, g_ref, lb_ref, on_ref, tri_ref, lvl_ref, o_ref, st_ref):
    t = n_blk * T_BLK
    cs = slice(0, HG_QK)
    heads = [slice(h * HG_KEY, (h + 1) * HG_KEY) for h in range(HG_HEADS)]
    lb = lb_ref[...]
    q = _load_rows(q_ref, blk0, n_blk, cs).astype(F32)
    zf = _load_rows(f_ref, blk0, n_blk, cs).astype(F32)
    v = _load_rows(v_ref, blk0, n_blk, cs)
    g = _load_rows(g_ref, blk0, n_blk, cs).astype(F32)
    e = jnp.exp(-jnp.abs(zf))
    rcp = 1.0 / (1.0 + e)
    pos = zf >= 0.0
    sig_p = jnp.where(pos, rcp, e * rcp)
    sig_n = jnp.where(pos, e * rcp, rcp)
    f = lb + (1.0 - lb) * sig_p
    log_f = jnp.log(jnp.maximum(f, HG_F_MIN))
    k = (1.0 - lb) * sig_n
    hi = log_f.astype(BF16)
    lo = (log_f - hi.astype(F32)).astype(BF16)
    tri = tri_ref[0:t, 0:t]
    cum = _dot(tri, hi) + _dot(tri, lo)
    lvl = lvl_ref[0:t, 0:t]
    r_idx = lax.broadcasted_iota(jnp.int32, (t, HG_QK), 0)
    c8 = _block_row_bcast(cum, HG_BOTTOM, HG_BOTTOM // 2 - 1)
    qe = (q * jnp.exp(cum - c8)).astype(BF16)
    ke = (k * jnp.exp(c8 - cum)).astype(BF16)
    n_lvl = len(HG_LEVELS)
    scores = [jnp.where(lvl == n_lvl, _dot_nt(qe[:, hs], ke[:, hs]), 0.0) for hs in heads]
    for li, m in enumerate(HG_LEVELS):
        if m > t:
            continue
        half = m // 2
        cmid = _block_row_bcast(cum, m, half - 1)
        upper = (r_idx & (m - 1)) >= half
        ex = jnp.exp(jnp.where(upper, cum - cmid, cmid - cum))
        qe = jnp.where(upper, q * ex, 0.0).astype(BF16)
        ke = jnp.where(upper, 0.0, k * ex).astype(BF16)
        scores = [s + jnp.where(lvl == li, _dot_nt(qe[:, hs], ke[:, hs]), 0.0) for s, hs in zip(scores, heads)]
    qd = (q * jnp.exp(cum)).astype(BF16)
    last = cum[t - 1:t, :]
    kd = (k * jnp.exp(last - cum)).astype(BF16)
    dec = jnp.exp(last)
    outs = []
    for h, hs in enumerate(heads):
        st = st_ref[h]
        o = _dot(scores[h].astype(BF16), v[:, hs]) + _dot_nt(qd[:, hs], st.astype(BF16))
        st_ref[h] = st * dec[:, hs] + _dot_tn(v[:, hs], kd[:, hs])
        outs.append(o * lax.rsqrt(jnp.mean(o * o, axis=-1, keepdims=True) + 1e-6))
    o = jnp.concatenate(outs, axis=1) * on_ref[...]
    _store_rows(o_ref, blk0, n_blk, cs, (o * (g * _sigmoid(g))).astype(BF16))


def _hgrn_kernel(q_ref, f_ref, v_ref, g_ref, lb_ref, on_ref, tri_ref, lvl_ref, o_ref, st_ref, *, n_chunks):
    bpc = HG_CHUNK // T_BLK
    chunk = functools.partial(_hgrn_chunk, q_ref=q_ref, f_ref=f_ref, v_ref=v_ref, g_ref=g_ref, lb_ref=lb_ref,
                              on_ref=on_ref, tri_ref=tri_ref, lvl_ref=lvl_ref, o_ref=o_ref, st_ref=st_ref)
    st_ref[...] = jnp.zeros_like(st_ref)
    chunk(0, 1)

    def body(i, carry):
        chunk(1 + i * bpc, bpc)
        return carry

    lax.fori_loop(0, n_chunks, body, 0)


def _hgrn(zhg, lb, onorm, tri, lvl, nb, nblk):
    z4 = zhg.reshape(nblk, nb, T_BLK, ZHG_W)
    col = lambda j: _seq_spec(nblk, HG_QK, lambda b: j)
    o = pl.pallas_call(
        functools.partial(_hgrn_kernel, n_chunks=(nblk - 1) * T_BLK // HG_CHUNK),
        out_shape=jax.ShapeDtypeStruct((nblk, nb, T_BLK, HG_VW), BF16),
        grid=(nb,),
        in_specs=[col(0), col(1), col(2), col(3),
                  _const_spec((1, HG_QK)), _const_spec((1, HG_VW)),
                  _const_spec((HG_CHUNK, HG_CHUNK)), _const_spec((HG_CHUNK, HG_CHUNK))],
        out_specs=_seq_spec(nblk, HG_VW, lambda b: 0),
        scratch_shapes=[pltpu.VMEM((HG_HEADS, HG_VAL, HG_KEY), F32)],
        compiler_params=_cparams(1),
        name="hgrn2",
    )(z4, z4, z4, z4, lb, onorm, tri, lvl)
    return o.reshape(nblk * nb * T_BLK, HG_VW)


def _hgrn_level_matrix():
    r = np.arange(HG_CHUNK)[:, None]
    c = np.arange(HG_CHUNK)[None, :]
    lvl = np.zeros((HG_CHUNK, HG_CHUNK), np.int32)
    for li, m in enumerate(HG_LEVELS):
        lvl = np.where(r // m == c // m, li, lvl)
    lvl = np.where(r // HG_BOTTOM == c // HG_BOTTOM, len(HG_LEVELS), lvl)
    return np.where(c <= r, lvl, -1).astype(np.int32)


def _back_kernel(om_ref, os_ref, oh_ref, gm_ref, gs_ref, gh_ref, h_ref,
                 wm_ref, ws_ref, wh_ref, wo_ref, g1_ref, b1_ref,
                 wg_ref, wu_ref, wd_ref, g2_ref, b2_ref, o_ref, *, alpha):
    mixed = _sigmoid(gm_ref[...].astype(F32)) * _dot(om_ref[...], wm_ref[...])
    mixed += _sigmoid(gs_ref[...].astype(F32)) * _dot(os_ref[...], ws_ref[...])
    mixed += _sigmoid(gh_ref[...].astype(F32)) * _dot(oh_ref[...], wh_ref[...])
    r = alpha * h_ref[...] + _dot(mixed.astype(BF16), wo_ref[...])
    h1 = _layer_norm(r, g1_ref[...], b1_ref[...])
    hb = h1.astype(BF16)
    a = _dot(hb, wg_ref[...])
    u = _dot(hb, wu_ref[...])
    r2 = alpha * h1 + _dot((a * _sigmoid(a) * u).astype(BF16), wd_ref[...])
    y = _layer_norm(r2, g2_ref[...], b2_ref[...])
    o_ref[...] = y.reshape(o_ref.shape)


def _back(om, os_, oh, zg, h, wm, ws, wh, wo, g1, b1, wg, wu, wd, g2, b2, nb, alpha, final):
    r, d = h.shape
    bw = om.shape[1]
    dff = wg.shape[1]
    rb = T_BLK * nb
    skip = 1 if final else 0
    n_blk = r // rb - skip
    row = lambda width, j=0: pl.BlockSpec((rb, width), lambda i: (i + skip, j))
    if final:
        out_shape = jax.ShapeDtypeStruct((nb, n_blk * T_BLK, d), F32)
        out_spec = pl.BlockSpec((nb, T_BLK, d), lambda i: (0, i, 0))
    else:
        out_shape = jax.ShapeDtypeStruct((r, d), F32)
        out_spec = pl.BlockSpec((rb, d), lambda i: (i, 0))
    return pl.pallas_call(
        functools.partial(_back_kernel, alpha=alpha),
        out_shape=out_shape,
        grid=(n_blk,),
        in_specs=[row(bw), row(bw), row(bw), row(d, 0), row(d, 1), row(d, 2), row(d),
                  _const_spec((bw, d)), _const_spec((bw, d)), _const_spec((bw, d)), _const_spec((d, d)),
                  _const_spec((1, d)), _const_spec((1, d)),
                  _const_spec((d, dff)), _const_spec((d, dff)), _const_spec((dff, d)),
                  _const_spec((1, d)), _const_spec((1, d))],
        out_specs=out_spec,
        compiler_params=_cparams(1),
        name="merge_ffn",
    )(om, os_, oh, zg, zg, zg, h, wm, ws, wh, wo, g1, b1, wg, wu, wd, g2, b2)


def _permute_w_in(w):
    mla_in = MLA_Q_RANK + MLA_KV_RANK + MLA_ROPE
    s5_0, hg_0 = mla_in, mla_in + S5_WIDTH
    g_0 = hg_0 + 2 * HG_QK + 2 * HG_VW
    pad = jnp.zeros((w.shape[0], ZKR_W - MLA_ROPE), w.dtype)
    return jnp.concatenate([w[:, s5_0:hg_0], w[:, 0:MLA_Q_RANK + MLA_KV_RANK], w[:, hg_0:g_0], w[:, g_0:],
                            w[:, MLA_Q_RANK + MLA_KV_RANK:mla_in], pad], axis=1).astype(BF16)


def _rot_half(x):
    x1, x2 = jnp.split(x, 2, axis=-1)
    return jnp.concatenate([-x2, x1], axis=-1)


def _mla_weights(w_uq, w_ukv):
    rq, rkv = w_uq.shape[0], w_ukv.shape[0]
    zpad = HEAD_PAD - MLA_NOPE - MLA_ROPE
    wq = w_uq.reshape(rq, MLA_HEADS, MLA_NOPE + MLA_ROPE)
    q_nope, q_rope = wq[..., :MLA_NOPE], wq[..., MLA_NOPE:]
    zq = jnp.zeros((rq, MLA_HEADS, zpad), w_uq.dtype)
    wq_p = jnp.concatenate([q_nope, q_rope, zq], axis=-1).reshape(rq, -1)
    wq_r = jnp.concatenate([jnp.zeros_like(q_nope), _rot_half(q_rope), zq], axis=-1).reshape(rq, -1)
    wkv = w_ukv.reshape(rkv, MLA_HEADS, MLA_NOPE + MLA_V)
    zk = jnp.zeros((rkv, MLA_HEADS, HEAD_PAD - MLA_NOPE), w_ukv.dtype)
    wk_p = jnp.concatenate([wkv[..., :MLA_NOPE], zk], axis=-1).reshape(rkv, -1)
    wv = wkv[..., MLA_NOPE:].reshape(rkv, -1)
    eye = jnp.eye(ZKR_W, MLA_ROPE, dtype=F32)
    place = jnp.concatenate([jnp.zeros((ZKR_W, MLA_NOPE), F32), eye, jnp.zeros((ZKR_W, zpad), F32)], axis=-1)
    place_r = jnp.concatenate([jnp.zeros((ZKR_W, MLA_NOPE), F32), _rot_half(eye),
                               jnp.zeros((ZKR_W, zpad), F32)], axis=-1)
    e = jnp.tile(place, (1, MLA_HEADS))
    er = jnp.tile(place_r, (1, MLA_HEADS))
    return [a.astype(BF16) for a in (wq_p, wq_r, wk_p, wv, e, er)]


def _rope_inv():
    inv = ROPE_THETA ** (-(jnp.arange(0, MLA_ROPE, 2, dtype=F32) / MLA_ROPE))
    z = jnp.zeros((HEAD_PAD - MLA_NOPE - MLA_ROPE,), F32)
    return jnp.concatenate([jnp.zeros((MLA_NOPE,), F32), inv, inv, z])[None, :]


def _s5_params(lam_re, lam_im, log_dt, b_re, b_im, c_re, c_im):
    lr = jnp.minimum(lam_re.astype(F32), -1e-4)
    li = lam_im.astype(F32)
    dt = jnp.exp(log_dt.astype(F32))[:, None]
    mag = jnp.exp(lr * dt)
    ab_r = mag * jnp.cos(li * dt)
    ab_i = mag * jnp.sin(li * dt)
    den = lr * lr + li * li
    nr = ab_r - 1.0
    coef_r = ((nr * lr + ab_i * li) / den)[..., None]
    coef_i = ((ab_i * lr - nr * li) / den)[..., None]
    bb_r = coef_r * b_re.astype(F32) - coef_i * b_im.astype(F32)
    bb_i = coef_r * b_im.astype(F32) + coef_i * b_re.astype(F32)
    n_slab = S5_WIDTH // S5_SLAB
    gps = S5_SLAB // S5_GROUP
    eye = jnp.eye(gps, dtype=F32)

    def in_mat(bb):
        b4 = bb.reshape(n_slab, gps, S5_STATE, S5_GROUP)
        return jnp.einsum('jgnc,gh->jgchn', b4, eye).reshape(n_slab, S5_SLAB, S5_SLAB_STATE)

    def out_mat(cc):
        c4 = cc.astype(F32).reshape(n_slab, gps, S5_GROUP, S5_STATE)
        return jnp.einsum('jgcn,gh->jgnhc', c4, eye).reshape(n_slab, S5_SLAB_STATE, S5_SLAB)

    bm = jnp.concatenate([in_mat(bb_r), in_mat(bb_i)], axis=2).astype(BF16)
    cm = jnp.concatenate([out_mat(c_re), -out_mat(c_im)], axis=1).astype(BF16)
    ar = ab_r.reshape(n_slab, 1, S5_SLAB_STATE)
    ai = ab_i.reshape(n_slab, 1, S5_SLAB_STATE)
    return bm, cm, ar, ai


def kernel(x, positions, meta_tokens, ln_in_g, ln_in_b, w_in, mla_q_norm, mla_w_uq, mla_kv_norm, mla_w_ukv,
           s5_lam_re, s5_lam_im, s5_log_dt, s5_b_re, s5_b_im, s5_c_re, s5_c_im, s5_d, s5_w_glu,
           hg_lb_logits, hg_out_norm, w_br_mla, w_br_s5, w_br_hg, w_out, ln1_g, ln1_b,
           w_ffn_gate, w_ffn_up, w_ffn_down, ln2_g, ln2_b):
    nb, s, d = x.shape
    depth = w_in.shape[0]
    nblk = (s + N_META) // T_BLK
    rb = T_BLK * nb
    n_rows = nblk * rb
    alpha = (2 * depth) ** 0.25
    row2 = lambda a: a.astype(F32)[None, :]

    h = _ln_in(x, meta_tokens.astype(x.dtype), row2(ln_in_g), row2(ln_in_b)).reshape(n_rows, d)

    meta_pos = jnp.broadcast_to(jnp.arange(N_META, dtype=jnp.int32)[None, :], (nb, N_META))
    pos = jnp.concatenate([meta_pos, positions.astype(jnp.int32) + N_META], axis=1)
    pos = pos.reshape(nb, nblk, T_BLK).transpose(1, 0, 2).reshape(n_rows, 1)
    inv128 = _rope_inv()
    p_lb = jax.nn.softmax(hg_lb_logits.astype(F32), axis=0)
    lower_bounds = jnp.cumsum(p_lb, axis=0) - p_lb[0]
    tri = jnp.asarray(np.tril(np.ones((HG_CHUNK, HG_CHUNK), np.float32)), BF16)
    lvl = jnp.asarray(_hgrn_level_matrix())
    perm = jnp.asarray(_time_major_perm(nb), BF16)

    for li in range(depth):
        s5_mats = _s5_params(s5_lam_re[li], s5_lam_im[li], s5_log_dt[li], s5_b_re[li], s5_b_im[li],
                             s5_c_re[li], s5_c_im[li])
        zhg, zg, q, k, v, y_s5 = _front(
            h, _permute_w_in(w_in[li]), pos, inv128, row2(mla_q_norm[li]), row2(mla_kv_norm[li]),
            *_mla_weights(mla_w_uq[li], mla_w_ukv[li]), perm, *s5_mats, row2(s5_d[li]),
            s5_w_glu[li].astype(BF16), nb)
        o_mla = _attention(q, k, v, nb, nblk)
        o_hg = _hgrn(zhg, lower_bounds[li][None, :], row2(hg_out_norm[li]), tri, lvl, nb, nblk)
        h = _back(o_mla, y_s5, o_hg, zg, h, w_br_mla[li].astype(BF16), w_br_s5[li].astype(BF16),
                  w_br_hg[li].astype(BF16), w_out[li].astype(BF16), row2(ln1_g[li]), row2(ln1_b[li]),
                  w_ffn_gate[li].astype(BF16), w_ffn_up[li].astype(BF16), w_ffn_down[li].astype(BF16),
                  row2(ln2_g[li]), row2(ln2_b[li]), nb, alpha, final=li == depth - 1)
    return h
```

```python
import functools
import math

import jax
import jax.numpy as jnp
import numpy as np
from jax import lax
from jax.experimental import pallas as pl
from jax.experimental.pallas import tpu as pltpu

F32 = jnp.float32
BF16 = jnp.bfloat16

N_META = 16
MLA_HEADS = 8
MLA_NOPE = 64
MLA_ROPE = 32
MLA_V = 64
MLA_Q_RANK = 256
MLA_KV_RANK = 256
ROPE_THETA = 10000.0
MASK_VALUE = -1e9
LOG2_E = math.log2(math.e)
HEAD_PAD = 128
S5_WIDTH = 512
S5_GROUP = 16
S5_GROUPS = S5_WIDTH // S5_GROUP
S5_STATE = 64
S5_SLAB = 128
S5_SLAB_STATE = (S5_SLAB // S5_GROUP) * S5_STATE
HG_HEADS = 4
HG_KEY = 128
HG_VAL = 128
HG_QK = HG_HEADS * HG_KEY
HG_VW = HG_HEADS * HG_VAL
HG_F_MIN = 1e-6
HG_CHUNK = 128
HG_LEVELS = (128, 64, 32, 16)
HG_BOTTOM = 8
N_BRANCH = 3
T_BLK = 16
ATT_TQ = 256
ATT_HEADS_PER_STEP = 4
ATT_HEADS_PER_BODY = 4
VMEM_LIMIT = 56 * 1024 * 1024

ZA_W = 1024
ZHG_W = 2048
ZG_W = 3072
ZKR_W = 128
Z_W = ZA_W + ZHG_W + ZG_W + ZKR_W
WIDE_CHUNK = 512
FILL_BEFORE_MLA = 3


def _cparams(n_grid, sem="parallel"):
    return pltpu.CompilerParams(dimension_semantics=(sem,) * n_grid, vmem_limit_bytes=VMEM_LIMIT)


def _const_spec(shape):
    nd = len(shape)
    return pl.BlockSpec(shape, lambda *_: (0,) * nd, pipeline_mode=pl.Buffered(1))


def _seq_spec(nblk, width, col):
    return pl.BlockSpec((nblk, None, T_BLK, width), lambda b, *g: (0, b, 0, col(b, *g)))


def _dot(a, b):
    return jnp.dot(a, b, preferred_element_type=F32)


def _dot_nt(a, b):
    return lax.dot_general(a, b, (((1,), (1,)), ((), ())), preferred_element_type=F32)


def _dot_tn(a, b):
    return lax.dot_general(a, b, (((0,), (0,)), ((), ())), preferred_element_type=F32)


def _sigmoid(x):
    return 1.0 / (1.0 + jnp.exp(-x))


def _layer_norm(x, g, b, eps=1e-5):
    mu = jnp.mean(x, axis=-1, keepdims=True)
    xc = x - mu
    var = jnp.mean(xc * xc, axis=-1, keepdims=True)
    return xc * lax.rsqrt(var + eps) * g + b


def _rms_norm(x, g, eps=1e-6):
    return x * lax.rsqrt(jnp.mean(x * x, axis=-1, keepdims=True) + eps) * g


def _load_rows(ref, blk0, n_blk, cs):
    x = ref[pl.ds(blk0, n_blk), :, cs]
    return x.reshape(n_blk * T_BLK, x.shape[-1])


def _store_rows(ref, blk0, n_blk, cs, x):
    ref[pl.ds(blk0, n_blk), :, cs] = x.reshape(n_blk, T_BLK, x.shape[-1])


def _ln_in_kernel(x_ref, meta_ref, g_ref, b_ref, o_ref, *, nb):
    i = pl.program_id(0)

    @pl.when(i == 0)
    def _():
        y = _layer_norm(meta_ref[...], g_ref[...], b_ref[...])
        for b in range(nb):
            o_ref[0, b] = y

    @pl.when(i > 0)
    def _():
        o_ref[0] = _layer_norm(x_ref[...], g_ref[...], b_ref[...])


def _ln_in(x, meta, g, b):
    nb, s, d = x.shape
    nblk = (s + N_META) // T_BLK
    return pl.pallas_call(
        functools.partial(_ln_in_kernel, nb=nb),
        out_shape=jax.ShapeDtypeStruct((nblk, nb, T_BLK, d), F32),
        grid=(nblk,),
        in_specs=[
            pl.BlockSpec((nb, T_BLK, d), lambda i: (0, jnp.maximum(i - 1, 0), 0)),
            _const_spec((N_META, d)),
            _const_spec((1, d)),
            _const_spec((1, d)),
        ],
        out_specs=pl.BlockSpec((1, nb, T_BLK, d), lambda i: (i, 0, 0, 0)),
        compiler_params=_cparams(1),
        name="ln_in",
    )(x, meta, g, b)


def _rope_tables(pos_ref, inv_ref):
    ang = pos_ref[...].astype(F32) * inv_ref[...]
    return jnp.cos(ang), jnp.sin(ang)


def _mla_prep(cq, ckv, kr, cos, sin, qn_ref, kvn_ref, wq_ref, wqr_ref, wk_ref, wv_ref, e_ref, er_ref,
              q_ref, k_ref, v_ref):
    scale = (MLA_NOPE + MLA_ROPE) ** -0.5 * LOG2_E
    cqn = _rms_norm(cq, qn_ref[...]).astype(BF16)
    ckvn = _rms_norm(ckv, kvn_ref[...]).astype(BF16)
    kr = kr.astype(BF16)
    cos_q = cos * scale
    sin_q = sin * scale
    for h in range(MLA_HEADS):
        cs = slice(h * HEAD_PAD, (h + 1) * HEAD_PAD)
        qf = _dot(cqn, wq_ref[:, cs])
        qr = _dot(cqn, wqr_ref[:, cs])
        q_ref[:, cs] = (qf * cos_q + qr * sin_q).astype(BF16)
        kf = _dot(ckvn, wk_ref[:, cs]) + _dot(kr, e_ref[:, cs])
        krot = _dot(kr, er_ref[:, cs])
        k_ref[:, cs] = (kf * cos + krot * sin).astype(BF16)
    v_ref[...] = _dot(ckvn, wv_ref[...]).astype(BF16)


def _s5_block(u, perm_ref, bm_ref, cm_ref, ar_ref, ai_ref, d_ref, wg_ref, o_ref,
              xr_ref, xi_ref, buf_ref, y_ref, nb, fillers):
    n_slab = S5_WIDTH // S5_SLAB
    ns = S5_SLAB_STATE
    u = _dot(perm_ref[...], u).astype(BF16)
    for j in range(n_slab):
        buf_ref[...] = _dot(u[:, j * S5_SLAB:(j + 1) * S5_SLAB], bm_ref[j])
        if fillers:
            fillers.pop(0)()
        ar = jnp.broadcast_to(ar_ref[j], (nb, ns))
        ai = jnp.broadcast_to(ai_ref[j], (nb, ns))
        xr = xr_ref[j]
        xi = xi_ref[j]
        for t in range(T_BLK):
            rows = slice(t * nb, (t + 1) * nb)
            nr = ar * xr - ai * xi + buf_ref[rows, 0:ns]
            ni = ar * xi + ai * xr + buf_ref[rows, ns:2 * ns]
            buf_ref[rows, 0:ns] = nr
            buf_ref[rows, ns:2 * ns] = ni
            xr, xi = nr, ni
        xr_ref[j] = xr
        xi_ref[j] = xi
        y_ref[:, j * S5_SLAB:(j + 1) * S5_SLAB] = _dot(buf_ref[...].astype(BF16), cm_ref[j])

    while fillers:
        fillers.pop(0)()
    y = y_ref[...] + d_ref[...] * u.astype(F32)
    y = 0.5 * y * (1.0 + jnp.tanh(math.sqrt(2.0 / math.pi) * (y + 0.044715 * (y * y * y))))
    gate = _sigmoid(_dot(y.astype(BF16), wg_ref[...]))
    out = (y * gate).astype(BF16)
    o_ref[...] = _dot_tn(perm_ref[...], out).astype(BF16)


def _front_kernel(h_ref, w_ref, pos_ref, inv_ref, qn_ref, kvn_ref, wq_ref, wqr_ref, wk_ref, wv_ref, e_ref, er_ref,
                  perm_ref, bm_ref, cm_ref, ar_ref, ai_ref, d_ref, wglu_ref,
                  zhg_ref, zg_ref, q_ref, k_ref, v_ref, ys5_ref,
                  xr_ref, xi_ref, buf_ref, y_ref, *, nb):
    @pl.when(pl.program_id(0) == 0)
    def _():
        xr_ref[...] = jnp.zeros_like(xr_ref)
        xi_ref[...] = jnp.zeros_like(xi_ref)

    cos, sin = _rope_tables(pos_ref, inv_ref)
    x = h_ref[...].astype(BF16)
    za = _dot(x, w_ref[:, 0:ZA_W])
    zkr = _dot(x, w_ref[:, ZA_W + ZHG_W + ZG_W:Z_W])

    def wide_chunk(o_ref, c, off):
        def run():
            o_ref[:, c:c + WIDE_CHUNK] = _dot(x, w_ref[:, off + c:off + c + WIDE_CHUNK]).astype(BF16)
        return run

    fillers = [wide_chunk(zhg_ref, c, ZA_W) for c in range(0, ZHG_W, WIDE_CHUNK)]
    fillers += [wide_chunk(zg_ref, c, ZA_W + ZHG_W) for c in range(0, ZG_W, WIDE_CHUNK)]
    for _ in range(FILL_BEFORE_MLA):
        fillers.pop(0)()
    c0 = S5_WIDTH
    _mla_prep(za[:, c0:c0 + MLA_Q_RANK], za[:, c0 + MLA_Q_RANK:c0 + MLA_Q_RANK + MLA_KV_RANK], zkr,
              cos, sin, qn_ref, kvn_ref, wq_ref, wqr_ref, wk_ref, wv_ref, e_ref, er_ref,
              q_ref, k_ref, v_ref)
    _s5_block(za[:, 0:S5_WIDTH].astype(BF16), perm_ref, bm_ref, cm_ref, ar_ref, ai_ref, d_ref, wglu_ref, ys5_ref,
              xr_ref, xi_ref, buf_ref, y_ref, nb, fillers)


def _front(h, w, pos, inv128, qn, kvn, wq, wqr, wk, wv, e, er, perm, bm, cm, ar, ai, dskip, wglu, nb):
    r, d = h.shape
    rb = T_BLK * nb
    hw = MLA_HEADS * HEAD_PAD
    vw = MLA_HEADS * MLA_V
    n_slab = S5_WIDTH // S5_SLAB
    row = lambda width: pl.BlockSpec((rb, width), lambda i: (i, 0))
    widths = (ZHG_W, ZG_W, hw, hw, vw, S5_WIDTH)
    return pl.pallas_call(
        functools.partial(_front_kernel, nb=nb),
        out_shape=[jax.ShapeDtypeStruct((r, wd), BF16) for wd in widths],
        grid=(r // rb,),
        in_specs=[
            row(d), _const_spec((d, Z_W)), row(1),
            _const_spec((1, HEAD_PAD)), _const_spec((1, MLA_Q_RANK)), _const_spec((1, MLA_KV_RANK)),
            _const_spec((MLA_Q_RANK, hw)), _const_spec((MLA_Q_RANK, hw)), _const_spec((MLA_KV_RANK, hw)),
            _const_spec((MLA_KV_RANK, vw)), _const_spec((ZKR_W, hw)), _const_spec((ZKR_W, hw)),
            _const_spec((rb, rb)),
            _const_spec((n_slab, S5_SLAB, 2 * S5_SLAB_STATE)),
            _const_spec((n_slab, 2 * S5_SLAB_STATE, S5_SLAB)),
            _const_spec((n_slab, 1, S5_SLAB_STATE)),
            _const_spec((n_slab, 1, S5_SLAB_STATE)),
            _const_spec((1, S5_WIDTH)),
            _const_spec((S5_WIDTH, S5_WIDTH)),
        ],
        out_specs=[row(wd) for wd in widths],
        scratch_shapes=[
            pltpu.VMEM((n_slab, nb, S5_SLAB_STATE), F32),
            pltpu.VMEM((n_slab, nb, S5_SLAB_STATE), F32),
            pltpu.VMEM((rb, 2 * S5_SLAB_STATE), F32),
            pltpu.VMEM((rb, S5_WIDTH), F32),
        ],
        compiler_params=_cparams(1, "arbitrary"),
        name="front",
    )(h, w, pos, inv128, qn, kvn, wq, wqr, wk, wv, e, er, perm, bm, cm, ar, ai, dskip, wglu)


def _attn_kernel(q_ref, k_ref, v_ref, o_ref, vt_ref, *, n_heads, hb, tq, nq):
    bpq = tq // T_BLK
    vw = n_heads * MLA_V

    for j in range(nq):
        vb = _load_rows(v_ref, 1 + j * bpq, bpq, slice(0, vw))
        vt_ref[j] = vb.astype(F32).T.astype(BF16)
    v0t = v_ref[0].astype(F32).T.astype(BF16)

    def col_softmax_update(m, l, acc, st, vt):
        mn = jnp.maximum(m, jnp.max(st, axis=0, keepdims=True))
        a = jnp.exp2(m - mn)
        p = jnp.exp2(st - mn)
        l = a * l + jnp.sum(p, axis=0, keepdims=True)
        acc = a * acc + _dot(vt, p.astype(BF16))
        return mn, l, acc

    def causal(st, limit):
        r = lax.broadcasted_iota(jnp.int32, st.shape, 0)
        c = lax.broadcasted_iota(jnp.int32, st.shape, 1)
        return jnp.where(r - c <= limit, st, MASK_VALUE * LOG2_E)

    for g0 in range(0, n_heads, hb):
        heads = range(g0, g0 + hb)
        qs = [slice(h * HEAD_PAD, (h + 1) * HEAD_PAD) for h in heads]
        vs = [slice(h * MLA_V, (h + 1) * MLA_V) for h in heads]
        os_ = slice(g0 * MLA_V, (g0 + hb) * MLA_V)
        k0 = [k_ref[0, :, c] for c in qs]

        outs = []
        for n in range(hb):
            st = causal(_dot_nt(k0[n], q_ref[0, :, qs[n]]), 0)
            p = jnp.exp2(st - jnp.max(st, axis=0, keepdims=True))
            outs.append(_dot(v0t[vs[n], :], p.astype(BF16)) / jnp.sum(p, axis=0, keepdims=True))
        o_ref[0, :, os_] = jnp.concatenate(outs, axis=0).T.astype(BF16)

        def q_block(i, carry):
            blk0 = 1 + i * bpq
            qt = [_load_rows(q_ref, blk0, bpq, c).astype(F32).T.astype(BF16) for c in qs]
            s_meta = [_dot(k0[n], qt[n]) for n in range(hb)]

            def scores(j, n):
                return _dot(_load_rows(k_ref, 1 + j * bpq, bpq, qs[n]), qt[n])

            state = []
            for n in range(hb):
                state += [jnp.full((1, tq), -1e30, F32), jnp.zeros((1, tq), F32), jnp.zeros((MLA_V, tq), F32)]
            s_first = tuple(scores(0, n) for n in range(hb))

            def kv_block(j, carry):
                st, s_cur = carry
                s_next = tuple(scores(j + 1, n) for n in range(hb))
                out = []
                for n in range(hb):
                    out += col_softmax_update(*st[3 * n:3 * n + 3], s_cur[n], vt_ref[j, vs[n], :])
                return tuple(out), s_next

            state, s_diag = lax.fori_loop(0, i, kv_block, (tuple(state), s_first))
            outs = []
            for n in range(hb):
                st3 = col_softmax_update(*state[3 * n:3 * n + 3], causal(s_diag[n], 0), vt_ref[i, vs[n], :])
                m, l, acc = col_softmax_update(*st3, s_meta[n], v0t[vs[n], :])
                outs.append(acc / l)
            o = jnp.concatenate(outs, axis=0).T.astype(BF16)
            _store_rows(o_ref, blk0, bpq, os_, o)
            return carry

        lax.fori_loop(0, nq, q_block, 0)


def _attention(q, k, v, nb, nblk):
    hps = ATT_HEADS_PER_STEP
    n_hp = MLA_HEADS // hps
    qw = hps * HEAD_PAD
    vw = hps * MLA_V
    s = (nblk - 1) * T_BLK
    tq = min(ATT_TQ, s)
    nq = s // tq
    col = lambda b, p: p
    o = pl.pallas_call(
        functools.partial(_attn_kernel, n_heads=hps, hb=ATT_HEADS_PER_BODY, tq=tq, nq=nq),
        out_shape=jax.ShapeDtypeStruct((nblk, nb, T_BLK, MLA_HEADS * MLA_V), BF16),
        grid=(nb, n_hp),
        in_specs=[_seq_spec(nblk, qw, col), _seq_spec(nblk, qw, col), _seq_spec(nblk, vw, col)],
        out_specs=_seq_spec(nblk, vw, col),
        scratch_shapes=[pltpu.VMEM((nq, vw, tq), BF16)],
        compiler_params=_cparams(2),
        name="mla_attn",
    )(q.reshape(nblk, nb, T_BLK, -1), k.reshape(nblk, nb, T_BLK, -1), v.reshape(nblk, nb, T_BLK, -1))
    return o.reshape(nblk * nb * T_BLK, MLA_HEADS * MLA_V)


def _time_major_perm(nb):
    p = np.zeros((T_BLK * nb, T_BLK * nb), np.float32)
    for b in range(nb):
        for t in range(T_BLK):
            p[t * nb + b, b * T_BLK + t] = 1.0
    return p


def _block_row_bcast(x, m, row):
    t, c = x.shape
    if m == t:
        return jnp.broadcast_to(x[row:row + 1, :], x.shape)
    x3 = x.reshape(t // m, m, c)
    return jnp.broadcast_to(x3[:, row:row + 1, :], x3.shape).reshape(t, c)


def _hgrn_chunk(blk0, n_blk, q_ref, f_ref, v_ref, g_ref, lb_ref, on_ref, tri_ref, lvl_ref, o_ref, st_ref):
    t = n_blk * T_BLK
    cs = slice(0, HG_QK)
    heads = [slice(h * HG_KEY, (h + 1) * HG_KEY) for h in range(HG_HEADS)]
    lb = lb_ref[...]
    q = _load_rows(q_ref, blk0, n_blk, cs).astype(F32)
    zf = _load_rows(f_ref, blk0, n_blk, cs).astype(F32)
    v = _load_rows(v_ref, blk0, n_blk, cs)
    g = _load_rows(g_ref, blk0, n_blk, cs).astype(F32)
    e = jnp.exp(-jnp.abs(zf))
    rcp = 1.0 / (1.0 + e)
    pos = zf >= 0.0
    sig_p = jnp.where(pos, rcp, e * rcp)
    sig_n = jnp.where(pos, e * rcp, rcp)
    f = lb + (1.0 - lb) * sig_p
    log_f = jnp.log(jnp.maximum(f, HG_F_MIN))
    k = (1.0 - lb) * sig_n
    hi = log_f.astype(BF16)
    lo = (log_f - hi.astype(F32)).astype(BF16)
    tri = tri_ref[0:t, 0:t]
    cum = _dot(tri, hi) + _dot(tri, lo)
    lvl = lvl_ref[0:t, 0:t]
    r_idx = lax.broadcasted_iota(jnp.int32, (t, HG_QK), 0)
    c8 = _block_row_bcast(cum, HG_BOTTOM, HG_BOTTOM // 2 - 1)
    qe = (q * jnp.exp(cum - c8)).astype(BF16)
    ke = (k * jnp.exp(c8 - cum)).astype(BF16)
    n_lvl = len(HG_LEVELS)
    scores = [jnp.where(lvl == n_lvl, _dot_nt(qe[:, hs], ke[:, hs]), 0.0) for hs in heads]
    for li, m in enumerate(HG_LEVELS):
        if m > t:
            continue
        half = m // 2
        cmid = _block_row_bcast(cum, m, half - 1)
        upper = (r_idx & (m - 1)) >= half
        ex = jnp.exp(jnp.where(upper, cum - cmid, cmid - cum))
        qe = jnp.where(upper, q * ex, 0.0).astype(BF16)
        ke = jnp.where(upper, 0.0, k * ex).astype(BF16)
        scores = [s + jnp.where(lvl == li, _dot_nt(qe[:, hs], ke[:, hs]), 0.0) for s, hs in zip(scores, heads)]
    qd = (q * jnp.exp(cum)).astype(BF16)
    last = cum[t - 1:t, :]
    kd = (k * jnp.exp(last - cum)).astype(BF16)
    dec = jnp.exp(last)
    outs = []
    for h, hs in enumerate(heads):
        st = st_ref[h]
        o = _dot(scores[h].astype(BF16), v[:, hs]) + _dot_nt(qd[:, hs], st.astype(BF16))
        st_ref[h] = st * dec[:, hs] + _dot_tn(v[:, hs], kd[:, hs])
        outs.append(o * lax.rsqrt(jnp.mean(o * o, axis=-1, keepdims=True) + 1e-6))
    o = jnp.concatenate(outs, axis=1) * on_ref[...]
    _store_rows(o_ref, blk0, n_blk, cs, (o * (g * _sigmoid(g))).astype(BF16))


def _hgrn_kernel(q_ref, f_ref, v_ref, g_ref, lb_ref, on_ref, tri_ref, lvl_ref, o_ref, st_ref, *, n_chunks):
    bpc = HG_CHUNK // T_BLK
    chunk = functools.partial(_hgrn_chunk, q_ref=q_ref, f_ref=f_ref, v_ref=v_ref, g_ref=g_ref, lb_ref=lb_ref,
                              on_ref=on_ref, tri_ref=tri_ref, lvl_ref=lvl_ref, o_ref=o_ref, st_ref=st_ref)
    st_ref[...] = jnp.zeros_like(st_ref)
    chunk(0, 1)

    def body(i, carry):
        chunk(1 + i * bpc, bpc)
        return carry

    lax.fori_loop(0, n_chunks, body, 0)


def _hgrn(zhg, lb, onorm, tri, lvl, nb, nblk):
    z4 = zhg.reshape(nblk, nb, T_BLK, ZHG_W)
    col = lambda j: _seq_spec(nblk, HG_QK, lambda b: j)
    o = pl.pallas_call(
        functools.partial(_hgrn_kernel, n_chunks=(nblk - 1) * T_BLK // HG_CHUNK),
        out_shape=jax.ShapeDtypeStruct((nblk, nb, T_BLK, HG_VW), BF16),
        grid=(nb,),
        in_specs=[col(0), col(1), col(2), col(3),
                  _const_spec((1, HG_QK)), _const_spec((1, HG_VW)),
                  _const_spec((HG_CHUNK, HG_CHUNK)), _const_spec((HG_CHUNK, HG_CHUNK))],
        out_specs=_seq_spec(nblk, HG_VW, lambda b: 0),
        scratch_shapes=[pltpu.VMEM((HG_HEADS, HG_VAL, HG_KEY), F32)],
        compiler_params=_cparams(1),
        name="hgrn2",
    )(z4, z4, z4, z4, lb, onorm, tri, lvl)
    return o.reshape(nblk * nb * T_BLK, HG_VW)


def _hgrn_level_matrix():
    r = np.arange(HG_CHUNK)[:, None]
    c = np.arange(HG_CHUNK)[None, :]
    lvl = np.zeros((HG_CHUNK, HG_CHUNK), np.int32)
    for li, m in enumerate(HG_LEVELS):
        lvl = np.where(r // m == c // m, li, lvl)
    lvl = np.where(r // HG_BOTTOM == c // HG_BOTTOM, len(HG_LEVELS), lvl)
    return np.where(c <= r, lvl, -1).astype(np.int32)


def _back_kernel(om_ref, os_ref, oh_ref, gm_ref, gs_ref, gh_ref, h_ref,
                 wm_ref, ws_ref, wh_ref, wo_ref, g1_ref, b1_ref,
                 wg_ref, wu_ref, wd_ref, g2_ref, b2_ref, o_ref, r1_ref, r2_ref, *, alpha):
    @pl.when(pl.program_id(0) == 0)
    def _():
        r1_ref[...] = jnp.zeros_like(r1_ref)
        r2_ref[...] = jnp.zeros_like(r2_ref)

    r1 = r1_ref[...]
    r2 = r2_ref[...]
    ym = _dot(om_ref[...], wm_ref[...])
    ys = _dot(os_ref[...], ws_ref[...])
    yh = _dot(oh_ref[...], wh_ref[...])
    o_ref[...] = _layer_norm(r2, g2_ref[...], b2_ref[...]).reshape(o_ref.shape)
    h1 = _layer_norm(r1, g1_ref[...], b1_ref[...])
    hb = h1.astype(BF16)
    a = _dot(hb, wg_ref[...])
    u = _dot(hb, wu_ref[...])
    mixed = _sigmoid(gm_ref[...].astype(F32)) * ym
    mixed += _sigmoid(gs_ref[...].astype(F32)) * ys
    mixed += _sigmoid(gh_ref[...].astype(F32)) * yh
    r1_ref[...] = alpha * h_ref[...] + _dot(mixed.astype(BF16), wo_ref[...])
    r2_ref[...] = alpha * h1 + _dot((a * _sigmoid(a) * u).astype(BF16), wd_ref[...])


def _back(om, os_, oh, zg, h, wm, ws, wh, wo, g1, b1, wg, wu, wd, g2, b2, nb, alpha, final):
    r, d = h.shape
    bw = om.shape[1]
    dff = wg.shape[1]
    rb = T_BLK * nb
    skip = 1 if final else 0
    n_blk = r // rb - skip
    lag = 2
    row = lambda width, j=0: pl.BlockSpec((rb, width), lambda i: (jnp.minimum(i, n_blk - 1) + skip, j))
    if final:
        out_shape = jax.ShapeDtypeStruct((nb, n_blk * T_BLK, d), F32)
        out_spec = pl.BlockSpec((nb, T_BLK, d), lambda i: (0, jnp.maximum(i - lag, 0), 0))
    else:
        out_shape = jax.ShapeDtypeStruct((r, d), F32)
        out_spec = pl.BlockSpec((rb, d), lambda i: (jnp.maximum(i - lag, 0), 0))
    return pl.pallas_call(
        functools.partial(_back_kernel, alpha=alpha),
        out_shape=out_shape,
        grid=(n_blk + lag,),
        in_specs=[row(bw), row(bw), row(bw), row(d, 0), row(d, 1), row(d, 2), row(d),
                  _const_spec((bw, d)), _const_spec((bw, d)), _const_spec((bw, d)), _const_spec((d, d)),
                  _const_spec((1, d)), _const_spec((1, d)),
                  _const_spec((d, dff)), _const_spec((d, dff)), _const_spec((dff, d)),
                  _const_spec((1, d)), _const_spec((1, d))],
        out_specs=out_spec,
        scratch_shapes=[pltpu.VMEM((rb, d), F32), pltpu.VMEM((rb, d), F32)],
        compiler_params=_cparams(1, "arbitrary"),
        name="merge_ffn",
    )(om, os_, oh, zg, zg, zg, h, wm, ws, wh, wo, g1, b1, wg, wu, wd, g2, b2)


def _permute_w_in(w):
    mla_in = MLA_Q_RANK + MLA_KV_RANK + MLA_ROPE
    s5_0, hg_0 = mla_in, mla_in + S5_WIDTH
    g_0 = hg_0 + 2 * HG_QK + 2 * HG_VW
    pad = jnp.zeros((w.shape[0], ZKR_W - MLA_ROPE), w.dtype)
    return jnp.concatenate([w[:, s5_0:hg_0], w[:, 0:MLA_Q_RANK + MLA_KV_RANK], w[:, hg_0:g_0], w[:, g_0:],
                            w[:, MLA_Q_RANK + MLA_KV_RANK:mla_in], pad], axis=1).astype(BF16)


def _rot_half(x):
    x1, x2 = jnp.split(x, 2, axis=-1)
    return jnp.concatenate([-x2, x1], axis=-1)


def _mla_weights(w_uq, w_ukv):
    rq, rkv = w_uq.shape[0], w_ukv.shape[0]
    zpad = HEAD_PAD - MLA_NOPE - MLA_ROPE
    wq = w_uq.reshape(rq, MLA_HEADS, MLA_NOPE + MLA_ROPE)
    q_nope, q_rope = wq[..., :MLA_NOPE], wq[..., MLA_NOPE:]
    zq = jnp.zeros((rq, MLA_HEADS, zpad), w_uq.dtype)
    wq_p = jnp.concatenate([q_nope, q_rope, zq], axis=-1).reshape(rq, -1)
    wq_r = jnp.concatenate([jnp.zeros_like(q_nope), _rot_half(q_rope), zq], axis=-1).reshape(rq, -1)
    wkv = w_ukv.reshape(rkv, MLA_HEADS, MLA_NOPE + MLA_V)
    zk = jnp.zeros((rkv, MLA_HEADS, HEAD_PAD - MLA_NOPE), w_ukv.dtype)
    wk_p = jnp.concatenate([wkv[..., :MLA_NOPE], zk], axis=-1).reshape(rkv, -1)
    wv = wkv[..., MLA_NOPE:].reshape(rkv, -1)
    eye = jnp.eye(ZKR_W, MLA_ROPE, dtype=F32)
    place = jnp.concatenate([jnp.zeros((ZKR_W, MLA_NOPE), F32), eye, jnp.zeros((ZKR_W, zpad), F32)], axis=-1)
    place_r = jnp.concatenate([jnp.zeros((ZKR_W, MLA_NOPE), F32), _rot_half(eye),
                               jnp.zeros((ZKR_W, zpad), F32)], axis=-1)
    e = jnp.tile(place, (1, MLA_HEADS))
    er = jnp.tile(place_r, (1, MLA_HEADS))
    return [a.astype(BF16) for a in (wq_p, wq_r, wk_p, wv, e, er)]


def _rope_inv():
    inv = ROPE_THETA ** (-(jnp.arange(0, MLA_ROPE, 2, dtype=F32) / MLA_ROPE))
    z = jnp.zeros((HEAD_PAD - MLA_NOPE - MLA_ROPE,), F32)
    return jnp.concatenate([jnp.zeros((MLA_NOPE,), F32), inv, inv, z])[None, :]


def _s5_params(lam_re, lam_im, log_dt, b_re, b_im, c_re, c_im):
    lr = jnp.minimum(lam_re.astype(F32), -1e-4)
    li = lam_im.astype(F32)
    dt = jnp.exp(log_dt.astype(F32))[:, None]
    mag = jnp.exp(lr * dt)
    ab_r = mag * jnp.cos(li * dt)
    ab_i = mag * jnp.sin(li * dt)
    den = lr * lr + li * li
    nr = ab_r - 1.0
    coef_r = ((nr * lr + ab_i * li) / den)[..., None]
    coef_i = ((ab_i * lr - nr * li) / den)[..., None]
    bb_r = coef_r * b_re.astype(F32) - coef_i * b_im.astype(F32)
    bb_i = coef_r * b_im.astype(F32) + coef_i * b_re.astype(F32)
    n_slab = S5_WIDTH // S5_SLAB
    gps = S5_SLAB // S5_GROUP
    eye = jnp.eye(gps, dtype=F32)

    def in_mat(bb):
        b4 = bb.reshape(n_slab, gps, S5_STATE, S5_GROUP)
        return jnp.einsum('jgnc,gh->jgchn', b4, eye).reshape(n_slab, S5_SLAB, S5_SLAB_STATE)

    def out_mat(cc):
        c4 = cc.astype(F32).reshape(n_slab, gps, S5_GROUP, S5_STATE)
        return jnp.einsum('jgcn,gh->jgnhc', c4, eye).reshape(n_slab, S5_SLAB_STATE, S5_SLAB)

    bm = jnp.concatenate([in_mat(bb_r), in_mat(bb_i)], axis=2).astype(BF16)
    cm = jnp.concatenate([out_mat(c_re), -out_mat(c_im)], axis=1).astype(BF16)
    ar = ab_r.reshape(n_slab, 1, S5_SLAB_STATE)
    ai = ab_i.reshape(n_slab, 1, S5_SLAB_STATE)
    return bm, cm, ar, ai


def kernel(x, positions, meta_tokens, ln_in_g, ln_in_b, w_in, mla_q_norm, mla_w_uq, mla_kv_norm, mla_w_ukv,
           s5_lam_re, s5_lam_im, s5_log_dt, s5_b_re, s5_b_im, s5_c_re, s5_c_im, s5_d, s5_w_glu,
           hg_lb_logits, hg_out_norm, w_br_mla, w_br_s5, w_br_hg, w_out, ln1_g, ln1_b,
           w_ffn_gate, w_ffn_up, w_ffn_down, ln2_g, ln2_b):
    nb, s, d = x.shape
    depth = w_in.shape[0]
    nblk = (s + N_META) // T_BLK
    rb = T_BLK * nb
    n_rows = nblk * rb
    alpha = (2 * depth) ** 0.25
    row2 = lambda a: a.astype(F32)[None, :]

    h = _ln_in(x, meta_tokens.astype(x.dtype), row2(ln_in_g), row2(ln_in_b)).reshape(n_rows, d)

    meta_pos = jnp.broadcast_to(jnp.arange(N_META, dtype=jnp.int32)[None, :], (nb, N_META))
    pos = jnp.concatenate([meta_pos, positions.astype(jnp.int32) + N_META], axis=1)
    pos = pos.reshape(nb, nblk, T_BLK).transpose(1, 0, 2).reshape(n_rows, 1)
    inv128 = _rope_inv()
    p_lb = jax.nn.softmax(hg_lb_logits.astype(F32), axis=0)
    lower_bounds = jnp.cumsum(p_lb, axis=0) - p_lb[0]
    tri = jnp.asarray(np.tril(np.ones((HG_CHUNK, HG_CHUNK), np.float32)), BF16)
    lvl = jnp.asarray(_hgrn_level_matrix())
    perm = jnp.asarray(_time_major_perm(nb), BF16)

    for li in range(depth):
        s5_mats = _s5_params(s5_lam_re[li], s5_lam_im[li], s5_log_dt[li], s5_b_re[li], s5_b_im[li],
                             s5_c_re[li], s5_c_im[li])
        zhg, zg, q, k, v, y_s5 = _front(
            h, _permute_w_in(w_in[li]), pos, inv128, row2(mla_q_norm[li]), row2(mla_kv_norm[li]),
            *_mla_weights(mla_w_uq[li], mla_w_ukv[li]), perm, *s5_mats, row2(s5_d[li]),
            s5_w_glu[li].astype(BF16), nb)
        o_mla = _attention(q, k, v, nb, nblk)
        o_hg = _hgrn(zhg, lower_bounds[li][None, :], row2(hg_out_norm[li]), tri, lvl, nb, nblk)
        h = _back(o_mla, y_s5, o_hg, zg, h, w_br_mla[li].astype(BF16), w_br_s5[li].astype(BF16),
                  w_br_hg[li].astype(BF16), w_out[li].astype(BF16), row2(ln1_g[li]), row2(ln1_b[li]),
                  w_ffn_gate[li].astype(BF16), w_ffn_up[li].astype(BF16), w_ffn_down[li].astype(BF16),
                  row2(ln2_g[li]), row2(ln2_b[li]), nb, alpha, final=li == depth - 1)
    return h
```

```python
import functools
import math

import jax
import jax.numpy as jnp
import numpy as np
from jax import lax
from jax.experimental import pallas as pl
from jax.experimental.pallas import tpu as pltpu

F32 = jnp.float32
BF16 = jnp.bfloat16

N_META = 16
MLA_HEADS = 8
MLA_NOPE = 64
MLA_ROPE = 32
MLA_V = 64
MLA_Q_RANK = 256
MLA_KV_RANK = 256
ROPE_THETA = 10000.0
MASK_VALUE = -1e9
LOG2_E = math.log2(math.e)
HEAD_PAD = 128
S5_WIDTH = 512
S5_GROUP = 16
S5_GROUPS = S5_WIDTH // S5_GROUP
S5_STATE = 64
S5_SLAB = 128
S5_SLAB_STATE = (S5_SLAB // S5_GROUP) * S5_STATE
HG_HEADS = 4
HG_KEY = 128
HG_VAL = 128
HG_QK = HG_HEADS * HG_KEY
HG_VW = HG_HEADS * HG_VAL
HG_F_MIN = 1e-6
HG_CHUNK = 128
HG_LEVELS = (128, 64, 32, 16)
HG_BOTTOM = 8
N_BRANCH = 3
T_BLK = 16
ATT_TQ = 256
ATT_HEADS_PER_STEP = 4
VMEM_LIMIT = 56 * 1024 * 1024

ZA_W = 1024
ZHG_W = 2048
ZG_W = 3072
ZKR_W = 128
Z_W = ZA_W + ZHG_W + ZG_W + ZKR_W
WIDE_CHUNK = 512
FILL_BEFORE_MLA = 3


def _cparams(n_grid, sem="parallel"):
    return pltpu.CompilerParams(dimension_semantics=(sem,) * n_grid, vmem_limit_bytes=VMEM_LIMIT)


def _const_spec(shape):
    nd = len(shape)
    return pl.BlockSpec(shape, lambda *_: (0,) * nd, pipeline_mode=pl.Buffered(1))


def _seq_spec(nblk, width, col):
    return pl.BlockSpec((nblk, None, T_BLK, width), lambda b, *g: (0, b, 0, col(b, *g)))


def _dot(a, b):
    return jnp.dot(a, b, preferred_element_type=F32)


def _dot_nt(a, b):
    return lax.dot_general(a, b, (((1,), (1,)), ((), ())), preferred_element_type=F32)


def _dot_tn(a, b):
    return lax.dot_general(a, b, (((0,), (0,)), ((), ())), preferred_element_type=F32)


def _sigmoid(x):
    return 1.0 / (1.0 + jnp.exp(-x))


def _layer_norm(x, g, b, eps=1e-5):
    mu = jnp.mean(x, axis=-1, keepdims=True)
    xc = x - mu
    var = jnp.mean(xc * xc, axis=-1, keepdims=True)
    return xc * lax.rsqrt(var + eps) * g + b


def _rms_norm(x, g, eps=1e-6):
    return x * lax.rsqrt(jnp.mean(x * x, axis=-1, keepdims=True) + eps) * g


def _load_rows(ref, blk0, n_blk, cs):
    x = ref[pl.ds(blk0, n_blk), :, cs]
    return x.reshape(n_blk * T_BLK, x.shape[-1])


def _store_rows(ref, blk0, n_blk, cs, x):
    ref[pl.ds(blk0, n_blk), :, cs] = x.reshape(n_blk, T_BLK, x.shape[-1])


def _ln_in_kernel(x_ref, meta_ref, g_ref, b_ref, o_ref, *, nb):
    i = pl.program_id(0)

    @pl.when(i == 0)
    def _():
        y = _layer_norm(meta_ref[...], g_ref[...], b_ref[...])
        for b in range(nb):
            o_ref[0, b] = y

    @pl.when(i > 0)
    def _():
        o_ref[0] = _layer_norm(x_ref[...], g_ref[...], b_ref[...])


def _ln_in(x, meta, g, b):
    nb, s, d = x.shape
    nblk = (s + N_META) // T_BLK
    return pl.pallas_call(
        functools.partial(_ln_in_kernel, nb=nb),
        out_shape=jax.ShapeDtypeStruct((nblk, nb, T_BLK, d), F32),
        grid=(nblk,),
        in_specs=[
            pl.BlockSpec((nb, T_BLK, d), lambda i: (0, jnp.maximum(i - 1, 0), 0)),
            _const_spec((N_META, d)),
            _const_spec((1, d)),
            _const_spec((1, d)),
        ],
        out_specs=pl.BlockSpec((1, nb, T_BLK, d), lambda i: (i, 0, 0, 0)),
        compiler_params=_cparams(1),
        name="ln_in",
    )(x, meta, g, b)


def _rope_tables(pos_ref, inv_ref):
    ang = pos_ref[...].astype(F32) * inv_ref[...]
    return jnp.cos(ang), jnp.sin(ang)


def _mla_prep(cq, ckv, kr, cos, sin, qn_ref, kvn_ref, wq_ref, wqr_ref, wk_ref, wv_ref, e_ref, er_ref,
              q_ref, k_ref, v_ref):
    scale = (MLA_NOPE + MLA_ROPE) ** -0.5 * LOG2_E
    cqn = _rms_norm(cq, qn_ref[...]).astype(BF16)
    ckvn = _rms_norm(ckv, kvn_ref[...]).astype(BF16)
    kr = kr.astype(BF16)
    cos_q = cos * scale
    sin_q = sin * scale
    for h in range(MLA_HEADS):
        cs = slice(h * HEAD_PAD, (h + 1) * HEAD_PAD)
        qf = _dot(cqn, wq_ref[:, cs])
        qr = _dot(cqn, wqr_ref[:, cs])
        q_ref[:, cs] = (qf * cos_q + qr * sin_q).astype(BF16)
        kf = _dot(ckvn, wk_ref[:, cs]) + _dot(kr, e_ref[:, cs])
        krot = _dot(kr, er_ref[:, cs])
        k_ref[:, cs] = (kf * cos + krot * sin).astype(BF16)
    v_ref[...] = _dot(ckvn, wv_ref[...]).astype(BF16)


def _s5_block(u, perm_ref, bm_ref, cm_ref, ar_ref, ai_ref, d_ref, wg_ref, o_ref,
              xr_ref, xi_ref, buf_ref, y_ref, nb, fillers):
    n_slab = S5_WIDTH // S5_SLAB
    ns = S5_SLAB_STATE
    u = _dot(perm_ref[...], u).astype(BF16)
    for j in range(n_slab):
        buf_ref[...] = _dot(u[:, j * S5_SLAB:(j + 1) * S5_SLAB], bm_ref[j])
        if fillers:
            fillers.pop(0)()
        ar = jnp.broadcast_to(ar_ref[j], (nb, ns))
        ai = jnp.broadcast_to(ai_ref[j], (nb, ns))
        xr = xr_ref[j]
        xi = xi_ref[j]
        for t in range(T_BLK):
            rows = slice(t * nb, (t + 1) * nb)
            nr = ar * xr - ai * xi + buf_ref[rows, 0:ns]
            ni = ar * xi + ai * xr + buf_ref[rows, ns:2 * ns]
            buf_ref[rows, 0:ns] = nr
            buf_ref[rows, ns:2 * ns] = ni
            xr, xi = nr, ni
        xr_ref[j] = xr
        xi_ref[j] = xi
        y_ref[:, j * S5_SLAB:(j + 1) * S5_SLAB] = _dot(buf_ref[...].astype(BF16), cm_ref[j])

    while fillers:
        fillers.pop(0)()
    y = y_ref[...] + d_ref[...] * u.astype(F32)
    y = 0.5 * y * (1.0 + jnp.tanh(math.sqrt(2.0 / math.pi) * (y + 0.044715 * (y * y * y))))
    gate = _sigmoid(_dot(y.astype(BF16), wg_ref[...]))
    out = (y * gate).astype(BF16)
    o_ref[...] = _dot_tn(perm_ref[...], out).astype(BF16)


def _front_kernel(h_ref, w_ref, pos_ref, inv_ref, qn_ref, kvn_ref, wq_ref, wqr_ref, wk_ref, wv_ref, e_ref, er_ref,
                  perm_ref, bm_ref, cm_ref, ar_ref, ai_ref, d_ref, wglu_ref,
                  zhg_ref, zg_ref, q_ref, k_ref, v_ref, ys5_ref,
                  xr_ref, xi_ref, buf_ref, y_ref, *, nb):
    @pl.when(pl.program_id(0) == 0)
    def _():
        xr_ref[...] = jnp.zeros_like(xr_ref)
        xi_ref[...] = jnp.zeros_like(xi_ref)

    cos, sin = _rope_tables(pos_ref, inv_ref)
    x = h_ref[...].astype(BF16)
    za = _dot(x, w_ref[:, 0:ZA_W])
    zkr = _dot(x, w_ref[:, ZA_W + ZHG_W + ZG_W:Z_W])

    def wide_chunk(o_ref, c, off):
        def run():
            o_ref[:, c:c + WIDE_CHUNK] = _dot(x, w_ref[:, off + c:off + c + WIDE_CHUNK]).astype(BF16)
        return run

    fillers = [wide_chunk(zhg_ref, c, ZA_W) for c in range(0, ZHG_W, WIDE_CHUNK)]
    fillers += [wide_chunk(zg_ref, c, ZA_W + ZHG_W) for c in range(0, ZG_W, WIDE_CHUNK)]
    for _ in range(FILL_BEFORE_MLA):
        fillers.pop(0)()
    c0 = S5_WIDTH
    _mla_prep(za[:, c0:c0 + MLA_Q_RANK], za[:, c0 + MLA_Q_RANK:c0 + MLA_Q_RANK + MLA_KV_RANK], zkr,
              cos, sin, qn_ref, kvn_ref, wq_ref, wqr_ref, wk_ref, wv_ref, e_ref, er_ref,
              q_ref, k_ref, v_ref)
    _s5_block(za[:, 0:S5_WIDTH].astype(BF16), perm_ref, bm_ref, cm_ref, ar_ref, ai_ref, d_ref, wglu_ref, ys5_ref,
              xr_ref, xi_ref, buf_ref, y_ref, nb, fillers)


def _front(h, w, pos, inv128, qn, kvn, wq, wqr, wk, wv, e, er, perm, bm, cm, ar, ai, dskip, wglu, nb):
    r, d = h.shape
    rb = T_BLK * nb
    hw = MLA_HEADS * HEAD_PAD
    vw = MLA_HEADS * MLA_V
    n_slab = S5_WIDTH // S5_SLAB
    row = lambda width: pl.BlockSpec((rb, width), lambda i: (i, 0))
    widths = (ZHG_W, ZG_W, hw, hw, vw, S5_WIDTH)
    return pl.pallas_call(
        functools.partial(_front_kernel, nb=nb),
        out_shape=[jax.ShapeDtypeStruct((r, wd), BF16) for wd in widths],
        grid=(r // rb,),
        in_specs=[
            row(d), _const_spec((d, Z_W)), row(1),
            _const_spec((1, HEAD_PAD)), _const_spec((1, MLA_Q_RANK)), _const_spec((1, MLA_KV_RANK)),
            _const_spec((MLA_Q_RANK, hw)), _const_spec((MLA_Q_RANK, hw)), _const_spec((MLA_KV_RANK, hw)),
            _const_spec((MLA_KV_RANK, vw)), _const_spec((ZKR_W, hw)), _const_spec((ZKR_W, hw)),
            _const_spec((rb, rb)),
            _const_spec((n_slab, S5_SLAB, 2 * S5_SLAB_STATE)),
            _const_spec((n_slab, 2 * S5_SLAB_STATE, S5_SLAB)),
            _const_spec((n_slab, 1, S5_SLAB_STATE)),
            _const_spec((n_slab, 1, S5_SLAB_STATE)),
            _const_spec((1, S5_WIDTH)),
            _const_spec((S5_WIDTH, S5_WIDTH)),
        ],
        out_specs=[row(wd) for wd in widths],
        scratch_shapes=[
            pltpu.VMEM((n_slab, nb, S5_SLAB_STATE), F32),
            pltpu.VMEM((n_slab, nb, S5_SLAB_STATE), F32),
            pltpu.VMEM((rb, 2 * S5_SLAB_STATE), F32),
            pltpu.VMEM((rb, S5_WIDTH), F32),
        ],
        compiler_params=_cparams(1, "arbitrary"),
        name="front",
    )(h, w, pos, inv128, qn, kvn, wq, wqr, wk, wv, e, er, perm, bm, cm, ar, ai, dskip, wglu)


def _attn_kernel(tab_ref, q_ref, k_ref, v_ref, o_ref, vt_ref, qt_ref, m_ref, l_ref, acc_ref, *, n_heads, tq, nq):
    bpq = tq // T_BLK
    vw = n_heads * MLA_V

    for j in range(nq):
        vb = _load_rows(v_ref, 1 + j * bpq, bpq, slice(0, vw))
        vt_ref[j] = vb.astype(F32).T.astype(BF16)
    v0t = v_ref[0].astype(F32).T.astype(BF16)

    for i in range(nq):
        qb = _load_rows(q_ref, 1 + i * bpq, bpq, slice(0, n_heads * HEAD_PAD))
        qt_ref[i] = qb.astype(F32).T.astype(BF16)

    def causal(st):
        r = lax.broadcasted_iota(jnp.int32, st.shape, 0)
        c = lax.broadcasted_iota(jnp.int32, st.shape, 1)
        return jnp.where(r <= c, st, MASK_VALUE * LOG2_E)

    heads = range(n_heads)
    qs = [slice(h * HEAD_PAD, (h + 1) * HEAD_PAD) for h in heads]
    vs = [slice(h * MLA_V, (h + 1) * MLA_V) for h in heads]
    k0 = [k_ref[0, :, c] for c in qs]

    outs = []
    for n in heads:
        st = causal(_dot_nt(k0[n], q_ref[0, :, qs[n]]))
        p = jnp.exp2(st - jnp.max(st, axis=0, keepdims=True))
        outs.append(_dot(v0t[vs[n], :], p.astype(BF16)) / jnp.sum(p, axis=0, keepdims=True))
    o_ref[0, :, :] = jnp.concatenate(outs, axis=0).T.astype(BF16)

    def key_tile(j, n):
        return _load_rows(k_ref, 1 + j * bpq, bpq, qs[n])

    def diag_scores(i):
        return tuple((_dot(key_tile(i, n), qt_ref[i, qs[n], :]), _dot(k0[n], qt_ref[i, qs[n], :])) for n in heads)

    def diag_stats(i, x):
        out = []
        for n in heads:
            st, sm = causal(x[n][0]), x[n][1]
            m = jnp.maximum(jnp.max(st, axis=0, keepdims=True), jnp.max(sm, axis=0, keepdims=True))
            p = jnp.exp2(st - m)
            pm = jnp.exp2(sm - m)
            m_ref[i, n] = m
            l_ref[i, n] = jnp.sum(p, axis=0, keepdims=True) + jnp.sum(pm, axis=0, keepdims=True)
            out.append((p.astype(BF16), pm.astype(BF16)))
        return tuple(out)

    def diag_values(i, y):
        for n in heads:
            acc_ref[i, vs[n], :] = _dot(vt_ref[i, vs[n], :], y[n][0]) + _dot(v0t[vs[n], :], y[n][1])

    _pipeline3(nq, diag_scores, diag_stats, diag_values)

    def off_scores(f):
        i, j = tab_ref[0, f], tab_ref[1, f]
        return tuple(_dot(key_tile(j, n), qt_ref[i, qs[n], :]) for n in heads)

    def off_stats(f, x):
        i = tab_ref[0, f]
        out = []
        for n in heads:
            m_old = m_ref[i, n]
            m = jnp.maximum(m_old, jnp.max(x[n], axis=0, keepdims=True))
            a = jnp.exp2(m_old - m)
            p = jnp.exp2(x[n] - m)
            m_ref[i, n] = m
            l_ref[i, n] = a * l_ref[i, n] + jnp.sum(p, axis=0, keepdims=True)
            out.append((a, p.astype(BF16)))
        return tuple(out)

    def off_values(f, y):
        i, j = tab_ref[0, f], tab_ref[1, f]
        for n in heads:
            acc_ref[i, vs[n], :] = y[n][0] * acc_ref[i, vs[n], :] + _dot(vt_ref[j, vs[n], :], y[n][1])

    _pipeline3(nq * (nq - 1) // 2, off_scores, off_stats, off_values)

    for i in range(nq):
        o = jnp.concatenate([acc_ref[i, vs[n], :] / l_ref[i, n] for n in heads], axis=0)
        _store_rows(o_ref, 1 + i * bpq, bpq, slice(0, vw), o.T.astype(BF16))


def _pipeline3(n, stage1, stage2, stage3):
    if n == 0:
        return
    x0 = stage1(0)
    x1 = stage1(1) if n > 1 else None
    y = stage2(0, x0)
    if n > 2:
        def body(t, carry):
            y_t, x_t1 = carry
            x_t2 = stage1(t + 2)
            y_t1 = stage2(t + 1, x_t1)
            stage3(t, y_t)
            return y_t1, x_t2

        y, x1 = lax.fori_loop(0, n - 2, body, (y, x1))
    if n > 1:
        stage3(n - 2, y)
        y = stage2(n - 1, x1)
    stage3(n - 1, y)


def _attention(q, k, v, nb, nblk):
    hps = ATT_HEADS_PER_STEP
    n_hp = MLA_HEADS // hps
    qw = hps * HEAD_PAD
    vw = hps * MLA_V
    s = (nblk - 1) * T_BLK
    tq = min(ATT_TQ, s)
    nq = s // tq
    col = lambda b, p: p
    pairs = [(i, j) for i in range(nq) for j in range(i)] or [(0, 0)]
    tab = jnp.asarray(np.array(pairs, np.int32).T)
    o = pl.pallas_call(
        functools.partial(_attn_kernel, n_heads=hps, tq=tq, nq=nq),
        out_shape=jax.ShapeDtypeStruct((nblk, nb, T_BLK, MLA_HEADS * MLA_V), BF16),
        grid=(nb, n_hp),
        in_specs=[pl.BlockSpec(memory_space=pltpu.SMEM),
                  _seq_spec(nblk, qw, col), _seq_spec(nblk, qw, col), _seq_spec(nblk, vw, col)],
        out_specs=_seq_spec(nblk, vw, col),
        scratch_shapes=[pltpu.VMEM((nq, vw, tq), BF16), pltpu.VMEM((nq, qw, tq), BF16),
                        pltpu.VMEM((nq, hps, 1, tq), F32), pltpu.VMEM((nq, hps, 1, tq), F32),
                        pltpu.VMEM((nq, vw, tq), F32)],
        compiler_params=_cparams(2),
        name="mla_attn",
    )(tab, q.reshape(nblk, nb, T_BLK, -1), k.reshape(nblk, nb, T_BLK, -1), v.reshape(nblk, nb, T_BLK, -1))
    return o.reshape(nblk * nb * T_BLK, MLA_HEADS * MLA_V)


def _time_major_perm(nb):
    p = np.zeros((T_BLK * nb, T_BLK * nb), np.float32)
    for b in range(nb):
        for t in range(T_BLK):
            p[t * nb + b, b * T_BLK + t] = 1.0
    return p


def _block_row_bcast(x, m, row):
    t, c = x.shape
    if m == t:
        return jnp.broadcast_to(x[row:row + 1, :], x.shape)
    x3 = x.reshape(t // m, m, c)
    return jnp.broadcast_to(x3[:, row:row + 1, :], x3.shape).reshape(t, c)


def _hgrn_chunk(blk0, n_blk, q_ref, f_ref, v_ref, g_ref, lb_ref, on_ref, tri_ref, lvl_ref, o_ref, st_ref):
    t = n_blk * T_BLK
    cs = slice(0, HG_QK)
    heads = [slice(h * HG_KEY, (h + 1) * HG_KEY) for h in range(HG_HEADS)]
    lb = lb_ref[...]
    q = _load_rows(q_ref, blk0, n_blk, cs).astype(F32)
    zf = _load_rows(f_ref, blk0, n_blk, cs).astype(F32)
    v = _load_rows(v_ref, blk0, n_blk, cs)
    g = _load_rows(g_ref, blk0, n_blk, cs).astype(F32)
    e = jnp.exp(-jnp.abs(zf))
    rcp = 1.0 / (1.0 + e)
    pos = zf >= 0.0
    sig_p = jnp.where(pos, rcp, e * rcp)
    sig_n = jnp.where(pos, e * rcp, rcp)
    f = lb + (1.0 - lb) * sig_p
    log_f = jnp.log(jnp.maximum(f, HG_F_MIN))
    k = (1.0 - lb) * sig_n
    hi = log_f.astype(BF16)
    lo = (log_f - hi.astype(F32)).astype(BF16)
    tri = tri_ref[0:t, 0:t]
    cum = _dot(tri, hi) + _dot(tri, lo)
    lvl = lvl_ref[0:t, 0:t]
    r_idx = lax.broadcasted_iota(jnp.int32, (t, HG_QK), 0)
    c8 = _block_row_bcast(cum, HG_BOTTOM, HG_BOTTOM // 2 - 1)
    qe = (q * jnp.exp(cum - c8)).astype(BF16)
    ke = (k * jnp.exp(c8 - cum)).astype(BF16)
    n_lvl = len(HG_LEVELS)
    scores = [jnp.where(lvl == n_lvl, _dot_nt(qe[:, hs], ke[:, hs]), 0.0) for hs in heads]
    for li, m in enumerate(HG_LEVELS):
        if m > t:
            continue
        half = m // 2
        cmid = _block_row_bcast(cum, m, half - 1)
        upper = (r_idx & (m - 1)) >= half
        ex = jnp.exp(jnp.where(upper, cum - cmid, cmid - cum))
        qe = jnp.where(upper, q * ex, 0.0).astype(BF16)
        ke = jnp.where(upper, 0.0, k * ex).astype(BF16)
        scores = [s + jnp.where(lvl == li, _dot_nt(qe[:, hs], ke[:, hs]), 0.0) for s, hs in zip(scores, heads)]
    qd = (q * jnp.exp(cum)).astype(BF16)
    last = cum[t - 1:t, :]
    kd = (k * jnp.exp(last - cum)).astype(BF16)
    dec = jnp.exp(last)
    outs = []
    for h, hs in enumerate(heads):
        st = st_ref[h]
        o = _dot(scores[h].astype(BF16), v[:, hs]) + _dot_nt(qd[:, hs], st.astype(BF16))
        st_ref[h] = st * dec[:, hs] + _dot_tn(v[:, hs], kd[:, hs])
        outs.append(o * lax.rsqrt(jnp.mean(o * o, axis=-1, keepdims=True) + 1e-6))
    o = jnp.concatenate(outs, axis=1) * on_ref[...]
    _store_rows(o_ref, blk0, n_blk, cs, (o * (g * _sigmoid(g))).astype(BF16))


def _hgrn_kernel(q_ref, f_ref, v_ref, g_ref, lb_ref, on_ref, tri_ref, lvl_ref, o_ref, st_ref, *, n_chunks):
    bpc = HG_CHUNK // T_BLK
    chunk = functools.partial(_hgrn_chunk, q_ref=q_ref, f_ref=f_ref, v_ref=v_ref, g_ref=g_ref, lb_ref=lb_ref,
                              on_ref=on_ref, tri_ref=tri_ref, lvl_ref=lvl_ref, o_ref=o_ref, st_ref=st_ref)
    st_ref[...] = jnp.zeros_like(st_ref)
    chunk(0, 1)

    def body(i, carry):
        chunk(1 + i * bpc, bpc)
        return carry

    lax.fori_loop(0, n_chunks, body, 0)


def _hgrn(zhg, lb, onorm, tri, lvl, nb, nblk):
    z4 = zhg.reshape(nblk, nb, T_BLK, ZHG_W)
    col = lambda j: _seq_spec(nblk, HG_QK, lambda b: j)
    o = pl.pallas_call(
        functools.partial(_hgrn_kernel, n_chunks=(nblk - 1) * T_BLK // HG_CHUNK),
        out_shape=jax.ShapeDtypeStruct((nblk, nb, T_BLK, HG_VW), BF16),
        grid=(nb,),
        in_specs=[col(0), col(1), col(2), col(3),
                  _const_spec((1, HG_QK)), _const_spec((1, HG_VW)),
                  _const_spec((HG_CHUNK, HG_CHUNK)), _const_spec((HG_CHUNK, HG_CHUNK))],
        out_specs=_seq_spec(nblk, HG_VW, lambda b: 0),
        scratch_shapes=[pltpu.VMEM((HG_HEADS, HG_VAL, HG_KEY), F32)],
        compiler_params=_cparams(1),
        name="hgrn2",
    )(z4, z4, z4, z4, lb, onorm, tri, lvl)
    return o.reshape(nblk * nb * T_BLK, HG_VW)


def _hgrn_level_matrix():
    r = np.arange(HG_CHUNK)[:, None]
    c = np.arange(HG_CHUNK)[None, :]
    lvl = np.zeros((HG_CHUNK, HG_CHUNK), np.int32)
    for li, m in enumerate(HG_LEVELS):
        lvl = np.where(r // m == c // m, li, lvl)
    lvl = np.where(r // HG_BOTTOM == c // HG_BOTTOM, len(HG_LEVELS), lvl)
    return np.where(c <= r, lvl, -1).astype(np.int32)


def _back_kernel(om_ref, os_ref, oh_ref, gm_ref, gs_ref, gh_ref, h_ref,
                 wm_ref, ws_ref, wh_ref, wo_ref, g1_ref, b1_ref,
                 wg_ref, wu_ref, wd_ref, g2_ref, b2_ref, o_ref, r1_ref, r2_ref, *, alpha):
    @pl.when(pl.program_id(0) == 0)
    def _():
        r1_ref[...] = jnp.zeros_like(r1_ref)
        r2_ref[...] = jnp.zeros_like(r2_ref)

    r1 = r1_ref[...]
    r2 = r2_ref[...]
    ym = _dot(om_ref[...], wm_ref[...])
    ys = _dot(os_ref[...], ws_ref[...])
    yh = _dot(oh_ref[...], wh_ref[...])
    o_ref[...] = _layer_norm(r2, g2_ref[...], b2_ref[...]).reshape(o_ref.shape)
    h1 = _layer_norm(r1, g1_ref[...], b1_ref[...])
    hb = h1.astype(BF16)
    a = _dot(hb, wg_ref[...])
    u = _dot(hb, wu_ref[...])
    mixed = _sigmoid(gm_ref[...].astype(F32)) * ym
    mixed += _sigmoid(gs_ref[...].astype(F32)) * ys
    mixed += _sigmoid(gh_ref[...].astype(F32)) * yh
    r1_ref[...] = alpha * h_ref[...] + _dot(mixed.astype(BF16), wo_ref[...])
    r2_ref[...] = alpha * h1 + _dot((a * _sigmoid(a) * u).astype(BF16), wd_ref[...])


def _back(om, os_, oh, zg, h, wm, ws, wh, wo, g1, b1, wg, wu, wd, g2, b2, nb, alpha, final):
    r, d = h.shape
    bw = om.shape[1]
    dff = wg.shape[1]
    rb = T_BLK * nb
    skip = 1 if final else 0
    n_blk = r // rb - skip
    lag = 2
    row = lambda width, j=0: pl.BlockSpec((rb, width), lambda i: (jnp.minimum(i, n_blk - 1) + skip, j))
    if final:
        out_shape = jax.ShapeDtypeStruct((nb, n_blk * T_BLK, d), F32)
        out_spec = pl.BlockSpec((nb, T_BLK, d), lambda i: (0, jnp.maximum(i - lag, 0), 0))
    else:
        out_shape = jax.ShapeDtypeStruct((r, d), F32)
        out_spec = pl.BlockSpec((rb, d), lambda i: (jnp.maximum(i - lag, 0), 0))
    return pl.pallas_call(
        functools.partial(_back_kernel, alpha=alpha),
        out_shape=out_shape,
        grid=(n_blk + lag,),
        in_specs=[row(bw), row(bw), row(bw), row(d, 0), row(d, 1), row(d, 2), row(d),
                  _const_spec((bw, d)), _const_spec((bw, d)), _const_spec((bw, d)), _const_spec((d, d)),
                  _const_spec((1, d)), _const_spec((1, d)),
                  _const_spec((d, dff)), _const_spec((d, dff)), _const_spec((dff, d)),
                  _const_spec((1, d)), _const_spec((1, d))],
        out_specs=out_spec,
        scratch_shapes=[pltpu.VMEM((rb, d), F32), pltpu.VMEM((rb, d), F32)],
        compiler_params=_cparams(1, "arbitrary"),
        name="merge_ffn",
    )(om, os_, oh, zg, zg, zg, h, wm, ws, wh, wo, g1, b1, wg, wu, wd, g2, b2)


def _permute_w_in(w):
    mla_in = MLA_Q_RANK + MLA_KV_RANK + MLA_ROPE
    s5_0, hg_0 = mla_in, mla_in + S5_WIDTH
    g_0 = hg_0 + 2 * HG_QK + 2 * HG_VW
    pad = jnp.zeros((w.shape[0], ZKR_W - MLA_ROPE), w.dtype)
    return jnp.concatenate([w[:, s5_0:hg_0], w[:, 0:MLA_Q_RANK + MLA_KV_RANK], w[:, hg_0:g_0], w[:, g_0:],
                            w[:, MLA_Q_RANK + MLA_KV_RANK:mla_in], pad], axis=1).astype(BF16)


def _rot_half(x):
    x1, x2 = jnp.split(x, 2, axis=-1)
    return jnp.concatenate([-x2, x1], axis=-1)


def _mla_weights(w_uq, w_ukv):
    rq, rkv = w_uq.shape[0], w_ukv.shape[0]
    zpad = HEAD_PAD - MLA_NOPE - MLA_ROPE
    wq = w_uq.reshape(rq, MLA_HEADS, MLA_NOPE + MLA_ROPE)
    q_nope, q_rope = wq[..., :MLA_NOPE], wq[..., MLA_NOPE:]
    zq = jnp.zeros((rq, MLA_HEADS, zpad), w_uq.dtype)
    wq_p = jnp.concatenate([q_nope, q_rope, zq], axis=-1).reshape(rq, -1)
    wq_r = jnp.concatenate([jnp.zeros_like(q_nope), _rot_half(q_rope), zq], axis=-1).reshape(rq, -1)
    wkv = w_ukv.reshape(rkv, MLA_HEADS, MLA_NOPE + MLA_V)
    zk = jnp.zeros((rkv, MLA_HEADS, HEAD_PAD - MLA_NOPE), w_ukv.dtype)
    wk_p = jnp.concatenate([wkv[..., :MLA_NOPE], zk], axis=-1).reshape(rkv, -1)
    wv = wkv[..., MLA_NOPE:].reshape(rkv, -1)
    eye = jnp.eye(ZKR_W, MLA_ROPE, dtype=F32)
    place = jnp.concatenate([jnp.zeros((ZKR_W, MLA_NOPE), F32), eye, jnp.zeros((ZKR_W, zpad), F32)], axis=-1)
    place_r = jnp.concatenate([jnp.zeros((ZKR_W, MLA_NOPE), F32), _rot_half(eye),
                               jnp.zeros((ZKR_W, zpad), F32)], axis=-1)
    e = jnp.tile(place, (1, MLA_HEADS))
    er = jnp.tile(place_r, (1, MLA_HEADS))
    return [a.astype(BF16) for a in (wq_p, wq_r, wk_p, wv, e, er)]


def _rope_inv():
    inv = ROPE_THETA ** (-(jnp.arange(0, MLA_ROPE, 2, dtype=F32) / MLA_ROPE))
    z = jnp.zeros((HEAD_PAD - MLA_NOPE - MLA_ROPE,), F32)
    return jnp.concatenate([jnp.zeros((MLA_NOPE,), F32), inv, inv, z])[None, :]


def _s5_params(lam_re, lam_im, log_dt, b_re, b_im, c_re, c_im):
    lr = jnp.minimum(lam_re.astype(F32), -1e-4)
    li = lam_im.astype(F32)
    dt = jnp.exp(log_dt.astype(F32))[:, None]
    mag = jnp.exp(lr * dt)
    ab_r = mag * jnp.cos(li * dt)
    ab_i = mag * jnp.sin(li * dt)
    den = lr * lr + li * li
    nr = ab_r - 1.0
    coef_r = ((nr * lr + ab_i * li) / den)[..., None]
    coef_i = ((ab_i * lr - nr * li) / den)[..., None]
    bb_r = coef_r * b_re.astype(F32) - coef_i * b_im.astype(F32)
    bb_i = coef_r * b_im.astype(F32) + coef_i * b_re.astype(F32)
    n_slab = S5_WIDTH // S5_SLAB
    gps = S5_SLAB // S5_GROUP
    eye = jnp.eye(gps, dtype=F32)

    def in_mat(bb):
        b4 = bb.reshape(n_slab, gps, S5_STATE, S5_GROUP)
        return jnp.einsum('jgnc,gh->jgchn', b4, eye).reshape(n_slab, S5_SLAB, S5_SLAB_STATE)

    def out_mat(cc):
        c4 = cc.astype(F32).reshape(n_slab, gps, S5_GROUP, S5_STATE)
        return jnp.einsum('jgcn,gh->jgnhc', c4, eye).reshape(n_slab, S5_SLAB_STATE, S5_SLAB)

    bm = jnp.concatenate([in_mat(bb_r), in_mat(bb_i)], axis=2).astype(BF16)
    cm = jnp.concatenate([out_mat(c_re), -out_mat(c_im)], axis=1).astype(BF16)
    ar = ab_r.reshape(n_slab, 1, S5_SLAB_STATE)
    ai = ab_i.reshape(n_slab, 1, S5_SLAB_STATE)
    return bm, cm, ar, ai


def kernel(x, positions, meta_tokens, ln_in_g, ln_in_b, w_in, mla_q_norm, mla_w_uq, mla_kv_norm, mla_w_ukv,
           s5_lam_re, s5_lam_im, s5_log_dt, s5_b_re, s5_b_im, s5_c_re, s5_c_im, s5_d, s5_w_glu,
           hg_lb_logits, hg_out_norm, w_br_mla, w_br_s5, w_br_hg, w_out, ln1_g, ln1_b,
           w_ffn_gate, w_ffn_up, w_ffn_down, ln2_g, ln2_b):
    nb, s, d = x.shape
    depth = w_in.shape[0]
    nblk = (s + N_META) // T_BLK
    rb = T_BLK * nb
    n_rows = nblk * rb
    alpha = (2 * depth) ** 0.25
    row2 = lambda a: a.astype(F32)[None, :]

    h = _ln_in(x, meta_tokens.astype(x.dtype), row2(ln_in_g), row2(ln_in_b)).reshape(n_rows, d)

    meta_pos = jnp.broadcast_to(jnp.arange(N_META, dtype=jnp.int32)[None, :], (nb, N_META))
    pos = jnp.concatenate([meta_pos, positions.astype(jnp.int32) + N_META], axis=1)
    pos = pos.reshape(nb, nblk, T_BLK).transpose(1, 0, 2).reshape(n_rows, 1)
    inv128 = _rope_inv()
    p_lb = jax.nn.softmax(hg_lb_logits.astype(F32), axis=0)
    lower_bounds = jnp.cumsum(p_lb, axis=0) - p_lb[0]
    tri = jnp.asarray(np.tril(np.ones((HG_CHUNK, HG_CHUNK), np.float32)), BF16)
    lvl = jnp.asarray(_hgrn_level_matrix())
    perm = jnp.asarray(_time_major_perm(nb), BF16)

    for li in range(depth):
        s5_mats = _s5_params(s5_lam_re[li], s5_lam_im[li], s5_log_dt[li], s5_b_re[li], s5_b_im[li],
                             s5_c_re[li], s5_c_im[li])
        zhg, zg, q, k, v, y_s5 = _front(
            h, _permute_w_in(w_in[li]), pos, inv128, row2(mla_q_norm[li]), row2(mla_kv_norm[li]),
            *_mla_weights(mla_w_uq[li], mla_w_ukv[li]), perm, *s5_mats, row2(s5_d[li]),
            s5_w_glu[li].astype(BF16), nb)
        o_mla = _attention(q, k, v, nb, nblk)
        o_hg = _hgrn(zhg, lower_bounds[li][None, :], row2(hg_out_norm[li]), tri, lvl, nb, nblk)
        h = _back(o_mla, y_s5, o_hg, zg, h, w_br_mla[li].astype(BF16), w_br_s5[li].astype(BF16),
                  w_br_hg[li].astype(BF16), w_out[li].astype(BF16), row2(ln1_g[li]), row2(ln1_b[li]),
                  w_ffn_gate[li].astype(BF16), w_ffn_up[li].astype(BF16), w_ffn_down[li].astype(BF16),
                  row2(ln2_g[li]), row2(ln2_b[li]), nb, alpha, final=li == depth - 1)
    return h
```

```python
import functools
import math

import jax
import jax.numpy as jnp
import numpy as np
from jax import lax
from jax.experimental import pallas as pl
from jax.experimental.pallas import tpu as pltpu

F32 = jnp.float32
BF16 = jnp.bfloat16

N_META = 16
MLA_HEADS = 8
MLA_NOPE = 64
MLA_ROPE = 32
MLA_V = 64
MLA_Q_RANK = 256
MLA_KV_RANK = 256
ROPE_THETA = 10000.0
MASK_VALUE = -1e9
LOG2_E = math.log2(math.e)
HEAD_PAD = 128
S5_WIDTH = 512
S5_GROUP = 16
S5_GROUPS = S5_WIDTH // S5_GROUP
S5_STATE = 64
S5_SLAB = 128
S5_SLAB_STATE = (S5_SLAB // S5_GROUP) * S5_STATE
HG_HEADS = 4
HG_KEY = 128
HG_VAL = 128
HG_QK = HG_HEADS * HG_KEY
HG_VW = HG_HEADS * HG_VAL
HG_F_MIN = 1e-6
HG_CHUNK = 128
HG_LEVELS = (128, 64, 32, 16)
HG_BOTTOM = 8
N_BRANCH = 3
T_BLK = 16
ATT_TQ = 256
ATT_HEADS_PER_STEP = 4
VMEM_LIMIT = 56 * 1024 * 1024

ZA_W = 1024
ZHG_W = 2048
ZG_W = 3072
ZKR_W = 128
Z_W = ZA_W + ZHG_W + ZG_W + ZKR_W
WIDE_CHUNK = 512
FILL_BEFORE_MLA = 3


def _cparams(n_grid, sem="parallel"):
    return pltpu.CompilerParams(dimension_semantics=(sem,) * n_grid, vmem_limit_bytes=VMEM_LIMIT)


def _const_spec(shape):
    nd = len(shape)
    return pl.BlockSpec(shape, lambda *_: (0,) * nd, pipeline_mode=pl.Buffered(1))


def _seq_spec(nblk, width, col):
    return pl.BlockSpec((nblk, None, T_BLK, width), lambda b, *g: (0, b, 0, col(b, *g)))


def _dot(a, b):
    return jnp.dot(a, b, preferred_element_type=F32)


def _dot_nt(a, b):
    return lax.dot_general(a, b, (((1,), (1,)), ((), ())), preferred_element_type=F32)


def _dot_tn(a, b):
    return lax.dot_general(a, b, (((0,), (0,)), ((), ())), preferred_element_type=F32)


def _sigmoid(x):
    return 1.0 / (1.0 + jnp.exp(-x))


def _layer_norm(x, g, b, eps=1e-5):
    mu = jnp.mean(x, axis=-1, keepdims=True)
    xc = x - mu
    var = jnp.mean(xc * xc, axis=-1, keepdims=True)
    return xc * lax.rsqrt(var + eps) * g + b


def _rms_norm(x, g, eps=1e-6):
    return x * lax.rsqrt(jnp.mean(x * x, axis=-1, keepdims=True) + eps) * g


def _load_rows(ref, blk0, n_blk, cs):
    x = ref[pl.ds(blk0, n_blk), :, cs]
    return x.reshape(n_blk * T_BLK, x.shape[-1])


def _store_rows(ref, blk0, n_blk, cs, x):
    ref[pl.ds(blk0, n_blk), :, cs] = x.reshape(n_blk, T_BLK, x.shape[-1])


def _rope_tables(pos_ref, inv_ref, place_ref, base_ref):
    ang = inv_ref[...] * pos_ref[...].astype(F32)

    def place(t):
        hi = t.astype(BF16)
        lo = (t - hi.astype(F32)).astype(BF16)
        return _dot_tn(hi, place_ref[...]) + _dot_tn(lo, place_ref[...])

    return place(jnp.cos(ang)) + base_ref[...], place(jnp.sin(ang))


def _mla_prep(cq, ckv, kr, cos, sin, qn_ref, kvn_ref, wq_ref, wqr_ref, wk_ref, wv_ref, e_ref, er_ref,
              q_ref, k_ref, v_ref):
    scale = (MLA_NOPE + MLA_ROPE) ** -0.5 * LOG2_E
    cqn = _rms_norm(cq, qn_ref[...]).astype(BF16)
    ckvn = _rms_norm(ckv, kvn_ref[...]).astype(BF16)
    kr = kr.astype(BF16)
    cos_q = cos * scale
    sin_q = sin * scale
    for h in range(MLA_HEADS):
        cs = slice(h * HEAD_PAD, (h + 1) * HEAD_PAD)
        qf = _dot(cqn, wq_ref[:, cs])
        qr = _dot(cqn, wqr_ref[:, cs])
        q_ref[:, cs] = (qf * cos_q + qr * sin_q).astype(BF16)
        kf = _dot(ckvn, wk_ref[:, cs]) + _dot(kr, e_ref[:, cs])
        krot = _dot(kr, er_ref[:, cs])
        k_ref[:, cs] = (kf * cos + krot * sin).astype(BF16)
    v_ref[...] = _dot(ckvn, wv_ref[...]).astype(BF16)


def _s5_block(u, perm_ref, bm_ref, cm_ref, ar_ref, ai_ref, d_ref, wg_ref, o_ref,
              xr_ref, xi_ref, buf_ref, y_ref, nb, fillers):
    n_slab = S5_WIDTH // S5_SLAB
    ns = S5_SLAB_STATE
    u = _dot(perm_ref[...], u).astype(BF16)
    for j in range(n_slab):
        buf_ref[...] = _dot(u[:, j * S5_SLAB:(j + 1) * S5_SLAB], bm_ref[j])
        if fillers:
            fillers.pop(0)()
        ar = jnp.broadcast_to(ar_ref[j], (nb, ns))
        ai = jnp.broadcast_to(ai_ref[j], (nb, ns))
        xr = xr_ref[j]
        xi = xi_ref[j]
        for t in range(T_BLK):
            rows = slice(t * nb, (t + 1) * nb)
            nr = ar * xr - ai * xi + buf_ref[rows, 0:ns]
            ni = ar * xi + ai * xr + buf_ref[rows, ns:2 * ns]
            buf_ref[rows, 0:ns] = nr
            buf_ref[rows, ns:2 * ns] = ni
            xr, xi = nr, ni
        xr_ref[j] = xr
        xi_ref[j] = xi
        y_ref[:, j * S5_SLAB:(j + 1) * S5_SLAB] = _dot(buf_ref[...].astype(BF16), cm_ref[j])

    while fillers:
        fillers.pop(0)()
    y = y_ref[...] + d_ref[...] * u.astype(F32)
    y = 0.5 * y * (1.0 + jnp.tanh(math.sqrt(2.0 / math.pi) * (y + 0.044715 * (y * y * y))))
    gate = _sigmoid(_dot(y.astype(BF16), wg_ref[...]))
    out = (y * gate).astype(BF16)
    o_ref[...] = _dot_tn(perm_ref[...], out).astype(BF16)


def _front_kernel(*refs, nb, ln_in):
    if ln_in:
        x_ref, meta_ref, lng_ref, lnb_ref, *refs = refs
    else:
        h_ref, *refs = refs
    (w_ref, pos_ref, inv_ref, place_ref, base_ref, qn_ref, kvn_ref, wq_ref, wqr_ref, wk_ref, wv_ref, e_ref, er_ref,
     perm_ref, bm_ref, cm_ref, ar_ref, ai_ref, d_ref, wglu_ref, *refs) = refs
    if ln_in:
        hout_ref, *refs = refs
    zhg_ref, zg_ref, q_ref, k_ref, v_ref, ys5_ref, xr_ref, xi_ref, buf_ref, y_ref = refs

    @pl.when(pl.program_id(0) == 0)
    def _():
        xr_ref[...] = jnp.zeros_like(xr_ref)
        xi_ref[...] = jnp.zeros_like(xi_ref)

    cos, sin = _rope_tables(pos_ref, inv_ref, place_ref, base_ref)
    if ln_in:
        hx = _layer_norm(x_ref[...], lng_ref[...], lnb_ref[...])
        hm = _layer_norm(meta_ref[...], lng_ref[...], lnb_ref[...])
        h = jnp.where(pl.program_id(0) == 0, jnp.broadcast_to(hm[None], hx.shape), hx)
        h = h.reshape(hout_ref.shape)
        hout_ref[...] = h
    else:
        h = h_ref[...]
    x = h.astype(BF16)
    za = _dot(x, w_ref[:, 0:ZA_W])
    zkr = _dot(x, w_ref[:, ZA_W + ZHG_W + ZG_W:Z_W])

    def wide_chunk(o_ref, c, off):
        def run():
            o_ref[:, c:c + WIDE_CHUNK] = _dot(x, w_ref[:, off + c:off + c + WIDE_CHUNK]).astype(BF16)
        return run

    fillers = [wide_chunk(zhg_ref, c, ZA_W) for c in range(0, ZHG_W, WIDE_CHUNK)]
    fillers += [wide_chunk(zg_ref, c, ZA_W + ZHG_W) for c in range(0, ZG_W, WIDE_CHUNK)]
    for _ in range(FILL_BEFORE_MLA):
        fillers.pop(0)()
    c0 = S5_WIDTH
    _mla_prep(za[:, c0:c0 + MLA_Q_RANK], za[:, c0 + MLA_Q_RANK:c0 + MLA_Q_RANK + MLA_KV_RANK], zkr,
              cos, sin, qn_ref, kvn_ref, wq_ref, wqr_ref, wk_ref, wv_ref, e_ref, er_ref,
              q_ref, k_ref, v_ref)
    _s5_block(za[:, 0:S5_WIDTH].astype(BF16), perm_ref, bm_ref, cm_ref, ar_ref, ai_ref, d_ref, wglu_ref, ys5_ref,
              xr_ref, xi_ref, buf_ref, y_ref, nb, fillers)


def _front(src, w, pos, rope, qn, kvn, wq, wqr, wk, wv, e, er, perm, bm, cm, ar, ai, dskip, wglu, nb):
    ln_in = isinstance(src, tuple)
    rb = T_BLK * nb
    if ln_in:
        x, meta, lng, lnb = src
        d = x.shape[-1]
        r = (x.shape[1] + N_META) * nb
        src_specs = [pl.BlockSpec((nb, T_BLK, d), lambda i: (0, jnp.maximum(i - 1, 0), 0)),
                     _const_spec((N_META, d)), _const_spec((1, d)), _const_spec((1, d))]
    else:
        src = (src,)
        r, d = src[0].shape
        src_specs = [pl.BlockSpec((rb, d), lambda i: (i, 0))]
    hw = MLA_HEADS * HEAD_PAD
    vw = MLA_HEADS * MLA_V
    n_slab = S5_WIDTH // S5_SLAB
    row = lambda width: pl.BlockSpec((rb, width), lambda i: (i, 0))
    widths = (ZHG_W, ZG_W, hw, hw, vw, S5_WIDTH)
    out_shape = [jax.ShapeDtypeStruct((r, wd), BF16) for wd in widths]
    out_specs = [row(wd) for wd in widths]
    if ln_in:
        out_shape.insert(0, jax.ShapeDtypeStruct((r, d), F32))
        out_specs.insert(0, row(d))
    return pl.pallas_call(
        functools.partial(_front_kernel, nb=nb, ln_in=ln_in),
        out_shape=out_shape,
        grid=(r // rb,),
        in_specs=src_specs + [
            _const_spec((d, Z_W)), pl.BlockSpec((None, 1, rb), lambda i: (i, 0, 0)),
            _const_spec((MLA_ROPE // 2, 1)), _const_spec((MLA_ROPE // 2, HEAD_PAD)), _const_spec((1, HEAD_PAD)),
            _const_spec((1, MLA_Q_RANK)), _const_spec((1, MLA_KV_RANK)),
            _const_spec((MLA_Q_RANK, hw)), _const_spec((MLA_Q_RANK, hw)), _const_spec((MLA_KV_RANK, hw)),
            _const_spec((MLA_KV_RANK, vw)), _const_spec((ZKR_W, hw)), _const_spec((ZKR_W, hw)),
            _const_spec((rb, rb)),
            _const_spec((n_slab, S5_SLAB, 2 * S5_SLAB_STATE)),
            _const_spec((n_slab, 2 * S5_SLAB_STATE, S5_SLAB)),
            _const_spec((n_slab, 1, S5_SLAB_STATE)),
            _const_spec((n_slab, 1, S5_SLAB_STATE)),
            _const_spec((1, S5_WIDTH)),
            _const_spec((S5_WIDTH, S5_WIDTH)),
        ],
        out_specs=out_specs,
        scratch_shapes=[
            pltpu.VMEM((n_slab, nb, S5_SLAB_STATE), F32),
            pltpu.VMEM((n_slab, nb, S5_SLAB_STATE), F32),
            pltpu.VMEM((rb, 2 * S5_SLAB_STATE), F32),
            pltpu.VMEM((rb, S5_WIDTH), F32),
        ],
        compiler_params=_cparams(1, "arbitrary"),
        name="front",
    )(*src, w, pos, *rope, qn, kvn, wq, wqr, wk, wv, e, er, perm, bm, cm, ar, ai, dskip, wglu)


def _attn_kernel(tab_ref, q_ref, k_ref, v_ref, o_ref, vt_ref, qt_ref, m_ref, l_ref, acc_ref, *, n_heads, tq, nq):
    bpq = tq // T_BLK
    vw = n_heads * MLA_V

    for j in range(nq):
        vb = _load_rows(v_ref, 1 + j * bpq, bpq, slice(0, vw))
        vt_ref[j] = vb.astype(F32).T.astype(BF16)
    v0t = v_ref[0].astype(F32).T.astype(BF16)

    for i in range(nq):
        qb = _load_rows(q_ref, 1 + i * bpq, bpq, slice(0, n_heads * HEAD_PAD))
        qt_ref[i] = qb.astype(F32).T.astype(BF16)

    def causal(st):
        r = lax.broadcasted_iota(jnp.int32, st.shape, 0)
        c = lax.broadcasted_iota(jnp.int32, st.shape, 1)
        return jnp.where(r <= c, st, MASK_VALUE * LOG2_E)

    heads = range(n_heads)
    qs = [slice(h * HEAD_PAD, (h + 1) * HEAD_PAD) for h in heads]
    vs = [slice(h * MLA_V, (h + 1) * MLA_V) for h in heads]
    k0 = [k_ref[0, :, c] for c in qs]

    outs = []
    for n in heads:
        st = causal(_dot_nt(k0[n], q_ref[0, :, qs[n]]))
        p = jnp.exp2(st - jnp.max(st, axis=0, keepdims=True))
        outs.append(_dot(v0t[vs[n], :], p.astype(BF16)) / jnp.sum(p, axis=0, keepdims=True))
    o_ref[0, :, :] = jnp.concatenate(outs, axis=0).T.astype(BF16)

    def key_tile(j, n):
        return _load_rows(k_ref, 1 + j * bpq, bpq, qs[n])

    def diag_scores(i):
        return tuple((_dot(key_tile(i, n), qt_ref[i, qs[n], :]), _dot(k0[n], qt_ref[i, qs[n], :])) for n in heads)

    def diag_stats(i, x):
        out = []
        for n in heads:
            st, sm = causal(x[n][0]), x[n][1]
            m = jnp.maximum(jnp.max(st, axis=0, keepdims=True), jnp.max(sm, axis=0, keepdims=True))
            p = jnp.exp2(st - m)
            pm = jnp.exp2(sm - m)
            m_ref[i, n] = m
            l_ref[i, n] = jnp.sum(p, axis=0, keepdims=True) + jnp.sum(pm, axis=0, keepdims=True)
            out.append((p.astype(BF16), pm.astype(BF16)))
        return tuple(out)

    def diag_values(i, y):
        for n in heads:
            acc_ref[i, vs[n], :] = _dot(vt_ref[i, vs[n], :], y[n][0]) + _dot(v0t[vs[n], :], y[n][1])

    _pipeline3(nq, diag_scores, diag_stats, diag_values)

    def off_scores(f):
        i, j = tab_ref[0, f], tab_ref[1, f]
        return tuple(_dot(key_tile(j, n), qt_ref[i, qs[n], :]) for n in heads)

    def off_stats(f, x):
        i = tab_ref[0, f]
        out = []
        for n in heads:
            m_old = m_ref[i, n]
            m = jnp.maximum(m_old, jnp.max(x[n], axis=0, keepdims=True))
            a = jnp.exp2(m_old - m)
            p = jnp.exp2(x[n] - m)
            m_ref[i, n] = m
            l_ref[i, n] = a * l_ref[i, n] + jnp.sum(p, axis=0, keepdims=True)
            out.append((a, p.astype(BF16)))
        return tuple(out)

    def off_values(f, y):
        i, j = tab_ref[0, f], tab_ref[1, f]
        for n in heads:
            acc_ref[i, vs[n], :] = y[n][0] * acc_ref[i, vs[n], :] + _dot(vt_ref[j, vs[n], :], y[n][1])

    _pipeline3(nq * (nq - 1) // 2, off_scores, off_stats, off_values)

    for i in range(nq):
        o = jnp.concatenate([acc_ref[i, vs[n], :] / l_ref[i, n] for n in heads], axis=0)
        _store_rows(o_ref, 1 + i * bpq, bpq, slice(0, vw), o.T.astype(BF16))


def _pipeline3(n, stage1, stage2, stage3):
    if n == 0:
        return
    x0 = stage1(0)
    x1 = stage1(1) if n > 1 else None
    y = stage2(0, x0)
    if n > 2:
        def body(t, carry):
            y_t, x_t1 = carry
            x_t2 = stage1(t + 2)
            y_t1 = stage2(t + 1, x_t1)
            stage3(t, y_t)
            return y_t1, x_t2

        y, x1 = lax.fori_loop(0, n - 2, body, (y, x1))
    if n > 1:
        stage3(n - 2, y)
        y = stage2(n - 1, x1)
    stage3(n - 1, y)


def _attention(q, k, v, nb, nblk):
    hps = ATT_HEADS_PER_STEP
    n_hp = MLA_HEADS // hps
    qw = hps * HEAD_PAD
    vw = hps * MLA_V
    s = (nblk - 1) * T_BLK
    tq = min(ATT_TQ, s)
    nq = s // tq
    col = lambda b, p: p
    pairs = [(i, j) for i in range(nq) for j in range(i)] or [(0, 0)]
    tab = jnp.asarray(np.array(pairs, np.int32).T)
    o = pl.pallas_call(
        functools.partial(_attn_kernel, n_heads=hps, tq=tq, nq=nq),
        out_shape=jax.ShapeDtypeStruct((nblk, nb, T_BLK, MLA_HEADS * MLA_V), BF16),
        grid=(nb, n_hp),
        in_specs=[pl.BlockSpec(memory_space=pltpu.SMEM),
                  _seq_spec(nblk, qw, col), _seq_spec(nblk, qw, col), _seq_spec(nblk, vw, col)],
        out_specs=_seq_spec(nblk, vw, col),
        scratch_shapes=[pltpu.VMEM((nq, vw, tq), BF16), pltpu.VMEM((nq, qw, tq), BF16),
                        pltpu.VMEM((nq, hps, 1, tq), F32), pltpu.VMEM((nq, hps, 1, tq), F32),
                        pltpu.VMEM((nq, vw, tq), F32)],
        compiler_params=_cparams(2),
        name="mla_attn",
    )(tab, q.reshape(nblk, nb, T_BLK, -1), k.reshape(nblk, nb, T_BLK, -1), v.reshape(nblk, nb, T_BLK, -1))
    return o.reshape(nblk * nb * T_BLK, MLA_HEADS * MLA_V)


def _time_major_perm(nb):
    p = np.zeros((T_BLK * nb, T_BLK * nb), np.float32)
    for b in range(nb):
        for t in range(T_BLK):
            p[t * nb + b, b * T_BLK + t] = 1.0
    return p


def _block_row_bcast(x, m, row):
    t, c = x.shape
    if m == t:
        return jnp.broadcast_to(x[row:row + 1, :], x.shape)
    x3 = x.reshape(t // m, m, c)
    return jnp.broadcast_to(x3[:, row:row + 1, :], x3.shape).reshape(t, c)


def _hgrn_chunk(blk0, n_blk, q_ref, f_ref, v_ref, g_ref, lb_ref, on_ref, tri_ref, lvl_ref, o_ref, st_ref):
    t = n_blk * T_BLK
    cs = slice(0, HG_QK)
    heads = [slice(h * HG_KEY, (h + 1) * HG_KEY) for h in range(HG_HEADS)]
    lb = lb_ref[...]
    q = _load_rows(q_ref, blk0, n_blk, cs).astype(F32)
    zf = _load_rows(f_ref, blk0, n_blk, cs).astype(F32)
    v = _load_rows(v_ref, blk0, n_blk, cs)
    g = _load_rows(g_ref, blk0, n_blk, cs).astype(F32)
    e = jnp.exp(-jnp.abs(zf))
    rcp = 1.0 / (1.0 + e)
    pos = zf >= 0.0
    sig_p = jnp.where(pos, rcp, e * rcp)
    sig_n = jnp.where(pos, e * rcp, rcp)
    f = lb + (1.0 - lb) * sig_p
    log_f = jnp.log(jnp.maximum(f, HG_F_MIN))
    k = (1.0 - lb) * sig_n
    hi = log_f.astype(BF16)
    lo = (log_f - hi.astype(F32)).astype(BF16)
    tri = tri_ref[0:t, 0:t]
    cum = _dot(tri, hi) + _dot(tri, lo)
    lvl = lvl_ref[0:t, 0:t]
    r_idx = lax.broadcasted_iota(jnp.int32, (t, HG_QK), 0)
    c8 = _block_row_bcast(cum, HG_BOTTOM, HG_BOTTOM // 2 - 1)
    qe = (q * jnp.exp(cum - c8)).astype(BF16)
    ke = (k * jnp.exp(c8 - cum)).astype(BF16)
    n_lvl = len(HG_LEVELS)
    scores = [jnp.where(lvl == n_lvl, _dot_nt(qe[:, hs], ke[:, hs]), 0.0) for hs in heads]
    for li, m in enumerate(HG_LEVELS):
        if m > t:
            continue
        half = m // 2
        cmid = _block_row_bcast(cum, m, half - 1)
        upper = (r_idx & (m - 1)) >= half
        ex = jnp.exp(jnp.where(upper, cum - cmid, cmid - cum))
        qe = jnp.where(upper, q * ex, 0.0).astype(BF16)
        ke = jnp.where(upper, 0.0, k * ex).astype(BF16)
        scores = [s + jnp.where(lvl == li, _dot_nt(qe[:, hs], ke[:, hs]), 0.0) for s, hs in zip(scores, heads)]
    qd = (q * jnp.exp(cum)).astype(BF16)
    last = cum[t - 1:t, :]
    kd = (k * jnp.exp(last - cum)).astype(BF16)
    dec = jnp.exp(last)
    outs = []
    for h, hs in enumerate(heads):
        st = st_ref[h]
        o = _dot(scores[h].astype(BF16), v[:, hs]) + _dot_nt(qd[:, hs], st.astype(BF16))
        st_ref[h] = st * dec[:, hs] + _dot_tn(v[:, hs], kd[:, hs])
        outs.append(o * lax.rsqrt(jnp.mean(o * o, axis=-1, keepdims=True) + 1e-6))
    o = jnp.concatenate(outs, axis=1) * on_ref[...]
    _store_rows(o_ref, blk0, n_blk, cs, (o * (g * _sigmoid(g))).astype(BF16))


def _hgrn_kernel(q_ref, f_ref, v_ref, g_ref, lb_ref, on_ref, tri_ref, lvl_ref, o_ref, st_ref, *, n_chunks):
    bpc = HG_CHUNK // T_BLK
    chunk = functools.partial(_hgrn_chunk, q_ref=q_ref, f_ref=f_ref, v_ref=v_ref, g_ref=g_ref, lb_ref=lb_ref,
                              on_ref=on_ref, tri_ref=tri_ref, lvl_ref=lvl_ref, o_ref=o_ref, st_ref=st_ref)
    st_ref[...] = jnp.zeros_like(st_ref)
    chunk(0, 1)

    def body(i, carry):
        chunk(1 + i * bpc, bpc)
        return carry

    lax.fori_loop(0, n_chunks, body, 0)


def _hgrn(zhg, lb, onorm, tri, lvl, nb, nblk):
    z4 = zhg.reshape(nblk, nb, T_BLK, ZHG_W)
    col = lambda j: _seq_spec(nblk, HG_QK, lambda b: j)
    o = pl.pallas_call(
        functools.partial(_hgrn_kernel, n_chunks=(nblk - 1) * T_BLK // HG_CHUNK),
        out_shape=jax.ShapeDtypeStruct((nblk, nb, T_BLK, HG_VW), BF16),
        grid=(nb,),
        in_specs=[col(0), col(1), col(2), col(3),
                  _const_spec((1, HG_QK)), _const_spec((1, HG_VW)),
                  _const_spec((HG_CHUNK, HG_CHUNK)), _const_spec((HG_CHUNK, HG_CHUNK))],
        out_specs=_seq_spec(nblk, HG_VW, lambda b: 0),
        scratch_shapes=[pltpu.VMEM((HG_HEADS, HG_VAL, HG_KEY), F32)],
        compiler_params=_cparams(1),
        name="hgrn2",
    )(z4, z4, z4, z4, lb, onorm, tri, lvl)
    return o.reshape(nblk * nb * T_BLK, HG_VW)


def _hgrn_level_matrix():
    r = np.arange(HG_CHUNK)[:, None]
    c = np.arange(HG_CHUNK)[None, :]
    lvl = np.zeros((HG_CHUNK, HG_CHUNK), np.int32)
    for li, m in enumerate(HG_LEVELS):
        lvl = np.where(r // m == c // m, li, lvl)
    lvl = np.where(r // HG_BOTTOM == c // HG_BOTTOM, len(HG_LEVELS), lvl)
    return np.where(c <= r, lvl, -1).astype(np.int32)


def _back_kernel(om_ref, os_ref, oh_ref, gm_ref, gs_ref, gh_ref, h_ref,
                 wm_ref, ws_ref, wh_ref, wo_ref, g1_ref, b1_ref,
                 wg_ref, wu_ref, wd_ref, g2_ref, b2_ref, o_ref, r1_ref, r2_ref, *, alpha):
    @pl.when(pl.program_id(0) == 0)
    def _():
        r1_ref[...] = jnp.zeros_like(r1_ref)
        r2_ref[...] = jnp.zeros_like(r2_ref)

    r1 = r1_ref[...]
    r2 = r2_ref[...]
    ym = _dot(om_ref[...], wm_ref[...])
    ys = _dot(os_ref[...], ws_ref[...])
    yh = _dot(oh_ref[...], wh_ref[...])
    o_ref[...] = _layer_norm(r2, g2_ref[...], b2_ref[...]).reshape(o_ref.shape)
    h1 = _layer_norm(r1, g1_ref[...], b1_ref[...])
    hb = h1.astype(BF16)
    a = _dot(hb, wg_ref[...])
    u = _dot(hb, wu_ref[...])
    mixed = _sigmoid(gm_ref[...].astype(F32)) * ym
    mixed += _sigmoid(gs_ref[...].astype(F32)) * ys
    mixed += _sigmoid(gh_ref[...].astype(F32)) * yh
    r1_ref[...] = alpha * h_ref[...] + _dot(mixed.astype(BF16), wo_ref[...])
    r2_ref[...] = alpha * h1 + _dot((a * _sigmoid(a) * u).astype(BF16), wd_ref[...])


def _back(om, os_, oh, zg, h, wm, ws, wh, wo, g1, b1, wg, wu, wd, g2, b2, nb, alpha, final):
    r, d = h.shape
    bw = om.shape[1]
    dff = wg.shape[1]
    rb = T_BLK * nb
    skip = 1 if final else 0
    n_blk = r // rb - skip
    lag = 2
    row = lambda width, j=0: pl.BlockSpec((rb, width), lambda i: (jnp.minimum(i, n_blk - 1) + skip, j))
    if final:
        out_shape = jax.ShapeDtypeStruct((nb, n_blk * T_BLK, d), F32)
        out_spec = pl.BlockSpec((nb, T_BLK, d), lambda i: (0, jnp.maximum(i - lag, 0), 0))
    else:
        out_shape = jax.ShapeDtypeStruct((r, d), F32)
        out_spec = pl.BlockSpec((rb, d), lambda i: (jnp.maximum(i - lag, 0), 0))
    return pl.pallas_call(
        functools.partial(_back_kernel, alpha=alpha),
        out_shape=out_shape,
        grid=(n_blk + lag,),
        in_specs=[row(bw), row(bw), row(bw), row(d, 0), row(d, 1), row(d, 2), row(d),
                  _const_spec((bw, d)), _const_spec((bw, d)), _const_spec((bw, d)), _const_spec((d, d)),
                  _const_spec((1, d)), _const_spec((1, d)),
                  _const_spec((d, dff)), _const_spec((d, dff)), _const_spec((dff, d)),
                  _const_spec((1, d)), _const_spec((1, d))],
        out_specs=out_spec,
        scratch_shapes=[pltpu.VMEM((rb, d), F32), pltpu.VMEM((rb, d), F32)],
        compiler_params=_cparams(1, "arbitrary"),
        name="merge_ffn",
    )(om, os_, oh, zg, zg, zg, h, wm, ws, wh, wo, g1, b1, wg, wu, wd, g2, b2)


def _permute_w_in(w):
    mla_in = MLA_Q_RANK + MLA_KV_RANK + MLA_ROPE
    s5_0, hg_0 = mla_in, mla_in + S5_WIDTH
    g_0 = hg_0 + 2 * HG_QK + 2 * HG_VW
    pad = jnp.zeros((w.shape[0], ZKR_W - MLA_ROPE), w.dtype)
    return jnp.concatenate([w[:, s5_0:hg_0], w[:, 0:MLA_Q_RANK + MLA_KV_RANK], w[:, hg_0:g_0], w[:, g_0:],
                            w[:, MLA_Q_RANK + MLA_KV_RANK:mla_in], pad], axis=1).astype(BF16)


def _rot_half(x):
    x1, x2 = jnp.split(x, 2, axis=-1)
    return jnp.concatenate([-x2, x1], axis=-1)


def _mla_weights(w_uq, w_ukv):
    rq, rkv = w_uq.shape[0], w_ukv.shape[0]
    zpad = HEAD_PAD - MLA_NOPE - MLA_ROPE
    wq = w_uq.reshape(rq, MLA_HEADS, MLA_NOPE + MLA_ROPE)
    q_nope, q_rope = wq[..., :MLA_NOPE], wq[..., MLA_NOPE:]
    zq = jnp.zeros((rq, MLA_HEADS, zpad), w_uq.dtype)
    wq_p = jnp.concatenate([q_nope, q_rope, zq], axis=-1).reshape(rq, -1)
    wq_r = jnp.concatenate([jnp.zeros_like(q_nope), _rot_half(q_rope), zq], axis=-1).reshape(rq, -1)
    wkv = w_ukv.reshape(rkv, MLA_HEADS, MLA_NOPE + MLA_V)
    zk = jnp.zeros((rkv, MLA_HEADS, HEAD_PAD - MLA_NOPE), w_ukv.dtype)
    wk_p = jnp.concatenate([wkv[..., :MLA_NOPE], zk], axis=-1).reshape(rkv, -1)
    wv = wkv[..., MLA_NOPE:].reshape(rkv, -1)
    eye = jnp.eye(ZKR_W, MLA_ROPE, dtype=F32)
    place = jnp.concatenate([jnp.zeros((ZKR_W, MLA_NOPE), F32), eye, jnp.zeros((ZKR_W, zpad), F32)], axis=-1)
    place_r = jnp.concatenate([jnp.zeros((ZKR_W, MLA_NOPE), F32), _rot_half(eye),
                               jnp.zeros((ZKR_W, zpad), F32)], axis=-1)
    e = jnp.tile(place, (1, MLA_HEADS))
    er = jnp.tile(place_r, (1, MLA_HEADS))
    return [a.astype(BF16) for a in (wq_p, wq_r, wk_p, wv, e, er)]


def _rope_consts():
    half = MLA_ROPE // 2
    inv = ROPE_THETA ** (-(jnp.arange(0, MLA_ROPE, 2, dtype=F32) / MLA_ROPE))
    place = np.zeros((half, HEAD_PAD), np.float32)
    place[np.arange(half), MLA_NOPE + np.arange(half)] = 1.0
    place[np.arange(half), MLA_NOPE + half + np.arange(half)] = 1.0
    base = np.zeros((1, HEAD_PAD), np.float32)
    base[0, :MLA_NOPE] = 1.0
    return inv[:, None], jnp.asarray(place, BF16), jnp.asarray(base)


def _s5_params(lam_re, lam_im, log_dt, b_re, b_im, c_re, c_im):
    lr = jnp.minimum(lam_re.astype(F32), -1e-4)
    li = lam_im.astype(F32)
    dt = jnp.exp(log_dt.astype(F32))[:, None]
    mag = jnp.exp(lr * dt)
    ab_r = mag * jnp.cos(li * dt)
    ab_i = mag * jnp.sin(li * dt)
    den = lr * lr + li * li
    nr = ab_r - 1.0
    coef_r = ((nr * lr + ab_i * li) / den)[..., None]
    coef_i = ((ab_i * lr - nr * li) / den)[..., None]
    bb_r = coef_r * b_re.astype(F32) - coef_i * b_im.astype(F32)
    bb_i = coef_r * b_im.astype(F32) + coef_i * b_re.astype(F32)
    n_slab = S5_WIDTH // S5_SLAB
    gps = S5_SLAB // S5_GROUP
    eye = jnp.eye(gps, dtype=F32)

    def in_mat(bb):
        b4 = bb.reshape(n_slab, gps, S5_STATE, S5_GROUP)
        return jnp.einsum('jgnc,gh->jgchn', b4, eye).reshape(n_slab, S5_SLAB, S5_SLAB_STATE)

    def out_mat(cc):
        c4 = cc.astype(F32).reshape(n_slab, gps, S5_GROUP, S5_STATE)
        return jnp.einsum('jgcn,gh->jgnhc', c4, eye).reshape(n_slab, S5_SLAB_STATE, S5_SLAB)

    bm = jnp.concatenate([in_mat(bb_r), in_mat(bb_i)], axis=2).astype(BF16)
    cm = jnp.concatenate([out_mat(c_re), -out_mat(c_im)], axis=1).astype(BF16)
    ar = ab_r.reshape(n_slab, 1, S5_SLAB_STATE)
    ai = ab_i.reshape(n_slab, 1, S5_SLAB_STATE)
    return bm, cm, ar, ai


def kernel(x, positions, meta_tokens, ln_in_g, ln_in_b, w_in, mla_q_norm, mla_w_uq, mla_kv_norm, mla_w_ukv,
           s5_lam_re, s5_lam_im, s5_log_dt, s5_b_re, s5_b_im, s5_c_re, s5_c_im, s5_d, s5_w_glu,
           hg_lb_logits, hg_out_norm, w_br_mla, w_br_s5, w_br_hg, w_out, ln1_g, ln1_b,
           w_ffn_gate, w_ffn_up, w_ffn_down, ln2_g, ln2_b):
    nb, s, d = x.shape
    depth = w_in.shape[0]
    nblk = (s + N_META) // T_BLK
    alpha = (2 * depth) ** 0.25
    row2 = lambda a: a.astype(F32)[None, :]

    meta_pos = jnp.broadcast_to(jnp.arange(N_META, dtype=jnp.int32)[None, :], (nb, N_META))
    pos = jnp.concatenate([meta_pos, positions.astype(jnp.int32) + N_META], axis=1)
    pos = pos.reshape(nb, nblk, T_BLK).transpose(1, 0, 2).reshape(nblk, 1, nb * T_BLK)
    rope = _rope_consts()
    p_lb = jax.nn.softmax(hg_lb_logits.astype(F32), axis=0)
    lower_bounds = jnp.cumsum(p_lb, axis=0) - p_lb[0]
    tri = jnp.asarray(np.tril(np.ones((HG_CHUNK, HG_CHUNK), np.float32)), BF16)
    lvl = jnp.asarray(_hgrn_level_matrix())
    perm = jnp.asarray(_time_major_perm(nb), BF16)

    h = (x, meta_tokens.astype(x.dtype), row2(ln_in_g), row2(ln_in_b))
    for li in range(depth):
        s5_mats = _s5_params(s5_lam_re[li], s5_lam_im[li], s5_log_dt[li], s5_b_re[li], s5_b_im[li],
                             s5_c_re[li], s5_c_im[li])
        outs = _front(
            h, _permute_w_in(w_in[li]), pos, rope, row2(mla_q_norm[li]), row2(mla_kv_norm[li]),
            *_mla_weights(mla_w_uq[li], mla_w_ukv[li]), perm, *s5_mats, row2(s5_d[li]),
            s5_w_glu[li].astype(BF16), nb)
        if li == 0:
            h, *outs = outs
        zhg, zg, q, k, v, y_s5 = outs
        o_mla = _attention(q, k, v, nb, nblk)
        o_hg = _hgrn(zhg, lower_bounds[li][None, :], row2(hg_out_norm[li]), tri, lvl, nb, nblk)
        h = _back(o_mla, y_s5, o_hg, zg, h, w_br_mla[li].astype(BF16), w_br_s5[li].astype(BF16),
                  w_br_hg[li].astype(BF16), w_out[li].astype(BF16), row2(ln1_g[li]), row2(ln1_b[li]),
                  w_ffn_gate[li].astype(BF16), w_ffn_up[li].astype(BF16), w_ffn_down[li].astype(BF16),
                  row2(ln2_g[li]), row2(ln2_b[li]), nb, alpha, final=li == depth - 1)
    return h
```

```python
import functools
import math

import jax
import jax.numpy as jnp
import numpy as np
from jax import lax
from jax.experimental import pallas as pl
from jax.experimental.pallas import tpu as pltpu

F32 = jnp.float32
BF16 = jnp.bfloat16

N_META = 16
MLA_HEADS = 8
MLA_NOPE = 64
MLA_ROPE = 32
MLA_V = 64
MLA_Q_RANK = 256
MLA_KV_RANK = 256
ROPE_THETA = 10000.0
MASK_VALUE = -1e9
LOG2_E = math.log2(math.e)
HEAD_PAD = 128
S5_WIDTH = 512
S5_GROUP = 16
S5_GROUPS = S5_WIDTH // S5_GROUP
S5_STATE = 64
S5_SLAB = 128
S5_SLAB_STATE = (S5_SLAB // S5_GROUP) * S5_STATE
HG_HEADS = 4
HG_KEY = 128
HG_VAL = 128
HG_QK = HG_HEADS * HG_KEY
HG_VW = HG_HEADS * HG_VAL
HG_F_MIN = 1e-6
HG_CHUNK = 128
HG_LEVELS = (128, 64, 32, 16)
HG_BOTTOM = 8
N_BRANCH = 3
T_BLK = 16
ATT_TQ = 256
ATT_HEADS_PER_STEP = 4
VMEM_LIMIT = 56 * 1024 * 1024

ZA_W = 1024
ZHG_W = 2048
ZG_W = 3072
ZKR_W = 128
Z_W = ZA_W + ZHG_W + ZG_W + ZKR_W
WIDE_CHUNK = 512
FILL_BEFORE_MLA = 3


def _cparams(n_grid, sem="parallel"):
    return pltpu.CompilerParams(dimension_semantics=(sem,) * n_grid, vmem_limit_bytes=VMEM_LIMIT)


def _const_spec(shape):
    nd = len(shape)
    return pl.BlockSpec(shape, lambda *_: (0,) * nd, pipeline_mode=pl.Buffered(1))


def _seq_spec(nblk, width, col):
    return pl.BlockSpec((nblk, None, T_BLK, width), lambda b, *g: (0, b, 0, col(b, *g)))


def _dot(a, b):
    return jnp.dot(a, b, preferred_element_type=F32)


def _dot_nt(a, b):
    return lax.dot_general(a, b, (((1,), (1,)), ((), ())), preferred_element_type=F32)


def _dot_tn(a, b):
    return lax.dot_general(a, b, (((0,), (0,)), ((), ())), preferred_element_type=F32)


def _sigmoid(x):
    return 1.0 / (1.0 + jnp.exp(-x))


def _layer_norm(x, g, b, eps=1e-5):
    mu = jnp.mean(x, axis=-1, keepdims=True)
    xc = x - mu
    var = jnp.mean(xc * xc, axis=-1, keepdims=True)
    return xc * lax.rsqrt(var + eps) * g + b


def _rms_norm(x, g, eps=1e-6):
    return x * lax.rsqrt(jnp.mean(x * x, axis=-1, keepdims=True) + eps) * g


def _load_rows(ref, blk0, n_blk, cs):
    x = ref[pl.ds(blk0, n_blk), :, cs]
    return x.reshape(n_blk * T_BLK, x.shape[-1])


def _store_rows(ref, blk0, n_blk, cs, x):
    ref[pl.ds(blk0, n_blk), :, cs] = x.reshape(n_blk, T_BLK, x.shape[-1])


def _rope_tables(pos_ref, inv_ref, place_ref, base_ref):
    ang = inv_ref[...] * pos_ref[...].astype(F32)

    def place(t):
        hi = t.astype(BF16)
        lo = (t - hi.astype(F32)).astype(BF16)
        return _dot_tn(hi, place_ref[...]) + _dot_tn(lo, place_ref[...])

    return place(jnp.cos(ang)) + base_ref[...], place(jnp.sin(ang))


def _mla_prep(cq, ckv, kr, cos, sin, qn_ref, kvn_ref, wq_ref, wqr_ref, wk_ref, wv_ref, e_ref, er_ref,
              q_ref, k_ref, v_ref):
    scale = (MLA_NOPE + MLA_ROPE) ** -0.5 * LOG2_E
    cqn = _rms_norm(cq, qn_ref[...]).astype(BF16)
    ckvn = _rms_norm(ckv, kvn_ref[...]).astype(BF16)
    kr = kr.astype(BF16)
    cos_q = cos * scale
    sin_q = sin * scale
    for h in range(MLA_HEADS):
        cs = slice(h * HEAD_PAD, (h + 1) * HEAD_PAD)
        qf = _dot(cqn, wq_ref[:, cs])
        qr = _dot(cqn, wqr_ref[:, cs])
        q_ref[:, cs] = (qf * cos_q + qr * sin_q).astype(BF16)
        kf = _dot(ckvn, wk_ref[:, cs]) + _dot(kr, e_ref[:, cs])
        krot = _dot(kr, er_ref[:, cs])
        k_ref[:, cs] = (kf * cos + krot * sin).astype(BF16)
    v_ref[...] = _dot(ckvn, wv_ref[...]).astype(BF16)


def _s5_block(u, perm_ref, bm_ref, cm_ref, ar_ref, ai_ref, d_ref, wg_ref, o_ref,
              xr_ref, xi_ref, buf_ref, y_ref, nb, fillers):
    n_slab = S5_WIDTH // S5_SLAB
    ns = S5_SLAB_STATE
    u = _dot(perm_ref[...], u).astype(BF16)
    for j in range(n_slab):
        buf_ref[...] = _dot(u[:, j * S5_SLAB:(j + 1) * S5_SLAB], bm_ref[j])
        if fillers:
            fillers.pop(0)()
        ar = jnp.broadcast_to(ar_ref[j], (nb, ns))
        ai = jnp.broadcast_to(ai_ref[j], (nb, ns))
        xr = xr_ref[j]
        xi = xi_ref[j]
        for t in range(T_BLK):
            rows = slice(t * nb, (t + 1) * nb)
            nr = ar * xr - ai * xi + buf_ref[rows, 0:ns]
            ni = ar * xi + ai * xr + buf_ref[rows, ns:2 * ns]
            buf_ref[rows, 0:ns] = nr
            buf_ref[rows, ns:2 * ns] = ni
            xr, xi = nr, ni
        xr_ref[j] = xr
        xi_ref[j] = xi
        y_ref[:, j * S5_SLAB:(j + 1) * S5_SLAB] = _dot(buf_ref[...].astype(BF16), cm_ref[j])

    while fillers:
        fillers.pop(0)()
    y = y_ref[...] + d_ref[...] * u.astype(F32)
    y = 0.5 * y * (1.0 + jnp.tanh(math.sqrt(2.0 / math.pi) * (y + 0.044715 * (y * y * y))))
    gate = _sigmoid(_dot(y.astype(BF16), wg_ref[...]))
    out = (y * gate).astype(BF16)
    o_ref[...] = _dot_tn(perm_ref[...], out).astype(BF16)


def _front_kernel(*refs, nb, ln_in):
    if ln_in:
        x_ref, meta_ref, lng_ref, lnb_ref, *refs = refs
    else:
        h_ref, *refs = refs
    (w_ref, pos_ref, inv_ref, place_ref, base_ref, qn_ref, kvn_ref, wq_ref, wqr_ref, wk_ref, wv_ref, e_ref, er_ref,
     perm_ref, bm_ref, cm_ref, ar_ref, ai_ref, d_ref, wglu_ref, *refs) = refs
    if ln_in:
        hout_ref, *refs = refs
    zhg_ref, zg_ref, q_ref, k_ref, v_ref, ys5_ref, xr_ref, xi_ref, buf_ref, y_ref = refs

    @pl.when(pl.program_id(0) == 0)
    def _():
        xr_ref[...] = jnp.zeros_like(xr_ref)
        xi_ref[...] = jnp.zeros_like(xi_ref)

    cos, sin = _rope_tables(pos_ref, inv_ref, place_ref, base_ref)
    if ln_in:
        hx = _layer_norm(x_ref[...], lng_ref[...], lnb_ref[...])
        hm = _layer_norm(meta_ref[...], lng_ref[...], lnb_ref[...])
        h = jnp.where(pl.program_id(0) == 0, jnp.broadcast_to(hm[None], hx.shape), hx)
        h = h.reshape(hout_ref.shape)
        hout_ref[...] = h
    else:
        h = h_ref[...]
    x = h.astype(BF16)
    za = _dot(x, w_ref[:, 0:ZA_W])
    zkr = _dot(x, w_ref[:, ZA_W + ZHG_W + ZG_W:Z_W])

    def wide_chunk(o_ref, c, off):
        def run():
            o_ref[:, c:c + WIDE_CHUNK] = _dot(x, w_ref[:, off + c:off + c + WIDE_CHUNK]).astype(BF16)
        return run

    fillers = [wide_chunk(zhg_ref, c, ZA_W) for c in range(0, ZHG_W, WIDE_CHUNK)]
    fillers += [wide_chunk(zg_ref, c, ZA_W + ZHG_W) for c in range(0, ZG_W, WIDE_CHUNK)]
    for _ in range(FILL_BEFORE_MLA):
        fillers.pop(0)()
    c0 = S5_WIDTH
    _mla_prep(za[:, c0:c0 + MLA_Q_RANK], za[:, c0 + MLA_Q_RANK:c0 + MLA_Q_RANK + MLA_KV_RANK], zkr,
              cos, sin, qn_ref, kvn_ref, wq_ref, wqr_ref, wk_ref, wv_ref, e_ref, er_ref,
              q_ref, k_ref, v_ref)
    _s5_block(za[:, 0:S5_WIDTH].astype(BF16), perm_ref, bm_ref, cm_ref, ar_ref, ai_ref, d_ref, wglu_ref, ys5_ref,
              xr_ref, xi_ref, buf_ref, y_ref, nb, fillers)


def _front(src, w, pos, rope, qn, kvn, wq, wqr, wk, wv, e, er, perm, bm, cm, ar, ai, dskip, wglu, nb):
    ln_in = isinstance(src, tuple)
    rb = T_BLK * nb
    if ln_in:
        x, meta, lng, lnb = src
        d = x.shape[-1]
        r = (x.shape[1] + N_META) * nb
        src_specs = [pl.BlockSpec((nb, T_BLK, d), lambda i: (0, jnp.maximum(i - 1, 0), 0)),
                     _const_spec((N_META, d)), _const_spec((1, d)), _const_spec((1, d))]
    else:
        src = (src,)
        r, d = src[0].shape
        src_specs = [pl.BlockSpec((rb, d), lambda i: (i, 0))]
    hw = MLA_HEADS * HEAD_PAD
    vw = MLA_HEADS * MLA_V
    n_slab = S5_WIDTH // S5_SLAB
    row = lambda width: pl.BlockSpec((rb, width), lambda i: (i, 0))
    widths = (ZHG_W, ZG_W, hw, hw, vw, S5_WIDTH)
    out_shape = [jax.ShapeDtypeStruct((r, wd), BF16) for wd in widths]
    out_specs = [row(wd) for wd in widths]
    if ln_in:
        out_shape.insert(0, jax.ShapeDtypeStruct((r, d), F32))
        out_specs.insert(0, row(d))
    return pl.pallas_call(
        functools.partial(_front_kernel, nb=nb, ln_in=ln_in),
        out_shape=out_shape,
        grid=(r // rb,),
        in_specs=src_specs + [
            _const_spec((d, Z_W)), pl.BlockSpec((None, 1, rb), lambda i: (i, 0, 0)),
            _const_spec((MLA_ROPE // 2, 1)), _const_spec((MLA_ROPE // 2, HEAD_PAD)), _const_spec((1, HEAD_PAD)),
            _const_spec((1, MLA_Q_RANK)), _const_spec((1, MLA_KV_RANK)),
            _const_spec((MLA_Q_RANK, hw)), _const_spec((MLA_Q_RANK, hw)), _const_spec((MLA_KV_RANK, hw)),
            _const_spec((MLA_KV_RANK, vw)), _const_spec((ZKR_W, hw)), _const_spec((ZKR_W, hw)),
            _const_spec((rb, rb)),
            _const_spec((n_slab, S5_SLAB, 2 * S5_SLAB_STATE)),
            _const_spec((n_slab, 2 * S5_SLAB_STATE, S5_SLAB)),
            _const_spec((n_slab, 1, S5_SLAB_STATE)),
            _const_spec((n_slab, 1, S5_SLAB_STATE)),
            _const_spec((1, S5_WIDTH)),
            _const_spec((S5_WIDTH, S5_WIDTH)),
        ],
        out_specs=out_specs,
        scratch_shapes=[
            pltpu.VMEM((n_slab, nb, S5_SLAB_STATE), F32),
            pltpu.VMEM((n_slab, nb, S5_SLAB_STATE), F32),
            pltpu.VMEM((rb, 2 * S5_SLAB_STATE), F32),
            pltpu.VMEM((rb, S5_WIDTH), F32),
        ],
        compiler_params=_cparams(1, "arbitrary"),
        name="front",
    )(*src, w, pos, *rope, qn, kvn, wq, wqr, wk, wv, e, er, perm, bm, cm, ar, ai, dskip, wglu)


def _attn_kernel(tab_ref, q_ref, k_ref, v_ref, o_ref, vt_ref, qt_ref, m_ref, l_ref, acc_ref, *, n_heads, tq, nq):
    bpq = tq // T_BLK
    vw = n_heads * MLA_V

    for j in range(nq):
        vb = _load_rows(v_ref, 1 + j * bpq, bpq, slice(0, vw))
        vt_ref[j] = vb.astype(F32).T.astype(BF16)
    v0t = v_ref[0].astype(F32).T.astype(BF16)

    for i in range(nq):
        qb = _load_rows(q_ref, 1 + i * bpq, bpq, slice(0, n_heads * HEAD_PAD))
        qt_ref[i] = qb.astype(F32).T.astype(BF16)

    def causal(st):
        r = lax.broadcasted_iota(jnp.int32, st.shape, 0)
        c = lax.broadcasted_iota(jnp.int32, st.shape, 1)
        return jnp.where(r <= c, st, MASK_VALUE * LOG2_E)

    heads = range(n_heads)
    qs = [slice(h * HEAD_PAD, (h + 1) * HEAD_PAD) for h in heads]
    vs = [slice(h * MLA_V, (h + 1) * MLA_V) for h in heads]
    k0 = [k_ref[0, :, c] for c in qs]

    outs = []
    for n in heads:
        st = causal(_dot_nt(k0[n], q_ref[0, :, qs[n]]))
        p = jnp.exp2(st - jnp.max(st, axis=0, keepdims=True))
        outs.append(_dot(v0t[vs[n], :], p.astype(BF16)) / jnp.sum(p, axis=0, keepdims=True))
    o_ref[0, :, :] = jnp.concatenate(outs, axis=0).T.astype(BF16)

    def key_tile(j, n):
        return _load_rows(k_ref, 1 + j * bpq, bpq, qs[n])

    def diag_scores(i):
        return tuple((_dot(key_tile(i, n), qt_ref[i, qs[n], :]), _dot(k0[n], qt_ref[i, qs[n], :])) for n in heads)

    def diag_stats(i, x):
        out = []
        for n in heads:
            st, sm = causal(x[n][0]), x[n][1]
            m = jnp.maximum(jnp.max(st, axis=0, keepdims=True), jnp.max(sm, axis=0, keepdims=True))
            p = jnp.exp2(st - m)
            pm = jnp.exp2(sm - m)
            m_ref[i, n] = m
            l_ref[i, n] = jnp.sum(p, axis=0, keepdims=True) + jnp.sum(pm, axis=0, keepdims=True)
            out.append((p.astype(BF16), pm.astype(BF16)))
        return tuple(out)

    def diag_values(i, y):
        for n in heads:
            acc_ref[i, vs[n], :] = _dot(vt_ref[i, vs[n], :], y[n][0]) + _dot(v0t[vs[n], :], y[n][1])

    _pipeline3(nq, diag_scores, diag_stats, diag_values)

    def off_scores(f):
        i, j = tab_ref[0, f], tab_ref[1, f]
        return tuple(_dot(key_tile(j, n), qt_ref[i, qs[n], :]) for n in heads)

    def off_stats(f, x):
        i = tab_ref[0, f]
        out = []
        for n in heads:
            m_old = m_ref[i, n]
            m = jnp.maximum(m_old, jnp.max(x[n], axis=0, keepdims=True))
            a = jnp.exp2(m_old - m)
            p = jnp.exp2(x[n] - m)
            m_ref[i, n] = m
            l_ref[i, n] = a * l_ref[i, n] + jnp.sum(p, axis=0, keepdims=True)
            out.append((a, p.astype(BF16)))
        return tuple(out)

    def off_values(f, y):
        i, j = tab_ref[0, f], tab_ref[1, f]
        for n in heads:
            acc_ref[i, vs[n], :] = y[n][0] * acc_ref[i, vs[n], :] + _dot(vt_ref[j, vs[n], :], y[n][1])

    _pipeline3(nq * (nq - 1) // 2, off_scores, off_stats, off_values)

    for i in range(nq):
        o = jnp.concatenate([acc_ref[i, vs[n], :] / l_ref[i, n] for n in heads], axis=0)
        _store_rows(o_ref, 1 + i * bpq, bpq, slice(0, vw), o.T.astype(BF16))


def _pipeline3(n, stage1, stage2, stage3):
    if n == 0:
        return
    x0 = stage1(0)
    x1 = stage1(1) if n > 1 else None
    y = stage2(0, x0)
    if n > 2:
        def body(t, carry):
            y_t, x_t1 = carry
            x_t2 = stage1(t + 2)
            y_t1 = stage2(t + 1, x_t1)
            stage3(t, y_t)
            return y_t1, x_t2

        y, x1 = lax.fori_loop(0, n - 2, body, (y, x1))
    if n > 1:
        stage3(n - 2, y)
        y = stage2(n - 1, x1)
    stage3(n - 1, y)


def _attention(q, k, v, nb, nblk):
    hps = ATT_HEADS_PER_STEP
    n_hp = MLA_HEADS // hps
    qw = hps * HEAD_PAD
    vw = hps * MLA_V
    s = (nblk - 1) * T_BLK
    tq = min(ATT_TQ, s)
    nq = s // tq
    col = lambda b, p: p
    pairs = [(i, j) for i in range(nq) for j in range(i)] or [(0, 0)]
    tab = jnp.asarray(np.array(pairs, np.int32).T)
    o = pl.pallas_call(
        functools.partial(_attn_kernel, n_heads=hps, tq=tq, nq=nq),
        out_shape=jax.ShapeDtypeStruct((nblk, nb, T_BLK, MLA_HEADS * MLA_V), BF16),
        grid=(nb, n_hp),
        in_specs=[pl.BlockSpec(memory_space=pltpu.SMEM),
                  _seq_spec(nblk, qw, col), _seq_spec(nblk, qw, col), _seq_spec(nblk, vw, col)],
        out_specs=_seq_spec(nblk, vw, col),
        scratch_shapes=[pltpu.VMEM((nq, vw, tq), BF16), pltpu.VMEM((nq, qw, tq), BF16),
                        pltpu.VMEM((nq, hps, 1, tq), F32), pltpu.VMEM((nq, hps, 1, tq), F32),
                        pltpu.VMEM((nq, vw, tq), F32)],
        compiler_params=_cparams(2),
        name="mla_attn",
    )(tab, q.reshape(nblk, nb, T_BLK, -1), k.reshape(nblk, nb, T_BLK, -1), v.reshape(nblk, nb, T_BLK, -1))
    return o.reshape(nblk * nb * T_BLK, MLA_HEADS * MLA_V)


def _time_major_perm(nb):
    p = np.zeros((T_BLK * nb, T_BLK * nb), np.float32)
    for b in range(nb):
        for t in range(T_BLK):
            p[t * nb + b, b * T_BLK + t] = 1.0
    return p


def _block_row_bcast(x, m, row):
    t, c = x.shape
    if m == t:
        return jnp.broadcast_to(x[row:row + 1, :], x.shape)
    x3 = x.reshape(t // m, m, c)
    return jnp.broadcast_to(x3[:, row:row + 1, :], x3.shape).reshape(t, c)


def _hgrn_gates(blk0, n_blk, f_ref, lb_ref):
    lb = lb_ref[...]
    zf = _load_rows(f_ref, blk0, n_blk, slice(0, HG_QK)).astype(F32)
    e = jnp.exp(-jnp.abs(zf))
    rcp = 1.0 / (1.0 + e)
    pos = zf >= 0.0
    sig_p = jnp.where(pos, rcp, e * rcp)
    sig_n = jnp.where(pos, e * rcp, rcp)
    f = lb + (1.0 - lb) * sig_p
    log_f = jnp.log2(jnp.maximum(f, HG_F_MIN))
    k = (1.0 - lb) * sig_n
    hi = log_f.astype(BF16)
    lo = (log_f - hi.astype(F32)).astype(BF16)
    return k, hi, lo


def _hgrn_mix(blk0, n_blk, gates, q_ref, v_ref, tri_ref, lvl_ref, st_ref):
    t = n_blk * T_BLK
    cs = slice(0, HG_QK)
    heads = [slice(h * HG_KEY, (h + 1) * HG_KEY) for h in range(HG_HEADS)]
    k, hi, lo = gates
    q = _load_rows(q_ref, blk0, n_blk, cs).astype(F32)
    v = _load_rows(v_ref, blk0, n_blk, cs)
    tri = tri_ref[0:t, 0:t]
    cum = _dot(tri, hi) + _dot(tri, lo)
    lvl = lvl_ref[0:t, 0:t]
    r_idx = lax.broadcasted_iota(jnp.int32, (t, HG_QK), 0)
    c8 = _block_row_bcast(cum, HG_BOTTOM, HG_BOTTOM // 2 - 1)
    qe = (q * jnp.exp2(cum - c8)).astype(BF16)
    ke = (k * jnp.exp2(c8 - cum)).astype(BF16)
    n_lvl = len(HG_LEVELS)
    scores = [jnp.where(lvl == n_lvl, _dot_nt(qe[:, hs], ke[:, hs]), 0.0) for hs in heads]
    for li, m in enumerate(HG_LEVELS):
        if m > t:
            continue
        half = m // 2
        cmid = _block_row_bcast(cum, m, half - 1)
        upper = (r_idx & (m - 1)) >= half
        ex = jnp.exp2(jnp.where(upper, cum - cmid, cmid - cum))
        qe = jnp.where(upper, q * ex, 0.0).astype(BF16)
        ke = jnp.where(upper, 0.0, k * ex).astype(BF16)
        scores = [jnp.where(lvl == li, _dot_nt(qe[:, hs], ke[:, hs]), s) for s, hs in zip(scores, heads)]
    qd = (q * jnp.exp2(cum)).astype(BF16)
    last = cum[t - 1:t, :]
    kd = (k * jnp.exp2(last - cum)).astype(BF16)
    dec = jnp.exp2(last)
    outs = []
    for h, hs in enumerate(heads):
        st = st_ref[h]
        outs.append(_dot(scores[h].astype(BF16), v[:, hs]) + _dot_nt(qd[:, hs], st.astype(BF16)))
        st_ref[h] = st * dec[:, hs] + _dot_tn(v[:, hs], kd[:, hs])
    return tuple(outs)


def _hgrn_out(blk0, n_blk, outs, g_ref, on_ref, o_ref):
    cs = slice(0, HG_VW)
    g = _load_rows(g_ref, blk0, n_blk, cs).astype(F32)
    o = jnp.concatenate([o * lax.rsqrt(jnp.mean(o * o, axis=-1, keepdims=True) + 1e-6) for o in outs], axis=1)
    _store_rows(o_ref, blk0, n_blk, cs, (o * on_ref[...] * (g * _sigmoid(g))).astype(BF16))


def _hgrn_kernel(q_ref, f_ref, v_ref, g_ref, lb_ref, on_ref, tri_ref, lvl_ref, o_ref, st_ref, *, n_chunks):
    bpc = HG_CHUNK // T_BLK
    gates = functools.partial(_hgrn_gates, f_ref=f_ref, lb_ref=lb_ref)
    mix = functools.partial(_hgrn_mix, q_ref=q_ref, v_ref=v_ref, tri_ref=tri_ref, lvl_ref=lvl_ref, st_ref=st_ref)
    out = functools.partial(_hgrn_out, g_ref=g_ref, on_ref=on_ref, o_ref=o_ref)
    st_ref[...] = jnp.zeros_like(st_ref)
    out(0, 1, mix(0, 1, gates(0, 1)))
    _pipeline3(n_chunks,
               lambda c: gates(1 + c * bpc, bpc),
               lambda c, x: mix(1 + c * bpc, bpc, x),
               lambda c, y: out(1 + c * bpc, bpc, y))


def _hgrn(zhg, lb, onorm, tri, lvl, nb, nblk):
    z4 = zhg.reshape(nblk, nb, T_BLK, ZHG_W)
    col = lambda j: _seq_spec(nblk, HG_QK, lambda b: j)
    o = pl.pallas_call(
        functools.partial(_hgrn_kernel, n_chunks=(nblk - 1) * T_BLK // HG_CHUNK),
        out_shape=jax.ShapeDtypeStruct((nblk, nb, T_BLK, HG_VW), BF16),
        grid=(nb,),
        in_specs=[col(0), col(1), col(2), col(3),
                  _const_spec((1, HG_QK)), _const_spec((1, HG_VW)),
                  _const_spec((HG_CHUNK, HG_CHUNK)), _const_spec((HG_CHUNK, HG_CHUNK))],
        out_specs=_seq_spec(nblk, HG_VW, lambda b: 0),
        scratch_shapes=[pltpu.VMEM((HG_HEADS, HG_VAL, HG_KEY), F32)],
        compiler_params=_cparams(1),
        name="hgrn2",
    )(z4, z4, z4, z4, lb, onorm, tri, lvl)
    return o.reshape(nblk * nb * T_BLK, HG_VW)


def _hgrn_level_matrix():
    r = np.arange(HG_CHUNK)[:, None]
    c = np.arange(HG_CHUNK)[None, :]
    lvl = np.zeros((HG_CHUNK, HG_CHUNK), np.int32)
    for li, m in enumerate(HG_LEVELS):
        lvl = np.where(r // m == c // m, li, lvl)
    lvl = np.where(r // HG_BOTTOM == c // HG_BOTTOM, len(HG_LEVELS), lvl)
    return np.where(c <= r, lvl, -1).astype(np.int32)


def _back_kernel(om_ref, os_ref, oh_ref, gm_ref, gs_ref, gh_ref, h_ref,
                 wm_ref, ws_ref, wh_ref, wo_ref, g1_ref, b1_ref,
                 wg_ref, wu_ref, wd_ref, g2_ref, b2_ref, o_ref, r1_ref, r2_ref, *, alpha):
    @pl.when(pl.program_id(0) == 0)
    def _():
        r1_ref[...] = jnp.zeros_like(r1_ref)
        r2_ref[...] = jnp.zeros_like(r2_ref)

    r1 = r1_ref[...]
    r2 = r2_ref[...]
    ym = _dot(om_ref[...], wm_ref[...])
    ys = _dot(os_ref[...], ws_ref[...])
    yh = _dot(oh_ref[...], wh_ref[...])
    o_ref[...] = _layer_norm(r2, g2_ref[...], b2_ref[...]).reshape(o_ref.shape)
    h1 = _layer_norm(r1, g1_ref[...], b1_ref[...])
    hb = h1.astype(BF16)
    a = _dot(hb, wg_ref[...])
    u = _dot(hb, wu_ref[...])
    mixed = _sigmoid(gm_ref[...].astype(F32)) * ym
    mixed += _sigmoid(gs_ref[...].astype(F32)) * ys
    mixed += _sigmoid(gh_ref[...].astype(F32)) * yh
    r1_ref[...] = alpha * h_ref[...] + _dot(mixed.astype(BF16), wo_ref[...])
    r2_ref[...] = alpha * h1 + _dot((a * _sigmoid(a) * u).astype(BF16), wd_ref[...])


def _back(om, os_, oh, zg, h, wm, ws, wh, wo, g1, b1, wg, wu, wd, g2, b2, nb, alpha, final):
    r, d = h.shape
    bw = om.shape[1]
    dff = wg.shape[1]
    rb = T_BLK * nb
    skip = 1 if final else 0
    n_blk = r // rb - skip
    lag = 2
    row = lambda width, j=0: pl.BlockSpec((rb, width), lambda i: (jnp.minimum(i, n_blk - 1) + skip, j))
    if final:
        out_shape = jax.ShapeDtypeStruct((nb, n_blk * T_BLK, d), F32)
        out_spec = pl.BlockSpec((nb, T_BLK, d), lambda i: (0, jnp.maximum(i - lag, 0), 0))
    else:
        out_shape = jax.ShapeDtypeStruct((r, d), F32)
        out_spec = pl.BlockSpec((rb, d), lambda i: (jnp.maximum(i - lag, 0), 0))
    return pl.pallas_call(
        functools.partial(_back_kernel, alpha=alpha),
        out_shape=out_shape,
        grid=(n_blk + lag,),
        in_specs=[row(bw), row(bw), row(bw), row(d, 0), row(d, 1), row(d, 2), row(d),
                  _const_spec((bw, d)), _const_spec((bw, d)), _const_spec((bw, d)), _const_spec((d, d)),
                  _const_spec((1, d)), _const_spec((1, d)),
                  _const_spec((d, dff)), _const_spec((d, dff)), _const_spec((dff, d)),
                  _const_spec((1, d)), _const_spec((1, d))],
        out_specs=out_spec,
        scratch_shapes=[pltpu.VMEM((rb, d), F32), pltpu.VMEM((rb, d), F32)],
        compiler_params=_cparams(1, "arbitrary"),
        name="merge_ffn",
    )(om, os_, oh, zg, zg, zg, h, wm, ws, wh, wo, g1, b1, wg, wu, wd, g2, b2)


def _permute_w_in(w):
    mla_in = MLA_Q_RANK + MLA_KV_RANK + MLA_ROPE
    s5_0, hg_0 = mla_in, mla_in + S5_WIDTH
    g_0 = hg_0 + 2 * HG_QK + 2 * HG_VW
    pad = jnp.zeros((w.shape[0], ZKR_W - MLA_ROPE), w.dtype)
    return jnp.concatenate([w[:, s5_0:hg_0], w[:, 0:MLA_Q_RANK + MLA_KV_RANK], w[:, hg_0:g_0], w[:, g_0:],
                            w[:, MLA_Q_RANK + MLA_KV_RANK:mla_in], pad], axis=1).astype(BF16)


def _rot_half(x):
    x1, x2 = jnp.split(x, 2, axis=-1)
    return jnp.concatenate([-x2, x1], axis=-1)


def _mla_weights(w_uq, w_ukv):
    rq, rkv = w_uq.shape[0], w_ukv.shape[0]
    zpad = HEAD_PAD - MLA_NOPE - MLA_ROPE
    wq = w_uq.reshape(rq, MLA_HEADS, MLA_NOPE + MLA_ROPE)
    q_nope, q_rope = wq[..., :MLA_NOPE], wq[..., MLA_NOPE:]
    zq = jnp.zeros((rq, MLA_HEADS, zpad), w_uq.dtype)
    wq_p = jnp.concatenate([q_nope, q_rope, zq], axis=-1).reshape(rq, -1)
    wq_r = jnp.concatenate([jnp.zeros_like(q_nope), _rot_half(q_rope), zq], axis=-1).reshape(rq, -1)
    wkv = w_ukv.reshape(rkv, MLA_HEADS, MLA_NOPE + MLA_V)
    zk = jnp.zeros((rkv, MLA_HEADS, HEAD_PAD - MLA_NOPE), w_ukv.dtype)
    wk_p = jnp.concatenate([wkv[..., :MLA_NOPE], zk], axis=-1).reshape(rkv, -1)
    wv = wkv[..., MLA_NOPE:].reshape(rkv, -1)
    eye = jnp.eye(ZKR_W, MLA_ROPE, dtype=F32)
    place = jnp.concatenate([jnp.zeros((ZKR_W, MLA_NOPE), F32), eye, jnp.zeros((ZKR_W, zpad), F32)], axis=-1)
    place_r = jnp.concatenate([jnp.zeros((ZKR_W, MLA_NOPE), F32), _rot_half(eye),
                               jnp.zeros((ZKR_W, zpad), F32)], axis=-1)
    e = jnp.tile(place, (1, MLA_HEADS))
    er = jnp.tile(place_r, (1, MLA_HEADS))
    return [a.astype(BF16) for a in (wq_p, wq_r, wk_p, wv, e, er)]


def _rope_consts():
    half = MLA_ROPE // 2
    inv = ROPE_THETA ** (-(jnp.arange(0, MLA_ROPE, 2, dtype=F32) / MLA_ROPE))
    place = np.zeros((half, HEAD_PAD), np.float32)
    place[np.arange(half), MLA_NOPE + np.arange(half)] = 1.0
    place[np.arange(half), MLA_NOPE + half + np.arange(half)] = 1.0
    base = np.zeros((1, HEAD_PAD), np.float32)
    base[0, :MLA_NOPE] = 1.0
    return inv[:, None], jnp.asarray(place, BF16), jnp.asarray(base)


def _s5_params(lam_re, lam_im, log_dt, b_re, b_im, c_re, c_im):
    lr = jnp.minimum(lam_re.astype(F32), -1e-4)
    li = lam_im.astype(F32)
    dt = jnp.exp(log_dt.astype(F32))[:, None]
    mag = jnp.exp(lr * dt)
    ab_r = mag * jnp.cos(li * dt)
    ab_i = mag * jnp.sin(li * dt)
    den = lr * lr + li * li
    nr = ab_r - 1.0
    coef_r = ((nr * lr + ab_i * li) / den)[..., None]
    coef_i = ((ab_i * lr - nr * li) / den)[..., None]
    bb_r = coef_r * b_re.astype(F32) - coef_i * b_im.astype(F32)
    bb_i = coef_r * b_im.astype(F32) + coef_i * b_re.astype(F32)
    n_slab = S5_WIDTH // S5_SLAB
    gps = S5_SLAB // S5_GROUP
    eye = jnp.eye(gps, dtype=F32)

    def in_mat(bb):
        b4 = bb.reshape(n_slab, gps, S5_STATE, S5_GROUP)
        return jnp.einsum('jgnc,gh->jgchn', b4, eye).reshape(n_slab, S5_SLAB, S5_SLAB_STATE)

    def out_mat(cc):
        c4 = cc.astype(F32).reshape(n_slab, gps, S5_GROUP, S5_STATE)
        return jnp.einsum('jgcn,gh->jgnhc', c4, eye).reshape(n_slab, S5_SLAB_STATE, S5_SLAB)

    bm = jnp.concatenate([in_mat(bb_r), in_mat(bb_i)], axis=2).astype(BF16)
    cm = jnp.concatenate([out_mat(c_re), -out_mat(c_im)], axis=1).astype(BF16)
    ar = ab_r.reshape(n_slab, 1, S5_SLAB_STATE)
    ai = ab_i.reshape(n_slab, 1, S5_SLAB_STATE)
    return bm, cm, ar, ai


def kernel(x, positions, meta_tokens, ln_in_g, ln_in_b, w_in, mla_q_norm, mla_w_uq, mla_kv_norm, mla_w_ukv,
           s5_lam_re, s5_lam_im, s5_log_dt, s5_b_re, s5_b_im, s5_c_re, s5_c_im, s5_d, s5_w_glu,
           hg_lb_logits, hg_out_norm, w_br_mla, w_br_s5, w_br_hg, w_out, ln1_g, ln1_b,
           w_ffn_gate, w_ffn_up, w_ffn_down, ln2_g, ln2_b):
    nb, s, d = x.shape
    depth = w_in.shape[0]
    nblk = (s + N_META) // T_BLK
    alpha = (2 * depth) ** 0.25
    row2 = lambda a: a.astype(F32)[None, :]

    meta_pos = jnp.broadcast_to(jnp.arange(N_META, dtype=jnp.int32)[None, :], (nb, N_META))
    pos = jnp.concatenate([meta_pos, positions.astype(jnp.int32) + N_META], axis=1)
    pos = pos.reshape(nb, nblk, T_BLK).transpose(1, 0, 2).reshape(nblk, 1, nb * T_BLK)
    rope = _rope_consts()
    p_lb = jax.nn.softmax(hg_lb_logits.astype(F32), axis=0)
    lower_bounds = jnp.cumsum(p_lb, axis=0) - p_lb[0]
    tri = jnp.asarray(np.tril(np.ones((HG_CHUNK, HG_CHUNK), np.float32)), BF16)
    lvl = jnp.asarray(_hgrn_level_matrix())
    perm = jnp.asarray(_time_major_perm(nb), BF16)

    h = (x, meta_tokens.astype(x.dtype), row2(ln_in_g), row2(ln_in_b))
    for li in range(depth):
        s5_mats = _s5_params(s5_lam_re[li], s5_lam_im[li], s5_log_dt[li], s5_b_re[li], s5_b_im[li],
                             s5_c_re[li], s5_c_im[li])
        outs = _front(
            h, _permute_w_in(w_in[li]), pos, rope, row2(mla_q_norm[li]), row2(mla_kv_norm[li]),
            *_mla_weights(mla_w_uq[li], mla_w_ukv[li]), perm, *s5_mats, row2(s5_d[li]),
            s5_w_glu[li].astype(BF16), nb)
        if li == 0:
            h, *outs = outs
        zhg, zg, q, k, v, y_s5 = outs
        o_mla = _attention(q, k, v, nb, nblk)
        o_hg = _hgrn(zhg, lower_bounds[li][None, :], row2(hg_out_norm[li]), tri, lvl, nb, nblk)
        h = _back(o_mla, y_s5, o_hg, zg, h, w_br_mla[li].astype(BF16), w_br_s5[li].astype(BF16),
                  w_br_hg[li].astype(BF16), w_out[li].astype(BF16), row2(ln1_g[li]), row2(ln1_b[li]),
                  w_ffn_gate[li].astype(BF16), w_ffn_up[li].astype(BF16), w_ffn_down[li].astype(BF16),
                  row2(ln2_g[li]), row2(ln2_b[li]), nb, alpha, final=li == depth - 1)
    return h
```

```python
import functools
import math

import jax
import jax.numpy as jnp
import numpy as np
from jax import lax
from jax.experimental import pallas as pl
from jax.experimental.pallas import tpu as pltpu

F32 = jnp.float32
BF16 = jnp.bfloat16

N_META = 16
MLA_HEADS = 8
MLA_NOPE = 64
MLA_ROPE = 32
MLA_V = 64
MLA_Q_RANK = 256
MLA_KV_RANK = 256
ROPE_THETA = 10000.0
MASK_VALUE = -1e9
LOG2_E = math.log2(math.e)
HEAD_PAD = 128
S5_WIDTH = 512
S5_GROUP = 16
S5_GROUPS = S5_WIDTH // S5_GROUP
S5_STATE = 64
S5_SLAB = 128
S5_SLAB_STATE = (S5_SLAB // S5_GROUP) * S5_STATE
HG_HEADS = 4
HG_KEY = 128
HG_VAL = 128
HG_QK = HG_HEADS * HG_KEY
HG_VW = HG_HEADS * HG_VAL
HG_F_MIN = 1e-6
HG_CHUNK = 128
HG_LEVELS = (128, 64, 32, 16)
HG_BOTTOM = 8
N_BRANCH = 3
T_BLK = 16
ATT_TQ = 256
ATT_HEADS_PER_STEP = 4
VMEM_LIMIT = 56 * 1024 * 1024

ZA_W = 1024
ZHG_W = 2048
ZG_W = 3072
ZKR_W = 128
Z_W = ZA_W + ZHG_W + ZG_W + ZKR_W
WIDE_CHUNK = 512
FILL_BEFORE_MLA = 3


def _cparams(n_grid, sem="parallel"):
    return pltpu.CompilerParams(dimension_semantics=(sem,) * n_grid, vmem_limit_bytes=VMEM_LIMIT)


def _const_spec(shape, layer=None):
    nd = len(shape)
    if layer is None:
        return pl.BlockSpec(shape, lambda *_: (0,) * nd, pipeline_mode=pl.Buffered(1))
    return pl.BlockSpec((None,) + tuple(shape), lambda *_: (layer,) + (0,) * nd, pipeline_mode=pl.Buffered(1))


def _seq_spec(nblk, width, col):
    return pl.BlockSpec((nblk, None, T_BLK, width), lambda b, *g: (0, b, 0, col(b, *g)))


def _dot(a, b):
    return jnp.dot(a, b, preferred_element_type=F32)


def _dot_nt(a, b):
    return lax.dot_general(a, b, (((1,), (1,)), ((), ())), preferred_element_type=F32)


def _dot_tn(a, b):
    return lax.dot_general(a, b, (((0,), (0,)), ((), ())), preferred_element_type=F32)


def _sigmoid(x):
    return 1.0 / (1.0 + jnp.exp(-x))


def _layer_norm(x, g, b, eps=1e-5):
    mu = jnp.mean(x, axis=-1, keepdims=True)
    xc = x - mu
    var = jnp.mean(xc * xc, axis=-1, keepdims=True)
    return xc * lax.rsqrt(var + eps) * g + b


def _rms_norm(x, g, eps=1e-6):
    return x * lax.rsqrt(jnp.mean(x * x, axis=-1, keepdims=True) + eps) * g


def _load_rows(ref, blk0, n_blk, cs):
    x = ref[pl.ds(blk0, n_blk), :, cs]
    return x.reshape(n_blk * T_BLK, x.shape[-1])


def _store_rows(ref, blk0, n_blk, cs, x):
    ref[pl.ds(blk0, n_blk), :, cs] = x.reshape(n_blk, T_BLK, x.shape[-1])


def _rope_tables(pos_ref, inv_ref, place_ref, base_ref):
    ang = inv_ref[...] * pos_ref[...].astype(F32)

    def place(t):
        hi = t.astype(BF16)
        lo = (t - hi.astype(F32)).astype(BF16)
        return _dot_tn(hi, place_ref[...]) + _dot_tn(lo, place_ref[...])

    return place(jnp.cos(ang)) + base_ref[...], place(jnp.sin(ang))


def _mla_prep(cq, ckv, kr, cos, sin, qn_ref, kvn_ref, wq_ref, wqr_ref, wk_ref, wv_ref, e_ref, er_ref,
              q_ref, k_ref, v_ref):
    scale = (MLA_NOPE + MLA_ROPE) ** -0.5 * LOG2_E
    cqn = _rms_norm(cq, qn_ref[...]).astype(BF16)
    ckvn = _rms_norm(ckv, kvn_ref[...]).astype(BF16)
    kr = kr.astype(BF16)
    cos_q = cos * scale
    sin_q = sin * scale
    for h in range(MLA_HEADS):
        cs = slice(h * HEAD_PAD, (h + 1) * HEAD_PAD)
        qf = _dot(cqn, wq_ref[:, cs])
        qr = _dot(cqn, wqr_ref[:, cs])
        q_ref[:, cs] = (qf * cos_q + qr * sin_q).astype(BF16)
        kf = _dot(ckvn, wk_ref[:, cs]) + _dot(kr, e_ref[:, cs])
        krot = _dot(kr, er_ref[:, cs])
        k_ref[:, cs] = (kf * cos + krot * sin).astype(BF16)
    v_ref[...] = _dot(ckvn, wv_ref[...]).astype(BF16)


def _s5_block(u, perm_ref, bm_ref, cm_ref, ar_ref, ai_ref, d_ref, wg_ref, o_ref,
              xr_ref, xi_ref, buf_ref, y_ref, nb, fillers):
    n_slab = S5_WIDTH // S5_SLAB
    ns = S5_SLAB_STATE
    u = _dot(perm_ref[...], u).astype(BF16)
    for j in range(n_slab):
        buf_ref[...] = _dot(u[:, j * S5_SLAB:(j + 1) * S5_SLAB], bm_ref[j])
        if fillers:
            fillers.pop(0)()
        ar = jnp.broadcast_to(ar_ref[j], (nb, ns))
        ai = jnp.broadcast_to(ai_ref[j], (nb, ns))
        xr = xr_ref[j]
        xi = xi_ref[j]
        for t in range(T_BLK):
            rows = slice(t * nb, (t + 1) * nb)
            nr = ar * xr - ai * xi + buf_ref[rows, 0:ns]
            ni = ar * xi + ai * xr + buf_ref[rows, ns:2 * ns]
            buf_ref[rows, 0:ns] = nr
            buf_ref[rows, ns:2 * ns] = ni
            xr, xi = nr, ni
        xr_ref[j] = xr
        xi_ref[j] = xi
        y_ref[:, j * S5_SLAB:(j + 1) * S5_SLAB] = _dot(buf_ref[...].astype(BF16), cm_ref[j])

    while fillers:
        fillers.pop(0)()
    y = y_ref[...] + d_ref[...] * u.astype(F32)
    y = 0.5 * y * (1.0 + jnp.tanh(math.sqrt(2.0 / math.pi) * (y + 0.044715 * (y * y * y))))
    gate = _sigmoid(_dot(y.astype(BF16), wg_ref[...]))
    out = (y * gate).astype(BF16)
    o_ref[...] = _dot_tn(perm_ref[...], out).astype(BF16)


def _front_kernel(*refs, nb, ln_in):
    if ln_in:
        x_ref, meta_ref, lng_ref, lnb_ref, *refs = refs
    else:
        h_ref, *refs = refs
    (w_ref, pos_ref, inv_ref, place_ref, base_ref, qn_ref, kvn_ref, wq_ref, wqr_ref, wk_ref, wv_ref, e_ref, er_ref,
     perm_ref, bm_ref, cm_ref, ar_ref, ai_ref, d_ref, wglu_ref, *refs) = refs
    if ln_in:
        hout_ref, *refs = refs
    zhg_ref, zg_ref, q_ref, k_ref, v_ref, ys5_ref, xr_ref, xi_ref, buf_ref, y_ref = refs

    @pl.when(pl.program_id(0) == 0)
    def _():
        xr_ref[...] = jnp.zeros_like(xr_ref)
        xi_ref[...] = jnp.zeros_like(xi_ref)

    cos, sin = _rope_tables(pos_ref, inv_ref, place_ref, base_ref)
    if ln_in:
        hx = _layer_norm(x_ref[...], lng_ref[...], lnb_ref[...])
        hm = _layer_norm(meta_ref[...], lng_ref[...], lnb_ref[...])
        h = jnp.where(pl.program_id(0) == 0, jnp.broadcast_to(hm[None], hx.shape), hx)
        h = h.reshape(hout_ref.shape)
        hout_ref[...] = h
    else:
        h = h_ref[...]
    x = h.astype(BF16)
    za = _dot(x, w_ref[:, 0:ZA_W])
    zkr = _dot(x, w_ref[:, ZA_W + ZHG_W + ZG_W:Z_W])

    def wide_chunk(o_ref, c, off):
        def run():
            o_ref[:, c:c + WIDE_CHUNK] = _dot(x, w_ref[:, off + c:off + c + WIDE_CHUNK]).astype(BF16)
        return run

    fillers = [wide_chunk(zhg_ref, c, ZA_W) for c in range(0, ZHG_W, WIDE_CHUNK)]
    fillers += [wide_chunk(zg_ref, c, ZA_W + ZHG_W) for c in range(0, ZG_W, WIDE_CHUNK)]
    for _ in range(FILL_BEFORE_MLA):
        fillers.pop(0)()
    c0 = S5_WIDTH
    _mla_prep(za[:, c0:c0 + MLA_Q_RANK], za[:, c0 + MLA_Q_RANK:c0 + MLA_Q_RANK + MLA_KV_RANK], zkr,
              cos, sin, qn_ref, kvn_ref, wq_ref, wqr_ref, wk_ref, wv_ref, e_ref, er_ref,
              q_ref, k_ref, v_ref)
    _s5_block(za[:, 0:S5_WIDTH].astype(BF16), perm_ref, bm_ref, cm_ref, ar_ref, ai_ref, d_ref, wglu_ref, ys5_ref,
              xr_ref, xi_ref, buf_ref, y_ref, nb, fillers)


def _front(src, w, pos, rope, qn, kvn, wq, wqr, wk, wv, e, er, perm, bm, cm, ar, ai, dskip, wglu, nb, li):
    ln_in = isinstance(src, tuple)
    rb = T_BLK * nb
    if ln_in:
        x, meta, lng, lnb = src
        d = x.shape[-1]
        r = (x.shape[1] + N_META) * nb
        src_specs = [pl.BlockSpec((nb, T_BLK, d), lambda i: (0, jnp.maximum(i - 1, 0), 0)),
                     _const_spec((N_META, d)), _const_spec((1, d)), _const_spec((1, d))]
    else:
        src = (src,)
        r, d = src[0].shape
        src_specs = [pl.BlockSpec((rb, d), lambda i: (i, 0))]
    hw = MLA_HEADS * HEAD_PAD
    vw = MLA_HEADS * MLA_V
    n_slab = S5_WIDTH // S5_SLAB
    row = lambda width: pl.BlockSpec((rb, width), lambda i: (i, 0))
    widths = (ZHG_W, ZG_W, hw, hw, vw, S5_WIDTH)
    out_shape = [jax.ShapeDtypeStruct((r, wd), BF16) for wd in widths]
    out_specs = [row(wd) for wd in widths]
    if ln_in:
        out_shape.insert(0, jax.ShapeDtypeStruct((r, d), F32))
        out_specs.insert(0, row(d))
    return pl.pallas_call(
        functools.partial(_front_kernel, nb=nb, ln_in=ln_in),
        out_shape=out_shape,
        grid=(r // rb,),
        in_specs=src_specs + [
            _const_spec((d, Z_W), li), pl.BlockSpec((None, 1, rb), lambda i: (i, 0, 0)),
            _const_spec((MLA_ROPE // 2, 1)), _const_spec((MLA_ROPE // 2, HEAD_PAD)), _const_spec((1, HEAD_PAD)),
            _const_spec((1, MLA_Q_RANK), li), _const_spec((1, MLA_KV_RANK), li),
            _const_spec((MLA_Q_RANK, hw), li), _const_spec((MLA_Q_RANK, hw), li), _const_spec((MLA_KV_RANK, hw), li),
            _const_spec((MLA_KV_RANK, vw), li), _const_spec((ZKR_W, hw), li), _const_spec((ZKR_W, hw), li),
            _const_spec((rb, rb)),
            _const_spec((n_slab, S5_SLAB, 2 * S5_SLAB_STATE), li),
            _const_spec((n_slab, 2 * S5_SLAB_STATE, S5_SLAB), li),
            _const_spec((n_slab, 1, S5_SLAB_STATE), li),
            _const_spec((n_slab, 1, S5_SLAB_STATE), li),
            _const_spec((1, S5_WIDTH), li),
            _const_spec((S5_WIDTH, S5_WIDTH), li),
        ],
        out_specs=out_specs,
        scratch_shapes=[
            pltpu.VMEM((n_slab, nb, S5_SLAB_STATE), F32),
            pltpu.VMEM((n_slab, nb, S5_SLAB_STATE), F32),
            pltpu.VMEM((rb, 2 * S5_SLAB_STATE), F32),
            pltpu.VMEM((rb, S5_WIDTH), F32),
        ],
        compiler_params=_cparams(1, "arbitrary"),
        name="front",
    )(*src, w, pos, *rope, qn, kvn, wq, wqr, wk, wv, e, er, perm, bm, cm, ar, ai, dskip, wglu)


def _attn_kernel(tab_ref, q_ref, k_ref, v_ref, o_ref, vt_ref, qt_ref, m_ref, l_ref, acc_ref, *, n_heads, tq, nq):
    bpq = tq // T_BLK
    vw = n_heads * MLA_V

    for j in range(nq):
        vb = _load_rows(v_ref, 1 + j * bpq, bpq, slice(0, vw))
        vt_ref[j] = vb.astype(F32).T.astype(BF16)
    v0t = v_ref[0].astype(F32).T.astype(BF16)

    for i in range(nq):
        qb = _load_rows(q_ref, 1 + i * bpq, bpq, slice(0, n_heads * HEAD_PAD))
        qt_ref[i] = qb.astype(F32).T.astype(BF16)

    def causal(st):
        r = lax.broadcasted_iota(jnp.int32, st.shape, 0)
        c = lax.broadcasted_iota(jnp.int32, st.shape, 1)
        return jnp.where(r <= c, st, MASK_VALUE * LOG2_E)

    heads = range(n_heads)
    qs = [slice(h * HEAD_PAD, (h + 1) * HEAD_PAD) for h in heads]
    vs = [slice(h * MLA_V, (h + 1) * MLA_V) for h in heads]
    k0 = [k_ref[0, :, c] for c in qs]

    outs = []
    for n in heads:
        st = causal(_dot_nt(k0[n], q_ref[0, :, qs[n]]))
        p = jnp.exp2(st - jnp.max(st, axis=0, keepdims=True))
        outs.append(_dot(v0t[vs[n], :], p.astype(BF16)) / jnp.sum(p, axis=0, keepdims=True))
    o_ref[0, :, :] = jnp.concatenate(outs, axis=0).T.astype(BF16)

    def key_tile(j, n):
        return _load_rows(k_ref, 1 + j * bpq, bpq, qs[n])

    def diag_scores(i):
        return tuple((_dot(key_tile(i, n), qt_ref[i, qs[n], :]), _dot(k0[n], qt_ref[i, qs[n], :])) for n in heads)

    def diag_stats(i, x):
        out = []
        for n in heads:
            st, sm = causal(x[n][0]), x[n][1]
            m = jnp.maximum(jnp.max(st, axis=0, keepdims=True), jnp.max(sm, axis=0, keepdims=True))
            p = jnp.exp2(st - m)
            pm = jnp.exp2(sm - m)
            m_ref[i, n] = m
            l_ref[i, n] = jnp.sum(p, axis=0, keepdims=True) + jnp.sum(pm, axis=0, keepdims=True)
            out.append((p.astype(BF16), pm.astype(BF16)))
        return tuple(out)

    def diag_values(i, y):
        for n in heads:
            acc_ref[i, vs[n], :] = _dot(vt_ref[i, vs[n], :], y[n][0]) + _dot(v0t[vs[n], :], y[n][1])

    _pipeline3(nq, diag_scores, diag_stats, diag_values)

    def off_scores(f):
        i, j = tab_ref[0, f], tab_ref[1, f]
        return tuple(_dot(key_tile(j, n), qt_ref[i, qs[n], :]) for n in heads)

    def off_stats(f, x):
        i = tab_ref[0, f]
        out = []
        for n in heads:
            m_old = m_ref[i, n]
            m = jnp.maximum(m_old, jnp.max(x[n], axis=0, keepdims=True))
            a = jnp.exp2(m_old - m)
            p = jnp.exp2(x[n] - m)
            m_ref[i, n] = m
            l_ref[i, n] = a * l_ref[i, n] + jnp.sum(p, axis=0, keepdims=True)
            out.append((a, p.astype(BF16)))
        return tuple(out)

    def off_values(f, y):
        i, j = tab_ref[0, f], tab_ref[1, f]
        for n in heads:
            acc_ref[i, vs[n], :] = y[n][0] * acc_ref[i, vs[n], :] + _dot(vt_ref[j, vs[n], :], y[n][1])

    _pipeline3(nq * (nq - 1) // 2, off_scores, off_stats, off_values)

    for i in range(nq):
        o = jnp.concatenate([acc_ref[i, vs[n], :] / l_ref[i, n] for n in heads], axis=0)
        _store_rows(o_ref, 1 + i * bpq, bpq, slice(0, vw), o.T.astype(BF16))


def _pipeline3(n, stage1, stage2, stage3):
    if n == 0:
        return
    x0 = stage1(0)
    x1 = stage1(1) if n > 1 else None
    y = stage2(0, x0)
    if n > 2:
        def body(t, carry):
            y_t, x_t1 = carry
            x_t2 = stage1(t + 2)
            y_t1 = stage2(t + 1, x_t1)
            stage3(t, y_t)
            return y_t1, x_t2

        y, x1 = lax.fori_loop(0, n - 2, body, (y, x1))
    if n > 1:
        stage3(n - 2, y)
        y = stage2(n - 1, x1)
    stage3(n - 1, y)


def _attention(q, k, v, nb, nblk):
    hps = ATT_HEADS_PER_STEP
    n_hp = MLA_HEADS // hps
    qw = hps * HEAD_PAD
    vw = hps * MLA_V
    s = (nblk - 1) * T_BLK
    tq = min(ATT_TQ, s)
    nq = s // tq
    col = lambda b, p: p
    pairs = [(i, j) for i in range(nq) for j in range(i)] or [(0, 0)]
    tab = jnp.asarray(np.array(pairs, np.int32).T)
    o = pl.pallas_call(
        functools.partial(_attn_kernel, n_heads=hps, tq=tq, nq=nq),
        out_shape=jax.ShapeDtypeStruct((nblk, nb, T_BLK, MLA_HEADS * MLA_V), BF16),
        grid=(nb, n_hp),
        in_specs=[pl.BlockSpec(memory_space=pltpu.SMEM),
                  _seq_spec(nblk, qw, col), _seq_spec(nblk, qw, col), _seq_spec(nblk, vw, col)],
        out_specs=_seq_spec(nblk, vw, col),
        scratch_shapes=[pltpu.VMEM((nq, vw, tq), BF16), pltpu.VMEM((nq, qw, tq), BF16),
                        pltpu.VMEM((nq, hps, 1, tq), F32), pltpu.VMEM((nq, hps, 1, tq), F32),
                        pltpu.VMEM((nq, vw, tq), F32)],
        compiler_params=_cparams(2),
        name="mla_attn",
    )(tab, q.reshape(nblk, nb, T_BLK, -1), k.reshape(nblk, nb, T_BLK, -1), v.reshape(nblk, nb, T_BLK, -1))
    return o.reshape(nblk * nb * T_BLK, MLA_HEADS * MLA_V)


def _time_major_perm(nb):
    p = np.zeros((T_BLK * nb, T_BLK * nb), np.float32)
    for b in range(nb):
        for t in range(T_BLK):
            p[t * nb + b, b * T_BLK + t] = 1.0
    return p


def _block_row_bcast(x, m, row):
    t, c = x.shape
    if m == t:
        return jnp.broadcast_to(x[row:row + 1, :], x.shape)
    x3 = x.reshape(t // m, m, c)
    return jnp.broadcast_to(x3[:, row:row + 1, :], x3.shape).reshape(t, c)


def _hgrn_gates(blk0, n_blk, f_ref, lb_ref):
    lb = lb_ref[...]
    zf = _load_rows(f_ref, blk0, n_blk, slice(0, HG_QK)).astype(F32)
    e = jnp.exp(-jnp.abs(zf))
    rcp = 1.0 / (1.0 + e)
    pos = zf >= 0.0
    sig_p = jnp.where(pos, rcp, e * rcp)
    sig_n = jnp.where(pos, e * rcp, rcp)
    f = lb + (1.0 - lb) * sig_p
    log_f = jnp.log2(jnp.maximum(f, HG_F_MIN))
    k = (1.0 - lb) * sig_n
    hi = log_f.astype(BF16)
    lo = (log_f - hi.astype(F32)).astype(BF16)
    return k, hi, lo


def _hgrn_mix(blk0, n_blk, gates, q_ref, v_ref, tri_ref, lvl_ref, st_ref):
    t = n_blk * T_BLK
    cs = slice(0, HG_QK)
    heads = [slice(h * HG_KEY, (h + 1) * HG_KEY) for h in range(HG_HEADS)]
    k, hi, lo = gates
    q = _load_rows(q_ref, blk0, n_blk, cs).astype(F32)
    v = _load_rows(v_ref, blk0, n_blk, cs)
    tri = tri_ref[0:t, 0:t]
    cum = _dot(tri, hi) + _dot(tri, lo)
    lvl = lvl_ref[0:t, 0:t]
    r_idx = lax.broadcasted_iota(jnp.int32, (t, HG_QK), 0)
    c8 = _block_row_bcast(cum, HG_BOTTOM, HG_BOTTOM // 2 - 1)
    qe = (q * jnp.exp2(cum - c8)).astype(BF16)
    ke = (k * jnp.exp2(c8 - cum)).astype(BF16)
    n_lvl = len(HG_LEVELS)
    scores = [jnp.where(lvl == n_lvl, _dot_nt(qe[:, hs], ke[:, hs]), 0.0) for hs in heads]
    for li, m in enumerate(HG_LEVELS):
        if m > t:
            continue
        half = m // 2
        cmid = _block_row_bcast(cum, m, half - 1)
        upper = (r_idx & (m - 1)) >= half
        ex = jnp.exp2(jnp.where(upper, cum - cmid, cmid - cum))
        qe = jnp.where(upper, q * ex, 0.0).astype(BF16)
        ke = jnp.where(upper, 0.0, k * ex).astype(BF16)
        scores = [jnp.where(lvl == li, _dot_nt(qe[:, hs], ke[:, hs]), s) for s, hs in zip(scores, heads)]
    qd = (q * jnp.exp2(cum)).astype(BF16)
    last = cum[t - 1:t, :]
    kd = (k * jnp.exp2(last - cum)).astype(BF16)
    dec = jnp.exp2(last)
    outs = []
    for h, hs in enumerate(heads):
        st = st_ref[h]
        outs.append(_dot(scores[h].astype(BF16), v[:, hs]) + _dot_nt(qd[:, hs], st.astype(BF16)))
        st_ref[h] = st * dec[:, hs] + _dot_tn(v[:, hs], kd[:, hs])
    return tuple(outs)


def _hgrn_out(blk0, n_blk, outs, g_ref, on_ref, o_ref):
    cs = slice(0, HG_VW)
    g = _load_rows(g_ref, blk0, n_blk, cs).astype(F32)
    o = jnp.concatenate([o * lax.rsqrt(jnp.mean(o * o, axis=-1, keepdims=True) + 1e-6) for o in outs], axis=1)
    _store_rows(o_ref, blk0, n_blk, cs, (o * on_ref[...] * (g * _sigmoid(g))).astype(BF16))


def _hgrn_kernel(q_ref, f_ref, v_ref, g_ref, lb_ref, on_ref, tri_ref, lvl_ref, o_ref, st_ref, *, n_chunks):
    bpc = HG_CHUNK // T_BLK
    gates = functools.partial(_hgrn_gates, f_ref=f_ref, lb_ref=lb_ref)
    mix = functools.partial(_hgrn_mix, q_ref=q_ref, v_ref=v_ref, tri_ref=tri_ref, lvl_ref=lvl_ref, st_ref=st_ref)
    out = functools.partial(_hgrn_out, g_ref=g_ref, on_ref=on_ref, o_ref=o_ref)
    st_ref[...] = jnp.zeros_like(st_ref)
    out(0, 1, mix(0, 1, gates(0, 1)))
    _pipeline3(n_chunks,
               lambda c: gates(1 + c * bpc, bpc),
               lambda c, x: mix(1 + c * bpc, bpc, x),
               lambda c, y: out(1 + c * bpc, bpc, y))


def _hgrn(zhg, lb, onorm, tri, lvl, nb, nblk, li):
    z4 = zhg.reshape(nblk, nb, T_BLK, ZHG_W)
    col = lambda j: _seq_spec(nblk, HG_QK, lambda b: j)
    o = pl.pallas_call(
        functools.partial(_hgrn_kernel, n_chunks=(nblk - 1) * T_BLK // HG_CHUNK),
        out_shape=jax.ShapeDtypeStruct((nblk, nb, T_BLK, HG_VW), BF16),
        grid=(nb,),
        in_specs=[col(0), col(1), col(2), col(3),
                  _const_spec((1, HG_QK), li), _const_spec((1, HG_VW), li),
                  _const_spec((HG_CHUNK, HG_CHUNK)), _const_spec((HG_CHUNK, HG_CHUNK))],
        out_specs=_seq_spec(nblk, HG_VW, lambda b: 0),
        scratch_shapes=[pltpu.VMEM((HG_HEADS, HG_VAL, HG_KEY), F32)],
        compiler_params=_cparams(1),
        name="hgrn2",
    )(z4, z4, z4, z4, lb, onorm, tri, lvl)
    return o.reshape(nblk * nb * T_BLK, HG_VW)


def _hgrn_level_matrix():
    r = np.arange(HG_CHUNK)[:, None]
    c = np.arange(HG_CHUNK)[None, :]
    lvl = np.zeros((HG_CHUNK, HG_CHUNK), np.int32)
    for li, m in enumerate(HG_LEVELS):
        lvl = np.where(r // m == c // m, li, lvl)
    lvl = np.where(r // HG_BOTTOM == c // HG_BOTTOM, len(HG_LEVELS), lvl)
    return np.where(c <= r, lvl, -1).astype(np.int32)


def _back_kernel(om_ref, os_ref, oh_ref, gm_ref, gs_ref, gh_ref, h_ref,
                 wm_ref, ws_ref, wh_ref, wo_ref, g1_ref, b1_ref,
                 wg_ref, wu_ref, wd_ref, g2_ref, b2_ref, o_ref, r1_ref, r2_ref, *, alpha):
    @pl.when(pl.program_id(0) == 0)
    def _():
        r1_ref[...] = jnp.zeros_like(r1_ref)
        r2_ref[...] = jnp.zeros_like(r2_ref)

    r1 = r1_ref[...]
    r2 = r2_ref[...]
    ym = _dot(om_ref[...], wm_ref[...])
    ys = _dot(os_ref[...], ws_ref[...])
    yh = _dot(oh_ref[...], wh_ref[...])
    o_ref[...] = _layer_norm(r2, g2_ref[...], b2_ref[...]).reshape(o_ref.shape)
    h1 = _layer_norm(r1, g1_ref[...], b1_ref[...])
    hb = h1.astype(BF16)
    a = _dot(hb, wg_ref[...])
    u = _dot(hb, wu_ref[...])
    mixed = _sigmoid(gm_ref[...].astype(F32)) * ym
    mixed += _sigmoid(gs_ref[...].astype(F32)) * ys
    mixed += _sigmoid(gh_ref[...].astype(F32)) * yh
    r1_ref[...] = alpha * h_ref[...] + _dot(mixed.astype(BF16), wo_ref[...])
    r2_ref[...] = alpha * h1 + _dot((a * _sigmoid(a) * u).astype(BF16), wd_ref[...])


def _back(om, os_, oh, zg, h, wm, ws, wh, wo, g1, b1, wg, wu, wd, g2, b2, nb, alpha, final, li):
    r, d = h.shape
    bw = om.shape[1]
    dff = wg.shape[-1]
    rb = T_BLK * nb
    skip = 1 if final else 0
    n_blk = r // rb - skip
    lag = 2
    row = lambda width, j=0: pl.BlockSpec((rb, width), lambda i: (jnp.minimum(i, n_blk - 1) + skip, j))
    if final:
        out_shape = jax.ShapeDtypeStruct((nb, n_blk * T_BLK, d), F32)
        out_spec = pl.BlockSpec((nb, T_BLK, d), lambda i: (0, jnp.maximum(i - lag, 0), 0))
    else:
        out_shape = jax.ShapeDtypeStruct((r, d), F32)
        out_spec = pl.BlockSpec((rb, d), lambda i: (jnp.maximum(i - lag, 0), 0))
    return pl.pallas_call(
        functools.partial(_back_kernel, alpha=alpha),
        out_shape=out_shape,
        grid=(n_blk + lag,),
        in_specs=[row(bw), row(bw), row(bw), row(d, 0), row(d, 1), row(d, 2), row(d),
                  _const_spec((bw, d), li), _const_spec((bw, d), li), _const_spec((bw, d), li),
                  _const_spec((d, d), li), _const_spec((1, d), li), _const_spec((1, d), li),
                  _const_spec((d, dff), li), _const_spec((d, dff), li), _const_spec((dff, d), li),
                  _const_spec((1, d), li), _const_spec((1, d), li)],
        out_specs=out_spec,
        scratch_shapes=[pltpu.VMEM((rb, d), F32), pltpu.VMEM((rb, d), F32)],
        compiler_params=_cparams(1, "arbitrary"),
        name="merge_ffn",
    )(om, os_, oh, zg, zg, zg, h, wm, ws, wh, wo, g1, b1, wg, wu, wd, g2, b2)


def _permute_w_in(w):
    mla_in = MLA_Q_RANK + MLA_KV_RANK + MLA_ROPE
    s5_0, hg_0 = mla_in, mla_in + S5_WIDTH
    g_0 = hg_0 + 2 * HG_QK + 2 * HG_VW
    pad = jnp.zeros((w.shape[0], ZKR_W - MLA_ROPE), w.dtype)
    return jnp.concatenate([w[:, s5_0:hg_0], w[:, 0:MLA_Q_RANK + MLA_KV_RANK], w[:, hg_0:g_0], w[:, g_0:],
                            w[:, MLA_Q_RANK + MLA_KV_RANK:mla_in], pad], axis=1).astype(BF16)


def _rot_half(x):
    x1, x2 = jnp.split(x, 2, axis=-1)
    return jnp.concatenate([-x2, x1], axis=-1)


def _mla_weights(w_uq, w_ukv):
    rq, rkv = w_uq.shape[0], w_ukv.shape[0]
    zpad = HEAD_PAD - MLA_NOPE - MLA_ROPE
    wq = w_uq.reshape(rq, MLA_HEADS, MLA_NOPE + MLA_ROPE)
    q_nope, q_rope = wq[..., :MLA_NOPE], wq[..., MLA_NOPE:]
    zq = jnp.zeros((rq, MLA_HEADS, zpad), w_uq.dtype)
    wq_p = jnp.concatenate([q_nope, q_rope, zq], axis=-1).reshape(rq, -1)
    wq_r = jnp.concatenate([jnp.zeros_like(q_nope), _rot_half(q_rope), zq], axis=-1).reshape(rq, -1)
    wkv = w_ukv.reshape(rkv, MLA_HEADS, MLA_NOPE + MLA_V)
    zk = jnp.zeros((rkv, MLA_HEADS, HEAD_PAD - MLA_NOPE), w_ukv.dtype)
    wk_p = jnp.concatenate([wkv[..., :MLA_NOPE], zk], axis=-1).reshape(rkv, -1)
    wv = wkv[..., MLA_NOPE:].reshape(rkv, -1)
    eye = jnp.eye(ZKR_W, MLA_ROPE, dtype=F32)
    place = jnp.concatenate([jnp.zeros((ZKR_W, MLA_NOPE), F32), eye, jnp.zeros((ZKR_W, zpad), F32)], axis=-1)
    place_r = jnp.concatenate([jnp.zeros((ZKR_W, MLA_NOPE), F32), _rot_half(eye),
                               jnp.zeros((ZKR_W, zpad), F32)], axis=-1)
    e = jnp.tile(place, (1, MLA_HEADS))
    er = jnp.tile(place_r, (1, MLA_HEADS))
    return [a.astype(BF16) for a in (wq_p, wq_r, wk_p, wv, e, er)]


def _rope_consts():
    half = MLA_ROPE // 2
    inv = ROPE_THETA ** (-(jnp.arange(0, MLA_ROPE, 2, dtype=F32) / MLA_ROPE))
    place = np.zeros((half, HEAD_PAD), np.float32)
    place[np.arange(half), MLA_NOPE + np.arange(half)] = 1.0
    place[np.arange(half), MLA_NOPE + half + np.arange(half)] = 1.0
    base = np.zeros((1, HEAD_PAD), np.float32)
    base[0, :MLA_NOPE] = 1.0
    return inv[:, None], jnp.asarray(place, BF16), jnp.asarray(base)


def _s5_params(lam_re, lam_im, log_dt, b_re, b_im, c_re, c_im):
    lr = jnp.minimum(lam_re.astype(F32), -1e-4)
    li = lam_im.astype(F32)
    dt = jnp.exp(log_dt.astype(F32))[:, None]
    mag = jnp.exp(lr * dt)
    ab_r = mag * jnp.cos(li * dt)
    ab_i = mag * jnp.sin(li * dt)
    den = lr * lr + li * li
    nr = ab_r - 1.0
    coef_r = ((nr * lr + ab_i * li) / den)[..., None]
    coef_i = ((ab_i * lr - nr * li) / den)[..., None]
    bb_r = coef_r * b_re.astype(F32) - coef_i * b_im.astype(F32)
    bb_i = coef_r * b_im.astype(F32) + coef_i * b_re.astype(F32)
    n_slab = S5_WIDTH // S5_SLAB
    gps = S5_SLAB // S5_GROUP
    eye = jnp.eye(gps, dtype=F32)

    def in_mat(bb):
        b4 = bb.reshape(n_slab, gps, S5_STATE, S5_GROUP)
        return jnp.einsum('jgnc,gh->jgchn', b4, eye).reshape(n_slab, S5_SLAB, S5_SLAB_STATE)

    def out_mat(cc):
        c4 = cc.astype(F32).reshape(n_slab, gps, S5_GROUP, S5_STATE)
        return jnp.einsum('jgcn,gh->jgnhc', c4, eye).reshape(n_slab, S5_SLAB_STATE, S5_SLAB)

    bm = jnp.concatenate([in_mat(bb_r), in_mat(bb_i)], axis=2).astype(BF16)
    cm = jnp.concatenate([out_mat(c_re), -out_mat(c_im)], axis=1).astype(BF16)
    ar = ab_r.reshape(n_slab, 1, S5_SLAB_STATE)
    ai = ab_i.reshape(n_slab, 1, S5_SLAB_STATE)
    return bm, cm, ar, ai


def kernel(x, positions, meta_tokens, ln_in_g, ln_in_b, w_in, mla_q_norm, mla_w_uq, mla_kv_norm, mla_w_ukv,
           s5_lam_re, s5_lam_im, s5_log_dt, s5_b_re, s5_b_im, s5_c_re, s5_c_im, s5_d, s5_w_glu,
           hg_lb_logits, hg_out_norm, w_br_mla, w_br_s5, w_br_hg, w_out, ln1_g, ln1_b,
           w_ffn_gate, w_ffn_up, w_ffn_down, ln2_g, ln2_b):
    nb, s, d = x.shape
    depth = w_in.shape[0]
    nblk = (s + N_META) // T_BLK
    alpha = (2 * depth) ** 0.25
    row2 = lambda a: a.astype(F32)[None, :]
    rows3 = lambda a: a.astype(F32)[:, None, :]
    bf16 = lambda a: a.astype(BF16)

    meta_pos = jnp.broadcast_to(jnp.arange(N_META, dtype=jnp.int32)[None, :], (nb, N_META))
    pos = jnp.concatenate([meta_pos, positions.astype(jnp.int32) + N_META], axis=1)
    pos = pos.reshape(nb, nblk, T_BLK).transpose(1, 0, 2).reshape(nblk, 1, nb * T_BLK)
    rope = _rope_consts()
    p_lb = jax.nn.softmax(hg_lb_logits.astype(F32), axis=0)
    lower_bounds = jnp.cumsum(p_lb, axis=0) - p_lb[0]
    tri = jnp.asarray(np.tril(np.ones((HG_CHUNK, HG_CHUNK), np.float32)), BF16)
    lvl = jnp.asarray(_hgrn_level_matrix())
    perm = jnp.asarray(_time_major_perm(nb), BF16)

    front_p = (jax.vmap(_permute_w_in)(w_in), pos, rope, rows3(mla_q_norm), rows3(mla_kv_norm),
               *jax.vmap(_mla_weights)(mla_w_uq, mla_w_ukv), perm,
               *jax.vmap(_s5_params)(s5_lam_re, s5_lam_im, s5_log_dt, s5_b_re, s5_b_im, s5_c_re, s5_c_im),
               rows3(s5_d), bf16(s5_w_glu))
    hgrn_p = (lower_bounds[:, None, :], rows3(hg_out_norm), tri, lvl)
    back_p = (bf16(w_br_mla), bf16(w_br_s5), bf16(w_br_hg), bf16(w_out), rows3(ln1_g), rows3(ln1_b),
              bf16(w_ffn_gate), bf16(w_ffn_up), bf16(w_ffn_down), rows3(ln2_g), rows3(ln2_b))

    h = (x, meta_tokens.astype(x.dtype), row2(ln_in_g), row2(ln_in_b))
    for li in range(depth):
        outs = _front(h, *front_p, nb, li)
        if li == 0:
            h, *outs = outs
        zhg, zg, q, k, v, y_s5 = outs
        o_mla = _attention(q, k, v, nb, nblk)
        o_hg = _hgrn(zhg, *hgrn_p, nb, nblk, li)
        h = _back(o_mla, y_s5, o_hg, zg, h, *back_p, nb, alpha, final=li == depth - 1, li=li)
    return h
```

```python
import functools
import math

import jax
import jax.numpy as jnp
import numpy as np
from jax import lax
from jax.experimental import pallas as pl
from jax.experimental.pallas import tpu as pltpu

F32 = jnp.float32
BF16 = jnp.bfloat16

N_META = 16
MLA_HEADS = 8
MLA_NOPE = 64
MLA_ROPE = 32
MLA_V = 64
MLA_Q_RANK = 256
MLA_KV_RANK = 256
ROPE_THETA = 10000.0
MASK_VALUE = -1e9
LOG2_E = math.log2(math.e)
HEAD_PAD = 128
S5_WIDTH = 512
S5_GROUP = 16
S5_GROUPS = S5_WIDTH // S5_GROUP
S5_STATE = 64
S5_SLAB = 128
S5_SLAB_STATE = (S5_SLAB // S5_GROUP) * S5_STATE
HG_HEADS = 4
HG_KEY = 128
HG_VAL = 128
HG_QK = HG_HEADS * HG_KEY
HG_VW = HG_HEADS * HG_VAL
HG_F_MIN = 1e-6
HG_CHUNK = 128
HG_LEVELS = (128, 64, 32, 16)
HG_BOTTOM = 8
N_BRANCH = 3
T_BLK = 16
ATT_TQ = 256
ATT_HEADS_PER_STEP = 4
VMEM_LIMIT = 56 * 1024 * 1024

ZA_W = 1024
ZHG_W = 2048
ZG_W = 3072
ZKR_W = 128
Z_W = ZA_W + ZHG_W + ZG_W + ZKR_W
WIDE_CHUNK = 512
FILL_BEFORE_MLA = 3


def _cparams(n_grid, sem="parallel"):
    return pltpu.CompilerParams(dimension_semantics=(sem,) * n_grid, vmem_limit_bytes=VMEM_LIMIT)


def _const_spec(shape, layer=None):
    nd = len(shape)
    if layer is None:
        return pl.BlockSpec(shape, lambda *_: (0,) * nd, pipeline_mode=pl.Buffered(1))
    return pl.BlockSpec((None,) + tuple(shape), lambda *_: (layer,) + (0,) * nd, pipeline_mode=pl.Buffered(1))


def _seq_spec(nblk, width, col):
    return pl.BlockSpec((nblk, None, T_BLK, width), lambda b, *g: (0, b, 0, col(b, *g)))


def _dot(a, b):
    return jnp.dot(a, b, preferred_element_type=F32)


def _dot_nt(a, b):
    return lax.dot_general(a, b, (((1,), (1,)), ((), ())), preferred_element_type=F32)


def _dot_tn(a, b):
    return lax.dot_general(a, b, (((0,), (0,)), ((), ())), preferred_element_type=F32)


def _sigmoid(x):
    return 1.0 / (1.0 + jnp.exp(-x))


def _layer_norm(x, g, b, eps=1e-5):
    mu = jnp.mean(x, axis=-1, keepdims=True)
    xc = x - mu
    var = jnp.mean(xc * xc, axis=-1, keepdims=True)
    return xc * lax.rsqrt(var + eps) * g + b


def _rms_norm(x, g, eps=1e-6):
    return x * lax.rsqrt(jnp.mean(x * x, axis=-1, keepdims=True) + eps) * g


def _load_rows(ref, blk0, n_blk, cs):
    x = ref[pl.ds(blk0, n_blk), :, cs]
    return x.reshape(n_blk * T_BLK, x.shape[-1])


def _store_rows(ref, blk0, n_blk, cs, x):
    ref[pl.ds(blk0, n_blk), :, cs] = x.reshape(n_blk, T_BLK, x.shape[-1])


def _rope_tables(pos_ref, inv_ref, place_ref, base_ref):
    ang = inv_ref[...] * pos_ref[...].astype(F32)

    def place(t):
        hi = t.astype(BF16)
        lo = (t - hi.astype(F32)).astype(BF16)
        return _dot_tn(hi, place_ref[...]) + _dot_tn(lo, place_ref[...])

    return place(jnp.cos(ang)) + base_ref[...], place(jnp.sin(ang))


def _mla_prep(cq, ckv, kr, cos, sin, qn_ref, kvn_ref, wq_ref, wqr_ref, wk_ref, wv_ref, e_ref, er_ref,
              q_ref, k_ref, v_ref):
    scale = (MLA_NOPE + MLA_ROPE) ** -0.5 * LOG2_E
    cqn = _rms_norm(cq, qn_ref[...]).astype(BF16)
    ckvn = _rms_norm(ckv, kvn_ref[...]).astype(BF16)
    kr = kr.astype(BF16)
    cos_q = cos * scale
    sin_q = sin * scale
    for h in range(MLA_HEADS):
        cs = slice(h * HEAD_PAD, (h + 1) * HEAD_PAD)
        qf = _dot(cqn, wq_ref[:, cs])
        qr = _dot(cqn, wqr_ref[:, cs])
        q_ref[:, cs] = (qf * cos_q + qr * sin_q).astype(BF16)
        kf = _dot(ckvn, wk_ref[:, cs]) + _dot(kr, e_ref[:, cs])
        krot = _dot(kr, er_ref[:, cs])
        k_ref[:, cs] = (kf * cos + krot * sin).astype(BF16)
    v_ref[...] = _dot(ckvn, wv_ref[...]).astype(BF16)


def _s5_block(u, perm_ref, bm_ref, cm_ref, ar_ref, ai_ref, d_ref, wg_ref, o_ref,
              xr_ref, xi_ref, buf_ref, y_ref, nb, fillers):
    n_slab = S5_WIDTH // S5_SLAB
    ns = S5_SLAB_STATE
    u = _dot(perm_ref[...], u).astype(BF16)
    for j in range(n_slab):
        buf_ref[...] = _dot(u[:, j * S5_SLAB:(j + 1) * S5_SLAB], bm_ref[j])
        if fillers:
            fillers.pop(0)()
        ar = jnp.broadcast_to(ar_ref[j], (nb, ns))
        ai = jnp.broadcast_to(ai_ref[j], (nb, ns))
        xr = xr_ref[j]
        xi = xi_ref[j]
        for t in range(T_BLK):
            rows = slice(t * nb, (t + 1) * nb)
            nr = ar * xr - ai * xi + buf_ref[rows, 0:ns]
            ni = ar * xi + ai * xr + buf_ref[rows, ns:2 * ns]
            buf_ref[rows, 0:ns] = nr
            buf_ref[rows, ns:2 * ns] = ni
            xr, xi = nr, ni
        xr_ref[j] = xr
        xi_ref[j] = xi
        y_ref[:, j * S5_SLAB:(j + 1) * S5_SLAB] = _dot(buf_ref[...].astype(BF16), cm_ref[j])

    while fillers:
        fillers.pop(0)()
    y = y_ref[...] + d_ref[...] * u.astype(F32)
    y = 0.5 * y * (1.0 + jnp.tanh(math.sqrt(2.0 / math.pi) * (y + 0.044715 * (y * y * y))))
    gate = _sigmoid(_dot(y.astype(BF16), wg_ref[...]))
    out = (y * gate).astype(BF16)
    o_ref[...] = _dot_tn(perm_ref[...], out).astype(BF16)


def _front_kernel(*refs, nb, ln_in):
    if ln_in:
        x_ref, meta_ref, lng_ref, lnb_ref, *refs = refs
    else:
        h_ref, *refs = refs
    (w_ref, pos_ref, inv_ref, place_ref, base_ref, qn_ref, kvn_ref, wq_ref, wqr_ref, wk_ref, wv_ref, e_ref, er_ref,
     perm_ref, bm_ref, cm_ref, ar_ref, ai_ref, d_ref, wglu_ref, *refs) = refs
    if ln_in:
        hout_ref, *refs = refs
    zhg_ref, zg_ref, q_ref, k_ref, v_ref, ys5_ref, xr_ref, xi_ref, buf_ref, y_ref = refs

    @pl.when(pl.program_id(0) == 0)
    def _():
        xr_ref[...] = jnp.zeros_like(xr_ref)
        xi_ref[...] = jnp.zeros_like(xi_ref)

    cos, sin = _rope_tables(pos_ref, inv_ref, place_ref, base_ref)
    if ln_in:
        hx = _layer_norm(x_ref[...], lng_ref[...], lnb_ref[...])
        hm = _layer_norm(meta_ref[...], lng_ref[...], lnb_ref[...])
        h = jnp.where(pl.program_id(0) == 0, jnp.broadcast_to(hm[None], hx.shape), hx)
        h = h.reshape(hout_ref.shape)
        hout_ref[...] = h
    else:
        h = h_ref[...]
    x = h.astype(BF16)
    za = _dot(x, w_ref[:, 0:ZA_W])
    zkr = _dot(x, w_ref[:, ZA_W + ZHG_W + ZG_W:Z_W])

    def wide_chunk(o_ref, c, off):
        def run():
            o_ref[:, c:c + WIDE_CHUNK] = _dot(x, w_ref[:, off + c:off + c + WIDE_CHUNK]).astype(BF16)
        return run

    fillers = [wide_chunk(zhg_ref, c, ZA_W) for c in range(0, ZHG_W, WIDE_CHUNK)]
    fillers += [wide_chunk(zg_ref, c, ZA_W + ZHG_W) for c in range(0, ZG_W, WIDE_CHUNK)]
    for _ in range(FILL_BEFORE_MLA):
        fillers.pop(0)()
    c0 = S5_WIDTH
    _mla_prep(za[:, c0:c0 + MLA_Q_RANK], za[:, c0 + MLA_Q_RANK:c0 + MLA_Q_RANK + MLA_KV_RANK], zkr,
              cos, sin, qn_ref, kvn_ref, wq_ref, wqr_ref, wk_ref, wv_ref, e_ref, er_ref,
              q_ref, k_ref, v_ref)
    _s5_block(za[:, 0:S5_WIDTH].astype(BF16), perm_ref, bm_ref, cm_ref, ar_ref, ai_ref, d_ref, wglu_ref, ys5_ref,
              xr_ref, xi_ref, buf_ref, y_ref, nb, fillers)


def _front(src, w, pos, rope, qn, kvn, wq, wqr, wk, wv, e, er, perm, bm, cm, ar, ai, dskip, wglu, nb, li):
    ln_in = isinstance(src, tuple)
    rb = T_BLK * nb
    if ln_in:
        x, meta, lng, lnb = src
        d = x.shape[-1]
        r = (x.shape[1] + N_META) * nb
        src_specs = [pl.BlockSpec((nb, T_BLK, d), lambda i: (0, jnp.maximum(i - 1, 0), 0)),
                     _const_spec((N_META, d)), _const_spec((1, d)), _const_spec((1, d))]
    else:
        src = (src,)
        r, d = src[0].shape
        src_specs = [pl.BlockSpec((rb, d), lambda i: (i, 0))]
    hw = MLA_HEADS * HEAD_PAD
    vw = MLA_HEADS * MLA_V
    n_slab = S5_WIDTH // S5_SLAB
    row = lambda width: pl.BlockSpec((rb, width), lambda i: (i, 0))
    widths = (ZHG_W, ZG_W, hw, hw, vw, S5_WIDTH)
    out_shape = [jax.ShapeDtypeStruct((r, wd), BF16) for wd in widths]
    out_specs = [row(wd) for wd in widths]
    if ln_in:
        out_shape.insert(0, jax.ShapeDtypeStruct((r, d), F32))
        out_specs.insert(0, row(d))
    return pl.pallas_call(
        functools.partial(_front_kernel, nb=nb, ln_in=ln_in),
        out_shape=out_shape,
        grid=(r // rb,),
        in_specs=src_specs + [
            _const_spec((d, Z_W), li), pl.BlockSpec((None, 1, rb), lambda i: (i, 0, 0)),
            _const_spec((MLA_ROPE // 2, 1)), _const_spec((MLA_ROPE // 2, HEAD_PAD)), _const_spec((1, HEAD_PAD)),
            _const_spec((1, MLA_Q_RANK), li), _const_spec((1, MLA_KV_RANK), li),
            _const_spec((MLA_Q_RANK, hw), li), _const_spec((MLA_Q_RANK, hw), li), _const_spec((MLA_KV_RANK, hw), li),
            _const_spec((MLA_KV_RANK, vw), li), _const_spec((ZKR_W, hw), li), _const_spec((ZKR_W, hw), li),
            _const_spec((rb, rb)),
            _const_spec((n_slab, S5_SLAB, 2 * S5_SLAB_STATE), li),
            _const_spec((n_slab, 2 * S5_SLAB_STATE, S5_SLAB), li),
            _const_spec((n_slab, 1, S5_SLAB_STATE), li),
            _const_spec((n_slab, 1, S5_SLAB_STATE), li),
            _const_spec((1, S5_WIDTH), li),
            _const_spec((S5_WIDTH, S5_WIDTH), li),
        ],
        out_specs=out_specs,
        scratch_shapes=[
            pltpu.VMEM((n_slab, nb, S5_SLAB_STATE), F32),
            pltpu.VMEM((n_slab, nb, S5_SLAB_STATE), F32),
            pltpu.VMEM((rb, 2 * S5_SLAB_STATE), F32),
            pltpu.VMEM((rb, S5_WIDTH), F32),
        ],
        compiler_params=_cparams(1, "arbitrary"),
        name="front",
    )(*src, w, pos, *rope, qn, kvn, wq, wqr, wk, wv, e, er, perm, bm, cm, ar, ai, dskip, wglu)


def _attn_kernel(tab_ref, q_ref, k_ref, v_ref, o_ref, vt_ref, qt_ref, m_ref, l_ref, acc_ref, *bufs, n_heads, tq, nq):
    xs_ref, xm_ref, ps_ref, pm_ref, a_ref = (bufs[2 * i:2 * i + 2] for i in range(5))
    bpq = tq // T_BLK
    vw = n_heads * MLA_V

    for j in range(nq):
        vb = _load_rows(v_ref, 1 + j * bpq, bpq, slice(0, vw))
        vt_ref[j] = vb.astype(F32).T.astype(BF16)
    v0t = v_ref[0].astype(F32).T.astype(BF16)

    for i in range(nq):
        qb = _load_rows(q_ref, 1 + i * bpq, bpq, slice(0, n_heads * HEAD_PAD))
        qt_ref[i] = qb.astype(F32).T.astype(BF16)

    def causal(st):
        r = lax.broadcasted_iota(jnp.int32, st.shape, 0)
        c = lax.broadcasted_iota(jnp.int32, st.shape, 1)
        return jnp.where(r <= c, st, MASK_VALUE * LOG2_E)

    heads = range(n_heads)
    qs = [slice(h * HEAD_PAD, (h + 1) * HEAD_PAD) for h in heads]
    vs = [slice(h * MLA_V, (h + 1) * MLA_V) for h in heads]
    k0 = [k_ref[0, :, c] for c in qs]

    outs = []
    for n in heads:
        st = causal(_dot_nt(k0[n], q_ref[0, :, qs[n]]))
        p = jnp.exp2(st - jnp.max(st, axis=0, keepdims=True))
        outs.append(_dot(v0t[vs[n], :], p.astype(BF16)) / jnp.sum(p, axis=0, keepdims=True))
    o_ref[0, :, :] = jnp.concatenate(outs, axis=0).T.astype(BF16)

    def key_tile(j, n):
        return _load_rows(k_ref, 1 + j * bpq, bpq, qs[n])

    def diag_scores(i, s):
        for n in heads:
            xs_ref[s][n] = _dot(key_tile(i, n), qt_ref[i, qs[n], :])
            xm_ref[s][n] = _dot(k0[n], qt_ref[i, qs[n], :])

    def diag_stats(i, s):
        for n in heads:
            st, sm = causal(xs_ref[s][n]), xm_ref[s][n]
            m = jnp.maximum(jnp.max(st, axis=0, keepdims=True), jnp.max(sm, axis=0, keepdims=True))
            p = jnp.exp2(st - m)
            pm = jnp.exp2(sm - m)
            m_ref[i, n] = m
            l_ref[i, n] = jnp.sum(p, axis=0, keepdims=True) + jnp.sum(pm, axis=0, keepdims=True)
            ps_ref[s][n] = p.astype(BF16)
            pm_ref[s][n] = pm.astype(BF16)

    def diag_values(i, s):
        for n in heads:
            acc_ref[i, vs[n], :] = _dot(vt_ref[i, vs[n], :], ps_ref[s][n]) + _dot(v0t[vs[n], :], pm_ref[s][n])

    _pipeline3(nq, diag_scores, diag_stats, diag_values)

    def off_scores(f, s):
        i, j = tab_ref[0, f], tab_ref[1, f]
        for n in heads:
            xs_ref[s][n] = _dot(key_tile(j, n), qt_ref[i, qs[n], :])

    def off_stats(f, s):
        i = tab_ref[0, f]
        for n in heads:
            x = xs_ref[s][n]
            m_old = m_ref[i, n]
            m = jnp.maximum(m_old, jnp.max(x, axis=0, keepdims=True))
            a = jnp.exp2(m_old - m)
            p = jnp.exp2(x - m)
            m_ref[i, n] = m
            l_ref[i, n] = a * l_ref[i, n] + jnp.sum(p, axis=0, keepdims=True)
            a_ref[s][n] = a
            ps_ref[s][n] = p.astype(BF16)

    def off_values(f, s):
        i, j = tab_ref[0, f], tab_ref[1, f]
        for n in heads:
            acc_ref[i, vs[n], :] = a_ref[s][n] * acc_ref[i, vs[n], :] + _dot(vt_ref[j, vs[n], :], ps_ref[s][n])

    _pipeline3(nq * (nq - 1) // 2, off_scores, off_stats, off_values)

    for i in range(nq):
        o = jnp.concatenate([acc_ref[i, vs[n], :] / l_ref[i, n] for n in heads], axis=0)
        _store_rows(o_ref, 1 + i * bpq, bpq, slice(0, vw), o.T.astype(BF16))


def _pipeline3(n, stage1, stage2, stage3):
    if n == 0:
        return
    stage1(0, 0)
    if n > 1:
        stage1(1, 1)
    stage2(0, 0)

    def step(t, par):
        stage1(t + 2, par)
        stage2(t + 1, 1 - par)
        stage3(t, par)

    steady = max(n - 2, 0)
    if steady >= 2:
        def body(u, carry):
            step(2 * u, 0)
            step(2 * u + 1, 1)
            return carry

        lax.fori_loop(0, steady // 2, body, 0)
    if steady % 2:
        step(steady - 1, (steady - 1) % 2)
    if n > 1:
        stage3(n - 2, (n - 2) % 2)
        stage2(n - 1, (n - 1) % 2)
    stage3(n - 1, (n - 1) % 2)


def _off_diagonal_order(nq):
    left = [(i, j) for i in range(nq) for j in range(i)]
    order = []
    while left:
        count = {}
        for i, _ in left:
            count[i] = count.get(i, 0) + 1
        ok = [p for p in left if not order or p[0] != order[-1][0]] or left
        pick = max(ok, key=lambda p: (count[p[0]], -p[1]))
        order.append(pick)
        left.remove(pick)
    return order or [(0, 0)]


def _attention(q, k, v, nb, nblk):
    hps = ATT_HEADS_PER_STEP
    n_hp = MLA_HEADS // hps
    qw = hps * HEAD_PAD
    vw = hps * MLA_V
    s = (nblk - 1) * T_BLK
    tq = min(ATT_TQ, s)
    nq = s // tq
    col = lambda b, p: p
    tab = jnp.asarray(np.array(_off_diagonal_order(nq), np.int32).T)
    o = pl.pallas_call(
        functools.partial(_attn_kernel, n_heads=hps, tq=tq, nq=nq),
        out_shape=jax.ShapeDtypeStruct((nblk, nb, T_BLK, MLA_HEADS * MLA_V), BF16),
        grid=(nb, n_hp),
        in_specs=[pl.BlockSpec(memory_space=pltpu.SMEM),
                  _seq_spec(nblk, qw, col), _seq_spec(nblk, qw, col), _seq_spec(nblk, vw, col)],
        out_specs=_seq_spec(nblk, vw, col),
        scratch_shapes=[pltpu.VMEM((nq, vw, tq), BF16), pltpu.VMEM((nq, qw, tq), BF16),
                        pltpu.VMEM((nq, hps, 1, tq), F32), pltpu.VMEM((nq, hps, 1, tq), F32),
                        pltpu.VMEM((nq, vw, tq), F32)]
        + 2 * [pltpu.VMEM((hps, tq, tq), F32)] + 2 * [pltpu.VMEM((hps, N_META, tq), F32)]
        + 2 * [pltpu.VMEM((hps, tq, tq), BF16)] + 2 * [pltpu.VMEM((hps, N_META, tq), BF16)]
        + 2 * [pltpu.VMEM((hps, 1, tq), F32)],
        compiler_params=_cparams(2),
        name="mla_attn",
    )(tab, q.reshape(nblk, nb, T_BLK, -1), k.reshape(nblk, nb, T_BLK, -1), v.reshape(nblk, nb, T_BLK, -1))
    return o.reshape(nblk * nb * T_BLK, MLA_HEADS * MLA_V)


def _time_major_perm(nb):
    p = np.zeros((T_BLK * nb, T_BLK * nb), np.float32)
    for b in range(nb):
        for t in range(T_BLK):
            p[t * nb + b, b * T_BLK + t] = 1.0
    return p


def _block_row_bcast(x, m, row):
    t, c = x.shape
    if m == t:
        return jnp.broadcast_to(x[row:row + 1, :], x.shape)
    x3 = x.reshape(t // m, m, c)
    return jnp.broadcast_to(x3[:, row:row + 1, :], x3.shape).reshape(t, c)


def _hgrn_gates(blk0, n_blk, f_ref, lb_ref):
    lb = lb_ref[...]
    zf = _load_rows(f_ref, blk0, n_blk, slice(0, HG_QK)).astype(F32)
    e = jnp.exp(-jnp.abs(zf))
    rcp = 1.0 / (1.0 + e)
    pos = zf >= 0.0
    sig_p = jnp.where(pos, rcp, e * rcp)
    sig_n = jnp.where(pos, e * rcp, rcp)
    f = lb + (1.0 - lb) * sig_p
    log_f = jnp.log2(jnp.maximum(f, HG_F_MIN))
    k = (1.0 - lb) * sig_n
    hi = log_f.astype(BF16)
    lo = (log_f - hi.astype(F32)).astype(BF16)
    return k, hi, lo


def _hgrn_mix(blk0, n_blk, gates, q_ref, v_ref, tri_ref, lvl_ref, st_ref):
    t = n_blk * T_BLK
    cs = slice(0, HG_QK)
    heads = [slice(h * HG_KEY, (h + 1) * HG_KEY) for h in range(HG_HEADS)]
    k, hi, lo = gates
    q = _load_rows(q_ref, blk0, n_blk, cs).astype(F32)
    v = _load_rows(v_ref, blk0, n_blk, cs)
    tri = tri_ref[0:t, 0:t]
    cum = _dot(tri, hi) + _dot(tri, lo)
    lvl = lvl_ref[0:t, 0:t]
    r_idx = lax.broadcasted_iota(jnp.int32, (t, HG_QK), 0)
    c8 = _block_row_bcast(cum, HG_BOTTOM, HG_BOTTOM // 2 - 1)
    qe = (q * jnp.exp2(cum - c8)).astype(BF16)
    ke = (k * jnp.exp2(c8 - cum)).astype(BF16)
    n_lvl = len(HG_LEVELS)
    scores = [jnp.where(lvl == n_lvl, _dot_nt(qe[:, hs], ke[:, hs]), 0.0) for hs in heads]
    for li, m in enumerate(HG_LEVELS):
        if m > t:
            continue
        half = m // 2
        cmid = _block_row_bcast(cum, m, half - 1)
        upper = (r_idx & (m - 1)) >= half
        ex = jnp.exp2(jnp.where(upper, cum - cmid, cmid - cum))
        qe = jnp.where(upper, q * ex, 0.0).astype(BF16)
        ke = jnp.where(upper, 0.0, k * ex).astype(BF16)
        scores = [jnp.where(lvl == li, _dot_nt(qe[:, hs], ke[:, hs]), s) for s, hs in zip(scores, heads)]
    qd = (q * jnp.exp2(cum)).astype(BF16)
    last = cum[t - 1:t, :]
    kd = (k * jnp.exp2(last - cum)).astype(BF16)
    dec = jnp.exp2(last)
    outs = []
    for h, hs in enumerate(heads):
        st = st_ref[h]
        outs.append(_dot(scores[h].astype(BF16), v[:, hs]) + _dot_nt(qd[:, hs], st.astype(BF16)))
        st_ref[h] = st * dec[:, hs] + _dot_tn(v[:, hs], kd[:, hs])
    return tuple(outs)


def _hgrn_out(blk0, n_blk, outs, g_ref, on_ref, o_ref):
    cs = slice(0, HG_VW)
    g = _load_rows(g_ref, blk0, n_blk, cs).astype(F32)
    o = jnp.concatenate([o * lax.rsqrt(jnp.mean(o * o, axis=-1, keepdims=True) + 1e-6) for o in outs], axis=1)
    _store_rows(o_ref, blk0, n_blk, cs, (o * on_ref[...] * (g * _sigmoid(g))).astype(BF16))


def _hgrn_kernel(q_ref, f_ref, v_ref, g_ref, lb_ref, on_ref, tri_ref, lvl_ref, o_ref, st_ref, *, n_chunks):
    bpc = HG_CHUNK // T_BLK
    gates = functools.partial(_hgrn_gates, f_ref=f_ref, lb_ref=lb_ref)
    mix = functools.partial(_hgrn_mix, q_ref=q_ref, v_ref=v_ref, tri_ref=tri_ref, lvl_ref=lvl_ref, st_ref=st_ref)
    out = functools.partial(_hgrn_out, g_ref=g_ref, on_ref=on_ref, o_ref=o_ref)
    st_ref[...] = jnp.zeros_like(st_ref)
    out(0, 1, mix(0, 1, gates(0, 1)))

    def body(c, carry):
        blk0 = 1 + c * bpc
        out(blk0, bpc, mix(blk0, bpc, gates(blk0, bpc)))
        return carry

    lax.fori_loop(0, n_chunks, body, 0)


def _hgrn(zhg, lb, onorm, tri, lvl, nb, nblk, li):
    z4 = zhg.reshape(nblk, nb, T_BLK, ZHG_W)
    col = lambda j: _seq_spec(nblk, HG_QK, lambda b: j)
    o = pl.pallas_call(
        functools.partial(_hgrn_kernel, n_chunks=(nblk - 1) * T_BLK // HG_CHUNK),
        out_shape=jax.ShapeDtypeStruct((nblk, nb, T_BLK, HG_VW), BF16),
        grid=(nb,),
        in_specs=[col(0), col(1), col(2), col(3),
                  _const_spec((1, HG_QK), li), _const_spec((1, HG_VW), li),
                  _const_spec((HG_CHUNK, HG_CHUNK)), _const_spec((HG_CHUNK, HG_CHUNK))],
        out_specs=_seq_spec(nblk, HG_VW, lambda b: 0),
        scratch_shapes=[pltpu.VMEM((HG_HEADS, HG_VAL, HG_KEY), F32)],
        compiler_params=_cparams(1),
        name="hgrn2",
    )(z4, z4, z4, z4, lb, onorm, tri, lvl)
    return o.reshape(nblk * nb * T_BLK, HG_VW)


def _hgrn_level_matrix():
    r = np.arange(HG_CHUNK)[:, None]
    c = np.arange(HG_CHUNK)[None, :]
    lvl = np.zeros((HG_CHUNK, HG_CHUNK), np.int32)
    for li, m in enumerate(HG_LEVELS):
        lvl = np.where(r // m == c // m, li, lvl)
    lvl = np.where(r // HG_BOTTOM == c // HG_BOTTOM, len(HG_LEVELS), lvl)
    return np.where(c <= r, lvl, -1).astype(np.int32)


def _back_kernel(om_ref, os_ref, oh_ref, gm_ref, gs_ref, gh_ref, h_ref,
                 wm_ref, ws_ref, wh_ref, wo_ref, g1_ref, b1_ref,
                 wg_ref, wu_ref, wd_ref, g2_ref, b2_ref, o_ref, r1_ref, r2_ref, *, alpha):
    @pl.when(pl.program_id(0) == 0)
    def _():
        r1_ref[...] = jnp.zeros_like(r1_ref)
        r2_ref[...] = jnp.zeros_like(r2_ref)

    r1 = r1_ref[...]
    r2 = r2_ref[...]
    ym = _dot(om_ref[...], wm_ref[...])
    ys = _dot(os_ref[...], ws_ref[...])
    yh = _dot(oh_ref[...], wh_ref[...])
    o_ref[...] = _layer_norm(r2, g2_ref[...], b2_ref[...]).reshape(o_ref.shape)
    h1 = _layer_norm(r1, g1_ref[...], b1_ref[...])
    hb = h1.astype(BF16)
    a = _dot(hb, wg_ref[...])
    u = _dot(hb, wu_ref[...])
    mixed = _sigmoid(gm_ref[...].astype(F32)) * ym
    mixed += _sigmoid(gs_ref[...].astype(F32)) * ys
    mixed += _sigmoid(gh_ref[...].astype(F32)) * yh
    r1_ref[...] = alpha * h_ref[...] + _dot(mixed.astype(BF16), wo_ref[...])
    r2_ref[...] = alpha * h1 + _dot((a * _sigmoid(a) * u).astype(BF16), wd_ref[...])


def _back(om, os_, oh, zg, h, wm, ws, wh, wo, g1, b1, wg, wu, wd, g2, b2, nb, alpha, final, li):
    r, d = h.shape
    bw = om.shape[1]
    dff = wg.shape[-1]
    rb = T_BLK * nb
    skip = 1 if final else 0
    n_blk = r // rb - skip
    lag = 2
    row = lambda width, j=0: pl.BlockSpec((rb, width), lambda i: (jnp.minimum(i, n_blk - 1) + skip, j))
    if final:
        out_shape = jax.ShapeDtypeStruct((nb, n_blk * T_BLK, d), F32)
        out_spec = pl.BlockSpec((nb, T_BLK, d), lambda i: (0, jnp.maximum(i - lag, 0), 0))
    else:
        out_shape = jax.ShapeDtypeStruct((r, d), F32)
        out_spec = pl.BlockSpec((rb, d), lambda i: (jnp.maximum(i - lag, 0), 0))
    return pl.pallas_call(
        functools.partial(_back_kernel, alpha=alpha),
        out_shape=out_shape,
        grid=(n_blk + lag,),
        in_specs=[row(bw), row(bw), row(bw), row(d, 0), row(d, 1), row(d, 2), row(d),
                  _const_spec((bw, d), li), _const_spec((bw, d), li), _const_spec((bw, d), li),
                  _const_spec((d, d), li), _const_spec((1, d), li), _const_spec((1, d), li),
                  _const_spec((d, dff), li), _const_spec((d, dff), li), _const_spec((dff, d), li),
                  _const_spec((1, d), li), _const_spec((1, d), li)],
        out_specs=out_spec,
        scratch_shapes=[pltpu.VMEM((rb, d), F32), pltpu.VMEM((rb, d), F32)],
        compiler_params=_cparams(1, "arbitrary"),
        name="merge_ffn",
    )(om, os_, oh, zg, zg, zg, h, wm, ws, wh, wo, g1, b1, wg, wu, wd, g2, b2)


def _permute_w_in(w):
    mla_in = MLA_Q_RANK + MLA_KV_RANK + MLA_ROPE
    s5_0, hg_0 = mla_in, mla_in + S5_WIDTH
    g_0 = hg_0 + 2 * HG_QK + 2 * HG_VW
    pad = jnp.zeros((w.shape[0], ZKR_W - MLA_ROPE), w.dtype)
    return jnp.concatenate([w[:, s5_0:hg_0], w[:, 0:MLA_Q_RANK + MLA_KV_RANK], w[:, hg_0:g_0], w[:, g_0:],
                            w[:, MLA_Q_RANK + MLA_KV_RANK:mla_in], pad], axis=1).astype(BF16)


def _rot_half(x):
    x1, x2 = jnp.split(x, 2, axis=-1)
    return jnp.concatenate([-x2, x1], axis=-1)


def _mla_weights(w_uq, w_ukv):
    rq, rkv = w_uq.shape[0], w_ukv.shape[0]
    zpad = HEAD_PAD - MLA_NOPE - MLA_ROPE
    wq = w_uq.reshape(rq, MLA_HEADS, MLA_NOPE + MLA_ROPE)
    q_nope, q_rope = wq[..., :MLA_NOPE], wq[..., MLA_NOPE:]
    zq = jnp.zeros((rq, MLA_HEADS, zpad), w_uq.dtype)
    wq_p = jnp.concatenate([q_nope, q_rope, zq], axis=-1).reshape(rq, -1)
    wq_r = jnp.concatenate([jnp.zeros_like(q_nope), _rot_half(q_rope), zq], axis=-1).reshape(rq, -1)
    wkv = w_ukv.reshape(rkv, MLA_HEADS, MLA_NOPE + MLA_V)
    zk = jnp.zeros((rkv, MLA_HEADS, HEAD_PAD - MLA_NOPE), w_ukv.dtype)
    wk_p = jnp.concatenate([wkv[..., :MLA_NOPE], zk], axis=-1).reshape(rkv, -1)
    wv = wkv[..., MLA_NOPE:].reshape(rkv, -1)
    eye = jnp.eye(ZKR_W, MLA_ROPE, dtype=F32)
    place = jnp.concatenate([jnp.zeros((ZKR_W, MLA_NOPE), F32), eye, jnp.zeros((ZKR_W, zpad), F32)], axis=-1)
    place_r = jnp.concatenate([jnp.zeros((ZKR_W, MLA_NOPE), F32), _rot_half(eye),
                               jnp.zeros((ZKR_W, zpad), F32)], axis=-1)
    e = jnp.tile(place, (1, MLA_HEADS))
    er = jnp.tile(place_r, (1, MLA_HEADS))
    return [a.astype(BF16) for a in (wq_p, wq_r, wk_p, wv, e, er)]


def _rope_consts():
    half = MLA_ROPE // 2
    inv = ROPE_THETA ** (-(jnp.arange(0, MLA_ROPE, 2, dtype=F32) / MLA_ROPE))
    place = np.zeros((half, HEAD_PAD), np.float32)
    place[np.arange(half), MLA_NOPE + np.arange(half)] = 1.0
    place[np.arange(half), MLA_NOPE + half + np.arange(half)] = 1.0
    base = np.zeros((1, HEAD_PAD), np.float32)
    base[0, :MLA_NOPE] = 1.0
    return inv[:, None], jnp.asarray(place, BF16), jnp.asarray(base)


def _s5_params(lam_re, lam_im, log_dt, b_re, b_im, c_re, c_im):
    lr = jnp.minimum(lam_re.astype(F32), -1e-4)
    li = lam_im.astype(F32)
    dt = jnp.exp(log_dt.astype(F32))[:, None]
    mag = jnp.exp(lr * dt)
    ab_r = mag * jnp.cos(li * dt)
    ab_i = mag * jnp.sin(li * dt)
    den = lr * lr + li * li
    nr = ab_r - 1.0
    coef_r = ((nr * lr + ab_i * li) / den)[..., None]
    coef_i = ((ab_i * lr - nr * li) / den)[..., None]
    bb_r = coef_r * b_re.astype(F32) - coef_i * b_im.astype(F32)
    bb_i = coef_r * b_im.astype(F32) + coef_i * b_re.astype(F32)
    n_slab = S5_WIDTH // S5_SLAB
    gps = S5_SLAB // S5_GROUP
    eye = jnp.eye(gps, dtype=F32)

    def in_mat(bb):
        b4 = bb.reshape(n_slab, gps, S5_STATE, S5_GROUP)
        return jnp.einsum('jgnc,gh->jgchn', b4, eye).reshape(n_slab, S5_SLAB, S5_SLAB_STATE)

    def out_mat(cc):
        c4 = cc.astype(F32).reshape(n_slab, gps, S5_GROUP, S5_STATE)
        return jnp.einsum('jgcn,gh->jgnhc', c4, eye).reshape(n_slab, S5_SLAB_STATE, S5_SLAB)

    bm = jnp.concatenate([in_mat(bb_r), in_mat(bb_i)], axis=2).astype(BF16)
    cm = jnp.concatenate([out_mat(c_re), -out_mat(c_im)], axis=1).astype(BF16)
    ar = ab_r.reshape(n_slab, 1, S5_SLAB_STATE)
    ai = ab_i.reshape(n_slab, 1, S5_SLAB_STATE)
    return bm, cm, ar, ai


def kernel(x, positions, meta_tokens, ln_in_g, ln_in_b, w_in, mla_q_norm, mla_w_uq, mla_kv_norm, mla_w_ukv,
           s5_lam_re, s5_lam_im, s5_log_dt, s5_b_re, s5_b_im, s5_c_re, s5_c_im, s5_d, s5_w_glu,
           hg_lb_logits, hg_out_norm, w_br_mla, w_br_s5, w_br_hg, w_out, ln1_g, ln1_b,
           w_ffn_gate, w_ffn_up, w_ffn_down, ln2_g, ln2_b):
    nb, s, d = x.shape
    depth = w_in.shape[0]
    nblk = (s + N_META) // T_BLK
    alpha = (2 * depth) ** 0.25
    row2 = lambda a: a.astype(F32)[None, :]
    rows3 = lambda a: a.astype(F32)[:, None, :]
    bf16 = lambda a: a.astype(BF16)

    meta_pos = jnp.broadcast_to(jnp.arange(N_META, dtype=jnp.int32)[None, :], (nb, N_META))
    pos = jnp.concatenate([meta_pos, positions.astype(jnp.int32) + N_META], axis=1)
    pos = pos.reshape(nb, nblk, T_BLK).transpose(1, 0, 2).reshape(nblk, 1, nb * T_BLK)
    rope = _rope_consts()
    p_lb = jax.nn.softmax(hg_lb_logits.astype(F32), axis=0)
    lower_bounds = jnp.cumsum(p_lb, axis=0) - p_lb[0]
    tri = jnp.asarray(np.tril(np.ones((HG_CHUNK, HG_CHUNK), np.float32)), BF16)
    lvl = jnp.asarray(_hgrn_level_matrix())
    perm = jnp.asarray(_time_major_perm(nb), BF16)

    front_p = (jax.vmap(_permute_w_in)(w_in), pos, rope, rows3(mla_q_norm), rows3(mla_kv_norm),
               *jax.vmap(_mla_weights)(mla_w_uq, mla_w_ukv), perm,
               *jax.vmap(_s5_params)(s5_lam_re, s5_lam_im, s5_log_dt, s5_b_re, s5_b_im, s5_c_re, s5_c_im),
               rows3(s5_d), bf16(s5_w_glu))
    hgrn_p = (lower_bounds[:, None, :], rows3(hg_out_norm), tri, lvl)
    back_p = (bf16(w_br_mla), bf16(w_br_s5), bf16(w_br_hg), bf16(w_out), rows3(ln1_g), rows3(ln1_b),
              bf16(w_ffn_gate), bf16(w_ffn_up), bf16(w_ffn_down), rows3(ln2_g), rows3(ln2_b))

    h = (x, meta_tokens.astype(x.dtype), row2(ln_in_g), row2(ln_in_b))
    for li in range(depth):
        outs = _front(h, *front_p, nb, li)
        if li == 0:
            h, *outs = outs
        zhg, zg, q, k, v, y_s5 = outs
        o_mla = _attention(q, k, v, nb, nblk)
        o_hg = _hgrn(zhg, *hgrn_p, nb, nblk, li)
        h = _back(o_mla, y_s5, o_hg, zg, h, *back_p, nb, alpha, final=li == depth - 1, li=li)
    return h
```

```python
import functools
import math

import jax
import jax.numpy as jnp
import numpy as np
from jax import lax
from jax.experimental import pallas as pl
from jax.experimental.pallas import tpu as pltpu

F32 = jnp.float32
BF16 = jnp.bfloat16

N_META = 16
MLA_HEADS = 8
MLA_NOPE = 64
MLA_ROPE = 32
MLA_V = 64
MLA_Q_RANK = 256
MLA_KV_RANK = 256
ROPE_THETA = 10000.0
MASK_VALUE = -1e9
LOG2_E = math.log2(math.e)
HEAD_PAD = 128
S5_WIDTH = 512
S5_GROUP = 16
S5_GROUPS = S5_WIDTH // S5_GROUP
S5_STATE = 64
S5_SLAB = 128
S5_SLAB_STATE = (S5_SLAB // S5_GROUP) * S5_STATE
HG_HEADS = 4
HG_KEY = 128
HG_VAL = 128
HG_QK = HG_HEADS * HG_KEY
HG_VW = HG_HEADS * HG_VAL
HG_F_MIN = 1e-6
HG_CHUNK = 128
HG_LEVELS = (128, 64, 32, 16)
HG_BOTTOM = 8
N_BRANCH = 3
T_BLK = 16
ATT_TQ = 256
ATT_HEADS_PER_STEP = 4
VMEM_LIMIT = 56 * 1024 * 1024

ZA_W = 1024
ZHG_W = 2048
ZG_W = 3072
ZKR_W = 128
Z_W = ZA_W + ZHG_W + ZG_W + ZKR_W
ZALL_WIDTHS = (ZHG_W, ZG_W, MLA_HEADS * HEAD_PAD, MLA_HEADS * HEAD_PAD, MLA_HEADS * MLA_V, S5_WIDTH)
ZALL_OFF = tuple(sum(ZALL_WIDTHS[:i]) for i in range(len(ZALL_WIDTHS)))
ZALL_W = sum(ZALL_WIDTHS)
OFF_HG, OFF_G, OFF_Q, OFF_K, OFF_V, OFF_Y = ZALL_OFF
WIDE_CHUNK = 512
FILL_BEFORE_MLA = 3


def _cparams(n_grid, sem="parallel"):
    return pltpu.CompilerParams(dimension_semantics=(sem,) * n_grid, vmem_limit_bytes=VMEM_LIMIT)


def _const_spec(shape, layer=None):
    nd = len(shape)
    if layer is None:
        return pl.BlockSpec(shape, lambda *_: (0,) * nd, pipeline_mode=pl.Buffered(1))
    return pl.BlockSpec((None,) + tuple(shape), lambda *_: (layer,) + (0,) * nd, pipeline_mode=pl.Buffered(1))


def _seq_spec(nblk, width, col):
    return pl.BlockSpec((nblk, None, T_BLK, width), lambda b, *g: (0, b, 0, col(b, *g)))


def _dot(a, b):
    return jnp.dot(a, b, preferred_element_type=F32)


def _dot_nt(a, b):
    return lax.dot_general(a, b, (((1,), (1,)), ((), ())), preferred_element_type=F32)


def _dot_tn(a, b):
    return lax.dot_general(a, b, (((0,), (0,)), ((), ())), preferred_element_type=F32)


def _sigmoid(x):
    return 1.0 / (1.0 + jnp.exp(-x))


def _layer_norm(x, g, b, eps=1e-5):
    mu = jnp.mean(x, axis=-1, keepdims=True)
    xc = x - mu
    var = jnp.mean(xc * xc, axis=-1, keepdims=True)
    return xc * lax.rsqrt(var + eps) * g + b


def _rms_norm(x, g, eps=1e-6):
    return x * lax.rsqrt(jnp.mean(x * x, axis=-1, keepdims=True) + eps) * g


def _load_rows(ref, blk0, n_blk, cs):
    x = ref[pl.ds(blk0, n_blk), :, cs]
    return x.reshape(n_blk * T_BLK, x.shape[-1])


def _store_rows(ref, blk0, n_blk, cs, x):
    ref[pl.ds(blk0, n_blk), :, cs] = x.reshape(n_blk, T_BLK, x.shape[-1])


def _rope_tables(pos_ref, inv_ref, place_ref, base_ref):
    ang = inv_ref[...] * pos_ref[...].astype(F32)

    def place(t):
        hi = t.astype(BF16)
        lo = (t - hi.astype(F32)).astype(BF16)
        return _dot_tn(hi, place_ref[...]) + _dot_tn(lo, place_ref[...])

    return place(jnp.cos(ang)) + base_ref[...], place(jnp.sin(ang))


def _mla_prep(cq, ckv, kr, cos, sin, qn_ref, kvn_ref, wq_ref, wqr_ref, wk_ref, wv_ref, e_ref, er_ref,
              q_ref, k_ref, v_ref):
    scale = (MLA_NOPE + MLA_ROPE) ** -0.5 * LOG2_E
    cqn = _rms_norm(cq, qn_ref[...]).astype(BF16)
    ckvn = _rms_norm(ckv, kvn_ref[...]).astype(BF16)
    kr = kr.astype(BF16)
    cos_q = cos * scale
    sin_q = sin * scale
    for h in range(MLA_HEADS):
        cs = slice(h * HEAD_PAD, (h + 1) * HEAD_PAD)
        qf = _dot(cqn, wq_ref[:, cs])
        qr = _dot(cqn, wqr_ref[:, cs])
        q_ref[:, cs] = (qf * cos_q + qr * sin_q).astype(BF16)
        kf = _dot(ckvn, wk_ref[:, cs]) + _dot(kr, e_ref[:, cs])
        krot = _dot(kr, er_ref[:, cs])
        k_ref[:, cs] = (kf * cos + krot * sin).astype(BF16)
    v_ref[...] = _dot(ckvn, wv_ref[...]).astype(BF16)


def _s5_block(u, perm_ref, bm_ref, cm_ref, ar_ref, ai_ref, d_ref, wg_ref, o_ref,
              xr_ref, xi_ref, buf_ref, y_ref, nb, fillers):
    n_slab = S5_WIDTH // S5_SLAB
    ns = S5_SLAB_STATE
    u = _dot(perm_ref[...], u).astype(BF16)
    for j in range(n_slab):
        buf_ref[...] = _dot(u[:, j * S5_SLAB:(j + 1) * S5_SLAB], bm_ref[j])
        if fillers:
            fillers.pop(0)()
        ar = jnp.broadcast_to(ar_ref[j], (nb, ns))
        ai = jnp.broadcast_to(ai_ref[j], (nb, ns))
        xr = xr_ref[j]
        xi = xi_ref[j]
        for t in range(T_BLK):
            rows = slice(t * nb, (t + 1) * nb)
            nr = ar * xr - ai * xi + buf_ref[rows, 0:ns]
            ni = ar * xi + ai * xr + buf_ref[rows, ns:2 * ns]
            buf_ref[rows, 0:ns] = nr
            buf_ref[rows, ns:2 * ns] = ni
            xr, xi = nr, ni
        xr_ref[j] = xr
        xi_ref[j] = xi
        y_ref[:, j * S5_SLAB:(j + 1) * S5_SLAB] = _dot(buf_ref[...].astype(BF16), cm_ref[j])

    while fillers:
        fillers.pop(0)()
    y = y_ref[...] + d_ref[...] * u.astype(F32)
    y = 0.5 * y * (1.0 + jnp.tanh(math.sqrt(2.0 / math.pi) * (y + 0.044715 * (y * y * y))))
    gate = _sigmoid(_dot(y.astype(BF16), wg_ref[...]))
    out = (y * gate).astype(BF16)
    o_ref[...] = _dot_tn(perm_ref[...], out).astype(BF16)


def _front_kernel(*refs, nb, ln_in):
    if ln_in:
        x_ref, meta_ref, lng_ref, lnb_ref, *refs = refs
    else:
        h_ref, *refs = refs
    (w_ref, pos_ref, inv_ref, place_ref, base_ref, qn_ref, kvn_ref, wq_ref, wqr_ref, wk_ref, wv_ref, e_ref, er_ref,
     perm_ref, bm_ref, cm_ref, ar_ref, ai_ref, d_ref, wglu_ref, *refs) = refs
    if ln_in:
        hout_ref, *refs = refs
    zall_ref, xr_ref, xi_ref, buf_ref, y_ref = refs
    zhg_ref, zg_ref, q_ref, k_ref, v_ref, ys5_ref = (
        zall_ref.at[:, o:o + w] for o, w in zip(ZALL_OFF, ZALL_WIDTHS))

    @pl.when(pl.program_id(0) == 0)
    def _():
        xr_ref[...] = jnp.zeros_like(xr_ref)
        xi_ref[...] = jnp.zeros_like(xi_ref)

    cos, sin = _rope_tables(pos_ref, inv_ref, place_ref, base_ref)
    if ln_in:
        hx = _layer_norm(x_ref[...], lng_ref[...], lnb_ref[...])
        hm = _layer_norm(meta_ref[...], lng_ref[...], lnb_ref[...])
        h = jnp.where(pl.program_id(0) == 0, jnp.broadcast_to(hm[None], hx.shape), hx)
        h = h.reshape(hout_ref.shape)
        hout_ref[...] = h
    else:
        h = h_ref[...]
    x = h.astype(BF16)
    za = _dot(x, w_ref[:, 0:ZA_W])
    zkr = _dot(x, w_ref[:, ZA_W + ZHG_W + ZG_W:Z_W])

    def wide_chunk(o_ref, c, off):
        def run():
            o_ref[:, c:c + WIDE_CHUNK] = _dot(x, w_ref[:, off + c:off + c + WIDE_CHUNK]).astype(BF16)
        return run

    fillers = [wide_chunk(zhg_ref, c, ZA_W) for c in range(0, ZHG_W, WIDE_CHUNK)]
    fillers += [wide_chunk(zg_ref, c, ZA_W + ZHG_W) for c in range(0, ZG_W, WIDE_CHUNK)]
    for _ in range(FILL_BEFORE_MLA):
        fillers.pop(0)()
    c0 = S5_WIDTH
    _mla_prep(za[:, c0:c0 + MLA_Q_RANK], za[:, c0 + MLA_Q_RANK:c0 + MLA_Q_RANK + MLA_KV_RANK], zkr,
              cos, sin, qn_ref, kvn_ref, wq_ref, wqr_ref, wk_ref, wv_ref, e_ref, er_ref,
              q_ref, k_ref, v_ref)
    _s5_block(za[:, 0:S5_WIDTH].astype(BF16), perm_ref, bm_ref, cm_ref, ar_ref, ai_ref, d_ref, wglu_ref, ys5_ref,
              xr_ref, xi_ref, buf_ref, y_ref, nb, fillers)


def _front(src, w, pos, rope, qn, kvn, wq, wqr, wk, wv, e, er, perm, bm, cm, ar, ai, dskip, wglu, nb, li):
    ln_in = isinstance(src, tuple)
    rb = T_BLK * nb
    if ln_in:
        x, meta, lng, lnb = src
        d = x.shape[-1]
        r = (x.shape[1] + N_META) * nb
        src_specs = [pl.BlockSpec((nb, T_BLK, d), lambda i: (0, jnp.maximum(i - 1, 0), 0)),
                     _const_spec((N_META, d)), _const_spec((1, d)), _const_spec((1, d))]
    else:
        src = (src,)
        r, d = src[0].shape
        src_specs = [pl.BlockSpec((rb, d), lambda i: (i, 0))]
    hw = MLA_HEADS * HEAD_PAD
    vw = MLA_HEADS * MLA_V
    n_slab = S5_WIDTH // S5_SLAB
    row = lambda width: pl.BlockSpec((rb, width), lambda i: (i, 0))
    out_shape = [jax.ShapeDtypeStruct((r, ZALL_W), BF16)]
    out_specs = [row(ZALL_W)]
    if ln_in:
        out_shape.insert(0, jax.ShapeDtypeStruct((r, d), F32))
        out_specs.insert(0, row(d))
    return pl.pallas_call(
        functools.partial(_front_kernel, nb=nb, ln_in=ln_in),
        out_shape=out_shape,
        grid=(r // rb,),
        in_specs=src_specs + [
            _const_spec((d, Z_W), li), pl.BlockSpec((None, 1, rb), lambda i: (i, 0, 0)),
            _const_spec((MLA_ROPE // 2, 1)), _const_spec((MLA_ROPE // 2, HEAD_PAD)), _const_spec((1, HEAD_PAD)),
            _const_spec((1, MLA_Q_RANK), li), _const_spec((1, MLA_KV_RANK), li),
            _const_spec((MLA_Q_RANK, hw), li), _const_spec((MLA_Q_RANK, hw), li), _const_spec((MLA_KV_RANK, hw), li),
            _const_spec((MLA_KV_RANK, vw), li), _const_spec((ZKR_W, hw), li), _const_spec((ZKR_W, hw), li),
            _const_spec((rb, rb)),
            _const_spec((n_slab, S5_SLAB, 2 * S5_SLAB_STATE), li),
            _const_spec((n_slab, 2 * S5_SLAB_STATE, S5_SLAB), li),
            _const_spec((n_slab, 1, S5_SLAB_STATE), li),
            _const_spec((n_slab, 1, S5_SLAB_STATE), li),
            _const_spec((1, S5_WIDTH), li),
            _const_spec((S5_WIDTH, S5_WIDTH), li),
        ],
        out_specs=out_specs,
        scratch_shapes=[
            pltpu.VMEM((n_slab, nb, S5_SLAB_STATE), F32),
            pltpu.VMEM((n_slab, nb, S5_SLAB_STATE), F32),
            pltpu.VMEM((rb, 2 * S5_SLAB_STATE), F32),
            pltpu.VMEM((rb, S5_WIDTH), F32),
        ],
        compiler_params=_cparams(1, "arbitrary"),
        name="front",
    )(*src, w, pos, *rope, qn, kvn, wq, wqr, wk, wv, e, er, perm, bm, cm, ar, ai, dskip, wglu)


def _attn_kernel(tab_ref, q_ref, k_ref, v_ref, o_ref, vt_ref, qt_ref, m_ref, l_ref, acc_ref, *bufs, n_heads, tq, nq):
    xs_ref, xm_ref, ps_ref, pm_ref, a_ref = (bufs[2 * i:2 * i + 2] for i in range(5))
    bpq = tq // T_BLK
    vw = n_heads * MLA_V

    for j in range(nq):
        vb = _load_rows(v_ref, 1 + j * bpq, bpq, slice(0, vw))
        vt_ref[j] = vb.astype(F32).T.astype(BF16)
    v0t = v_ref[0].astype(F32).T.astype(BF16)

    for i in range(nq):
        qb = _load_rows(q_ref, 1 + i * bpq, bpq, slice(0, n_heads * HEAD_PAD))
        qt_ref[i] = qb.astype(F32).T.astype(BF16)

    def causal(st):
        r = lax.broadcasted_iota(jnp.int32, st.shape, 0)
        c = lax.broadcasted_iota(jnp.int32, st.shape, 1)
        return jnp.where(r <= c, st, MASK_VALUE * LOG2_E)

    heads = range(n_heads)
    qs = [slice(h * HEAD_PAD, (h + 1) * HEAD_PAD) for h in heads]
    vs = [slice(h * MLA_V, (h + 1) * MLA_V) for h in heads]
    k0 = [k_ref[0, :, c] for c in qs]

    outs = []
    for n in heads:
        st = causal(_dot_nt(k0[n], q_ref[0, :, qs[n]]))
        p = jnp.exp2(st - jnp.max(st, axis=0, keepdims=True))
        outs.append(_dot(v0t[vs[n], :], p.astype(BF16)) / jnp.sum(p, axis=0, keepdims=True))
    o_ref[0, :, :] = jnp.concatenate(outs, axis=0).T.astype(BF16)

    def key_tile(j, n):
        return _load_rows(k_ref, 1 + j * bpq, bpq, qs[n])

    def diag_scores(i, s):
        for n in heads:
            xs_ref[s][n] = _dot(key_tile(i, n), qt_ref[i, qs[n], :])
            xm_ref[s][n] = _dot(k0[n], qt_ref[i, qs[n], :])

    def diag_stats(i, s):
        for n in heads:
            st, sm = causal(xs_ref[s][n]), xm_ref[s][n]
            m = jnp.maximum(jnp.max(st, axis=0, keepdims=True), jnp.max(sm, axis=0, keepdims=True))
            p = jnp.exp2(st - m)
            pm = jnp.exp2(sm - m)
            m_ref[i, n] = m
            l_ref[i, n] = jnp.sum(p, axis=0, keepdims=True) + jnp.sum(pm, axis=0, keepdims=True)
            ps_ref[s][n] = p.astype(BF16)
            pm_ref[s][n] = pm.astype(BF16)

    def diag_values(i, s):
        for n in heads:
            acc_ref[i, vs[n], :] = _dot(vt_ref[i, vs[n], :], ps_ref[s][n]) + _dot(v0t[vs[n], :], pm_ref[s][n])

    _pipeline3(nq, diag_scores, diag_stats, diag_values)

    def off_scores(f, s):
        i, j = tab_ref[0, f], tab_ref[1, f]
        for n in heads:
            xs_ref[s][n] = _dot(key_tile(j, n), qt_ref[i, qs[n], :])

    def off_stats(f, s):
        i = tab_ref[0, f]
        for n in heads:
            x = xs_ref[s][n]
            m_old = m_ref[i, n]
            m = jnp.maximum(m_old, jnp.max(x, axis=0, keepdims=True))
            a = jnp.exp2(m_old - m)
            p = jnp.exp2(x - m)
            m_ref[i, n] = m
            l_ref[i, n] = a * l_ref[i, n] + jnp.sum(p, axis=0, keepdims=True)
            a_ref[s][n] = a
            ps_ref[s][n] = p.astype(BF16)

    def off_values(f, s):
        i, j = tab_ref[0, f], tab_ref[1, f]
        for n in heads:
            acc_ref[i, vs[n], :] = a_ref[s][n] * acc_ref[i, vs[n], :] + _dot(vt_ref[j, vs[n], :], ps_ref[s][n])

    _pipeline3(nq * (nq - 1) // 2, off_scores, off_stats, off_values)

    for i in range(nq):
        o = jnp.concatenate([acc_ref[i, vs[n], :] / l_ref[i, n] for n in heads], axis=0)
        _store_rows(o_ref, 1 + i * bpq, bpq, slice(0, vw), o.T.astype(BF16))


def _pipeline3(n, stage1, stage2, stage3):
    if n == 0:
        return
    stage1(0, 0)
    if n > 1:
        stage1(1, 1)
    stage2(0, 0)

    def step(t, par):
        stage1(t + 2, par)
        stage2(t + 1, 1 - par)
        stage3(t, par)

    steady = max(n - 2, 0)
    if steady >= 2:
        def body(u, carry):
            step(2 * u, 0)
            step(2 * u + 1, 1)
            return carry

        lax.fori_loop(0, steady // 2, body, 0)
    if steady % 2:
        step(steady - 1, (steady - 1) % 2)
    if n > 1:
        stage3(n - 2, (n - 2) % 2)
        stage2(n - 1, (n - 1) % 2)
    stage3(n - 1, (n - 1) % 2)


def _off_diagonal_order(nq):
    left = [(i, j) for i in range(nq) for j in range(i)]
    order = []
    while left:
        count = {}
        for i, _ in left:
            count[i] = count.get(i, 0) + 1
        ok = [p for p in left if not order or p[0] != order[-1][0]] or left
        pick = max(ok, key=lambda p: (count[p[0]], -p[1]))
        order.append(pick)
        left.remove(pick)
    return order or [(0, 0)]


def _attention(zall, nb, nblk):
    z4 = zall.reshape(nblk, nb, T_BLK, ZALL_W)
    hps = ATT_HEADS_PER_STEP
    n_hp = MLA_HEADS // hps
    qw = hps * HEAD_PAD
    vw = hps * MLA_V
    s = (nblk - 1) * T_BLK
    tq = min(ATT_TQ, s)
    nq = s // tq
    col = lambda b, p: p
    tab = jnp.asarray(np.array(_off_diagonal_order(nq), np.int32).T)
    o = pl.pallas_call(
        functools.partial(_attn_kernel, n_heads=hps, tq=tq, nq=nq),
        out_shape=jax.ShapeDtypeStruct((nblk, nb, T_BLK, MLA_HEADS * MLA_V), BF16),
        grid=(nb, n_hp),
        in_specs=[pl.BlockSpec(memory_space=pltpu.SMEM),
                  _seq_spec(nblk, qw, lambda b, p: OFF_Q // qw + p),
                  _seq_spec(nblk, qw, lambda b, p: OFF_K // qw + p),
                  _seq_spec(nblk, vw, lambda b, p: OFF_V // vw + p)],
        out_specs=_seq_spec(nblk, vw, col),
        scratch_shapes=[pltpu.VMEM((nq, vw, tq), BF16), pltpu.VMEM((nq, qw, tq), BF16),
                        pltpu.VMEM((nq, hps, 1, tq), F32), pltpu.VMEM((nq, hps, 1, tq), F32),
                        pltpu.VMEM((nq, vw, tq), F32)]
        + 2 * [pltpu.VMEM((hps, tq, tq), F32)] + 2 * [pltpu.VMEM((hps, N_META, tq), F32)]
        + 2 * [pltpu.VMEM((hps, tq, tq), BF16)] + 2 * [pltpu.VMEM((hps, N_META, tq), BF16)]
        + 2 * [pltpu.VMEM((hps, 1, tq), F32)],
        compiler_params=_cparams(2),
        name="mla_attn",
    )(tab, z4, z4, z4)
    return o.reshape(nblk * nb * T_BLK, MLA_HEADS * MLA_V)


def _time_major_perm(nb):
    p = np.zeros((T_BLK * nb, T_BLK * nb), np.float32)
    for b in range(nb):
        for t in range(T_BLK):
            p[t * nb + b, b * T_BLK + t] = 1.0
    return p


def _block_row_bcast(x, m, row):
    t, c = x.shape
    if m == t:
        return jnp.broadcast_to(x[row:row + 1, :], x.shape)
    x3 = x.reshape(t // m, m, c)
    return jnp.broadcast_to(x3[:, row:row + 1, :], x3.shape).reshape(t, c)


def _hgrn_gates(blk0, n_blk, f_ref, lb_ref):
    lb = lb_ref[...]
    zf = _load_rows(f_ref, blk0, n_blk, slice(0, HG_QK)).astype(F32)
    e = jnp.exp(-jnp.abs(zf))
    rcp = 1.0 / (1.0 + e)
    pos = zf >= 0.0
    sig_p = jnp.where(pos, rcp, e * rcp)
    sig_n = jnp.where(pos, e * rcp, rcp)
    f = lb + (1.0 - lb) * sig_p
    log_f = jnp.log2(jnp.maximum(f, HG_F_MIN))
    k = (1.0 - lb) * sig_n
    hi = log_f.astype(BF16)
    lo = (log_f - hi.astype(F32)).astype(BF16)
    return k, hi, lo


def _hgrn_mix(blk0, n_blk, gates, q_ref, v_ref, tri_ref, lvl_ref, st_ref):
    t = n_blk * T_BLK
    cs = slice(0, HG_QK)
    heads = [slice(h * HG_KEY, (h + 1) * HG_KEY) for h in range(HG_HEADS)]
    k, hi, lo = gates
    q = _load_rows(q_ref, blk0, n_blk, cs).astype(F32)
    v = _load_rows(v_ref, blk0, n_blk, cs)
    tri = tri_ref[0:t, 0:t]
    cum = _dot(tri, hi) + _dot(tri, lo)
    lvl = lvl_ref[0:t, 0:t]
    r_idx = lax.broadcasted_iota(jnp.int32, (t, HG_QK), 0)
    c8 = _block_row_bcast(cum, HG_BOTTOM, HG_BOTTOM // 2 - 1)
    qe = (q * jnp.exp2(cum - c8)).astype(BF16)
    ke = (k * jnp.exp2(c8 - cum)).astype(BF16)
    n_lvl = len(HG_LEVELS)
    scores = [jnp.where(lvl == n_lvl, _dot_nt(qe[:, hs], ke[:, hs]), 0.0) for hs in heads]
    for li, m in enumerate(HG_LEVELS):
        if m > t:
            continue
        half = m // 2
        cmid = _block_row_bcast(cum, m, half - 1)
        upper = (r_idx & (m - 1)) >= half
        ex = jnp.exp2(jnp.where(upper, cum - cmid, cmid - cum))
        qe = jnp.where(upper, q * ex, 0.0).astype(BF16)
        ke = jnp.where(upper, 0.0, k * ex).astype(BF16)
        scores = [jnp.where(lvl == li, _dot_nt(qe[:, hs], ke[:, hs]), s) for s, hs in zip(scores, heads)]
    qd = (q * jnp.exp2(cum)).astype(BF16)
    last = cum[t - 1:t, :]
    kd = (k * jnp.exp2(last - cum)).astype(BF16)
    dec = jnp.exp2(last)
    outs = []
    for h, hs in enumerate(heads):
        st = st_ref[h]
        outs.append(_dot(scores[h].astype(BF16), v[:, hs]) + _dot_nt(qd[:, hs], st.astype(BF16)))
        st_ref[h] = st * dec[:, hs] + _dot_tn(v[:, hs], kd[:, hs])
    return tuple(outs)


def _hgrn_out(blk0, n_blk, outs, g_ref, on_ref, o_ref):
    cs = slice(0, HG_VW)
    g = _load_rows(g_ref, blk0, n_blk, cs).astype(F32)
    o = jnp.concatenate([o * lax.rsqrt(jnp.mean(o * o, axis=-1, keepdims=True) + 1e-6) for o in outs], axis=1)
    _store_rows(o_ref, blk0, n_blk, cs, (o * on_ref[...] * (g * _sigmoid(g))).astype(BF16))


def _hgrn_kernel(q_ref, f_ref, v_ref, g_ref, lb_ref, on_ref, tri_ref, lvl_ref, o_ref, st_ref, *, n_chunks):
    bpc = HG_CHUNK // T_BLK
    gates = functools.partial(_hgrn_gates, f_ref=f_ref, lb_ref=lb_ref)
    mix = functools.partial(_hgrn_mix, q_ref=q_ref, v_ref=v_ref, tri_ref=tri_ref, lvl_ref=lvl_ref, st_ref=st_ref)
    out = functools.partial(_hgrn_out, g_ref=g_ref, on_ref=on_ref, o_ref=o_ref)
    st_ref[...] = jnp.zeros_like(st_ref)
    out(0, 1, mix(0, 1, gates(0, 1)))

    def body(c, carry):
        blk0 = 1 + c * bpc
        out(blk0, bpc, mix(blk0, bpc, gates(blk0, bpc)))
        return carry

    lax.fori_loop(0, n_chunks, body, 0)


def _hgrn(zall, lb, onorm, tri, lvl, nb, nblk, li):
    z4 = zall.reshape(nblk, nb, T_BLK, ZALL_W)
    col = lambda j: _seq_spec(nblk, HG_QK, lambda b: OFF_HG // HG_QK + j)
    o = pl.pallas_call(
        functools.partial(_hgrn_kernel, n_chunks=(nblk - 1) * T_BLK // HG_CHUNK),
        out_shape=jax.ShapeDtypeStruct((nblk, nb, T_BLK, HG_VW), BF16),
        grid=(nb,),
        in_specs=[col(0), col(1), col(2), col(3),
                  _const_spec((1, HG_QK), li), _const_spec((1, HG_VW), li),
                  _const_spec((HG_CHUNK, HG_CHUNK)), _const_spec((HG_CHUNK, HG_CHUNK))],
        out_specs=_seq_spec(nblk, HG_VW, lambda b: 0),
        scratch_shapes=[pltpu.VMEM((HG_HEADS, HG_VAL, HG_KEY), F32)],
        compiler_params=_cparams(1),
        name="hgrn2",
    )(z4, z4, z4, z4, lb, onorm, tri, lvl)
    return o.reshape(nblk * nb * T_BLK, HG_VW)


def _hgrn_level_matrix():
    r = np.arange(HG_CHUNK)[:, None]
    c = np.arange(HG_CHUNK)[None, :]
    lvl = np.zeros((HG_CHUNK, HG_CHUNK), np.int32)
    for li, m in enumerate(HG_LEVELS):
        lvl = np.where(r // m == c // m, li, lvl)
    lvl = np.where(r // HG_BOTTOM == c // HG_BOTTOM, len(HG_LEVELS), lvl)
    return np.where(c <= r, lvl, -1).astype(np.int32)


def _back_kernel(om_ref, os_ref, oh_ref, gm_ref, gs_ref, gh_ref, h_ref,
                 wm_ref, ws_ref, wh_ref, wo_ref, g1_ref, b1_ref,
                 wg_ref, wu_ref, wd_ref, g2_ref, b2_ref, o_ref, r1_ref, r2_ref, *, alpha):
    @pl.when(pl.program_id(0) == 0)
    def _():
        r1_ref[...] = jnp.zeros_like(r1_ref)
        r2_ref[...] = jnp.zeros_like(r2_ref)

    r1 = r1_ref[...]
    r2 = r2_ref[...]
    ym = _dot(om_ref[...], wm_ref[...])
    ys = _dot(os_ref[...], ws_ref[...])
    yh = _dot(oh_ref[...], wh_ref[...])
    o_ref[...] = _layer_norm(r2, g2_ref[...], b2_ref[...]).reshape(o_ref.shape)
    h1 = _layer_norm(r1, g1_ref[...], b1_ref[...])
    hb = h1.astype(BF16)
    a = _dot(hb, wg_ref[...])
    u = _dot(hb, wu_ref[...])
    mixed = _sigmoid(gm_ref[...].astype(F32)) * ym
    mixed += _sigmoid(gs_ref[...].astype(F32)) * ys
    mixed += _sigmoid(gh_ref[...].astype(F32)) * yh
    r1_ref[...] = alpha * h_ref[...] + _dot(mixed.astype(BF16), wo_ref[...])
    r2_ref[...] = alpha * h1 + _dot((a * _sigmoid(a) * u).astype(BF16), wd_ref[...])


def _back(om, oh, zall, h, wm, ws, wh, wo, g1, b1, wg, wu, wd, g2, b2, nb, alpha, final, li):
    r, d = h.shape
    bw = om.shape[1]
    dff = wg.shape[-1]
    rb = T_BLK * nb
    skip = 1 if final else 0
    n_blk = r // rb - skip
    lag = 2
    row = lambda width, j=0: pl.BlockSpec((rb, width), lambda i: (jnp.minimum(i, n_blk - 1) + skip, j))
    if final:
        out_shape = jax.ShapeDtypeStruct((nb, n_blk * T_BLK, d), F32)
        out_spec = pl.BlockSpec((nb, T_BLK, d), lambda i: (0, jnp.maximum(i - lag, 0), 0))
    else:
        out_shape = jax.ShapeDtypeStruct((r, d), F32)
        out_spec = pl.BlockSpec((rb, d), lambda i: (jnp.maximum(i - lag, 0), 0))
    return pl.pallas_call(
        functools.partial(_back_kernel, alpha=alpha),
        out_shape=out_shape,
        grid=(n_blk + lag,),
        in_specs=[row(bw), row(bw, OFF_Y // bw), row(bw),
                  row(d, OFF_G // d), row(d, OFF_G // d + 1), row(d, OFF_G // d + 2), row(d),
                  _const_spec((bw, d), li), _const_spec((bw, d), li), _const_spec((bw, d), li),
                  _const_spec((d, d), li), _const_spec((1, d), li), _const_spec((1, d), li),
                  _const_spec((d, dff), li), _const_spec((d, dff), li), _const_spec((dff, d), li),
                  _const_spec((1, d), li), _const_spec((1, d), li)],
        out_specs=out_spec,
        scratch_shapes=[pltpu.VMEM((rb, d), F32), pltpu.VMEM((rb, d), F32)],
        compiler_params=_cparams(1, "arbitrary"),
        name="merge_ffn",
    )(om, zall, oh, zall, zall, zall, h, wm, ws, wh, wo, g1, b1, wg, wu, wd, g2, b2)


def _permute_w_in(w):
    mla_in = MLA_Q_RANK + MLA_KV_RANK + MLA_ROPE
    s5_0, hg_0 = mla_in, mla_in + S5_WIDTH
    g_0 = hg_0 + 2 * HG_QK + 2 * HG_VW
    pad = jnp.zeros((w.shape[0], ZKR_W - MLA_ROPE), w.dtype)
    return jnp.concatenate([w[:, s5_0:hg_0], w[:, 0:MLA_Q_RANK + MLA_KV_RANK], w[:, hg_0:g_0], w[:, g_0:],
                            w[:, MLA_Q_RANK + MLA_KV_RANK:mla_in], pad], axis=1).astype(BF16)


def _rot_half(x):
    x1, x2 = jnp.split(x, 2, axis=-1)
    return jnp.concatenate([-x2, x1], axis=-1)


def _mla_weights(w_uq, w_ukv):
    rq, rkv = w_uq.shape[0], w_ukv.shape[0]
    zpad = HEAD_PAD - MLA_NOPE - MLA_ROPE
    wq = w_uq.reshape(rq, MLA_HEADS, MLA_NOPE + MLA_ROPE)
    q_nope, q_rope = wq[..., :MLA_NOPE], wq[..., MLA_NOPE:]
    zq = jnp.zeros((rq, MLA_HEADS, zpad), w_uq.dtype)
    wq_p = jnp.concatenate([q_nope, q_rope, zq], axis=-1).reshape(rq, -1)
    wq_r = jnp.concatenate([jnp.zeros_like(q_nope), _rot_half(q_rope), zq], axis=-1).reshape(rq, -1)
    wkv = w_ukv.reshape(rkv, MLA_HEADS, MLA_NOPE + MLA_V)
    zk = jnp.zeros((rkv, MLA_HEADS, HEAD_PAD - MLA_NOPE), w_ukv.dtype)
    wk_p = jnp.concatenate([wkv[..., :MLA_NOPE], zk], axis=-1).reshape(rkv, -1)
    wv = wkv[..., MLA_NOPE:].reshape(rkv, -1)
    eye = jnp.eye(ZKR_W, MLA_ROPE, dtype=F32)
    place = jnp.concatenate([jnp.zeros((ZKR_W, MLA_NOPE), F32), eye, jnp.zeros((ZKR_W, zpad), F32)], axis=-1)
    place_r = jnp.concatenate([jnp.zeros((ZKR_W, MLA_NOPE), F32), _rot_half(eye),
                               jnp.zeros((ZKR_W, zpad), F32)], axis=-1)
    e = jnp.tile(place, (1, MLA_HEADS))
    er = jnp.tile(place_r, (1, MLA_HEADS))
    return [a.astype(BF16) for a in (wq_p, wq_r, wk_p, wv, e, er)]


def _rope_consts():
    half = MLA_ROPE // 2
    inv = ROPE_THETA ** (-(jnp.arange(0, MLA_ROPE, 2, dtype=F32) / MLA_ROPE))
    place = np.zeros((half, HEAD_PAD), np.float32)
    place[np.arange(half), MLA_NOPE + np.arange(half)] = 1.0
    place[np.arange(half), MLA_NOPE + half + np.arange(half)] = 1.0
    base = np.zeros((1, HEAD_PAD), np.float32)
    base[0, :MLA_NOPE] = 1.0
    return inv[:, None], jnp.asarray(place, BF16), jnp.asarray(base)


def _s5_params(lam_re, lam_im, log_dt, b_re, b_im, c_re, c_im):
    lr = jnp.minimum(lam_re.astype(F32), -1e-4)
    li = lam_im.astype(F32)
    dt = jnp.exp(log_dt.astype(F32))[:, None]
    mag = jnp.exp(lr * dt)
    ab_r = mag * jnp.cos(li * dt)
    ab_i = mag * jnp.sin(li * dt)
    den = lr * lr + li * li
    nr = ab_r - 1.0
    coef_r = ((nr * lr + ab_i * li) / den)[..., None]
    coef_i = ((ab_i * lr - nr * li) / den)[..., None]
    bb_r = coef_r * b_re.astype(F32) - coef_i * b_im.astype(F32)
    bb_i = coef_r * b_im.astype(F32) + coef_i * b_re.astype(F32)
    n_slab = S5_WIDTH // S5_SLAB
    gps = S5_SLAB // S5_GROUP
    eye = jnp.eye(gps, dtype=F32)

    def in_mat(bb):
        b4 = bb.reshape(n_slab, gps, S5_STATE, S5_GROUP)
        return jnp.einsum('jgnc,gh->jgchn', b4, eye).reshape(n_slab, S5_SLAB, S5_SLAB_STATE)

    def out_mat(cc):
        c4 = cc.astype(F32).reshape(n_slab, gps, S5_GROUP, S5_STATE)
        return jnp.einsum('jgcn,gh->jgnhc', c4, eye).reshape(n_slab, S5_SLAB_STATE, S5_SLAB)

    bm = jnp.concatenate([in_mat(bb_r), in_mat(bb_i)], axis=2).astype(BF16)
    cm = jnp.concatenate([out_mat(c_re), -out_mat(c_im)], axis=1).astype(BF16)
    ar = ab_r.reshape(n_slab, 1, S5_SLAB_STATE)
    ai = ab_i.reshape(n_slab, 1, S5_SLAB_STATE)
    return bm, cm, ar, ai


def kernel(x, positions, meta_tokens, ln_in_g, ln_in_b, w_in, mla_q_norm, mla_w_uq, mla_kv_norm, mla_w_ukv,
           s5_lam_re, s5_lam_im, s5_log_dt, s5_b_re, s5_b_im, s5_c_re, s5_c_im, s5_d, s5_w_glu,
           hg_lb_logits, hg_out_norm, w_br_mla, w_br_s5, w_br_hg, w_out, ln1_g, ln1_b,
           w_ffn_gate, w_ffn_up, w_ffn_down, ln2_g, ln2_b):
    nb, s, d = x.shape
    depth = w_in.shape[0]
    nblk = (s + N_META) // T_BLK
    alpha = (2 * depth) ** 0.25
    row2 = lambda a: a.astype(F32)[None, :]
    rows3 = lambda a: a.astype(F32)[:, None, :]
    bf16 = lambda a: a.astype(BF16)

    meta_pos = jnp.broadcast_to(jnp.arange(N_META, dtype=jnp.int32)[None, :], (nb, N_META))
    pos = jnp.concatenate([meta_pos, positions.astype(jnp.int32) + N_META], axis=1)
    pos = pos.reshape(nb, nblk, T_BLK).transpose(1, 0, 2).reshape(nblk, 1, nb * T_BLK)
    rope = _rope_consts()
    p_lb = jax.nn.softmax(hg_lb_logits.astype(F32), axis=0)
    lower_bounds = jnp.cumsum(p_lb, axis=0) - p_lb[0]
    tri = jnp.asarray(np.tril(np.ones((HG_CHUNK, HG_CHUNK), np.float32)), BF16)
    lvl = jnp.asarray(_hgrn_level_matrix())
    perm = jnp.asarray(_time_major_perm(nb), BF16)

    front_p = (jax.vmap(_permute_w_in)(w_in), pos, rope, rows3(mla_q_norm), rows3(mla_kv_norm),
               *jax.vmap(_mla_weights)(mla_w_uq, mla_w_ukv), perm,
               *jax.vmap(_s5_params)(s5_lam_re, s5_lam_im, s5_log_dt, s5_b_re, s5_b_im, s5_c_re, s5_c_im),
               rows3(s5_d), bf16(s5_w_glu))
    hgrn_p = (lower_bounds[:, None, :], rows3(hg_out_norm), tri, lvl)
    back_p = (bf16(w_br_mla), bf16(w_br_s5), bf16(w_br_hg), bf16(w_out), rows3(ln1_g), rows3(ln1_b),
              bf16(w_ffn_gate), bf16(w_ffn_up), bf16(w_ffn_down), rows3(ln2_g), rows3(ln2_b))

    h = (x, meta_tokens.astype(x.dtype), row2(ln_in_g), row2(ln_in_b))
    for li in range(depth):
        outs = _front(h, *front_p, nb, li)
        if li == 0:
            h, *outs = outs
        zall, = outs
        o_mla = _attention(zall, nb, nblk)
        o_hg = _hgrn(zall, *hgrn_p, nb, nblk, li)
        h = _back(o_mla, o_hg, zall, h, *back_p, nb, alpha, final=li == depth - 1, li=li)
    return h
```

```python
import functools
import math

import jax
import jax.numpy as jnp
import numpy as np
from jax import lax
from jax.experimental import pallas as pl
from jax.experimental.pallas import tpu as pltpu

F32 = jnp.float32
BF16 = jnp.bfloat16

N_META = 16
MLA_HEADS = 8
MLA_NOPE = 64
MLA_ROPE = 32
MLA_V = 64
MLA_Q_RANK = 256
MLA_KV_RANK = 256
ROPE_THETA = 10000.0
MASK_VALUE = -1e9
LOG2_E = math.log2(math.e)
HEAD_PAD = 128
S5_WIDTH = 512
S5_GROUP = 16
S5_GROUPS = S5_WIDTH // S5_GROUP
S5_STATE = 64
S5_SLAB = 128
S5_SLAB_STATE = (S5_SLAB // S5_GROUP) * S5_STATE
HG_HEADS = 4
HG_KEY = 128
HG_VAL = 128
HG_QK = HG_HEADS * HG_KEY
HG_VW = HG_HEADS * HG_VAL
HG_F_MIN = 1e-6
HG_CHUNK = 128
HG_LEVELS = (128, 64, 32, 16)
HG_BOTTOM = 8
N_BRANCH = 3
T_BLK = 16
ATT_TQ = 256
ATT_HEADS_PER_STEP = 4
VMEM_LIMIT = 56 * 1024 * 1024

ZA_W = 1024
ZHG_W = 2048
ZG_W = 3072
ZKR_W = 128
Z_W = ZA_W + ZHG_W + ZG_W + ZKR_W
ZALL_WIDTHS = (ZHG_W, ZG_W, MLA_HEADS * HEAD_PAD, MLA_HEADS * HEAD_PAD, MLA_HEADS * MLA_V, S5_WIDTH)
ZALL_OFF = tuple(sum(ZALL_WIDTHS[:i]) for i in range(len(ZALL_WIDTHS)))
ZALL_W = sum(ZALL_WIDTHS)
OFF_HG, OFF_G, OFF_Q, OFF_K, OFF_V, OFF_Y = ZALL_OFF
WIDE_CHUNK = 512
FILL_BEFORE_MLA = 3


def _cparams(n_grid, sem="parallel"):
    return pltpu.CompilerParams(dimension_semantics=(sem,) * n_grid, vmem_limit_bytes=VMEM_LIMIT)


def _const_spec(shape, layer=None):
    nd = len(shape)
    if layer is None:
        return pl.BlockSpec(shape, lambda *_: (0,) * nd, pipeline_mode=pl.Buffered(1))
    return pl.BlockSpec((None,) + tuple(shape), lambda *_: (layer,) + (0,) * nd, pipeline_mode=pl.Buffered(1))


def _seq_spec(nblk, width, col):
    return pl.BlockSpec((nblk, None, T_BLK, width), lambda b, *g: (0, b, 0, col(b, *g)))


def _dot(a, b):
    return jnp.dot(a, b, preferred_element_type=F32)


def _dot_nt(a, b):
    return lax.dot_general(a, b, (((1,), (1,)), ((), ())), preferred_element_type=F32)


def _dot_tn(a, b):
    return lax.dot_general(a, b, (((0,), (0,)), ((), ())), preferred_element_type=F32)


def _sigmoid(x):
    return 1.0 / (1.0 + jnp.exp(-x))


def _layer_norm(x, g, b, eps=1e-5):
    mu = jnp.mean(x, axis=-1, keepdims=True)
    xc = x - mu
    var = jnp.mean(xc * xc, axis=-1, keepdims=True)
    return xc * lax.rsqrt(var + eps) * g + b


def _rms_norm(x, g, eps=1e-6):
    return x * lax.rsqrt(jnp.mean(x * x, axis=-1, keepdims=True) + eps) * g


def _load_rows(ref, blk0, n_blk, cs):
    x = ref[pl.ds(blk0, n_blk), :, cs]
    return x.reshape(n_blk * T_BLK, x.shape[-1])


def _store_rows(ref, blk0, n_blk, cs, x):
    ref[pl.ds(blk0, n_blk), :, cs] = x.reshape(n_blk, T_BLK, x.shape[-1])


def _rope_tables(pos_ref, inv_ref, place_ref, base_ref):
    ang = inv_ref[...] * pos_ref[...].astype(F32)

    def place(t):
        hi = t.astype(BF16)
        lo = (t - hi.astype(F32)).astype(BF16)
        return _dot_tn(hi, place_ref[...]) + _dot_tn(lo, place_ref[...])

    return place(jnp.cos(ang)) + base_ref[...], place(jnp.sin(ang))


def _mla_prep(cq, ckv, kr, cos, sin, qn_ref, kvn_ref, wq_ref, wk_ref, wv_ref, q_ref, k_ref, v_ref):
    scale = (MLA_NOPE + MLA_ROPE) ** -0.5 * LOG2_E
    half = MLA_ROPE // 2
    lane = lax.broadcasted_iota(jnp.int32, (1, HEAD_PAD), 1)
    sin_up = jnp.where((lane >= MLA_NOPE + half) & (lane < MLA_NOPE + MLA_ROPE), sin, 0.0)
    sin_dn = jnp.where((lane >= MLA_NOPE) & (lane < MLA_NOPE + half), -sin, 0.0)

    def rope(x, c, s_up, s_dn):
        return x * c + pltpu.roll(x, half, 1) * s_up + pltpu.roll(x, HEAD_PAD - half, 1) * s_dn

    cqn = _rms_norm(cq, qn_ref[...]).astype(BF16)
    ckvn = _rms_norm(ckv, kvn_ref[...]).astype(BF16)
    k_rope = rope(kr, cos, sin_up, sin_dn)
    cos_q, sin_up_q, sin_dn_q = cos * scale, sin_up * scale, sin_dn * scale
    for h in range(MLA_HEADS):
        cs = slice(h * HEAD_PAD, (h + 1) * HEAD_PAD)
        q_ref[:, cs] = rope(_dot(cqn, wq_ref[:, cs]), cos_q, sin_up_q, sin_dn_q).astype(BF16)
        k_ref[:, cs] = (_dot(ckvn, wk_ref[:, cs]) + k_rope).astype(BF16)
    v_ref[...] = _dot(ckvn, wv_ref[...]).astype(BF16)


def _s5_block(u, perm_ref, bm_ref, cm_ref, ar_ref, ai_ref, d_ref, wg_ref, o_ref,
              xr_ref, xi_ref, buf_ref, y_ref, nb, fillers):
    n_slab = S5_WIDTH // S5_SLAB
    ns = S5_SLAB_STATE
    u = _dot(perm_ref[...], u).astype(BF16)
    for j in range(n_slab):
        buf_ref[...] = _dot(u[:, j * S5_SLAB:(j + 1) * S5_SLAB], bm_ref[j])
        if fillers:
            fillers.pop(0)()
        ar = jnp.broadcast_to(ar_ref[j], (nb, ns))
        ai = jnp.broadcast_to(ai_ref[j], (nb, ns))
        xr = xr_ref[j]
        xi = xi_ref[j]
        for t in range(T_BLK):
            rows = slice(t * nb, (t + 1) * nb)
            nr = ar * xr - ai * xi + buf_ref[rows, 0:ns]
            ni = ar * xi + ai * xr + buf_ref[rows, ns:2 * ns]
            buf_ref[rows, 0:ns] = nr
            buf_ref[rows, ns:2 * ns] = ni
            xr, xi = nr, ni
        xr_ref[j] = xr
        xi_ref[j] = xi
        y_ref[:, j * S5_SLAB:(j + 1) * S5_SLAB] = _dot(buf_ref[...].astype(BF16), cm_ref[j])

    while fillers:
        fillers.pop(0)()
    y = y_ref[...] + d_ref[...] * u.astype(F32)
    y = 0.5 * y * (1.0 + jnp.tanh(math.sqrt(2.0 / math.pi) * (y + 0.044715 * (y * y * y))))
    gate = _sigmoid(_dot(y.astype(BF16), wg_ref[...]))
    out = (y * gate).astype(BF16)
    o_ref[...] = _dot_tn(perm_ref[...], out).astype(BF16)


def _front_kernel(*refs, nb, ln_in):
    if ln_in:
        x_ref, meta_ref, lng_ref, lnb_ref, *refs = refs
    else:
        h_ref, *refs = refs
    (w_ref, pos_ref, inv_ref, place_ref, base_ref, qn_ref, kvn_ref, wq_ref, wk_ref, wv_ref,
     perm_ref, bm_ref, cm_ref, ar_ref, ai_ref, d_ref, wglu_ref, *refs) = refs
    if ln_in:
        hout_ref, *refs = refs
    zall_ref, xr_ref, xi_ref, buf_ref, y_ref = refs
    zhg_ref, zg_ref, q_ref, k_ref, v_ref, ys5_ref = (
        zall_ref.at[:, o:o + w] for o, w in zip(ZALL_OFF, ZALL_WIDTHS))

    @pl.when(pl.program_id(0) == 0)
    def _():
        xr_ref[...] = jnp.zeros_like(xr_ref)
        xi_ref[...] = jnp.zeros_like(xi_ref)

    cos, sin = _rope_tables(pos_ref, inv_ref, place_ref, base_ref)
    if ln_in:
        hx = _layer_norm(x_ref[...], lng_ref[...], lnb_ref[...])
        hm = _layer_norm(meta_ref[...], lng_ref[...], lnb_ref[...])
        h = jnp.where(pl.program_id(0) == 0, jnp.broadcast_to(hm[None], hx.shape), hx)
        h = h.reshape(hout_ref.shape)
        hout_ref[...] = h
    else:
        h = h_ref[...]
    x = h.astype(BF16)
    za = _dot(x, w_ref[:, 0:ZA_W])
    zkr = _dot(x, w_ref[:, ZA_W + ZHG_W + ZG_W:Z_W])

    def wide_chunk(o_ref, c, off):
        def run():
            o_ref[:, c:c + WIDE_CHUNK] = _dot(x, w_ref[:, off + c:off + c + WIDE_CHUNK]).astype(BF16)
        return run

    fillers = [wide_chunk(zhg_ref, c, ZA_W) for c in range(0, ZHG_W, WIDE_CHUNK)]
    fillers += [wide_chunk(zg_ref, c, ZA_W + ZHG_W) for c in range(0, ZG_W, WIDE_CHUNK)]
    for _ in range(FILL_BEFORE_MLA):
        fillers.pop(0)()
    c0 = S5_WIDTH
    _mla_prep(za[:, c0:c0 + MLA_Q_RANK], za[:, c0 + MLA_Q_RANK:c0 + MLA_Q_RANK + MLA_KV_RANK], zkr,
              cos, sin, qn_ref, kvn_ref, wq_ref, wk_ref, wv_ref, q_ref, k_ref, v_ref)
    _s5_block(za[:, 0:S5_WIDTH].astype(BF16), perm_ref, bm_ref, cm_ref, ar_ref, ai_ref, d_ref, wglu_ref, ys5_ref,
              xr_ref, xi_ref, buf_ref, y_ref, nb, fillers)


def _front(src, w, pos, rope, qn, kvn, wq, wk, wv, perm, bm, cm, ar, ai, dskip, wglu, nb, li):
    ln_in = isinstance(src, tuple)
    rb = T_BLK * nb
    if ln_in:
        x, meta, lng, lnb = src
        d = x.shape[-1]
        r = (x.shape[1] + N_META) * nb
        src_specs = [pl.BlockSpec((nb, T_BLK, d), lambda i: (0, jnp.maximum(i - 1, 0), 0)),
                     _const_spec((N_META, d)), _const_spec((1, d)), _const_spec((1, d))]
    else:
        src = (src,)
        r, d = src[0].shape
        src_specs = [pl.BlockSpec((rb, d), lambda i: (i, 0))]
    hw = MLA_HEADS * HEAD_PAD
    vw = MLA_HEADS * MLA_V
    n_slab = S5_WIDTH // S5_SLAB
    row = lambda width: pl.BlockSpec((rb, width), lambda i: (i, 0))
    out_shape = [jax.ShapeDtypeStruct((r, ZALL_W), BF16)]
    out_specs = [row(ZALL_W)]
    if ln_in:
        out_shape.insert(0, jax.ShapeDtypeStruct((r, d), F32))
        out_specs.insert(0, row(d))
    return pl.pallas_call(
        functools.partial(_front_kernel, nb=nb, ln_in=ln_in),
        out_shape=out_shape,
        grid=(r // rb,),
        in_specs=src_specs + [
            _const_spec((d, Z_W), li), pl.BlockSpec((None, 1, rb), lambda i: (i, 0, 0)),
            _const_spec((MLA_ROPE // 2, 1)), _const_spec((MLA_ROPE // 2, HEAD_PAD)), _const_spec((1, HEAD_PAD)),
            _const_spec((1, MLA_Q_RANK), li), _const_spec((1, MLA_KV_RANK), li),
            _const_spec((MLA_Q_RANK, hw), li), _const_spec((MLA_KV_RANK, hw), li), _const_spec((MLA_KV_RANK, vw), li),
            _const_spec((rb, rb)),
            _const_spec((n_slab, S5_SLAB, 2 * S5_SLAB_STATE), li),
            _const_spec((n_slab, 2 * S5_SLAB_STATE, S5_SLAB), li),
            _const_spec((n_slab, 1, S5_SLAB_STATE), li),
            _const_spec((n_slab, 1, S5_SLAB_STATE), li),
            _const_spec((1, S5_WIDTH), li),
            _const_spec((S5_WIDTH, S5_WIDTH), li),
        ],
        out_specs=out_specs,
        scratch_shapes=[
            pltpu.VMEM((n_slab, nb, S5_SLAB_STATE), F32),
            pltpu.VMEM((n_slab, nb, S5_SLAB_STATE), F32),
            pltpu.VMEM((rb, 2 * S5_SLAB_STATE), F32),
            pltpu.VMEM((rb, S5_WIDTH), F32),
        ],
        compiler_params=_cparams(1, "arbitrary"),
        name="front",
    )(*src, w, pos, *rope, qn, kvn, wq, wk, wv, perm, bm, cm, ar, ai, dskip, wglu)


def _attn_kernel(tab_ref, q_ref, k_ref, v_ref, o_ref, vt_ref, qt_ref, m_ref, l_ref, acc_ref, *bufs, n_heads, tq, nq):
    xs_ref, xm_ref, ps_ref, pm_ref, a_ref = (bufs[2 * i:2 * i + 2] for i in range(5))
    bpq = tq // T_BLK
    vw = n_heads * MLA_V

    for j in range(nq):
        vb = _load_rows(v_ref, 1 + j * bpq, bpq, slice(0, vw))
        vt_ref[j] = vb.astype(F32).T.astype(BF16)
    v0t = v_ref[0].astype(F32).T.astype(BF16)

    for i in range(nq):
        qb = _load_rows(q_ref, 1 + i * bpq, bpq, slice(0, n_heads * HEAD_PAD))
        qt_ref[i] = qb.astype(F32).T.astype(BF16)

    def causal(st):
        r = lax.broadcasted_iota(jnp.int32, st.shape, 0)
        c = lax.broadcasted_iota(jnp.int32, st.shape, 1)
        return jnp.where(r <= c, st, MASK_VALUE * LOG2_E)

    heads = range(n_heads)
    qs = [slice(h * HEAD_PAD, (h + 1) * HEAD_PAD) for h in heads]
    vs = [slice(h * MLA_V, (h + 1) * MLA_V) for h in heads]
    k0 = [k_ref[0, :, c] for c in qs]

    outs = []
    for n in heads:
        st = causal(_dot_nt(k0[n], q_ref[0, :, qs[n]]))
        p = jnp.exp2(st - jnp.max(st, axis=0, keepdims=True))
        outs.append(_dot(v0t[vs[n], :], p.astype(BF16)) / jnp.sum(p, axis=0, keepdims=True))
    o_ref[0, :, :] = jnp.concatenate(outs, axis=0).T.astype(BF16)

    def key_tile(j, n):
        return _load_rows(k_ref, 1 + j * bpq, bpq, qs[n])

    def diag_scores(i, s):
        for n in heads:
            xs_ref[s][n] = _dot(key_tile(i, n), qt_ref[i, qs[n], :])
            xm_ref[s][n] = _dot(k0[n], qt_ref[i, qs[n], :])

    def diag_stats(i, s):
        for n in heads:
            st, sm = causal(xs_ref[s][n]), xm_ref[s][n]
            m = jnp.maximum(jnp.max(st, axis=0, keepdims=True), jnp.max(sm, axis=0, keepdims=True))
            p = jnp.exp2(st - m)
            pm = jnp.exp2(sm - m)
            m_ref[i, n] = m
            l_ref[i, n] = jnp.sum(p, axis=0, keepdims=True) + jnp.sum(pm, axis=0, keepdims=True)
            ps_ref[s][n] = p.astype(BF16)
            pm_ref[s][n] = pm.astype(BF16)

    def diag_values(i, s):
        for n in heads:
            acc_ref[i, vs[n], :] = _dot(vt_ref[i, vs[n], :], ps_ref[s][n]) + _dot(v0t[vs[n], :], pm_ref[s][n])

    _pipeline3(nq, diag_scores, diag_stats, diag_values)

    def off_scores(f, s):
        i, j = tab_ref[0, f], tab_ref[1, f]
        for n in heads:
            xs_ref[s][n] = _dot(key_tile(j, n), qt_ref[i, qs[n], :])

    def off_stats(f, s):
        i = tab_ref[0, f]
        for n in heads:
            x = xs_ref[s][n]
            m_old = m_ref[i, n]
            m = jnp.maximum(m_old, jnp.max(x, axis=0, keepdims=True))
            a = jnp.exp2(m_old - m)
            p = jnp.exp2(x - m)
            m_ref[i, n] = m
            l_ref[i, n] = a * l_ref[i, n] + jnp.sum(p, axis=0, keepdims=True)
            a_ref[s][n] = a
            ps_ref[s][n] = p.astype(BF16)

    def off_values(f, s):
        i, j = tab_ref[0, f], tab_ref[1, f]
        for n in heads:
            acc_ref[i, vs[n], :] = a_ref[s][n] * acc_ref[i, vs[n], :] + _dot(vt_ref[j, vs[n], :], ps_ref[s][n])

    _pipeline3(nq * (nq - 1) // 2, off_scores, off_stats, off_values)

    for i in range(nq):
        o = jnp.concatenate([acc_ref[i, vs[n], :] / l_ref[i, n] for n in heads], axis=0)
        _store_rows(o_ref, 1 + i * bpq, bpq, slice(0, vw), o.T.astype(BF16))


def _pipeline3(n, stage1, stage2, stage3):
    if n == 0:
        return
    stage1(0, 0)
    if n > 1:
        stage1(1, 1)
    stage2(0, 0)

    def step(t, par):
        stage1(t + 2, par)
        stage2(t + 1, 1 - par)
        stage3(t, par)

    steady = max(n - 2, 0)
    if steady >= 2:
        def body(u, carry):
            step(2 * u, 0)
            step(2 * u + 1, 1)
            return carry

        lax.fori_loop(0, steady // 2, body, 0)
    if steady % 2:
        step(steady - 1, (steady - 1) % 2)
    if n > 1:
        stage3(n - 2, (n - 2) % 2)
        stage2(n - 1, (n - 1) % 2)
    stage3(n - 1, (n - 1) % 2)


def _off_diagonal_order(nq):
    left = [(i, j) for i in range(nq) for j in range(i)]
    order = []
    while left:
        count = {}
        for i, _ in left:
            count[i] = count.get(i, 0) + 1
        ok = [p for p in left if not order or p[0] != order[-1][0]] or left
        pick = max(ok, key=lambda p: (count[p[0]], -p[1]))
        order.append(pick)
        left.remove(pick)
    return order or [(0, 0)]


def _attention(zall, nb, nblk):
    z4 = zall.reshape(nblk, nb, T_BLK, ZALL_W)
    hps = ATT_HEADS_PER_STEP
    n_hp = MLA_HEADS // hps
    qw = hps * HEAD_PAD
    vw = hps * MLA_V
    s = (nblk - 1) * T_BLK
    tq = min(ATT_TQ, s)
    nq = s // tq
    col = lambda b, p: p
    tab = jnp.asarray(np.array(_off_diagonal_order(nq), np.int32).T)
    o = pl.pallas_call(
        functools.partial(_attn_kernel, n_heads=hps, tq=tq, nq=nq),
        out_shape=jax.ShapeDtypeStruct((nblk, nb, T_BLK, MLA_HEADS * MLA_V), BF16),
        grid=(nb, n_hp),
        in_specs=[pl.BlockSpec(memory_space=pltpu.SMEM),
                  _seq_spec(nblk, qw, lambda b, p: OFF_Q // qw + p),
                  _seq_spec(nblk, qw, lambda b, p: OFF_K // qw + p),
                  _seq_spec(nblk, vw, lambda b, p: OFF_V // vw + p)],
        out_specs=_seq_spec(nblk, vw, col),
        scratch_shapes=[pltpu.VMEM((nq, vw, tq), BF16), pltpu.VMEM((nq, qw, tq), BF16),
                        pltpu.VMEM((nq, hps, 1, tq), F32), pltpu.VMEM((nq, hps, 1, tq), F32),
                        pltpu.VMEM((nq, vw, tq), F32)]
        + 2 * [pltpu.VMEM((hps, tq, tq), F32)] + 2 * [pltpu.VMEM((hps, N_META, tq), F32)]
        + 2 * [pltpu.VMEM((hps, tq, tq), BF16)] + 2 * [pltpu.VMEM((hps, N_META, tq), BF16)]
        + 2 * [pltpu.VMEM((hps, 1, tq), F32)],
        compiler_params=_cparams(2),
        name="mla_attn",
    )(tab, z4, z4, z4)
    return o.reshape(nblk * nb * T_BLK, MLA_HEADS * MLA_V)


def _time_major_perm(nb):
    p = np.zeros((T_BLK * nb, T_BLK * nb), np.float32)
    for b in range(nb):
        for t in range(T_BLK):
            p[t * nb + b, b * T_BLK + t] = 1.0
    return p


def _block_row_bcast(x, m, row):
    t, c = x.shape
    if m == t:
        return jnp.broadcast_to(x[row:row + 1, :], x.shape)
    x3 = x.reshape(t // m, m, c)
    return jnp.broadcast_to(x3[:, row:row + 1, :], x3.shape).reshape(t, c)


def _hgrn_gates(blk0, n_blk, f_ref, lb_ref):
    lb = lb_ref[...]
    zf = _load_rows(f_ref, blk0, n_blk, slice(0, HG_QK)).astype(F32)
    e = jnp.exp(-jnp.abs(zf))
    rcp = 1.0 / (1.0 + e)
    pos = zf >= 0.0
    sig_p = jnp.where(pos, rcp, e * rcp)
    sig_n = jnp.where(pos, e * rcp, rcp)
    f = lb + (1.0 - lb) * sig_p
    log_f = jnp.log2(jnp.maximum(f, HG_F_MIN))
    k = (1.0 - lb) * sig_n
    hi = log_f.astype(BF16)
    lo = (log_f - hi.astype(F32)).astype(BF16)
    return k, hi, lo


def _hgrn_mix(blk0, n_blk, gates, q_ref, v_ref, tri_ref, lvl_ref, st_ref):
    t = n_blk * T_BLK
    cs = slice(0, HG_QK)
    heads = [slice(h * HG_KEY, (h + 1) * HG_KEY) for h in range(HG_HEADS)]
    k, hi, lo = gates
    q = _load_rows(q_ref, blk0, n_blk, cs).astype(F32)
    v = _load_rows(v_ref, blk0, n_blk, cs)
    tri = tri_ref[0:t, 0:t]
    cum = _dot(tri, hi) + _dot(tri, lo)
    lvl = lvl_ref[0:t, 0:t]
    r_idx = lax.broadcasted_iota(jnp.int32, (t, HG_QK), 0)
    c8 = _block_row_bcast(cum, HG_BOTTOM, HG_BOTTOM // 2 - 1)
    qe = (q * jnp.exp2(cum - c8)).astype(BF16)
    ke = (k * jnp.exp2(c8 - cum)).astype(BF16)
    n_lvl = len(HG_LEVELS)
    scores = [jnp.where(lvl == n_lvl, _dot_nt(qe[:, hs], ke[:, hs]), 0.0) for hs in heads]
    for li, m in enumerate(HG_LEVELS):
        if m > t:
            continue
        half = m // 2
        cmid = _block_row_bcast(cum, m, half - 1)
        upper = (r_idx & (m - 1)) >= half
        ex = jnp.exp2(jnp.where(upper, cum - cmid, cmid - cum))
        qe = jnp.where(upper, q * ex, 0.0).astype(BF16)
        ke = jnp.where(upper, 0.0, k * ex).astype(BF16)
        scores = [jnp.where(lvl == li, _dot_nt(qe[:, hs], ke[:, hs]), s) for s, hs in zip(scores, heads)]
    qd = (q * jnp.exp2(cum)).astype(BF16)
    last = cum[t - 1:t, :]
    kd = (k * jnp.exp2(last - cum)).astype(BF16)
    dec = jnp.exp2(last)
    outs = []
    for h, hs in enumerate(heads):
        st = st_ref[h]
        outs.append(_dot(scores[h].astype(BF16), v[:, hs]) + _dot_nt(qd[:, hs], st.astype(BF16)))
        st_ref[h] = st * dec[:, hs] + _dot_tn(v[:, hs], kd[:, hs])
    return tuple(outs)


def _hgrn_out(blk0, n_blk, outs, g_ref, on_ref, o_ref):
    cs = slice(0, HG_VW)
    g = _load_rows(g_ref, blk0, n_blk, cs).astype(F32)
    o = jnp.concatenate([o * lax.rsqrt(jnp.mean(o * o, axis=-1, keepdims=True) + 1e-6) for o in outs], axis=1)
    _store_rows(o_ref, blk0, n_blk, cs, (o * on_ref[...] * (g * _sigmoid(g))).astype(BF16))


def _hgrn_kernel(q_ref, f_ref, v_ref, g_ref, lb_ref, on_ref, tri_ref, lvl_ref, o_ref, st_ref, *, n_chunks):
    bpc = HG_CHUNK // T_BLK
    gates = functools.partial(_hgrn_gates, f_ref=f_ref, lb_ref=lb_ref)
    mix = functools.partial(_hgrn_mix, q_ref=q_ref, v_ref=v_ref, tri_ref=tri_ref, lvl_ref=lvl_ref, st_ref=st_ref)
    out = functools.partial(_hgrn_out, g_ref=g_ref, on_ref=on_ref, o_ref=o_ref)
    st_ref[...] = jnp.zeros_like(st_ref)
    out(0, 1, mix(0, 1, gates(0, 1)))

    def body(c, carry):
        blk0 = 1 + c * bpc
        out(blk0, bpc, mix(blk0, bpc, gates(blk0, bpc)))
        return carry

    lax.fori_loop(0, n_chunks, body, 0)


def _hgrn(zall, lb, onorm, tri, lvl, nb, nblk, li):
    z4 = zall.reshape(nblk, nb, T_BLK, ZALL_W)
    col = lambda j: _seq_spec(nblk, HG_QK, lambda b: OFF_HG // HG_QK + j)
    o = pl.pallas_call(
        functools.partial(_hgrn_kernel, n_chunks=(nblk - 1) * T_BLK // HG_CHUNK),
        out_shape=jax.ShapeDtypeStruct((nblk, nb, T_BLK, HG_VW), BF16),
        grid=(nb,),
        in_specs=[col(0), col(1), col(2), col(3),
                  _const_spec((1, HG_QK), li), _const_spec((1, HG_VW), li),
                  _const_spec((HG_CHUNK, HG_CHUNK)), _const_spec((HG_CHUNK, HG_CHUNK))],
        out_specs=_seq_spec(nblk, HG_VW, lambda b: 0),
        scratch_shapes=[pltpu.VMEM((HG_HEADS, HG_VAL, HG_KEY), F32)],
        compiler_params=_cparams(1),
        name="hgrn2",
    )(z4, z4, z4, z4, lb, onorm, tri, lvl)
    return o.reshape(nblk * nb * T_BLK, HG_VW)


def _hgrn_level_matrix():
    r = np.arange(HG_CHUNK)[:, None]
    c = np.arange(HG_CHUNK)[None, :]
    lvl = np.zeros((HG_CHUNK, HG_CHUNK), np.int32)
    for li, m in enumerate(HG_LEVELS):
        lvl = np.where(r // m == c // m, li, lvl)
    lvl = np.where(r // HG_BOTTOM == c // HG_BOTTOM, len(HG_LEVELS), lvl)
    return np.where(c <= r, lvl, -1).astype(np.int32)


def _back_kernel(om_ref, os_ref, oh_ref, gm_ref, gs_ref, gh_ref, h_ref,
                 wm_ref, ws_ref, wh_ref, wo_ref, g1_ref, b1_ref,
                 wg_ref, wu_ref, wd_ref, g2_ref, b2_ref, o_ref, r1_ref, r2_ref, *, alpha):
    @pl.when(pl.program_id(0) == 0)
    def _():
        r1_ref[...] = jnp.zeros_like(r1_ref)
        r2_ref[...] = jnp.zeros_like(r2_ref)

    r1 = r1_ref[...]
    r2 = r2_ref[...]
    ym = _dot(om_ref[...], wm_ref[...])
    ys = _dot(os_ref[...], ws_ref[...])
    yh = _dot(oh_ref[...], wh_ref[...])
    o_ref[...] = _layer_norm(r2, g2_ref[...], b2_ref[...]).reshape(o_ref.shape)
    h1 = _layer_norm(r1, g1_ref[...], b1_ref[...])
    hb = h1.astype(BF16)
    a = _dot(hb, wg_ref[...])
    u = _dot(hb, wu_ref[...])
    mixed = _sigmoid(gm_ref[...].astype(F32)) * ym
    mixed += _sigmoid(gs_ref[...].astype(F32)) * ys
    mixed += _sigmoid(gh_ref[...].astype(F32)) * yh
    r1_ref[...] = alpha * h_ref[...] + _dot(mixed.astype(BF16), wo_ref[...])
    r2_ref[...] = alpha * h1 + _dot((a * _sigmoid(a) * u).astype(BF16), wd_ref[...])


def _back(om, oh, zall, h, wm, ws, wh, wo, g1, b1, wg, wu, wd, g2, b2, nb, alpha, final, li):
    r, d = h.shape
    bw = om.shape[1]
    dff = wg.shape[-1]
    rb = T_BLK * nb
    skip = 1 if final else 0
    n_blk = r // rb - skip
    lag = 2
    row = lambda width, j=0: pl.BlockSpec((rb, width), lambda i: (jnp.minimum(i, n_blk - 1) + skip, j))
    if final:
        out_shape = jax.ShapeDtypeStruct((nb, n_blk * T_BLK, d), F32)
        out_spec = pl.BlockSpec((nb, T_BLK, d), lambda i: (0, jnp.maximum(i - lag, 0), 0))
    else:
        out_shape = jax.ShapeDtypeStruct((r, d), F32)
        out_spec = pl.BlockSpec((rb, d), lambda i: (jnp.maximum(i - lag, 0), 0))
    return pl.pallas_call(
        functools.partial(_back_kernel, alpha=alpha),
        out_shape=out_shape,
        grid=(n_blk + lag,),
        in_specs=[row(bw), row(bw, OFF_Y // bw), row(bw),
                  row(d, OFF_G // d), row(d, OFF_G // d + 1), row(d, OFF_G // d + 2), row(d),
                  _const_spec((bw, d), li), _const_spec((bw, d), li), _const_spec((bw, d), li),
                  _const_spec((d, d), li), _const_spec((1, d), li), _const_spec((1, d), li),
                  _const_spec((d, dff), li), _const_spec((d, dff), li), _const_spec((dff, d), li),
                  _const_spec((1, d), li), _const_spec((1, d), li)],
        out_specs=out_spec,
        scratch_shapes=[pltpu.VMEM((rb, d), F32), pltpu.VMEM((rb, d), F32)],
        compiler_params=_cparams(1, "arbitrary"),
        name="merge_ffn",
    )(om, zall, oh, zall, zall, zall, h, wm, ws, wh, wo, g1, b1, wg, wu, wd, g2, b2)


def _permute_w_in(w):
    mla_in = MLA_Q_RANK + MLA_KV_RANK + MLA_ROPE
    s5_0, hg_0 = mla_in, mla_in + S5_WIDTH
    g_0 = hg_0 + 2 * HG_QK + 2 * HG_VW
    pad0 = jnp.zeros((w.shape[0], MLA_NOPE), w.dtype)
    pad1 = jnp.zeros((w.shape[0], ZKR_W - MLA_NOPE - MLA_ROPE), w.dtype)
    return jnp.concatenate([w[:, s5_0:hg_0], w[:, 0:MLA_Q_RANK + MLA_KV_RANK], w[:, hg_0:g_0], w[:, g_0:],
                            pad0, w[:, MLA_Q_RANK + MLA_KV_RANK:mla_in], pad1], axis=1).astype(BF16)


def _mla_weights(w_uq, w_ukv):
    rq, rkv = w_uq.shape[0], w_ukv.shape[0]
    zpad = HEAD_PAD - MLA_NOPE - MLA_ROPE
    wq = w_uq.reshape(rq, MLA_HEADS, MLA_NOPE + MLA_ROPE)
    q_nope, q_rope = wq[..., :MLA_NOPE], wq[..., MLA_NOPE:]
    zq = jnp.zeros((rq, MLA_HEADS, zpad), w_uq.dtype)
    wq_p = jnp.concatenate([q_nope, q_rope, zq], axis=-1).reshape(rq, -1)
    wkv = w_ukv.reshape(rkv, MLA_HEADS, MLA_NOPE + MLA_V)
    zk = jnp.zeros((rkv, MLA_HEADS, HEAD_PAD - MLA_NOPE), w_ukv.dtype)
    wk_p = jnp.concatenate([wkv[..., :MLA_NOPE], zk], axis=-1).reshape(rkv, -1)
    wv = wkv[..., MLA_NOPE:].reshape(rkv, -1)
    return [a.astype(BF16) for a in (wq_p, wk_p, wv)]


def _rope_consts():
    half = MLA_ROPE // 2
    inv = ROPE_THETA ** (-(jnp.arange(0, MLA_ROPE, 2, dtype=F32) / MLA_ROPE))
    place = np.zeros((half, HEAD_PAD), np.float32)
    place[np.arange(half), MLA_NOPE + np.arange(half)] = 1.0
    place[np.arange(half), MLA_NOPE + half + np.arange(half)] = 1.0
    base = np.zeros((1, HEAD_PAD), np.float32)
    base[0, :MLA_NOPE] = 1.0
    return inv[:, None], jnp.asarray(place, BF16), jnp.asarray(base)


def _s5_params(lam_re, lam_im, log_dt, b_re, b_im, c_re, c_im):
    lr = jnp.minimum(lam_re.astype(F32), -1e-4)
    li = lam_im.astype(F32)
    dt = jnp.exp(log_dt.astype(F32))[:, None]
    mag = jnp.exp(lr * dt)
    ab_r = mag * jnp.cos(li * dt)
    ab_i = mag * jnp.sin(li * dt)
    den = lr * lr + li * li
    nr = ab_r - 1.0
    coef_r = ((nr * lr + ab_i * li) / den)[..., None]
    coef_i = ((ab_i * lr - nr * li) / den)[..., None]
    bb_r = coef_r * b_re.astype(F32) - coef_i * b_im.astype(F32)
    bb_i = coef_r * b_im.astype(F32) + coef_i * b_re.astype(F32)
    n_slab = S5_WIDTH // S5_SLAB
    gps = S5_SLAB // S5_GROUP
    eye = jnp.eye(gps, dtype=F32)

    def in_mat(bb):
        b4 = bb.reshape(n_slab, gps, S5_STATE, S5_GROUP)
        return jnp.einsum('jgnc,gh->jgchn', b4, eye).reshape(n_slab, S5_SLAB, S5_SLAB_STATE)

    def out_mat(cc):
        c4 = cc.astype(F32).reshape(n_slab, gps, S5_GROUP, S5_STATE)
        return jnp.einsum('jgcn,gh->jgnhc', c4, eye).reshape(n_slab, S5_SLAB_STATE, S5_SLAB)

    bm = jnp.concatenate([in_mat(bb_r), in_mat(bb_i)], axis=2).astype(BF16)
    cm = jnp.concatenate([out_mat(c_re), -out_mat(c_im)], axis=1).astype(BF16)
    ar = ab_r.reshape(n_slab, 1, S5_SLAB_STATE)
    ai = ab_i.reshape(n_slab, 1, S5_SLAB_STATE)
    return bm, cm, ar, ai


def kernel(x, positions, meta_tokens, ln_in_g, ln_in_b, w_in, mla_q_norm, mla_w_uq, mla_kv_norm, mla_w_ukv,
           s5_lam_re, s5_lam_im, s5_log_dt, s5_b_re, s5_b_im, s5_c_re, s5_c_im, s5_d, s5_w_glu,
           hg_lb_logits, hg_out_norm, w_br_mla, w_br_s5, w_br_hg, w_out, ln1_g, ln1_b,
           w_ffn_gate, w_ffn_up, w_ffn_down, ln2_g, ln2_b):
    nb, s, d = x.shape
    depth = w_in.shape[0]
    nblk = (s + N_META) // T_BLK
    alpha = (2 * depth) ** 0.25
    row2 = lambda a: a.astype(F32)[None, :]
    rows3 = lambda a: a.astype(F32)[:, None, :]
    bf16 = lambda a: a.astype(BF16)

    meta_pos = jnp.broadcast_to(jnp.arange(N_META, dtype=jnp.int32)[None, :], (nb, N_META))
    pos = jnp.concatenate([meta_pos, positions.astype(jnp.int32) + N_META], axis=1)
    pos = pos.reshape(nb, nblk, T_BLK).transpose(1, 0, 2).reshape(nblk, 1, nb * T_BLK)
    rope = _rope_consts()
    p_lb = jax.nn.softmax(hg_lb_logits.astype(F32), axis=0)
    lower_bounds = jnp.cumsum(p_lb, axis=0) - p_lb[0]
    tri = jnp.asarray(np.tril(np.ones((HG_CHUNK, HG_CHUNK), np.float32)), BF16)
    lvl = jnp.asarray(_hgrn_level_matrix())
    perm = jnp.asarray(_time_major_perm(nb), BF16)

    front_p = (jax.vmap(_permute_w_in)(w_in), pos, rope, rows3(mla_q_norm), rows3(mla_kv_norm),
               *jax.vmap(_mla_weights)(mla_w_uq, mla_w_ukv), perm,
               *jax.vmap(_s5_params)(s5_lam_re, s5_lam_im, s5_log_dt, s5_b_re, s5_b_im, s5_c_re, s5_c_im),
               rows3(s5_d), bf16(s5_w_glu))
    hgrn_p = (lower_bounds[:, None, :], rows3(hg_out_norm), tri, lvl)
    back_p = (bf16(w_br_mla), bf16(w_br_s5), bf16(w_br_hg), bf16(w_out), rows3(ln1_g), rows3(ln1_b),
              bf16(w_ffn_gate), bf16(w_ffn_up), bf16(w_ffn_down), rows3(ln2_g), rows3(ln2_b))

    h = (x, meta_tokens.astype(x.dtype), row2(ln_in_g), row2(ln_in_b))
    for li in range(depth):
        outs = _front(h, *front_p, nb, li)
        if li == 0:
            h, *outs = outs
        zall, = outs
        o_mla = _attention(zall, nb, nblk)
        o_hg = _hgrn(zall, *hgrn_p, nb, nblk, li)
        h = _back(o_mla, o_hg, zall, h, *back_p, nb, alpha, final=li == depth - 1, li=li)
    return h
```

```python
import functools
import math

import jax
import jax.numpy as jnp
import numpy as np
from jax import lax
from jax.experimental import pallas as pl
from jax.experimental.pallas import tpu as pltpu

F32 = jnp.float32
BF16 = jnp.bfloat16

N_META = 16
MLA_HEADS = 8
MLA_NOPE = 64
MLA_ROPE = 32
MLA_V = 64
MLA_Q_RANK = 256
MLA_KV_RANK = 256
ROPE_THETA = 10000.0
MASK_VALUE = -1e9
LOG2_E = math.log2(math.e)
HEAD_PAD = 128
S5_WIDTH = 512
S5_GROUP = 16
S5_GROUPS = S5_WIDTH // S5_GROUP
S5_STATE = 64
S5_SLAB = 128
S5_SLAB_STATE = (S5_SLAB // S5_GROUP) * S5_STATE
HG_HEADS = 4
HG_KEY = 128
HG_VAL = 128
HG_QK = HG_HEADS * HG_KEY
HG_VW = HG_HEADS * HG_VAL
HG_F_MIN = 1e-6
HG_CHUNK = 128
HG_LEVELS = (128, 64, 32, 16)
HG_BOTTOM = 8
N_BRANCH = 3
T_BLK = 16
ATT_TQ = 256
ATT_HEADS_PER_STEP = 4
ATT_VE = MLA_V + 16
VMEM_LIMIT = 56 * 1024 * 1024

ZA_W = 1024
ZHG_W = 2048
ZG_W = 3072
ZKR_W = 128
Z_W = ZA_W + ZHG_W + ZG_W + ZKR_W
ZALL_WIDTHS = (ZHG_W, ZG_W, MLA_HEADS * HEAD_PAD, MLA_HEADS * HEAD_PAD, MLA_HEADS * MLA_V, S5_WIDTH)
ZALL_OFF = tuple(sum(ZALL_WIDTHS[:i]) for i in range(len(ZALL_WIDTHS)))
ZALL_W = sum(ZALL_WIDTHS)
OFF_HG, OFF_G, OFF_Q, OFF_K, OFF_V, OFF_Y = ZALL_OFF
WIDE_CHUNK = 512
FILL_BEFORE_MLA = 3


def _cparams(n_grid, sem="parallel"):
    return pltpu.CompilerParams(dimension_semantics=(sem,) * n_grid, vmem_limit_bytes=VMEM_LIMIT)


def _const_spec(shape, layer=None):
    nd = len(shape)
    if layer is None:
        return pl.BlockSpec(shape, lambda *_: (0,) * nd, pipeline_mode=pl.Buffered(1))
    return pl.BlockSpec((None,) + tuple(shape), lambda *_: (layer,) + (0,) * nd, pipeline_mode=pl.Buffered(1))


def _seq_spec(nblk, width, col):
    return pl.BlockSpec((nblk, None, T_BLK, width), lambda b, *g: (0, b, 0, col(b, *g)))


def _dot(a, b):
    return jnp.dot(a, b, preferred_element_type=F32)


def _dot_nt(a, b):
    return lax.dot_general(a, b, (((1,), (1,)), ((), ())), preferred_element_type=F32)


def _dot_tn(a, b):
    return lax.dot_general(a, b, (((0,), (0,)), ((), ())), preferred_element_type=F32)


def _sigmoid(x):
    return 1.0 / (1.0 + jnp.exp(-x))


def _layer_norm(x, g, b, eps=1e-5):
    mu = jnp.mean(x, axis=-1, keepdims=True)
    xc = x - mu
    var = jnp.mean(xc * xc, axis=-1, keepdims=True)
    return xc * lax.rsqrt(var + eps) * g + b


def _rms_norm(x, g, eps=1e-6):
    return x * lax.rsqrt(jnp.mean(x * x, axis=-1, keepdims=True) + eps) * g


def _load_rows(ref, blk0, n_blk, cs):
    x = ref[pl.ds(blk0, n_blk), :, cs]
    return x.reshape(n_blk * T_BLK, x.shape[-1])


def _store_rows(ref, blk0, n_blk, cs, x):
    ref[pl.ds(blk0, n_blk), :, cs] = x.reshape(n_blk, T_BLK, x.shape[-1])


def _rope_tables(pos_ref, inv_ref, place_ref, base_ref):
    ang = inv_ref[...] * pos_ref[...].astype(F32)

    def place(t):
        hi = t.astype(BF16)
        lo = (t - hi.astype(F32)).astype(BF16)
        return _dot_tn(hi, place_ref[...]) + _dot_tn(lo, place_ref[...])

    return place(jnp.cos(ang)) + base_ref[...], place(jnp.sin(ang))


def _mla_prep(cq, ckv, kr, cos, sin, qn_ref, kvn_ref, wq_ref, wk_ref, wv_ref, q_ref, k_ref, v_ref):
    scale = (MLA_NOPE + MLA_ROPE) ** -0.5 * LOG2_E
    half = MLA_ROPE // 2
    lane = lax.broadcasted_iota(jnp.int32, (1, HEAD_PAD), 1)
    sin_up = jnp.where((lane >= MLA_NOPE + half) & (lane < MLA_NOPE + MLA_ROPE), sin, 0.0)
    sin_dn = jnp.where((lane >= MLA_NOPE) & (lane < MLA_NOPE + half), -sin, 0.0)

    def rope(x, c, s_up, s_dn):
        return x * c + pltpu.roll(x, half, 1) * s_up + pltpu.roll(x, HEAD_PAD - half, 1) * s_dn

    cqn = _rms_norm(cq, qn_ref[...]).astype(BF16)
    ckvn = _rms_norm(ckv, kvn_ref[...]).astype(BF16)
    k_rope = rope(kr, cos, sin_up, sin_dn)
    cos_q, sin_up_q, sin_dn_q = cos * scale, sin_up * scale, sin_dn * scale
    for h in range(MLA_HEADS):
        cs = slice(h * HEAD_PAD, (h + 1) * HEAD_PAD)
        q_ref[:, cs] = rope(_dot(cqn, wq_ref[:, cs]), cos_q, sin_up_q, sin_dn_q).astype(BF16)
        k_ref[:, cs] = (_dot(ckvn, wk_ref[:, cs]) + k_rope).astype(BF16)
    v_ref[...] = _dot(ckvn, wv_ref[...]).astype(BF16)


def _s5_block(u, perm_ref, bm_ref, cm_ref, ar_ref, ai_ref, d_ref, wg_ref, o_ref,
              xr_ref, xi_ref, buf_ref, y_ref, nb, fillers):
    n_slab = S5_WIDTH // S5_SLAB
    ns = S5_SLAB_STATE
    u = _dot(perm_ref[...], u).astype(BF16)
    for j in range(n_slab):
        buf_ref[...] = _dot(u[:, j * S5_SLAB:(j + 1) * S5_SLAB], bm_ref[j])
        if fillers:
            fillers.pop(0)()
        ar = jnp.broadcast_to(ar_ref[j], (nb, ns))
        ai = jnp.broadcast_to(ai_ref[j], (nb, ns))
        xr = xr_ref[j]
        xi = xi_ref[j]
        for t in range(T_BLK):
            rows = slice(t * nb, (t + 1) * nb)
            nr = ar * xr - ai * xi + buf_ref[rows, 0:ns]
            ni = ar * xi + ai * xr + buf_ref[rows, ns:2 * ns]
            buf_ref[rows, 0:ns] = nr
            buf_ref[rows, ns:2 * ns] = ni
            xr, xi = nr, ni
        xr_ref[j] = xr
        xi_ref[j] = xi
        y_ref[:, j * S5_SLAB:(j + 1) * S5_SLAB] = _dot(buf_ref[...].astype(BF16), cm_ref[j])

    while fillers:
        fillers.pop(0)()
    y = y_ref[...] + d_ref[...] * u.astype(F32)
    y = 0.5 * y * (1.0 + jnp.tanh(math.sqrt(2.0 / math.pi) * (y + 0.044715 * (y * y * y))))
    gate = _sigmoid(_dot(y.astype(BF16), wg_ref[...]))
    out = (y * gate).astype(BF16)
    o_ref[...] = _dot_tn(perm_ref[...], out).astype(BF16)


def _front_kernel(*refs, nb, ln_in):
    if ln_in:
        x_ref, meta_ref, lng_ref, lnb_ref, *refs = refs
    else:
        h_ref, *refs = refs
    (w_ref, pos_ref, inv_ref, place_ref, base_ref, qn_ref, kvn_ref, wq_ref, wk_ref, wv_ref,
     perm_ref, bm_ref, cm_ref, ar_ref, ai_ref, d_ref, wglu_ref, *refs) = refs
    if ln_in:
        hout_ref, *refs = refs
    zall_ref, xr_ref, xi_ref, buf_ref, y_ref = refs
    zhg_ref, zg_ref, q_ref, k_ref, v_ref, ys5_ref = (
        zall_ref.at[:, o:o + w] for o, w in zip(ZALL_OFF, ZALL_WIDTHS))

    @pl.when(pl.program_id(0) == 0)
    def _():
        xr_ref[...] = jnp.zeros_like(xr_ref)
        xi_ref[...] = jnp.zeros_like(xi_ref)

    cos, sin = _rope_tables(pos_ref, inv_ref, place_ref, base_ref)
    if ln_in:
        hx = _layer_norm(x_ref[...], lng_ref[...], lnb_ref[...])
        hm = _layer_norm(meta_ref[...], lng_ref[...], lnb_ref[...])
        h = jnp.where(pl.program_id(0) == 0, jnp.broadcast_to(hm[None], hx.shape), hx)
        h = h.reshape(hout_ref.shape)
        hout_ref[...] = h
    else:
        h = h_ref[...]
    x = h.astype(BF16)
    za = _dot(x, w_ref[:, 0:ZA_W])
    zkr = _dot(x, w_ref[:, ZA_W + ZHG_W + ZG_W:Z_W])

    def wide_chunk(o_ref, c, off):
        def run():
            o_ref[:, c:c + WIDE_CHUNK] = _dot(x, w_ref[:, off + c:off + c + WIDE_CHUNK]).astype(BF16)
        return run

    fillers = [wide_chunk(zhg_ref, c, ZA_W) for c in range(0, ZHG_W, WIDE_CHUNK)]
    fillers += [wide_chunk(zg_ref, c, ZA_W + ZHG_W) for c in range(0, ZG_W, WIDE_CHUNK)]
    for _ in range(FILL_BEFORE_MLA):
        fillers.pop(0)()
    c0 = S5_WIDTH
    _mla_prep(za[:, c0:c0 + MLA_Q_RANK], za[:, c0 + MLA_Q_RANK:c0 + MLA_Q_RANK + MLA_KV_RANK], zkr,
              cos, sin, qn_ref, kvn_ref, wq_ref, wk_ref, wv_ref, q_ref, k_ref, v_ref)
    _s5_block(za[:, 0:S5_WIDTH].astype(BF16), perm_ref, bm_ref, cm_ref, ar_ref, ai_ref, d_ref, wglu_ref, ys5_ref,
              xr_ref, xi_ref, buf_ref, y_ref, nb, fillers)


def _front(src, w, pos, rope, qn, kvn, wq, wk, wv, perm, bm, cm, ar, ai, dskip, wglu, nb, li):
    ln_in = isinstance(src, tuple)
    rb = T_BLK * nb
    if ln_in:
        x, meta, lng, lnb = src
        d = x.shape[-1]
        r = (x.shape[1] + N_META) * nb
        src_specs = [pl.BlockSpec((nb, T_BLK, d), lambda i: (0, jnp.maximum(i - 1, 0), 0)),
                     _const_spec((N_META, d)), _const_spec((1, d)), _const_spec((1, d))]
    else:
        src = (src,)
        r, d = src[0].shape
        src_specs = [pl.BlockSpec((rb, d), lambda i: (i, 0))]
    hw = MLA_HEADS * HEAD_PAD
    vw = MLA_HEADS * MLA_V
    n_slab = S5_WIDTH // S5_SLAB
    row = lambda width: pl.BlockSpec((rb, width), lambda i: (i, 0))
    out_shape = [jax.ShapeDtypeStruct((r, ZALL_W), BF16)]
    out_specs = [row(ZALL_W)]
    if ln_in:
        out_shape.insert(0, jax.ShapeDtypeStruct((r, d), F32))
        out_specs.insert(0, row(d))
    return pl.pallas_call(
        functools.partial(_front_kernel, nb=nb, ln_in=ln_in),
        out_shape=out_shape,
        grid=(r // rb,),
        in_specs=src_specs + [
            _const_spec((d, Z_W), li), pl.BlockSpec((None, 1, rb), lambda i: (i, 0, 0)),
            _const_spec((MLA_ROPE // 2, 1)), _const_spec((MLA_ROPE // 2, HEAD_PAD)), _const_spec((1, HEAD_PAD)),
            _const_spec((1, MLA_Q_RANK), li), _const_spec((1, MLA_KV_RANK), li),
            _const_spec((MLA_Q_RANK, hw), li), _const_spec((MLA_KV_RANK, hw), li), _const_spec((MLA_KV_RANK, vw), li),
            _const_spec((rb, rb)),
            _const_spec((n_slab, S5_SLAB, 2 * S5_SLAB_STATE), li),
            _const_spec((n_slab, 2 * S5_SLAB_STATE, S5_SLAB), li),
            _const_spec((n_slab, 1, S5_SLAB_STATE), li),
            _const_spec((n_slab, 1, S5_SLAB_STATE), li),
            _const_spec((1, S5_WIDTH), li),
            _const_spec((S5_WIDTH, S5_WIDTH), li),
        ],
        out_specs=out_specs,
        scratch_shapes=[
            pltpu.VMEM((n_slab, nb, S5_SLAB_STATE), F32),
            pltpu.VMEM((n_slab, nb, S5_SLAB_STATE), F32),
            pltpu.VMEM((rb, 2 * S5_SLAB_STATE), F32),
            pltpu.VMEM((rb, S5_WIDTH), F32),
        ],
        compiler_params=_cparams(1, "arbitrary"),
        name="front",
    )(*src, w, pos, *rope, qn, kvn, wq, wk, wv, perm, bm, cm, ar, ai, dskip, wglu)


def _attn_kernel(tab_ref, q_ref, k_ref, v_ref, o_ref, vt_ref, qt_ref, m_ref, acc_ref, *bufs, n_heads, tq, nq):
    xs_ref, xm_ref, ps_ref, pm_ref, a_ref = (bufs[2 * i:2 * i + 2] for i in range(5))
    bpq = tq // T_BLK
    vw = n_heads * MLA_V

    def values_t(vb):
        vt = vb.astype(F32).T
        ext = jnp.where(lax.broadcasted_iota(jnp.int32, (ATT_VE - MLA_V, vt.shape[1]), 0) == 0, 1.0, 0.0)
        parts = []
        for h in range(n_heads):
            parts += [vt[h * MLA_V:(h + 1) * MLA_V, :], ext]
        return jnp.concatenate(parts, axis=0).astype(BF16)

    for j in range(nq):
        vt_ref[j] = values_t(_load_rows(v_ref, 1 + j * bpq, bpq, slice(0, vw)))
    v0t = values_t(v_ref[0])

    for i in range(nq):
        qb = _load_rows(q_ref, 1 + i * bpq, bpq, slice(0, n_heads * HEAD_PAD))
        qt_ref[i] = qb.astype(F32).T.astype(BF16)

    def causal(st):
        r = lax.broadcasted_iota(jnp.int32, st.shape, 0)
        c = lax.broadcasted_iota(jnp.int32, st.shape, 1)
        return jnp.where(r <= c, st, MASK_VALUE * LOG2_E)

    heads = range(n_heads)
    qs = [slice(h * HEAD_PAD, (h + 1) * HEAD_PAD) for h in heads]
    vs = [slice(h * ATT_VE, (h + 1) * ATT_VE) for h in heads]
    k0 = [k_ref[0, :, c] for c in qs]

    def normalised(acc):
        return acc[0:MLA_V, :] / acc[MLA_V:MLA_V + 1, :]

    outs = []
    for n in heads:
        st = causal(_dot_nt(k0[n], q_ref[0, :, qs[n]]))
        p = jnp.exp2(st - jnp.max(st, axis=0, keepdims=True))
        outs.append(normalised(_dot(v0t[vs[n], :], p.astype(BF16))))
    o_ref[0, :, :] = jnp.concatenate(outs, axis=0).T.astype(BF16)

    def key_tile(j, n):
        return _load_rows(k_ref, 1 + j * bpq, bpq, qs[n])

    def diag_scores(i, s):
        for n in heads:
            xs_ref[s][n] = _dot(key_tile(i, n), qt_ref[i, qs[n], :])
            xm_ref[s][n] = _dot(k0[n], qt_ref[i, qs[n], :])

    def diag_stats(i, s):
        for n in heads:
            st, sm = causal(xs_ref[s][n]), xm_ref[s][n]
            m = jnp.maximum(jnp.max(st, axis=0, keepdims=True), jnp.max(sm, axis=0, keepdims=True))
            p = jnp.exp2(st - m)
            pm = jnp.exp2(sm - m)
            m_ref[i, n] = m
            ps_ref[s][n] = p.astype(BF16)
            pm_ref[s][n] = pm.astype(BF16)

    def diag_values(i, s):
        for n in heads:
            acc_ref[i, vs[n], :] = _dot(vt_ref[i, vs[n], :], ps_ref[s][n]) + _dot(v0t[vs[n], :], pm_ref[s][n])

    _pipeline3(nq, diag_scores, diag_stats, diag_values)

    def off_scores(f, s):
        i, j = tab_ref[0, f], tab_ref[1, f]
        for n in heads:
            xs_ref[s][n] = _dot(key_tile(j, n), qt_ref[i, qs[n], :])

    def off_stats(f, s):
        i = tab_ref[0, f]
        for n in heads:
            x = xs_ref[s][n]
            m_old = m_ref[i, n]
            m = jnp.maximum(m_old, jnp.max(x, axis=0, keepdims=True))
            a = jnp.exp2(m_old - m)
            p = jnp.exp2(x - m)
            m_ref[i, n] = m
            a_ref[s][n] = a
            ps_ref[s][n] = p.astype(BF16)

    def off_values(f, s):
        i, j = tab_ref[0, f], tab_ref[1, f]
        for n in heads:
            acc_ref[i, vs[n], :] = a_ref[s][n] * acc_ref[i, vs[n], :] + _dot(vt_ref[j, vs[n], :], ps_ref[s][n])

    _pipeline3(nq * (nq - 1) // 2, off_scores, off_stats, off_values)

    for i in range(nq):
        o = jnp.concatenate([normalised(acc_ref[i, vs[n], :]) for n in heads], axis=0)
        _store_rows(o_ref, 1 + i * bpq, bpq, slice(0, vw), o.T.astype(BF16))


def _pipeline3(n, stage1, stage2, stage3):
    if n == 0:
        return
    stage1(0, 0)
    if n > 1:
        stage1(1, 1)
    stage2(0, 0)

    def step(t, par):
        stage1(t + 2, par)
        stage2(t + 1, 1 - par)
        stage3(t, par)

    steady = max(n - 2, 0)
    if steady >= 2:
        def body(u, carry):
            step(2 * u, 0)
            step(2 * u + 1, 1)
            return carry

        lax.fori_loop(0, steady // 2, body, 0)
    if steady % 2:
        step(steady - 1, (steady - 1) % 2)
    if n > 1:
        stage3(n - 2, (n - 2) % 2)
        stage2(n - 1, (n - 1) % 2)
    stage3(n - 1, (n - 1) % 2)


def _off_diagonal_order(nq):
    left = [(i, j) for i in range(nq) for j in range(i)]
    order = []
    while left:
        count = {}
        for i, _ in left:
            count[i] = count.get(i, 0) + 1
        ok = [p for p in left if not order or p[0] != order[-1][0]] or left
        pick = max(ok, key=lambda p: (count[p[0]], -p[1]))
        order.append(pick)
        left.remove(pick)
    return order or [(0, 0)]


def _attention(zall, nb, nblk):
    z4 = zall.reshape(nblk, nb, T_BLK, ZALL_W)
    hps = ATT_HEADS_PER_STEP
    n_hp = MLA_HEADS // hps
    qw = hps * HEAD_PAD
    vw = hps * MLA_V
    s = (nblk - 1) * T_BLK
    tq = min(ATT_TQ, s)
    nq = s // tq
    col = lambda b, p: p
    tab = jnp.asarray(np.array(_off_diagonal_order(nq), np.int32).T)
    o = pl.pallas_call(
        functools.partial(_attn_kernel, n_heads=hps, tq=tq, nq=nq),
        out_shape=jax.ShapeDtypeStruct((nblk, nb, T_BLK, MLA_HEADS * MLA_V), BF16),
        grid=(nb, n_hp),
        in_specs=[pl.BlockSpec(memory_space=pltpu.SMEM),
                  _seq_spec(nblk, qw, lambda b, p: OFF_Q // qw + p),
                  _seq_spec(nblk, qw, lambda b, p: OFF_K // qw + p),
                  _seq_spec(nblk, vw, lambda b, p: OFF_V // vw + p)],
        out_specs=_seq_spec(nblk, vw, col),
        scratch_shapes=[pltpu.VMEM((nq, hps * ATT_VE, tq), BF16), pltpu.VMEM((nq, qw, tq), BF16),
                        pltpu.VMEM((nq, hps, 1, tq), F32), pltpu.VMEM((nq, hps * ATT_VE, tq), F32)]
        + 2 * [pltpu.VMEM((hps, tq, tq), F32)] + 2 * [pltpu.VMEM((hps, N_META, tq), F32)]
        + 2 * [pltpu.VMEM((hps, tq, tq), BF16)] + 2 * [pltpu.VMEM((hps, N_META, tq), BF16)]
        + 2 * [pltpu.VMEM((hps, 1, tq), F32)],
        compiler_params=_cparams(2),
        name="mla_attn",
    )(tab, z4, z4, z4)
    return o.reshape(nblk * nb * T_BLK, MLA_HEADS * MLA_V)


def _time_major_perm(nb):
    p = np.zeros((T_BLK * nb, T_BLK * nb), np.float32)
    for b in range(nb):
        for t in range(T_BLK):
            p[t * nb + b, b * T_BLK + t] = 1.0
    return p


def _block_row_bcast(x, m, row):
    t, c = x.shape
    if m == t:
        return jnp.broadcast_to(x[row:row + 1, :], x.shape)
    x3 = x.reshape(t // m, m, c)
    return jnp.broadcast_to(x3[:, row:row + 1, :], x3.shape).reshape(t, c)


def _hgrn_gates(blk0, n_blk, f_ref, lb_ref):
    lb = lb_ref[...]
    zf = _load_rows(f_ref, blk0, n_blk, slice(0, HG_QK)).astype(F32)
    e = jnp.exp(-jnp.abs(zf))
    rcp = 1.0 / (1.0 + e)
    pos = zf >= 0.0
    sig_p = jnp.where(pos, rcp, e * rcp)
    sig_n = jnp.where(pos, e * rcp, rcp)
    f = lb + (1.0 - lb) * sig_p
    log_f = jnp.log2(jnp.maximum(f, HG_F_MIN))
    k = (1.0 - lb) * sig_n
    hi = log_f.astype(BF16)
    lo = (log_f - hi.astype(F32)).astype(BF16)
    return k, hi, lo


def _hgrn_mix(blk0, n_blk, gates, q_ref, v_ref, tri_ref, lvl_ref, st_ref):
    t = n_blk * T_BLK
    cs = slice(0, HG_QK)
    heads = [slice(h * HG_KEY, (h + 1) * HG_KEY) for h in range(HG_HEADS)]
    k, hi, lo = gates
    q = _load_rows(q_ref, blk0, n_blk, cs).astype(F32)
    v = _load_rows(v_ref, blk0, n_blk, cs)
    tri = tri_ref[0:t, 0:t]
    cum = _dot(tri, hi) + _dot(tri, lo)
    lvl = lvl_ref[0:t, 0:t]
    r_idx = lax.broadcasted_iota(jnp.int32, (t, HG_QK), 0)
    c8 = _block_row_bcast(cum, HG_BOTTOM, HG_BOTTOM // 2 - 1)
    qe = (q * jnp.exp2(cum - c8)).astype(BF16)
    ke = (k * jnp.exp2(c8 - cum)).astype(BF16)
    n_lvl = len(HG_LEVELS)
    scores = [jnp.where(lvl == n_lvl, _dot_nt(qe[:, hs], ke[:, hs]), 0.0) for hs in heads]
    for li, m in enumerate(HG_LEVELS):
        if m > t:
            continue
        half = m // 2
        cmid = _block_row_bcast(cum, m, half - 1)
        upper = (r_idx & (m - 1)) >= half
        ex = jnp.exp2(jnp.where(upper, cum - cmid, cmid - cum))
        qe = jnp.where(upper, q * ex, 0.0).astype(BF16)
        ke = jnp.where(upper, 0.0, k * ex).astype(BF16)
        scores = [jnp.where(lvl == li, _dot_nt(qe[:, hs], ke[:, hs]), s) for s, hs in zip(scores, heads)]
    qd = (q * jnp.exp2(cum)).astype(BF16)
    last = cum[t - 1:t, :]
    kd = (k * jnp.exp2(last - cum)).astype(BF16)
    dec = jnp.exp2(last)
    outs = []
    for h, hs in enumerate(heads):
        st = st_ref[h]
        outs.append(_dot(scores[h].astype(BF16), v[:, hs]) + _dot_nt(qd[:, hs], st.astype(BF16)))
        st_ref[h] = st * dec[:, hs] + _dot_tn(v[:, hs], kd[:, hs])
    return tuple(outs)


def _hgrn_out(blk0, n_blk, outs, g_ref, on_ref, o_ref):
    cs = slice(0, HG_VW)
    g = _load_rows(g_ref, blk0, n_blk, cs).astype(F32)
    o = jnp.concatenate([o * lax.rsqrt(jnp.mean(o * o, axis=-1, keepdims=True) + 1e-6) for o in outs], axis=1)
    _store_rows(o_ref, blk0, n_blk, cs, (o * on_ref[...] * (g * _sigmoid(g))).astype(BF16))


def _hgrn_kernel(q_ref, f_ref, v_ref, g_ref, lb_ref, on_ref, tri_ref, lvl_ref, o_ref, st_ref, *, n_chunks):
    bpc = HG_CHUNK // T_BLK
    gates = functools.partial(_hgrn_gates, f_ref=f_ref, lb_ref=lb_ref)
    mix = functools.partial(_hgrn_mix, q_ref=q_ref, v_ref=v_ref, tri_ref=tri_ref, lvl_ref=lvl_ref, st_ref=st_ref)
    out = functools.partial(_hgrn_out, g_ref=g_ref, on_ref=on_ref, o_ref=o_ref)
    st_ref[...] = jnp.zeros_like(st_ref)
    out(0, 1, mix(0, 1, gates(0, 1)))

    def body(c, carry):
        blk0 = 1 + c * bpc
        out(blk0, bpc, mix(blk0, bpc, gates(blk0, bpc)))
        return carry

    lax.fori_loop(0, n_chunks, body, 0)


def _hgrn(zall, lb, onorm, tri, lvl, nb, nblk, li):
    z4 = zall.reshape(nblk, nb, T_BLK, ZALL_W)
    col = lambda j: _seq_spec(nblk, HG_QK, lambda b: OFF_HG // HG_QK + j)
    o = pl.pallas_call(
        functools.partial(_hgrn_kernel, n_chunks=(nblk - 1) * T_BLK // HG_CHUNK),
        out_shape=jax.ShapeDtypeStruct((nblk, nb, T_BLK, HG_VW), BF16),
        grid=(nb,),
        in_specs=[col(0), col(1), col(2), col(3),
                  _const_spec((1, HG_QK), li), _const_spec((1, HG_VW), li),
                  _const_spec((HG_CHUNK, HG_CHUNK)), _const_spec((HG_CHUNK, HG_CHUNK))],
        out_specs=_seq_spec(nblk, HG_VW, lambda b: 0),
        scratch_shapes=[pltpu.VMEM((HG_HEADS, HG_VAL, HG_KEY), F32)],
        compiler_params=_cparams(1),
        name="hgrn2",
    )(z4, z4, z4, z4, lb, onorm, tri, lvl)
    return o.reshape(nblk * nb * T_BLK, HG_VW)


def _hgrn_level_matrix():
    r = np.arange(HG_CHUNK)[:, None]
    c = np.arange(HG_CHUNK)[None, :]
    lvl = np.zeros((HG_CHUNK, HG_CHUNK), np.int32)
    for li, m in enumerate(HG_LEVELS):
        lvl = np.where(r // m == c // m, li, lvl)
    lvl = np.where(r // HG_BOTTOM == c // HG_BOTTOM, len(HG_LEVELS), lvl)
    return np.where(c <= r, lvl, -1).astype(np.int32)


def _back_kernel(om_ref, os_ref, oh_ref, gm_ref, gs_ref, gh_ref, h_ref,
                 wm_ref, ws_ref, wh_ref, wo_ref, g1_ref, b1_ref,
                 wg_ref, wu_ref, wd_ref, g2_ref, b2_ref, o_ref, r1_ref, r2_ref, *, alpha):
    @pl.when(pl.program_id(0) == 0)
    def _():
        r1_ref[...] = jnp.zeros_like(r1_ref)
        r2_ref[...] = jnp.zeros_like(r2_ref)

    r1 = r1_ref[...]
    r2 = r2_ref[...]
    ym = _dot(om_ref[...], wm_ref[...])
    ys = _dot(os_ref[...], ws_ref[...])
    yh = _dot(oh_ref[...], wh_ref[...])
    o_ref[...] = _layer_norm(r2, g2_ref[...], b2_ref[...]).reshape(o_ref.shape)
    h1 = _layer_norm(r1, g1_ref[...], b1_ref[...])
    hb = h1.astype(BF16)
    a = _dot(hb, wg_ref[...])
    u = _dot(hb, wu_ref[...])
    mixed = _sigmoid(gm_ref[...].astype(F32)) * ym
    mixed += _sigmoid(gs_ref[...].astype(F32)) * ys
    mixed += _sigmoid(gh_ref[...].astype(F32)) * yh
    r1_ref[...] = alpha * h_ref[...] + _dot(mixed.astype(BF16), wo_ref[...])
    r2_ref[...] = alpha * h1 + _dot((a * _sigmoid(a) * u).astype(BF16), wd_ref[...])


def _back(om, oh, zall, h, wm, ws, wh, wo, g1, b1, wg, wu, wd, g2, b2, nb, alpha, final, li):
    r, d = h.shape
    bw = om.shape[1]
    dff = wg.shape[-1]
    rb = T_BLK * nb
    skip = 1 if final else 0
    n_blk = r // rb - skip
    lag = 2
    row = lambda width, j=0: pl.BlockSpec((rb, width), lambda i: (jnp.minimum(i, n_blk - 1) + skip, j))
    if final:
        out_shape = jax.ShapeDtypeStruct((nb, n_blk * T_BLK, d), F32)
        out_spec = pl.BlockSpec((nb, T_BLK, d), lambda i: (0, jnp.maximum(i - lag, 0), 0))
    else:
        out_shape = jax.ShapeDtypeStruct((r, d), F32)
        out_spec = pl.BlockSpec((rb, d), lambda i: (jnp.maximum(i - lag, 0), 0))
    return pl.pallas_call(
        functools.partial(_back_kernel, alpha=alpha),
        out_shape=out_shape,
        grid=(n_blk + lag,),
        in_specs=[row(bw), row(bw, OFF_Y // bw), row(bw),
                  row(d, OFF_G // d), row(d, OFF_G // d + 1), row(d, OFF_G // d + 2), row(d),
                  _const_spec((bw, d), li), _const_spec((bw, d), li), _const_spec((bw, d), li),
                  _const_spec((d, d), li), _const_spec((1, d), li), _const_spec((1, d), li),
                  _const_spec((d, dff), li), _const_spec((d, dff), li), _const_spec((dff, d), li),
                  _const_spec((1, d), li), _const_spec((1, d), li)],
        out_specs=out_spec,
        scratch_shapes=[pltpu.VMEM((rb, d), F32), pltpu.VMEM((rb, d), F32)],
        compiler_params=_cparams(1, "arbitrary"),
        name="merge_ffn",
    )(om, zall, oh, zall, zall, zall, h, wm, ws, wh, wo, g1, b1, wg, wu, wd, g2, b2)


def _permute_w_in(w):
    mla_in = MLA_Q_RANK + MLA_KV_RANK + MLA_ROPE
    s5_0, hg_0 = mla_in, mla_in + S5_WIDTH
    g_0 = hg_0 + 2 * HG_QK + 2 * HG_VW
    pad0 = jnp.zeros((w.shape[0], MLA_NOPE), w.dtype)
    pad1 = jnp.zeros((w.shape[0], ZKR_W - MLA_NOPE - MLA_ROPE), w.dtype)
    return jnp.concatenate([w[:, s5_0:hg_0], w[:, 0:MLA_Q_RANK + MLA_KV_RANK], w[:, hg_0:g_0], w[:, g_0:],
                            pad0, w[:, MLA_Q_RANK + MLA_KV_RANK:mla_in], pad1], axis=1).astype(BF16)


def _mla_weights(w_uq, w_ukv):
    rq, rkv = w_uq.shape[0], w_ukv.shape[0]
    zpad = HEAD_PAD - MLA_NOPE - MLA_ROPE
    wq = w_uq.reshape(rq, MLA_HEADS, MLA_NOPE + MLA_ROPE)
    q_nope, q_rope = wq[..., :MLA_NOPE], wq[..., MLA_NOPE:]
    zq = jnp.zeros((rq, MLA_HEADS, zpad), w_uq.dtype)
    wq_p = jnp.concatenate([q_nope, q_rope, zq], axis=-1).reshape(rq, -1)
    wkv = w_ukv.reshape(rkv, MLA_HEADS, MLA_NOPE + MLA_V)
    zk = jnp.zeros((rkv, MLA_HEADS, HEAD_PAD - MLA_NOPE), w_ukv.dtype)
    wk_p = jnp.concatenate([wkv[..., :MLA_NOPE], zk], axis=-1).reshape(rkv, -1)
    wv = wkv[..., MLA_NOPE:].reshape(rkv, -1)
    return [a.astype(BF16) for a in (wq_p, wk_p, wv)]


def _rope_consts():
    half = MLA_ROPE // 2
    inv = ROPE_THETA ** (-(jnp.arange(0, MLA_ROPE, 2, dtype=F32) / MLA_ROPE))
    place = np.zeros((half, HEAD_PAD), np.float32)
    place[np.arange(half), MLA_NOPE + np.arange(half)] = 1.0
    place[np.arange(half), MLA_NOPE + half + np.arange(half)] = 1.0
    base = np.zeros((1, HEAD_PAD), np.float32)
    base[0, :MLA_NOPE] = 1.0
    return inv[:, None], jnp.asarray(place, BF16), jnp.asarray(base)


def _s5_params(lam_re, lam_im, log_dt, b_re, b_im, c_re, c_im):
    lr = jnp.minimum(lam_re.astype(F32), -1e-4)
    li = lam_im.astype(F32)
    dt = jnp.exp(log_dt.astype(F32))[:, None]
    mag = jnp.exp(lr * dt)
    ab_r = mag * jnp.cos(li * dt)
    ab_i = mag * jnp.sin(li * dt)
    den = lr * lr + li * li
    nr = ab_r - 1.0
    coef_r = ((nr * lr + ab_i * li) / den)[..., None]
    coef_i = ((ab_i * lr - nr * li) / den)[..., None]
    bb_r = coef_r * b_re.astype(F32) - coef_i * b_im.astype(F32)
    bb_i = coef_r * b_im.astype(F32) + coef_i * b_re.astype(F32)
    n_slab = S5_WIDTH // S5_SLAB
    gps = S5_SLAB // S5_GROUP
    eye = jnp.eye(gps, dtype=F32)

    def in_mat(bb):
        b4 = bb.reshape(n_slab, gps, S5_STATE, S5_GROUP)
        return jnp.einsum('jgnc,gh->jgchn', b4, eye).reshape(n_slab, S5_SLAB, S5_SLAB_STATE)

    def out_mat(cc):
        c4 = cc.astype(F32).reshape(n_slab, gps, S5_GROUP, S5_STATE)
        return jnp.einsum('jgcn,gh->jgnhc', c4, eye).reshape(n_slab, S5_SLAB_STATE, S5_SLAB)

    bm = jnp.concatenate([in_mat(bb_r), in_mat(bb_i)], axis=2).astype(BF16)
    cm = jnp.concatenate([out_mat(c_re), -out_mat(c_im)], axis=1).astype(BF16)
    ar = ab_r.reshape(n_slab, 1, S5_SLAB_STATE)
    ai = ab_i.reshape(n_slab, 1, S5_SLAB_STATE)
    return bm, cm, ar, ai


def kernel(x, positions, meta_tokens, ln_in_g, ln_in_b, w_in, mla_q_norm, mla_w_uq, mla_kv_norm, mla_w_ukv,
           s5_lam_re, s5_lam_im, s5_log_dt, s5_b_re, s5_b_im, s5_c_re, s5_c_im, s5_d, s5_w_glu,
           hg_lb_logits, hg_out_norm, w_br_mla, w_br_s5, w_br_hg, w_out, ln1_g, ln1_b,
           w_ffn_gate, w_ffn_up, w_ffn_down, ln2_g, ln2_b):
    nb, s, d = x.shape
    depth = w_in.shape[0]
    nblk = (s + N_META) // T_BLK
    alpha = (2 * depth) ** 0.25
    row2 = lambda a: a.astype(F32)[None, :]
    rows3 = lambda a: a.astype(F32)[:, None, :]
    bf16 = lambda a: a.astype(BF16)

    meta_pos = jnp.broadcast_to(jnp.arange(N_META, dtype=jnp.int32)[None, :], (nb, N_META))
    pos = jnp.concatenate([meta_pos, positions.astype(jnp.int32) + N_META], axis=1)
    pos = pos.reshape(nb, nblk, T_BLK).transpose(1, 0, 2).reshape(nblk, 1, nb * T_BLK)
    rope = _rope_consts()
    p_lb = jax.nn.softmax(hg_lb_logits.astype(F32), axis=0)
    lower_bounds = jnp.cumsum(p_lb, axis=0) - p_lb[0]
    tri = jnp.asarray(np.tril(np.ones((HG_CHUNK, HG_CHUNK), np.float32)), BF16)
    lvl = jnp.asarray(_hgrn_level_matrix())
    perm = jnp.asarray(_time_major_perm(nb), BF16)

    front_p = (jax.vmap(_permute_w_in)(w_in), pos, rope, rows3(mla_q_norm), rows3(mla_kv_norm),
               *jax.vmap(_mla_weights)(mla_w_uq, mla_w_ukv), perm,
               *jax.vmap(_s5_params)(s5_lam_re, s5_lam_im, s5_log_dt, s5_b_re, s5_b_im, s5_c_re, s5_c_im),
               rows3(s5_d), bf16(s5_w_glu))
    hgrn_p = (lower_bounds[:, None, :], rows3(hg_out_norm), tri, lvl)
    back_p = (bf16(w_br_mla), bf16(w_br_s5), bf16(w_br_hg), bf16(w_out), rows3(ln1_g), rows3(ln1_b),
              bf16(w_ffn_gate), bf16(w_ffn_up), bf16(w_ffn_down), rows3(ln2_g), rows3(ln2_b))

    h = (x, meta_tokens.astype(x.dtype), row2(ln_in_g), row2(ln_in_b))
    for li in range(depth):
        outs = _front(h, *front_p, nb, li)
        if li == 0:
            h, *outs = outs
        zall, = outs
        o_mla = _attention(zall, nb, nblk)
        o_hg = _hgrn(zall, *hgrn_p, nb, nblk, li)
        h = _back(o_mla, o_hg, zall, h, *back_p, nb, alpha, final=li == depth - 1, li=li)
    return h
```

```python
import functools
import math

import jax
import jax.numpy as jnp
import numpy as np
from jax import lax
from jax.experimental import pallas as pl
from jax.experimental.pallas import tpu as pltpu

F32 = jnp.float32
BF16 = jnp.bfloat16

N_META = 16
MLA_HEADS = 8
MLA_NOPE = 64
MLA_ROPE = 32
MLA_V = 64
MLA_Q_RANK = 256
MLA_KV_RANK = 256
ROPE_THETA = 10000.0
MASK_VALUE = -1e9
LOG2_E = math.log2(math.e)
HEAD_PAD = 128
S5_WIDTH = 512
S5_GROUP = 16
S5_GROUPS = S5_WIDTH // S5_GROUP
S5_STATE = 64
S5_SLAB = 128
S5_SLAB_STATE = (S5_SLAB // S5_GROUP) * S5_STATE
HG_HEADS = 4
HG_KEY = 128
HG_VAL = 128
HG_QK = HG_HEADS * HG_KEY
HG_VW = HG_HEADS * HG_VAL
HG_F_MIN = 1e-6
HG_CHUNK = 128
HG_LEVELS = (128, 64, 32, 16)
HG_BOTTOM = 8
N_BRANCH = 3
T_BLK = 16
ATT_TQ = 256
ATT_HEADS_PER_STEP = 4
ATT_VE = MLA_V + 16
VMEM_LIMIT = 56 * 1024 * 1024

ZA_W = 1024
ZHG_W = 2048
ZG_W = 3072
ZKR_W = 128
Z_W = ZA_W + ZHG_W + ZG_W + ZKR_W
ZALL_WIDTHS = (ZHG_W, ZG_W, MLA_HEADS * HEAD_PAD, MLA_HEADS * HEAD_PAD, MLA_HEADS * MLA_V, S5_WIDTH)
ZALL_OFF = tuple(sum(ZALL_WIDTHS[:i]) for i in range(len(ZALL_WIDTHS)))
ZALL_W = sum(ZALL_WIDTHS)
OFF_HG, OFF_G, OFF_Q, OFF_K, OFF_V, OFF_Y = ZALL_OFF
WIDE_CHUNK = 512
FILL_BEFORE_MLA = 3


def _cparams(n_grid, sem="parallel"):
    return pltpu.CompilerParams(dimension_semantics=(sem,) * n_grid, vmem_limit_bytes=VMEM_LIMIT)


def _const_spec(shape, layer=None):
    nd = len(shape)
    if layer is None:
        return pl.BlockSpec(shape, lambda *_: (0,) * nd, pipeline_mode=pl.Buffered(1))
    return pl.BlockSpec((None,) + tuple(shape), lambda *_: (layer,) + (0,) * nd, pipeline_mode=pl.Buffered(1))


def _seq_spec(nblk, width, col):
    return pl.BlockSpec((nblk, None, T_BLK, width), lambda b, *g: (0, b, 0, col(b, *g)))


def _dot(a, b):
    return jnp.dot(a, b, preferred_element_type=F32)


def _dot_nt(a, b):
    return lax.dot_general(a, b, (((1,), (1,)), ((), ())), preferred_element_type=F32)


def _dot_tn(a, b):
    return lax.dot_general(a, b, (((0,), (0,)), ((), ())), preferred_element_type=F32)


def _sigmoid(x):
    return 1.0 / (1.0 + jnp.exp(-x))


def _layer_norm(x, g, b, eps=1e-5):
    mu = jnp.mean(x, axis=-1, keepdims=True)
    xc = x - mu
    var = jnp.mean(xc * xc, axis=-1, keepdims=True)
    return xc * lax.rsqrt(var + eps) * g + b


def _rms_norm(x, g, eps=1e-6):
    return x * lax.rsqrt(jnp.mean(x * x, axis=-1, keepdims=True) + eps) * g


def _load_rows(ref, blk0, n_blk, cs):
    x = ref[pl.ds(blk0, n_blk), :, cs]
    return x.reshape(n_blk * T_BLK, x.shape[-1])


def _store_rows(ref, blk0, n_blk, cs, x):
    ref[pl.ds(blk0, n_blk), :, cs] = x.reshape(n_blk, T_BLK, x.shape[-1])


def _rope_tables(pos_ref, inv_ref, place_ref, base_ref):
    ang = inv_ref[...] * pos_ref[...].astype(F32)

    def place(t):
        hi = t.astype(BF16)
        lo = (t - hi.astype(F32)).astype(BF16)
        return _dot_tn(hi, place_ref[...]) + _dot_tn(lo, place_ref[...])

    return place(jnp.cos(ang)) + base_ref[...], place(jnp.sin(ang))


def _mla_prep(cq, ckv, kr, cos, sin, qn_ref, kvn_ref, wq_ref, wk_ref, wv_ref, q_ref, k_ref, v_ref):
    scale = (MLA_NOPE + MLA_ROPE) ** -0.5 * LOG2_E
    half = MLA_ROPE // 2
    lane = lax.broadcasted_iota(jnp.int32, (1, HEAD_PAD), 1)
    sin_up = jnp.where((lane >= MLA_NOPE + half) & (lane < MLA_NOPE + MLA_ROPE), sin, 0.0)
    sin_dn = jnp.where((lane >= MLA_NOPE) & (lane < MLA_NOPE + half), -sin, 0.0)

    def rope(x, c, s_up, s_dn):
        return x * c + pltpu.roll(x, half, 1) * s_up + pltpu.roll(x, HEAD_PAD - half, 1) * s_dn

    cqn = _rms_norm(cq, qn_ref[...]).astype(BF16)
    ckvn = _rms_norm(ckv, kvn_ref[...]).astype(BF16)
    k_rope = rope(kr, cos, sin_up, sin_dn)
    cos_q, sin_up_q, sin_dn_q = cos * scale, sin_up * scale, sin_dn * scale
    for h in range(MLA_HEADS):
        cs = slice(h * HEAD_PAD, (h + 1) * HEAD_PAD)
        q_ref[:, cs] = rope(_dot(cqn, wq_ref[:, cs]), cos_q, sin_up_q, sin_dn_q).astype(BF16)
        k_ref[:, cs] = (_dot(ckvn, wk_ref[:, cs]) + k_rope).astype(BF16)
    v_ref[...] = _dot(ckvn, wv_ref[...]).astype(BF16)


def _s5_block(u, perm_ref, bm_ref, cm_ref, ar_ref, ai_ref, d_ref, wg_ref, o_ref,
              xr_ref, xi_ref, buf_ref, y_ref, nb, fillers):
    n_slab = S5_WIDTH // S5_SLAB
    ns = S5_SLAB_STATE
    u = _dot(perm_ref[...], u).astype(BF16)
    for j in range(n_slab):
        buf_ref[...] = _dot(u[:, j * S5_SLAB:(j + 1) * S5_SLAB], bm_ref[j])
        if fillers:
            fillers.pop(0)()
        ar = jnp.broadcast_to(ar_ref[j], (nb, ns))
        ai = jnp.broadcast_to(ai_ref[j], (nb, ns))
        xr = xr_ref[j]
        xi = xi_ref[j]
        for t in range(T_BLK):
            rows = slice(t * nb, (t + 1) * nb)
            nr = ar * xr - ai * xi + buf_ref[rows, 0:ns]
            ni = ar * xi + ai * xr + buf_ref[rows, ns:2 * ns]
            buf_ref[rows, 0:ns] = nr
            buf_ref[rows, ns:2 * ns] = ni
            xr, xi = nr, ni
        xr_ref[j] = xr
        xi_ref[j] = xi
        y_ref[:, j * S5_SLAB:(j + 1) * S5_SLAB] = _dot(buf_ref[...].astype(BF16), cm_ref[j])

    while fillers:
        fillers.pop(0)()
    y = y_ref[...] + d_ref[...] * u.astype(F32)
    y = 0.5 * y * (1.0 + jnp.tanh(math.sqrt(2.0 / math.pi) * (y + 0.044715 * (y * y * y))))
    gate = _sigmoid(_dot(y.astype(BF16), wg_ref[...]))
    out = (y * gate).astype(BF16)
    o_ref[...] = _dot_tn(perm_ref[...], out).astype(BF16)


def _front_kernel(*refs, nb, ln_in):
    if ln_in:
        x_ref, meta_ref, lng_ref, lnb_ref, *refs = refs
    else:
        h_ref, *refs = refs
    (w_ref, pos_ref, inv_ref, place_ref, base_ref, qn_ref, kvn_ref, wq_ref, wk_ref, wv_ref,
     perm_ref, bm_ref, cm_ref, ar_ref, ai_ref, d_ref, wglu_ref, *refs) = refs
    if ln_in:
        hout_ref, *refs = refs
    zall_ref, xr_ref, xi_ref, buf_ref, y_ref = refs
    zhg_ref, zg_ref, q_ref, k_ref, v_ref, ys5_ref = (
        zall_ref.at[:, o:o + w] for o, w in zip(ZALL_OFF, ZALL_WIDTHS))

    @pl.when(pl.program_id(0) == 0)
    def _():
        xr_ref[...] = jnp.zeros_like(xr_ref)
        xi_ref[...] = jnp.zeros_like(xi_ref)

    cos, sin = _rope_tables(pos_ref, inv_ref, place_ref, base_ref)
    if ln_in:
        hx = _layer_norm(x_ref[...], lng_ref[...], lnb_ref[...])
        hm = _layer_norm(meta_ref[...], lng_ref[...], lnb_ref[...])
        h = jnp.where(pl.program_id(0) == 0, jnp.broadcast_to(hm[None], hx.shape), hx)
        h = h.reshape(hout_ref.shape)
        hout_ref[...] = h
    else:
        h = h_ref[...]
    x = h.astype(BF16)
    za = _dot(x, w_ref[:, 0:ZA_W])
    zkr = _dot(x, w_ref[:, ZA_W + ZHG_W + ZG_W:Z_W])

    def wide_chunk(o_ref, c, off):
        def run():
            o_ref[:, c:c + WIDE_CHUNK] = _dot(x, w_ref[:, off + c:off + c + WIDE_CHUNK]).astype(BF16)
        return run

    fillers = [wide_chunk(zhg_ref, c, ZA_W) for c in range(0, ZHG_W, WIDE_CHUNK)]
    fillers += [wide_chunk(zg_ref, c, ZA_W + ZHG_W) for c in range(0, ZG_W, WIDE_CHUNK)]
    for _ in range(FILL_BEFORE_MLA):
        fillers.pop(0)()
    c0 = S5_WIDTH
    _mla_prep(za[:, c0:c0 + MLA_Q_RANK], za[:, c0 + MLA_Q_RANK:c0 + MLA_Q_RANK + MLA_KV_RANK], zkr,
              cos, sin, qn_ref, kvn_ref, wq_ref, wk_ref, wv_ref, q_ref, k_ref, v_ref)
    _s5_block(za[:, 0:S5_WIDTH].astype(BF16), perm_ref, bm_ref, cm_ref, ar_ref, ai_ref, d_ref, wglu_ref, ys5_ref,
              xr_ref, xi_ref, buf_ref, y_ref, nb, fillers)


def _front(src, w, pos, rope, qn, kvn, wq, wk, wv, perm, bm, cm, ar, ai, dskip, wglu, nb, li):
    ln_in = isinstance(src, tuple)
    rb = T_BLK * nb
    if ln_in:
        x, meta, lng, lnb = src
        d = x.shape[-1]
        r = (x.shape[1] + N_META) * nb
        src_specs = [pl.BlockSpec((nb, T_BLK, d), lambda i: (0, jnp.maximum(i - 1, 0), 0)),
                     _const_spec((N_META, d)), _const_spec((1, d)), _const_spec((1, d))]
    else:
        src = (src,)
        r, d = src[0].shape
        src_specs = [pl.BlockSpec((rb, d), lambda i: (i, 0))]
    hw = MLA_HEADS * HEAD_PAD
    vw = MLA_HEADS * MLA_V
    n_slab = S5_WIDTH // S5_SLAB
    row = lambda width: pl.BlockSpec((rb, width), lambda i: (i, 0))
    out_shape = [jax.ShapeDtypeStruct((r, ZALL_W), BF16)]
    out_specs = [row(ZALL_W)]
    if ln_in:
        out_shape.insert(0, jax.ShapeDtypeStruct((r, d), F32))
        out_specs.insert(0, row(d))
    return pl.pallas_call(
        functools.partial(_front_kernel, nb=nb, ln_in=ln_in),
        out_shape=out_shape,
        grid=(r // rb,),
        in_specs=src_specs + [
            _const_spec((d, Z_W), li), pl.BlockSpec((None, 1, rb), lambda i: (i, 0, 0)),
            _const_spec((MLA_ROPE // 2, 1)), _const_spec((MLA_ROPE // 2, HEAD_PAD)), _const_spec((1, HEAD_PAD)),
            _const_spec((1, MLA_Q_RANK), li), _const_spec((1, MLA_KV_RANK), li),
            _const_spec((MLA_Q_RANK, hw), li), _const_spec((MLA_KV_RANK, hw), li), _const_spec((MLA_KV_RANK, vw), li),
            _const_spec((rb, rb)),
            _const_spec((n_slab, S5_SLAB, 2 * S5_SLAB_STATE), li),
            _const_spec((n_slab, 2 * S5_SLAB_STATE, S5_SLAB), li),
            _const_spec((n_slab, 1, S5_SLAB_STATE), li),
            _const_spec((n_slab, 1, S5_SLAB_STATE), li),
            _const_spec((1, S5_WIDTH), li),
            _const_spec((S5_WIDTH, S5_WIDTH), li),
        ],
        out_specs=out_specs,
        scratch_shapes=[
            pltpu.VMEM((n_slab, nb, S5_SLAB_STATE), F32),
            pltpu.VMEM((n_slab, nb, S5_SLAB_STATE), F32),
            pltpu.VMEM((rb, 2 * S5_SLAB_STATE), F32),
            pltpu.VMEM((rb, S5_WIDTH), F32),
        ],
        compiler_params=_cparams(1, "arbitrary"),
        name="front",
    )(*src, w, pos, *rope, qn, kvn, wq, wk, wv, perm, bm, cm, ar, ai, dskip, wglu)


def _attn_kernel(tab_ref, q_ref, k_ref, v_ref, o_ref, vt_ref, qt_ref, m_ref, acc_ref, *bufs, n_heads, tq, nq):
    xs_ref, xm_ref, ps_ref, pm_ref, a_ref = (bufs[2 * i:2 * i + 2] for i in range(5))
    bpq = tq // T_BLK
    vw = n_heads * MLA_V

    def values_t(vb):
        vt = vb.astype(F32).T
        ext = jnp.where(lax.broadcasted_iota(jnp.int32, (ATT_VE - MLA_V, vt.shape[1]), 0) == 0, 1.0, 0.0)
        parts = []
        for h in range(n_heads):
            parts += [vt[h * MLA_V:(h + 1) * MLA_V, :], ext]
        return jnp.concatenate(parts, axis=0).astype(BF16)

    for j in range(nq):
        vt_ref[j] = values_t(_load_rows(v_ref, 1 + j * bpq, bpq, slice(0, vw)))
    v0t = values_t(v_ref[0])

    for i in range(nq):
        qb = _load_rows(q_ref, 1 + i * bpq, bpq, slice(0, n_heads * HEAD_PAD))
        qt_ref[i] = qb.astype(F32).T.astype(BF16)

    def causal(st):
        r = lax.broadcasted_iota(jnp.int32, st.shape, 0)
        c = lax.broadcasted_iota(jnp.int32, st.shape, 1)
        return jnp.where(r <= c, st, MASK_VALUE * LOG2_E)

    heads = range(n_heads)
    qs = [slice(h * HEAD_PAD, (h + 1) * HEAD_PAD) for h in heads]
    vs = [slice(h * ATT_VE, (h + 1) * ATT_VE) for h in heads]
    k0 = [k_ref[0, :, c] for c in qs]

    def normalised(acc):
        return acc[0:MLA_V, :] / acc[MLA_V:MLA_V + 1, :]

    outs = []
    for n in heads:
        st = causal(_dot_nt(k0[n], q_ref[0, :, qs[n]]))
        p = jnp.exp2(st - jnp.max(st, axis=0, keepdims=True))
        outs.append(normalised(_dot(v0t[vs[n], :], p.astype(BF16))))
    o_ref[0, :, :] = jnp.concatenate(outs, axis=0).T.astype(BF16)

    def key_tile(j, n):
        return _load_rows(k_ref, 1 + j * bpq, bpq, qs[n])

    def diag_scores(i, s):
        for n in heads:
            xs_ref[s][n] = _dot(key_tile(i, n), qt_ref[i, qs[n], :])
            xm_ref[s][n] = _dot(k0[n], qt_ref[i, qs[n], :])

    def diag_stats(i, s):
        for n in heads:
            st, sm = causal(xs_ref[s][n]), xm_ref[s][n]
            m = jnp.maximum(jnp.max(st, axis=0, keepdims=True), jnp.max(sm, axis=0, keepdims=True))
            p = jnp.exp2(st - m)
            pm = jnp.exp2(sm - m)
            m_ref[i, n] = m
            ps_ref[s][n] = p.astype(BF16)
            pm_ref[s][n] = pm.astype(BF16)

    def diag_values(i, s):
        for n in heads:
            acc_ref[i, vs[n], :] = _dot(vt_ref[i, vs[n], :], ps_ref[s][n]) + _dot(v0t[vs[n], :], pm_ref[s][n])

    _pipeline3(nq, diag_scores, diag_stats, diag_values)

    def off_scores(f, s):
        i, j = tab_ref[0, f], tab_ref[1, f]
        for n in heads:
            xs_ref[s][n] = _dot(key_tile(j, n), qt_ref[i, qs[n], :])

    def off_stats(f, s):
        i = tab_ref[0, f]
        for n in heads:
            x = xs_ref[s][n]
            m_old = m_ref[i, n]
            m = jnp.maximum(m_old, jnp.max(x, axis=0, keepdims=True))
            a = jnp.exp2(m_old - m)
            p = jnp.exp2(x - m)
            m_ref[i, n] = m
            a_ref[s][n] = a
            ps_ref[s][n] = p.astype(BF16)

    def off_values(f, s):
        i, j = tab_ref[0, f], tab_ref[1, f]
        for n in heads:
            acc_ref[i, vs[n], :] = a_ref[s][n] * acc_ref[i, vs[n], :] + _dot(vt_ref[j, vs[n], :], ps_ref[s][n])

    _pipeline3(nq * (nq - 1) // 2, off_scores, off_stats, off_values)

    for i in range(nq):
        o = jnp.concatenate([normalised(acc_ref[i, vs[n], :]) for n in heads], axis=0)
        _store_rows(o_ref, 1 + i * bpq, bpq, slice(0, vw), o.T.astype(BF16))


def _pipeline3(n, stage1, stage2, stage3):
    if n == 0:
        return
    stage1(0, 0)
    if n > 1:
        stage1(1, 1)
    stage2(0, 0)

    def step(t, par):
        stage1(t + 2, par)
        stage2(t + 1, 1 - par)
        stage3(t, par)

    steady = max(n - 2, 0)
    if steady >= 2:
        def body(u, carry):
            step(2 * u, 0)
            step(2 * u + 1, 1)
            return carry

        lax.fori_loop(0, steady // 2, body, 0)
    if steady % 2:
        step(steady - 1, (steady - 1) % 2)
    if n > 1:
        stage3(n - 2, (n - 2) % 2)
        stage2(n - 1, (n - 1) % 2)
    stage3(n - 1, (n - 1) % 2)


def _off_diagonal_order(nq):
    left = [(i, j) for i in range(nq) for j in range(i)]
    order = []
    while left:
        count = {}
        for i, _ in left:
            count[i] = count.get(i, 0) + 1
        ok = [p for p in left if not order or p[0] != order[-1][0]] or left
        pick = max(ok, key=lambda p: (count[p[0]], -p[1]))
        order.append(pick)
        left.remove(pick)
    return order or [(0, 0)]


def _attention(zall, nb, nblk):
    z4 = zall.reshape(nblk, nb, T_BLK, ZALL_W)
    hps = ATT_HEADS_PER_STEP
    n_hp = MLA_HEADS // hps
    qw = hps * HEAD_PAD
    vw = hps * MLA_V
    s = (nblk - 1) * T_BLK
    tq = min(ATT_TQ, s)
    nq = s // tq
    col = lambda b, p: p
    tab = jnp.asarray(np.array(_off_diagonal_order(nq), np.int32).T)
    o = pl.pallas_call(
        functools.partial(_attn_kernel, n_heads=hps, tq=tq, nq=nq),
        out_shape=jax.ShapeDtypeStruct((nblk, nb, T_BLK, MLA_HEADS * MLA_V), BF16),
        grid=(nb, n_hp),
        in_specs=[pl.BlockSpec(memory_space=pltpu.SMEM),
                  _seq_spec(nblk, qw, lambda b, p: OFF_Q // qw + p),
                  _seq_spec(nblk, qw, lambda b, p: OFF_K // qw + p),
                  _seq_spec(nblk, vw, lambda b, p: OFF_V // vw + p)],
        out_specs=_seq_spec(nblk, vw, col),
        scratch_shapes=[pltpu.VMEM((nq, hps * ATT_VE, tq), BF16), pltpu.VMEM((nq, qw, tq), BF16),
                        pltpu.VMEM((nq, hps, 1, tq), F32), pltpu.VMEM((nq, hps * ATT_VE, tq), F32)]
        + 2 * [pltpu.VMEM((hps, tq, tq), F32)] + 2 * [pltpu.VMEM((hps, N_META, tq), F32)]
        + 2 * [pltpu.VMEM((hps, tq, tq), BF16)] + 2 * [pltpu.VMEM((hps, N_META, tq), BF16)]
        + 2 * [pltpu.VMEM((hps, 1, tq), F32)],
        compiler_params=_cparams(2),
        name="mla_attn",
    )(tab, z4, z4, z4)
    return o.reshape(nblk * nb * T_BLK, MLA_HEADS * MLA_V)


def _time_major_perm(nb):
    p = np.zeros((T_BLK * nb, T_BLK * nb), np.float32)
    for b in range(nb):
        for t in range(T_BLK):
            p[t * nb + b, b * T_BLK + t] = 1.0
    return p


def _block_row_bcast(x, m, row):
    t, c = x.shape
    if m == t:
        return jnp.broadcast_to(x[row:row + 1, :], x.shape)
    x3 = x.reshape(t // m, m, c)
    return jnp.broadcast_to(x3[:, row:row + 1, :], x3.shape).reshape(t, c)


def _hgrn_gates(blk0, n_blk, f_ref, lb_ref):
    lb = lb_ref[...]
    zf = _load_rows(f_ref, blk0, n_blk, slice(0, HG_QK)).astype(F32)
    e = jnp.exp(-jnp.abs(zf))
    rcp = 1.0 / (1.0 + e)
    pos = zf >= 0.0
    sig_p = jnp.where(pos, rcp, e * rcp)
    sig_n = jnp.where(pos, e * rcp, rcp)
    f = lb + (1.0 - lb) * sig_p
    log_f = jnp.log2(jnp.maximum(f, HG_F_MIN))
    k = (1.0 - lb) * sig_n
    hi = log_f.astype(BF16)
    lo = (log_f - hi.astype(F32)).astype(BF16)
    return k, hi, lo


def _hgrn_mix(blk0, n_blk, gates, q_ref, v_ref, tri_ref, lvl_ref, st_ref):
    t = n_blk * T_BLK
    cs = slice(0, HG_QK)
    heads = [slice(h * HG_KEY, (h + 1) * HG_KEY) for h in range(HG_HEADS)]
    k, hi, lo = gates
    q = _load_rows(q_ref, blk0, n_blk, cs).astype(F32)
    v = _load_rows(v_ref, blk0, n_blk, cs)
    tri = tri_ref[0:t, 0:t]
    cum = _dot(tri, hi) + _dot(tri, lo)
    lvl = lvl_ref[0:t, 0:t]
    r_idx = lax.broadcasted_iota(jnp.int32, (t, HG_QK), 0)
    c8 = _block_row_bcast(cum, HG_BOTTOM, HG_BOTTOM // 2 - 1)
    qe = (q * jnp.exp2(cum - c8)).astype(BF16)
    ke = (k * jnp.exp2(c8 - cum)).astype(BF16)
    n_lvl = len(HG_LEVELS)
    scores = [jnp.where(lvl == n_lvl, _dot_nt(qe[:, hs], ke[:, hs]), 0.0) for hs in heads]
    for li, m in enumerate(HG_LEVELS):
        if m > t:
            continue
        half = m // 2
        cmid = _block_row_bcast(cum, m, half - 1)
        upper = (r_idx & (m - 1)) >= half
        ex = jnp.exp2(jnp.where(upper, cum - cmid, cmid - cum))
        qe = jnp.where(upper, q * ex, 0.0).astype(BF16)
        ke = jnp.where(upper, 0.0, k * ex).astype(BF16)
        scores = [jnp.where(lvl == li, _dot_nt(qe[:, hs], ke[:, hs]), s) for s, hs in zip(scores, heads)]
    qd = (q * jnp.exp2(cum)).astype(BF16)
    last = cum[t - 1:t, :]
    kd = (k * jnp.exp2(last - cum)).astype(BF16)
    dec = jnp.exp2(last)
    outs = []
    for h, hs in enumerate(heads):
        st = st_ref[h]
        outs.append(_dot(scores[h].astype(BF16), v[:, hs]) + _dot_nt(qd[:, hs], st.astype(BF16)))
        st_ref[h] = st * dec[:, hs] + _dot_tn(v[:, hs], kd[:, hs])
    return tuple(outs)


def _hgrn_out(blk0, n_blk, outs, g_ref, on_ref, o_ref):
    cs = slice(0, HG_VW)
    g = _load_rows(g_ref, blk0, n_blk, cs).astype(F32)
    o = jnp.concatenate([o * lax.rsqrt(jnp.mean(o * o, axis=-1, keepdims=True) + 1e-6) for o in outs], axis=1)
    _store_rows(o_ref, blk0, n_blk, cs, (o * on_ref[...] * (g * _sigmoid(g))).astype(BF16))


def _hgrn_kernel(q_ref, f_ref, v_ref, g_ref, lb_ref, on_ref, tri_ref, lvl_ref, o_ref, st_ref, ob0_ref, ob1_ref,
                 *, n_chunks):
    bpc = HG_CHUNK // T_BLK
    ob_ref = (ob0_ref, ob1_ref)
    heads = [slice(h * HG_VAL, (h + 1) * HG_VAL) for h in range(HG_HEADS)]
    gates = functools.partial(_hgrn_gates, f_ref=f_ref, lb_ref=lb_ref)
    mix = functools.partial(_hgrn_mix, q_ref=q_ref, v_ref=v_ref, tri_ref=tri_ref, lvl_ref=lvl_ref, st_ref=st_ref)
    out = functools.partial(_hgrn_out, g_ref=g_ref, on_ref=on_ref, o_ref=o_ref)
    st_ref[...] = jnp.zeros_like(st_ref)
    out(0, 1, mix(0, 1, gates(0, 1)))

    def stage_mix(c, s):
        blk0 = 1 + c * bpc
        for hs, o in zip(heads, mix(blk0, bpc, gates(blk0, bpc))):
            ob_ref[s][:, hs] = o

    def stage_out(c, s):
        out(1 + c * bpc, bpc, tuple(ob_ref[s][:, hs] for hs in heads))

    _pipeline3(n_chunks, lambda c, s: None, stage_mix, stage_out)


def _hgrn(zall, lb, onorm, tri, lvl, nb, nblk, li):
    z4 = zall.reshape(nblk, nb, T_BLK, ZALL_W)
    col = lambda j: _seq_spec(nblk, HG_QK, lambda b: OFF_HG // HG_QK + j)
    o = pl.pallas_call(
        functools.partial(_hgrn_kernel, n_chunks=(nblk - 1) * T_BLK // HG_CHUNK),
        out_shape=jax.ShapeDtypeStruct((nblk, nb, T_BLK, HG_VW), BF16),
        grid=(nb,),
        in_specs=[col(0), col(1), col(2), col(3),
                  _const_spec((1, HG_QK), li), _const_spec((1, HG_VW), li),
                  _const_spec((HG_CHUNK, HG_CHUNK)), _const_spec((HG_CHUNK, HG_CHUNK))],
        out_specs=_seq_spec(nblk, HG_VW, lambda b: 0),
        scratch_shapes=[pltpu.VMEM((HG_HEADS, HG_VAL, HG_KEY), F32)] + 2 * [pltpu.VMEM((HG_CHUNK, HG_VW), F32)],
        compiler_params=_cparams(1),
        name="hgrn2",
    )(z4, z4, z4, z4, lb, onorm, tri, lvl)
    return o.reshape(nblk * nb * T_BLK, HG_VW)


def _hgrn_level_matrix():
    r = np.arange(HG_CHUNK)[:, None]
    c = np.arange(HG_CHUNK)[None, :]
    lvl = np.zeros((HG_CHUNK, HG_CHUNK), np.int32)
    for li, m in enumerate(HG_LEVELS):
        lvl = np.where(r // m == c // m, li, lvl)
    lvl = np.where(r // HG_BOTTOM == c // HG_BOTTOM, len(HG_LEVELS), lvl)
    return np.where(c <= r, lvl, -1).astype(np.int32)


def _back_kernel(om_ref, os_ref, oh_ref, gm_ref, gs_ref, gh_ref, h_ref,
                 wm_ref, ws_ref, wh_ref, wo_ref, g1_ref, b1_ref,
                 wg_ref, wu_ref, wd_ref, g2_ref, b2_ref, o_ref, r1_ref, r2_ref, *, alpha):
    @pl.when(pl.program_id(0) == 0)
    def _():
        r1_ref[...] = jnp.zeros_like(r1_ref)
        r2_ref[...] = jnp.zeros_like(r2_ref)

    r1 = r1_ref[...]
    r2 = r2_ref[...]
    ym = _dot(om_ref[...], wm_ref[...])
    ys = _dot(os_ref[...], ws_ref[...])
    yh = _dot(oh_ref[...], wh_ref[...])
    o_ref[...] = _layer_norm(r2, g2_ref[...], b2_ref[...]).reshape(o_ref.shape)
    h1 = _layer_norm(r1, g1_ref[...], b1_ref[...])
    hb = h1.astype(BF16)
    a = _dot(hb, wg_ref[...])
    u = _dot(hb, wu_ref[...])
    mixed = _sigmoid(gm_ref[...].astype(F32)) * ym
    mixed += _sigmoid(gs_ref[...].astype(F32)) * ys
    mixed += _sigmoid(gh_ref[...].astype(F32)) * yh
    r1_ref[...] = alpha * h_ref[...] + _dot(mixed.astype(BF16), wo_ref[...])
    r2_ref[...] = alpha * h1 + _dot((a * _sigmoid(a) * u).astype(BF16), wd_ref[...])


def _back(om, oh, zall, h, wm, ws, wh, wo, g1, b1, wg, wu, wd, g2, b2, nb, alpha, final, li):
    r, d = h.shape
    bw = om.shape[1]
    dff = wg.shape[-1]
    rb = T_BLK * nb
    skip = 1 if final else 0
    n_blk = r // rb - skip
    lag = 2
    row = lambda width, j=0: pl.BlockSpec((rb, width), lambda i: (jnp.minimum(i, n_blk - 1) + skip, j))
    if final:
        out_shape = jax.ShapeDtypeStruct((nb, n_blk * T_BLK, d), F32)
        out_spec = pl.BlockSpec((nb, T_BLK, d), lambda i: (0, jnp.maximum(i - lag, 0), 0))
    else:
        out_shape = jax.ShapeDtypeStruct((r, d), F32)
        out_spec = pl.BlockSpec((rb, d), lambda i: (jnp.maximum(i - lag, 0), 0))
    return pl.pallas_call(
        functools.partial(_back_kernel, alpha=alpha),
        out_shape=out_shape,
        grid=(n_blk + lag,),
        in_specs=[row(bw), row(bw, OFF_Y // bw), row(bw),
                  row(d, OFF_G // d), row(d, OFF_G // d + 1), row(d, OFF_G // d + 2), row(d),
                  _const_spec((bw, d), li), _const_spec((bw, d), li), _const_spec((bw, d), li),
                  _const_spec((d, d), li), _const_spec((1, d), li), _const_spec((1, d), li),
                  _const_spec((d, dff), li), _const_spec((d, dff), li), _const_spec((dff, d), li),
                  _const_spec((1, d), li), _const_spec((1, d), li)],
        out_specs=out_spec,
        scratch_shapes=[pltpu.VMEM((rb, d), F32), pltpu.VMEM((rb, d), F32)],
        compiler_params=_cparams(1, "arbitrary"),
        name="merge_ffn",
    )(om, zall, oh, zall, zall, zall, h, wm, ws, wh, wo, g1, b1, wg, wu, wd, g2, b2)


def _permute_w_in(w):
    mla_in = MLA_Q_RANK + MLA_KV_RANK + MLA_ROPE
    s5_0, hg_0 = mla_in, mla_in + S5_WIDTH
    g_0 = hg_0 + 2 * HG_QK + 2 * HG_VW
    pad0 = jnp.zeros((w.shape[0], MLA_NOPE), w.dtype)
    pad1 = jnp.zeros((w.shape[0], ZKR_W - MLA_NOPE - MLA_ROPE), w.dtype)
    return jnp.concatenate([w[:, s5_0:hg_0], w[:, 0:MLA_Q_RANK + MLA_KV_RANK], w[:, hg_0:g_0], w[:, g_0:],
                            pad0, w[:, MLA_Q_RANK + MLA_KV_RANK:mla_in], pad1], axis=1).astype(BF16)


def _mla_weights(w_uq, w_ukv):
    rq, rkv = w_uq.shape[0], w_ukv.shape[0]
    zpad = HEAD_PAD - MLA_NOPE - MLA_ROPE
    wq = w_uq.reshape(rq, MLA_HEADS, MLA_NOPE + MLA_ROPE)
    q_nope, q_rope = wq[..., :MLA_NOPE], wq[..., MLA_NOPE:]
    zq = jnp.zeros((rq, MLA_HEADS, zpad), w_uq.dtype)
    wq_p = jnp.concatenate([q_nope, q_rope, zq], axis=-1).reshape(rq, -1)
    wkv = w_ukv.reshape(rkv, MLA_HEADS, MLA_NOPE + MLA_V)
    zk = jnp.zeros((rkv, MLA_HEADS, HEAD_PAD - MLA_NOPE), w_ukv.dtype)
    wk_p = jnp.concatenate([wkv[..., :MLA_NOPE], zk], axis=-1).reshape(rkv, -1)
    wv = wkv[..., MLA_NOPE:].reshape(rkv, -1)
    return [a.astype(BF16) for a in (wq_p, wk_p, wv)]


def _rope_consts():
    half = MLA_ROPE // 2
    inv = ROPE_THETA ** (-(jnp.arange(0, MLA_ROPE, 2, dtype=F32) / MLA_ROPE))
    place = np.zeros((half, HEAD_PAD), np.float32)
    place[np.arange(half), MLA_NOPE + np.arange(half)] = 1.0
    place[np.arange(half), MLA_NOPE + half + np.arange(half)] = 1.0
    base = np.zeros((1, HEAD_PAD), np.float32)
    base[0, :MLA_NOPE] = 1.0
    return inv[:, None], jnp.asarray(place, BF16), jnp.asarray(base)


def _s5_params(lam_re, lam_im, log_dt, b_re, b_im, c_re, c_im):
    lr = jnp.minimum(lam_re.astype(F32), -1e-4)
    li = lam_im.astype(F32)
    dt = jnp.exp(log_dt.astype(F32))[:, None]
    mag = jnp.exp(lr * dt)
    ab_r = mag * jnp.cos(li * dt)
    ab_i = mag * jnp.sin(li * dt)
    den = lr * lr + li * li
    nr = ab_r - 1.0
    coef_r = ((nr * lr + ab_i * li) / den)[..., None]
    coef_i = ((ab_i * lr - nr * li) / den)[..., None]
    bb_r = coef_r * b_re.astype(F32) - coef_i * b_im.astype(F32)
    bb_i = coef_r * b_im.astype(F32) + coef_i * b_re.astype(F32)
    n_slab = S5_WIDTH // S5_SLAB
    gps = S5_SLAB // S5_GROUP
    eye = jnp.eye(gps, dtype=F32)

    def in_mat(bb):
        b4 = bb.reshape(n_slab, gps, S5_STATE, S5_GROUP)
        return jnp.einsum('jgnc,gh->jgchn', b4, eye).reshape(n_slab, S5_SLAB, S5_SLAB_STATE)

    def out_mat(cc):
        c4 = cc.astype(F32).reshape(n_slab, gps, S5_GROUP, S5_STATE)
        return jnp.einsum('jgcn,gh->jgnhc', c4, eye).reshape(n_slab, S5_SLAB_STATE, S5_SLAB)

    bm = jnp.concatenate([in_mat(bb_r), in_mat(bb_i)], axis=2).astype(BF16)
    cm = jnp.concatenate([out_mat(c_re), -out_mat(c_im)], axis=1).astype(BF16)
    ar = ab_r.reshape(n_slab, 1, S5_SLAB_STATE)
    ai = ab_i.reshape(n_slab, 1, S5_SLAB_STATE)
    return bm, cm, ar, ai


def kernel(x, positions, meta_tokens, ln_in_g, ln_in_b, w_in, mla_q_norm, mla_w_uq, mla_kv_norm, mla_w_ukv,
           s5_lam_re, s5_lam_im, s5_log_dt, s5_b_re, s5_b_im, s5_c_re, s5_c_im, s5_d, s5_w_glu,
           hg_lb_logits, hg_out_norm, w_br_mla, w_br_s5, w_br_hg, w_out, ln1_g, ln1_b,
           w_ffn_gate, w_ffn_up, w_ffn_down, ln2_g, ln2_b):
    nb, s, d = x.shape
    depth = w_in.shape[0]
    nblk = (s + N_META) // T_BLK
    alpha = (2 * depth) ** 0.25
    row2 = lambda a: a.astype(F32)[None, :]
    rows3 = lambda a: a.astype(F32)[:, None, :]
    bf16 = lambda a: a.astype(BF16)

    meta_pos = jnp.broadcast_to(jnp.arange(N_META, dtype=jnp.int32)[None, :], (nb, N_META))
    pos = jnp.concatenate([meta_pos, positions.astype(jnp.int32) + N_META], axis=1)
    pos = pos.reshape(nb, nblk, T_BLK).transpose(1, 0, 2).reshape(nblk, 1, nb * T_BLK)
    rope = _rope_consts()
    p_lb = jax.nn.softmax(hg_lb_logits.astype(F32), axis=0)
    lower_bounds = jnp.cumsum(p_lb, axis=0) - p_lb[0]
    tri = jnp.asarray(np.tril(np.ones((HG_CHUNK, HG_CHUNK), np.float32)), BF16)
    lvl = jnp.asarray(_hgrn_level_matrix())
    perm = jnp.asarray(_time_major_perm(nb), BF16)

    front_p = (jax.vmap(_permute_w_in)(w_in), pos, rope, rows3(mla_q_norm), rows3(mla_kv_norm),
               *jax.vmap(_mla_weights)(mla_w_uq, mla_w_ukv), perm,
               *jax.vmap(_s5_params)(s5_lam_re, s5_lam_im, s5_log_dt, s5_b_re, s5_b_im, s5_c_re, s5_c_im),
               rows3(s5_d), bf16(s5_w_glu))
    hgrn_p = (lower_bounds[:, None, :], rows3(hg_out_norm), tri, lvl)
    back_p = (bf16(w_br_mla), bf16(w_br_s5), bf16(w_br_hg), bf16(w_out), rows3(ln1_g), rows3(ln1_b),
              bf16(w_ffn_gate), bf16(w_ffn_up), bf16(w_ffn_down), rows3(ln2_g), rows3(ln2_b))

    h = (x, meta_tokens.astype(x.dtype), row2(ln_in_g), row2(ln_in_b))
    for li in range(depth):
        outs = _front(h, *front_p, nb, li)
        if li == 0:
            h, *outs = outs
        zall, = outs
        o_mla = _attention(zall, nb, nblk)
        o_hg = _hgrn(zall, *hgrn_p, nb, nblk, li)
        h = _back(o_mla, o_hg, zall, h, *back_p, nb, alpha, final=li == depth - 1, li=li)
    return h
```

```python
import functools
import math

import jax
import jax.numpy as jnp
import numpy as np
from jax import lax
from jax.experimental import pallas as pl
from jax.experimental.pallas import tpu as pltpu

F32 = jnp.float32
BF16 = jnp.bfloat16

N_META = 16
MLA_HEADS = 8
MLA_NOPE = 64
MLA_ROPE = 32
MLA_V = 64
MLA_Q_RANK = 256
MLA_KV_RANK = 256
ROPE_THETA = 10000.0
MASK_VALUE = -1e9
LOG2_E = math.log2(math.e)
HEAD_PAD = 128
S5_WIDTH = 512
S5_GROUP = 16
S5_GROUPS = S5_WIDTH // S5_GROUP
S5_STATE = 64
S5_SLAB = 128
S5_SLAB_STATE = (S5_SLAB // S5_GROUP) * S5_STATE
HG_HEADS = 4
HG_KEY = 128
HG_VAL = 128
HG_QK = HG_HEADS * HG_KEY
HG_VW = HG_HEADS * HG_VAL
HG_F_MIN = 1e-6
HG_CHUNK = 128
HG_LEVELS = (128, 64, 32, 16)
HG_BOTTOM = 8
N_BRANCH = 3
T_BLK = 16
ATT_TQ = 256
ATT_HEADS_PER_STEP = 4
ATT_VE = MLA_V + 16
VMEM_LIMIT = 56 * 1024 * 1024

ZA_W = 1024
ZHG_W = 2048
ZG_W = 3072
ZKR_W = 128
Z_W = ZA_W + ZHG_W + ZG_W + ZKR_W
ZALL_WIDTHS = (ZHG_W, ZG_W, MLA_HEADS * HEAD_PAD, MLA_HEADS * HEAD_PAD, MLA_HEADS * MLA_V, S5_WIDTH)
ZALL_OFF = tuple(sum(ZALL_WIDTHS[:i]) for i in range(len(ZALL_WIDTHS)))
ZALL_W = sum(ZALL_WIDTHS)
OFF_HG, OFF_G, OFF_Q, OFF_K, OFF_V, OFF_Y = ZALL_OFF
WIDE_CHUNK = 512
FILL_BEFORE_MLA = 3


def _cparams(n_grid, sem="parallel"):
    return pltpu.CompilerParams(dimension_semantics=(sem,) * n_grid, vmem_limit_bytes=VMEM_LIMIT)


def _const_spec(shape, layer=None):
    nd = len(shape)
    if layer is None:
        return pl.BlockSpec(shape, lambda *_: (0,) * nd, pipeline_mode=pl.Buffered(1))
    return pl.BlockSpec((None,) + tuple(shape), lambda *_: (layer,) + (0,) * nd, pipeline_mode=pl.Buffered(1))


def _seq_spec(nblk, width, col):
    return pl.BlockSpec((nblk, None, T_BLK, width), lambda b, *g: (0, b, 0, col(b, *g)))


def _dot(a, b):
    return jnp.dot(a, b, preferred_element_type=F32)


def _dot_nt(a, b):
    return lax.dot_general(a, b, (((1,), (1,)), ((), ())), preferred_element_type=F32)


def _dot_tn(a, b):
    return lax.dot_general(a, b, (((0,), (0,)), ((), ())), preferred_element_type=F32)


def _sigmoid(x):
    return 1.0 / (1.0 + jnp.exp(-x))


def _layer_norm(x, g, b, eps=1e-5):
    mu = jnp.mean(x, axis=-1, keepdims=True)
    xc = x - mu
    var = jnp.mean(xc * xc, axis=-1, keepdims=True)
    return xc * lax.rsqrt(var + eps) * g + b


def _rms_norm(x, g, eps=1e-6):
    return x * lax.rsqrt(jnp.mean(x * x, axis=-1, keepdims=True) + eps) * g


def _load_rows(ref, blk0, n_blk, cs):
    x = ref[pl.ds(blk0, n_blk), :, cs]
    return x.reshape(n_blk * T_BLK, x.shape[-1])


def _store_rows(ref, blk0, n_blk, cs, x):
    ref[pl.ds(blk0, n_blk), :, cs] = x.reshape(n_blk, T_BLK, x.shape[-1])


def _rope_tables(pos_ref, inv_ref, place_ref, base_ref):
    ang = inv_ref[...] * pos_ref[...].astype(F32)

    def place(t):
        hi = t.astype(BF16)
        lo = (t - hi.astype(F32)).astype(BF16)
        return _dot_tn(hi, place_ref[...]) + _dot_tn(lo, place_ref[...])

    return place(jnp.cos(ang)) + base_ref[...], place(jnp.sin(ang))


def _mla_prep(cq, ckv, kr, cos, sin, qn_ref, kvn_ref, wq_ref, wk_ref, wv_ref, q_ref, k_ref, v_ref):
    scale = (MLA_NOPE + MLA_ROPE) ** -0.5 * LOG2_E
    half = MLA_ROPE // 2
    lane = lax.broadcasted_iota(jnp.int32, (1, HEAD_PAD), 1)
    sin_up = jnp.where((lane >= MLA_NOPE + half) & (lane < MLA_NOPE + MLA_ROPE), sin, 0.0)
    sin_dn = jnp.where((lane >= MLA_NOPE) & (lane < MLA_NOPE + half), -sin, 0.0)

    def rope(x, c, s_up, s_dn):
        return x * c + pltpu.roll(x, half, 1) * s_up + pltpu.roll(x, HEAD_PAD - half, 1) * s_dn

    cqn = _rms_norm(cq, qn_ref[...]).astype(BF16)
    ckvn = _rms_norm(ckv, kvn_ref[...]).astype(BF16)
    k_rope = rope(kr, cos, sin_up, sin_dn)
    cos_q, sin_up_q, sin_dn_q = cos * scale, sin_up * scale, sin_dn * scale
    for h in range(MLA_HEADS):
        cs = slice(h * HEAD_PAD, (h + 1) * HEAD_PAD)
        q_ref[:, cs] = rope(_dot(cqn, wq_ref[:, cs]), cos_q, sin_up_q, sin_dn_q).astype(BF16)
        k_ref[:, cs] = (_dot(ckvn, wk_ref[:, cs]) + k_rope).astype(BF16)
    v_ref[...] = _dot(ckvn, wv_ref[...]).astype(BF16)


def _s5_block(u, perm_ref, bm_ref, cm_ref, ar_ref, ai_ref, d_ref, wg_ref, o_ref,
              xr_ref, xi_ref, buf_ref, y_ref, nb, fillers):
    n_slab = S5_WIDTH // S5_SLAB
    ns = S5_SLAB_STATE
    u = _dot(perm_ref[...], u).astype(BF16)
    for j in range(n_slab):
        buf_ref[...] = _dot(u[:, j * S5_SLAB:(j + 1) * S5_SLAB], bm_ref[j])
        if fillers:
            fillers.pop(0)()
        ar = jnp.broadcast_to(ar_ref[j], (nb, ns))
        ai = jnp.broadcast_to(ai_ref[j], (nb, ns))
        xr = xr_ref[j]
        xi = xi_ref[j]
        for t in range(T_BLK):
            rows = slice(t * nb, (t + 1) * nb)
            nr = ar * xr - ai * xi + buf_ref[rows, 0:ns]
            ni = ar * xi + ai * xr + buf_ref[rows, ns:2 * ns]
            buf_ref[rows, 0:ns] = nr
            buf_ref[rows, ns:2 * ns] = ni
            xr, xi = nr, ni
        xr_ref[j] = xr
        xi_ref[j] = xi
        y_ref[:, j * S5_SLAB:(j + 1) * S5_SLAB] = _dot(buf_ref[...].astype(BF16), cm_ref[j])

    while fillers:
        fillers.pop(0)()
    y = y_ref[...] + d_ref[...] * u.astype(F32)
    y = 0.5 * y * (1.0 + jnp.tanh(math.sqrt(2.0 / math.pi) * (y + 0.044715 * (y * y * y))))
    gate = _sigmoid(_dot(y.astype(BF16), wg_ref[...]))
    out = (y * gate).astype(BF16)
    o_ref[...] = _dot_tn(perm_ref[...], out).astype(BF16)


def _front_kernel(*refs, nb, ln_in):
    if ln_in:
        x_ref, meta_ref, lng_ref, lnb_ref, *refs = refs
    else:
        h_ref, *refs = refs
    (w_ref, pos_ref, inv_ref, place_ref, base_ref, qn_ref, kvn_ref, wq_ref, wk_ref, wv_ref,
     perm_ref, bm_ref, cm_ref, ar_ref, ai_ref, d_ref, wglu_ref, *refs) = refs
    if ln_in:
        hout_ref, *refs = refs
    zall_ref, xr_ref, xi_ref, buf_ref, y_ref = refs
    zhg_ref, zg_ref, q_ref, k_ref, v_ref, ys5_ref = (
        zall_ref.at[:, o:o + w] for o, w in zip(ZALL_OFF, ZALL_WIDTHS))

    @pl.when(pl.program_id(0) == 0)
    def _():
        xr_ref[...] = jnp.zeros_like(xr_ref)
        xi_ref[...] = jnp.zeros_like(xi_ref)

    if ln_in:
        hx = _layer_norm(x_ref[...], lng_ref[...], lnb_ref[...])
        hm = _layer_norm(meta_ref[...], lng_ref[...], lnb_ref[...])
        h = jnp.where(pl.program_id(0) == 0, jnp.broadcast_to(hm[None], hx.shape), hx)
        h = h.reshape(hout_ref.shape)
        hout_ref[...] = h
    else:
        h = h_ref[...]
    x = h.astype(BF16)
    za = _dot(x, w_ref[:, 0:ZA_W])
    zkr = _dot(x, w_ref[:, ZA_W + ZHG_W + ZG_W:Z_W])

    def wide_chunk(o_ref, c, off):
        def run():
            o_ref[:, c:c + WIDE_CHUNK] = _dot(x, w_ref[:, off + c:off + c + WIDE_CHUNK]).astype(BF16)
        return run

    fillers = [wide_chunk(zhg_ref, c, ZA_W) for c in range(0, ZHG_W, WIDE_CHUNK)]
    fillers += [wide_chunk(zg_ref, c, ZA_W + ZHG_W) for c in range(0, ZG_W, WIDE_CHUNK)]
    for _ in range(FILL_BEFORE_MLA):
        fillers.pop(0)()
    cos, sin = _rope_tables(pos_ref, inv_ref, place_ref, base_ref)
    c0 = S5_WIDTH
    _mla_prep(za[:, c0:c0 + MLA_Q_RANK], za[:, c0 + MLA_Q_RANK:c0 + MLA_Q_RANK + MLA_KV_RANK], zkr,
              cos, sin, qn_ref, kvn_ref, wq_ref, wk_ref, wv_ref, q_ref, k_ref, v_ref)
    _s5_block(za[:, 0:S5_WIDTH].astype(BF16), perm_ref, bm_ref, cm_ref, ar_ref, ai_ref, d_ref, wglu_ref, ys5_ref,
              xr_ref, xi_ref, buf_ref, y_ref, nb, fillers)


def _front(src, w, pos, rope, qn, kvn, wq, wk, wv, perm, bm, cm, ar, ai, dskip, wglu, nb, li):
    ln_in = isinstance(src, tuple)
    rb = T_BLK * nb
    if ln_in:
        x, meta, lng, lnb = src
        d = x.shape[-1]
        r = (x.shape[1] + N_META) * nb
        src_specs = [pl.BlockSpec((nb, T_BLK, d), lambda i: (0, jnp.maximum(i - 1, 0), 0)),
                     _const_spec((N_META, d)), _const_spec((1, d)), _const_spec((1, d))]
    else:
        src = (src,)
        r, d = src[0].shape
        src_specs = [pl.BlockSpec((rb, d), lambda i: (i, 0))]
    hw = MLA_HEADS * HEAD_PAD
    vw = MLA_HEADS * MLA_V
    n_slab = S5_WIDTH // S5_SLAB
    row = lambda width: pl.BlockSpec((rb, width), lambda i: (i, 0))
    out_shape = [jax.ShapeDtypeStruct((r, ZALL_W), BF16)]
    out_specs = [row(ZALL_W)]
    if ln_in:
        out_shape.insert(0, jax.ShapeDtypeStruct((r, d), F32))
        out_specs.insert(0, row(d))
    return pl.pallas_call(
        functools.partial(_front_kernel, nb=nb, ln_in=ln_in),
        out_shape=out_shape,
        grid=(r // rb,),
        in_specs=src_specs + [
            _const_spec((d, Z_W), li), pl.BlockSpec((None, 1, rb), lambda i: (i, 0, 0)),
            _const_spec((MLA_ROPE // 2, 1)), _const_spec((MLA_ROPE // 2, HEAD_PAD)), _const_spec((1, HEAD_PAD)),
            _const_spec((1, MLA_Q_RANK), li), _const_spec((1, MLA_KV_RANK), li),
            _const_spec((MLA_Q_RANK, hw), li), _const_spec((MLA_KV_RANK, hw), li), _const_spec((MLA_KV_RANK, vw), li),
            _const_spec((rb, rb)),
            _const_spec((n_slab, S5_SLAB, 2 * S5_SLAB_STATE), li),
            _const_spec((n_slab, 2 * S5_SLAB_STATE, S5_SLAB), li),
            _const_spec((n_slab, 1, S5_SLAB_STATE), li),
            _const_spec((n_slab, 1, S5_SLAB_STATE), li),
            _const_spec((1, S5_WIDTH), li),
            _const_spec((S5_WIDTH, S5_WIDTH), li),
        ],
        out_specs=out_specs,
        scratch_shapes=[
            pltpu.VMEM((n_slab, nb, S5_SLAB_STATE), F32),
            pltpu.VMEM((n_slab, nb, S5_SLAB_STATE), F32),
            pltpu.VMEM((rb, 2 * S5_SLAB_STATE), F32),
            pltpu.VMEM((rb, S5_WIDTH), F32),
        ],
        compiler_params=_cparams(1, "arbitrary"),
        name="front",
    )(*src, w, pos, *rope, qn, kvn, wq, wk, wv, perm, bm, cm, ar, ai, dskip, wglu)


def _attn_kernel(tab_ref, q_ref, k_ref, v_ref, o_ref, vt_ref, qt_ref, m_ref, acc_ref, *bufs, n_heads, tq, nq):
    xs_ref, xm_ref, ps_ref, pm_ref, a_ref = (bufs[2 * i:2 * i + 2] for i in range(5))
    bpq = tq // T_BLK
    vw = n_heads * MLA_V

    def values_t(vb):
        vt = vb.astype(F32).T
        ext = jnp.where(lax.broadcasted_iota(jnp.int32, (ATT_VE - MLA_V, vt.shape[1]), 0) == 0, 1.0, 0.0)
        parts = []
        for h in range(n_heads):
            parts += [vt[h * MLA_V:(h + 1) * MLA_V, :], ext]
        return jnp.concatenate(parts, axis=0).astype(BF16)

    for j in range(nq):
        vt_ref[j] = values_t(_load_rows(v_ref, 1 + j * bpq, bpq, slice(0, vw)))
    v0t = values_t(v_ref[0])

    for i in range(nq):
        qb = _load_rows(q_ref, 1 + i * bpq, bpq, slice(0, n_heads * HEAD_PAD))
        qt_ref[i] = qb.astype(F32).T.astype(BF16)

    def causal(st):
        r = lax.broadcasted_iota(jnp.int32, st.shape, 0)
        c = lax.broadcasted_iota(jnp.int32, st.shape, 1)
        return jnp.where(r <= c, st, MASK_VALUE * LOG2_E)

    heads = range(n_heads)
    qs = [slice(h * HEAD_PAD, (h + 1) * HEAD_PAD) for h in heads]
    vs = [slice(h * ATT_VE, (h + 1) * ATT_VE) for h in heads]
    k0 = [k_ref[0, :, c] for c in qs]

    def normalised(acc):
        return acc[0:MLA_V, :] / acc[MLA_V:MLA_V + 1, :]

    outs = []
    for n in heads:
        st = causal(_dot_nt(k0[n], q_ref[0, :, qs[n]]))
        p = jnp.exp2(st - jnp.max(st, axis=0, keepdims=True))
        outs.append(normalised(_dot(v0t[vs[n], :], p.astype(BF16))))
    o_ref[0, :, :] = jnp.concatenate(outs, axis=0).T.astype(BF16)

    def key_tile(j, n):
        return _load_rows(k_ref, 1 + j * bpq, bpq, qs[n])

    def diag_scores(i, s):
        for n in heads:
            xs_ref[s][n] = _dot(key_tile(i, n), qt_ref[i, qs[n], :])
            xm_ref[s][n] = _dot(k0[n], qt_ref[i, qs[n], :])

    def diag_stats(i, s):
        for n in heads:
            st, sm = causal(xs_ref[s][n]), xm_ref[s][n]
            m = jnp.maximum(jnp.max(st, axis=0, keepdims=True), jnp.max(sm, axis=0, keepdims=True))
            p = jnp.exp2(st - m)
            pm = jnp.exp2(sm - m)
            m_ref[i, n] = m
            ps_ref[s][n] = p.astype(BF16)
            pm_ref[s][n] = pm.astype(BF16)

    def diag_values(i, s):
        for n in heads:
            acc_ref[i, vs[n], :] = _dot(vt_ref[i, vs[n], :], ps_ref[s][n]) + _dot(v0t[vs[n], :], pm_ref[s][n])

    _pipeline3(nq, diag_scores, diag_stats, diag_values)

    def off_scores(f, s):
        i, j = tab_ref[0, f], tab_ref[1, f]
        for n in heads:
            xs_ref[s][n] = _dot(key_tile(j, n), qt_ref[i, qs[n], :])

    def off_stats(f, s):
        i = tab_ref[0, f]
        for n in heads:
            x = xs_ref[s][n]
            m_old = m_ref[i, n]
            m = jnp.maximum(m_old, jnp.max(x, axis=0, keepdims=True))
            a = jnp.exp2(m_old - m)
            p = jnp.exp2(x - m)
            m_ref[i, n] = m
            a_ref[s][n] = a
            ps_ref[s][n] = p.astype(BF16)

    def off_values(f, s):
        i, j = tab_ref[0, f], tab_ref[1, f]
        for n in heads:
            acc_ref[i, vs[n], :] = a_ref[s][n] * acc_ref[i, vs[n], :] + _dot(vt_ref[j, vs[n], :], ps_ref[s][n])

    _pipeline3(nq * (nq - 1) // 2, off_scores, off_stats, off_values)

    for i in range(nq):
        o = jnp.concatenate([normalised(acc_ref[i, vs[n], :]) for n in heads], axis=0)
        _store_rows(o_ref, 1 + i * bpq, bpq, slice(0, vw), o.T.astype(BF16))


def _pipeline3(n, stage1, stage2, stage3):
    if n == 0:
        return
    stage1(0, 0)
    if n > 1:
        stage1(1, 1)
    stage2(0, 0)

    def step(t, par):
        stage1(t + 2, par)
        stage2(t + 1, 1 - par)
        stage3(t, par)

    steady = max(n - 2, 0)
    if steady >= 2:
        def body(u, carry):
            step(2 * u, 0)
            step(2 * u + 1, 1)
            return carry

        lax.fori_loop(0, steady // 2, body, 0)
    if steady % 2:
        step(steady - 1, (steady - 1) % 2)
    if n > 1:
        stage3(n - 2, (n - 2) % 2)
        stage2(n - 1, (n - 1) % 2)
    stage3(n - 1, (n - 1) % 2)


def _off_diagonal_order(nq):
    left = [(i, j) for i in range(nq) for j in range(i)]
    order = []
    while left:
        count = {}
        for i, _ in left:
            count[i] = count.get(i, 0) + 1
        ok = [p for p in left if not order or p[0] != order[-1][0]] or left
        pick = max(ok, key=lambda p: (count[p[0]], -p[1]))
        order.append(pick)
        left.remove(pick)
    return order or [(0, 0)]


def _attention(zall, nb, nblk):
    z4 = zall.reshape(nblk, nb, T_BLK, ZALL_W)
    hps = ATT_HEADS_PER_STEP
    n_hp = MLA_HEADS // hps
    qw = hps * HEAD_PAD
    vw = hps * MLA_V
    s = (nblk - 1) * T_BLK
    tq = min(ATT_TQ, s)
    nq = s // tq
    col = lambda b, p: p
    tab = jnp.asarray(np.array(_off_diagonal_order(nq), np.int32).T)
    o = pl.pallas_call(
        functools.partial(_attn_kernel, n_heads=hps, tq=tq, nq=nq),
        out_shape=jax.ShapeDtypeStruct((nblk, nb, T_BLK, MLA_HEADS * MLA_V), BF16),
        grid=(nb, n_hp),
        in_specs=[pl.BlockSpec(memory_space=pltpu.SMEM),
                  _seq_spec(nblk, qw, lambda b, p: OFF_Q // qw + p),
                  _seq_spec(nblk, qw, lambda b, p: OFF_K // qw + p),
                  _seq_spec(nblk, vw, lambda b, p: OFF_V // vw + p)],
        out_specs=_seq_spec(nblk, vw, col),
        scratch_shapes=[pltpu.VMEM((nq, hps * ATT_VE, tq), BF16), pltpu.VMEM((nq, qw, tq), BF16),
                        pltpu.VMEM((nq, hps, 1, tq), F32), pltpu.VMEM((nq, hps * ATT_VE, tq), F32)]
        + 2 * [pltpu.VMEM((hps, tq, tq), F32)] + 2 * [pltpu.VMEM((hps, N_META, tq), F32)]
        + 2 * [pltpu.VMEM((hps, tq, tq), BF16)] + 2 * [pltpu.VMEM((hps, N_META, tq), BF16)]
        + 2 * [pltpu.VMEM((hps, 1, tq), F32)],
        compiler_params=_cparams(2),
        name="mla_attn",
    )(tab, z4, z4, z4)
    return o.reshape(nblk * nb * T_BLK, MLA_HEADS * MLA_V)


def _time_major_perm(nb):
    p = np.zeros((T_BLK * nb, T_BLK * nb), np.float32)
    for b in range(nb):
        for t in range(T_BLK):
            p[t * nb + b, b * T_BLK + t] = 1.0
    return p


def _block_row_bcast(x, m, row):
    t, c = x.shape
    if m == t:
        return jnp.broadcast_to(x[row:row + 1, :], x.shape)
    x3 = x.reshape(t // m, m, c)
    return jnp.broadcast_to(x3[:, row:row + 1, :], x3.shape).reshape(t, c)


def _hgrn_gates(blk0, n_blk, f_ref, lb_ref):
    lb = lb_ref[...]
    zf = _load_rows(f_ref, blk0, n_blk, slice(0, HG_QK)).astype(F32)
    e = jnp.exp(-jnp.abs(zf))
    rcp = 1.0 / (1.0 + e)
    pos = zf >= 0.0
    sig_p = jnp.where(pos, rcp, e * rcp)
    sig_n = jnp.where(pos, e * rcp, rcp)
    f = lb + (1.0 - lb) * sig_p
    log_f = jnp.log2(jnp.maximum(f, HG_F_MIN))
    k = (1.0 - lb) * sig_n
    hi = log_f.astype(BF16)
    lo = (log_f - hi.astype(F32)).astype(BF16)
    return k, hi, lo


def _hgrn_mix(blk0, n_blk, gates, q_ref, v_ref, tri_ref, lvl_ref, st_ref):
    t = n_blk * T_BLK
    cs = slice(0, HG_QK)
    heads = [slice(h * HG_KEY, (h + 1) * HG_KEY) for h in range(HG_HEADS)]
    k, hi, lo = gates
    q = _load_rows(q_ref, blk0, n_blk, cs).astype(F32)
    v = _load_rows(v_ref, blk0, n_blk, cs)
    tri = tri_ref[0:t, 0:t]
    cum = _dot(tri, hi) + _dot(tri, lo)
    lvl = lvl_ref[0:t, 0:t]
    r_idx = lax.broadcasted_iota(jnp.int32, (t, HG_QK), 0)
    c8 = _block_row_bcast(cum, HG_BOTTOM, HG_BOTTOM // 2 - 1)
    qe = (q * jnp.exp2(cum - c8)).astype(BF16)
    ke = (k * jnp.exp2(c8 - cum)).astype(BF16)
    n_lvl = len(HG_LEVELS)
    scores = [jnp.where(lvl == n_lvl, _dot_nt(qe[:, hs], ke[:, hs]), 0.0) for hs in heads]
    for li, m in enumerate(HG_LEVELS):
        if m > t:
            continue
        half = m // 2
        cmid = _block_row_bcast(cum, m, half - 1)
        upper = (r_idx & (m - 1)) >= half
        ex = jnp.exp2(jnp.where(upper, cum - cmid, cmid - cum))
        qe = jnp.where(upper, q * ex, 0.0).astype(BF16)
        ke = jnp.where(upper, 0.0, k * ex).astype(BF16)
        scores = [jnp.where(lvl == li, _dot_nt(qe[:, hs], ke[:, hs]), s) for s, hs in zip(scores, heads)]
    qd = (q * jnp.exp2(cum)).astype(BF16)
    last = cum[t - 1:t, :]
    kd = (k * jnp.exp2(last - cum)).astype(BF16)
    dec = jnp.exp2(last)
    outs = []
    for h, hs in enumerate(heads):
        st = st_ref[h]
        outs.append(_dot(scores[h].astype(BF16), v[:, hs]) + _dot_nt(qd[:, hs], st.astype(BF16)))
        st_ref[h] = st * dec[:, hs] + _dot_tn(v[:, hs], kd[:, hs])
    return tuple(outs)


def _hgrn_out(blk0, n_blk, outs, g_ref, on_ref, o_ref):
    cs = slice(0, HG_VW)
    g = _load_rows(g_ref, blk0, n_blk, cs).astype(F32)
    o = jnp.concatenate([o * lax.rsqrt(jnp.mean(o * o, axis=-1, keepdims=True) + 1e-6) for o in outs], axis=1)
    _store_rows(o_ref, blk0, n_blk, cs, (o * on_ref[...] * (g * _sigmoid(g))).astype(BF16))


def _hgrn_kernel(q_ref, f_ref, v_ref, g_ref, lb_ref, on_ref, tri_ref, lvl_ref, o_ref, st_ref, ob0_ref, ob1_ref,
                 *, n_chunks):
    bpc = HG_CHUNK // T_BLK
    ob_ref = (ob0_ref, ob1_ref)
    heads = [slice(h * HG_VAL, (h + 1) * HG_VAL) for h in range(HG_HEADS)]
    gates = functools.partial(_hgrn_gates, f_ref=f_ref, lb_ref=lb_ref)
    mix = functools.partial(_hgrn_mix, q_ref=q_ref, v_ref=v_ref, tri_ref=tri_ref, lvl_ref=lvl_ref, st_ref=st_ref)
    out = functools.partial(_hgrn_out, g_ref=g_ref, on_ref=on_ref, o_ref=o_ref)
    st_ref[...] = jnp.zeros_like(st_ref)
    out(0, 1, mix(0, 1, gates(0, 1)))

    def stage_mix(c, s):
        blk0 = 1 + c * bpc
        for hs, o in zip(heads, mix(blk0, bpc, gates(blk0, bpc))):
            ob_ref[s][:, hs] = o

    def stage_out(c, s):
        out(1 + c * bpc, bpc, tuple(ob_ref[s][:, hs] for hs in heads))

    _pipeline3(n_chunks, lambda c, s: None, stage_mix, stage_out)


def _hgrn(zall, lb, onorm, tri, lvl, nb, nblk, li):
    z4 = zall.reshape(nblk, nb, T_BLK, ZALL_W)
    col = lambda j: _seq_spec(nblk, HG_QK, lambda b: OFF_HG // HG_QK + j)
    o = pl.pallas_call(
        functools.partial(_hgrn_kernel, n_chunks=(nblk - 1) * T_BLK // HG_CHUNK),
        out_shape=jax.ShapeDtypeStruct((nblk, nb, T_BLK, HG_VW), BF16),
        grid=(nb,),
        in_specs=[col(0), col(1), col(2), col(3),
                  _const_spec((1, HG_QK), li), _const_spec((1, HG_VW), li),
                  _const_spec((HG_CHUNK, HG_CHUNK)), _const_spec((HG_CHUNK, HG_CHUNK))],
        out_specs=_seq_spec(nblk, HG_VW, lambda b: 0),
        scratch_shapes=[pltpu.VMEM((HG_HEADS, HG_VAL, HG_KEY), F32)] + 2 * [pltpu.VMEM((HG_CHUNK, HG_VW), F32)],
        compiler_params=_cparams(1),
        name="hgrn2",
    )(z4, z4, z4, z4, lb, onorm, tri, lvl)
    return o.reshape(nblk * nb * T_BLK, HG_VW)


def _hgrn_level_matrix():
    r = np.arange(HG_CHUNK)[:, None]
    c = np.arange(HG_CHUNK)[None, :]
    lvl = np.zeros((HG_CHUNK, HG_CHUNK), np.int32)
    for li, m in enumerate(HG_LEVELS):
        lvl = np.where(r // m == c // m, li, lvl)
    lvl = np.where(r // HG_BOTTOM == c // HG_BOTTOM, len(HG_LEVELS), lvl)
    return np.where(c <= r, lvl, -1).astype(np.int32)


def _back_kernel(om_ref, os_ref, oh_ref, gm_ref, gs_ref, gh_ref, h_ref,
                 wm_ref, ws_ref, wh_ref, wo_ref, g1_ref, b1_ref,
                 wg_ref, wu_ref, wd_ref, g2_ref, b2_ref, o_ref, r1_ref, r2_ref, *, alpha):
    @pl.when(pl.program_id(0) == 0)
    def _():
        r1_ref[...] = jnp.zeros_like(r1_ref)
        r2_ref[...] = jnp.zeros_like(r2_ref)

    r1 = r1_ref[...]
    r2 = r2_ref[...]
    ym = _dot(om_ref[...], wm_ref[...])
    ys = _dot(os_ref[...], ws_ref[...])
    yh = _dot(oh_ref[...], wh_ref[...])
    o_ref[...] = _layer_norm(r2, g2_ref[...], b2_ref[...]).reshape(o_ref.shape)
    h1 = _layer_norm(r1, g1_ref[...], b1_ref[...])
    hb = h1.astype(BF16)
    a = _dot(hb, wg_ref[...])
    u = _dot(hb, wu_ref[...])
    mixed = _sigmoid(gm_ref[...].astype(F32)) * ym
    mixed += _sigmoid(gs_ref[...].astype(F32)) * ys
    mixed += _sigmoid(gh_ref[...].astype(F32)) * yh
    r1_ref[...] = alpha * h_ref[...] + _dot(mixed.astype(BF16), wo_ref[...])
    r2_ref[...] = alpha * h1 + _dot((a * _sigmoid(a) * u).astype(BF16), wd_ref[...])


def _back(om, oh, zall, h, wm, ws, wh, wo, g1, b1, wg, wu, wd, g2, b2, nb, alpha, final, li):
    r, d = h.shape
    bw = om.shape[1]
    dff = wg.shape[-1]
    rb = T_BLK * nb
    skip = 1 if final else 0
    n_blk = r // rb - skip
    lag = 2
    row = lambda width, j=0: pl.BlockSpec((rb, width), lambda i: (jnp.minimum(i, n_blk - 1) + skip, j))
    if final:
        out_shape = jax.ShapeDtypeStruct((nb, n_blk * T_BLK, d), F32)
        out_spec = pl.BlockSpec((nb, T_BLK, d), lambda i: (0, jnp.maximum(i - lag, 0), 0))
    else:
        out_shape = jax.ShapeDtypeStruct((r, d), F32)
        out_spec = pl.BlockSpec((rb, d), lambda i: (jnp.maximum(i - lag, 0), 0))
    return pl.pallas_call(
        functools.partial(_back_kernel, alpha=alpha),
        out_shape=out_shape,
        grid=(n_blk + lag,),
        in_specs=[row(bw), row(bw, OFF_Y // bw), row(bw),
                  row(d, OFF_G // d), row(d, OFF_G // d + 1), row(d, OFF_G // d + 2), row(d),
                  _const_spec((bw, d), li), _const_spec((bw, d), li), _const_spec((bw, d), li),
                  _const_spec((d, d), li), _const_spec((1, d), li), _const_spec((1, d), li),
                  _const_spec((d, dff), li), _const_spec((d, dff), li), _const_spec((dff, d), li),
                  _const_spec((1, d), li), _const_spec((1, d), li)],
        out_specs=out_spec,
        scratch_shapes=[pltpu.VMEM((rb, d), F32), pltpu.VMEM((rb, d), F32)],
        compiler_params=_cparams(1, "arbitrary"),
        name="merge_ffn",
    )(om, zall, oh, zall, zall, zall, h, wm, ws, wh, wo, g1, b1, wg, wu, wd, g2, b2)


def _permute_w_in(w):
    mla_in = MLA_Q_RANK + MLA_KV_RANK + MLA_ROPE
    s5_0, hg_0 = mla_in, mla_in + S5_WIDTH
    g_0 = hg_0 + 2 * HG_QK + 2 * HG_VW
    pad0 = jnp.zeros((w.shape[0], MLA_NOPE), w.dtype)
    pad1 = jnp.zeros((w.shape[0], ZKR_W - MLA_NOPE - MLA_ROPE), w.dtype)
    return jnp.concatenate([w[:, s5_0:hg_0], w[:, 0:MLA_Q_RANK + MLA_KV_RANK], w[:, hg_0:g_0], w[:, g_0:],
                            pad0, w[:, MLA_Q_RANK + MLA_KV_RANK:mla_in], pad1], axis=1).astype(BF16)


def _mla_weights(w_uq, w_ukv):
    rq, rkv = w_uq.shape[0], w_ukv.shape[0]
    zpad = HEAD_PAD - MLA_NOPE - MLA_ROPE
    wq = w_uq.reshape(rq, MLA_HEADS, MLA_NOPE + MLA_ROPE)
    q_nope, q_rope = wq[..., :MLA_NOPE], wq[..., MLA_NOPE:]
    zq = jnp.zeros((rq, MLA_HEADS, zpad), w_uq.dtype)
    wq_p = jnp.concatenate([q_nope, q_rope, zq], axis=-1).reshape(rq, -1)
    wkv = w_ukv.reshape(rkv, MLA_HEADS, MLA_NOPE + MLA_V)
    zk = jnp.zeros((rkv, MLA_HEADS, HEAD_PAD - MLA_NOPE), w_ukv.dtype)
    wk_p = jnp.concatenate([wkv[..., :MLA_NOPE], zk], axis=-1).reshape(rkv, -1)
    wv = wkv[..., MLA_NOPE:].reshape(rkv, -1)
    return [a.astype(BF16) for a in (wq_p, wk_p, wv)]


def _rope_consts():
    half = MLA_ROPE // 2
    inv = ROPE_THETA ** (-(jnp.arange(0, MLA_ROPE, 2, dtype=F32) / MLA_ROPE))
    place = np.zeros((half, HEAD_PAD), np.float32)
    place[np.arange(half), MLA_NOPE + np.arange(half)] = 1.0
    place[np.arange(half), MLA_NOPE + half + np.arange(half)] = 1.0
    base = np.zeros((1, HEAD_PAD), np.float32)
    base[0, :MLA_NOPE] = 1.0
    return inv[:, None], jnp.asarray(place, BF16), jnp.asarray(base)


def _s5_params(lam_re, lam_im, log_dt, b_re, b_im, c_re, c_im):
    lr = jnp.minimum(lam_re.astype(F32), -1e-4)
    li = lam_im.astype(F32)
    dt = jnp.exp(log_dt.astype(F32))[:, None]
    mag = jnp.exp(lr * dt)
    ab_r = mag * jnp.cos(li * dt)
    ab_i = mag * jnp.sin(li * dt)
    den = lr * lr + li * li
    nr = ab_r - 1.0
    coef_r = ((nr * lr + ab_i * li) / den)[..., None]
    coef_i = ((ab_i * lr - nr * li) / den)[..., None]
    bb_r = coef_r * b_re.astype(F32) - coef_i * b_im.astype(F32)
    bb_i = coef_r * b_im.astype(F32) + coef_i * b_re.astype(F32)
    n_slab = S5_WIDTH // S5_SLAB
    gps = S5_SLAB // S5_GROUP
    eye = jnp.eye(gps, dtype=F32)

    def in_mat(bb):
        b4 = bb.reshape(n_slab, gps, S5_STATE, S5_GROUP)
        return jnp.einsum('jgnc,gh->jgchn', b4, eye).reshape(n_slab, S5_SLAB, S5_SLAB_STATE)

    def out_mat(cc):
        c4 = cc.astype(F32).reshape(n_slab, gps, S5_GROUP, S5_STATE)
        return jnp.einsum('jgcn,gh->jgnhc', c4, eye).reshape(n_slab, S5_SLAB_STATE, S5_SLAB)

    bm = jnp.concatenate([in_mat(bb_r), in_mat(bb_i)], axis=2).astype(BF16)
    cm = jnp.concatenate([out_mat(c_re), -out_mat(c_im)], axis=1).astype(BF16)
    ar = ab_r.reshape(n_slab, 1, S5_SLAB_STATE)
    ai = ab_i.reshape(n_slab, 1, S5_SLAB_STATE)
    return bm, cm, ar, ai


def kernel(x, positions, meta_tokens, ln_in_g, ln_in_b, w_in, mla_q_norm, mla_w_uq, mla_kv_norm, mla_w_ukv,
           s5_lam_re, s5_lam_im, s5_log_dt, s5_b_re, s5_b_im, s5_c_re, s5_c_im, s5_d, s5_w_glu,
           hg_lb_logits, hg_out_norm, w_br_mla, w_br_s5, w_br_hg, w_out, ln1_g, ln1_b,
           w_ffn_gate, w_ffn_up, w_ffn_down, ln2_g, ln2_b):
    nb, s, d = x.shape
    depth = w_in.shape[0]
    nblk = (s + N_META) // T_BLK
    alpha = (2 * depth) ** 0.25
    row2 = lambda a: a.astype(F32)[None, :]
    rows3 = lambda a: a.astype(F32)[:, None, :]
    bf16 = lambda a: a.astype(BF16)

    meta_pos = jnp.broadcast_to(jnp.arange(N_META, dtype=jnp.int32)[None, :], (nb, N_META))
    pos = jnp.concatenate([meta_pos, positions.astype(jnp.int32) + N_META], axis=1)
    pos = pos.reshape(nb, nblk, T_BLK).transpose(1, 0, 2).reshape(nblk, 1, nb * T_BLK)
    rope = _rope_consts()
    p_lb = jax.nn.softmax(hg_lb_logits.astype(F32), axis=0)
    lower_bounds = jnp.cumsum(p_lb, axis=0) - p_lb[0]
    tri = jnp.asarray(np.tril(np.ones((HG_CHUNK, HG_CHUNK), np.float32)), BF16)
    lvl = jnp.asarray(_hgrn_level_matrix())
    perm = jnp.asarray(_time_major_perm(nb), BF16)

    front_p = (jax.vmap(_permute_w_in)(w_in), pos, rope, rows3(mla_q_norm), rows3(mla_kv_norm),
               *jax.vmap(_mla_weights)(mla_w_uq, mla_w_ukv), perm,
               *jax.vmap(_s5_params)(s5_lam_re, s5_lam_im, s5_log_dt, s5_b_re, s5_b_im, s5_c_re, s5_c_im),
               rows3(s5_d), bf16(s5_w_glu))
    hgrn_p = (lower_bounds[:, None, :], rows3(hg_out_norm), tri, lvl)
    back_p = (bf16(w_br_mla), bf16(w_br_s5), bf16(w_br_hg), bf16(w_out), rows3(ln1_g), rows3(ln1_b),
              bf16(w_ffn_gate), bf16(w_ffn_up), bf16(w_ffn_down), rows3(ln2_g), rows3(ln2_b))

    h = (x, meta_tokens.astype(x.dtype), row2(ln_in_g), row2(ln_in_b))
    for li in range(depth):
        outs = _front(h, *front_p, nb, li)
        if li == 0:
            h, *outs = outs
        zall, = outs
        o_mla = _attention(zall, nb, nblk)
        o_hg = _hgrn(zall, *hgrn_p, nb, nblk, li)
        h = _back(o_mla, o_hg, zall, h, *back_p, nb, alpha, final=li == depth - 1, li=li)
    return h
```

```python
import functools
import math

import jax
import jax.numpy as jnp
import numpy as np
from jax import lax
from jax.experimental import pallas as pl
from jax.experimental.pallas import tpu as pltpu

F32 = jnp.float32
BF16 = jnp.bfloat16

N_META = 16
MLA_HEADS = 8
MLA_NOPE = 64
MLA_ROPE = 32
MLA_V = 64
MLA_Q_RANK = 256
MLA_KV_RANK = 256
ROPE_THETA = 10000.0
MASK_VALUE = -1e9
LOG2_E = math.log2(math.e)
HEAD_PAD = 128
S5_WIDTH = 512
S5_GROUP = 16
S5_GROUPS = S5_WIDTH // S5_GROUP
S5_STATE = 64
S5_SLAB = 128
S5_SLAB_STATE = (S5_SLAB // S5_GROUP) * S5_STATE
HG_HEADS = 4
HG_KEY = 128
HG_VAL = 128
HG_QK = HG_HEADS * HG_KEY
HG_VW = HG_HEADS * HG_VAL
HG_F_MIN = 1e-6
HG_CHUNK = 128
HG_LEVELS = (128, 64, 32, 16)
HG_BOTTOM = 8
HG_SLOTS = 3
N_BRANCH = 3
T_BLK = 16
ATT_TQ = 256
ATT_HEADS_PER_STEP = 4
ATT_VE = MLA_V + 16
VMEM_LIMIT = 56 * 1024 * 1024

ZA_W = 1024
ZHG_W = 2048
ZG_W = 3072
ZKR_W = 128
Z_W = ZA_W + ZHG_W + ZG_W + ZKR_W
ZALL_WIDTHS = (ZHG_W, ZG_W, MLA_HEADS * HEAD_PAD, MLA_HEADS * HEAD_PAD, MLA_HEADS * MLA_V, S5_WIDTH)
ZALL_OFF = tuple(sum(ZALL_WIDTHS[:i]) for i in range(len(ZALL_WIDTHS)))
ZALL_W = sum(ZALL_WIDTHS)
OFF_HG, OFF_G, OFF_Q, OFF_K, OFF_V, OFF_Y = ZALL_OFF
WIDE_CHUNK = 512
FILL_BEFORE_MLA = 3


def _cparams(n_grid, sem="parallel"):
    return pltpu.CompilerParams(dimension_semantics=(sem,) * n_grid, vmem_limit_bytes=VMEM_LIMIT)


def _const_spec(shape, layer=None):
    nd = len(shape)
    if layer is None:
        return pl.BlockSpec(shape, lambda *_: (0,) * nd, pipeline_mode=pl.Buffered(1))
    return pl.BlockSpec((None,) + tuple(shape), lambda *_: (layer,) + (0,) * nd, pipeline_mode=pl.Buffered(1))


def _seq_spec(nblk, width, col):
    return pl.BlockSpec((nblk, None, T_BLK, width), lambda b, *g: (0, b, 0, col(b, *g)))


def _dot(a, b):
    return jnp.dot(a, b, preferred_element_type=F32)


def _dot_nt(a, b):
    return lax.dot_general(a, b, (((1,), (1,)), ((), ())), preferred_element_type=F32)


def _dot_tn(a, b):
    return lax.dot_general(a, b, (((0,), (0,)), ((), ())), preferred_element_type=F32)


def _sigmoid(x):
    return 1.0 / (1.0 + jnp.exp(-x))


def _layer_norm(x, g, b, eps=1e-5):
    mu = jnp.mean(x, axis=-1, keepdims=True)
    xc = x - mu
    var = jnp.mean(xc * xc, axis=-1, keepdims=True)
    return xc * lax.rsqrt(var + eps) * g + b


def _rms_norm(x, g, eps=1e-6):
    return x * lax.rsqrt(jnp.mean(x * x, axis=-1, keepdims=True) + eps) * g


def _load_rows(ref, blk0, n_blk, cs):
    x = ref[pl.ds(blk0, n_blk), :, cs]
    return x.reshape(n_blk * T_BLK, x.shape[-1])


def _store_rows(ref, blk0, n_blk, cs, x):
    ref[pl.ds(blk0, n_blk), :, cs] = x.reshape(n_blk, T_BLK, x.shape[-1])


def _rope_tables(pos_ref, inv_ref, place_ref, base_ref):
    ang = inv_ref[...] * pos_ref[...].astype(F32)

    def place(t):
        hi = t.astype(BF16)
        lo = (t - hi.astype(F32)).astype(BF16)
        return _dot_tn(hi, place_ref[...]) + _dot_tn(lo, place_ref[...])

    return place(jnp.cos(ang)) + base_ref[...], place(jnp.sin(ang))


def _mla_prep(cq, ckv, kr, cos, sin, qn_ref, kvn_ref, wq_ref, wk_ref, wv_ref, q_ref, k_ref, v_ref):
    scale = (MLA_NOPE + MLA_ROPE) ** -0.5 * LOG2_E
    half = MLA_ROPE // 2
    lane = lax.broadcasted_iota(jnp.int32, (1, HEAD_PAD), 1)
    sin_up = jnp.where((lane >= MLA_NOPE + half) & (lane < MLA_NOPE + MLA_ROPE), sin, 0.0)
    sin_dn = jnp.where((lane >= MLA_NOPE) & (lane < MLA_NOPE + half), -sin, 0.0)

    def rope(x, c, s_up, s_dn):
        return x * c + pltpu.roll(x, half, 1) * s_up + pltpu.roll(x, HEAD_PAD - half, 1) * s_dn

    cqn = _rms_norm(cq, qn_ref[...]).astype(BF16)
    ckvn = _rms_norm(ckv, kvn_ref[...]).astype(BF16)
    k_rope = rope(kr, cos, sin_up, sin_dn)
    cos_q, sin_up_q, sin_dn_q = cos * scale, sin_up * scale, sin_dn * scale
    for h in range(MLA_HEADS):
        cs = slice(h * HEAD_PAD, (h + 1) * HEAD_PAD)
        q_ref[:, cs] = rope(_dot(cqn, wq_ref[:, cs]), cos_q, sin_up_q, sin_dn_q).astype(BF16)
        k_ref[:, cs] = (_dot(ckvn, wk_ref[:, cs]) + k_rope).astype(BF16)
    v_ref[...] = _dot(ckvn, wv_ref[...]).astype(BF16)


def _s5_block(u, perm_ref, bm_ref, cm_ref, ar_ref, ai_ref, d_ref, wg_ref, o_ref,
              xr_ref, xi_ref, buf_ref, y_ref, nb, fillers):
    n_slab = S5_WIDTH // S5_SLAB
    ns = S5_SLAB_STATE
    u = _dot(perm_ref[...], u).astype(BF16)
    for j in range(n_slab):
        buf_ref[...] = _dot(u[:, j * S5_SLAB:(j + 1) * S5_SLAB], bm_ref[j])
        if fillers:
            fillers.pop(0)()
        ar = jnp.broadcast_to(ar_ref[j], (nb, ns))
        ai = jnp.broadcast_to(ai_ref[j], (nb, ns))
        xr = xr_ref[j]
        xi = xi_ref[j]
        for t in range(T_BLK):
            rows = slice(t * nb, (t + 1) * nb)
            nr = ar * xr - ai * xi + buf_ref[rows, 0:ns]
            ni = ar * xi + ai * xr + buf_ref[rows, ns:2 * ns]
            buf_ref[rows, 0:ns] = nr
            buf_ref[rows, ns:2 * ns] = ni
            xr, xi = nr, ni
        xr_ref[j] = xr
        xi_ref[j] = xi
        y_ref[:, j * S5_SLAB:(j + 1) * S5_SLAB] = _dot(buf_ref[...].astype(BF16), cm_ref[j])

    while fillers:
        fillers.pop(0)()
    y = y_ref[...] + d_ref[...] * u.astype(F32)
    y = 0.5 * y * (1.0 + jnp.tanh(math.sqrt(2.0 / math.pi) * (y + 0.044715 * (y * y * y))))
    gate = _sigmoid(_dot(y.astype(BF16), wg_ref[...]))
    out = (y * gate).astype(BF16)
    o_ref[...] = _dot_tn(perm_ref[...], out).astype(BF16)


def _front_kernel(*refs, nb, ln_in):
    if ln_in:
        x_ref, meta_ref, lng_ref, lnb_ref, *refs = refs
    else:
        h_ref, *refs = refs
    (w_ref, pos_ref, inv_ref, place_ref, base_ref, qn_ref, kvn_ref, wq_ref, wk_ref, wv_ref,
     perm_ref, bm_ref, cm_ref, ar_ref, ai_ref, d_ref, wglu_ref, *refs) = refs
    if ln_in:
        hout_ref, *refs = refs
    zall_ref, xr_ref, xi_ref, buf_ref, y_ref = refs
    zhg_ref, zg_ref, q_ref, k_ref, v_ref, ys5_ref = (
        zall_ref.at[:, o:o + w] for o, w in zip(ZALL_OFF, ZALL_WIDTHS))

    @pl.when(pl.program_id(0) == 0)
    def _():
        xr_ref[...] = jnp.zeros_like(xr_ref)
        xi_ref[...] = jnp.zeros_like(xi_ref)

    if ln_in:
        hx = _layer_norm(x_ref[...], lng_ref[...], lnb_ref[...])
        hm = _layer_norm(meta_ref[...], lng_ref[...], lnb_ref[...])
        h = jnp.where(pl.program_id(0) == 0, jnp.broadcast_to(hm[None], hx.shape), hx)
        h = h.reshape(hout_ref.shape)
        hout_ref[...] = h
    else:
        h = h_ref[...]
    x = h.astype(BF16)
    za = _dot(x, w_ref[:, 0:ZA_W])
    zkr = _dot(x, w_ref[:, ZA_W + ZHG_W + ZG_W:Z_W])

    def wide_chunk(o_ref, c, off):
        def run():
            o_ref[:, c:c + WIDE_CHUNK] = _dot(x, w_ref[:, off + c:off + c + WIDE_CHUNK]).astype(BF16)
        return run

    fillers = [wide_chunk(zhg_ref, c, ZA_W) for c in range(0, ZHG_W, WIDE_CHUNK)]
    fillers += [wide_chunk(zg_ref, c, ZA_W + ZHG_W) for c in range(0, ZG_W, WIDE_CHUNK)]
    for _ in range(FILL_BEFORE_MLA):
        fillers.pop(0)()
    cos, sin = _rope_tables(pos_ref, inv_ref, place_ref, base_ref)
    c0 = S5_WIDTH
    _mla_prep(za[:, c0:c0 + MLA_Q_RANK], za[:, c0 + MLA_Q_RANK:c0 + MLA_Q_RANK + MLA_KV_RANK], zkr,
              cos, sin, qn_ref, kvn_ref, wq_ref, wk_ref, wv_ref, q_ref, k_ref, v_ref)
    _s5_block(za[:, 0:S5_WIDTH].astype(BF16), perm_ref, bm_ref, cm_ref, ar_ref, ai_ref, d_ref, wglu_ref, ys5_ref,
              xr_ref, xi_ref, buf_ref, y_ref, nb, fillers)


def _front(src, w, pos, rope, qn, kvn, wq, wk, wv, perm, bm, cm, ar, ai, dskip, wglu, nb, li):
    ln_in = isinstance(src, tuple)
    rb = T_BLK * nb
    if ln_in:
        x, meta, lng, lnb = src
        d = x.shape[-1]
        r = (x.shape[1] + N_META) * nb
        src_specs = [pl.BlockSpec((nb, T_BLK, d), lambda i: (0, jnp.maximum(i - 1, 0), 0)),
                     _const_spec((N_META, d)), _const_spec((1, d)), _const_spec((1, d))]
    else:
        src = (src,)
        r, d = src[0].shape
        src_specs = [pl.BlockSpec((rb, d), lambda i: (i, 0))]
    hw = MLA_HEADS * HEAD_PAD
    vw = MLA_HEADS * MLA_V
    n_slab = S5_WIDTH // S5_SLAB
    row = lambda width: pl.BlockSpec((rb, width), lambda i: (i, 0))
    out_shape = [jax.ShapeDtypeStruct((r, ZALL_W), BF16)]
    out_specs = [row(ZALL_W)]
    if ln_in:
        out_shape.insert(0, jax.ShapeDtypeStruct((r, d), F32))
        out_specs.insert(0, row(d))
    return pl.pallas_call(
        functools.partial(_front_kernel, nb=nb, ln_in=ln_in),
        out_shape=out_shape,
        grid=(r // rb,),
        in_specs=src_specs + [
            _const_spec((d, Z_W), li), pl.BlockSpec((None, 1, rb), lambda i: (i, 0, 0)),
            _const_spec((MLA_ROPE // 2, 1)), _const_spec((MLA_ROPE // 2, HEAD_PAD)), _const_spec((1, HEAD_PAD)),
            _const_spec((1, MLA_Q_RANK), li), _const_spec((1, MLA_KV_RANK), li),
            _const_spec((MLA_Q_RANK, hw), li), _const_spec((MLA_KV_RANK, hw), li), _const_spec((MLA_KV_RANK, vw), li),
            _const_spec((rb, rb)),
            _const_spec((n_slab, S5_SLAB, 2 * S5_SLAB_STATE), li),
            _const_spec((n_slab, 2 * S5_SLAB_STATE, S5_SLAB), li),
            _const_spec((n_slab, 1, S5_SLAB_STATE), li),
            _const_spec((n_slab, 1, S5_SLAB_STATE), li),
            _const_spec((1, S5_WIDTH), li),
            _const_spec((S5_WIDTH, S5_WIDTH), li),
        ],
        out_specs=out_specs,
        scratch_shapes=[
            pltpu.VMEM((n_slab, nb, S5_SLAB_STATE), F32),
            pltpu.VMEM((n_slab, nb, S5_SLAB_STATE), F32),
            pltpu.VMEM((rb, 2 * S5_SLAB_STATE), F32),
            pltpu.VMEM((rb, S5_WIDTH), F32),
        ],
        compiler_params=_cparams(1, "arbitrary"),
        name="front",
    )(*src, w, pos, *rope, qn, kvn, wq, wk, wv, perm, bm, cm, ar, ai, dskip, wglu)


def _attn_kernel(tab_ref, q_ref, k_ref, v_ref, o_ref, vt_ref, qt_ref, m_ref, acc_ref, *bufs, n_heads, tq, nq):
    xs_ref, xm_ref, ps_ref, pm_ref, a_ref = (bufs[2 * i:2 * i + 2] for i in range(5))
    bpq = tq // T_BLK
    vw = n_heads * MLA_V

    def values_t(vb):
        vt = vb.astype(F32).T
        ext = jnp.where(lax.broadcasted_iota(jnp.int32, (ATT_VE - MLA_V, vt.shape[1]), 0) == 0, 1.0, 0.0)
        parts = []
        for h in range(n_heads):
            parts += [vt[h * MLA_V:(h + 1) * MLA_V, :], ext]
        return jnp.concatenate(parts, axis=0).astype(BF16)

    for j in range(nq):
        vt_ref[j] = values_t(_load_rows(v_ref, 1 + j * bpq, bpq, slice(0, vw)))
    v0t = values_t(v_ref[0])

    for i in range(nq):
        qb = _load_rows(q_ref, 1 + i * bpq, bpq, slice(0, n_heads * HEAD_PAD))
        qt_ref[i] = qb.astype(F32).T.astype(BF16)

    def causal(st):
        r = lax.broadcasted_iota(jnp.int32, st.shape, 0)
        c = lax.broadcasted_iota(jnp.int32, st.shape, 1)
        return jnp.where(r <= c, st, MASK_VALUE * LOG2_E)

    heads = range(n_heads)
    qs = [slice(h * HEAD_PAD, (h + 1) * HEAD_PAD) for h in heads]
    vs = [slice(h * ATT_VE, (h + 1) * ATT_VE) for h in heads]
    k0 = [k_ref[0, :, c] for c in qs]

    def normalised(acc):
        return acc[0:MLA_V, :] / acc[MLA_V:MLA_V + 1, :]

    outs = []
    for n in heads:
        st = causal(_dot_nt(k0[n], q_ref[0, :, qs[n]]))
        p = jnp.exp2(st - jnp.max(st, axis=0, keepdims=True))
        outs.append(normalised(_dot(v0t[vs[n], :], p.astype(BF16))))
    o_ref[0, :, :] = jnp.concatenate(outs, axis=0).T.astype(BF16)

    def key_tile(j, n):
        return _load_rows(k_ref, 1 + j * bpq, bpq, qs[n])

    def diag_scores(i, s):
        for n in heads:
            xs_ref[s][n] = _dot(key_tile(i, n), qt_ref[i, qs[n], :])
            xm_ref[s][n] = _dot(k0[n], qt_ref[i, qs[n], :])

    def diag_stats(i, s):
        for n in heads:
            st, sm = causal(xs_ref[s][n]), xm_ref[s][n]
            m = jnp.maximum(jnp.max(st, axis=0, keepdims=True), jnp.max(sm, axis=0, keepdims=True))
            p = jnp.exp2(st - m)
            pm = jnp.exp2(sm - m)
            m_ref[i, n] = m
            ps_ref[s][n] = p.astype(BF16)
            pm_ref[s][n] = pm.astype(BF16)

    def diag_values(i, s):
        for n in heads:
            acc_ref[i, vs[n], :] = _dot(vt_ref[i, vs[n], :], ps_ref[s][n]) + _dot(v0t[vs[n], :], pm_ref[s][n])

    _pipeline3(nq, diag_scores, diag_stats, diag_values)

    def off_scores(f, s):
        i, j = tab_ref[0, f], tab_ref[1, f]
        for n in heads:
            xs_ref[s][n] = _dot(key_tile(j, n), qt_ref[i, qs[n], :])

    def off_stats(f, s):
        i = tab_ref[0, f]
        for n in heads:
            x = xs_ref[s][n]
            m_old = m_ref[i, n]
            m = jnp.maximum(m_old, jnp.max(x, axis=0, keepdims=True))
            a = jnp.exp2(m_old - m)
            p = jnp.exp2(x - m)
            m_ref[i, n] = m
            a_ref[s][n] = a
            ps_ref[s][n] = p.astype(BF16)

    def off_values(f, s):
        i, j = tab_ref[0, f], tab_ref[1, f]
        for n in heads:
            acc_ref[i, vs[n], :] = a_ref[s][n] * acc_ref[i, vs[n], :] + _dot(vt_ref[j, vs[n], :], ps_ref[s][n])

    _pipeline3(nq * (nq - 1) // 2, off_scores, off_stats, off_values)

    for i in range(nq):
        o = jnp.concatenate([normalised(acc_ref[i, vs[n], :]) for n in heads], axis=0)
        _store_rows(o_ref, 1 + i * bpq, bpq, slice(0, vw), o.T.astype(BF16))


def _pipeline3(n, stage1, stage2, stage3, n_slots=2):
    if n == 0:
        return
    stage1(0, 0)
    if n > 1:
        stage1(1, 1 % n_slots)
    stage2(0, 0)

    def step(t, k):
        stage1(t + 2, (k + 2) % n_slots)
        stage2(t + 1, (k + 1) % n_slots)
        stage3(t, k)

    steady = max(n - 2, 0)
    trips = steady // n_slots
    if trips:
        def body(u, carry):
            for k in range(n_slots):
                step(n_slots * u + k, k)
            return carry

        lax.fori_loop(0, trips, body, 0)
    for t in range(trips * n_slots, steady):
        step(t, t % n_slots)
    if n > 1:
        stage3(n - 2, (n - 2) % n_slots)
        stage2(n - 1, (n - 1) % n_slots)
    stage3(n - 1, (n - 1) % n_slots)


def _off_diagonal_order(nq):
    left = [(i, j) for i in range(nq) for j in range(i)]
    order = []
    while left:
        count = {}
        for i, _ in left:
            count[i] = count.get(i, 0) + 1
        ok = [p for p in left if not order or p[0] != order[-1][0]] or left
        pick = max(ok, key=lambda p: (count[p[0]], -p[1]))
        order.append(pick)
        left.remove(pick)
    return order or [(0, 0)]


def _attention(zall, nb, nblk):
    z4 = zall.reshape(nblk, nb, T_BLK, ZALL_W)
    hps = ATT_HEADS_PER_STEP
    n_hp = MLA_HEADS // hps
    qw = hps * HEAD_PAD
    vw = hps * MLA_V
    s = (nblk - 1) * T_BLK
    tq = min(ATT_TQ, s)
    nq = s // tq
    col = lambda b, p: p
    tab = jnp.asarray(np.array(_off_diagonal_order(nq), np.int32).T)
    o = pl.pallas_call(
        functools.partial(_attn_kernel, n_heads=hps, tq=tq, nq=nq),
        out_shape=jax.ShapeDtypeStruct((nblk, nb, T_BLK, MLA_HEADS * MLA_V), BF16),
        grid=(nb, n_hp),
        in_specs=[pl.BlockSpec(memory_space=pltpu.SMEM),
                  _seq_spec(nblk, qw, lambda b, p: OFF_Q // qw + p),
                  _seq_spec(nblk, qw, lambda b, p: OFF_K // qw + p),
                  _seq_spec(nblk, vw, lambda b, p: OFF_V // vw + p)],
        out_specs=_seq_spec(nblk, vw, col),
        scratch_shapes=[pltpu.VMEM((nq, hps * ATT_VE, tq), BF16), pltpu.VMEM((nq, qw, tq), BF16),
                        pltpu.VMEM((nq, hps, 1, tq), F32), pltpu.VMEM((nq, hps * ATT_VE, tq), F32)]
        + 2 * [pltpu.VMEM((hps, tq, tq), F32)] + 2 * [pltpu.VMEM((hps, N_META, tq), F32)]
        + 2 * [pltpu.VMEM((hps, tq, tq), BF16)] + 2 * [pltpu.VMEM((hps, N_META, tq), BF16)]
        + 2 * [pltpu.VMEM((hps, 1, tq), F32)],
        compiler_params=_cparams(2),
        name="mla_attn",
    )(tab, z4, z4, z4)
    return o.reshape(nblk * nb * T_BLK, MLA_HEADS * MLA_V)


def _time_major_perm(nb):
    p = np.zeros((T_BLK * nb, T_BLK * nb), np.float32)
    for b in range(nb):
        for t in range(T_BLK):
            p[t * nb + b, b * T_BLK + t] = 1.0
    return p


def _block_row_bcast(x, m, row):
    t, c = x.shape
    if m == t:
        return jnp.broadcast_to(x[row:row + 1, :], x.shape)
    x3 = x.reshape(t // m, m, c)
    return jnp.broadcast_to(x3[:, row:row + 1, :], x3.shape).reshape(t, c)


def _hgrn_gates(blk0, n_blk, f_ref, lb_ref):
    lb = lb_ref[...]
    zf = _load_rows(f_ref, blk0, n_blk, slice(0, HG_QK)).astype(F32)
    e = jnp.exp(-jnp.abs(zf))
    rcp = 1.0 / (1.0 + e)
    pos = zf >= 0.0
    sig_p = jnp.where(pos, rcp, e * rcp)
    sig_n = jnp.where(pos, e * rcp, rcp)
    f = lb + (1.0 - lb) * sig_p
    log_f = jnp.log2(jnp.maximum(f, HG_F_MIN))
    k = (1.0 - lb) * sig_n
    hi = log_f.astype(BF16)
    lo = (log_f - hi.astype(F32)).astype(BF16)
    return k, hi, lo


def _hgrn_mix(blk0, n_blk, gates, q_ref, v_ref, tri_ref, lvl_ref, st_ref):
    t = n_blk * T_BLK
    cs = slice(0, HG_QK)
    heads = [slice(h * HG_KEY, (h + 1) * HG_KEY) for h in range(HG_HEADS)]
    k, hi, lo = gates
    q = _load_rows(q_ref, blk0, n_blk, cs).astype(F32)
    v = _load_rows(v_ref, blk0, n_blk, cs)
    tri = tri_ref[0:t, 0:t]
    cum = _dot(tri, hi) + _dot(tri, lo)
    lvl = lvl_ref[0:t, 0:t]
    r_idx = lax.broadcasted_iota(jnp.int32, (t, HG_QK), 0)
    c8 = _block_row_bcast(cum, HG_BOTTOM, HG_BOTTOM // 2 - 1)
    qe = (q * jnp.exp2(cum - c8)).astype(BF16)
    ke = (k * jnp.exp2(c8 - cum)).astype(BF16)
    n_lvl = len(HG_LEVELS)
    scores = [jnp.where(lvl == n_lvl, _dot_nt(qe[:, hs], ke[:, hs]), 0.0) for hs in heads]
    for li, m in enumerate(HG_LEVELS):
        if m > t:
            continue
        half = m // 2
        cmid = _block_row_bcast(cum, m, half - 1)
        upper = (r_idx & (m - 1)) >= half
        ex = jnp.exp2(jnp.where(upper, cum - cmid, cmid - cum))
        qe = jnp.where(upper, q * ex, 0.0).astype(BF16)
        ke = jnp.where(upper, 0.0, k * ex).astype(BF16)
        scores = [jnp.where(lvl == li, _dot_nt(qe[:, hs], ke[:, hs]), s) for s, hs in zip(scores, heads)]
    qd = (q * jnp.exp2(cum)).astype(BF16)
    last = cum[t - 1:t, :]
    kd = (k * jnp.exp2(last - cum)).astype(BF16)
    dec = jnp.exp2(last)
    outs = []
    for h, hs in enumerate(heads):
        st = st_ref[h]
        outs.append(_dot(scores[h].astype(BF16), v[:, hs]) + _dot_nt(qd[:, hs], st.astype(BF16)))
        st_ref[h] = st * dec[:, hs] + _dot_tn(v[:, hs], kd[:, hs])
    return tuple(outs)


def _hgrn_out(blk0, n_blk, outs, g_ref, on_ref, o_ref):
    cs = slice(0, HG_VW)
    g = _load_rows(g_ref, blk0, n_blk, cs).astype(F32)
    o = jnp.concatenate([o * lax.rsqrt(jnp.mean(o * o, axis=-1, keepdims=True) + 1e-6) for o in outs], axis=1)
    _store_rows(o_ref, blk0, n_blk, cs, (o * on_ref[...] * (g * _sigmoid(g))).astype(BF16))


def _hgrn_kernel(q_ref, f_ref, v_ref, g_ref, lb_ref, on_ref, tri_ref, lvl_ref, o_ref, st_ref, *bufs, n_chunks):
    kb_ref, hi_ref, lo_ref, ob_ref = (bufs[HG_SLOTS * i:HG_SLOTS * (i + 1)] for i in range(4))
    bpc = HG_CHUNK // T_BLK
    heads = [slice(h * HG_VAL, (h + 1) * HG_VAL) for h in range(HG_HEADS)]
    gates = functools.partial(_hgrn_gates, f_ref=f_ref, lb_ref=lb_ref)
    mix = functools.partial(_hgrn_mix, q_ref=q_ref, v_ref=v_ref, tri_ref=tri_ref, lvl_ref=lvl_ref, st_ref=st_ref)
    out = functools.partial(_hgrn_out, g_ref=g_ref, on_ref=on_ref, o_ref=o_ref)
    st_ref[...] = jnp.zeros_like(st_ref)
    out(0, 1, mix(0, 1, gates(0, 1)))

    def stage_gates(c, s):
        k, hi, lo = gates(1 + c * bpc, bpc)
        kb_ref[s][...], hi_ref[s][...], lo_ref[s][...] = k, hi.astype(F32), lo.astype(F32)

    def stage_mix(c, s):
        outs = mix(1 + c * bpc, bpc, (kb_ref[s][...], hi_ref[s][...].astype(BF16), lo_ref[s][...].astype(BF16)))
        for hs, o in zip(heads, outs):
            ob_ref[s][:, hs] = o

    def stage_out(c, s):
        out(1 + c * bpc, bpc, tuple(ob_ref[s][:, hs] for hs in heads))

    _pipeline3(n_chunks, stage_gates, stage_mix, stage_out, n_slots=HG_SLOTS)


def _hgrn(zall, lb, onorm, tri, lvl, nb, nblk, li):
    z4 = zall.reshape(nblk, nb, T_BLK, ZALL_W)
    col = lambda j: _seq_spec(nblk, HG_QK, lambda b: OFF_HG // HG_QK + j)
    o = pl.pallas_call(
        functools.partial(_hgrn_kernel, n_chunks=(nblk - 1) * T_BLK // HG_CHUNK),
        out_shape=jax.ShapeDtypeStruct((nblk, nb, T_BLK, HG_VW), BF16),
        grid=(nb,),
        in_specs=[col(0), col(1), col(2), col(3),
                  _const_spec((1, HG_QK), li), _const_spec((1, HG_VW), li),
                  _const_spec((HG_CHUNK, HG_CHUNK)), _const_spec((HG_CHUNK, HG_CHUNK))],
        out_specs=_seq_spec(nblk, HG_VW, lambda b: 0),
        scratch_shapes=[pltpu.VMEM((HG_HEADS, HG_VAL, HG_KEY), F32)]
        + 3 * HG_SLOTS * [pltpu.VMEM((HG_CHUNK, HG_QK), F32)]
        + HG_SLOTS * [pltpu.VMEM((HG_CHUNK, HG_VW), F32)],
        compiler_params=_cparams(1),
        name="hgrn2",
    )(z4, z4, z4, z4, lb, onorm, tri, lvl)
    return o.reshape(nblk * nb * T_BLK, HG_VW)


def _hgrn_level_matrix():
    r = np.arange(HG_CHUNK)[:, None]
    c = np.arange(HG_CHUNK)[None, :]
    lvl = np.zeros((HG_CHUNK, HG_CHUNK), np.int32)
    for li, m in enumerate(HG_LEVELS):
        lvl = np.where(r // m == c // m, li, lvl)
    lvl = np.where(r // HG_BOTTOM == c // HG_BOTTOM, len(HG_LEVELS), lvl)
    return np.where(c <= r, lvl, -1).astype(np.int32)


def _back_kernel(om_ref, os_ref, oh_ref, gm_ref, gs_ref, gh_ref, h_ref,
                 wm_ref, ws_ref, wh_ref, wo_ref, g1_ref, b1_ref,
                 wg_ref, wu_ref, wd_ref, g2_ref, b2_ref, o_ref, r1_ref, r2_ref, *, alpha):
    @pl.when(pl.program_id(0) == 0)
    def _():
        r1_ref[...] = jnp.zeros_like(r1_ref)
        r2_ref[...] = jnp.zeros_like(r2_ref)

    r1 = r1_ref[...]
    r2 = r2_ref[...]
    ym = _dot(om_ref[...], wm_ref[...])
    ys = _dot(os_ref[...], ws_ref[...])
    yh = _dot(oh_ref[...], wh_ref[...])
    o_ref[...] = _layer_norm(r2, g2_ref[...], b2_ref[...]).reshape(o_ref.shape)
    h1 = _layer_norm(r1, g1_ref[...], b1_ref[...])
    hb = h1.astype(BF16)
    a = _dot(hb, wg_ref[...])
    u = _dot(hb, wu_ref[...])
    mixed = _sigmoid(gm_ref[...].astype(F32)) * ym
    mixed += _sigmoid(gs_ref[...].astype(F32)) * ys
    mixed += _sigmoid(gh_ref[...].astype(F32)) * yh
    r1_ref[...] = alpha * h_ref[...] + _dot(mixed.astype(BF16), wo_ref[...])
    r2_ref[...] = alpha * h1 + _dot((a * _sigmoid(a) * u).astype(BF16), wd_ref[...])


def _back(om, oh, zall, h, wm, ws, wh, wo, g1, b1, wg, wu, wd, g2, b2, nb, alpha, final, li):
    r, d = h.shape
    bw = om.shape[1]
    dff = wg.shape[-1]
    rb = T_BLK * nb
    skip = 1 if final else 0
    n_blk = r // rb - skip
    lag = 2
    row = lambda width, j=0: pl.BlockSpec((rb, width), lambda i: (jnp.minimum(i, n_blk - 1) + skip, j))
    if final:
        out_shape = jax.ShapeDtypeStruct((nb, n_blk * T_BLK, d), F32)
        out_spec = pl.BlockSpec((nb, T_BLK, d), lambda i: (0, jnp.maximum(i - lag, 0), 0))
    else:
        out_shape = jax.ShapeDtypeStruct((r, d), F32)
        out_spec = pl.BlockSpec((rb, d), lambda i: (jnp.maximum(i - lag, 0), 0))
    return pl.pallas_call(
        functools.partial(_back_kernel, alpha=alpha),
        out_shape=out_shape,
        grid=(n_blk + lag,),
        in_specs=[row(bw), row(bw, OFF_Y // bw), row(bw),
                  row(d, OFF_G // d), row(d, OFF_G // d + 1), row(d, OFF_G // d + 2), row(d),
                  _const_spec((bw, d), li), _const_spec((bw, d), li), _const_spec((bw, d), li),
                  _const_spec((d, d), li), _const_spec((1, d), li), _const_spec((1, d), li),
                  _const_spec((d, dff), li), _const_spec((d, dff), li), _const_spec((dff, d), li),
                  _const_spec((1, d), li), _const_spec((1, d), li)],
        out_specs=out_spec,
        scratch_shapes=[pltpu.VMEM((rb, d), F32), pltpu.VMEM((rb, d), F32)],
        compiler_params=_cparams(1, "arbitrary"),
        name="merge_ffn",
    )(om, zall, oh, zall, zall, zall, h, wm, ws, wh, wo, g1, b1, wg, wu, wd, g2, b2)


def _permute_w_in(w):
    mla_in = MLA_Q_RANK + MLA_KV_RANK + MLA_ROPE
    s5_0, hg_0 = mla_in, mla_in + S5_WIDTH
    g_0 = hg_0 + 2 * HG_QK + 2 * HG_VW
    pad0 = jnp.zeros((w.shape[0], MLA_NOPE), w.dtype)
    pad1 = jnp.zeros((w.shape[0], ZKR_W - MLA_NOPE - MLA_ROPE), w.dtype)
    return jnp.concatenate([w[:, s5_0:hg_0], w[:, 0:MLA_Q_RANK + MLA_KV_RANK], w[:, hg_0:g_0], w[:, g_0:],
                            pad0, w[:, MLA_Q_RANK + MLA_KV_RANK:mla_in], pad1], axis=1).astype(BF16)


def _mla_weights(w_uq, w_ukv):
    rq, rkv = w_uq.shape[0], w_ukv.shape[0]
    zpad = HEAD_PAD - MLA_NOPE - MLA_ROPE
    wq = w_uq.reshape(rq, MLA_HEADS, MLA_NOPE + MLA_ROPE)
    q_nope, q_rope = wq[..., :MLA_NOPE], wq[..., MLA_NOPE:]
    zq = jnp.zeros((rq, MLA_HEADS, zpad), w_uq.dtype)
    wq_p = jnp.concatenate([q_nope, q_rope, zq], axis=-1).reshape(rq, -1)
    wkv = w_ukv.reshape(rkv, MLA_HEADS, MLA_NOPE + MLA_V)
    zk = jnp.zeros((rkv, MLA_HEADS, HEAD_PAD - MLA_NOPE), w_ukv.dtype)
    wk_p = jnp.concatenate([wkv[..., :MLA_NOPE], zk], axis=-1).reshape(rkv, -1)
    wv = wkv[..., MLA_NOPE:].reshape(rkv, -1)
    return [a.astype(BF16) for a in (wq_p, wk_p, wv)]


def _rope_consts():
    half = MLA_ROPE // 2
    inv = ROPE_THETA ** (-(jnp.arange(0, MLA_ROPE, 2, dtype=F32) / MLA_ROPE))
    place = np.zeros((half, HEAD_PAD), np.float32)
    place[np.arange(half), MLA_NOPE + np.arange(half)] = 1.0
    place[np.arange(half), MLA_NOPE + half + np.arange(half)] = 1.0
    base = np.zeros((1, HEAD_PAD), np.float32)
    base[0, :MLA_NOPE] = 1.0
    return inv[:, None], jnp.asarray(place, BF16), jnp.asarray(base)


def _s5_params(lam_re, lam_im, log_dt, b_re, b_im, c_re, c_im):
    lr = jnp.minimum(lam_re.astype(F32), -1e-4)
    li = lam_im.astype(F32)
    dt = jnp.exp(log_dt.astype(F32))[:, None]
    mag = jnp.exp(lr * dt)
    ab_r = mag * jnp.cos(li * dt)
    ab_i = mag * jnp.sin(li * dt)
    den = lr * lr + li * li
    nr = ab_r - 1.0
    coef_r = ((nr * lr + ab_i * li) / den)[..., None]
    coef_i = ((ab_i * lr - nr * li) / den)[..., None]
    bb_r = coef_r * b_re.astype(F32) - coef_i * b_im.astype(F32)
    bb_i = coef_r * b_im.astype(F32) + coef_i * b_re.astype(F32)
    n_slab = S5_WIDTH // S5_SLAB
    gps = S5_SLAB // S5_GROUP
    eye = jnp.eye(gps, dtype=F32)

    def in_mat(bb):
        b4 = bb.reshape(n_slab, gps, S5_STATE, S5_GROUP)
        return jnp.einsum('jgnc,gh->jgchn', b4, eye).reshape(n_slab, S5_SLAB, S5_SLAB_STATE)

    def out_mat(cc):
        c4 = cc.astype(F32).reshape(n_slab, gps, S5_GROUP, S5_STATE)
        return jnp.einsum('jgcn,gh->jgnhc', c4, eye).reshape(n_slab, S5_SLAB_STATE, S5_SLAB)

    bm = jnp.concatenate([in_mat(bb_r), in_mat(bb_i)], axis=2).astype(BF16)
    cm = jnp.concatenate([out_mat(c_re), -out_mat(c_im)], axis=1).astype(BF16)
    ar = ab_r.reshape(n_slab, 1, S5_SLAB_STATE)
    ai = ab_i.reshape(n_slab, 1, S5_SLAB_STATE)
    return bm, cm, ar, ai


def kernel(x, positions, meta_tokens, ln_in_g, ln_in_b, w_in, mla_q_norm, mla_w_uq, mla_kv_norm, mla_w_ukv,
           s5_lam_re, s5_lam_im, s5_log_dt, s5_b_re, s5_b_im, s5_c_re, s5_c_im, s5_d, s5_w_glu,
           hg_lb_logits, hg_out_norm, w_br_mla, w_br_s5, w_br_hg, w_out, ln1_g, ln1_b,
           w_ffn_gate, w_ffn_up, w_ffn_down, ln2_g, ln2_b):
    nb, s, d = x.shape
    depth = w_in.shape[0]
    nblk = (s + N_META) // T_BLK
    alpha = (2 * depth) ** 0.25
    row2 = lambda a: a.astype(F32)[None, :]
    rows3 = lambda a: a.astype(F32)[:, None, :]
    bf16 = lambda a: a.astype(BF16)

    meta_pos = jnp.broadcast_to(jnp.arange(N_META, dtype=jnp.int32)[None, :], (nb, N_META))
    pos = jnp.concatenate([meta_pos, positions.astype(jnp.int32) + N_META], axis=1)
    pos = pos.reshape(nb, nblk, T_BLK).transpose(1, 0, 2).reshape(nblk, 1, nb * T_BLK)
    rope = _rope_consts()
    p_lb = jax.nn.softmax(hg_lb_logits.astype(F32), axis=0)
    lower_bounds = jnp.cumsum(p_lb, axis=0) - p_lb[0]
    tri = jnp.asarray(np.tril(np.ones((HG_CHUNK, HG_CHUNK), np.float32)), BF16)
    lvl = jnp.asarray(_hgrn_level_matrix())
    perm = jnp.asarray(_time_major_perm(nb), BF16)

    front_p = (jax.vmap(_permute_w_in)(w_in), pos, rope, rows3(mla_q_norm), rows3(mla_kv_norm),
               *jax.vmap(_mla_weights)(mla_w_uq, mla_w_ukv), perm,
               *jax.vmap(_s5_params)(s5_lam_re, s5_lam_im, s5_log_dt, s5_b_re, s5_b_im, s5_c_re, s5_c_im),
               rows3(s5_d), bf16(s5_w_glu))
    hgrn_p = (lower_bounds[:, None, :], rows3(hg_out_norm), tri, lvl)
    back_p = (bf16(w_br_mla), bf16(w_br_s5), bf16(w_br_hg), bf16(w_out), rows3(ln1_g), rows3(ln1_b),
              bf16(w_ffn_gate), bf16(w_ffn_up), bf16(w_ffn_down), rows3(ln2_g), rows3(ln2_b))

    h = (x, meta_tokens.astype(x.dtype), row2(ln_in_g), row2(ln_in_b))
    for li in range(depth):
        outs = _front(h, *front_p, nb, li)
        if li == 0:
            h, *outs = outs
        zall, = outs
        o_mla = _attention(zall, nb, nblk)
        o_hg = _hgrn(zall, *hgrn_p, nb, nblk, li)
        h = _back(o_mla, o_hg, zall, h, *back_p, nb, alpha, final=li == depth - 1, li=li)
    return h
```

```python
import functools
import math

import jax
import jax.numpy as jnp
import numpy as np
from jax import lax
from jax.experimental import pallas as pl
from jax.experimental.pallas import tpu as pltpu

F32 = jnp.float32
BF16 = jnp.bfloat16

N_META = 16
MLA_HEADS = 8
MLA_NOPE = 64
MLA_ROPE = 32
MLA_V = 64
MLA_Q_RANK = 256
MLA_KV_RANK = 256
ROPE_THETA = 10000.0
MASK_VALUE = -1e9
LOG2_E = math.log2(math.e)
HEAD_PAD = 128
S5_WIDTH = 512
S5_GROUP = 16
S5_GROUPS = S5_WIDTH // S5_GROUP
S5_STATE = 64
S5_SLAB = 128
S5_SLAB_STATE = (S5_SLAB // S5_GROUP) * S5_STATE
HG_HEADS = 4
HG_KEY = 128
HG_VAL = 128
HG_QK = HG_HEADS * HG_KEY
HG_VW = HG_HEADS * HG_VAL
HG_F_MIN = 1e-6
HG_CHUNK = 128
HG_LEVELS = (128, 64, 32, 16)
HG_BOTTOM = 8
HG_SLOTS = 3
N_BRANCH = 3
T_BLK = 16
ATT_TQ = 256
ATT_HEADS_PER_STEP = 4
ATT_VE = MLA_V + 16
VMEM_LIMIT = 56 * 1024 * 1024

ZA_W = 1024
ZHG_W = 2048
ZG_W = 3072
ZKR_W = 128
Z_W = ZA_W + ZHG_W + ZG_W + ZKR_W
ZALL_WIDTHS = (ZHG_W, ZG_W, MLA_HEADS * HEAD_PAD, MLA_HEADS * HEAD_PAD, MLA_HEADS * MLA_V, S5_WIDTH)
ZALL_OFF = tuple(sum(ZALL_WIDTHS[:i]) for i in range(len(ZALL_WIDTHS)))
ZALL_W = sum(ZALL_WIDTHS)
OFF_HG, OFF_G, OFF_Q, OFF_K, OFF_V, OFF_Y = ZALL_OFF
WIDE_CHUNK = 512
FILL_BEFORE_MLA = 3


def _cparams(n_grid, sem="parallel"):
    return pltpu.CompilerParams(dimension_semantics=(sem,) * n_grid, vmem_limit_bytes=VMEM_LIMIT)


def _const_spec(shape, layer=None):
    nd = len(shape)
    if layer is None:
        return pl.BlockSpec(shape, lambda *_: (0,) * nd, pipeline_mode=pl.Buffered(1))
    return pl.BlockSpec((None,) + tuple(shape), lambda *_: (layer,) + (0,) * nd, pipeline_mode=pl.Buffered(1))


def _seq_spec(nblk, width, col):
    return pl.BlockSpec((nblk, None, T_BLK, width), lambda b, *g: (0, b, 0, col(b, *g)))


def _dot(a, b):
    return jnp.dot(a, b, preferred_element_type=F32)


def _dot_nt(a, b):
    return lax.dot_general(a, b, (((1,), (1,)), ((), ())), preferred_element_type=F32)


def _dot_tn(a, b):
    return lax.dot_general(a, b, (((0,), (0,)), ((), ())), preferred_element_type=F32)


def _sigmoid(x):
    return 1.0 / (1.0 + jnp.exp(-x))


def _layer_norm(x, g, b, eps=1e-5):
    mu = jnp.mean(x, axis=-1, keepdims=True)
    xc = x - mu
    var = jnp.mean(xc * xc, axis=-1, keepdims=True)
    return xc * lax.rsqrt(var + eps) * g + b


def _rms_norm(x, g, eps=1e-6):
    return x * lax.rsqrt(jnp.mean(x * x, axis=-1, keepdims=True) + eps) * g


def _load_rows(ref, blk0, n_blk, cs):
    x = ref[pl.ds(blk0, n_blk), :, cs]
    return x.reshape(n_blk * T_BLK, x.shape[-1])


def _store_rows(ref, blk0, n_blk, cs, x):
    ref[pl.ds(blk0, n_blk), :, cs] = x.reshape(n_blk, T_BLK, x.shape[-1])


def _rope_tables(pos_ref, inv_ref, place_ref, base_ref):
    ang = inv_ref[...] * pos_ref[...].astype(F32)

    def place(t):
        hi = t.astype(BF16)
        lo = (t - hi.astype(F32)).astype(BF16)
        return _dot_tn(hi, place_ref[...]) + _dot_tn(lo, place_ref[...])

    return place(jnp.cos(ang)) + base_ref[...], place(jnp.sin(ang))


def _mla_prep(cq, ckv, kr, cos, sin, qn_ref, kvn_ref, wq_ref, wk_ref, wv_ref, q_ref, k_ref, v_ref):
    scale = (MLA_NOPE + MLA_ROPE) ** -0.5 * LOG2_E
    half = MLA_ROPE // 2
    lane = lax.broadcasted_iota(jnp.int32, (1, HEAD_PAD), 1)
    sin_up = jnp.where((lane >= MLA_NOPE + half) & (lane < MLA_NOPE + MLA_ROPE), sin, 0.0)
    sin_dn = jnp.where((lane >= MLA_NOPE) & (lane < MLA_NOPE + half), -sin, 0.0)

    def rope(x, c, s_up, s_dn):
        return x * c + pltpu.roll(x, half, 1) * s_up + pltpu.roll(x, HEAD_PAD - half, 1) * s_dn

    cqn = _rms_norm(cq, qn_ref[...]).astype(BF16)
    ckvn = _rms_norm(ckv, kvn_ref[...]).astype(BF16)
    k_rope = rope(kr, cos, sin_up, sin_dn)
    cos_q, sin_up_q, sin_dn_q = cos * scale, sin_up * scale, sin_dn * scale
    for h in range(MLA_HEADS):
        cs = slice(h * HEAD_PAD, (h + 1) * HEAD_PAD)
        q_ref[:, cs] = rope(_dot(cqn, wq_ref[:, cs]), cos_q, sin_up_q, sin_dn_q).astype(BF16)
        k_ref[:, cs] = (_dot(ckvn, wk_ref[:, cs]) + k_rope).astype(BF16)
    v_ref[...] = _dot(ckvn, wv_ref[...]).astype(BF16)


def _s5_block(u, perm_ref, bm_ref, cm_ref, ar_ref, ai_ref, d_ref, wg_ref, o_ref,
              xr_ref, xi_ref, buf_ref, y_ref, nb, fillers):
    n_slab = S5_WIDTH // S5_SLAB
    ns = S5_SLAB_STATE
    u = _dot(perm_ref[...], u).astype(BF16)
    for j in range(n_slab):
        buf_ref[...] = _dot(u[:, j * S5_SLAB:(j + 1) * S5_SLAB], bm_ref[j])
        if fillers:
            fillers.pop(0)()
        ar = jnp.broadcast_to(ar_ref[j], (nb, ns))
        ai = jnp.broadcast_to(ai_ref[j], (nb, ns))
        xr = xr_ref[j]
        xi = xi_ref[j]
        for t in range(T_BLK):
            rows = slice(t * nb, (t + 1) * nb)
            nr = ar * xr - ai * xi + buf_ref[rows, 0:ns]
            ni = ar * xi + ai * xr + buf_ref[rows, ns:2 * ns]
            buf_ref[rows, 0:ns] = nr
            buf_ref[rows, ns:2 * ns] = ni
            xr, xi = nr, ni
        xr_ref[j] = xr
        xi_ref[j] = xi
        y_ref[:, j * S5_SLAB:(j + 1) * S5_SLAB] = _dot(buf_ref[...].astype(BF16), cm_ref[j])

    while fillers:
        fillers.pop(0)()
    y = y_ref[...] + d_ref[...] * u.astype(F32)
    y = 0.5 * y * (1.0 + jnp.tanh(math.sqrt(2.0 / math.pi) * (y + 0.044715 * (y * y * y))))
    gate = _sigmoid(_dot(y.astype(BF16), wg_ref[...]))
    out = (y * gate).astype(BF16)
    o_ref[...] = _dot_tn(perm_ref[...], out).astype(BF16)


def _front_kernel(*refs, nb, ln_in):
    if ln_in:
        x_ref, meta_ref, lng_ref, lnb_ref, *refs = refs
    else:
        h_ref, *refs = refs
    (w_ref, pos_ref, inv_ref, place_ref, base_ref, qn_ref, kvn_ref, wq_ref, wk_ref, wv_ref,
     perm_ref, bm_ref, cm_ref, ar_ref, ai_ref, d_ref, wglu_ref, *refs) = refs
    if ln_in:
        hout_ref, *refs = refs
    zall_ref, xr_ref, xi_ref, buf_ref, y_ref = refs
    zhg_ref, zg_ref, q_ref, k_ref, v_ref, ys5_ref = (
        zall_ref.at[:, o:o + w] for o, w in zip(ZALL_OFF, ZALL_WIDTHS))

    @pl.when(pl.program_id(0) == 0)
    def _():
        xr_ref[...] = jnp.zeros_like(xr_ref)
        xi_ref[...] = jnp.zeros_like(xi_ref)

    if ln_in:
        hx = _layer_norm(x_ref[...], lng_ref[...], lnb_ref[...])
        hm = _layer_norm(meta_ref[...], lng_ref[...], lnb_ref[...])
        h = jnp.where(pl.program_id(0) == 0, jnp.broadcast_to(hm[None], hx.shape), hx)
        h = h.reshape(hout_ref.shape)
        hout_ref[...] = h
    else:
        h = h_ref[...]
    x = h.astype(BF16)
    za = _dot(x, w_ref[:, 0:ZA_W])
    zkr = _dot(x, w_ref[:, ZA_W + ZHG_W + ZG_W:Z_W])

    def wide_chunk(o_ref, c, off):
        def run():
            o_ref[:, c:c + WIDE_CHUNK] = _dot(x, w_ref[:, off + c:off + c + WIDE_CHUNK]).astype(BF16)
        return run

    fillers = [wide_chunk(zhg_ref, c, ZA_W) for c in range(0, ZHG_W, WIDE_CHUNK)]
    fillers += [wide_chunk(zg_ref, c, ZA_W + ZHG_W) for c in range(0, ZG_W, WIDE_CHUNK)]
    for _ in range(FILL_BEFORE_MLA):
        fillers.pop(0)()
    cos, sin = _rope_tables(pos_ref, inv_ref, place_ref, base_ref)
    c0 = S5_WIDTH
    _mla_prep(za[:, c0:c0 + MLA_Q_RANK], za[:, c0 + MLA_Q_RANK:c0 + MLA_Q_RANK + MLA_KV_RANK], zkr,
              cos, sin, qn_ref, kvn_ref, wq_ref, wk_ref, wv_ref, q_ref, k_ref, v_ref)
    _s5_block(za[:, 0:S5_WIDTH].astype(BF16), perm_ref, bm_ref, cm_ref, ar_ref, ai_ref, d_ref, wglu_ref, ys5_ref,
              xr_ref, xi_ref, buf_ref, y_ref, nb, fillers)


def _front(src, w, pos, rope, qn, kvn, wq, wk, wv, perm, bm, cm, ar, ai, dskip, wglu, nb, li):
    ln_in = isinstance(src, tuple)
    rb = T_BLK * nb
    if ln_in:
        x, meta, lng, lnb = src
        d = x.shape[-1]
        r = (x.shape[1] + N_META) * nb
        src_specs = [pl.BlockSpec((nb, T_BLK, d), lambda i: (0, jnp.maximum(i - 1, 0), 0)),
                     _const_spec((N_META, d)), _const_spec((1, d)), _const_spec((1, d))]
    else:
        src = (src,)
        r, d = src[0].shape
        src_specs = [pl.BlockSpec((rb, d), lambda i: (i, 0))]
    hw = MLA_HEADS * HEAD_PAD
    vw = MLA_HEADS * MLA_V
    n_slab = S5_WIDTH // S5_SLAB
    row = lambda width: pl.BlockSpec((rb, width), lambda i: (i, 0))
    out_shape = [jax.ShapeDtypeStruct((r, ZALL_W), BF16)]
    out_specs = [row(ZALL_W)]
    if ln_in:
        out_shape.insert(0, jax.ShapeDtypeStruct((r, d), F32))
        out_specs.insert(0, row(d))
    return pl.pallas_call(
        functools.partial(_front_kernel, nb=nb, ln_in=ln_in),
        out_shape=out_shape,
        grid=(r // rb,),
        in_specs=src_specs + [
            _const_spec((d, Z_W), li), pl.BlockSpec((None, 1, rb), lambda i: (i, 0, 0)),
            _const_spec((MLA_ROPE // 2, 1)), _const_spec((MLA_ROPE // 2, HEAD_PAD)), _const_spec((1, HEAD_PAD)),
            _const_spec((1, MLA_Q_RANK), li), _const_spec((1, MLA_KV_RANK), li),
            _const_spec((MLA_Q_RANK, hw), li), _const_spec((MLA_KV_RANK, hw), li), _const_spec((MLA_KV_RANK, vw), li),
            _const_spec((rb, rb)),
            _const_spec((n_slab, S5_SLAB, 2 * S5_SLAB_STATE), li),
            _const_spec((n_slab, 2 * S5_SLAB_STATE, S5_SLAB), li),
            _const_spec((n_slab, 1, S5_SLAB_STATE), li),
            _const_spec((n_slab, 1, S5_SLAB_STATE), li),
            _const_spec((1, S5_WIDTH), li),
            _const_spec((S5_WIDTH, S5_WIDTH), li),
        ],
        out_specs=out_specs,
        scratch_shapes=[
            pltpu.VMEM((n_slab, nb, S5_SLAB_STATE), F32),
            pltpu.VMEM((n_slab, nb, S5_SLAB_STATE), F32),
            pltpu.VMEM((rb, 2 * S5_SLAB_STATE), F32),
            pltpu.VMEM((rb, S5_WIDTH), F32),
        ],
        compiler_params=_cparams(1, "arbitrary"),
        name="front",
    )(*src, w, pos, *rope, qn, kvn, wq, wk, wv, perm, bm, cm, ar, ai, dskip, wglu)


def _attn_kernel(tab_ref, q_ref, k_ref, v_ref, o_ref, vt_ref, qt_ref, m_ref, acc_ref, *bufs, n_heads, tq, nq):
    xs_ref, xm_ref, ps_ref, pm_ref, a_ref = (bufs[2 * i:2 * i + 2] for i in range(5))
    bpq = tq // T_BLK
    vw = n_heads * MLA_V

    def values_t(vb):
        vt = vb.T
        rows = lax.broadcasted_iota(jnp.int32, (ATT_VE - MLA_V, vt.shape[1]), 0)
        ext = jnp.where(rows == 0, 1.0, 0.0).astype(BF16)
        parts = []
        for h in range(n_heads):
            parts += [vt[h * MLA_V:(h + 1) * MLA_V, :], ext]
        return jnp.concatenate(parts, axis=0)

    for j in range(nq):
        vt_ref[j] = values_t(_load_rows(v_ref, 1 + j * bpq, bpq, slice(0, vw)))
    v0t = values_t(v_ref[0])

    for i in range(nq):
        qb = _load_rows(q_ref, 1 + i * bpq, bpq, slice(0, n_heads * HEAD_PAD))
        qt_ref[i] = qb.T

    def causal(st):
        r = lax.broadcasted_iota(jnp.int32, st.shape, 0)
        c = lax.broadcasted_iota(jnp.int32, st.shape, 1)
        return jnp.where(r <= c, st, MASK_VALUE * LOG2_E)

    heads = range(n_heads)
    qs = [slice(h * HEAD_PAD, (h + 1) * HEAD_PAD) for h in heads]
    vs = [slice(h * ATT_VE, (h + 1) * ATT_VE) for h in heads]
    k0 = [k_ref[0, :, c] for c in qs]

    def normalised(acc):
        return acc[0:MLA_V, :] / acc[MLA_V:MLA_V + 1, :]

    outs = []
    for n in heads:
        st = causal(_dot_nt(k0[n], q_ref[0, :, qs[n]]))
        p = jnp.exp2(st - jnp.max(st, axis=0, keepdims=True))
        outs.append(normalised(_dot(v0t[vs[n], :], p.astype(BF16))))
    o_ref[0, :, :] = jnp.concatenate(outs, axis=0).T.astype(BF16)

    def key_tile(j, n):
        return _load_rows(k_ref, 1 + j * bpq, bpq, qs[n])

    def diag_scores(i, s):
        for n in heads:
            xs_ref[s][n] = _dot(key_tile(i, n), qt_ref[i, qs[n], :])
            xm_ref[s][n] = _dot(k0[n], qt_ref[i, qs[n], :])

    def diag_stats(i, s):
        for n in heads:
            st, sm = causal(xs_ref[s][n]), xm_ref[s][n]
            m = jnp.maximum(jnp.max(st, axis=0, keepdims=True), jnp.max(sm, axis=0, keepdims=True))
            p = jnp.exp2(st - m)
            pm = jnp.exp2(sm - m)
            m_ref[i, n] = m
            ps_ref[s][n] = p.astype(BF16)
            pm_ref[s][n] = pm.astype(BF16)

    def diag_values(i, s):
        for n in heads:
            acc_ref[i, vs[n], :] = _dot(vt_ref[i, vs[n], :], ps_ref[s][n]) + _dot(v0t[vs[n], :], pm_ref[s][n])

    _pipeline3(nq, diag_scores, diag_stats, diag_values)

    def off_scores(f, s):
        i, j = tab_ref[0, f], tab_ref[1, f]
        for n in heads:
            xs_ref[s][n] = _dot(key_tile(j, n), qt_ref[i, qs[n], :])

    def off_stats(f, s):
        i = tab_ref[0, f]
        for n in heads:
            x = xs_ref[s][n]
            m_old = m_ref[i, n]
            m = jnp.maximum(m_old, jnp.max(x, axis=0, keepdims=True))
            a = jnp.exp2(m_old - m)
            p = jnp.exp2(x - m)
            m_ref[i, n] = m
            a_ref[s][n] = a
            ps_ref[s][n] = p.astype(BF16)

    def off_values(f, s):
        i, j = tab_ref[0, f], tab_ref[1, f]
        for n in heads:
            acc_ref[i, vs[n], :] = a_ref[s][n] * acc_ref[i, vs[n], :] + _dot(vt_ref[j, vs[n], :], ps_ref[s][n])

    _pipeline3(nq * (nq - 1) // 2, off_scores, off_stats, off_values)

    for i in range(nq):
        o = jnp.concatenate([normalised(acc_ref[i, vs[n], :]) for n in heads], axis=0)
        _store_rows(o_ref, 1 + i * bpq, bpq, slice(0, vw), o.astype(BF16).T)


def _pipeline3(n, stage1, stage2, stage3, n_slots=2):
    if n == 0:
        return
    stage1(0, 0)
    if n > 1:
        stage1(1, 1 % n_slots)
    stage2(0, 0)

    def step(t, k):
        stage1(t + 2, (k + 2) % n_slots)
        stage2(t + 1, (k + 1) % n_slots)
        stage3(t, k)

    steady = max(n - 2, 0)
    trips = steady // n_slots
    if trips:
        def body(u, carry):
            for k in range(n_slots):
                step(n_slots * u + k, k)
            return carry

        lax.fori_loop(0, trips, body, 0)
    for t in range(trips * n_slots, steady):
        step(t, t % n_slots)
    if n > 1:
        stage3(n - 2, (n - 2) % n_slots)
        stage2(n - 1, (n - 1) % n_slots)
    stage3(n - 1, (n - 1) % n_slots)


def _off_diagonal_order(nq):
    left = [(i, j) for i in range(nq) for j in range(i)]
    order = []
    while left:
        count = {}
        for i, _ in left:
            count[i] = count.get(i, 0) + 1
        ok = [p for p in left if not order or p[0] != order[-1][0]] or left
        pick = max(ok, key=lambda p: (count[p[0]], -p[1]))
        order.append(pick)
        left.remove(pick)
    return order or [(0, 0)]


def _attention(zall, nb, nblk):
    z4 = zall.reshape(nblk, nb, T_BLK, ZALL_W)
    hps = ATT_HEADS_PER_STEP
    n_hp = MLA_HEADS // hps
    qw = hps * HEAD_PAD
    vw = hps * MLA_V
    s = (nblk - 1) * T_BLK
    tq = min(ATT_TQ, s)
    nq = s // tq
    col = lambda b, p: p
    tab = jnp.asarray(np.array(_off_diagonal_order(nq), np.int32).T)
    o = pl.pallas_call(
        functools.partial(_attn_kernel, n_heads=hps, tq=tq, nq=nq),
        out_shape=jax.ShapeDtypeStruct((nblk, nb, T_BLK, MLA_HEADS * MLA_V), BF16),
        grid=(nb, n_hp),
        in_specs=[pl.BlockSpec(memory_space=pltpu.SMEM),
                  _seq_spec(nblk, qw, lambda b, p: OFF_Q // qw + p),
                  _seq_spec(nblk, qw, lambda b, p: OFF_K // qw + p),
                  _seq_spec(nblk, vw, lambda b, p: OFF_V // vw + p)],
        out_specs=_seq_spec(nblk, vw, col),
        scratch_shapes=[pltpu.VMEM((nq, hps * ATT_VE, tq), BF16), pltpu.VMEM((nq, qw, tq), BF16),
                        pltpu.VMEM((nq, hps, 1, tq), F32), pltpu.VMEM((nq, hps * ATT_VE, tq), F32)]
        + 2 * [pltpu.VMEM((hps, tq, tq), F32)] + 2 * [pltpu.VMEM((hps, N_META, tq), F32)]
        + 2 * [pltpu.VMEM((hps, tq, tq), BF16)] + 2 * [pltpu.VMEM((hps, N_META, tq), BF16)]
        + 2 * [pltpu.VMEM((hps, 1, tq), F32)],
        compiler_params=_cparams(2),
        name="mla_attn",
    )(tab, z4, z4, z4)
    return o.reshape(nblk * nb * T_BLK, MLA_HEADS * MLA_V)


def _time_major_perm(nb):
    p = np.zeros((T_BLK * nb, T_BLK * nb), np.float32)
    for b in range(nb):
        for t in range(T_BLK):
            p[t * nb + b, b * T_BLK + t] = 1.0
    return p


def _block_row_bcast(x, m, row):
    t, c = x.shape
    if m == t:
        return jnp.broadcast_to(x[row:row + 1, :], x.shape)
    x3 = x.reshape(t // m, m, c)
    return jnp.broadcast_to(x3[:, row:row + 1, :], x3.shape).reshape(t, c)


def _hgrn_gates(blk0, n_blk, f_ref, lb_ref):
    lb = lb_ref[...]
    zf = _load_rows(f_ref, blk0, n_blk, slice(0, HG_QK)).astype(F32)
    e = jnp.exp(-jnp.abs(zf))
    rcp = 1.0 / (1.0 + e)
    pos = zf >= 0.0
    sig_p = jnp.where(pos, rcp, e * rcp)
    sig_n = jnp.where(pos, e * rcp, rcp)
    f = lb + (1.0 - lb) * sig_p
    log_f = jnp.log2(jnp.maximum(f, HG_F_MIN))
    k = (1.0 - lb) * sig_n
    hi = log_f.astype(BF16)
    lo = (log_f - hi.astype(F32)).astype(BF16)
    return k, hi, lo


def _hgrn_mix(blk0, n_blk, gates, q_ref, v_ref, tri_ref, lvl_ref, st_ref):
    t = n_blk * T_BLK
    cs = slice(0, HG_QK)
    heads = [slice(h * HG_KEY, (h + 1) * HG_KEY) for h in range(HG_HEADS)]
    k, hi, lo = gates
    q = _load_rows(q_ref, blk0, n_blk, cs).astype(F32)
    v = _load_rows(v_ref, blk0, n_blk, cs)
    tri = tri_ref[0:t, 0:t]
    cum = _dot(tri, hi) + _dot(tri, lo)
    lvl = lvl_ref[0:t, 0:t]
    r_idx = lax.broadcasted_iota(jnp.int32, (t, HG_QK), 0)
    c8 = _block_row_bcast(cum, HG_BOTTOM, HG_BOTTOM // 2 - 1)
    qe = (q * jnp.exp2(cum - c8)).astype(BF16)
    ke = (k * jnp.exp2(c8 - cum)).astype(BF16)
    n_lvl = len(HG_LEVELS)
    scores = [jnp.where(lvl == n_lvl, _dot_nt(qe[:, hs], ke[:, hs]), 0.0) for hs in heads]
    for li, m in enumerate(HG_LEVELS):
        if m > t:
            continue
        half = m // 2
        cmid = _block_row_bcast(cum, m, half - 1)
        upper = (r_idx & (m - 1)) >= half
        ex = jnp.exp2(jnp.where(upper, cum - cmid, cmid - cum))
        qe = jnp.where(upper, q * ex, 0.0).astype(BF16)
        ke = jnp.where(upper, 0.0, k * ex).astype(BF16)
        scores = [jnp.where(lvl == li, _dot_nt(qe[:, hs], ke[:, hs]), s) for s, hs in zip(scores, heads)]
    qd = (q * jnp.exp2(cum)).astype(BF16)
    last = cum[t - 1:t, :]
    kd = (k * jnp.exp2(last - cum)).astype(BF16)
    dec = jnp.exp2(last)
    outs = []
    for h, hs in enumerate(heads):
        st = st_ref[h]
        outs.append(_dot(scores[h].astype(BF16), v[:, hs]) + _dot_nt(qd[:, hs], st.astype(BF16)))
        st_ref[h] = st * dec[:, hs] + _dot_tn(v[:, hs], kd[:, hs])
    return tuple(outs)


def _hgrn_out(blk0, n_blk, outs, g_ref, on_ref, o_ref):
    cs = slice(0, HG_VW)
    g = _load_rows(g_ref, blk0, n_blk, cs).astype(F32)
    o = jnp.concatenate([o * lax.rsqrt(jnp.mean(o * o, axis=-1, keepdims=True) + 1e-6) for o in outs], axis=1)
    _store_rows(o_ref, blk0, n_blk, cs, (o * on_ref[...] * (g * _sigmoid(g))).astype(BF16))


def _hgrn_kernel(q_ref, f_ref, v_ref, g_ref, lb_ref, on_ref, tri_ref, lvl_ref, o_ref, st_ref, *bufs, n_chunks):
    kb_ref, hi_ref, lo_ref, ob_ref = (bufs[HG_SLOTS * i:HG_SLOTS * (i + 1)] for i in range(4))
    bpc = HG_CHUNK // T_BLK
    heads = [slice(h * HG_VAL, (h + 1) * HG_VAL) for h in range(HG_HEADS)]
    gates = functools.partial(_hgrn_gates, f_ref=f_ref, lb_ref=lb_ref)
    mix = functools.partial(_hgrn_mix, q_ref=q_ref, v_ref=v_ref, tri_ref=tri_ref, lvl_ref=lvl_ref, st_ref=st_ref)
    out = functools.partial(_hgrn_out, g_ref=g_ref, on_ref=on_ref, o_ref=o_ref)
    st_ref[...] = jnp.zeros_like(st_ref)
    out(0, 1, mix(0, 1, gates(0, 1)))

    def stage_gates(c, s):
        k, hi, lo = gates(1 + c * bpc, bpc)
        kb_ref[s][...], hi_ref[s][...], lo_ref[s][...] = k, hi.astype(F32), lo.astype(F32)

    def stage_mix(c, s):
        outs = mix(1 + c * bpc, bpc, (kb_ref[s][...], hi_ref[s][...].astype(BF16), lo_ref[s][...].astype(BF16)))
        for hs, o in zip(heads, outs):
            ob_ref[s][:, hs] = o

    def stage_out(c, s):
        out(1 + c * bpc, bpc, tuple(ob_ref[s][:, hs] for hs in heads))

    _pipeline3(n_chunks, stage_gates, stage_mix, stage_out, n_slots=HG_SLOTS)


def _hgrn(zall, lb, onorm, tri, lvl, nb, nblk, li):
    z4 = zall.reshape(nblk, nb, T_BLK, ZALL_W)
    col = lambda j: _seq_spec(nblk, HG_QK, lambda b: OFF_HG // HG_QK + j)
    o = pl.pallas_call(
        functools.partial(_hgrn_kernel, n_chunks=(nblk - 1) * T_BLK // HG_CHUNK),
        out_shape=jax.ShapeDtypeStruct((nblk, nb, T_BLK, HG_VW), BF16),
        grid=(nb,),
        in_specs=[col(0), col(1), col(2), col(3),
                  _const_spec((1, HG_QK), li), _const_spec((1, HG_VW), li),
                  _const_spec((HG_CHUNK, HG_CHUNK)), _const_spec((HG_CHUNK, HG_CHUNK))],
        out_specs=_seq_spec(nblk, HG_VW, lambda b: 0),
        scratch_shapes=[pltpu.VMEM((HG_HEADS, HG_VAL, HG_KEY), F32)]
        + 3 * HG_SLOTS * [pltpu.VMEM((HG_CHUNK, HG_QK), F32)]
        + HG_SLOTS * [pltpu.VMEM((HG_CHUNK, HG_VW), F32)],
        compiler_params=_cparams(1),
        name="hgrn2",
    )(z4, z4, z4, z4, lb, onorm, tri, lvl)
    return o.reshape(nblk * nb * T_BLK, HG_VW)


def _hgrn_level_matrix():
    r = np.arange(HG_CHUNK)[:, None]
    c = np.arange(HG_CHUNK)[None, :]
    lvl = np.zeros((HG_CHUNK, HG_CHUNK), np.int32)
    for li, m in enumerate(HG_LEVELS):
        lvl = np.where(r // m == c // m, li, lvl)
    lvl = np.where(r // HG_BOTTOM == c // HG_BOTTOM, len(HG_LEVELS), lvl)
    return np.where(c <= r, lvl, -1).astype(np.int32)


def _back_kernel(om_ref, os_ref, oh_ref, gm_ref, gs_ref, gh_ref, h_ref,
                 wm_ref, ws_ref, wh_ref, wo_ref, g1_ref, b1_ref,
                 wg_ref, wu_ref, wd_ref, g2_ref, b2_ref, o_ref, r1_ref, r2_ref, *, alpha):
    @pl.when(pl.program_id(0) == 0)
    def _():
        r1_ref[...] = jnp.zeros_like(r1_ref)
        r2_ref[...] = jnp.zeros_like(r2_ref)

    r1 = r1_ref[...]
    r2 = r2_ref[...]
    ym = _dot(om_ref[...], wm_ref[...])
    ys = _dot(os_ref[...], ws_ref[...])
    yh = _dot(oh_ref[...], wh_ref[...])
    o_ref[...] = _layer_norm(r2, g2_ref[...], b2_ref[...]).reshape(o_ref.shape)
    h1 = _layer_norm(r1, g1_ref[...], b1_ref[...])
    hb = h1.astype(BF16)
    a = _dot(hb, wg_ref[...])
    u = _dot(hb, wu_ref[...])
    mixed = _sigmoid(gm_ref[...].astype(F32)) * ym
    mixed += _sigmoid(gs_ref[...].astype(F32)) * ys
    mixed += _sigmoid(gh_ref[...].astype(F32)) * yh
    r1_ref[...] = alpha * h_ref[...] + _dot(mixed.astype(BF16), wo_ref[...])
    r2_ref[...] = alpha * h1 + _dot((a * _sigmoid(a) * u).astype(BF16), wd_ref[...])


def _back(om, oh, zall, h, wm, ws, wh, wo, g1, b1, wg, wu, wd, g2, b2, nb, alpha, final, li):
    r, d = h.shape
    bw = om.shape[1]
    dff = wg.shape[-1]
    rb = T_BLK * nb
    skip = 1 if final else 0
    n_blk = r // rb - skip
    lag = 2
    row = lambda width, j=0: pl.BlockSpec((rb, width), lambda i: (jnp.minimum(i, n_blk - 1) + skip, j))
    if final:
        out_shape = jax.ShapeDtypeStruct((nb, n_blk * T_BLK, d), F32)
        out_spec = pl.BlockSpec((nb, T_BLK, d), lambda i: (0, jnp.maximum(i - lag, 0), 0))
    else:
        out_shape = jax.ShapeDtypeStruct((r, d), F32)
        out_spec = pl.BlockSpec((rb, d), lambda i: (jnp.maximum(i - lag, 0), 0))
    return pl.pallas_call(
        functools.partial(_back_kernel, alpha=alpha),
        out_shape=out_shape,
        grid=(n_blk + lag,),
        in_specs=[row(bw), row(bw, OFF_Y // bw), row(bw),
                  row(d, OFF_G // d), row(d, OFF_G // d + 1), row(d, OFF_G // d + 2), row(d),
                  _const_spec((bw, d), li), _const_spec((bw, d), li), _const_spec((bw, d), li),
                  _const_spec((d, d), li), _const_spec((1, d), li), _const_spec((1, d), li),
                  _const_spec((d, dff), li), _const_spec((d, dff), li), _const_spec((dff, d), li),
                  _const_spec((1, d), li), _const_spec((1, d), li)],
        out_specs=out_spec,
        scratch_shapes=[pltpu.VMEM((rb, d), F32), pltpu.VMEM((rb, d), F32)],
        compiler_params=_cparams(1, "arbitrary"),
        name="merge_ffn",
    )(om, zall, oh, zall, zall, zall, h, wm, ws, wh, wo, g1, b1, wg, wu, wd, g2, b2)


def _permute_w_in(w):
    mla_in = MLA_Q_RANK + MLA_KV_RANK + MLA_ROPE
    s5_0, hg_0 = mla_in, mla_in + S5_WIDTH
    g_0 = hg_0 + 2 * HG_QK + 2 * HG_VW
    pad0 = jnp.zeros((w.shape[0], MLA_NOPE), w.dtype)
    pad1 = jnp.zeros((w.shape[0], ZKR_W - MLA_NOPE - MLA_ROPE), w.dtype)
    return jnp.concatenate([w[:, s5_0:hg_0], w[:, 0:MLA_Q_RANK + MLA_KV_RANK], w[:, hg_0:g_0], w[:, g_0:],
                            pad0, w[:, MLA_Q_RANK + MLA_KV_RANK:mla_in], pad1], axis=1).astype(BF16)


def _mla_weights(w_uq, w_ukv):
    rq, rkv = w_uq.shape[0], w_ukv.shape[0]
    zpad = HEAD_PAD - MLA_NOPE - MLA_ROPE
    wq = w_uq.reshape(rq, MLA_HEADS, MLA_NOPE + MLA_ROPE)
    q_nope, q_rope = wq[..., :MLA_NOPE], wq[..., MLA_NOPE:]
    zq = jnp.zeros((rq, MLA_HEADS, zpad), w_uq.dtype)
    wq_p = jnp.concatenate([q_nope, q_rope, zq], axis=-1).reshape(rq, -1)
    wkv = w_ukv.reshape(rkv, MLA_HEADS, MLA_NOPE + MLA_V)
    zk = jnp.zeros((rkv, MLA_HEADS, HEAD_PAD - MLA_NOPE), w_ukv.dtype)
    wk_p = jnp.concatenate([wkv[..., :MLA_NOPE], zk], axis=-1).reshape(rkv, -1)
    wv = wkv[..., MLA_NOPE:].reshape(rkv, -1)
    return [a.astype(BF16) for a in (wq_p, wk_p, wv)]


def _rope_consts():
    half = MLA_ROPE // 2
    inv = ROPE_THETA ** (-(jnp.arange(0, MLA_ROPE, 2, dtype=F32) / MLA_ROPE))
    place = np.zeros((half, HEAD_PAD), np.float32)
    place[np.arange(half), MLA_NOPE + np.arange(half)] = 1.0
    place[np.arange(half), MLA_NOPE + half + np.arange(half)] = 1.0
    base = np.zeros((1, HEAD_PAD), np.float32)
    base[0, :MLA_NOPE] = 1.0
    return inv[:, None], jnp.asarray(place, BF16), jnp.asarray(base)


def _s5_params(lam_re, lam_im, log_dt, b_re, b_im, c_re, c_im):
    lr = jnp.minimum(lam_re.astype(F32), -1e-4)
    li = lam_im.astype(F32)
    dt = jnp.exp(log_dt.astype(F32))[:, None]
    mag = jnp.exp(lr * dt)
    ab_r = mag * jnp.cos(li * dt)
    ab_i = mag * jnp.sin(li * dt)
    den = lr * lr + li * li
    nr = ab_r - 1.0
    coef_r = ((nr * lr + ab_i * li) / den)[..., None]
    coef_i = ((ab_i * lr - nr * li) / den)[..., None]
    bb_r = coef_r * b_re.astype(F32) - coef_i * b_im.astype(F32)
    bb_i = coef_r * b_im.astype(F32) + coef_i * b_re.astype(F32)
    n_slab = S5_WIDTH // S5_SLAB
    gps = S5_SLAB // S5_GROUP
    eye = jnp.eye(gps, dtype=F32)

    def in_mat(bb):
        b4 = bb.reshape(n_slab, gps, S5_STATE, S5_GROUP)
        return jnp.einsum('jgnc,gh->jgchn', b4, eye).reshape(n_slab, S5_SLAB, S5_SLAB_STATE)

    def out_mat(cc):
        c4 = cc.astype(F32).reshape(n_slab, gps, S5_GROUP, S5_STATE)
        return jnp.einsum('jgcn,gh->jgnhc', c4, eye).reshape(n_slab, S5_SLAB_STATE, S5_SLAB)

    bm = jnp.concatenate([in_mat(bb_r), in_mat(bb_i)], axis=2).astype(BF16)
    cm = jnp.concatenate([out_mat(c_re), -out_mat(c_im)], axis=1).astype(BF16)
    ar = ab_r.reshape(n_slab, 1, S5_SLAB_STATE)
    ai = ab_i.reshape(n_slab, 1, S5_SLAB_STATE)
    return bm, cm, ar, ai


def kernel(x, positions, meta_tokens, ln_in_g, ln_in_b, w_in, mla_q_norm, mla_w_uq, mla_kv_norm, mla_w_ukv,
           s5_lam_re, s5_lam_im, s5_log_dt, s5_b_re, s5_b_im, s5_c_re, s5_c_im, s5_d, s5_w_glu,
           hg_lb_logits, hg_out_norm, w_br_mla, w_br_s5, w_br_hg, w_out, ln1_g, ln1_b,
           w_ffn_gate, w_ffn_up, w_ffn_down, ln2_g, ln2_b):
    nb, s, d = x.shape
    depth = w_in.shape[0]
    nblk = (s + N_META) // T_BLK
    alpha = (2 * depth) ** 0.25
    row2 = lambda a: a.astype(F32)[None, :]
    rows3 = lambda a: a.astype(F32)[:, None, :]
    bf16 = lambda a: a.astype(BF16)

    meta_pos = jnp.broadcast_to(jnp.arange(N_META, dtype=jnp.int32)[None, :], (nb, N_META))
    pos = jnp.concatenate([meta_pos, positions.astype(jnp.int32) + N_META], axis=1)
    pos = pos.reshape(nb, nblk, T_BLK).transpose(1, 0, 2).reshape(nblk, 1, nb * T_BLK)
    rope = _rope_consts()
    p_lb = jax.nn.softmax(hg_lb_logits.astype(F32), axis=0)
    lower_bounds = jnp.cumsum(p_lb, axis=0) - p_lb[0]
    tri = jnp.asarray(np.tril(np.ones((HG_CHUNK, HG_CHUNK), np.float32)), BF16)
    lvl = jnp.asarray(_hgrn_level_matrix())
    perm = jnp.asarray(_time_major_perm(nb), BF16)

    front_p = (jax.vmap(_permute_w_in)(w_in), pos, rope, rows3(mla_q_norm), rows3(mla_kv_norm),
               *jax.vmap(_mla_weights)(mla_w_uq, mla_w_ukv), perm,
               *jax.vmap(_s5_params)(s5_lam_re, s5_lam_im, s5_log_dt, s5_b_re, s5_b_im, s5_c_re, s5_c_im),
               rows3(s5_d), bf16(s5_w_glu))
    hgrn_p = (lower_bounds[:, None, :], rows3(hg_out_norm), tri, lvl)
    back_p = (bf16(w_br_mla), bf16(w_br_s5), bf16(w_br_hg), bf16(w_out), rows3(ln1_g), rows3(ln1_b),
              bf16(w_ffn_gate), bf16(w_ffn_up), bf16(w_ffn_down), rows3(ln2_g), rows3(ln2_b))

    h = (x, meta_tokens.astype(x.dtype), row2(ln_in_g), row2(ln_in_b))
    for li in range(depth):
        outs = _front(h, *front_p, nb, li)
        if li == 0:
            h, *outs = outs
        zall, = outs
        o_mla = _attention(zall, nb, nblk)
        o_hg = _hgrn(zall, *hgrn_p, nb, nblk, li)
        h = _back(o_mla, o_hg, zall, h, *back_p, nb, alpha, final=li == depth - 1, li=li)
    return h
```

```python
import functools
import math

import jax
import jax.numpy as jnp
import numpy as np
from jax import lax
from jax.experimental import pallas as pl
from jax.experimental.pallas import tpu as pltpu

F32 = jnp.float32
BF16 = jnp.bfloat16

N_META = 16
MLA_HEADS = 8
MLA_NOPE = 64
MLA_ROPE = 32
MLA_V = 64
MLA_Q_RANK = 256
MLA_KV_RANK = 256
ROPE_THETA = 10000.0
MASK_VALUE = -1e9
LOG2_E = math.log2(math.e)
HEAD_PAD = 128
S5_WIDTH = 512
S5_GROUP = 16
S5_GROUPS = S5_WIDTH // S5_GROUP
S5_STATE = 64
S5_SLAB = 128
S5_SLAB_STATE = (S5_SLAB // S5_GROUP) * S5_STATE
HG_HEADS = 4
HG_KEY = 128
HG_VAL = 128
HG_QK = HG_HEADS * HG_KEY
HG_VW = HG_HEADS * HG_VAL
HG_F_MIN = 1e-6
HG_CHUNK = 128
HG_LEVELS = (128, 64, 32, 16)
HG_BOTTOM = 8
HG_SLOTS = 3
N_BRANCH = 3
T_BLK = 16
ATT_TQ = 256
ATT_HEADS_PER_STEP = 4
ATT_STEPS_PER_BODY = 8
ATT_VE = MLA_V + 16
VMEM_LIMIT = 56 * 1024 * 1024

ZA_W = 1024
ZHG_W = 2048
ZG_W = 3072
ZKR_W = 128
Z_W = ZA_W + ZHG_W + ZG_W + ZKR_W
ZALL_WIDTHS = (ZHG_W, ZG_W, MLA_HEADS * HEAD_PAD, MLA_HEADS * HEAD_PAD, MLA_HEADS * MLA_V, S5_WIDTH)
ZALL_OFF = tuple(sum(ZALL_WIDTHS[:i]) for i in range(len(ZALL_WIDTHS)))
ZALL_W = sum(ZALL_WIDTHS)
OFF_HG, OFF_G, OFF_Q, OFF_K, OFF_V, OFF_Y = ZALL_OFF
WIDE_CHUNK = 512
FILL_BEFORE_MLA = 3


def _cparams(n_grid, sem="parallel"):
    return pltpu.CompilerParams(dimension_semantics=(sem,) * n_grid, vmem_limit_bytes=VMEM_LIMIT)


def _const_spec(shape, layer=None):
    nd = len(shape)
    if layer is None:
        return pl.BlockSpec(shape, lambda *_: (0,) * nd, pipeline_mode=pl.Buffered(1))
    return pl.BlockSpec((None,) + tuple(shape), lambda *_: (layer,) + (0,) * nd, pipeline_mode=pl.Buffered(1))


def _seq_spec(nblk, width, col):
    return pl.BlockSpec((nblk, None, T_BLK, width), lambda b, *g: (0, b, 0, col(b, *g)))


def _dot(a, b):
    return jnp.dot(a, b, preferred_element_type=F32)


def _dot_nt(a, b):
    return lax.dot_general(a, b, (((1,), (1,)), ((), ())), preferred_element_type=F32)


def _dot_tn(a, b):
    return lax.dot_general(a, b, (((0,), (0,)), ((), ())), preferred_element_type=F32)


def _sigmoid(x):
    return 1.0 / (1.0 + jnp.exp(-x))


def _layer_norm(x, g, b, eps=1e-5):
    mu = jnp.mean(x, axis=-1, keepdims=True)
    xc = x - mu
    var = jnp.mean(xc * xc, axis=-1, keepdims=True)
    return xc * lax.rsqrt(var + eps) * g + b


def _rms_norm(x, g, eps=1e-6):
    return x * lax.rsqrt(jnp.mean(x * x, axis=-1, keepdims=True) + eps) * g


def _load_rows(ref, blk0, n_blk, cs):
    x = ref[pl.ds(blk0, n_blk), :, cs]
    return x.reshape(n_blk * T_BLK, x.shape[-1])


def _store_rows(ref, blk0, n_blk, cs, x):
    ref[pl.ds(blk0, n_blk), :, cs] = x.reshape(n_blk, T_BLK, x.shape[-1])


def _rope_tables(pos_ref, inv_ref, place_ref, base_ref):
    ang = inv_ref[...] * pos_ref[...].astype(F32)

    def place(t):
        hi = t.astype(BF16)
        lo = (t - hi.astype(F32)).astype(BF16)
        return _dot_tn(hi, place_ref[...]) + _dot_tn(lo, place_ref[...])

    return place(jnp.cos(ang)) + base_ref[...], place(jnp.sin(ang))


def _mla_prep(cq, ckv, kr, cos, sin, qn_ref, kvn_ref, wq_ref, wk_ref, wv_ref, q_ref, k_ref, v_ref):
    scale = (MLA_NOPE + MLA_ROPE) ** -0.5 * LOG2_E
    half = MLA_ROPE // 2
    lane = lax.broadcasted_iota(jnp.int32, (1, HEAD_PAD), 1)
    sin_up = jnp.where((lane >= MLA_NOPE + half) & (lane < MLA_NOPE + MLA_ROPE), sin, 0.0)
    sin_dn = jnp.where((lane >= MLA_NOPE) & (lane < MLA_NOPE + half), -sin, 0.0)

    def rope(x, c, s_up, s_dn):
        return x * c + pltpu.roll(x, half, 1) * s_up + pltpu.roll(x, HEAD_PAD - half, 1) * s_dn

    cqn = _rms_norm(cq, qn_ref[...]).astype(BF16)
    ckvn = _rms_norm(ckv, kvn_ref[...]).astype(BF16)
    k_rope = rope(kr, cos, sin_up, sin_dn)
    cos_q, sin_up_q, sin_dn_q = cos * scale, sin_up * scale, sin_dn * scale
    for h in range(MLA_HEADS):
        cs = slice(h * HEAD_PAD, (h + 1) * HEAD_PAD)
        q_ref[:, cs] = rope(_dot(cqn, wq_ref[:, cs]), cos_q, sin_up_q, sin_dn_q).astype(BF16)
        k_ref[:, cs] = (_dot(ckvn, wk_ref[:, cs]) + k_rope).astype(BF16)
    v_ref[...] = _dot(ckvn, wv_ref[...]).astype(BF16)


def _s5_block(u, perm_ref, bm_ref, cm_ref, ar_ref, ai_ref, d_ref, wg_ref, o_ref,
              xr_ref, xi_ref, buf_ref, y_ref, nb, fillers):
    n_slab = S5_WIDTH // S5_SLAB
    ns = S5_SLAB_STATE
    u = _dot(perm_ref[...], u).astype(BF16)
    for j in range(n_slab):
        buf_ref[...] = _dot(u[:, j * S5_SLAB:(j + 1) * S5_SLAB], bm_ref[j])
        if fillers:
            fillers.pop(0)()
        ar = jnp.broadcast_to(ar_ref[j], (nb, ns))
        ai = jnp.broadcast_to(ai_ref[j], (nb, ns))
        xr = xr_ref[j]
        xi = xi_ref[j]
        for t in range(T_BLK):
            rows = slice(t * nb, (t + 1) * nb)
            nr = ar * xr - ai * xi + buf_ref[rows, 0:ns]
            ni = ar * xi + ai * xr + buf_ref[rows, ns:2 * ns]
            buf_ref[rows, 0:ns] = nr
            buf_ref[rows, ns:2 * ns] = ni
            xr, xi = nr, ni
        xr_ref[j] = xr
        xi_ref[j] = xi
        y_ref[:, j * S5_SLAB:(j + 1) * S5_SLAB] = _dot(buf_ref[...].astype(BF16), cm_ref[j])

    while fillers:
        fillers.pop(0)()
    y = y_ref[...] + d_ref[...] * u.astype(F32)
    y = 0.5 * y * (1.0 + jnp.tanh(math.sqrt(2.0 / math.pi) * (y + 0.044715 * (y * y * y))))
    gate = _sigmoid(_dot(y.astype(BF16), wg_ref[...]))
    out = (y * gate).astype(BF16)
    o_ref[...] = _dot_tn(perm_ref[...], out).astype(BF16)


def _front_kernel(*refs, nb, ln_in):
    if ln_in:
        x_ref, meta_ref, lng_ref, lnb_ref, *refs = refs
    else:
        h_ref, *refs = refs
    (w_ref, pos_ref, inv_ref, place_ref, base_ref, qn_ref, kvn_ref, wq_ref, wk_ref, wv_ref,
     perm_ref, bm_ref, cm_ref, ar_ref, ai_ref, d_ref, wglu_ref, *refs) = refs
    if ln_in:
        hout_ref, *refs = refs
    zall_ref, xr_ref, xi_ref, buf_ref, y_ref = refs
    zhg_ref, zg_ref, q_ref, k_ref, v_ref, ys5_ref = (
        zall_ref.at[:, o:o + w] for o, w in zip(ZALL_OFF, ZALL_WIDTHS))

    @pl.when(pl.program_id(0) == 0)
    def _():
        xr_ref[...] = jnp.zeros_like(xr_ref)
        xi_ref[...] = jnp.zeros_like(xi_ref)

    if ln_in:
        hx = _layer_norm(x_ref[...], lng_ref[...], lnb_ref[...])
        hm = _layer_norm(meta_ref[...], lng_ref[...], lnb_ref[...])
        h = jnp.where(pl.program_id(0) == 0, jnp.broadcast_to(hm[None], hx.shape), hx)
        h = h.reshape(hout_ref.shape)
        hout_ref[...] = h
    else:
        h = h_ref[...]
    x = h.astype(BF16)
    za = _dot(x, w_ref[:, 0:ZA_W])
    zkr = _dot(x, w_ref[:, ZA_W + ZHG_W + ZG_W:Z_W])

    def wide_chunk(o_ref, c, off):
        def run():
            o_ref[:, c:c + WIDE_CHUNK] = _dot(x, w_ref[:, off + c:off + c + WIDE_CHUNK]).astype(BF16)
        return run

    fillers = [wide_chunk(zhg_ref, c, ZA_W) for c in range(0, ZHG_W, WIDE_CHUNK)]
    fillers += [wide_chunk(zg_ref, c, ZA_W + ZHG_W) for c in range(0, ZG_W, WIDE_CHUNK)]
    for _ in range(FILL_BEFORE_MLA):
        fillers.pop(0)()
    cos, sin = _rope_tables(pos_ref, inv_ref, place_ref, base_ref)
    c0 = S5_WIDTH
    _mla_prep(za[:, c0:c0 + MLA_Q_RANK], za[:, c0 + MLA_Q_RANK:c0 + MLA_Q_RANK + MLA_KV_RANK], zkr,
              cos, sin, qn_ref, kvn_ref, wq_ref, wk_ref, wv_ref, q_ref, k_ref, v_ref)
    _s5_block(za[:, 0:S5_WIDTH].astype(BF16), perm_ref, bm_ref, cm_ref, ar_ref, ai_ref, d_ref, wglu_ref, ys5_ref,
              xr_ref, xi_ref, buf_ref, y_ref, nb, fillers)


def _front(src, w, pos, rope, qn, kvn, wq, wk, wv, perm, bm, cm, ar, ai, dskip, wglu, nb, li):
    ln_in = isinstance(src, tuple)
    rb = T_BLK * nb
    if ln_in:
        x, meta, lng, lnb = src
        d = x.shape[-1]
        r = (x.shape[1] + N_META) * nb
        src_specs = [pl.BlockSpec((nb, T_BLK, d), lambda i: (0, jnp.maximum(i - 1, 0), 0)),
                     _const_spec((N_META, d)), _const_spec((1, d)), _const_spec((1, d))]
    else:
        src = (src,)
        r, d = src[0].shape
        src_specs = [pl.BlockSpec((rb, d), lambda i: (i, 0))]
    hw = MLA_HEADS * HEAD_PAD
    vw = MLA_HEADS * MLA_V
    n_slab = S5_WIDTH // S5_SLAB
    row = lambda width: pl.BlockSpec((rb, width), lambda i: (i, 0))
    out_shape = [jax.ShapeDtypeStruct((r, ZALL_W), BF16)]
    out_specs = [row(ZALL_W)]
    if ln_in:
        out_shape.insert(0, jax.ShapeDtypeStruct((r, d), F32))
        out_specs.insert(0, row(d))
    return pl.pallas_call(
        functools.partial(_front_kernel, nb=nb, ln_in=ln_in),
        out_shape=out_shape,
        grid=(r // rb,),
        in_specs=src_specs + [
            _const_spec((d, Z_W), li), pl.BlockSpec((None, 1, rb), lambda i: (i, 0, 0)),
            _const_spec((MLA_ROPE // 2, 1)), _const_spec((MLA_ROPE // 2, HEAD_PAD)), _const_spec((1, HEAD_PAD)),
            _const_spec((1, MLA_Q_RANK), li), _const_spec((1, MLA_KV_RANK), li),
            _const_spec((MLA_Q_RANK, hw), li), _const_spec((MLA_KV_RANK, hw), li), _const_spec((MLA_KV_RANK, vw), li),
            _const_spec((rb, rb)),
            _const_spec((n_slab, S5_SLAB, 2 * S5_SLAB_STATE), li),
            _const_spec((n_slab, 2 * S5_SLAB_STATE, S5_SLAB), li),
            _const_spec((n_slab, 1, S5_SLAB_STATE), li),
            _const_spec((n_slab, 1, S5_SLAB_STATE), li),
            _const_spec((1, S5_WIDTH), li),
            _const_spec((S5_WIDTH, S5_WIDTH), li),
        ],
        out_specs=out_specs,
        scratch_shapes=[
            pltpu.VMEM((n_slab, nb, S5_SLAB_STATE), F32),
            pltpu.VMEM((n_slab, nb, S5_SLAB_STATE), F32),
            pltpu.VMEM((rb, 2 * S5_SLAB_STATE), F32),
            pltpu.VMEM((rb, S5_WIDTH), F32),
        ],
        compiler_params=_cparams(1, "arbitrary"),
        name="front",
    )(*src, w, pos, *rope, qn, kvn, wq, wk, wv, perm, bm, cm, ar, ai, dskip, wglu)


def _attn_kernel(tab_ref, q_ref, k_ref, v_ref, o_ref, vt_ref, qt_ref, m_ref, acc_ref, *bufs, n_heads, tq, nq):
    xs_ref, xm_ref, ps_ref, pm_ref, a_ref = (bufs[2 * i:2 * i + 2] for i in range(5))
    bpq = tq // T_BLK
    vw = n_heads * MLA_V

    def values_t(vb):
        vt = vb.T
        rows = lax.broadcasted_iota(jnp.int32, (ATT_VE - MLA_V, vt.shape[1]), 0)
        ext = jnp.where(rows == 0, 1.0, 0.0).astype(BF16)
        parts = []
        for h in range(n_heads):
            parts += [vt[h * MLA_V:(h + 1) * MLA_V, :], ext]
        return jnp.concatenate(parts, axis=0)

    for j in range(nq):
        vt_ref[j] = values_t(_load_rows(v_ref, 1 + j * bpq, bpq, slice(0, vw)))
    v0t = values_t(v_ref[0])

    for i in range(nq):
        qb = _load_rows(q_ref, 1 + i * bpq, bpq, slice(0, n_heads * HEAD_PAD))
        qt_ref[i] = qb.T

    def causal(st):
        r = lax.broadcasted_iota(jnp.int32, st.shape, 0)
        c = lax.broadcasted_iota(jnp.int32, st.shape, 1)
        return jnp.where(r <= c, st, MASK_VALUE * LOG2_E)

    heads = range(n_heads)
    qs = [slice(h * HEAD_PAD, (h + 1) * HEAD_PAD) for h in heads]
    vs = [slice(h * ATT_VE, (h + 1) * ATT_VE) for h in heads]
    k0 = [k_ref[0, :, c] for c in qs]

    def normalised(acc):
        return acc[0:MLA_V, :] / acc[MLA_V:MLA_V + 1, :]

    outs = []
    for n in heads:
        st = causal(_dot_nt(k0[n], q_ref[0, :, qs[n]]))
        p = jnp.exp2(st - jnp.max(st, axis=0, keepdims=True))
        outs.append(normalised(_dot(v0t[vs[n], :], p.astype(BF16))))
    o_ref[0, :, :] = jnp.concatenate(outs, axis=0).T.astype(BF16)

    def key_tile(j, n):
        return _load_rows(k_ref, 1 + j * bpq, bpq, qs[n])

    def diag_scores(i, s):
        for n in heads:
            xs_ref[s][n] = _dot(key_tile(i, n), qt_ref[i, qs[n], :])
            xm_ref[s][n] = _dot(k0[n], qt_ref[i, qs[n], :])

    def diag_stats(i, s):
        for n in heads:
            st, sm = causal(xs_ref[s][n]), xm_ref[s][n]
            m = jnp.maximum(jnp.max(st, axis=0, keepdims=True), jnp.max(sm, axis=0, keepdims=True))
            p = jnp.exp2(st - m)
            pm = jnp.exp2(sm - m)
            m_ref[i, n] = m
            ps_ref[s][n] = p.astype(BF16)
            pm_ref[s][n] = pm.astype(BF16)

    def diag_values(i, s):
        for n in heads:
            acc_ref[i, vs[n], :] = _dot(vt_ref[i, vs[n], :], ps_ref[s][n]) + _dot(v0t[vs[n], :], pm_ref[s][n])

    _pipeline3(nq, diag_scores, diag_stats, diag_values, steps_per_body=ATT_STEPS_PER_BODY)

    def off_scores(f, s):
        i, j = tab_ref[0, f], tab_ref[1, f]
        for n in heads:
            xs_ref[s][n] = _dot(key_tile(j, n), qt_ref[i, qs[n], :])

    def off_stats(f, s):
        i = tab_ref[0, f]
        for n in heads:
            x = xs_ref[s][n]
            m_old = m_ref[i, n]
            m = jnp.maximum(m_old, jnp.max(x, axis=0, keepdims=True))
            a = jnp.exp2(m_old - m)
            p = jnp.exp2(x - m)
            m_ref[i, n] = m
            a_ref[s][n] = a
            ps_ref[s][n] = p.astype(BF16)

    def off_values(f, s):
        i, j = tab_ref[0, f], tab_ref[1, f]
        for n in heads:
            acc_ref[i, vs[n], :] = a_ref[s][n] * acc_ref[i, vs[n], :] + _dot(vt_ref[j, vs[n], :], ps_ref[s][n])

    _pipeline3(nq * (nq - 1) // 2, off_scores, off_stats, off_values, steps_per_body=ATT_STEPS_PER_BODY)

    for i in range(nq):
        o = jnp.concatenate([normalised(acc_ref[i, vs[n], :]) for n in heads], axis=0)
        _store_rows(o_ref, 1 + i * bpq, bpq, slice(0, vw), o.astype(BF16).T)


def _pipeline3(n, stage1, stage2, stage3, n_slots=2, steps_per_body=None):
    if n == 0:
        return
    stage1(0, 0)
    if n > 1:
        stage1(1, 1 % n_slots)
    stage2(0, 0)

    def step(t, k):
        stage1(t + 2, (k + 2) % n_slots)
        stage2(t + 1, (k + 1) % n_slots)
        stage3(t, k)

    steady = max(n - 2, 0)
    spb = steps_per_body or n_slots
    trips = steady // spb
    if trips:
        def body(u, carry):
            for k in range(spb):
                step(spb * u + k, k % n_slots)
            return carry

        lax.fori_loop(0, trips, body, 0)
    for t in range(trips * spb, steady):
        step(t, t % n_slots)
    if n > 1:
        stage3(n - 2, (n - 2) % n_slots)
        stage2(n - 1, (n - 1) % n_slots)
    stage3(n - 1, (n - 1) % n_slots)


def _off_diagonal_order(nq):
    left = [(i, j) for i in range(nq) for j in range(i)]
    order = []
    while left:
        count = {}
        for i, _ in left:
            count[i] = count.get(i, 0) + 1
        ok = [p for p in left if not order or p[0] != order[-1][0]] or left
        pick = max(ok, key=lambda p: (count[p[0]], -p[1]))
        order.append(pick)
        left.remove(pick)
    return order or [(0, 0)]


def _attention(zall, nb, nblk):
    z4 = zall.reshape(nblk, nb, T_BLK, ZALL_W)
    hps = ATT_HEADS_PER_STEP
    n_hp = MLA_HEADS // hps
    qw = hps * HEAD_PAD
    vw = hps * MLA_V
    s = (nblk - 1) * T_BLK
    tq = min(ATT_TQ, s)
    nq = s // tq
    col = lambda b, p: p
    tab = jnp.asarray(np.array(_off_diagonal_order(nq), np.int32).T)
    o = pl.pallas_call(
        functools.partial(_attn_kernel, n_heads=hps, tq=tq, nq=nq),
        out_shape=jax.ShapeDtypeStruct((nblk, nb, T_BLK, MLA_HEADS * MLA_V), BF16),
        grid=(nb, n_hp),
        in_specs=[pl.BlockSpec(memory_space=pltpu.SMEM),
                  _seq_spec(nblk, qw, lambda b, p: OFF_Q // qw + p),
                  _seq_spec(nblk, qw, lambda b, p: OFF_K // qw + p),
                  _seq_spec(nblk, vw, lambda b, p: OFF_V // vw + p)],
        out_specs=_seq_spec(nblk, vw, col),
        scratch_shapes=[pltpu.VMEM((nq, hps * ATT_VE, tq), BF16), pltpu.VMEM((nq, qw, tq), BF16),
                        pltpu.VMEM((nq, hps, 1, tq), F32), pltpu.VMEM((nq, hps * ATT_VE, tq), F32)]
        + 2 * [pltpu.VMEM((hps, tq, tq), F32)] + 2 * [pltpu.VMEM((hps, N_META, tq), F32)]
        + 2 * [pltpu.VMEM((hps, tq, tq), BF16)] + 2 * [pltpu.VMEM((hps, N_META, tq), BF16)]
        + 2 * [pltpu.VMEM((hps, 1, tq), F32)],
        compiler_params=_cparams(2),
        name="mla_attn",
    )(tab, z4, z4, z4)
    return o.reshape(nblk * nb * T_BLK, MLA_HEADS * MLA_V)


def _time_major_perm(nb):
    p = np.zeros((T_BLK * nb, T_BLK * nb), np.float32)
    for b in range(nb):
        for t in range(T_BLK):
            p[t * nb + b, b * T_BLK + t] = 1.0
    return p


def _block_row_bcast(x, m, row):
    t, c = x.shape
    if m == t:
        return jnp.broadcast_to(x[row:row + 1, :], x.shape)
    x3 = x.reshape(t // m, m, c)
    return jnp.broadcast_to(x3[:, row:row + 1, :], x3.shape).reshape(t, c)


def _hgrn_gates(blk0, n_blk, f_ref, lb_ref):
    lb = lb_ref[...]
    zf = _load_rows(f_ref, blk0, n_blk, slice(0, HG_QK)).astype(F32)
    e = jnp.exp(-jnp.abs(zf))
    rcp = 1.0 / (1.0 + e)
    pos = zf >= 0.0
    sig_p = jnp.where(pos, rcp, e * rcp)
    sig_n = jnp.where(pos, e * rcp, rcp)
    f = lb + (1.0 - lb) * sig_p
    log_f = jnp.log2(jnp.maximum(f, HG_F_MIN))
    k = (1.0 - lb) * sig_n
    hi = log_f.astype(BF16)
    lo = (log_f - hi.astype(F32)).astype(BF16)
    return k, hi, lo


def _hgrn_mix(blk0, n_blk, gates, q_ref, v_ref, tri_ref, lvl_ref, st_ref):
    t = n_blk * T_BLK
    cs = slice(0, HG_QK)
    heads = [slice(h * HG_KEY, (h + 1) * HG_KEY) for h in range(HG_HEADS)]
    k, hi, lo = gates
    q = _load_rows(q_ref, blk0, n_blk, cs).astype(F32)
    v = _load_rows(v_ref, blk0, n_blk, cs)
    tri = tri_ref[0:t, 0:t]
    cum = _dot(tri, hi) + _dot(tri, lo)
    lvl = lvl_ref[0:t, 0:t]
    r_idx = lax.broadcasted_iota(jnp.int32, (t, HG_QK), 0)
    c8 = _block_row_bcast(cum, HG_BOTTOM, HG_BOTTOM // 2 - 1)
    qe = (q * jnp.exp2(cum - c8)).astype(BF16)
    ke = (k * jnp.exp2(c8 - cum)).astype(BF16)
    n_lvl = len(HG_LEVELS)
    scores = [jnp.where(lvl == n_lvl, _dot_nt(qe[:, hs], ke[:, hs]), 0.0) for hs in heads]
    for li, m in enumerate(HG_LEVELS):
        if m > t:
            continue
        half = m // 2
        cmid = _block_row_bcast(cum, m, half - 1)
        upper = (r_idx & (m - 1)) >= half
        ex = jnp.exp2(jnp.where(upper, cum - cmid, cmid - cum))
        qe = jnp.where(upper, q * ex, 0.0).astype(BF16)
        ke = jnp.where(upper, 0.0, k * ex).astype(BF16)
        scores = [jnp.where(lvl == li, _dot_nt(qe[:, hs], ke[:, hs]), s) for s, hs in zip(scores, heads)]
    qd = (q * jnp.exp2(cum)).astype(BF16)
    last = cum[t - 1:t, :]
    kd = (k * jnp.exp2(last - cum)).astype(BF16)
    dec = jnp.exp2(last)
    outs = []
    for h, hs in enumerate(heads):
        st = st_ref[h]
        outs.append(_dot(scores[h].astype(BF16), v[:, hs]) + _dot_nt(qd[:, hs], st.astype(BF16)))
        st_ref[h] = st * dec[:, hs] + _dot_tn(v[:, hs], kd[:, hs])
    return tuple(outs)


def _hgrn_out(blk0, n_blk, outs, g_ref, on_ref, o_ref):
    cs = slice(0, HG_VW)
    g = _load_rows(g_ref, blk0, n_blk, cs).astype(F32)
    o = jnp.concatenate([o * lax.rsqrt(jnp.mean(o * o, axis=-1, keepdims=True) + 1e-6) for o in outs], axis=1)
    _store_rows(o_ref, blk0, n_blk, cs, (o * on_ref[...] * (g * _sigmoid(g))).astype(BF16))


def _hgrn_kernel(q_ref, f_ref, v_ref, g_ref, lb_ref, on_ref, tri_ref, lvl_ref, o_ref, st_ref, *bufs, n_chunks):
    kb_ref, hi_ref, lo_ref, ob_ref = (bufs[HG_SLOTS * i:HG_SLOTS * (i + 1)] for i in range(4))
    bpc = HG_CHUNK // T_BLK
    heads = [slice(h * HG_VAL, (h + 1) * HG_VAL) for h in range(HG_HEADS)]
    gates = functools.partial(_hgrn_gates, f_ref=f_ref, lb_ref=lb_ref)
    mix = functools.partial(_hgrn_mix, q_ref=q_ref, v_ref=v_ref, tri_ref=tri_ref, lvl_ref=lvl_ref, st_ref=st_ref)
    out = functools.partial(_hgrn_out, g_ref=g_ref, on_ref=on_ref, o_ref=o_ref)
    st_ref[...] = jnp.zeros_like(st_ref)
    out(0, 1, mix(0, 1, gates(0, 1)))

    def stage_gates(c, s):
        k, hi, lo = gates(1 + c * bpc, bpc)
        kb_ref[s][...], hi_ref[s][...], lo_ref[s][...] = k, hi.astype(F32), lo.astype(F32)

    def stage_mix(c, s):
        outs = mix(1 + c * bpc, bpc, (kb_ref[s][...], hi_ref[s][...].astype(BF16), lo_ref[s][...].astype(BF16)))
        for hs, o in zip(heads, outs):
            ob_ref[s][:, hs] = o

    def stage_out(c, s):
        out(1 + c * bpc, bpc, tuple(ob_ref[s][:, hs] for hs in heads))

    _pipeline3(n_chunks, stage_gates, stage_mix, stage_out, n_slots=HG_SLOTS)


def _hgrn(zall, lb, onorm, tri, lvl, nb, nblk, li):
    z4 = zall.reshape(nblk, nb, T_BLK, ZALL_W)
    col = lambda j: _seq_spec(nblk, HG_QK, lambda b: OFF_HG // HG_QK + j)
    o = pl.pallas_call(
        functools.partial(_hgrn_kernel, n_chunks=(nblk - 1) * T_BLK // HG_CHUNK),
        out_shape=jax.ShapeDtypeStruct((nblk, nb, T_BLK, HG_VW), BF16),
        grid=(nb,),
        in_specs=[col(0), col(1), col(2), col(3),
                  _const_spec((1, HG_QK), li), _const_spec((1, HG_VW), li),
                  _const_spec((HG_CHUNK, HG_CHUNK)), _const_spec((HG_CHUNK, HG_CHUNK))],
        out_specs=_seq_spec(nblk, HG_VW, lambda b: 0),
        scratch_shapes=[pltpu.VMEM((HG_HEADS, HG_VAL, HG_KEY), F32)]
        + 3 * HG_SLOTS * [pltpu.VMEM((HG_CHUNK, HG_QK), F32)]
        + HG_SLOTS * [pltpu.VMEM((HG_CHUNK, HG_VW), F32)],
        compiler_params=_cparams(1),
        name="hgrn2",
    )(z4, z4, z4, z4, lb, onorm, tri, lvl)
    return o.reshape(nblk * nb * T_BLK, HG_VW)


def _hgrn_level_matrix():
    r = np.arange(HG_CHUNK)[:, None]
    c = np.arange(HG_CHUNK)[None, :]
    lvl = np.zeros((HG_CHUNK, HG_CHUNK), np.int32)
    for li, m in enumerate(HG_LEVELS):
        lvl = np.where(r // m == c // m, li, lvl)
    lvl = np.where(r // HG_BOTTOM == c // HG_BOTTOM, len(HG_LEVELS), lvl)
    return np.where(c <= r, lvl, -1).astype(np.int32)


def _back_kernel(om_ref, os_ref, oh_ref, gm_ref, gs_ref, gh_ref, h_ref,
                 wm_ref, ws_ref, wh_ref, wo_ref, g1_ref, b1_ref,
                 wg_ref, wu_ref, wd_ref, g2_ref, b2_ref, o_ref, r1_ref, r2_ref, *, alpha):
    @pl.when(pl.program_id(0) == 0)
    def _():
        r1_ref[...] = jnp.zeros_like(r1_ref)
        r2_ref[...] = jnp.zeros_like(r2_ref)

    r1 = r1_ref[...]
    r2 = r2_ref[...]
    ym = _dot(om_ref[...], wm_ref[...])
    ys = _dot(os_ref[...], ws_ref[...])
    yh = _dot(oh_ref[...], wh_ref[...])
    o_ref[...] = _layer_norm(r2, g2_ref[...], b2_ref[...]).reshape(o_ref.shape)
    h1 = _layer_norm(r1, g1_ref[...], b1_ref[...])
    hb = h1.astype(BF16)
    a = _dot(hb, wg_ref[...])
    u = _dot(hb, wu_ref[...])
    mixed = _sigmoid(gm_ref[...].astype(F32)) * ym
    mixed += _sigmoid(gs_ref[...].astype(F32)) * ys
    mixed += _sigmoid(gh_ref[...].astype(F32)) * yh
    r1_ref[...] = alpha * h_ref[...] + _dot(mixed.astype(BF16), wo_ref[...])
    r2_ref[...] = alpha * h1 + _dot((a * _sigmoid(a) * u).astype(BF16), wd_ref[...])


def _back(om, oh, zall, h, wm, ws, wh, wo, g1, b1, wg, wu, wd, g2, b2, nb, alpha, final, li):
    r, d = h.shape
    bw = om.shape[1]
    dff = wg.shape[-1]
    rb = T_BLK * nb
    skip = 1 if final else 0
    n_blk = r // rb - skip
    lag = 2
    row = lambda width, j=0: pl.BlockSpec((rb, width), lambda i: (jnp.minimum(i, n_blk - 1) + skip, j))
    if final:
        out_shape = jax.ShapeDtypeStruct((nb, n_blk * T_BLK, d), F32)
        out_spec = pl.BlockSpec((nb, T_BLK, d), lambda i: (0, jnp.maximum(i - lag, 0), 0))
    else:
        out_shape = jax.ShapeDtypeStruct((r, d), F32)
        out_spec = pl.BlockSpec((rb, d), lambda i: (jnp.maximum(i - lag, 0), 0))
    return pl.pallas_call(
        functools.partial(_back_kernel, alpha=alpha),
        out_shape=out_shape,
        grid=(n_blk + lag,),
        in_specs=[row(bw), row(bw, OFF_Y // bw), row(bw),
                  row(d, OFF_G // d), row(d, OFF_G // d + 1), row(d, OFF_G // d + 2), row(d),
                  _const_spec((bw, d), li), _const_spec((bw, d), li), _const_spec((bw, d), li),
                  _const_spec((d, d), li), _const_spec((1, d), li), _const_spec((1, d), li),
                  _const_spec((d, dff), li), _const_spec((d, dff), li), _const_spec((dff, d), li),
                  _const_spec((1, d), li), _const_spec((1, d), li)],
        out_specs=out_spec,
        scratch_shapes=[pltpu.VMEM((rb, d), F32), pltpu.VMEM((rb, d), F32)],
        compiler_params=_cparams(1, "arbitrary"),
        name="merge_ffn",
    )(om, zall, oh, zall, zall, zall, h, wm, ws, wh, wo, g1, b1, wg, wu, wd, g2, b2)


def _permute_w_in(w):
    mla_in = MLA_Q_RANK + MLA_KV_RANK + MLA_ROPE
    s5_0, hg_0 = mla_in, mla_in + S5_WIDTH
    g_0 = hg_0 + 2 * HG_QK + 2 * HG_VW
    pad0 = jnp.zeros((w.shape[0], MLA_NOPE), w.dtype)
    pad1 = jnp.zeros((w.shape[0], ZKR_W - MLA_NOPE - MLA_ROPE), w.dtype)
    return jnp.concatenate([w[:, s5_0:hg_0], w[:, 0:MLA_Q_RANK + MLA_KV_RANK], w[:, hg_0:g_0], w[:, g_0:],
                            pad0, w[:, MLA_Q_RANK + MLA_KV_RANK:mla_in], pad1], axis=1).astype(BF16)


def _mla_weights(w_uq, w_ukv):
    rq, rkv = w_uq.shape[0], w_ukv.shape[0]
    zpad = HEAD_PAD - MLA_NOPE - MLA_ROPE
    wq = w_uq.reshape(rq, MLA_HEADS, MLA_NOPE + MLA_ROPE)
    q_nope, q_rope = wq[..., :MLA_NOPE], wq[..., MLA_NOPE:]
    zq = jnp.zeros((rq, MLA_HEADS, zpad), w_uq.dtype)
    wq_p = jnp.concatenate([q_nope, q_rope, zq], axis=-1).reshape(rq, -1)
    wkv = w_ukv.reshape(rkv, MLA_HEADS, MLA_NOPE + MLA_V)
    zk = jnp.zeros((rkv, MLA_HEADS, HEAD_PAD - MLA_NOPE), w_ukv.dtype)
    wk_p = jnp.concatenate([wkv[..., :MLA_NOPE], zk], axis=-1).reshape(rkv, -1)
    wv = wkv[..., MLA_NOPE:].reshape(rkv, -1)
    return [a.astype(BF16) for a in (wq_p, wk_p, wv)]


def _rope_consts():
    half = MLA_ROPE // 2
    inv = ROPE_THETA ** (-(jnp.arange(0, MLA_ROPE, 2, dtype=F32) / MLA_ROPE))
    place = np.zeros((half, HEAD_PAD), np.float32)
    place[np.arange(half), MLA_NOPE + np.arange(half)] = 1.0
    place[np.arange(half), MLA_NOPE + half + np.arange(half)] = 1.0
    base = np.zeros((1, HEAD_PAD), np.float32)
    base[0, :MLA_NOPE] = 1.0
    return inv[:, None], jnp.asarray(place, BF16), jnp.asarray(base)


def _s5_params(lam_re, lam_im, log_dt, b_re, b_im, c_re, c_im):
    lr = jnp.minimum(lam_re.astype(F32), -1e-4)
    li = lam_im.astype(F32)
    dt = jnp.exp(log_dt.astype(F32))[:, None]
    mag = jnp.exp(lr * dt)
    ab_r = mag * jnp.cos(li * dt)
    ab_i = mag * jnp.sin(li * dt)
    den = lr * lr + li * li
    nr = ab_r - 1.0
    coef_r = ((nr * lr + ab_i * li) / den)[..., None]
    coef_i = ((ab_i * lr - nr * li) / den)[..., None]
    bb_r = coef_r * b_re.astype(F32) - coef_i * b_im.astype(F32)
    bb_i = coef_r * b_im.astype(F32) + coef_i * b_re.astype(F32)
    n_slab = S5_WIDTH // S5_SLAB
    gps = S5_SLAB // S5_GROUP
    eye = jnp.eye(gps, dtype=F32)

    def in_mat(bb):
        b4 = bb.reshape(n_slab, gps, S5_STATE, S5_GROUP)
        return jnp.einsum('jgnc,gh->jgchn', b4, eye).reshape(n_slab, S5_SLAB, S5_SLAB_STATE)

    def out_mat(cc):
        c4 = cc.astype(F32).reshape(n_slab, gps, S5_GROUP, S5_STATE)
        return jnp.einsum('jgcn,gh->jgnhc', c4, eye).reshape(n_slab, S5_SLAB_STATE, S5_SLAB)

    bm = jnp.concatenate([in_mat(bb_r), in_mat(bb_i)], axis=2).astype(BF16)
    cm = jnp.concatenate([out_mat(c_re), -out_mat(c_im)], axis=1).astype(BF16)
    ar = ab_r.reshape(n_slab, 1, S5_SLAB_STATE)
    ai = ab_i.reshape(n_slab, 1, S5_SLAB_STATE)
    return bm, cm, ar, ai


def kernel(x, positions, meta_tokens, ln_in_g, ln_in_b, w_in, mla_q_norm, mla_w_uq, mla_kv_norm, mla_w_ukv,
           s5_lam_re, s5_lam_im, s5_log_dt, s5_b_re, s5_b_im, s5_c_re, s5_c_im, s5_d, s5_w_glu,
           hg_lb_logits, hg_out_norm, w_br_mla, w_br_s5, w_br_hg, w_out, ln1_g, ln1_b,
           w_ffn_gate, w_ffn_up, w_ffn_down, ln2_g, ln2_b):
    nb, s, d = x.shape
    depth = w_in.shape[0]
    nblk = (s + N_META) // T_BLK
    alpha = (2 * depth) ** 0.25
    row2 = lambda a: a.astype(F32)[None, :]
    rows3 = lambda a: a.astype(F32)[:, None, :]
    bf16 = lambda a: a.astype(BF16)

    meta_pos = jnp.broadcast_to(jnp.arange(N_META, dtype=jnp.int32)[None, :], (nb, N_META))
    pos = jnp.concatenate([meta_pos, positions.astype(jnp.int32) + N_META], axis=1)
    pos = pos.reshape(nb, nblk, T_BLK).transpose(1, 0, 2).reshape(nblk, 1, nb * T_BLK)
    rope = _rope_consts()
    p_lb = jax.nn.softmax(hg_lb_logits.astype(F32), axis=0)
    lower_bounds = jnp.cumsum(p_lb, axis=0) - p_lb[0]
    tri = jnp.asarray(np.tril(np.ones((HG_CHUNK, HG_CHUNK), np.float32)), BF16)
    lvl = jnp.asarray(_hgrn_level_matrix())
    perm = jnp.asarray(_time_major_perm(nb), BF16)

    front_p = (jax.vmap(_permute_w_in)(w_in), pos, rope, rows3(mla_q_norm), rows3(mla_kv_norm),
               *jax.vmap(_mla_weights)(mla_w_uq, mla_w_ukv), perm,
               *jax.vmap(_s5_params)(s5_lam_re, s5_lam_im, s5_log_dt, s5_b_re, s5_b_im, s5_c_re, s5_c_im),
               rows3(s5_d), bf16(s5_w_glu))
    hgrn_p = (lower_bounds[:, None, :], rows3(hg_out_norm), tri, lvl)
    back_p = (bf16(w_br_mla), bf16(w_br_s5), bf16(w_br_hg), bf16(w_out), rows3(ln1_g), rows3(ln1_b),
              bf16(w_ffn_gate), bf16(w_ffn_up), bf16(w_ffn_down), rows3(ln2_g), rows3(ln2_b))

    h = (x, meta_tokens.astype(x.dtype), row2(ln_in_g), row2(ln_in_b))
    for li in range(depth):
        outs = _front(h, *front_p, nb, li)
        if li == 0:
            h, *outs = outs
        zall, = outs
        o_mla = _attention(zall, nb, nblk)
        o_hg = _hgrn(zall, *hgrn_p, nb, nblk, li)
        h = _back(o_mla, o_hg, zall, h, *back_p, nb, alpha, final=li == depth - 1, li=li)
    return h
```

```python
import functools
import math

import jax
import jax.numpy as jnp
import numpy as np
from jax import lax
from jax.experimental import pallas as pl
from jax.experimental.pallas import tpu as pltpu

F32 = jnp.float32
BF16 = jnp.bfloat16

N_META = 16
MLA_HEADS = 8
MLA_NOPE = 64
MLA_ROPE = 32
MLA_V = 64
MLA_Q_RANK = 256
MLA_KV_RANK = 256
ROPE_THETA = 10000.0
MASK_VALUE = -1e9
LOG2_E = math.log2(math.e)
HEAD_PAD = 128
S5_WIDTH = 512
S5_GROUP = 16
S5_GROUPS = S5_WIDTH // S5_GROUP
S5_STATE = 64
S5_SLAB = 128
S5_SLAB_STATE = (S5_SLAB // S5_GROUP) * S5_STATE
HG_HEADS = 4
HG_KEY = 128
HG_VAL = 128
HG_QK = HG_HEADS * HG_KEY
HG_VW = HG_HEADS * HG_VAL
HG_F_MIN = 1e-6
HG_CHUNK = 128
HG_LEVELS = (128, 64, 32, 16)
HG_BOTTOM = 8
HG_SLOTS = 3
N_BRANCH = 3
T_BLK = 16
ATT_TQ = 256
ATT_HEADS_PER_STEP = 4
ATT_STEPS_PER_BODY = 14
ATT_VE = MLA_V + 16
VMEM_LIMIT = 56 * 1024 * 1024

ZA_W = 1024
ZHG_W = 2048
ZG_W = 3072
ZKR_W = 128
Z_W = ZA_W + ZHG_W + ZG_W + ZKR_W
ZALL_WIDTHS = (ZHG_W, ZG_W, MLA_HEADS * HEAD_PAD, MLA_HEADS * HEAD_PAD, MLA_HEADS * MLA_V, S5_WIDTH)
ZALL_OFF = tuple(sum(ZALL_WIDTHS[:i]) for i in range(len(ZALL_WIDTHS)))
ZALL_W = sum(ZALL_WIDTHS)
OFF_HG, OFF_G, OFF_Q, OFF_K, OFF_V, OFF_Y = ZALL_OFF
WIDE_CHUNK = 512
FILL_BEFORE_MLA = 3


def _cparams(n_grid, sem="parallel"):
    return pltpu.CompilerParams(dimension_semantics=(sem,) * n_grid, vmem_limit_bytes=VMEM_LIMIT)


def _const_spec(shape, layer=None):
    nd = len(shape)
    if layer is None:
        return pl.BlockSpec(shape, lambda *_: (0,) * nd, pipeline_mode=pl.Buffered(1))
    return pl.BlockSpec((None,) + tuple(shape), lambda *_: (layer,) + (0,) * nd, pipeline_mode=pl.Buffered(1))


def _seq_spec(nblk, width, col):
    return pl.BlockSpec((nblk, None, T_BLK, width), lambda b, *g: (0, b, 0, col(b, *g)))


def _dot(a, b):
    return jnp.dot(a, b, preferred_element_type=F32)


def _dot_nt(a, b):
    return lax.dot_general(a, b, (((1,), (1,)), ((), ())), preferred_element_type=F32)


def _dot_tn(a, b):
    return lax.dot_general(a, b, (((0,), (0,)), ((), ())), preferred_element_type=F32)


def _sigmoid(x):
    return 1.0 / (1.0 + jnp.exp(-x))


def _layer_norm(x, g, b, eps=1e-5):
    mu = jnp.mean(x, axis=-1, keepdims=True)
    xc = x - mu
    var = jnp.mean(xc * xc, axis=-1, keepdims=True)
    return xc * lax.rsqrt(var + eps) * g + b


def _rms_norm(x, g, eps=1e-6):
    return x * lax.rsqrt(jnp.mean(x * x, axis=-1, keepdims=True) + eps) * g


def _load_rows(ref, blk0, n_blk, cs):
    x = ref[pl.ds(blk0, n_blk), :, cs]
    return x.reshape(n_blk * T_BLK, x.shape[-1])


def _store_rows(ref, blk0, n_blk, cs, x):
    ref[pl.ds(blk0, n_blk), :, cs] = x.reshape(n_blk, T_BLK, x.shape[-1])


def _rope_tables(pos_ref, inv_ref, place_ref, base_ref):
    ang = inv_ref[...] * pos_ref[...].astype(F32)

    def place(t):
        hi = t.astype(BF16)
        lo = (t - hi.astype(F32)).astype(BF16)
        return _dot_tn(hi, place_ref[...]) + _dot_tn(lo, place_ref[...])

    return place(jnp.cos(ang)) + base_ref[...], place(jnp.sin(ang))


def _mla_prep(cq, ckv, kr, cos, sin, qn_ref, kvn_ref, wq_ref, wk_ref, wv_ref, q_ref, k_ref, v_ref):
    scale = (MLA_NOPE + MLA_ROPE) ** -0.5 * LOG2_E
    half = MLA_ROPE // 2
    lane = lax.broadcasted_iota(jnp.int32, (1, HEAD_PAD), 1)
    sin_up = jnp.where((lane >= MLA_NOPE + half) & (lane < MLA_NOPE + MLA_ROPE), sin, 0.0)
    sin_dn = jnp.where((lane >= MLA_NOPE) & (lane < MLA_NOPE + half), -sin, 0.0)

    def rope(x, c, s_up, s_dn):
        return x * c + pltpu.roll(x, half, 1) * s_up + pltpu.roll(x, HEAD_PAD - half, 1) * s_dn

    cqn = _rms_norm(cq, qn_ref[...]).astype(BF16)
    ckvn = _rms_norm(ckv, kvn_ref[...]).astype(BF16)
    k_rope = rope(kr, cos, sin_up, sin_dn)
    cos_q, sin_up_q, sin_dn_q = cos * scale, sin_up * scale, sin_dn * scale
    for h in range(MLA_HEADS):
        cs = slice(h * HEAD_PAD, (h + 1) * HEAD_PAD)
        q_ref[:, cs] = rope(_dot(cqn, wq_ref[:, cs]), cos_q, sin_up_q, sin_dn_q).astype(BF16)
        k_ref[:, cs] = (_dot(ckvn, wk_ref[:, cs]) + k_rope).astype(BF16)
    v_ref[...] = _dot(ckvn, wv_ref[...]).astype(BF16)


def _s5_block(u, perm_ref, bm_ref, cm_ref, ar_ref, ai_ref, d_ref, wg_ref, o_ref,
              xr_ref, xi_ref, buf_ref, y_ref, nb, fillers):
    n_slab = S5_WIDTH // S5_SLAB
    ns = S5_SLAB_STATE
    u = _dot(perm_ref[...], u).astype(BF16)
    for j in range(n_slab):
        buf_ref[...] = _dot(u[:, j * S5_SLAB:(j + 1) * S5_SLAB], bm_ref[j])
        if fillers:
            fillers.pop(0)()
        ar = jnp.broadcast_to(ar_ref[j], (nb, ns))
        ai = jnp.broadcast_to(ai_ref[j], (nb, ns))
        xr = xr_ref[j]
        xi = xi_ref[j]
        for t in range(T_BLK):
            rows = slice(t * nb, (t + 1) * nb)
            nr = ar * xr - ai * xi + buf_ref[rows, 0:ns]
            ni = ar * xi + ai * xr + buf_ref[rows, ns:2 * ns]
            buf_ref[rows, 0:ns] = nr
            buf_ref[rows, ns:2 * ns] = ni
            xr, xi = nr, ni
        xr_ref[j] = xr
        xi_ref[j] = xi
        y_ref[:, j * S5_SLAB:(j + 1) * S5_SLAB] = _dot(buf_ref[...].astype(BF16), cm_ref[j])

    while fillers:
        fillers.pop(0)()
    y = y_ref[...] + d_ref[...] * u.astype(F32)
    y = 0.5 * y * (1.0 + jnp.tanh(math.sqrt(2.0 / math.pi) * (y + 0.044715 * (y * y * y))))
    gate = _sigmoid(_dot(y.astype(BF16), wg_ref[...]))
    out = (y * gate).astype(BF16)
    o_ref[...] = _dot_tn(perm_ref[...], out).astype(BF16)


def _front_kernel(*refs, nb, ln_in):
    if ln_in:
        x_ref, meta_ref, lng_ref, lnb_ref, *refs = refs
    else:
        h_ref, *refs = refs
    (w_ref, pos_ref, inv_ref, place_ref, base_ref, qn_ref, kvn_ref, wq_ref, wk_ref, wv_ref,
     perm_ref, bm_ref, cm_ref, ar_ref, ai_ref, d_ref, wglu_ref, *refs) = refs
    if ln_in:
        hout_ref, *refs = refs
    zall_ref, xr_ref, xi_ref, buf_ref, y_ref = refs
    zhg_ref, zg_ref, q_ref, k_ref, v_ref, ys5_ref = (
        zall_ref.at[:, o:o + w] for o, w in zip(ZALL_OFF, ZALL_WIDTHS))

    @pl.when(pl.program_id(0) == 0)
    def _():
        xr_ref[...] = jnp.zeros_like(xr_ref)
        xi_ref[...] = jnp.zeros_like(xi_ref)

    if ln_in:
        hx = _layer_norm(x_ref[...], lng_ref[...], lnb_ref[...])
        hm = _layer_norm(meta_ref[...], lng_ref[...], lnb_ref[...])
        h = jnp.where(pl.program_id(0) == 0, jnp.broadcast_to(hm[None], hx.shape), hx)
        h = h.reshape(hout_ref.shape)
        hout_ref[...] = h
    else:
        h = h_ref[...]
    x = h.astype(BF16)
    za = _dot(x, w_ref[:, 0:ZA_W])
    zkr = _dot(x, w_ref[:, ZA_W + ZHG_W + ZG_W:Z_W])

    def wide_chunk(o_ref, c, off):
        def run():
            o_ref[:, c:c + WIDE_CHUNK] = _dot(x, w_ref[:, off + c:off + c + WIDE_CHUNK]).astype(BF16)
        return run

    fillers = [wide_chunk(zhg_ref, c, ZA_W) for c in range(0, ZHG_W, WIDE_CHUNK)]
    fillers += [wide_chunk(zg_ref, c, ZA_W + ZHG_W) for c in range(0, ZG_W, WIDE_CHUNK)]
    for _ in range(FILL_BEFORE_MLA):
        fillers.pop(0)()
    cos, sin = _rope_tables(pos_ref, inv_ref, place_ref, base_ref)
    c0 = S5_WIDTH
    _mla_prep(za[:, c0:c0 + MLA_Q_RANK], za[:, c0 + MLA_Q_RANK:c0 + MLA_Q_RANK + MLA_KV_RANK], zkr,
              cos, sin, qn_ref, kvn_ref, wq_ref, wk_ref, wv_ref, q_ref, k_ref, v_ref)
    _s5_block(za[:, 0:S5_WIDTH].astype(BF16), perm_ref, bm_ref, cm_ref, ar_ref, ai_ref, d_ref, wglu_ref, ys5_ref,
              xr_ref, xi_ref, buf_ref, y_ref, nb, fillers)


def _front(src, w, pos, rope, qn, kvn, wq, wk, wv, perm, bm, cm, ar, ai, dskip, wglu, nb, li):
    ln_in = isinstance(src, tuple)
    rb = T_BLK * nb
    if ln_in:
        x, meta, lng, lnb = src
        d = x.shape[-1]
        r = (x.shape[1] + N_META) * nb
        src_specs = [pl.BlockSpec((nb, T_BLK, d), lambda i: (0, jnp.maximum(i - 1, 0), 0)),
                     _const_spec((N_META, d)), _const_spec((1, d)), _const_spec((1, d))]
    else:
        src = (src,)
        r, d = src[0].shape
        src_specs = [pl.BlockSpec((rb, d), lambda i: (i, 0))]
    hw = MLA_HEADS * HEAD_PAD
    vw = MLA_HEADS * MLA_V
    n_slab = S5_WIDTH // S5_SLAB
    row = lambda width: pl.BlockSpec((rb, width), lambda i: (i, 0))
    out_shape = [jax.ShapeDtypeStruct((r, ZALL_W), BF16)]
    out_specs = [row(ZALL_W)]
    if ln_in:
        out_shape.insert(0, jax.ShapeDtypeStruct((r, d), F32))
        out_specs.insert(0, row(d))
    return pl.pallas_call(
        functools.partial(_front_kernel, nb=nb, ln_in=ln_in),
        out_shape=out_shape,
        grid=(r // rb,),
        in_specs=src_specs + [
            _const_spec((d, Z_W), li), pl.BlockSpec((None, 1, rb), lambda i: (i, 0, 0)),
            _const_spec((MLA_ROPE // 2, 1)), _const_spec((MLA_ROPE // 2, HEAD_PAD)), _const_spec((1, HEAD_PAD)),
            _const_spec((1, MLA_Q_RANK), li), _const_spec((1, MLA_KV_RANK), li),
            _const_spec((MLA_Q_RANK, hw), li), _const_spec((MLA_KV_RANK, hw), li), _const_spec((MLA_KV_RANK, vw), li),
            _const_spec((rb, rb)),
            _const_spec((n_slab, S5_SLAB, 2 * S5_SLAB_STATE), li),
            _const_spec((n_slab, 2 * S5_SLAB_STATE, S5_SLAB), li),
            _const_spec((n_slab, 1, S5_SLAB_STATE), li),
            _const_spec((n_slab, 1, S5_SLAB_STATE), li),
            _const_spec((1, S5_WIDTH), li),
            _const_spec((S5_WIDTH, S5_WIDTH), li),
        ],
        out_specs=out_specs,
        scratch_shapes=[
            pltpu.VMEM((n_slab, nb, S5_SLAB_STATE), F32),
            pltpu.VMEM((n_slab, nb, S5_SLAB_STATE), F32),
            pltpu.VMEM((rb, 2 * S5_SLAB_STATE), F32),
            pltpu.VMEM((rb, S5_WIDTH), F32),
        ],
        compiler_params=_cparams(1, "arbitrary"),
        name="front",
    )(*src, w, pos, *rope, qn, kvn, wq, wk, wv, perm, bm, cm, ar, ai, dskip, wglu)


def _attn_kernel(tab_ref, q_ref, k_ref, v_ref, o_ref, vt_ref, qt_ref, m_ref, acc_ref, *bufs, n_heads, tq, nq):
    xs_ref, xm_ref, ps_ref, pm_ref, a_ref = (bufs[2 * i:2 * i + 2] for i in range(5))
    bpq = tq // T_BLK
    vw = n_heads * MLA_V

    def values_t(vb):
        vt = vb.T
        rows = lax.broadcasted_iota(jnp.int32, (ATT_VE - MLA_V, vt.shape[1]), 0)
        ext = jnp.where(rows == 0, 1.0, 0.0).astype(BF16)
        parts = []
        for h in range(n_heads):
            parts += [vt[h * MLA_V:(h + 1) * MLA_V, :], ext]
        return jnp.concatenate(parts, axis=0)

    for j in range(nq):
        vt_ref[j] = values_t(_load_rows(v_ref, 1 + j * bpq, bpq, slice(0, vw)))
    v0t = values_t(v_ref[0])

    for i in range(nq):
        qb = _load_rows(q_ref, 1 + i * bpq, bpq, slice(0, n_heads * HEAD_PAD))
        qt_ref[i] = qb.T

    def causal(st):
        r = lax.broadcasted_iota(jnp.int32, st.shape, 0)
        c = lax.broadcasted_iota(jnp.int32, st.shape, 1)
        return jnp.where(r <= c, st, MASK_VALUE * LOG2_E)

    heads = range(n_heads)
    qs = [slice(h * HEAD_PAD, (h + 1) * HEAD_PAD) for h in heads]
    vs = [slice(h * ATT_VE, (h + 1) * ATT_VE) for h in heads]
    k0 = [k_ref[0, :, c] for c in qs]

    def normalised(acc):
        return acc[0:MLA_V, :] / acc[MLA_V:MLA_V + 1, :]

    outs = []
    for n in heads:
        st = causal(_dot_nt(k0[n], q_ref[0, :, qs[n]]))
        p = jnp.exp2(st - jnp.max(st, axis=0, keepdims=True))
        outs.append(normalised(_dot(v0t[vs[n], :], p.astype(BF16))))
    o_ref[0, :, :] = jnp.concatenate(outs, axis=0).T.astype(BF16)

    def key_tile(j, n):
        return _load_rows(k_ref, 1 + j * bpq, bpq, qs[n])

    def diag_scores(i, s):
        for n in heads:
            xs_ref[s][n] = _dot(key_tile(i, n), qt_ref[i, qs[n], :])
            xm_ref[s][n] = _dot(k0[n], qt_ref[i, qs[n], :])

    def diag_stats(i, s):
        for n in heads:
            st, sm = causal(xs_ref[s][n]), xm_ref[s][n]
            m = jnp.maximum(jnp.max(st, axis=0, keepdims=True), jnp.max(sm, axis=0, keepdims=True))
            p = jnp.exp2(st - m)
            pm = jnp.exp2(sm - m)
            m_ref[i, n] = m
            ps_ref[s][n] = p.astype(BF16)
            pm_ref[s][n] = pm.astype(BF16)

    def diag_values(i, s):
        for n in heads:
            acc_ref[i, vs[n], :] = _dot(vt_ref[i, vs[n], :], ps_ref[s][n]) + _dot(v0t[vs[n], :], pm_ref[s][n])

    _pipeline3(nq, diag_scores, diag_stats, diag_values, steps_per_body=ATT_STEPS_PER_BODY)

    def off_scores(f, s):
        i, j = tab_ref[0, f], tab_ref[1, f]
        for n in heads:
            xs_ref[s][n] = _dot(key_tile(j, n), qt_ref[i, qs[n], :])

    def off_stats(f, s):
        i = tab_ref[0, f]
        for n in heads:
            x = xs_ref[s][n]
            m_old = m_ref[i, n]
            m = jnp.maximum(m_old, jnp.max(x, axis=0, keepdims=True))
            a = jnp.exp2(m_old - m)
            p = jnp.exp2(x - m)
            m_ref[i, n] = m
            a_ref[s][n] = a
            ps_ref[s][n] = p.astype(BF16)

    def off_values(f, s):
        i, j = tab_ref[0, f], tab_ref[1, f]
        for n in heads:
            acc_ref[i, vs[n], :] = a_ref[s][n] * acc_ref[i, vs[n], :] + _dot(vt_ref[j, vs[n], :], ps_ref[s][n])

    _pipeline3(nq * (nq - 1) // 2, off_scores, off_stats, off_values, steps_per_body=ATT_STEPS_PER_BODY)

    for i in range(nq):
        o = jnp.concatenate([normalised(acc_ref[i, vs[n], :]) for n in heads], axis=0)
        _store_rows(o_ref, 1 + i * bpq, bpq, slice(0, vw), o.astype(BF16).T)


def _pipeline3(n, stage1, stage2, stage3, n_slots=2, steps_per_body=None):
    if n == 0:
        return
    stage1(0, 0)
    if n > 1:
        stage1(1, 1 % n_slots)
    stage2(0, 0)

    def step(t, k):
        stage1(t + 2, (k + 2) % n_slots)
        stage2(t + 1, (k + 1) % n_slots)
        stage3(t, k)

    steady = max(n - 2, 0)
    spb = steps_per_body or n_slots
    trips = steady // spb
    if trips:
        def body(u, carry):
            for k in range(spb):
                step(spb * u + k, k % n_slots)
            return carry

        lax.fori_loop(0, trips, body, 0)
    for t in range(trips * spb, steady):
        step(t, t % n_slots)
    if n > 1:
        stage3(n - 2, (n - 2) % n_slots)
        stage2(n - 1, (n - 1) % n_slots)
    stage3(n - 1, (n - 1) % n_slots)


def _off_diagonal_order(nq):
    left = [(i, j) for i in range(nq) for j in range(i)]
    order = []
    while left:
        count = {}
        for i, _ in left:
            count[i] = count.get(i, 0) + 1
        ok = [p for p in left if not order or p[0] != order[-1][0]] or left
        pick = max(ok, key=lambda p: (count[p[0]], -p[1]))
        order.append(pick)
        left.remove(pick)
    return order or [(0, 0)]


def _attention(zall, nb, nblk):
    z4 = zall.reshape(nblk, nb, T_BLK, ZALL_W)
    hps = ATT_HEADS_PER_STEP
    n_hp = MLA_HEADS // hps
    qw = hps * HEAD_PAD
    vw = hps * MLA_V
    s = (nblk - 1) * T_BLK
    tq = min(ATT_TQ, s)
    nq = s // tq
    col = lambda b, p: p
    tab = jnp.asarray(np.array(_off_diagonal_order(nq), np.int32).T)
    o = pl.pallas_call(
        functools.partial(_attn_kernel, n_heads=hps, tq=tq, nq=nq),
        out_shape=jax.ShapeDtypeStruct((nblk, nb, T_BLK, MLA_HEADS * MLA_V), BF16),
        grid=(nb, n_hp),
        in_specs=[pl.BlockSpec(memory_space=pltpu.SMEM),
                  _seq_spec(nblk, qw, lambda b, p: OFF_Q // qw + p),
                  _seq_spec(nblk, qw, lambda b, p: OFF_K // qw + p),
                  _seq_spec(nblk, vw, lambda b, p: OFF_V // vw + p)],
        out_specs=_seq_spec(nblk, vw, col),
        scratch_shapes=[pltpu.VMEM((nq, hps * ATT_VE, tq), BF16), pltpu.VMEM((nq, qw, tq), BF16),
                        pltpu.VMEM((nq, hps, 1, tq), F32), pltpu.VMEM((nq, hps * ATT_VE, tq), F32)]
        + 2 * [pltpu.VMEM((hps, tq, tq), F32)] + 2 * [pltpu.VMEM((hps, N_META, tq), F32)]
        + 2 * [pltpu.VMEM((hps, tq, tq), BF16)] + 2 * [pltpu.VMEM((hps, N_META, tq), BF16)]
        + 2 * [pltpu.VMEM((hps, 1, tq), F32)],
        compiler_params=_cparams(2),
        name="mla_attn",
    )(tab, z4, z4, z4)
    return o.reshape(nblk * nb * T_BLK, MLA_HEADS * MLA_V)


def _time_major_perm(nb):
    p = np.zeros((T_BLK * nb, T_BLK * nb), np.float32)
    for b in range(nb):
        for t in range(T_BLK):
            p[t * nb + b, b * T_BLK + t] = 1.0
    return p


def _block_row_bcast(x, m, row):
    t, c = x.shape
    if m == t:
        return jnp.broadcast_to(x[row:row + 1, :], x.shape)
    x3 = x.reshape(t // m, m, c)
    return jnp.broadcast_to(x3[:, row:row + 1, :], x3.shape).reshape(t, c)


def _hgrn_gates(blk0, n_blk, f_ref, lb_ref):
    lb = lb_ref[...]
    zf = _load_rows(f_ref, blk0, n_blk, slice(0, HG_QK)).astype(F32)
    e = jnp.exp(-jnp.abs(zf))
    rcp = 1.0 / (1.0 + e)
    pos = zf >= 0.0
    sig_p = jnp.where(pos, rcp, e * rcp)
    sig_n = jnp.where(pos, e * rcp, rcp)
    f = lb + (1.0 - lb) * sig_p
    log_f = jnp.log2(jnp.maximum(f, HG_F_MIN))
    k = (1.0 - lb) * sig_n
    hi = log_f.astype(BF16)
    lo = (log_f - hi.astype(F32)).astype(BF16)
    return k, hi, lo


def _hgrn_mix(blk0, n_blk, gates, q_ref, v_ref, tri_ref, lvl_ref, st_ref):
    t = n_blk * T_BLK
    cs = slice(0, HG_QK)
    heads = [slice(h * HG_KEY, (h + 1) * HG_KEY) for h in range(HG_HEADS)]
    k, hi, lo = gates
    q = _load_rows(q_ref, blk0, n_blk, cs).astype(F32)
    v = _load_rows(v_ref, blk0, n_blk, cs)
    tri = tri_ref[0:t, 0:t]
    cum = _dot(tri, hi) + _dot(tri, lo)
    lvl = lvl_ref[0:t, 0:t]
    r_idx = lax.broadcasted_iota(jnp.int32, (t, HG_QK), 0)
    c8 = _block_row_bcast(cum, HG_BOTTOM, HG_BOTTOM // 2 - 1)
    qe = (q * jnp.exp2(cum - c8)).astype(BF16)
    ke = (k * jnp.exp2(c8 - cum)).astype(BF16)
    n_lvl = len(HG_LEVELS)
    scores = [jnp.where(lvl == n_lvl, _dot_nt(qe[:, hs], ke[:, hs]), 0.0) for hs in heads]
    for li, m in enumerate(HG_LEVELS):
        if m > t:
            continue
        half = m // 2
        cmid = _block_row_bcast(cum, m, half - 1)
        upper = (r_idx & (m - 1)) >= half
        ex = jnp.exp2(jnp.where(upper, cum - cmid, cmid - cum))
        qe = jnp.where(upper, q * ex, 0.0).astype(BF16)
        ke = jnp.where(upper, 0.0, k * ex).astype(BF16)
        scores = [jnp.where(lvl == li, _dot_nt(qe[:, hs], ke[:, hs]), s) for s, hs in zip(scores, heads)]
    qd = (q * jnp.exp2(cum)).astype(BF16)
    last = cum[t - 1:t, :]
    kd = (k * jnp.exp2(last - cum)).astype(BF16)
    dec = jnp.exp2(last)
    outs = []
    for h, hs in enumerate(heads):
        st = st_ref[h]
        outs.append(_dot(scores[h].astype(BF16), v[:, hs]) + _dot_nt(qd[:, hs], st.astype(BF16)))
        st_ref[h] = st * dec[:, hs] + _dot_tn(v[:, hs], kd[:, hs])
    return tuple(outs)


def _hgrn_out(blk0, n_blk, outs, g_ref, on_ref, o_ref):
    cs = slice(0, HG_VW)
    g = _load_rows(g_ref, blk0, n_blk, cs).astype(F32)
    o = jnp.concatenate([o * lax.rsqrt(jnp.mean(o * o, axis=-1, keepdims=True) + 1e-6) for o in outs], axis=1)
    _store_rows(o_ref, blk0, n_blk, cs, (o * on_ref[...] * (g * _sigmoid(g))).astype(BF16))


def _hgrn_kernel(q_ref, f_ref, v_ref, g_ref, lb_ref, on_ref, tri_ref, lvl_ref, o_ref, st_ref, *bufs, n_chunks):
    kb_ref, hi_ref, lo_ref, ob_ref = (bufs[HG_SLOTS * i:HG_SLOTS * (i + 1)] for i in range(4))
    bpc = HG_CHUNK // T_BLK
    heads = [slice(h * HG_VAL, (h + 1) * HG_VAL) for h in range(HG_HEADS)]
    gates = functools.partial(_hgrn_gates, f_ref=f_ref, lb_ref=lb_ref)
    mix = functools.partial(_hgrn_mix, q_ref=q_ref, v_ref=v_ref, tri_ref=tri_ref, lvl_ref=lvl_ref, st_ref=st_ref)
    out = functools.partial(_hgrn_out, g_ref=g_ref, on_ref=on_ref, o_ref=o_ref)
    st_ref[...] = jnp.zeros_like(st_ref)
    out(0, 1, mix(0, 1, gates(0, 1)))

    def stage_gates(c, s):
        k, hi, lo = gates(1 + c * bpc, bpc)
        kb_ref[s][...], hi_ref[s][...], lo_ref[s][...] = k, hi.astype(F32), lo.astype(F32)

    def stage_mix(c, s):
        outs = mix(1 + c * bpc, bpc, (kb_ref[s][...], hi_ref[s][...].astype(BF16), lo_ref[s][...].astype(BF16)))
        for hs, o in zip(heads, outs):
            ob_ref[s][:, hs] = o

    def stage_out(c, s):
        out(1 + c * bpc, bpc, tuple(ob_ref[s][:, hs] for hs in heads))

    _pipeline3(n_chunks, stage_gates, stage_mix, stage_out, n_slots=HG_SLOTS)


def _hgrn(zall, lb, onorm, tri, lvl, nb, nblk, li):
    z4 = zall.reshape(nblk, nb, T_BLK, ZALL_W)
    col = lambda j: _seq_spec(nblk, HG_QK, lambda b: OFF_HG // HG_QK + j)
    o = pl.pallas_call(
        functools.partial(_hgrn_kernel, n_chunks=(nblk - 1) * T_BLK // HG_CHUNK),
        out_shape=jax.ShapeDtypeStruct((nblk, nb, T_BLK, HG_VW), BF16),
        grid=(nb,),
        in_specs=[col(0), col(1), col(2), col(3),
                  _const_spec((1, HG_QK), li), _const_spec((1, HG_VW), li),
                  _const_spec((HG_CHUNK, HG_CHUNK)), _const_spec((HG_CHUNK, HG_CHUNK))],
        out_specs=_seq_spec(nblk, HG_VW, lambda b: 0),
        scratch_shapes=[pltpu.VMEM((HG_HEADS, HG_VAL, HG_KEY), F32)]
        + 3 * HG_SLOTS * [pltpu.VMEM((HG_CHUNK, HG_QK), F32)]
        + HG_SLOTS * [pltpu.VMEM((HG_CHUNK, HG_VW), F32)],
        compiler_params=_cparams(1),
        name="hgrn2",
    )(z4, z4, z4, z4, lb, onorm, tri, lvl)
    return o.reshape(nblk * nb * T_BLK, HG_VW)


def _hgrn_level_matrix():
    r = np.arange(HG_CHUNK)[:, None]
    c = np.arange(HG_CHUNK)[None, :]
    lvl = np.zeros((HG_CHUNK, HG_CHUNK), np.int32)
    for li, m in enumerate(HG_LEVELS):
        lvl = np.where(r // m == c // m, li, lvl)
    lvl = np.where(r // HG_BOTTOM == c // HG_BOTTOM, len(HG_LEVELS), lvl)
    return np.where(c <= r, lvl, -1).astype(np.int32)


def _back_kernel(om_ref, os_ref, oh_ref, gm_ref, gs_ref, gh_ref, h_ref,
                 wm_ref, ws_ref, wh_ref, wo_ref, g1_ref, b1_ref,
                 wg_ref, wu_ref, wd_ref, g2_ref, b2_ref, o_ref, r1_ref, r2_ref, *, alpha):
    @pl.when(pl.program_id(0) == 0)
    def _():
        r1_ref[...] = jnp.zeros_like(r1_ref)
        r2_ref[...] = jnp.zeros_like(r2_ref)

    r1 = r1_ref[...]
    r2 = r2_ref[...]
    ym = _dot(om_ref[...], wm_ref[...])
    ys = _dot(os_ref[...], ws_ref[...])
    yh = _dot(oh_ref[...], wh_ref[...])
    o_ref[...] = _layer_norm(r2, g2_ref[...], b2_ref[...]).reshape(o_ref.shape)
    h1 = _layer_norm(r1, g1_ref[...], b1_ref[...])
    hb = h1.astype(BF16)
    a = _dot(hb, wg_ref[...])
    u = _dot(hb, wu_ref[...])
    mixed = _sigmoid(gm_ref[...].astype(F32)) * ym
    mixed += _sigmoid(gs_ref[...].astype(F32)) * ys
    mixed += _sigmoid(gh_ref[...].astype(F32)) * yh
    r1_ref[...] = alpha * h_ref[...] + _dot(mixed.astype(BF16), wo_ref[...])
    r2_ref[...] = alpha * h1 + _dot((a * _sigmoid(a) * u).astype(BF16), wd_ref[...])


def _back(om, oh, zall, h, wm, ws, wh, wo, g1, b1, wg, wu, wd, g2, b2, nb, alpha, final, li):
    r, d = h.shape
    bw = om.shape[1]
    dff = wg.shape[-1]
    rb = T_BLK * nb
    skip = 1 if final else 0
    n_blk = r // rb - skip
    lag = 2
    row = lambda width, j=0: pl.BlockSpec((rb, width), lambda i: (jnp.minimum(i, n_blk - 1) + skip, j))
    if final:
        out_shape = jax.ShapeDtypeStruct((nb, n_blk * T_BLK, d), F32)
        out_spec = pl.BlockSpec((nb, T_BLK, d), lambda i: (0, jnp.maximum(i - lag, 0), 0))
    else:
        out_shape = jax.ShapeDtypeStruct((r, d), F32)
        out_spec = pl.BlockSpec((rb, d), lambda i: (jnp.maximum(i - lag, 0), 0))
    return pl.pallas_call(
        functools.partial(_back_kernel, alpha=alpha),
        out_shape=out_shape,
        grid=(n_blk + lag,),
        in_specs=[row(bw), row(bw, OFF_Y // bw), row(bw),
                  row(d, OFF_G // d), row(d, OFF_G // d + 1), row(d, OFF_G // d + 2), row(d),
                  _const_spec((bw, d), li), _const_spec((bw, d), li), _const_spec((bw, d), li),
                  _const_spec((d, d), li), _const_spec((1, d), li), _const_spec((1, d), li),
                  _const_spec((d, dff), li), _const_spec((d, dff), li), _const_spec((dff, d), li),
                  _const_spec((1, d), li), _const_spec((1, d), li)],
        out_specs=out_spec,
        scratch_shapes=[pltpu.VMEM((rb, d), F32), pltpu.VMEM((rb, d), F32)],
        compiler_params=_cparams(1, "arbitrary"),
        name="merge_ffn",
    )(om, zall, oh, zall, zall, zall, h, wm, ws, wh, wo, g1, b1, wg, wu, wd, g2, b2)


def _permute_w_in(w):
    mla_in = MLA_Q_RANK + MLA_KV_RANK + MLA_ROPE
    s5_0, hg_0 = mla_in, mla_in + S5_WIDTH
    g_0 = hg_0 + 2 * HG_QK + 2 * HG_VW
    pad0 = jnp.zeros((w.shape[0], MLA_NOPE), w.dtype)
    pad1 = jnp.zeros((w.shape[0], ZKR_W - MLA_NOPE - MLA_ROPE), w.dtype)
    return jnp.concatenate([w[:, s5_0:hg_0], w[:, 0:MLA_Q_RANK + MLA_KV_RANK], w[:, hg_0:g_0], w[:, g_0:],
                            pad0, w[:, MLA_Q_RANK + MLA_KV_RANK:mla_in], pad1], axis=1).astype(BF16)


def _mla_weights(w_uq, w_ukv):
    rq, rkv = w_uq.shape[0], w_ukv.shape[0]
    zpad = HEAD_PAD - MLA_NOPE - MLA_ROPE
    wq = w_uq.reshape(rq, MLA_HEADS, MLA_NOPE + MLA_ROPE)
    q_nope, q_rope = wq[..., :MLA_NOPE], wq[..., MLA_NOPE:]
    zq = jnp.zeros((rq, MLA_HEADS, zpad), w_uq.dtype)
    wq_p = jnp.concatenate([q_nope, q_rope, zq], axis=-1).reshape(rq, -1)
    wkv = w_ukv.reshape(rkv, MLA_HEADS, MLA_NOPE + MLA_V)
    zk = jnp.zeros((rkv, MLA_HEADS, HEAD_PAD - MLA_NOPE), w_ukv.dtype)
    wk_p = jnp.concatenate([wkv[..., :MLA_NOPE], zk], axis=-1).reshape(rkv, -1)
    wv = wkv[..., MLA_NOPE:].reshape(rkv, -1)
    return [a.astype(BF16) for a in (wq_p, wk_p, wv)]


def _rope_consts():
    half = MLA_ROPE // 2
    inv = ROPE_THETA ** (-(jnp.arange(0, MLA_ROPE, 2, dtype=F32) / MLA_ROPE))
    place = np.zeros((half, HEAD_PAD), np.float32)
    place[np.arange(half), MLA_NOPE + np.arange(half)] = 1.0
    place[np.arange(half), MLA_NOPE + half + np.arange(half)] = 1.0
    base = np.zeros((1, HEAD_PAD), np.float32)
    base[0, :MLA_NOPE] = 1.0
    return inv[:, None], jnp.asarray(place, BF16), jnp.asarray(base)


def _s5_params(lam_re, lam_im, log_dt, b_re, b_im, c_re, c_im):
    lr = jnp.minimum(lam_re.astype(F32), -1e-4)
    li = lam_im.astype(F32)
    dt = jnp.exp(log_dt.astype(F32))[:, None]
    mag = jnp.exp(lr * dt)
    ab_r = mag * jnp.cos(li * dt)
    ab_i = mag * jnp.sin(li * dt)
    den = lr * lr + li * li
    nr = ab_r - 1.0
    coef_r = ((nr * lr + ab_i * li) / den)[..., None]
    coef_i = ((ab_i * lr - nr * li) / den)[..., None]
    bb_r = coef_r * b_re.astype(F32) - coef_i * b_im.astype(F32)
    bb_i = coef_r * b_im.astype(F32) + coef_i * b_re.astype(F32)
    n_slab = S5_WIDTH // S5_SLAB
    gps = S5_SLAB // S5_GROUP
    eye = jnp.eye(gps, dtype=F32)

    def in_mat(bb):
        b4 = bb.reshape(n_slab, gps, S5_STATE, S5_GROUP)
        return jnp.einsum('jgnc,gh->jgchn', b4, eye).reshape(n_slab, S5_SLAB, S5_SLAB_STATE)

    def out_mat(cc):
        c4 = cc.astype(F32).reshape(n_slab, gps, S5_GROUP, S5_STATE)
        return jnp.einsum('jgcn,gh->jgnhc', c4, eye).reshape(n_slab, S5_SLAB_STATE, S5_SLAB)

    bm = jnp.concatenate([in_mat(bb_r), in_mat(bb_i)], axis=2).astype(BF16)
    cm = jnp.concatenate([out_mat(c_re), -out_mat(c_im)], axis=1).astype(BF16)
    ar = ab_r.reshape(n_slab, 1, S5_SLAB_STATE)
    ai = ab_i.reshape(n_slab, 1, S5_SLAB_STATE)
    return bm, cm, ar, ai


def kernel(x, positions, meta_tokens, ln_in_g, ln_in_b, w_in, mla_q_norm, mla_w_uq, mla_kv_norm, mla_w_ukv,
           s5_lam_re, s5_lam_im, s5_log_dt, s5_b_re, s5_b_im, s5_c_re, s5_c_im, s5_d, s5_w_glu,
           hg_lb_logits, hg_out_norm, w_br_mla, w_br_s5, w_br_hg, w_out, ln1_g, ln1_b,
           w_ffn_gate, w_ffn_up, w_ffn_down, ln2_g, ln2_b):
    nb, s, d = x.shape
    depth = w_in.shape[0]
    nblk = (s + N_META) // T_BLK
    alpha = (2 * depth) ** 0.25
    row2 = lambda a: a.astype(F32)[None, :]
    rows3 = lambda a: a.astype(F32)[:, None, :]
    bf16 = lambda a: a.astype(BF16)

    meta_pos = jnp.broadcast_to(jnp.arange(N_META, dtype=jnp.int32)[None, :], (nb, N_META))
    pos = jnp.concatenate([meta_pos, positions.astype(jnp.int32) + N_META], axis=1)
    pos = pos.reshape(nb, nblk, T_BLK).transpose(1, 0, 2).reshape(nblk, 1, nb * T_BLK)
    rope = _rope_consts()
    p_lb = jax.nn.softmax(hg_lb_logits.astype(F32), axis=0)
    lower_bounds = jnp.cumsum(p_lb, axis=0) - p_lb[0]
    tri = jnp.asarray(np.tril(np.ones((HG_CHUNK, HG_CHUNK), np.float32)), BF16)
    lvl = jnp.asarray(_hgrn_level_matrix())
    perm = jnp.asarray(_time_major_perm(nb), BF16)

    front_p = (jax.vmap(_permute_w_in)(w_in), pos, rope, rows3(mla_q_norm), rows3(mla_kv_norm),
               *jax.vmap(_mla_weights)(mla_w_uq, mla_w_ukv), perm,
               *jax.vmap(_s5_params)(s5_lam_re, s5_lam_im, s5_log_dt, s5_b_re, s5_b_im, s5_c_re, s5_c_im),
               rows3(s5_d), bf16(s5_w_glu))
    hgrn_p = (lower_bounds[:, None, :], rows3(hg_out_norm), tri, lvl)
    back_p = (bf16(w_br_mla), bf16(w_br_s5), bf16(w_br_hg), bf16(w_out), rows3(ln1_g), rows3(ln1_b),
              bf16(w_ffn_gate), bf16(w_ffn_up), bf16(w_ffn_down), rows3(ln2_g), rows3(ln2_b))

    h = (x, meta_tokens.astype(x.dtype), row2(ln_in_g), row2(ln_in_b))
    for li in range(depth):
        outs = _front(h, *front_p, nb, li)
        if li == 0:
            h, *outs = outs
        zall, = outs
        o_mla = _attention(zall, nb, nblk)
        o_hg = _hgrn(zall, *hgrn_p, nb, nblk, li)
        h = _back(o_mla, o_hg, zall, h, *back_p, nb, alpha, final=li == depth - 1, li=li)
    return h
```

```python
import functools
import math

import jax
import jax.numpy as jnp
import numpy as np
from jax import lax
from jax.experimental import pallas as pl
from jax.experimental.pallas import tpu as pltpu

F32 = jnp.float32
BF16 = jnp.bfloat16

N_META = 16
MLA_HEADS = 8
MLA_NOPE = 64
MLA_ROPE = 32
MLA_V = 64
MLA_Q_RANK = 256
MLA_KV_RANK = 256
ROPE_THETA = 10000.0
MASK_VALUE = -1e9
LOG2_E = math.log2(math.e)
HEAD_PAD = 128
S5_WIDTH = 512
S5_GROUP = 16
S5_STATE = 64
S5_SLAB = 128
S5_SLAB_STATE = (S5_SLAB // S5_GROUP) * S5_STATE
HG_HEADS = 4
HG_KEY = 128
HG_VAL = 128
HG_QK = HG_HEADS * HG_KEY
HG_VW = HG_HEADS * HG_VAL
HG_F_MIN = 1e-6
HG_CHUNK = 128
HG_LEVELS = (128, 64, 32, 16)
HG_BOTTOM = 8
HG_SLOTS = 3
T_BLK = 16
ATT_TQ = 256
ATT_HEADS_PER_STEP = 4
ATT_STEPS_PER_BODY = 14
ATT_VE = MLA_V + 16
VMEM_LIMIT = 56 * 1024 * 1024

ZA_W = 1024
ZHG_W = 2048
ZG_W = 3072
ZKR_W = 128
Z_W = ZA_W + ZHG_W + ZG_W + ZKR_W
ZALL_WIDTHS = (ZHG_W, ZG_W, MLA_HEADS * HEAD_PAD, MLA_HEADS * HEAD_PAD, MLA_HEADS * MLA_V, S5_WIDTH)
ZALL_OFF = tuple(sum(ZALL_WIDTHS[:i]) for i in range(len(ZALL_WIDTHS)))
ZALL_W = sum(ZALL_WIDTHS)
OFF_HG, OFF_G, OFF_Q, OFF_K, OFF_V, OFF_Y = ZALL_OFF
WIDE_CHUNK = 512
FILL_BEFORE_MLA = 3


def _cparams(n_grid, sem="parallel"):
    return pltpu.CompilerParams(dimension_semantics=(sem,) * n_grid, vmem_limit_bytes=VMEM_LIMIT)


def _const_spec(shape, layer=None):
    nd = len(shape)
    if layer is None:
        return pl.BlockSpec(shape, lambda *_: (0,) * nd, pipeline_mode=pl.Buffered(1))
    return pl.BlockSpec((None,) + tuple(shape), lambda *_: (layer,) + (0,) * nd, pipeline_mode=pl.Buffered(1))


def _seq_spec(nblk, width, col):
    return pl.BlockSpec((nblk, None, T_BLK, width), lambda b, *g: (0, b, 0, col(b, *g)))


def _dot(a, b):
    return jnp.dot(a, b, preferred_element_type=F32)


def _dot_nt(a, b):
    return lax.dot_general(a, b, (((1,), (1,)), ((), ())), preferred_element_type=F32)


def _dot_tn(a, b):
    return lax.dot_general(a, b, (((0,), (0,)), ((), ())), preferred_element_type=F32)


def _sigmoid(x):
    return 1.0 / (1.0 + jnp.exp(-x))


def _layer_norm(x, g, b, eps=1e-5):
    mu = jnp.mean(x, axis=-1, keepdims=True)
    xc = x - mu
    var = jnp.mean(xc * xc, axis=-1, keepdims=True)
    return xc * lax.rsqrt(var + eps) * g + b


def _rms_norm(x, g, eps=1e-6):
    return x * lax.rsqrt(jnp.mean(x * x, axis=-1, keepdims=True) + eps) * g


def _load_rows(ref, blk0, n_blk, cs):
    x = ref[pl.ds(blk0, n_blk), :, cs]
    return x.reshape(n_blk * T_BLK, x.shape[-1])


def _store_rows(ref, blk0, n_blk, cs, x):
    ref[pl.ds(blk0, n_blk), :, cs] = x.reshape(n_blk, T_BLK, x.shape[-1])


def _rope_tables(pos_ref, inv_ref, place_ref, base_ref):
    ang = inv_ref[...] * pos_ref[...].astype(F32)

    def place(t):
        hi = t.astype(BF16)
        lo = (t - hi.astype(F32)).astype(BF16)
        return _dot_tn(hi, place_ref[...]) + _dot_tn(lo, place_ref[...])

    return place(jnp.cos(ang)) + base_ref[...], place(jnp.sin(ang))


def _mla_prep(cq, ckv, kr, cos, sin, qn_ref, kvn_ref, wq_ref, wk_ref, wv_ref, q_ref, k_ref, v_ref):
    scale = (MLA_NOPE + MLA_ROPE) ** -0.5 * LOG2_E
    half = MLA_ROPE // 2
    lane = lax.broadcasted_iota(jnp.int32, (1, HEAD_PAD), 1)
    sin_up = jnp.where((lane >= MLA_NOPE + half) & (lane < MLA_NOPE + MLA_ROPE), sin, 0.0)
    sin_dn = jnp.where((lane >= MLA_NOPE) & (lane < MLA_NOPE + half), -sin, 0.0)

    def rope(x, c, s_up, s_dn):
        return x * c + pltpu.roll(x, half, 1) * s_up + pltpu.roll(x, HEAD_PAD - half, 1) * s_dn

    cqn = _rms_norm(cq, qn_ref[...]).astype(BF16)
    ckvn = _rms_norm(ckv, kvn_ref[...]).astype(BF16)
    k_rope = rope(kr, cos, sin_up, sin_dn)
    cos_q, sin_up_q, sin_dn_q = cos * scale, sin_up * scale, sin_dn * scale
    for h in range(MLA_HEADS):
        cs = slice(h * HEAD_PAD, (h + 1) * HEAD_PAD)
        q_ref[:, cs] = rope(_dot(cqn, wq_ref[:, cs]), cos_q, sin_up_q, sin_dn_q).astype(BF16)
        k_ref[:, cs] = (_dot(ckvn, wk_ref[:, cs]) + k_rope).astype(BF16)
    v_ref[...] = _dot(ckvn, wv_ref[...]).astype(BF16)


def _s5_block(u, perm_ref, bm_ref, cm_ref, ar_ref, ai_ref, d_ref, wg_ref, o_ref,
              xr_ref, xi_ref, buf_ref, y_ref, nb, fillers):
    n_slab = S5_WIDTH // S5_SLAB
    ns = S5_SLAB_STATE
    u = _dot(perm_ref[...], u).astype(BF16)
    for j in range(n_slab):
        buf_ref[...] = _dot(u[:, j * S5_SLAB:(j + 1) * S5_SLAB], bm_ref[j])
        if fillers:
            fillers.pop(0)()
        ar = jnp.broadcast_to(ar_ref[j], (nb, ns))
        ai = jnp.broadcast_to(ai_ref[j], (nb, ns))
        xr = xr_ref[j]
        xi = xi_ref[j]
        for t in range(T_BLK):
            rows = slice(t * nb, (t + 1) * nb)
            nr = ar * xr - ai * xi + buf_ref[rows, 0:ns]
            ni = ar * xi + ai * xr + buf_ref[rows, ns:2 * ns]
            buf_ref[rows, 0:ns] = nr
            buf_ref[rows, ns:2 * ns] = ni
            xr, xi = nr, ni
        xr_ref[j] = xr
        xi_ref[j] = xi
        y_ref[:, j * S5_SLAB:(j + 1) * S5_SLAB] = _dot(buf_ref[...].astype(BF16), cm_ref[j])

    while fillers:
        fillers.pop(0)()
    y = y_ref[...] + d_ref[...] * u.astype(F32)
    y = 0.5 * y * (1.0 + jnp.tanh(math.sqrt(2.0 / math.pi) * (y + 0.044715 * (y * y * y))))
    gate = _sigmoid(_dot(y.astype(BF16), wg_ref[...]))
    out = (y * gate).astype(BF16)
    o_ref[...] = _dot_tn(perm_ref[...], out).astype(BF16)


def _front_kernel(*refs, nb, ln_in):
    if ln_in:
        x_ref, meta_ref, lng_ref, lnb_ref, *refs = refs
    else:
        h_ref, *refs = refs
    (w_ref, pos_ref, inv_ref, place_ref, base_ref, qn_ref, kvn_ref, wq_ref, wk_ref, wv_ref,
     perm_ref, bm_ref, cm_ref, ar_ref, ai_ref, d_ref, wglu_ref, *refs) = refs
    if ln_in:
        hout_ref, *refs = refs
    zall_ref, xr_ref, xi_ref, buf_ref, y_ref = refs
    zhg_ref, zg_ref, q_ref, k_ref, v_ref, ys5_ref = (
        zall_ref.at[:, o:o + w] for o, w in zip(ZALL_OFF, ZALL_WIDTHS))

    @pl.when(pl.program_id(0) == 0)
    def _():
        xr_ref[...] = jnp.zeros_like(xr_ref)
        xi_ref[...] = jnp.zeros_like(xi_ref)

    if ln_in:
        hx = _layer_norm(x_ref[...], lng_ref[...], lnb_ref[...])
        hm = _layer_norm(meta_ref[...], lng_ref[...], lnb_ref[...])
        h = jnp.where(pl.program_id(0) == 0, jnp.broadcast_to(hm[None], hx.shape), hx)
        h = h.reshape(hout_ref.shape)
        hout_ref[...] = h
    else:
        h = h_ref[...]
    x = h.astype(BF16)
    za = _dot(x, w_ref[:, 0:ZA_W])
    zkr = _dot(x, w_ref[:, ZA_W + ZHG_W + ZG_W:Z_W])

    def wide_chunk(o_ref, c, off):
        def run():
            o_ref[:, c:c + WIDE_CHUNK] = _dot(x, w_ref[:, off + c:off + c + WIDE_CHUNK]).astype(BF16)
        return run

    fillers = [wide_chunk(zhg_ref, c, ZA_W) for c in range(0, ZHG_W, WIDE_CHUNK)]
    fillers += [wide_chunk(zg_ref, c, ZA_W + ZHG_W) for c in range(0, ZG_W, WIDE_CHUNK)]
    for _ in range(FILL_BEFORE_MLA):
        fillers.pop(0)()
    cos, sin = _rope_tables(pos_ref, inv_ref, place_ref, base_ref)
    c0 = S5_WIDTH
    _mla_prep(za[:, c0:c0 + MLA_Q_RANK], za[:, c0 + MLA_Q_RANK:c0 + MLA_Q_RANK + MLA_KV_RANK], zkr,
              cos, sin, qn_ref, kvn_ref, wq_ref, wk_ref, wv_ref, q_ref, k_ref, v_ref)
    _s5_block(za[:, 0:S5_WIDTH].astype(BF16), perm_ref, bm_ref, cm_ref, ar_ref, ai_ref, d_ref, wglu_ref, ys5_ref,
              xr_ref, xi_ref, buf_ref, y_ref, nb, fillers)


def _front(src, w, pos, rope, qn, kvn, wq, wk, wv, perm, bm, cm, ar, ai, dskip, wglu, nb, li):
    ln_in = isinstance(src, tuple)
    rb = T_BLK * nb
    if ln_in:
        x, meta, lng, lnb = src
        d = x.shape[-1]
        r = (x.shape[1] + N_META) * nb
        src_specs = [pl.BlockSpec((nb, T_BLK, d), lambda i: (0, jnp.maximum(i - 1, 0), 0)),
                     _const_spec((N_META, d)), _const_spec((1, d)), _const_spec((1, d))]
    else:
        src = (src,)
        r, d = src[0].shape
        src_specs = [pl.BlockSpec((rb, d), lambda i: (i, 0))]
    hw = MLA_HEADS * HEAD_PAD
    vw = MLA_HEADS * MLA_V
    n_slab = S5_WIDTH // S5_SLAB
    row = lambda width: pl.BlockSpec((rb, width), lambda i: (i, 0))
    out_shape = [jax.ShapeDtypeStruct((r, ZALL_W), BF16)]
    out_specs = [row(ZALL_W)]
    if ln_in:
        out_shape.insert(0, jax.ShapeDtypeStruct((r, d), F32))
        out_specs.insert(0, row(d))
    return pl.pallas_call(
        functools.partial(_front_kernel, nb=nb, ln_in=ln_in),
        out_shape=out_shape,
        grid=(r // rb,),
        in_specs=src_specs + [
            _const_spec((d, Z_W), li), pl.BlockSpec((None, 1, rb), lambda i: (i, 0, 0)),
            _const_spec((MLA_ROPE // 2, 1)), _const_spec((MLA_ROPE // 2, HEAD_PAD)), _const_spec((1, HEAD_PAD)),
            _const_spec((1, MLA_Q_RANK), li), _const_spec((1, MLA_KV_RANK), li),
            _const_spec((MLA_Q_RANK, hw), li), _const_spec((MLA_KV_RANK, hw), li), _const_spec((MLA_KV_RANK, vw), li),
            _const_spec((rb, rb)),
            _const_spec((n_slab, S5_SLAB, 2 * S5_SLAB_STATE), li),
            _const_spec((n_slab, 2 * S5_SLAB_STATE, S5_SLAB), li),
            _const_spec((n_slab, 1, S5_SLAB_STATE), li),
            _const_spec((n_slab, 1, S5_SLAB_STATE), li),
            _const_spec((1, S5_WIDTH), li),
            _const_spec((S5_WIDTH, S5_WIDTH), li),
        ],
        out_specs=out_specs,
        scratch_shapes=[
            pltpu.VMEM((n_slab, nb, S5_SLAB_STATE), F32),
            pltpu.VMEM((n_slab, nb, S5_SLAB_STATE), F32),
            pltpu.VMEM((rb, 2 * S5_SLAB_STATE), F32),
            pltpu.VMEM((rb, S5_WIDTH), F32),
        ],
        compiler_params=_cparams(1, "arbitrary"),
        name="front",
    )(*src, w, pos, *rope, qn, kvn, wq, wk, wv, perm, bm, cm, ar, ai, dskip, wglu)


def _attn_kernel(tab_ref, q_ref, k_ref, v_ref, o_ref, vt_ref, qt_ref, m_ref, acc_ref, *bufs, n_heads, tq, nq):
    xs_ref, xm_ref, ps_ref, pm_ref, a_ref = (bufs[2 * i:2 * i + 2] for i in range(5))
    bpq = tq // T_BLK
    vw = n_heads * MLA_V

    def values_t(vb):
        vt = vb.T
        rows = lax.broadcasted_iota(jnp.int32, (ATT_VE - MLA_V, vt.shape[1]), 0)
        ext = jnp.where(rows == 0, 1.0, 0.0).astype(BF16)
        parts = []
        for h in range(n_heads):
            parts += [vt[h * MLA_V:(h + 1) * MLA_V, :], ext]
        return jnp.concatenate(parts, axis=0)

    for j in range(nq):
        vt_ref[j] = values_t(_load_rows(v_ref, 1 + j * bpq, bpq, slice(0, vw)))
    v0t = values_t(v_ref[0])

    for i in range(nq):
        qb = _load_rows(q_ref, 1 + i * bpq, bpq, slice(0, n_heads * HEAD_PAD))
        qt_ref[i] = qb.T

    def causal(st):
        r = lax.broadcasted_iota(jnp.int32, st.shape, 0)
        c = lax.broadcasted_iota(jnp.int32, st.shape, 1)
        return jnp.where(r <= c, st, MASK_VALUE * LOG2_E)

    heads = range(n_heads)
    qs = [slice(h * HEAD_PAD, (h + 1) * HEAD_PAD) for h in heads]
    vs = [slice(h * ATT_VE, (h + 1) * ATT_VE) for h in heads]
    k0 = [k_ref[0, :, c] for c in qs]

    def normalised(acc):
        return acc[0:MLA_V, :] / acc[MLA_V:MLA_V + 1, :]

    outs = []
    for n in heads:
        st = causal(_dot_nt(k0[n], q_ref[0, :, qs[n]]))
        p = jnp.exp2(st - jnp.max(st, axis=0, keepdims=True))
        outs.append(normalised(_dot(v0t[vs[n], :], p.astype(BF16))))
    o_ref[0, :, :] = jnp.concatenate(outs, axis=0).T.astype(BF16)

    def key_tile(j, n):
        return _load_rows(k_ref, 1 + j * bpq, bpq, qs[n])

    def diag_scores(i, s):
        for n in heads:
            xs_ref[s][n] = _dot(key_tile(i, n), qt_ref[i, qs[n], :])
            xm_ref[s][n] = _dot(k0[n], qt_ref[i, qs[n], :])

    def diag_stats(i, s):
        for n in heads:
            st, sm = causal(xs_ref[s][n]), xm_ref[s][n]
            m = jnp.maximum(jnp.max(st, axis=0, keepdims=True), jnp.max(sm, axis=0, keepdims=True))
            p = jnp.exp2(st - m)
            pm = jnp.exp2(sm - m)
            m_ref[i, n] = m
            ps_ref[s][n] = p.astype(BF16)
            pm_ref[s][n] = pm.astype(BF16)

    def diag_values(i, s):
        for n in heads:
            acc_ref[i, vs[n], :] = _dot(vt_ref[i, vs[n], :], ps_ref[s][n]) + _dot(v0t[vs[n], :], pm_ref[s][n])

    _pipeline3(nq, diag_scores, diag_stats, diag_values, steps_per_body=ATT_STEPS_PER_BODY)

    def off_scores(f, s):
        i, j = tab_ref[0, f], tab_ref[1, f]
        for n in heads:
            xs_ref[s][n] = _dot(key_tile(j, n), qt_ref[i, qs[n], :])

    def off_stats(f, s):
        i = tab_ref[0, f]
        for n in heads:
            x = xs_ref[s][n]
            m_old = m_ref[i, n]
            m = jnp.maximum(m_old, jnp.max(x, axis=0, keepdims=True))
            a = jnp.exp2(m_old - m)
            p = jnp.exp2(x - m)
            m_ref[i, n] = m
            a_ref[s][n] = a
            ps_ref[s][n] = p.astype(BF16)

    def off_values(f, s):
        i, j = tab_ref[0, f], tab_ref[1, f]
        for n in heads:
            acc_ref[i, vs[n], :] = a_ref[s][n] * acc_ref[i, vs[n], :] + _dot(vt_ref[j, vs[n], :], ps_ref[s][n])

    _pipeline3(nq * (nq - 1) // 2, off_scores, off_stats, off_values, steps_per_body=ATT_STEPS_PER_BODY)

    for i in range(nq):
        o = jnp.concatenate([normalised(acc_ref[i, vs[n], :]) for n in heads], axis=0)
        _store_rows(o_ref, 1 + i * bpq, bpq, slice(0, vw), o.astype(BF16).T)


def _pipeline3(n, stage1, stage2, stage3, n_slots=2, steps_per_body=None):
    if n == 0:
        return
    stage1(0, 0)
    if n > 1:
        stage1(1, 1 % n_slots)
    stage2(0, 0)

    def step(t, k):
        stage1(t + 2, (k + 2) % n_slots)
        stage2(t + 1, (k + 1) % n_slots)
        stage3(t, k)

    steady = max(n - 2, 0)
    spb = steps_per_body or n_slots
    trips = steady // spb
    if trips:
        def body(u, carry):
            for k in range(spb):
                step(spb * u + k, k % n_slots)
            return carry

        lax.fori_loop(0, trips, body, 0)
    for t in range(trips * spb, steady):
        step(t, t % n_slots)
    if n > 1:
        stage3(n - 2, (n - 2) % n_slots)
        stage2(n - 1, (n - 1) % n_slots)
    stage3(n - 1, (n - 1) % n_slots)


def _off_diagonal_order(nq):
    left = [(i, j) for i in range(nq) for j in range(i)]
    order = []
    while left:
        count = {}
        for i, _ in left:
            count[i] = count.get(i, 0) + 1
        ok = [p for p in left if not order or p[0] != order[-1][0]] or left
        pick = max(ok, key=lambda p: (count[p[0]], -p[1]))
        order.append(pick)
        left.remove(pick)
    return order or [(0, 0)]


def _attention(zall, nb, nblk):
    z4 = zall.reshape(nblk, nb, T_BLK, ZALL_W)
    hps = ATT_HEADS_PER_STEP
    n_hp = MLA_HEADS // hps
    qw = hps * HEAD_PAD
    vw = hps * MLA_V
    s = (nblk - 1) * T_BLK
    tq = min(ATT_TQ, s)
    nq = s // tq
    col = lambda b, p: p
    tab = jnp.asarray(np.array(_off_diagonal_order(nq), np.int32).T)
    o = pl.pallas_call(
        functools.partial(_attn_kernel, n_heads=hps, tq=tq, nq=nq),
        out_shape=jax.ShapeDtypeStruct((nblk, nb, T_BLK, MLA_HEADS * MLA_V), BF16),
        grid=(nb, n_hp),
        in_specs=[pl.BlockSpec(memory_space=pltpu.SMEM),
                  _seq_spec(nblk, qw, lambda b, p: OFF_Q // qw + p),
                  _seq_spec(nblk, qw, lambda b, p: OFF_K // qw + p),
                  _seq_spec(nblk, vw, lambda b, p: OFF_V // vw + p)],
        out_specs=_seq_spec(nblk, vw, col),
        scratch_shapes=[pltpu.VMEM((nq, hps * ATT_VE, tq), BF16), pltpu.VMEM((nq, qw, tq), BF16),
                        pltpu.VMEM((nq, hps, 1, tq), F32), pltpu.VMEM((nq, hps * ATT_VE, tq), F32)]
        + 2 * [pltpu.VMEM((hps, tq, tq), F32)] + 2 * [pltpu.VMEM((hps, N_META, tq), F32)]
        + 2 * [pltpu.VMEM((hps, tq, tq), BF16)] + 2 * [pltpu.VMEM((hps, N_META, tq), BF16)]
        + 2 * [pltpu.VMEM((hps, 1, tq), F32)],
        compiler_params=_cparams(2),
        name="mla_attn",
    )(tab, z4, z4, z4)
    return o.reshape(nblk * nb * T_BLK, MLA_HEADS * MLA_V)


def _time_major_perm(nb):
    p = np.zeros((T_BLK * nb, T_BLK * nb), np.float32)
    for b in range(nb):
        for t in range(T_BLK):
            p[t * nb + b, b * T_BLK + t] = 1.0
    return p


def _block_row_bcast(x, m, row):
    t, c = x.shape
    if m == t:
        return jnp.broadcast_to(x[row:row + 1, :], x.shape)
    x3 = x.reshape(t // m, m, c)
    return jnp.broadcast_to(x3[:, row:row + 1, :], x3.shape).reshape(t, c)


def _hgrn_gates(blk0, n_blk, f_ref, lb_ref):
    lb = lb_ref[...]
    zf = _load_rows(f_ref, blk0, n_blk, slice(0, HG_QK)).astype(F32)
    e = jnp.exp(-jnp.abs(zf))
    rcp = 1.0 / (1.0 + e)
    pos = zf >= 0.0
    sig_p = jnp.where(pos, rcp, e * rcp)
    sig_n = jnp.where(pos, e * rcp, rcp)
    f = lb + (1.0 - lb) * sig_p
    log_f = jnp.log2(jnp.maximum(f, HG_F_MIN))
    k = (1.0 - lb) * sig_n
    hi = log_f.astype(BF16)
    lo = (log_f - hi.astype(F32)).astype(BF16)
    return k, hi, lo


def _hgrn_mix(blk0, n_blk, gates, q_ref, v_ref, tri_ref, lvl_ref, st_ref):
    t = n_blk * T_BLK
    cs = slice(0, HG_QK)
    heads = [slice(h * HG_KEY, (h + 1) * HG_KEY) for h in range(HG_HEADS)]
    k, hi, lo = gates
    q = _load_rows(q_ref, blk0, n_blk, cs).astype(F32)
    v = _load_rows(v_ref, blk0, n_blk, cs)
    tri = tri_ref[0:t, 0:t]
    cum = _dot(tri, hi) + _dot(tri, lo)
    lvl = lvl_ref[0:t, 0:t]
    r_idx = lax.broadcasted_iota(jnp.int32, (t, HG_QK), 0)
    c8 = _block_row_bcast(cum, HG_BOTTOM, HG_BOTTOM // 2 - 1)
    qe = (q * jnp.exp2(cum - c8)).astype(BF16)
    ke = (k * jnp.exp2(c8 - cum)).astype(BF16)
    n_lvl = len(HG_LEVELS)
    scores = [jnp.where(lvl == n_lvl, _dot_nt(qe[:, hs], ke[:, hs]), 0.0) for hs in heads]
    for li, m in enumerate(HG_LEVELS):
        if m > t:
            continue
        half = m // 2
        cmid = _block_row_bcast(cum, m, half - 1)
        upper = (r_idx & (m - 1)) >= half
        ex = jnp.exp2(jnp.where(upper, cum - cmid, cmid - cum))
        qe = jnp.where(upper, q * ex, 0.0).astype(BF16)
        ke = jnp.where(upper, 0.0, k * ex).astype(BF16)
        scores = [jnp.where(lvl == li, _dot_nt(qe[:, hs], ke[:, hs]), s) for s, hs in zip(scores, heads)]
    qd = (q * jnp.exp2(cum)).astype(BF16)
    last = cum[t - 1:t, :]
    kd = (k * jnp.exp2(last - cum)).astype(BF16)
    dec = jnp.exp2(last)
    outs = []
    for h, hs in enumerate(heads):
        st = st_ref[h]
        outs.append(_dot(scores[h].astype(BF16), v[:, hs]) + _dot_nt(qd[:, hs], st.astype(BF16)))
        st_ref[h] = st * dec[:, hs] + _dot_tn(v[:, hs], kd[:, hs])
    return tuple(outs)


def _hgrn_out(blk0, n_blk, outs, g_ref, on_ref, o_ref):
    cs = slice(0, HG_VW)
    g = _load_rows(g_ref, blk0, n_blk, cs).astype(F32)
    o = jnp.concatenate([o * lax.rsqrt(jnp.mean(o * o, axis=-1, keepdims=True) + 1e-6) for o in outs], axis=1)
    _store_rows(o_ref, blk0, n_blk, cs, (o * on_ref[...] * (g * _sigmoid(g))).astype(BF16))


def _hgrn_kernel(q_ref, f_ref, v_ref, g_ref, lb_ref, on_ref, tri_ref, lvl_ref, o_ref, st_ref, *bufs, n_chunks):
    kb_ref, hi_ref, lo_ref, ob_ref = (bufs[HG_SLOTS * i:HG_SLOTS * (i + 1)] for i in range(4))
    bpc = HG_CHUNK // T_BLK
    heads = [slice(h * HG_VAL, (h + 1) * HG_VAL) for h in range(HG_HEADS)]
    gates = functools.partial(_hgrn_gates, f_ref=f_ref, lb_ref=lb_ref)
    mix = functools.partial(_hgrn_mix, q_ref=q_ref, v_ref=v_ref, tri_ref=tri_ref, lvl_ref=lvl_ref, st_ref=st_ref)
    out = functools.partial(_hgrn_out, g_ref=g_ref, on_ref=on_ref, o_ref=o_ref)
    st_ref[...] = jnp.zeros_like(st_ref)
    out(0, 1, mix(0, 1, gates(0, 1)))

    def stage_gates(c, s):
        k, hi, lo = gates(1 + c * bpc, bpc)
        kb_ref[s][...], hi_ref[s][...], lo_ref[s][...] = k, hi.astype(F32), lo.astype(F32)

    def stage_mix(c, s):
        outs = mix(1 + c * bpc, bpc, (kb_ref[s][...], hi_ref[s][...].astype(BF16), lo_ref[s][...].astype(BF16)))
        for hs, o in zip(heads, outs):
            ob_ref[s][:, hs] = o

    def stage_out(c, s):
        out(1 + c * bpc, bpc, tuple(ob_ref[s][:, hs] for hs in heads))

    _pipeline3(n_chunks, stage_gates, stage_mix, stage_out, n_slots=HG_SLOTS)


def _hgrn(zall, lb, onorm, tri, lvl, nb, nblk, li):
    z4 = zall.reshape(nblk, nb, T_BLK, ZALL_W)
    col = lambda j: _seq_spec(nblk, HG_QK, lambda b: OFF_HG // HG_QK + j)
    o = pl.pallas_call(
        functools.partial(_hgrn_kernel, n_chunks=(nblk - 1) * T_BLK // HG_CHUNK),
        out_shape=jax.ShapeDtypeStruct((nblk, nb, T_BLK, HG_VW), BF16),
        grid=(nb,),
        in_specs=[col(0), col(1), col(2), col(3),
                  _const_spec((1, HG_QK), li), _const_spec((1, HG_VW), li),
                  _const_spec((HG_CHUNK, HG_CHUNK)), _const_spec((HG_CHUNK, HG_CHUNK))],
        out_specs=_seq_spec(nblk, HG_VW, lambda b: 0),
        scratch_shapes=[pltpu.VMEM((HG_HEADS, HG_VAL, HG_KEY), F32)]
        + 3 * HG_SLOTS * [pltpu.VMEM((HG_CHUNK, HG_QK), F32)]
        + HG_SLOTS * [pltpu.VMEM((HG_CHUNK, HG_VW), F32)],
        compiler_params=_cparams(1),
        name="hgrn2",
    )(z4, z4, z4, z4, lb, onorm, tri, lvl)
    return o.reshape(nblk * nb * T_BLK, HG_VW)


def _hgrn_level_matrix():
    r = np.arange(HG_CHUNK)[:, None]
    c = np.arange(HG_CHUNK)[None, :]
    lvl = np.zeros((HG_CHUNK, HG_CHUNK), np.int32)
    for li, m in enumerate(HG_LEVELS):
        lvl = np.where(r // m == c // m, li, lvl)
    lvl = np.where(r // HG_BOTTOM == c // HG_BOTTOM, len(HG_LEVELS), lvl)
    return np.where(c <= r, lvl, -1).astype(np.int32)


def _back_kernel(om_ref, os_ref, oh_ref, gm_ref, gs_ref, gh_ref, h_ref,
                 wm_ref, ws_ref, wh_ref, wo_ref, g1_ref, b1_ref,
                 wg_ref, wu_ref, wd_ref, g2_ref, b2_ref, o_ref, r1_ref, r2_ref, *, alpha):
    @pl.when(pl.program_id(0) == 0)
    def _():
        r1_ref[...] = jnp.zeros_like(r1_ref)
        r2_ref[...] = jnp.zeros_like(r2_ref)

    r1 = r1_ref[...]
    r2 = r2_ref[...]
    ym = _dot(om_ref[...], wm_ref[...])
    ys = _dot(os_ref[...], ws_ref[...])
    yh = _dot(oh_ref[...], wh_ref[...])
    o_ref[...] = _layer_norm(r2, g2_ref[...], b2_ref[...]).reshape(o_ref.shape)
    h1 = _layer_norm(r1, g1_ref[...], b1_ref[...])
    hb = h1.astype(BF16)
    a = _dot(hb, wg_ref[...])
    u = _dot(hb, wu_ref[...])
    mixed = _sigmoid(gm_ref[...].astype(F32)) * ym
    mixed += _sigmoid(gs_ref[...].astype(F32)) * ys
    mixed += _sigmoid(gh_ref[...].astype(F32)) * yh
    r1_ref[...] = alpha * h_ref[...] + _dot(mixed.astype(BF16), wo_ref[...])
    r2_ref[...] = alpha * h1 + _dot((a * _sigmoid(a) * u).astype(BF16), wd_ref[...])


def _back(om, oh, zall, h, wm, ws, wh, wo, g1, b1, wg, wu, wd, g2, b2, nb, alpha, final, li):
    r, d = h.shape
    bw = om.shape[1]
    dff = wg.shape[-1]
    rb = T_BLK * nb
    skip = 1 if final else 0
    n_blk = r // rb - skip
    lag = 2
    row = lambda width, j=0: pl.BlockSpec((rb, width), lambda i: (jnp.minimum(i, n_blk - 1) + skip, j))
    if final:
        out_shape = jax.ShapeDtypeStruct((nb, n_blk * T_BLK, d), F32)
        out_spec = pl.BlockSpec((nb, T_BLK, d), lambda i: (0, jnp.maximum(i - lag, 0), 0))
    else:
        out_shape = jax.ShapeDtypeStruct((r, d), F32)
        out_spec = pl.BlockSpec((rb, d), lambda i: (jnp.maximum(i - lag, 0), 0))
    return pl.pallas_call(
        functools.partial(_back_kernel, alpha=alpha),
        out_shape=out_shape,
        grid=(n_blk + lag,),
        in_specs=[row(bw), row(bw, OFF_Y // bw), row(bw),
                  row(d, OFF_G // d), row(d, OFF_G // d + 1), row(d, OFF_G // d + 2), row(d),
                  _const_spec((bw, d), li), _const_spec((bw, d), li), _const_spec((bw, d), li),
                  _const_spec((d, d), li), _const_spec((1, d), li), _const_spec((1, d), li),
                  _const_spec((d, dff), li), _const_spec((d, dff), li), _const_spec((dff, d), li),
                  _const_spec((1, d), li), _const_spec((1, d), li)],
        out_specs=out_spec,
        scratch_shapes=[pltpu.VMEM((rb, d), F32), pltpu.VMEM((rb, d), F32)],
        compiler_params=_cparams(1, "arbitrary"),
        name="merge_ffn",
    )(om, zall, oh, zall, zall, zall, h, wm, ws, wh, wo, g1, b1, wg, wu, wd, g2, b2)


def _permute_w_in(w):
    w = w.astype(BF16)
    lat = MLA_Q_RANK + MLA_KV_RANK
    s5_0 = lat + MLA_ROPE
    hg_0 = s5_0 + S5_WIDTH
    pad0 = jnp.zeros((w.shape[0], MLA_NOPE), BF16)
    pad1 = jnp.zeros((w.shape[0], ZKR_W - MLA_NOPE - MLA_ROPE), BF16)
    return jnp.concatenate([w[:, s5_0:hg_0], w[:, 0:lat], w[:, hg_0:], pad0, w[:, lat:s5_0], pad1], axis=1)


def _mla_weights(w_uq, w_ukv):
    rq, rkv = w_uq.shape[0], w_ukv.shape[0]
    zpad = HEAD_PAD - MLA_NOPE - MLA_ROPE
    wq = w_uq.reshape(rq, MLA_HEADS, MLA_NOPE + MLA_ROPE)
    q_nope, q_rope = wq[..., :MLA_NOPE], wq[..., MLA_NOPE:]
    zq = jnp.zeros((rq, MLA_HEADS, zpad), w_uq.dtype)
    wq_p = jnp.concatenate([q_nope, q_rope, zq], axis=-1).reshape(rq, -1)
    wkv = w_ukv.reshape(rkv, MLA_HEADS, MLA_NOPE + MLA_V)
    zk = jnp.zeros((rkv, MLA_HEADS, HEAD_PAD - MLA_NOPE), w_ukv.dtype)
    wk_p = jnp.concatenate([wkv[..., :MLA_NOPE], zk], axis=-1).reshape(rkv, -1)
    wv = wkv[..., MLA_NOPE:].reshape(rkv, -1)
    return [a.astype(BF16) for a in (wq_p, wk_p, wv)]


def _rope_consts():
    half = MLA_ROPE // 2
    inv = ROPE_THETA ** (-(jnp.arange(0, MLA_ROPE, 2, dtype=F32) / MLA_ROPE))
    place = np.zeros((half, HEAD_PAD), np.float32)
    place[np.arange(half), MLA_NOPE + np.arange(half)] = 1.0
    place[np.arange(half), MLA_NOPE + half + np.arange(half)] = 1.0
    base = np.zeros((1, HEAD_PAD), np.float32)
    base[0, :MLA_NOPE] = 1.0
    return inv[:, None], jnp.asarray(place, BF16), jnp.asarray(base)


def _s5_params(lam_re, lam_im, log_dt, b_re, b_im, c_re, c_im):
    lr = jnp.minimum(lam_re.astype(F32), -1e-4)
    li = lam_im.astype(F32)
    dt = jnp.exp(log_dt.astype(F32))[:, None]
    mag = jnp.exp(lr * dt)
    ab_r = mag * jnp.cos(li * dt)
    ab_i = mag * jnp.sin(li * dt)
    den = lr * lr + li * li
    nr = ab_r - 1.0
    coef_r = ((nr * lr + ab_i * li) / den)[..., None]
    coef_i = ((ab_i * lr - nr * li) / den)[..., None]
    bb_r = coef_r * b_re.astype(F32) - coef_i * b_im.astype(F32)
    bb_i = coef_r * b_im.astype(F32) + coef_i * b_re.astype(F32)
    n_slab = S5_WIDTH // S5_SLAB
    gps = S5_SLAB // S5_GROUP
    eye = jnp.eye(gps, dtype=F32)

    def in_mat(bb):
        b4 = bb.reshape(n_slab, gps, S5_STATE, S5_GROUP)
        return jnp.einsum('jgnc,gh->jgchn', b4, eye).reshape(n_slab, S5_SLAB, S5_SLAB_STATE)

    def out_mat(cc):
        c4 = cc.astype(F32).reshape(n_slab, gps, S5_GROUP, S5_STATE)
        return jnp.einsum('jgcn,gh->jgnhc', c4, eye).reshape(n_slab, S5_SLAB_STATE, S5_SLAB)

    bm = jnp.concatenate([in_mat(bb_r), in_mat(bb_i)], axis=2).astype(BF16)
    cm = jnp.concatenate([out_mat(c_re), -out_mat(c_im)], axis=1).astype(BF16)
    ar = ab_r.reshape(n_slab, 1, S5_SLAB_STATE)
    ai = ab_i.reshape(n_slab, 1, S5_SLAB_STATE)
    return bm, cm, ar, ai


def kernel(x, positions, meta_tokens, ln_in_g, ln_in_b, w_in, mla_q_norm, mla_w_uq, mla_kv_norm, mla_w_ukv,
           s5_lam_re, s5_lam_im, s5_log_dt, s5_b_re, s5_b_im, s5_c_re, s5_c_im, s5_d, s5_w_glu,
           hg_lb_logits, hg_out_norm, w_br_mla, w_br_s5, w_br_hg, w_out, ln1_g, ln1_b,
           w_ffn_gate, w_ffn_up, w_ffn_down, ln2_g, ln2_b):
    nb, s, d = x.shape
    depth = w_in.shape[0]
    nblk = (s + N_META) // T_BLK
    alpha = (2 * depth) ** 0.25
    row2 = lambda a: a.astype(F32)[None, :]
    rows3 = lambda a: a.astype(F32)[:, None, :]
    bf16 = lambda a: a.astype(BF16)

    meta_pos = jnp.broadcast_to(jnp.arange(N_META, dtype=jnp.int32)[None, :], (nb, N_META))
    pos = jnp.concatenate([meta_pos, positions.astype(jnp.int32) + N_META], axis=1)
    pos = pos.reshape(nb, nblk, T_BLK).transpose(1, 0, 2).reshape(nblk, 1, nb * T_BLK)
    rope = _rope_consts()
    p_lb = jax.nn.softmax(hg_lb_logits.astype(F32), axis=0)
    lower_bounds = jnp.cumsum(p_lb, axis=0) - p_lb[0]
    tri = jnp.asarray(np.tril(np.ones((HG_CHUNK, HG_CHUNK), np.float32)), BF16)
    lvl = jnp.asarray(_hgrn_level_matrix())
    perm = jnp.asarray(_time_major_perm(nb), BF16)

    front_p = (jax.vmap(_permute_w_in)(w_in), pos, rope, rows3(mla_q_norm), rows3(mla_kv_norm),
               *jax.vmap(_mla_weights)(mla_w_uq, mla_w_ukv), perm,
               *jax.vmap(_s5_params)(s5_lam_re, s5_lam_im, s5_log_dt, s5_b_re, s5_b_im, s5_c_re, s5_c_im),
               rows3(s5_d), bf16(s5_w_glu))
    hgrn_p = (lower_bounds[:, None, :], rows3(hg_out_norm), tri, lvl)
    back_p = (bf16(w_br_mla), bf16(w_br_s5), bf16(w_br_hg), bf16(w_out), rows3(ln1_g), rows3(ln1_b),
              bf16(w_ffn_gate), bf16(w_ffn_up), bf16(w_ffn_down), rows3(ln2_g), rows3(ln2_b))

    h = (x, meta_tokens.astype(x.dtype), row2(ln_in_g), row2(ln_in_b))
    for li in range(depth):
        outs = _front(h, *front_p, nb, li)
        if li == 0:
            h, *outs = outs
        zall, = outs
        o_mla = _attention(zall, nb, nblk)
        o_hg = _hgrn(zall, *hgrn_p, nb, nblk, li)
        h = _back(o_mla, o_hg, zall, h, *back_p, nb, alpha, final=li == depth - 1, li=li)
    return h
```

```python
import functools
import math

import jax
import jax.numpy as jnp
import numpy as np
from jax import lax
from jax.experimental import pallas as pl
from jax.experimental.pallas import tpu as pltpu

F32 = jnp.float32
BF16 = jnp.bfloat16

N_META = 16
MLA_HEADS = 8
MLA_NOPE = 64
MLA_ROPE = 32
MLA_V = 64
MLA_Q_RANK = 256
MLA_KV_RANK = 256
ROPE_THETA = 10000.0
MASK_VALUE = -1e9
LOG2_E = math.log2(math.e)
HEAD_PAD = 128
S5_WIDTH = 512
S5_GROUP = 16
S5_STATE = 64
S5_SLAB = 128
S5_SLAB_STATE = (S5_SLAB // S5_GROUP) * S5_STATE
HG_HEADS = 4
HG_KEY = 128
HG_VAL = 128
HG_QK = HG_HEADS * HG_KEY
HG_VW = HG_HEADS * HG_VAL
HG_F_MIN = 1e-6
HG_CHUNK = 128
HG_LEVELS = (128, 64, 32, 16)
HG_BOTTOM = 8
HG_SLOTS = 3
T_BLK = 16
ATT_TQ = 256
ATT_HEADS_PER_STEP = 4
ATT_STEPS_PER_BODY = 14
ATT_VE = MLA_V + 16
VMEM_LIMIT = 56 * 1024 * 1024

ZA_W = 1024
ZHG_W = 2048
ZG_W = 3072
ZKR_W = 128
Z_W = ZA_W + ZHG_W + ZG_W + ZKR_W
ZALL_WIDTHS = (ZHG_W, ZG_W, MLA_HEADS * HEAD_PAD, MLA_HEADS * HEAD_PAD, MLA_HEADS * MLA_V, S5_WIDTH)
ZALL_OFF = tuple(sum(ZALL_WIDTHS[:i]) for i in range(len(ZALL_WIDTHS)))
ZALL_W = sum(ZALL_WIDTHS)
OFF_HG, OFF_G, OFF_Q, OFF_K, OFF_V, OFF_Y = ZALL_OFF
WIDE_CHUNK = 512
FILL_BEFORE_MLA = 3


def _cparams(n_grid, sem="parallel"):
    return pltpu.CompilerParams(dimension_semantics=(sem,) * n_grid, vmem_limit_bytes=VMEM_LIMIT)


def _const_spec(shape, layer=None):
    nd = len(shape)
    if layer is None:
        return pl.BlockSpec(shape, lambda *_: (0,) * nd, pipeline_mode=pl.Buffered(1))
    return pl.BlockSpec((None,) + tuple(shape), lambda *_: (layer,) + (0,) * nd, pipeline_mode=pl.Buffered(1))


def _seq_spec(nblk, width, col):
    return pl.BlockSpec((nblk, None, T_BLK, width), lambda b, *g: (0, b, 0, col(b, *g)))


def _dot(a, b):
    return jnp.dot(a, b, preferred_element_type=F32)


def _dot_nt(a, b):
    return lax.dot_general(a, b, (((1,), (1,)), ((), ())), preferred_element_type=F32)


def _dot_tn(a, b):
    return lax.dot_general(a, b, (((0,), (0,)), ((), ())), preferred_element_type=F32)


def _sigmoid(x):
    return 1.0 / (1.0 + jnp.exp(-x))


def _layer_norm(x, g, b, eps=1e-5):
    mu = jnp.mean(x, axis=-1, keepdims=True)
    xc = x - mu
    var = jnp.mean(xc * xc, axis=-1, keepdims=True)
    return xc * lax.rsqrt(var + eps) * g + b


def _rms_norm(x, g, eps=1e-6):
    return x * lax.rsqrt(jnp.mean(x * x, axis=-1, keepdims=True) + eps) * g


def _load_rows(ref, blk0, n_blk, cs):
    x = ref[pl.ds(blk0, n_blk), :, cs]
    return x.reshape(n_blk * T_BLK, x.shape[-1])


def _store_rows(ref, blk0, n_blk, cs, x):
    ref[pl.ds(blk0, n_blk), :, cs] = x.reshape(n_blk, T_BLK, x.shape[-1])


def _rope_tables(pos_ref, inv_ref, place_ref, base_ref):
    ang = inv_ref[...] * pos_ref[...].astype(F32)

    def place(t):
        hi = t.astype(BF16)
        lo = (t - hi.astype(F32)).astype(BF16)
        return _dot_tn(hi, place_ref[...]) + _dot_tn(lo, place_ref[...])

    return place(jnp.cos(ang)) + base_ref[...], place(jnp.sin(ang))


def _mla_prep(cq, ckv, kr, cos, sin, qn_ref, kvn_ref, wq_ref, wk_ref, wv_ref, q_ref, k_ref, v_ref):
    scale = (MLA_NOPE + MLA_ROPE) ** -0.5 * LOG2_E
    half = MLA_ROPE // 2
    lane = lax.broadcasted_iota(jnp.int32, (1, HEAD_PAD), 1)
    sin_up = jnp.where((lane >= MLA_NOPE + half) & (lane < MLA_NOPE + MLA_ROPE), sin, 0.0)
    sin_dn = jnp.where((lane >= MLA_NOPE) & (lane < MLA_NOPE + half), -sin, 0.0)

    def rope(x, c, s_up, s_dn):
        return x * c + pltpu.roll(x, half, 1) * s_up + pltpu.roll(x, HEAD_PAD - half, 1) * s_dn

    cqn = _rms_norm(cq, qn_ref[...]).astype(BF16)
    ckvn = _rms_norm(ckv, kvn_ref[...]).astype(BF16)
    k_rope = rope(kr, cos, sin_up, sin_dn)
    cos_q, sin_up_q, sin_dn_q = cos * scale, sin_up * scale, sin_dn * scale
    for h in range(MLA_HEADS):
        cs = slice(h * HEAD_PAD, (h + 1) * HEAD_PAD)
        q_ref[:, cs] = rope(_dot(cqn, wq_ref[:, cs]), cos_q, sin_up_q, sin_dn_q).astype(BF16)
        k_ref[:, cs] = (_dot(ckvn, wk_ref[:, cs]) + k_rope).astype(BF16)
    v_ref[...] = _dot(ckvn, wv_ref[...]).astype(BF16)


def _s5_block(u, perm_ref, bm_ref, cm_ref, ar_ref, ai_ref, d_ref, wg_ref, o_ref,
              xr_ref, xi_ref, buf_ref, y_ref, nb, fillers):
    n_slab = S5_WIDTH // S5_SLAB
    ns = S5_SLAB_STATE
    u = _dot(perm_ref[...], u).astype(BF16)
    for j in range(n_slab):
        buf_ref[...] = _dot(u[:, j * S5_SLAB:(j + 1) * S5_SLAB], bm_ref[j])
        if fillers:
            fillers.pop(0)()
        ar = jnp.broadcast_to(ar_ref[j], (nb, ns))
        ai = jnp.broadcast_to(ai_ref[j], (nb, ns))
        xr = xr_ref[j]
        xi = xi_ref[j]
        for t in range(T_BLK):
            rows = slice(t * nb, (t + 1) * nb)
            nr = ar * xr - ai * xi + buf_ref[rows, 0:ns]
            ni = ar * xi + ai * xr + buf_ref[rows, ns:2 * ns]
            buf_ref[rows, 0:ns] = nr
            buf_ref[rows, ns:2 * ns] = ni
            xr, xi = nr, ni
        xr_ref[j] = xr
        xi_ref[j] = xi
        y_ref[:, j * S5_SLAB:(j + 1) * S5_SLAB] = _dot(buf_ref[...].astype(BF16), cm_ref[j])

    while fillers:
        fillers.pop(0)()
    y = y_ref[...] + d_ref[...] * u.astype(F32)
    y = 0.5 * y * (1.0 + jnp.tanh(math.sqrt(2.0 / math.pi) * (y + 0.044715 * (y * y * y))))
    gate = _sigmoid(_dot(y.astype(BF16), wg_ref[...]))
    out = (y * gate).astype(BF16)
    o_ref[...] = _dot_tn(perm_ref[...], out).astype(BF16)


def _front_kernel(*refs, nb, ln_in):
    if ln_in:
        x_ref, meta_ref, lng_ref, lnb_ref, *refs = refs
    else:
        h_ref, *refs = refs
    (w_ref, pos_ref, inv_ref, place_ref, base_ref, qn_ref, kvn_ref, wq_ref, wk_ref, wv_ref,
     perm_ref, bm_ref, cm_ref, ar_ref, ai_ref, d_ref, wglu_ref, *refs) = refs
    if ln_in:
        hout_ref, *refs = refs
    zall_ref, xr_ref, xi_ref, buf_ref, y_ref = refs
    zhg_ref, zg_ref, q_ref, k_ref, v_ref, ys5_ref = (
        zall_ref.at[:, o:o + w] for o, w in zip(ZALL_OFF, ZALL_WIDTHS))

    @pl.when(pl.program_id(0) == 0)
    def _():
        xr_ref[...] = jnp.zeros_like(xr_ref)
        xi_ref[...] = jnp.zeros_like(xi_ref)

    if ln_in:
        hx = _layer_norm(x_ref[...], lng_ref[...], lnb_ref[...])
        hm = _layer_norm(meta_ref[...], lng_ref[...], lnb_ref[...])
        h = jnp.where(pl.program_id(0) == 0, jnp.broadcast_to(hm[None], hx.shape), hx)
        h = h.reshape(hout_ref.shape)
        hout_ref[...] = h
    else:
        h = h_ref[...]
    x = h.astype(BF16)
    za = _dot(x, w_ref[:, 0:ZA_W])
    zkr = _dot(x, w_ref[:, ZA_W + ZHG_W + ZG_W:Z_W])

    def wide_chunk(o_ref, c, off):
        def run():
            o_ref[:, c:c + WIDE_CHUNK] = _dot(x, w_ref[:, off + c:off + c + WIDE_CHUNK]).astype(BF16)
        return run

    fillers = [wide_chunk(zhg_ref, c, ZA_W) for c in range(0, ZHG_W, WIDE_CHUNK)]
    fillers += [wide_chunk(zg_ref, c, ZA_W + ZHG_W) for c in range(0, ZG_W, WIDE_CHUNK)]
    for _ in range(FILL_BEFORE_MLA):
        fillers.pop(0)()
    cos, sin = _rope_tables(pos_ref, inv_ref, place_ref, base_ref)
    c0 = S5_WIDTH
    _mla_prep(za[:, c0:c0 + MLA_Q_RANK], za[:, c0 + MLA_Q_RANK:c0 + MLA_Q_RANK + MLA_KV_RANK], zkr,
              cos, sin, qn_ref, kvn_ref, wq_ref, wk_ref, wv_ref, q_ref, k_ref, v_ref)
    _s5_block(za[:, 0:S5_WIDTH].astype(BF16), perm_ref, bm_ref, cm_ref, ar_ref, ai_ref, d_ref, wglu_ref, ys5_ref,
              xr_ref, xi_ref, buf_ref, y_ref, nb, fillers)


def _front(src, w, pos, rope, qn, kvn, wq, wk, wv, perm, bm, cm, ar, ai, dskip, wglu, nb, li):
    ln_in = isinstance(src, tuple)
    rb = T_BLK * nb
    if ln_in:
        x, meta, lng, lnb = src
        d = x.shape[-1]
        r = (x.shape[1] + N_META) * nb
        src_specs = [pl.BlockSpec((nb, T_BLK, d), lambda i: (0, jnp.maximum(i - 1, 0), 0)),
                     _const_spec((N_META, d)), _const_spec((1, d)), _const_spec((1, d))]
    else:
        src = (src,)
        r, d = src[0].shape
        src_specs = [pl.BlockSpec((rb, d), lambda i: (i, 0))]
    hw = MLA_HEADS * HEAD_PAD
    vw = MLA_HEADS * MLA_V
    n_slab = S5_WIDTH // S5_SLAB
    row = lambda width: pl.BlockSpec((rb, width), lambda i: (i, 0))
    out_shape = [jax.ShapeDtypeStruct((r, ZALL_W), BF16)]
    out_specs = [row(ZALL_W)]
    if ln_in:
        out_shape.insert(0, jax.ShapeDtypeStruct((r, d), F32))
        out_specs.insert(0, row(d))
    return pl.pallas_call(
        functools.partial(_front_kernel, nb=nb, ln_in=ln_in),
        out_shape=out_shape,
        grid=(r // rb,),
        in_specs=src_specs + [
            _const_spec((d, Z_W), li), pl.BlockSpec((None, 1, rb), lambda i: (i, 0, 0)),
            _const_spec((MLA_ROPE // 2, 1)), _const_spec((MLA_ROPE // 2, HEAD_PAD)), _const_spec((1, HEAD_PAD)),
            _const_spec((1, MLA_Q_RANK), li), _const_spec((1, MLA_KV_RANK), li),
            _const_spec((MLA_Q_RANK, hw), li), _const_spec((MLA_KV_RANK, hw), li), _const_spec((MLA_KV_RANK, vw), li),
            _const_spec((rb, rb)),
            _const_spec((n_slab, S5_SLAB, 2 * S5_SLAB_STATE), li),
            _const_spec((n_slab, 2 * S5_SLAB_STATE, S5_SLAB), li),
            _const_spec((n_slab, 1, S5_SLAB_STATE), li),
            _const_spec((n_slab, 1, S5_SLAB_STATE), li),
            _const_spec((1, S5_WIDTH), li),
            _const_spec((S5_WIDTH, S5_WIDTH), li),
        ],
        out_specs=out_specs,
        scratch_shapes=[
            pltpu.VMEM((n_slab, nb, S5_SLAB_STATE), F32),
            pltpu.VMEM((n_slab, nb, S5_SLAB_STATE), F32),
            pltpu.VMEM((rb, 2 * S5_SLAB_STATE), F32),
            pltpu.VMEM((rb, S5_WIDTH), F32),
        ],
        compiler_params=_cparams(1, "arbitrary"),
        name="front",
    )(*src, w, pos, *rope, qn, kvn, wq, wk, wv, perm, bm, cm, ar, ai, dskip, wglu)


def _attn_kernel(tab_ref, q_ref, k_ref, v_ref, o_ref, vt_ref, qt_ref, m_ref, acc_ref, *bufs, n_heads, tq, nq):
    xs_ref, xm_ref, ps_ref, pm_ref, a_ref = (bufs[2 * i:2 * i + 2] for i in range(5))
    bpq = tq // T_BLK
    vw = n_heads * MLA_V

    def values_t(vb):
        vt = vb.T
        rows = lax.broadcasted_iota(jnp.int32, (ATT_VE - MLA_V, vt.shape[1]), 0)
        ext = jnp.where(rows == 0, 1.0, 0.0).astype(BF16)
        parts = []
        for h in range(n_heads):
            parts += [vt[h * MLA_V:(h + 1) * MLA_V, :], ext]
        return jnp.concatenate(parts, axis=0)

    for j in range(nq):
        vt_ref[j] = values_t(_load_rows(v_ref, 1 + j * bpq, bpq, slice(0, vw)))
    v0t = values_t(v_ref[0])

    for i in range(nq):
        qb = _load_rows(q_ref, 1 + i * bpq, bpq, slice(0, n_heads * HEAD_PAD))
        qt_ref[i] = qb.T

    def causal(st):
        r = lax.broadcasted_iota(jnp.int32, st.shape, 0)
        c = lax.broadcasted_iota(jnp.int32, st.shape, 1)
        return jnp.where(r <= c, st, MASK_VALUE * LOG2_E)

    heads = range(n_heads)
    qs = [slice(h * HEAD_PAD, (h + 1) * HEAD_PAD) for h in heads]
    vs = [slice(h * ATT_VE, (h + 1) * ATT_VE) for h in heads]
    k0 = [k_ref[0, :, c] for c in qs]

    def normalised(acc):
        return acc[0:MLA_V, :] / acc[MLA_V:MLA_V + 1, :]

    outs = []
    for n in heads:
        st = causal(_dot_nt(k0[n], q_ref[0, :, qs[n]]))
        p = jnp.exp2(st - jnp.max(st, axis=0, keepdims=True))
        outs.append(normalised(_dot(v0t[vs[n], :], p.astype(BF16))))
    o_ref[0, :, :] = jnp.concatenate(outs, axis=0).T.astype(BF16)

    def key_tile(j, n):
        return _load_rows(k_ref, 1 + j * bpq, bpq, qs[n])

    def diag_scores(i, s):
        for n in heads:
            xs_ref[s][n] = _dot(key_tile(i, n), qt_ref[i, qs[n], :])
            xm_ref[s][n] = _dot(k0[n], qt_ref[i, qs[n], :])

    def diag_stats(i, s):
        for n in heads:
            st, sm = causal(xs_ref[s][n]), xm_ref[s][n]
            m = jnp.maximum(jnp.max(st, axis=0, keepdims=True), jnp.max(sm, axis=0, keepdims=True))
            p = jnp.exp2(st - m)
            pm = jnp.exp2(sm - m)
            m_ref[i, n] = m
            ps_ref[s][n] = p.astype(BF16)
            pm_ref[s][n] = pm.astype(BF16)

    def diag_values(i, s):
        for n in heads:
            acc_ref[i, vs[n], :] = _dot(vt_ref[i, vs[n], :], ps_ref[s][n]) + _dot(v0t[vs[n], :], pm_ref[s][n])

    _pipeline3(nq, diag_scores, diag_stats, diag_values, steps_per_body=ATT_STEPS_PER_BODY)

    def off_scores(f, s):
        i, j = tab_ref[0, f], tab_ref[1, f]
        for n in heads:
            xs_ref[s][n] = _dot(key_tile(j, n), qt_ref[i, qs[n], :])

    def off_stats(f, s):
        i = tab_ref[0, f]
        for n in heads:
            x = xs_ref[s][n]
            m_old = m_ref[i, n]
            m = jnp.maximum(m_old, jnp.max(x, axis=0, keepdims=True))
            a = jnp.exp2(m_old - m)
            p = jnp.exp2(x - m)
            m_ref[i, n] = m
            a_ref[s][n] = a
            ps_ref[s][n] = p.astype(BF16)

    def off_values(f, s):
        i, j = tab_ref[0, f], tab_ref[1, f]
        for n in heads:
            acc_ref[i, vs[n], :] = a_ref[s][n] * acc_ref[i, vs[n], :] + _dot(vt_ref[j, vs[n], :], ps_ref[s][n])

    _pipeline3(nq * (nq - 1) // 2, off_scores, off_stats, off_values, steps_per_body=ATT_STEPS_PER_BODY)

    for i in range(nq):
        o = jnp.concatenate([normalised(acc_ref[i, vs[n], :]) for n in heads], axis=0)
        _store_rows(o_ref, 1 + i * bpq, bpq, slice(0, vw), o.astype(BF16).T)


def _pipeline3(n, stage1, stage2, stage3, n_slots=2, steps_per_body=None):
    if n == 0:
        return
    stage1(0, 0)
    if n > 1:
        stage1(1, 1 % n_slots)
    stage2(0, 0)

    def step(t, k):
        stage1(t + 2, (k + 2) % n_slots)
        stage2(t + 1, (k + 1) % n_slots)
        stage3(t, k)

    steady = max(n - 2, 0)
    spb = steps_per_body or n_slots
    trips = steady // spb
    if trips:
        def body(u, carry):
            for k in range(spb):
                step(spb * u + k, k % n_slots)
            return carry

        lax.fori_loop(0, trips, body, 0)
    for t in range(trips * spb, steady):
        step(t, t % n_slots)
    if n > 1:
        stage3(n - 2, (n - 2) % n_slots)
        stage2(n - 1, (n - 1) % n_slots)
    stage3(n - 1, (n - 1) % n_slots)


def _off_diagonal_order(nq):
    left = [(i, j) for i in range(nq) for j in range(i)]
    order = []
    while left:
        count = {}
        for i, _ in left:
            count[i] = count.get(i, 0) + 1
        ok = [p for p in left if not order or p[0] != order[-1][0]] or left
        pick = max(ok, key=lambda p: (count[p[0]], -p[1]))
        order.append(pick)
        left.remove(pick)
    return order or [(0, 0)]


def _attention(zall, nb, nblk):
    z4 = zall.reshape(nblk, nb, T_BLK, ZALL_W)
    hps = ATT_HEADS_PER_STEP
    n_hp = MLA_HEADS // hps
    qw = hps * HEAD_PAD
    vw = hps * MLA_V
    s = (nblk - 1) * T_BLK
    tq = min(ATT_TQ, s)
    nq = s // tq
    col = lambda b, p: p
    tab = jnp.asarray(np.array(_off_diagonal_order(nq), np.int32).T)
    o = pl.pallas_call(
        functools.partial(_attn_kernel, n_heads=hps, tq=tq, nq=nq),
        out_shape=jax.ShapeDtypeStruct((nblk, nb, T_BLK, MLA_HEADS * MLA_V), BF16),
        grid=(nb, n_hp),
        in_specs=[pl.BlockSpec(memory_space=pltpu.SMEM),
                  _seq_spec(nblk, qw, lambda b, p: OFF_Q // qw + p),
                  _seq_spec(nblk, qw, lambda b, p: OFF_K // qw + p),
                  _seq_spec(nblk, vw, lambda b, p: OFF_V // vw + p)],
        out_specs=_seq_spec(nblk, vw, col),
        scratch_shapes=[pltpu.VMEM((nq, hps * ATT_VE, tq), BF16), pltpu.VMEM((nq, qw, tq), BF16),
                        pltpu.VMEM((nq, hps, 1, tq), F32), pltpu.VMEM((nq, hps * ATT_VE, tq), F32)]
        + 2 * [pltpu.VMEM((hps, tq, tq), F32)] + 2 * [pltpu.VMEM((hps, N_META, tq), F32)]
        + 2 * [pltpu.VMEM((hps, tq, tq), BF16)] + 2 * [pltpu.VMEM((hps, N_META, tq), BF16)]
        + 2 * [pltpu.VMEM((hps, 1, tq), F32)],
        compiler_params=_cparams(2),
        name="mla_attn",
    )(tab, z4, z4, z4)
    return o.reshape(nblk * nb * T_BLK, MLA_HEADS * MLA_V)


def _time_major_perm(nb):
    p = np.zeros((T_BLK * nb, T_BLK * nb), np.float32)
    for b in range(nb):
        for t in range(T_BLK):
            p[t * nb + b, b * T_BLK + t] = 1.0
    return p


def _block_row_bcast(x, m, row):
    t, c = x.shape
    if m == t:
        return jnp.broadcast_to(x[row:row + 1, :], x.shape)
    x3 = x.reshape(t // m, m, c)
    return jnp.broadcast_to(x3[:, row:row + 1, :], x3.shape).reshape(t, c)


def _hgrn_gates(blk0, n_blk, f_ref, lb_ref):
    lb = lb_ref[...]
    zf = _load_rows(f_ref, blk0, n_blk, slice(0, HG_QK)).astype(F32)
    e = jnp.exp(-jnp.abs(zf))
    rcp = 1.0 / (1.0 + e)
    pos = zf >= 0.0
    sig_p = jnp.where(pos, rcp, e * rcp)
    sig_n = jnp.where(pos, e * rcp, rcp)
    f = lb + (1.0 - lb) * sig_p
    log_f = jnp.log2(jnp.maximum(f, HG_F_MIN))
    k = (1.0 - lb) * sig_n
    hi = log_f.astype(BF16)
    lo = (log_f - hi.astype(F32)).astype(BF16)
    return k, hi, lo


def _hgrn_mix(blk0, n_blk, gates, q_ref, v_ref, tri_ref, lvl_ref, st_ref):
    t = n_blk * T_BLK
    cs = slice(0, HG_QK)
    heads = [slice(h * HG_KEY, (h + 1) * HG_KEY) for h in range(HG_HEADS)]
    k, hi, lo = gates
    q = _load_rows(q_ref, blk0, n_blk, cs).astype(F32)
    v = _load_rows(v_ref, blk0, n_blk, cs)
    tri = tri_ref[0:t, 0:t]
    cum = _dot(tri, hi) + _dot(tri, lo)
    lvl = lvl_ref[0:t, 0:t]
    r_idx = lax.broadcasted_iota(jnp.int32, (t, HG_QK), 0)
    c8 = _block_row_bcast(cum, HG_BOTTOM, HG_BOTTOM // 2 - 1)
    qe = (q * jnp.exp2(cum - c8)).astype(BF16)
    ke = (k * jnp.exp2(c8 - cum)).astype(BF16)
    n_lvl = len(HG_LEVELS)
    scores = [jnp.where(lvl == n_lvl, _dot_nt(qe[:, hs], ke[:, hs]), 0.0) for hs in heads]
    for li, m in enumerate(HG_LEVELS):
        if m > t:
            continue
        half = m // 2
        cmid = _block_row_bcast(cum, m, half - 1)
        upper = (r_idx & (m - 1)) >= half
        ex = jnp.exp2(jnp.where(upper, cum - cmid, cmid - cum))
        qe = jnp.where(upper, q * ex, 0.0).astype(BF16)
        ke = jnp.where(upper, 0.0, k * ex).astype(BF16)
        scores = [jnp.where(lvl == li, _dot_nt(qe[:, hs], ke[:, hs]), s) for s, hs in zip(scores, heads)]
    qd = (q * jnp.exp2(cum)).astype(BF16)
    last = cum[t - 1:t, :]
    kd = (k * jnp.exp2(last - cum)).astype(BF16)
    dec = jnp.exp2(last)
    outs = []
    for h, hs in enumerate(heads):
        st = st_ref[h]
        outs.append(_dot(scores[h].astype(BF16), v[:, hs]) + _dot_nt(qd[:, hs], st.astype(BF16)))
        st_ref[h] = st * dec[:, hs] + _dot_tn(v[:, hs], kd[:, hs])
    return tuple(outs)


def _hgrn_out(blk0, n_blk, outs, g_ref, on_ref, o_ref):
    cs = slice(0, HG_VW)
    g = _load_rows(g_ref, blk0, n_blk, cs).astype(F32)
    o = jnp.concatenate([o * lax.rsqrt(jnp.mean(o * o, axis=-1, keepdims=True) + 1e-6) for o in outs], axis=1)
    _store_rows(o_ref, blk0, n_blk, cs, (o * on_ref[...] * (g * _sigmoid(g))).astype(BF16))


def _hgrn_kernel(q_ref, f_ref, v_ref, g_ref, lb_ref, on_ref, tri_ref, lvl_ref, o_ref, st_ref, *bufs, n_chunks):
    kb_ref, hi_ref, lo_ref, ob_ref = (bufs[HG_SLOTS * i:HG_SLOTS * (i + 1)] for i in range(4))
    bpc = HG_CHUNK // T_BLK
    heads = [slice(h * HG_VAL, (h + 1) * HG_VAL) for h in range(HG_HEADS)]
    gates = functools.partial(_hgrn_gates, f_ref=f_ref, lb_ref=lb_ref)
    mix = functools.partial(_hgrn_mix, q_ref=q_ref, v_ref=v_ref, tri_ref=tri_ref, lvl_ref=lvl_ref, st_ref=st_ref)
    out = functools.partial(_hgrn_out, g_ref=g_ref, on_ref=on_ref, o_ref=o_ref)
    st_ref[...] = jnp.zeros_like(st_ref)
    out(0, 1, mix(0, 1, gates(0, 1)))

    def stage_gates(c, s):
        k, hi, lo = gates(1 + c * bpc, bpc)
        kb_ref[s][...], hi_ref[s][...], lo_ref[s][...] = k, hi.astype(F32), lo.astype(F32)

    def stage_mix(c, s):
        outs = mix(1 + c * bpc, bpc, (kb_ref[s][...], hi_ref[s][...].astype(BF16), lo_ref[s][...].astype(BF16)))
        for hs, o in zip(heads, outs):
            ob_ref[s][:, hs] = o

    def stage_out(c, s):
        out(1 + c * bpc, bpc, tuple(ob_ref[s][:, hs] for hs in heads))

    _pipeline3(n_chunks, stage_gates, stage_mix, stage_out, n_slots=HG_SLOTS)


def _hgrn(zall, lb, onorm, tri, lvl, nb, nblk, li):
    z4 = zall.reshape(nblk, nb, T_BLK, ZALL_W)
    col = lambda j: _seq_spec(nblk, HG_QK, lambda b: OFF_HG // HG_QK + j)
    o = pl.pallas_call(
        functools.partial(_hgrn_kernel, n_chunks=(nblk - 1) * T_BLK // HG_CHUNK),
        out_shape=jax.ShapeDtypeStruct((nblk, nb, T_BLK, HG_VW), BF16),
        grid=(nb,),
        in_specs=[col(0), col(1), col(2), col(3),
                  _const_spec((1, HG_QK), li), _const_spec((1, HG_VW), li),
                  _const_spec((HG_CHUNK, HG_CHUNK)), _const_spec((HG_CHUNK, HG_CHUNK))],
        out_specs=_seq_spec(nblk, HG_VW, lambda b: 0),
        scratch_shapes=[pltpu.VMEM((HG_HEADS, HG_VAL, HG_KEY), F32)]
        + 3 * HG_SLOTS * [pltpu.VMEM((HG_CHUNK, HG_QK), F32)]
        + HG_SLOTS * [pltpu.VMEM((HG_CHUNK, HG_VW), F32)],
        compiler_params=_cparams(1),
        name="hgrn2",
    )(z4, z4, z4, z4, lb, onorm, tri, lvl)
    return o.reshape(nblk * nb * T_BLK, HG_VW)


def _hgrn_level_matrix():
    r = np.arange(HG_CHUNK)[:, None]
    c = np.arange(HG_CHUNK)[None, :]
    lvl = np.zeros((HG_CHUNK, HG_CHUNK), np.int32)
    for li, m in enumerate(HG_LEVELS):
        lvl = np.where(r // m == c // m, li, lvl)
    lvl = np.where(r // HG_BOTTOM == c // HG_BOTTOM, len(HG_LEVELS), lvl)
    return np.where(c <= r, lvl, -1).astype(np.int32)


def _back_kernel(om_ref, os_ref, oh_ref, gm_ref, gs_ref, gh_ref, h_ref,
                 wm_ref, ws_ref, wh_ref, wo_ref, g1_ref, b1_ref,
                 wg_ref, wu_ref, wd_ref, g2_ref, b2_ref, o_ref, r1_ref, r2_ref, *, alpha):
    @pl.when(pl.program_id(0) == 0)
    def _():
        r1_ref[...] = jnp.zeros_like(r1_ref)
        r2_ref[...] = jnp.zeros_like(r2_ref)

    r1 = r1_ref[...]
    r2 = r2_ref[...]
    ym = _dot(om_ref[...], wm_ref[...])
    ys = _dot(os_ref[...], ws_ref[...])
    yh = _dot(oh_ref[...], wh_ref[...])
    o_ref[...] = _layer_norm(r2, g2_ref[...], b2_ref[...]).reshape(o_ref.shape)
    h1 = _layer_norm(r1, g1_ref[...], b1_ref[...])
    hb = h1.astype(BF16)
    a = _dot(hb, wg_ref[...])
    u = _dot(hb, wu_ref[...])
    mixed = _sigmoid(gm_ref[...].astype(F32)) * ym
    mixed += _sigmoid(gs_ref[...].astype(F32)) * ys
    mixed += _sigmoid(gh_ref[...].astype(F32)) * yh
    r1_ref[...] = alpha * h_ref[...] + _dot(mixed.astype(BF16), wo_ref[...])
    r2_ref[...] = alpha * h1 + _dot((a * _sigmoid(a) * u).astype(BF16), wd_ref[...])


def _back(om, oh, zall, h, wm, ws, wh, wo, g1, b1, wg, wu, wd, g2, b2, nb, alpha, final, li):
    r, d = h.shape
    bw = om.shape[1]
    dff = wg.shape[-1]
    rb = T_BLK * nb
    skip = 1 if final else 0
    n_blk = r // rb - skip
    lag = 2
    row = lambda width, j=0: pl.BlockSpec((rb, width), lambda i: (jnp.minimum(i, n_blk - 1) + skip, j))
    if final:
        out_shape = jax.ShapeDtypeStruct((nb, n_blk * T_BLK, d), F32)
        out_spec = pl.BlockSpec((nb, T_BLK, d), lambda i: (0, jnp.maximum(i - lag, 0), 0))
    else:
        out_shape = jax.ShapeDtypeStruct((r, d), F32)
        out_spec = pl.BlockSpec((rb, d), lambda i: (jnp.maximum(i - lag, 0), 0))
    return pl.pallas_call(
        functools.partial(_back_kernel, alpha=alpha),
        out_shape=out_shape,
        grid=(n_blk + lag,),
        in_specs=[row(bw), row(bw, OFF_Y // bw), row(bw),
                  row(d, OFF_G // d), row(d, OFF_G // d + 1), row(d, OFF_G // d + 2), row(d),
                  _const_spec((bw, d), li), _const_spec((bw, d), li), _const_spec((bw, d), li),
                  _const_spec((d, d), li), _const_spec((1, d), li), _const_spec((1, d), li),
                  _const_spec((d, dff), li), _const_spec((d, dff), li), _const_spec((dff, d), li),
                  _const_spec((1, d), li), _const_spec((1, d), li)],
        out_specs=out_spec,
        scratch_shapes=[pltpu.VMEM((rb, d), F32), pltpu.VMEM((rb, d), F32)],
        compiler_params=_cparams(1, "arbitrary"),
        name="merge_ffn",
    )(om, zall, oh, zall, zall, zall, h, wm, ws, wh, wo, g1, b1, wg, wu, wd, g2, b2)


def _permute_w_in_kernel(w_ref, o_ref):
    lat = MLA_Q_RANK + MLA_KV_RANK
    s5_0 = lat + MLA_ROPE
    hg_0 = s5_0 + S5_WIDTH
    wide = ZHG_W + ZG_W
    o_ref[:, 0:S5_WIDTH] = w_ref[:, s5_0:hg_0].astype(BF16)
    o_ref[:, S5_WIDTH:ZA_W] = w_ref[:, 0:lat].astype(BF16)
    o_ref[:, ZA_W:ZA_W + wide] = w_ref[:, hg_0:hg_0 + wide].astype(BF16)
    o_ref[:, ZA_W + wide:Z_W] = jnp.zeros((o_ref.shape[0], ZKR_W), BF16)
    o_ref[:, ZA_W + wide + MLA_NOPE:ZA_W + wide + MLA_NOPE + MLA_ROPE] = w_ref[:, lat:s5_0].astype(BF16)


def _permute_w_in(w):
    depth, d, d_in = w.shape
    rows = 128
    return pl.pallas_call(
        _permute_w_in_kernel,
        out_shape=jax.ShapeDtypeStruct((depth, d, Z_W), BF16),
        grid=(depth, d // rows),
        in_specs=[pl.BlockSpec((None, rows, d_in), lambda l, i: (l, i, 0))],
        out_specs=pl.BlockSpec((None, rows, Z_W), lambda l, i: (l, i, 0)),
        compiler_params=_cparams(2),
        name="permute_w_in",
    )(w)


def _mla_weights(w_uq, w_ukv):
    rq, rkv = w_uq.shape[0], w_ukv.shape[0]
    zpad = HEAD_PAD - MLA_NOPE - MLA_ROPE
    wq = w_uq.reshape(rq, MLA_HEADS, MLA_NOPE + MLA_ROPE)
    q_nope, q_rope = wq[..., :MLA_NOPE], wq[..., MLA_NOPE:]
    zq = jnp.zeros((rq, MLA_HEADS, zpad), w_uq.dtype)
    wq_p = jnp.concatenate([q_nope, q_rope, zq], axis=-1).reshape(rq, -1)
    wkv = w_ukv.reshape(rkv, MLA_HEADS, MLA_NOPE + MLA_V)
    zk = jnp.zeros((rkv, MLA_HEADS, HEAD_PAD - MLA_NOPE), w_ukv.dtype)
    wk_p = jnp.concatenate([wkv[..., :MLA_NOPE], zk], axis=-1).reshape(rkv, -1)
    wv = wkv[..., MLA_NOPE:].reshape(rkv, -1)
    return [a.astype(BF16) for a in (wq_p, wk_p, wv)]


def _rope_consts():
    half = MLA_ROPE // 2
    inv = ROPE_THETA ** (-(jnp.arange(0, MLA_ROPE, 2, dtype=F32) / MLA_ROPE))
    place = np.zeros((half, HEAD_PAD), np.float32)
    place[np.arange(half), MLA_NOPE + np.arange(half)] = 1.0
    place[np.arange(half), MLA_NOPE + half + np.arange(half)] = 1.0
    base = np.zeros((1, HEAD_PAD), np.float32)
    base[0, :MLA_NOPE] = 1.0
    return inv[:, None], jnp.asarray(place, BF16), jnp.asarray(base)


def _s5_params(lam_re, lam_im, log_dt, b_re, b_im, c_re, c_im):
    lr = jnp.minimum(lam_re.astype(F32), -1e-4)
    li = lam_im.astype(F32)
    dt = jnp.exp(log_dt.astype(F32))[:, None]
    mag = jnp.exp(lr * dt)
    ab_r = mag * jnp.cos(li * dt)
    ab_i = mag * jnp.sin(li * dt)
    den = lr * lr + li * li
    nr = ab_r - 1.0
    coef_r = ((nr * lr + ab_i * li) / den)[..., None]
    coef_i = ((ab_i * lr - nr * li) / den)[..., None]
    bb_r = coef_r * b_re.astype(F32) - coef_i * b_im.astype(F32)
    bb_i = coef_r * b_im.astype(F32) + coef_i * b_re.astype(F32)
    n_slab = S5_WIDTH // S5_SLAB
    gps = S5_SLAB // S5_GROUP
    eye = jnp.eye(gps, dtype=F32)

    def in_mat(bb):
        b4 = bb.reshape(n_slab, gps, S5_STATE, S5_GROUP)
        return jnp.einsum('jgnc,gh->jgchn', b4, eye).reshape(n_slab, S5_SLAB, S5_SLAB_STATE)

    def out_mat(cc):
        c4 = cc.astype(F32).reshape(n_slab, gps, S5_GROUP, S5_STATE)
        return jnp.einsum('jgcn,gh->jgnhc', c4, eye).reshape(n_slab, S5_SLAB_STATE, S5_SLAB)

    bm = jnp.concatenate([in_mat(bb_r), in_mat(bb_i)], axis=2).astype(BF16)
    cm = jnp.concatenate([out_mat(c_re), -out_mat(c_im)], axis=1).astype(BF16)
    ar = ab_r.reshape(n_slab, 1, S5_SLAB_STATE)
    ai = ab_i.reshape(n_slab, 1, S5_SLAB_STATE)
    return bm, cm, ar, ai


def kernel(x, positions, meta_tokens, ln_in_g, ln_in_b, w_in, mla_q_norm, mla_w_uq, mla_kv_norm, mla_w_ukv,
           s5_lam_re, s5_lam_im, s5_log_dt, s5_b_re, s5_b_im, s5_c_re, s5_c_im, s5_d, s5_w_glu,
           hg_lb_logits, hg_out_norm, w_br_mla, w_br_s5, w_br_hg, w_out, ln1_g, ln1_b,
           w_ffn_gate, w_ffn_up, w_ffn_down, ln2_g, ln2_b):
    nb, s, d = x.shape
    depth = w_in.shape[0]
    nblk = (s + N_META) // T_BLK
    alpha = (2 * depth) ** 0.25
    row2 = lambda a: a.astype(F32)[None, :]
    rows3 = lambda a: a.astype(F32)[:, None, :]
    bf16 = lambda a: a.astype(BF16)

    meta_pos = jnp.broadcast_to(jnp.arange(N_META, dtype=jnp.int32)[None, :], (nb, N_META))
    pos = jnp.concatenate([meta_pos, positions.astype(jnp.int32) + N_META], axis=1)
    pos = pos.reshape(nb, nblk, T_BLK).transpose(1, 0, 2).reshape(nblk, 1, nb * T_BLK)
    rope = _rope_consts()
    p_lb = jax.nn.softmax(hg_lb_logits.astype(F32), axis=0)
    lower_bounds = jnp.cumsum(p_lb, axis=0) - p_lb[0]
    tri = jnp.asarray(np.tril(np.ones((HG_CHUNK, HG_CHUNK), np.float32)), BF16)
    lvl = jnp.asarray(_hgrn_level_matrix())
    perm = jnp.asarray(_time_major_perm(nb), BF16)

    front_p = (_permute_w_in(w_in), pos, rope, rows3(mla_q_norm), rows3(mla_kv_norm),
               *jax.vmap(_mla_weights)(mla_w_uq, mla_w_ukv), perm,
               *jax.vmap(_s5_params)(s5_lam_re, s5_lam_im, s5_log_dt, s5_b_re, s5_b_im, s5_c_re, s5_c_im),
               rows3(s5_d), bf16(s5_w_glu))
    hgrn_p = (lower_bounds[:, None, :], rows3(hg_out_norm), tri, lvl)
    back_p = (bf16(w_br_mla), bf16(w_br_s5), bf16(w_br_hg), bf16(w_out), rows3(ln1_g), rows3(ln1_b),
              bf16(w_ffn_gate), bf16(w_ffn_up), bf16(w_ffn_down), rows3(ln2_g), rows3(ln2_b))

    h = (x, meta_tokens.astype(x.dtype), row2(ln_in_g), row2(ln_in_b))
    for li in range(depth):
        outs = _front(h, *front_p, nb, li)
        if li == 0:
            h, *outs = outs
        zall, = outs
        o_mla = _attention(zall, nb, nblk)
        o_hg = _hgrn(zall, *hgrn_p, nb, nblk, li)
        h = _back(o_mla, o_hg, zall, h, *back_p, nb, alpha, final=li == depth - 1, li=li)
    return h
```

```python
import functools
import math

import jax
import jax.numpy as jnp
import numpy as np
from jax import lax
from jax.experimental import pallas as pl
from jax.experimental.pallas import tpu as pltpu

F32 = jnp.float32
BF16 = jnp.bfloat16

N_META = 16
MLA_HEADS = 8
MLA_NOPE = 64
MLA_ROPE = 32
MLA_V = 64
MLA_Q_RANK = 256
MLA_KV_RANK = 256
ROPE_THETA = 10000.0
MASK_VALUE = -1e9
LOG2_E = math.log2(math.e)
HEAD_PAD = 128
S5_WIDTH = 512
S5_GROUP = 16
S5_STATE = 64
S5_SLAB = 128
S5_SLAB_STATE = (S5_SLAB // S5_GROUP) * S5_STATE
HG_HEADS = 4
HG_KEY = 128
HG_VAL = 128
HG_QK = HG_HEADS * HG_KEY
HG_VW = HG_HEADS * HG_VAL
HG_F_MIN = 1e-6
HG_CHUNK = 128
HG_LEVELS = (128, 64, 32, 16)
HG_BOTTOM = 8
HG_SLOTS = 3
T_BLK = 16
ATT_TQ = 256
ATT_HEADS_PER_STEP = 4
ATT_STEPS_PER_BODY = 14
ATT_VE = MLA_V + 16
VMEM_LIMIT = 56 * 1024 * 1024

ZA_W = 1024
ZHG_W = 2048
ZG_W = 3072
ZKR_W = 128
Z_W = ZA_W + ZHG_W + ZG_W + ZKR_W
ZALL_WIDTHS = (ZHG_W, ZG_W, MLA_HEADS * HEAD_PAD, MLA_HEADS * HEAD_PAD, MLA_HEADS * MLA_V, S5_WIDTH)
ZALL_OFF = tuple(sum(ZALL_WIDTHS[:i]) for i in range(len(ZALL_WIDTHS)))
ZALL_W = sum(ZALL_WIDTHS)
OFF_HG, OFF_G, OFF_Q, OFF_K, OFF_V, OFF_Y = ZALL_OFF
WIDE_CHUNK = 512
FILL_BEFORE_MLA = 3


def _cparams(n_grid, sem="parallel"):
    return pltpu.CompilerParams(dimension_semantics=(sem,) * n_grid, vmem_limit_bytes=VMEM_LIMIT)


def _const_spec(shape, layer=None):
    nd = len(shape)
    if layer is None:
        return pl.BlockSpec(shape, lambda *_: (0,) * nd, pipeline_mode=pl.Buffered(1))
    return pl.BlockSpec((None,) + tuple(shape), lambda *_: (layer,) + (0,) * nd, pipeline_mode=pl.Buffered(1))


def _seq_spec(nblk, width, col):
    return pl.BlockSpec((nblk, None, T_BLK, width), lambda b, *g: (0, b, 0, col(b, *g)))


def _dot(a, b):
    return jnp.dot(a, b, preferred_element_type=F32)


def _dot_nt(a, b):
    return lax.dot_general(a, b, (((1,), (1,)), ((), ())), preferred_element_type=F32)


def _dot_tn(a, b):
    return lax.dot_general(a, b, (((0,), (0,)), ((), ())), preferred_element_type=F32)


def _sigmoid(x):
    return 1.0 / (1.0 + jnp.exp(-x))


def _layer_norm(x, g, b, eps=1e-5):
    mu = jnp.mean(x, axis=-1, keepdims=True)
    xc = x - mu
    var = jnp.mean(xc * xc, axis=-1, keepdims=True)
    return xc * lax.rsqrt(var + eps) * g + b


def _rms_norm(x, g, eps=1e-6):
    return x * lax.rsqrt(jnp.mean(x * x, axis=-1, keepdims=True) + eps) * g


def _load_rows(ref, blk0, n_blk, cs):
    x = ref[pl.ds(blk0, n_blk), :, cs]
    return x.reshape(n_blk * T_BLK, x.shape[-1])


def _store_rows(ref, blk0, n_blk, cs, x):
    ref[pl.ds(blk0, n_blk), :, cs] = x.reshape(n_blk, T_BLK, x.shape[-1])


def _rope_tables(pos_ref, inv_ref, place_ref, base_ref):
    ang = inv_ref[...] * pos_ref[...].astype(F32)

    def place(t):
        hi = t.astype(BF16)
        lo = (t - hi.astype(F32)).astype(BF16)
        return _dot_tn(hi, place_ref[...]) + _dot_tn(lo, place_ref[...])

    return place(jnp.cos(ang)) + base_ref[...], place(jnp.sin(ang))


def _mla_prep(cq, ckv, kr, cos, sin, qn_ref, kvn_ref, wq_ref, wk_ref, wv_ref, q_ref, k_ref, v_ref):
    scale = (MLA_NOPE + MLA_ROPE) ** -0.5 * LOG2_E
    half = MLA_ROPE // 2
    lane = lax.broadcasted_iota(jnp.int32, (1, HEAD_PAD), 1)
    sin_up = jnp.where((lane >= MLA_NOPE + half) & (lane < MLA_NOPE + MLA_ROPE), sin, 0.0)
    sin_dn = jnp.where((lane >= MLA_NOPE) & (lane < MLA_NOPE + half), -sin, 0.0)

    def rope(x, c, s_up, s_dn):
        return x * c + pltpu.roll(x, half, 1) * s_up + pltpu.roll(x, HEAD_PAD - half, 1) * s_dn

    cqn = _rms_norm(cq, qn_ref[...]).astype(BF16)
    ckvn = _rms_norm(ckv, kvn_ref[...]).astype(BF16)
    k_rope = rope(kr, cos, sin_up, sin_dn)
    cos_q, sin_up_q, sin_dn_q = cos * scale, sin_up * scale, sin_dn * scale
    for h in range(MLA_HEADS):
        cs = slice(h * HEAD_PAD, (h + 1) * HEAD_PAD)
        q_ref[:, cs] = rope(_dot(cqn, wq_ref[:, cs]), cos_q, sin_up_q, sin_dn_q).astype(BF16)
        k_ref[:, cs] = (_dot(ckvn, wk_ref[:, cs]) + k_rope).astype(BF16)
    v_ref[...] = _dot(ckvn, wv_ref[...]).astype(BF16)


def _s5_block(u, perm_ref, bm_ref, cm_ref, ar_ref, ai_ref, d_ref, wg_ref, o_ref,
              xr_ref, xi_ref, buf_ref, y_ref, nb, fillers):
    n_slab = S5_WIDTH // S5_SLAB
    ns = S5_SLAB_STATE
    u = _dot(perm_ref[...], u).astype(BF16)
    for j in range(n_slab):
        buf_ref[...] = _dot(u[:, j * S5_SLAB:(j + 1) * S5_SLAB], bm_ref[j])
        if fillers:
            fillers.pop(0)()
        ar = jnp.broadcast_to(ar_ref[j], (nb, ns))
        ai = jnp.broadcast_to(ai_ref[j], (nb, ns))
        xr = xr_ref[j]
        xi = xi_ref[j]
        for t in range(T_BLK):
            rows = slice(t * nb, (t + 1) * nb)
            nr = ar * xr - ai * xi + buf_ref[rows, 0:ns]
            ni = ar * xi + ai * xr + buf_ref[rows, ns:2 * ns]
            buf_ref[rows, 0:ns] = nr
            buf_ref[rows, ns:2 * ns] = ni
            xr, xi = nr, ni
        xr_ref[j] = xr
        xi_ref[j] = xi
        y_ref[:, j * S5_SLAB:(j + 1) * S5_SLAB] = _dot(buf_ref[...].astype(BF16), cm_ref[j])

    while fillers:
        fillers.pop(0)()
    y = y_ref[...] + d_ref[...] * u.astype(F32)
    y = 0.5 * y * (1.0 + jnp.tanh(math.sqrt(2.0 / math.pi) * (y + 0.044715 * (y * y * y))))
    gate = _sigmoid(_dot(y.astype(BF16), wg_ref[...]))
    out = (y * gate).astype(BF16)
    o_ref[...] = _dot_tn(perm_ref[...], out).astype(BF16)


def _front_kernel(*refs, nb, ln_in):
    if ln_in:
        x_ref, meta_ref, lng_ref, lnb_ref, *refs = refs
    else:
        h_ref, *refs = refs
    (w_ref, pos_ref, inv_ref, place_ref, base_ref, qn_ref, kvn_ref, wq_ref, wk_ref, wv_ref,
     perm_ref, bm_ref, cm_ref, ar_ref, ai_ref, d_ref, wglu_ref, *refs) = refs
    if ln_in:
        hout_ref, *refs = refs
    zall_ref, xr_ref, xi_ref, buf_ref, y_ref = refs
    zhg_ref, zg_ref, q_ref, k_ref, v_ref, ys5_ref = (
        zall_ref.at[:, o:o + w] for o, w in zip(ZALL_OFF, ZALL_WIDTHS))

    @pl.when(pl.program_id(0) == 0)
    def _():
        xr_ref[...] = jnp.zeros_like(xr_ref)
        xi_ref[...] = jnp.zeros_like(xi_ref)

    if ln_in:
        hx = _layer_norm(x_ref[...], lng_ref[...], lnb_ref[...])
        hm = _layer_norm(meta_ref[...], lng_ref[...], lnb_ref[...])
        h = jnp.where(pl.program_id(0) == 0, jnp.broadcast_to(hm[None], hx.shape), hx)
        h = h.reshape(hout_ref.shape)
        hout_ref[...] = h
    else:
        h = h_ref[...]
    x = h.astype(BF16)
    za = _dot(x, w_ref[:, 0:ZA_W])
    zkr = _dot(x, w_ref[:, ZA_W + ZHG_W + ZG_W:Z_W])

    def wide_chunk(o_ref, c, off):
        def run():
            o_ref[:, c:c + WIDE_CHUNK] = _dot(x, w_ref[:, off + c:off + c + WIDE_CHUNK]).astype(BF16)
        return run

    fillers = [wide_chunk(zhg_ref, c, ZA_W) for c in range(0, ZHG_W, WIDE_CHUNK)]
    fillers += [wide_chunk(zg_ref, c, ZA_W + ZHG_W) for c in range(0, ZG_W, WIDE_CHUNK)]
    for _ in range(FILL_BEFORE_MLA):
        fillers.pop(0)()
    cos, sin = _rope_tables(pos_ref, inv_ref, place_ref, base_ref)
    c0 = S5_WIDTH
    _mla_prep(za[:, c0:c0 + MLA_Q_RANK], za[:, c0 + MLA_Q_RANK:c0 + MLA_Q_RANK + MLA_KV_RANK], zkr,
              cos, sin, qn_ref, kvn_ref, wq_ref, wk_ref, wv_ref, q_ref, k_ref, v_ref)
    _s5_block(za[:, 0:S5_WIDTH].astype(BF16), perm_ref, bm_ref, cm_ref, ar_ref, ai_ref, d_ref, wglu_ref, ys5_ref,
              xr_ref, xi_ref, buf_ref, y_ref, nb, fillers)


def _front(src, w, pos, rope, qn, kvn, wq, wk, wv, perm, bm, cm, ar, ai, dskip, wglu, nb, li):
    ln_in = isinstance(src, tuple)
    rb = T_BLK * nb
    if ln_in:
        x, meta, lng, lnb = src
        d = x.shape[-1]
        r = (x.shape[1] + N_META) * nb
        src_specs = [pl.BlockSpec((nb, T_BLK, d), lambda i: (0, jnp.maximum(i - 1, 0), 0)),
                     _const_spec((N_META, d)), _const_spec((1, d)), _const_spec((1, d))]
    else:
        src = (src,)
        r, d = src[0].shape
        src_specs = [pl.BlockSpec((rb, d), lambda i: (i, 0))]
    hw = MLA_HEADS * HEAD_PAD
    vw = MLA_HEADS * MLA_V
    n_slab = S5_WIDTH // S5_SLAB
    row = lambda width: pl.BlockSpec((rb, width), lambda i: (i, 0))
    out_shape = [jax.ShapeDtypeStruct((r, ZALL_W), BF16)]
    out_specs = [row(ZALL_W)]
    if ln_in:
        out_shape.insert(0, jax.ShapeDtypeStruct((r, d), F32))
        out_specs.insert(0, row(d))
    return pl.pallas_call(
        functools.partial(_front_kernel, nb=nb, ln_in=ln_in),
        out_shape=out_shape,
        grid=(r // rb,),
        in_specs=src_specs + [
            _const_spec((d, Z_W), li), pl.BlockSpec((None, 1, rb), lambda i: (i, 0, 0)),
            _const_spec((MLA_ROPE // 2, 1)), _const_spec((MLA_ROPE // 2, HEAD_PAD)), _const_spec((1, HEAD_PAD)),
            _const_spec((1, MLA_Q_RANK), li), _const_spec((1, MLA_KV_RANK), li),
            _const_spec((MLA_Q_RANK, hw), li), _const_spec((MLA_KV_RANK, hw), li), _const_spec((MLA_KV_RANK, vw), li),
            _const_spec((rb, rb)),
            _const_spec((n_slab, S5_SLAB, 2 * S5_SLAB_STATE), li),
            _const_spec((n_slab, 2 * S5_SLAB_STATE, S5_SLAB), li),
            _const_spec((n_slab, 1, S5_SLAB_STATE), li),
            _const_spec((n_slab, 1, S5_SLAB_STATE), li),
            _const_spec((1, S5_WIDTH), li),
            _const_spec((S5_WIDTH, S5_WIDTH), li),
        ],
        out_specs=out_specs,
        scratch_shapes=[
            pltpu.VMEM((n_slab, nb, S5_SLAB_STATE), F32),
            pltpu.VMEM((n_slab, nb, S5_SLAB_STATE), F32),
            pltpu.VMEM((rb, 2 * S5_SLAB_STATE), F32),
            pltpu.VMEM((rb, S5_WIDTH), F32),
        ],
        compiler_params=_cparams(1, "arbitrary"),
        name="front",
    )(*src, w, pos, *rope, qn, kvn, wq, wk, wv, perm, bm, cm, ar, ai, dskip, wglu)


def _attn_kernel(tab_ref, q_ref, k_ref, v_ref, o_ref, vt_ref, qt_ref, m_ref, acc_ref, *bufs, n_heads, tq, nq):
    xs_ref, xm_ref, ps_ref, pm_ref, a_ref = (bufs[2 * i:2 * i + 2] for i in range(5))
    bpq = tq // T_BLK
    vw = n_heads * MLA_V

    def values_t(vb):
        vt = vb.T
        rows = lax.broadcasted_iota(jnp.int32, (ATT_VE - MLA_V, vt.shape[1]), 0)
        ext = jnp.where(rows == 0, 1.0, 0.0).astype(BF16)
        parts = []
        for h in range(n_heads):
            parts += [vt[h * MLA_V:(h + 1) * MLA_V, :], ext]
        return jnp.concatenate(parts, axis=0)

    for j in range(nq):
        vt_ref[j] = values_t(_load_rows(v_ref, 1 + j * bpq, bpq, slice(0, vw)))
    v0t = values_t(v_ref[0])

    for i in range(nq):
        qb = _load_rows(q_ref, 1 + i * bpq, bpq, slice(0, n_heads * HEAD_PAD))
        qt_ref[i] = qb.T

    def causal(st):
        r = lax.broadcasted_iota(jnp.int32, st.shape, 0)
        c = lax.broadcasted_iota(jnp.int32, st.shape, 1)
        return jnp.where(r <= c, st, MASK_VALUE * LOG2_E)

    heads = range(n_heads)
    qs = [slice(h * HEAD_PAD, (h + 1) * HEAD_PAD) for h in heads]
    vs = [slice(h * ATT_VE, (h + 1) * ATT_VE) for h in heads]
    k0 = [k_ref[0, :, c] for c in qs]

    def normalised(acc):
        return acc[0:MLA_V, :] / acc[MLA_V:MLA_V + 1, :]

    outs = []
    for n in heads:
        st = causal(_dot_nt(k0[n], q_ref[0, :, qs[n]]))
        p = jnp.exp2(st - jnp.max(st, axis=0, keepdims=True))
        outs.append(normalised(_dot(v0t[vs[n], :], p.astype(BF16))))
    o_ref[0, :, :] = jnp.concatenate(outs, axis=0).T.astype(BF16)

    def key_tile(j, n):
        return _load_rows(k_ref, 1 + j * bpq, bpq, qs[n])

    def diag_scores(i, s):
        for n in heads:
            xs_ref[s][n] = _dot(key_tile(i, n), qt_ref[i, qs[n], :])
            xm_ref[s][n] = _dot(k0[n], qt_ref[i, qs[n], :])

    def diag_stats(i, s):
        for n in heads:
            st, sm = causal(xs_ref[s][n]), xm_ref[s][n]
            m = jnp.maximum(jnp.max(st, axis=0, keepdims=True), jnp.max(sm, axis=0, keepdims=True))
            p = jnp.exp2(st - m)
            pm = jnp.exp2(sm - m)
            m_ref[i, n] = m
            ps_ref[s][n] = p.astype(BF16)
            pm_ref[s][n] = pm.astype(BF16)

    def diag_values(i, s):
        for n in heads:
            acc_ref[i, vs[n], :] = _dot(vt_ref[i, vs[n], :], ps_ref[s][n]) + _dot(v0t[vs[n], :], pm_ref[s][n])

    _pipeline3(nq, diag_scores, diag_stats, diag_values, steps_per_body=ATT_STEPS_PER_BODY)

    def off_scores(f, s):
        i, j = tab_ref[0, f], tab_ref[1, f]
        for n in heads:
            xs_ref[s][n] = _dot(key_tile(j, n), qt_ref[i, qs[n], :])

    def off_stats(f, s):
        i = tab_ref[0, f]
        for n in heads:
            x = xs_ref[s][n]
            m_old = m_ref[i, n]
            m = jnp.maximum(m_old, jnp.max(x, axis=0, keepdims=True))
            a = jnp.exp2(m_old - m)
            p = jnp.exp2(x - m)
            m_ref[i, n] = m
            a_ref[s][n] = a
            ps_ref[s][n] = p.astype(BF16)

    def off_values(f, s):
        i, j = tab_ref[0, f], tab_ref[1, f]
        for n in heads:
            acc_ref[i, vs[n], :] = a_ref[s][n] * acc_ref[i, vs[n], :] + _dot(vt_ref[j, vs[n], :], ps_ref[s][n])

    _pipeline3(nq * (nq - 1) // 2, off_scores, off_stats, off_values, steps_per_body=ATT_STEPS_PER_BODY)

    for i in range(nq):
        o = jnp.concatenate([normalised(acc_ref[i, vs[n], :]) for n in heads], axis=0)
        _store_rows(o_ref, 1 + i * bpq, bpq, slice(0, vw), o.astype(BF16).T)


def _pipeline3(n, stage1, stage2, stage3, n_slots=2, steps_per_body=None):
    if n == 0:
        return
    stage1(0, 0)
    if n > 1:
        stage1(1, 1 % n_slots)
    stage2(0, 0)

    def step(t, k):
        stage1(t + 2, (k + 2) % n_slots)
        stage2(t + 1, (k + 1) % n_slots)
        stage3(t, k)

    steady = max(n - 2, 0)
    spb = steps_per_body or n_slots
    trips = steady // spb
    if trips:
        def body(u, carry):
            for k in range(spb):
                step(spb * u + k, k % n_slots)
            return carry

        lax.fori_loop(0, trips, body, 0)
    for t in range(trips * spb, steady):
        step(t, t % n_slots)
    if n > 1:
        stage3(n - 2, (n - 2) % n_slots)
        stage2(n - 1, (n - 1) % n_slots)
    stage3(n - 1, (n - 1) % n_slots)


def _off_diagonal_order(nq):
    left = [(i, j) for i in range(nq) for j in range(i)]
    order = []
    while left:
        count = {}
        for i, _ in left:
            count[i] = count.get(i, 0) + 1
        ok = [p for p in left if not order or p[0] != order[-1][0]] or left
        pick = max(ok, key=lambda p: (count[p[0]], -p[1]))
        order.append(pick)
        left.remove(pick)
    return order or [(0, 0)]


def _attention(zall, nb, nblk):
    z4 = zall.reshape(nblk, nb, T_BLK, ZALL_W)
    hps = ATT_HEADS_PER_STEP
    n_hp = MLA_HEADS // hps
    qw = hps * HEAD_PAD
    vw = hps * MLA_V
    s = (nblk - 1) * T_BLK
    tq = min(ATT_TQ, s)
    nq = s // tq
    col = lambda b, p: p
    tab = jnp.asarray(np.array(_off_diagonal_order(nq), np.int32).T)
    o = pl.pallas_call(
        functools.partial(_attn_kernel, n_heads=hps, tq=tq, nq=nq),
        out_shape=jax.ShapeDtypeStruct((nblk, nb, T_BLK, MLA_HEADS * MLA_V), BF16),
        grid=(nb, n_hp),
        in_specs=[pl.BlockSpec(memory_space=pltpu.SMEM),
                  _seq_spec(nblk, qw, lambda b, p: OFF_Q // qw + p),
                  _seq_spec(nblk, qw, lambda b, p: OFF_K // qw + p),
                  _seq_spec(nblk, vw, lambda b, p: OFF_V // vw + p)],
        out_specs=_seq_spec(nblk, vw, col),
        scratch_shapes=[pltpu.VMEM((nq, hps * ATT_VE, tq), BF16), pltpu.VMEM((nq, qw, tq), BF16),
                        pltpu.VMEM((nq, hps, 1, tq), F32), pltpu.VMEM((nq, hps * ATT_VE, tq), F32)]
        + 2 * [pltpu.VMEM((hps, tq, tq), F32)] + 2 * [pltpu.VMEM((hps, N_META, tq), F32)]
        + 2 * [pltpu.VMEM((hps, tq, tq), BF16)] + 2 * [pltpu.VMEM((hps, N_META, tq), BF16)]
        + 2 * [pltpu.VMEM((hps, 1, tq), F32)],
        compiler_params=_cparams(2),
        name="mla_attn",
    )(tab, z4, z4, z4)
    return o.reshape(nblk * nb * T_BLK, MLA_HEADS * MLA_V)


def _time_major_perm(nb):
    p = np.zeros((T_BLK * nb, T_BLK * nb), np.float32)
    for b in range(nb):
        for t in range(T_BLK):
            p[t * nb + b, b * T_BLK + t] = 1.0
    return p


def _block_row_bcast(x, m, row):
    t, c = x.shape
    if m == t:
        return jnp.broadcast_to(x[row:row + 1, :], x.shape)
    x3 = x.reshape(t // m, m, c)
    return jnp.broadcast_to(x3[:, row:row + 1, :], x3.shape).reshape(t, c)


def _hgrn_gates(blk0, n_blk, f_ref, lb_ref):
    lb = lb_ref[...]
    zf = _load_rows(f_ref, blk0, n_blk, slice(0, HG_QK)).astype(F32)
    e = jnp.exp(-jnp.abs(zf))
    rcp = 1.0 / (1.0 + e)
    pos = zf >= 0.0
    sig_p = jnp.where(pos, rcp, e * rcp)
    sig_n = jnp.where(pos, e * rcp, rcp)
    f = lb + (1.0 - lb) * sig_p
    log_f = jnp.log2(jnp.maximum(f, HG_F_MIN))
    k = (1.0 - lb) * sig_n
    hi = log_f.astype(BF16)
    lo = (log_f - hi.astype(F32)).astype(BF16)
    return k, hi, lo


def _hgrn_mix(blk0, n_blk, gates, q_ref, v_ref, tri_ref, lvl_ref, st_ref):
    t = n_blk * T_BLK
    cs = slice(0, HG_QK)
    heads = [slice(h * HG_KEY, (h + 1) * HG_KEY) for h in range(HG_HEADS)]
    k, hi, lo = gates
    q = _load_rows(q_ref, blk0, n_blk, cs).astype(F32)
    v = _load_rows(v_ref, blk0, n_blk, cs)
    tri = tri_ref[0:t, 0:t]
    cum = _dot(tri, hi) + _dot(tri, lo)
    lvl = lvl_ref[0:t, 0:t]
    r_idx = lax.broadcasted_iota(jnp.int32, (t, HG_QK), 0)
    c8 = _block_row_bcast(cum, HG_BOTTOM, HG_BOTTOM // 2 - 1)
    qe = (q * jnp.exp2(cum - c8)).astype(BF16)
    ke = (k * jnp.exp2(c8 - cum)).astype(BF16)
    n_lvl = len(HG_LEVELS)
    scores = [jnp.where(lvl == n_lvl, _dot_nt(qe[:, hs], ke[:, hs]), 0.0) for hs in heads]
    for li, m in enumerate(HG_LEVELS):
        if m > t:
            continue
        half = m // 2
        cmid = _block_row_bcast(cum, m, half - 1)
        upper = (r_idx & (m - 1)) >= half
        ex = jnp.exp2(jnp.where(upper, cum - cmid, cmid - cum))
        qe = jnp.where(upper, q * ex, 0.0).astype(BF16)
        ke = jnp.where(upper, 0.0, k * ex).astype(BF16)
        scores = [jnp.where(lvl == li, _dot_nt(qe[:, hs], ke[:, hs]), s) for s, hs in zip(scores, heads)]
    qd = (q * jnp.exp2(cum)).astype(BF16)
    last = cum[t - 1:t, :]
    kd = (k * jnp.exp2(last - cum)).astype(BF16)
    dec = jnp.exp2(last)
    outs = []
    for h, hs in enumerate(heads):
        st = st_ref[h]
        outs.append(_dot(scores[h].astype(BF16), v[:, hs]) + _dot_nt(qd[:, hs], st.astype(BF16)))
        st_ref[h] = st * dec[:, hs] + _dot_tn(v[:, hs], kd[:, hs])
    return tuple(outs)


def _hgrn_out(blk0, n_blk, outs, g_ref, on_ref, o_ref):
    cs = slice(0, HG_VW)
    g = _load_rows(g_ref, blk0, n_blk, cs).astype(F32)
    o = jnp.concatenate([o * lax.rsqrt(jnp.mean(o * o, axis=-1, keepdims=True) + 1e-6) for o in outs], axis=1)
    _store_rows(o_ref, blk0, n_blk, cs, (o * on_ref[...] * (g * _sigmoid(g))).astype(BF16))


def _hgrn_kernel(q_ref, f_ref, v_ref, g_ref, lb_ref, on_ref, tri_ref, lvl_ref, o_ref, st_ref, *bufs, n_chunks):
    kb_ref, hi_ref, lo_ref, ob_ref = (bufs[HG_SLOTS * i:HG_SLOTS * (i + 1)] for i in range(4))
    bpc = HG_CHUNK // T_BLK
    heads = [slice(h * HG_VAL, (h + 1) * HG_VAL) for h in range(HG_HEADS)]
    gates = functools.partial(_hgrn_gates, f_ref=f_ref, lb_ref=lb_ref)
    mix = functools.partial(_hgrn_mix, q_ref=q_ref, v_ref=v_ref, tri_ref=tri_ref, lvl_ref=lvl_ref, st_ref=st_ref)
    out = functools.partial(_hgrn_out, g_ref=g_ref, on_ref=on_ref, o_ref=o_ref)
    st_ref[...] = jnp.zeros_like(st_ref)
    out(0, 1, mix(0, 1, gates(0, 1)))

    def stage_gates(c, s):
        k, hi, lo = gates(1 + c * bpc, bpc)
        kb_ref[s][...], hi_ref[s][...], lo_ref[s][...] = k, hi.astype(F32), lo.astype(F32)

    def stage_mix(c, s):
        outs = mix(1 + c * bpc, bpc, (kb_ref[s][...], hi_ref[s][...].astype(BF16), lo_ref[s][...].astype(BF16)))
        for hs, o in zip(heads, outs):
            ob_ref[s][:, hs] = o

    def stage_out(c, s):
        out(1 + c * bpc, bpc, tuple(ob_ref[s][:, hs] for hs in heads))

    _pipeline3(n_chunks, stage_gates, stage_mix, stage_out, n_slots=HG_SLOTS)


def _hgrn(zall, lb, onorm, tri, lvl, nb, nblk, li):
    z4 = zall.reshape(nblk, nb, T_BLK, ZALL_W)
    col = lambda j: _seq_spec(nblk, HG_QK, lambda b: OFF_HG // HG_QK + j)
    o = pl.pallas_call(
        functools.partial(_hgrn_kernel, n_chunks=(nblk - 1) * T_BLK // HG_CHUNK),
        out_shape=jax.ShapeDtypeStruct((nblk, nb, T_BLK, HG_VW), BF16),
        grid=(nb,),
        in_specs=[col(0), col(1), col(2), col(3),
                  _const_spec((1, HG_QK), li), _const_spec((1, HG_VW), li),
                  _const_spec((HG_CHUNK, HG_CHUNK)), _const_spec((HG_CHUNK, HG_CHUNK))],
        out_specs=_seq_spec(nblk, HG_VW, lambda b: 0),
        scratch_shapes=[pltpu.VMEM((HG_HEADS, HG_VAL, HG_KEY), F32)]
        + 3 * HG_SLOTS * [pltpu.VMEM((HG_CHUNK, HG_QK), F32)]
        + HG_SLOTS * [pltpu.VMEM((HG_CHUNK, HG_VW), F32)],
        compiler_params=_cparams(1),
        name="hgrn2",
    )(z4, z4, z4, z4, lb, onorm, tri, lvl)
    return o.reshape(nblk * nb * T_BLK, HG_VW)


def _hgrn_level_matrix():
    r = np.arange(HG_CHUNK)[:, None]
    c = np.arange(HG_CHUNK)[None, :]
    lvl = np.zeros((HG_CHUNK, HG_CHUNK), np.int32)
    for li, m in enumerate(HG_LEVELS):
        lvl = np.where(r // m == c // m, li, lvl)
    lvl = np.where(r // HG_BOTTOM == c // HG_BOTTOM, len(HG_LEVELS), lvl)
    return np.where(c <= r, lvl, -1).astype(np.int32)


def _back_kernel(om_ref, os_ref, oh_ref, gm_ref, gs_ref, gh_ref, h_ref,
                 wm_ref, ws_ref, wh_ref, wo_ref, g1_ref, b1_ref,
                 wg_ref, wu_ref, wd_ref, g2_ref, b2_ref, o_ref, r1_ref, r2_ref, *, alpha):
    @pl.when(pl.program_id(0) == 0)
    def _():
        r1_ref[...] = jnp.zeros_like(r1_ref)
        r2_ref[...] = jnp.zeros_like(r2_ref)

    r1 = r1_ref[...]
    r2 = r2_ref[...]
    ym = _dot(om_ref[...], wm_ref[...])
    ys = _dot(os_ref[...], ws_ref[...])
    yh = _dot(oh_ref[...], wh_ref[...])
    o_ref[...] = _layer_norm(r2, g2_ref[...], b2_ref[...]).reshape(o_ref.shape)
    h1 = _layer_norm(r1, g1_ref[...], b1_ref[...])
    hb = h1.astype(BF16)
    a = _dot(hb, wg_ref[...])
    u = _dot(hb, wu_ref[...])
    mixed = _sigmoid(gm_ref[...].astype(F32)) * ym
    mixed += _sigmoid(gs_ref[...].astype(F32)) * ys
    mixed += _sigmoid(gh_ref[...].astype(F32)) * yh
    r1_ref[...] = alpha * h_ref[...] + _dot(mixed.astype(BF16), wo_ref[...])
    r2_ref[...] = alpha * h1 + _dot((a * _sigmoid(a) * u).astype(BF16), wd_ref[...])


def _back(om, oh, zall, h, wm, ws, wh, wo, g1, b1, wg, wu, wd, g2, b2, nb, alpha, final, li):
    r, d = h.shape
    bw = om.shape[1]
    dff = wg.shape[-1]
    rb = T_BLK * nb
    skip = 1 if final else 0
    n_blk = r // rb - skip
    lag = 2
    row = lambda width, j=0: pl.BlockSpec((rb, width), lambda i: (jnp.minimum(i, n_blk - 1) + skip, j))
    if final:
        out_shape = jax.ShapeDtypeStruct((nb, n_blk * T_BLK, d), F32)
        out_spec = pl.BlockSpec((nb, T_BLK, d), lambda i: (0, jnp.maximum(i - lag, 0), 0))
    else:
        out_shape = jax.ShapeDtypeStruct((r, d), F32)
        out_spec = pl.BlockSpec((rb, d), lambda i: (jnp.maximum(i - lag, 0), 0))
    return pl.pallas_call(
        functools.partial(_back_kernel, alpha=alpha),
        out_shape=out_shape,
        grid=(n_blk + lag,),
        in_specs=[row(bw), row(bw, OFF_Y // bw), row(bw),
                  row(d, OFF_G // d), row(d, OFF_G // d + 1), row(d, OFF_G // d + 2), row(d),
                  _const_spec((bw, d), li), _const_spec((bw, d), li), _const_spec((bw, d), li),
                  _const_spec((d, d), li), _const_spec((1, d), li), _const_spec((1, d), li),
                  _const_spec((d, dff), li), _const_spec((d, dff), li), _const_spec((dff, d), li),
                  _const_spec((1, d), li), _const_spec((1, d), li)],
        out_specs=out_spec,
        scratch_shapes=[pltpu.VMEM((rb, d), F32), pltpu.VMEM((rb, d), F32)],
        compiler_params=_cparams(1, "arbitrary"),
        name="merge_ffn",
    )(om, zall, oh, zall, zall, zall, h, wm, ws, wh, wo, g1, b1, wg, wu, wd, g2, b2)


def _permute_w_in_kernel(w_ref, o_ref):
    lat = MLA_Q_RANK + MLA_KV_RANK
    s5_0 = lat + MLA_ROPE
    hg_0 = s5_0 + S5_WIDTH
    wide = ZHG_W + ZG_W
    o_ref[:, 0:S5_WIDTH] = w_ref[:, s5_0:hg_0].astype(BF16)
    o_ref[:, S5_WIDTH:ZA_W] = w_ref[:, 0:lat].astype(BF16)
    o_ref[:, ZA_W:ZA_W + wide] = w_ref[:, hg_0:hg_0 + wide].astype(BF16)
    o_ref[:, ZA_W + wide:Z_W] = jnp.zeros((o_ref.shape[0], ZKR_W), BF16)
    o_ref[:, ZA_W + wide + MLA_NOPE:ZA_W + wide + MLA_NOPE + MLA_ROPE] = w_ref[:, lat:s5_0].astype(BF16)


def _permute_w_in(w):
    depth, d, d_in = w.shape
    rows = 128
    return pl.pallas_call(
        _permute_w_in_kernel,
        out_shape=jax.ShapeDtypeStruct((depth, d, Z_W), BF16),
        grid=(depth, d // rows),
        in_specs=[pl.BlockSpec((None, rows, d_in), lambda l, i: (l, i, 0))],
        out_specs=pl.BlockSpec((None, rows, Z_W), lambda l, i: (l, i, 0)),
        compiler_params=_cparams(2),
        name="permute_w_in",
    )(w)


def _mla_weights(w_uq, w_ukv):
    rq, rkv = w_uq.shape[0], w_ukv.shape[0]
    zpad = HEAD_PAD - MLA_NOPE - MLA_ROPE
    wq = w_uq.reshape(rq, MLA_HEADS, MLA_NOPE + MLA_ROPE)
    q_nope, q_rope = wq[..., :MLA_NOPE], wq[..., MLA_NOPE:]
    zq = jnp.zeros((rq, MLA_HEADS, zpad), w_uq.dtype)
    wq_p = jnp.concatenate([q_nope, q_rope, zq], axis=-1).reshape(rq, -1)
    wkv = w_ukv.reshape(rkv, MLA_HEADS, MLA_NOPE + MLA_V)
    zk = jnp.zeros((rkv, MLA_HEADS, HEAD_PAD - MLA_NOPE), w_ukv.dtype)
    wk_p = jnp.concatenate([wkv[..., :MLA_NOPE], zk], axis=-1).reshape(rkv, -1)
    wv = wkv[..., MLA_NOPE:].reshape(rkv, -1)
    return [a.astype(BF16) for a in (wq_p, wk_p, wv)]


def _rope_consts():
    half = MLA_ROPE // 2
    inv = ROPE_THETA ** (-(jnp.arange(0, MLA_ROPE, 2, dtype=F32) / MLA_ROPE))
    place = np.zeros((half, HEAD_PAD), np.float32)
    place[np.arange(half), MLA_NOPE + np.arange(half)] = 1.0
    place[np.arange(half), MLA_NOPE + half + np.arange(half)] = 1.0
    base = np.zeros((1, HEAD_PAD), np.float32)
    base[0, :MLA_NOPE] = 1.0
    return inv[:, None], jnp.asarray(place, BF16), jnp.asarray(base)


def _s5_params(lam_re, lam_im, log_dt, b_re, b_im, c_re, c_im):
    lr = jnp.minimum(lam_re.astype(F32), -1e-4)
    li = lam_im.astype(F32)
    dt = jnp.exp(log_dt.astype(F32))[:, None]
    mag = jnp.exp(lr * dt)
    ab_r = mag * jnp.cos(li * dt)
    ab_i = mag * jnp.sin(li * dt)
    den = lr * lr + li * li
    nr = ab_r - 1.0
    coef_r = ((nr * lr + ab_i * li) / den)[..., None]
    coef_i = ((ab_i * lr - nr * li) / den)[..., None]
    bb_r = coef_r * b_re.astype(F32) - coef_i * b_im.astype(F32)
    bb_i = coef_r * b_im.astype(F32) + coef_i * b_re.astype(F32)
    n_slab = S5_WIDTH // S5_SLAB
    gps = S5_SLAB // S5_GROUP
    eye = jnp.eye(gps, dtype=F32)

    def in_mat(bb):
        b4 = bb.reshape(n_slab, gps, S5_STATE, S5_GROUP)
        return jnp.einsum('jgnc,gh->jgchn', b4, eye).reshape(n_slab, S5_SLAB, S5_SLAB_STATE)

    def out_mat(cc):
        c4 = cc.astype(F32).reshape(n_slab, gps, S5_GROUP, S5_STATE)
        return jnp.einsum('jgcn,gh->jgnhc', c4, eye).reshape(n_slab, S5_SLAB_STATE, S5_SLAB)

    bm = jnp.concatenate([in_mat(bb_r), in_mat(bb_i)], axis=2).astype(BF16)
    cm = jnp.concatenate([out_mat(c_re), -out_mat(c_im)], axis=1).astype(BF16)
    ar = ab_r.reshape(n_slab, 1, S5_SLAB_STATE)
    ai = ab_i.reshape(n_slab, 1, S5_SLAB_STATE)
    return bm, cm, ar, ai


def kernel(x, positions, meta_tokens, ln_in_g, ln_in_b, w_in, mla_q_norm, mla_w_uq, mla_kv_norm, mla_w_ukv,
           s5_lam_re, s5_lam_im, s5_log_dt, s5_b_re, s5_b_im, s5_c_re, s5_c_im, s5_d, s5_w_glu,
           hg_lb_logits, hg_out_norm, w_br_mla, w_br_s5, w_br_hg, w_out, ln1_g, ln1_b,
           w_ffn_gate, w_ffn_up, w_ffn_down, ln2_g, ln2_b):
    nb, s, d = x.shape
    depth = w_in.shape[0]
    assert N_META == T_BLK and s % HG_CHUNK == 0 and s % min(ATT_TQ, s) == 0, (nb, s, d)
    assert meta_tokens.shape == (N_META, d) and positions.shape == (nb, s), (meta_tokens.shape, positions.shape)
    nblk = (s + N_META) // T_BLK
    alpha = (2 * depth) ** 0.25
    row2 = lambda a: a.astype(F32)[None, :]
    rows3 = lambda a: a.astype(F32)[:, None, :]
    bf16 = lambda a: a.astype(BF16)

    meta_pos = jnp.broadcast_to(jnp.arange(N_META, dtype=jnp.int32)[None, :], (nb, N_META))
    pos = jnp.concatenate([meta_pos, positions.astype(jnp.int32) + N_META], axis=1)
    pos = pos.reshape(nb, nblk, T_BLK).transpose(1, 0, 2).reshape(nblk, 1, nb * T_BLK)
    rope = _rope_consts()
    p_lb = jax.nn.softmax(hg_lb_logits.astype(F32), axis=0)
    lower_bounds = jnp.cumsum(p_lb, axis=0) - p_lb[0]
    tri = jnp.asarray(np.tril(np.ones((HG_CHUNK, HG_CHUNK), np.float32)), BF16)
    lvl = jnp.asarray(_hgrn_level_matrix())
    perm = jnp.asarray(_time_major_perm(nb), BF16)

    front_p = (_permute_w_in(w_in), pos, rope, rows3(mla_q_norm), rows3(mla_kv_norm),
               *jax.vmap(_mla_weights)(mla_w_uq, mla_w_ukv), perm,
               *jax.vmap(_s5_params)(s5_lam_re, s5_lam_im, s5_log_dt, s5_b_re, s5_b_im, s5_c_re, s5_c_im),
               rows3(s5_d), bf16(s5_w_glu))
    hgrn_p = (lower_bounds[:, None, :], rows3(hg_out_norm), tri, lvl)
    back_p = (bf16(w_br_mla), bf16(w_br_s5), bf16(w_br_hg), bf16(w_out), rows3(ln1_g), rows3(ln1_b),
              bf16(w_ffn_gate), bf16(w_ffn_up), bf16(w_ffn_down), rows3(ln2_g), rows3(ln2_b))

    h = (x, meta_tokens.astype(x.dtype), row2(ln_in_g), row2(ln_in_b))
    for li in range(depth):
        outs = _front(h, *front_p, nb, li)
        if li == 0:
            h, *outs = outs
        zall, = outs
        o_mla = _attention(zall, nb, nblk)
        o_hg = _hgrn(zall, *hgrn_p, nb, nblk, li)
        h = _back(o_mla, o_hg, zall, h, *back_p, nb, alpha, final=li == depth - 1, li=li)
    return h
```

```python
import functools
import math

import jax
import jax.numpy as jnp
import numpy as np
from jax import lax
from jax.experimental import pallas as pl
from jax.experimental.pallas import tpu as pltpu

F32 = jnp.float32
BF16 = jnp.bfloat16

N_META = 16
MLA_HEADS = 8
MLA_NOPE = 64
MLA_ROPE = 32
MLA_V = 64
MLA_Q_RANK = 256
MLA_KV_RANK = 256
ROPE_THETA = 10000.0
MASK_VALUE = -1e9
LOG2_E = math.log2(math.e)
HEAD_PAD = 128
S5_WIDTH = 512
S5_GROUP = 16
S5_STATE = 64
S5_SLAB = 128
S5_SLAB_STATE = (S5_SLAB // S5_GROUP) * S5_STATE
HG_HEADS = 4
HG_KEY = 128
HG_VAL = 128
HG_QK = HG_HEADS * HG_KEY
HG_VW = HG_HEADS * HG_VAL
HG_F_MIN = 1e-6
HG_CHUNK = 128
HG_LEVELS = (128, 64, 32, 16)
HG_BOTTOM = 8
HG_SLOTS = 3
T_BLK = 16
ATT_TQ = 256
ATT_HEADS_PER_STEP = 4
ATT_VE = MLA_V + 16
VMEM_LIMIT = 56 * 1024 * 1024

ZA_W = 1024
ZHG_W = 2048
ZG_W = 3072
ZKR_W = 128
Z_W = ZA_W + ZHG_W + ZG_W + ZKR_W
ZALL_WIDTHS = (ZHG_W, ZG_W, MLA_HEADS * HEAD_PAD, MLA_HEADS * HEAD_PAD, MLA_HEADS * MLA_V, S5_WIDTH)
ZALL_OFF = tuple(sum(ZALL_WIDTHS[:i]) for i in range(len(ZALL_WIDTHS)))
ZALL_W = sum(ZALL_WIDTHS)
OFF_HG, OFF_G, OFF_Q, OFF_K, OFF_V, OFF_Y = ZALL_OFF
WIDE_CHUNK = 512
FILL_BEFORE_MLA = 3


def _cparams(n_grid, sem="parallel"):
    return pltpu.CompilerParams(dimension_semantics=(sem,) * n_grid, vmem_limit_bytes=VMEM_LIMIT)


def _const_spec(shape, layer=None):
    nd = len(shape)
    if layer is None:
        return pl.BlockSpec(shape, lambda *_: (0,) * nd, pipeline_mode=pl.Buffered(1))
    return pl.BlockSpec((None,) + tuple(shape), lambda *_: (layer,) + (0,) * nd, pipeline_mode=pl.Buffered(1))


def _seq_spec(nblk, width, col):
    return pl.BlockSpec((nblk, None, T_BLK, width), lambda b, *g: (0, b, 0, col(b, *g)))


def _dot(a, b):
    return jnp.dot(a, b, preferred_element_type=F32)


def _dot_nt(a, b):
    return lax.dot_general(a, b, (((1,), (1,)), ((), ())), preferred_element_type=F32)


def _dot_tn(a, b):
    return lax.dot_general(a, b, (((0,), (0,)), ((), ())), preferred_element_type=F32)


def _sigmoid(x):
    return 1.0 / (1.0 + jnp.exp(-x))


def _layer_norm(x, g, b, eps=1e-5):
    mu = jnp.mean(x, axis=-1, keepdims=True)
    xc = x - mu
    var = jnp.mean(xc * xc, axis=-1, keepdims=True)
    return xc * lax.rsqrt(var + eps) * g + b


def _rms_norm(x, g, eps=1e-6):
    return x * lax.rsqrt(jnp.mean(x * x, axis=-1, keepdims=True) + eps) * g


def _load_rows(ref, blk0, n_blk, cs):
    x = ref[pl.ds(blk0, n_blk), :, cs]
    return x.reshape(n_blk * T_BLK, x.shape[-1])


def _store_rows(ref, blk0, n_blk, cs, x):
    ref[pl.ds(blk0, n_blk), :, cs] = x.reshape(n_blk, T_BLK, x.shape[-1])


def _rope_tables(pos_ref, inv_ref, place_ref, base_ref):
    ang = inv_ref[...] * pos_ref[...].astype(F32)

    def place(t):
        hi = t.astype(BF16)
        lo = (t - hi.astype(F32)).astype(BF16)
        return _dot_tn(hi, place_ref[...]) + _dot_tn(lo, place_ref[...])

    return place(jnp.cos(ang)) + base_ref[...], place(jnp.sin(ang))


def _mla_prep(cq, ckv, kr, cos, sin, qn_ref, kvn_ref, wq_ref, wk_ref, wv_ref, q_ref, k_ref, v_ref):
    scale = (MLA_NOPE + MLA_ROPE) ** -0.5 * LOG2_E
    half = MLA_ROPE // 2
    lane = lax.broadcasted_iota(jnp.int32, (1, HEAD_PAD), 1)
    sin_up = jnp.where((lane >= MLA_NOPE + half) & (lane < MLA_NOPE + MLA_ROPE), sin, 0.0)
    sin_dn = jnp.where((lane >= MLA_NOPE) & (lane < MLA_NOPE + half), -sin, 0.0)

    def rope(x, c, s_up, s_dn):
        return x * c + pltpu.roll(x, half, 1) * s_up + pltpu.roll(x, HEAD_PAD - half, 1) * s_dn

    cqn = _rms_norm(cq, qn_ref[...]).astype(BF16)
    ckvn = _rms_norm(ckv, kvn_ref[...]).astype(BF16)
    k_rope = rope(kr, cos, sin_up, sin_dn)
    cos_q, sin_up_q, sin_dn_q = cos * scale, sin_up * scale, sin_dn * scale
    for h in range(MLA_HEADS):
        cs = slice(h * HEAD_PAD, (h + 1) * HEAD_PAD)
        q_ref[:, cs] = rope(_dot(cqn, wq_ref[:, cs]), cos_q, sin_up_q, sin_dn_q).astype(BF16)
        k_ref[:, cs] = (_dot(ckvn, wk_ref[:, cs]) + k_rope).astype(BF16)
    v_ref[...] = _dot(ckvn, wv_ref[...]).astype(BF16)


def _s5_block(u, perm_ref, bm_ref, cm_ref, ar_ref, ai_ref, d_ref, wg_ref, o_ref,
              xr_ref, xi_ref, buf_ref, y_ref, nb, fillers):
    n_slab = S5_WIDTH // S5_SLAB
    ns = S5_SLAB_STATE
    u = _dot(perm_ref[...], u).astype(BF16)
    for j in range(n_slab):
        buf_ref[...] = _dot(u[:, j * S5_SLAB:(j + 1) * S5_SLAB], bm_ref[j])
        if fillers:
            fillers.pop(0)()
        ar = jnp.broadcast_to(ar_ref[j], (nb, ns))
        ai = jnp.broadcast_to(ai_ref[j], (nb, ns))
        xr = xr_ref[j]
        xi = xi_ref[j]
        for t in range(T_BLK):
            rows = slice(t * nb, (t + 1) * nb)
            nr = ar * xr - ai * xi + buf_ref[rows, 0:ns]
            ni = ar * xi + ai * xr + buf_ref[rows, ns:2 * ns]
            buf_ref[rows, 0:ns] = nr
            buf_ref[rows, ns:2 * ns] = ni
            xr, xi = nr, ni
        xr_ref[j] = xr
        xi_ref[j] = xi
        y_ref[:, j * S5_SLAB:(j + 1) * S5_SLAB] = _dot(buf_ref[...].astype(BF16), cm_ref[j])

    while fillers:
        fillers.pop(0)()
    y = y_ref[...] + d_ref[...] * u.astype(F32)
    y = 0.5 * y * (1.0 + jnp.tanh(math.sqrt(2.0 / math.pi) * (y + 0.044715 * (y * y * y))))
    gate = _sigmoid(_dot(y.astype(BF16), wg_ref[...]))
    out = (y * gate).astype(BF16)
    o_ref[...] = _dot_tn(perm_ref[...], out).astype(BF16)


def _front_kernel(*refs, nb, ln_in):
    if ln_in:
        x_ref, meta_ref, lng_ref, lnb_ref, *refs = refs
    else:
        h_ref, *refs = refs
    (w_ref, pos_ref, inv_ref, place_ref, base_ref, qn_ref, kvn_ref, wq_ref, wk_ref, wv_ref,
     perm_ref, bm_ref, cm_ref, ar_ref, ai_ref, d_ref, wglu_ref, *refs) = refs
    if ln_in:
        hout_ref, *refs = refs
    zall_ref, xr_ref, xi_ref, buf_ref, y_ref = refs
    zhg_ref, zg_ref, q_ref, k_ref, v_ref, ys5_ref = (
        zall_ref.at[:, o:o + w] for o, w in zip(ZALL_OFF, ZALL_WIDTHS))

    @pl.when(pl.program_id(0) == 0)
    def _():
        xr_ref[...] = jnp.zeros_like(xr_ref)
        xi_ref[...] = jnp.zeros_like(xi_ref)

    if ln_in:
        hx = _layer_norm(x_ref[...], lng_ref[...], lnb_ref[...])
        hm = _layer_norm(meta_ref[...], lng_ref[...], lnb_ref[...])
        h = jnp.where(pl.program_id(0) == 0, jnp.broadcast_to(hm[None], hx.shape), hx)
        h = h.reshape(hout_ref.shape)
        hout_ref[...] = h
    else:
        h = h_ref[...]
    x = h.astype(BF16)
    za = _dot(x, w_ref[:, 0:ZA_W])
    zkr = _dot(x, w_ref[:, ZA_W + ZHG_W + ZG_W:Z_W])

    def wide_chunk(o_ref, c, off):
        def run():
            o_ref[:, c:c + WIDE_CHUNK] = _dot(x, w_ref[:, off + c:off + c + WIDE_CHUNK]).astype(BF16)
        return run

    fillers = [wide_chunk(zhg_ref, c, ZA_W) for c in range(0, ZHG_W, WIDE_CHUNK)]
    fillers += [wide_chunk(zg_ref, c, ZA_W + ZHG_W) for c in range(0, ZG_W, WIDE_CHUNK)]
    for _ in range(FILL_BEFORE_MLA):
        fillers.pop(0)()
    cos, sin = _rope_tables(pos_ref, inv_ref, place_ref, base_ref)
    c0 = S5_WIDTH
    _mla_prep(za[:, c0:c0 + MLA_Q_RANK], za[:, c0 + MLA_Q_RANK:c0 + MLA_Q_RANK + MLA_KV_RANK], zkr,
              cos, sin, qn_ref, kvn_ref, wq_ref, wk_ref, wv_ref, q_ref, k_ref, v_ref)
    _s5_block(za[:, 0:S5_WIDTH].astype(BF16), perm_ref, bm_ref, cm_ref, ar_ref, ai_ref, d_ref, wglu_ref, ys5_ref,
              xr_ref, xi_ref, buf_ref, y_ref, nb, fillers)


def _front(src, w, pos, rope, qn, kvn, wq, wk, wv, perm, bm, cm, ar, ai, dskip, wglu, nb, li):
    ln_in = isinstance(src, tuple)
    rb = T_BLK * nb
    if ln_in:
        x, meta, lng, lnb = src
        d = x.shape[-1]
        r = (x.shape[1] + N_META) * nb
        src_specs = [pl.BlockSpec((nb, T_BLK, d), lambda i: (0, jnp.maximum(i - 1, 0), 0)),
                     _const_spec((N_META, d)), _const_spec((1, d)), _const_spec((1, d))]
    else:
        src = (src,)
        r, d = src[0].shape
        src_specs = [pl.BlockSpec((rb, d), lambda i: (i, 0))]
    hw = MLA_HEADS * HEAD_PAD
    vw = MLA_HEADS * MLA_V
    n_slab = S5_WIDTH // S5_SLAB
    row = lambda width: pl.BlockSpec((rb, width), lambda i: (i, 0))
    out_shape = [jax.ShapeDtypeStruct((r, ZALL_W), BF16)]
    out_specs = [row(ZALL_W)]
    if ln_in:
        out_shape.insert(0, jax.ShapeDtypeStruct((r, d), F32))
        out_specs.insert(0, row(d))
    return pl.pallas_call(
        functools.partial(_front_kernel, nb=nb, ln_in=ln_in),
        out_shape=out_shape,
        grid=(r // rb,),
        in_specs=src_specs + [
            _const_spec((d, Z_W), li), pl.BlockSpec((None, 1, rb), lambda i: (i, 0, 0)),
            _const_spec((MLA_ROPE // 2, 1)), _const_spec((MLA_ROPE // 2, HEAD_PAD)), _const_spec((1, HEAD_PAD)),
            _const_spec((1, MLA_Q_RANK), li), _const_spec((1, MLA_KV_RANK), li),
            _const_spec((MLA_Q_RANK, hw), li), _const_spec((MLA_KV_RANK, hw), li), _const_spec((MLA_KV_RANK, vw), li),
            _const_spec((rb, rb)),
            _const_spec((n_slab, S5_SLAB, 2 * S5_SLAB_STATE), li),
            _const_spec((n_slab, 2 * S5_SLAB_STATE, S5_SLAB), li),
            _const_spec((n_slab, 1, S5_SLAB_STATE), li),
            _const_spec((n_slab, 1, S5_SLAB_STATE), li),
            _const_spec((1, S5_WIDTH), li),
            _const_spec((S5_WIDTH, S5_WIDTH), li),
        ],
        out_specs=out_specs,
        scratch_shapes=[
            pltpu.VMEM((n_slab, nb, S5_SLAB_STATE), F32),
            pltpu.VMEM((n_slab, nb, S5_SLAB_STATE), F32),
            pltpu.VMEM((rb, 2 * S5_SLAB_STATE), F32),
            pltpu.VMEM((rb, S5_WIDTH), F32),
        ],
        compiler_params=_cparams(1, "arbitrary"),
        name="front",
    )(*src, w, pos, *rope, qn, kvn, wq, wk, wv, perm, bm, cm, ar, ai, dskip, wglu)


def _attn_kernel(tab_ref, q_ref, k_ref, v_ref, o_ref, vt_ref, qt_ref, m_ref, acc_ref, *bufs, n_heads, tq, nq):
    xs_ref, xm_ref, ps_ref, pm_ref, a_ref = (bufs[2 * i:2 * i + 2] for i in range(5))
    bpq = tq // T_BLK
    vw = n_heads * MLA_V

    def values_t(vb):
        vt = vb.T
        rows = lax.broadcasted_iota(jnp.int32, (ATT_VE - MLA_V, vt.shape[1]), 0)
        ext = jnp.where(rows == 0, 1.0, 0.0).astype(BF16)
        parts = []
        for h in range(n_heads):
            parts += [vt[h * MLA_V:(h + 1) * MLA_V, :], ext]
        return jnp.concatenate(parts, axis=0)

    for j in range(nq):
        vt_ref[j] = values_t(_load_rows(v_ref, 1 + j * bpq, bpq, slice(0, vw)))
    v0t = values_t(v_ref[0])

    for i in range(nq):
        qb = _load_rows(q_ref, 1 + i * bpq, bpq, slice(0, n_heads * HEAD_PAD))
        qt_ref[i] = qb.T

    def causal(st):
        r = lax.broadcasted_iota(jnp.int32, st.shape, 0)
        c = lax.broadcasted_iota(jnp.int32, st.shape, 1)
        return jnp.where(r <= c, st, MASK_VALUE * LOG2_E)

    heads = range(n_heads)
    qs = [slice(h * HEAD_PAD, (h + 1) * HEAD_PAD) for h in heads]
    vs = [slice(h * ATT_VE, (h + 1) * ATT_VE) for h in heads]
    k0 = [k_ref[0, :, c] for c in qs]

    def normalised(acc):
        return acc[0:MLA_V, :] / acc[MLA_V:MLA_V + 1, :]

    outs = []
    for n in heads:
        st = causal(_dot_nt(k0[n], q_ref[0, :, qs[n]]))
        p = jnp.exp2(st - jnp.max(st, axis=0, keepdims=True))
        outs.append(normalised(_dot(v0t[vs[n], :], p.astype(BF16))))
    o_ref[0, :, :] = jnp.concatenate(outs, axis=0).T.astype(BF16)

    def key_tile(j, n):
        return _load_rows(k_ref, 1 + j * bpq, bpq, qs[n])

    def diag_scores(i, s):
        for n in heads:
            xs_ref[s][n] = _dot(key_tile(i, n), qt_ref[i, qs[n], :])
            xm_ref[s][n] = _dot(k0[n], qt_ref[i, qs[n], :])

    def diag_stats(i, s):
        for n in heads:
            st, sm = causal(xs_ref[s][n]), xm_ref[s][n]
            m = jnp.maximum(jnp.max(st, axis=0, keepdims=True), jnp.max(sm, axis=0, keepdims=True))
            p = jnp.exp2(st - m)
            pm = jnp.exp2(sm - m)
            m_ref[i, n] = m
            ps_ref[s][n] = p.astype(BF16)
            pm_ref[s][n] = pm.astype(BF16)

    def diag_values(i, s):
        for n in heads:
            acc_ref[i, vs[n], :] = _dot(vt_ref[i, vs[n], :], ps_ref[s][n]) + _dot(v0t[vs[n], :], pm_ref[s][n])

    def off_scores(f, s):
        i, j = tab_ref[0, f], tab_ref[1, f]
        for n in heads:
            xs_ref[s][n] = _dot(key_tile(j, n), qt_ref[i, qs[n], :])

    def off_stats(f, s):
        i = tab_ref[0, f]
        for n in heads:
            x = xs_ref[s][n]
            m_old = m_ref[i, n]
            m = jnp.maximum(m_old, jnp.max(x, axis=0, keepdims=True))
            a = jnp.exp2(m_old - m)
            p = jnp.exp2(x - m)
            m_ref[i, n] = m
            a_ref[s][n] = a
            ps_ref[s][n] = p.astype(BF16)

    def off_values(f, s):
        i, j = tab_ref[0, f], tab_ref[1, f]
        for n in heads:
            acc_ref[i, vs[n], :] = a_ref[s][n] * acc_ref[i, vs[n], :] + _dot(vt_ref[j, vs[n], :], ps_ref[s][n])

    n_items = nq + nq * (nq - 1) // 2

    def pick(diag_stage, off_stage):
        return lambda t, s: diag_stage(t, s) if t < nq else off_stage(t - nq, s)

    _pipeline3(n_items, pick(diag_scores, off_scores), pick(diag_stats, off_stats), pick(diag_values, off_values),
               steps_per_body=n_items + n_items % 2)

    for i in range(nq):
        o = jnp.concatenate([normalised(acc_ref[i, vs[n], :]) for n in heads], axis=0)
        _store_rows(o_ref, 1 + i * bpq, bpq, slice(0, vw), o.astype(BF16).T)


def _pipeline3(n, stage1, stage2, stage3, n_slots=2, steps_per_body=None):
    if n == 0:
        return
    stage1(0, 0)
    if n > 1:
        stage1(1, 1 % n_slots)
    stage2(0, 0)

    def step(t, k):
        stage1(t + 2, (k + 2) % n_slots)
        stage2(t + 1, (k + 1) % n_slots)
        stage3(t, k)

    steady = max(n - 2, 0)
    spb = steps_per_body or n_slots
    trips = steady // spb
    if trips:
        def body(u, carry):
            for k in range(spb):
                step(spb * u + k, k % n_slots)
            return carry

        lax.fori_loop(0, trips, body, 0)
    for t in range(trips * spb, steady):
        step(t, t % n_slots)
    if n > 1:
        stage3(n - 2, (n - 2) % n_slots)
        stage2(n - 1, (n - 1) % n_slots)
    stage3(n - 1, (n - 1) % n_slots)


def _off_diagonal_order(nq):
    left = [(i, j) for i in range(nq) for j in range(i)]
    order = []
    while left:
        count = {}
        for i, _ in left:
            count[i] = count.get(i, 0) + 1
        ok = [p for p in left if not order or p[0] != order[-1][0]] or left
        pick = max(ok, key=lambda p: (count[p[0]], -p[1]))
        order.append(pick)
        left.remove(pick)
    return order or [(0, 0)]


def _attention(zall, nb, nblk):
    z4 = zall.reshape(nblk, nb, T_BLK, ZALL_W)
    hps = ATT_HEADS_PER_STEP
    n_hp = MLA_HEADS // hps
    qw = hps * HEAD_PAD
    vw = hps * MLA_V
    s = (nblk - 1) * T_BLK
    tq = min(ATT_TQ, s)
    nq = s // tq
    col = lambda b, p: p
    tab = jnp.asarray(np.array(_off_diagonal_order(nq), np.int32).T)
    o = pl.pallas_call(
        functools.partial(_attn_kernel, n_heads=hps, tq=tq, nq=nq),
        out_shape=jax.ShapeDtypeStruct((nblk, nb, T_BLK, MLA_HEADS * MLA_V), BF16),
        grid=(nb, n_hp),
        in_specs=[pl.BlockSpec(memory_space=pltpu.SMEM),
                  _seq_spec(nblk, qw, lambda b, p: OFF_Q // qw + p),
                  _seq_spec(nblk, qw, lambda b, p: OFF_K // qw + p),
                  _seq_spec(nblk, vw, lambda b, p: OFF_V // vw + p)],
        out_specs=_seq_spec(nblk, vw, col),
        scratch_shapes=[pltpu.VMEM((nq, hps * ATT_VE, tq), BF16), pltpu.VMEM((nq, qw, tq), BF16),
                        pltpu.VMEM((nq, hps, 1, tq), F32), pltpu.VMEM((nq, hps * ATT_VE, tq), F32)]
        + 2 * [pltpu.VMEM((hps, tq, tq), F32)] + 2 * [pltpu.VMEM((hps, N_META, tq), F32)]
        + 2 * [pltpu.VMEM((hps, tq, tq), BF16)] + 2 * [pltpu.VMEM((hps, N_META, tq), BF16)]
        + 2 * [pltpu.VMEM((hps, 1, tq), F32)],
        compiler_params=_cparams(2),
        name="mla_attn",
    )(tab, z4, z4, z4)
    return o.reshape(nblk * nb * T_BLK, MLA_HEADS * MLA_V)


def _time_major_perm(nb):
    p = np.zeros((T_BLK * nb, T_BLK * nb), np.float32)
    for b in range(nb):
        for t in range(T_BLK):
            p[t * nb + b, b * T_BLK + t] = 1.0
    return p


def _block_row_bcast(x, m, row):
    t, c = x.shape
    if m == t:
        return jnp.broadcast_to(x[row:row + 1, :], x.shape)
    x3 = x.reshape(t // m, m, c)
    return jnp.broadcast_to(x3[:, row:row + 1, :], x3.shape).reshape(t, c)


def _hgrn_gates(blk0, n_blk, f_ref, lb_ref):
    lb = lb_ref[...]
    zf = _load_rows(f_ref, blk0, n_blk, slice(0, HG_QK)).astype(F32)
    e = jnp.exp(-jnp.abs(zf))
    rcp = 1.0 / (1.0 + e)
    pos = zf >= 0.0
    sig_p = jnp.where(pos, rcp, e * rcp)
    sig_n = jnp.where(pos, e * rcp, rcp)
    f = lb + (1.0 - lb) * sig_p
    log_f = jnp.log2(jnp.maximum(f, HG_F_MIN))
    k = (1.0 - lb) * sig_n
    hi = log_f.astype(BF16)
    lo = (log_f - hi.astype(F32)).astype(BF16)
    return k, hi, lo


def _hgrn_mix(blk0, n_blk, gates, q_ref, v_ref, tri_ref, lvl_ref, st_ref):
    t = n_blk * T_BLK
    cs = slice(0, HG_QK)
    heads = [slice(h * HG_KEY, (h + 1) * HG_KEY) for h in range(HG_HEADS)]
    k, hi, lo = gates
    q = _load_rows(q_ref, blk0, n_blk, cs).astype(F32)
    v = _load_rows(v_ref, blk0, n_blk, cs)
    tri = tri_ref[0:t, 0:t]
    cum = _dot(tri, hi) + _dot(tri, lo)
    lvl = lvl_ref[0:t, 0:t]
    r_idx = lax.broadcasted_iota(jnp.int32, (t, HG_QK), 0)
    c8 = _block_row_bcast(cum, HG_BOTTOM, HG_BOTTOM // 2 - 1)
    qe = (q * jnp.exp2(cum - c8)).astype(BF16)
    ke = (k * jnp.exp2(c8 - cum)).astype(BF16)
    n_lvl = len(HG_LEVELS)
    scores = [jnp.where(lvl == n_lvl, _dot_nt(qe[:, hs], ke[:, hs]), 0.0) for hs in heads]
    for li, m in enumerate(HG_LEVELS):
        if m > t:
            continue
        half = m // 2
        cmid = _block_row_bcast(cum, m, half - 1)
        upper = (r_idx & (m - 1)) >= half
        ex = jnp.exp2(jnp.where(upper, cum - cmid, cmid - cum))
        qe = jnp.where(upper, q * ex, 0.0).astype(BF16)
        ke = jnp.where(upper, 0.0, k * ex).astype(BF16)
        scores = [jnp.where(lvl == li, _dot_nt(qe[:, hs], ke[:, hs]), s) for s, hs in zip(scores, heads)]
    qd = (q * jnp.exp2(cum)).astype(BF16)
    last = cum[t - 1:t, :]
    kd = (k * jnp.exp2(last - cum)).astype(BF16)
    dec = jnp.exp2(last)
    outs = []
    for h, hs in enumerate(heads):
        st = st_ref[h]
        outs.append(_dot(scores[h].astype(BF16), v[:, hs]) + _dot_nt(qd[:, hs], st.astype(BF16)))
        st_ref[h] = st * dec[:, hs] + _dot_tn(v[:, hs], kd[:, hs])
    return tuple(outs)


def _hgrn_out(blk0, n_blk, outs, g_ref, on_ref, o_ref):
    cs = slice(0, HG_VW)
    g = _load_rows(g_ref, blk0, n_blk, cs).astype(F32)
    o = jnp.concatenate([o * lax.rsqrt(jnp.mean(o * o, axis=-1, keepdims=True) + 1e-6) for o in outs], axis=1)
    _store_rows(o_ref, blk0, n_blk, cs, (o * on_ref[...] * (g * _sigmoid(g))).astype(BF16))


def _hgrn_kernel(q_ref, f_ref, v_ref, g_ref, lb_ref, on_ref, tri_ref, lvl_ref, o_ref, st_ref, *bufs, n_chunks):
    kb_ref, hi_ref, lo_ref, ob_ref = (bufs[HG_SLOTS * i:HG_SLOTS * (i + 1)] for i in range(4))
    bpc = HG_CHUNK // T_BLK
    heads = [slice(h * HG_VAL, (h + 1) * HG_VAL) for h in range(HG_HEADS)]
    gates = functools.partial(_hgrn_gates, f_ref=f_ref, lb_ref=lb_ref)
    mix = functools.partial(_hgrn_mix, q_ref=q_ref, v_ref=v_ref, tri_ref=tri_ref, lvl_ref=lvl_ref, st_ref=st_ref)
    out = functools.partial(_hgrn_out, g_ref=g_ref, on_ref=on_ref, o_ref=o_ref)
    st_ref[...] = jnp.zeros_like(st_ref)
    out(0, 1, mix(0, 1, gates(0, 1)))

    def stage_gates(c, s):
        k, hi, lo = gates(1 + c * bpc, bpc)
        kb_ref[s][...], hi_ref[s][...], lo_ref[s][...] = k, hi.astype(F32), lo.astype(F32)

    def stage_mix(c, s):
        outs = mix(1 + c * bpc, bpc, (kb_ref[s][...], hi_ref[s][...].astype(BF16), lo_ref[s][...].astype(BF16)))
        for hs, o in zip(heads, outs):
            ob_ref[s][:, hs] = o

    def stage_out(c, s):
        out(1 + c * bpc, bpc, tuple(ob_ref[s][:, hs] for hs in heads))

    _pipeline3(n_chunks, stage_gates, stage_mix, stage_out, n_slots=HG_SLOTS)


def _hgrn(zall, lb, onorm, tri, lvl, nb, nblk, li):
    z4 = zall.reshape(nblk, nb, T_BLK, ZALL_W)
    col = lambda j: _seq_spec(nblk, HG_QK, lambda b: OFF_HG // HG_QK + j)
    o = pl.pallas_call(
        functools.partial(_hgrn_kernel, n_chunks=(nblk - 1) * T_BLK // HG_CHUNK),
        out_shape=jax.ShapeDtypeStruct((nblk, nb, T_BLK, HG_VW), BF16),
        grid=(nb,),
        in_specs=[col(0), col(1), col(2), col(3),
                  _const_spec((1, HG_QK), li), _const_spec((1, HG_VW), li),
                  _const_spec((HG_CHUNK, HG_CHUNK)), _const_spec((HG_CHUNK, HG_CHUNK))],
        out_specs=_seq_spec(nblk, HG_VW, lambda b: 0),
        scratch_shapes=[pltpu.VMEM((HG_HEADS, HG_VAL, HG_KEY), F32)]
        + 3 * HG_SLOTS * [pltpu.VMEM((HG_CHUNK, HG_QK), F32)]
        + HG_SLOTS * [pltpu.VMEM((HG_CHUNK, HG_VW), F32)],
        compiler_params=_cparams(1),
        name="hgrn2",
    )(z4, z4, z4, z4, lb, onorm, tri, lvl)
    return o.reshape(nblk * nb * T_BLK, HG_VW)


def _hgrn_level_matrix():
    r = np.arange(HG_CHUNK)[:, None]
    c = np.arange(HG_CHUNK)[None, :]
    lvl = np.zeros((HG_CHUNK, HG_CHUNK), np.int32)
    for li, m in enumerate(HG_LEVELS):
        lvl = np.where(r // m == c // m, li, lvl)
    lvl = np.where(r // HG_BOTTOM == c // HG_BOTTOM, len(HG_LEVELS), lvl)
    return np.where(c <= r, lvl, -1).astype(np.int32)


def _back_kernel(om_ref, os_ref, oh_ref, gm_ref, gs_ref, gh_ref, h_ref,
                 wm_ref, ws_ref, wh_ref, wo_ref, g1_ref, b1_ref,
                 wg_ref, wu_ref, wd_ref, g2_ref, b2_ref, o_ref, r1_ref, r2_ref, *, alpha):
    @pl.when(pl.program_id(0) == 0)
    def _():
        r1_ref[...] = jnp.zeros_like(r1_ref)
        r2_ref[...] = jnp.zeros_like(r2_ref)

    r1 = r1_ref[...]
    r2 = r2_ref[...]
    ym = _dot(om_ref[...], wm_ref[...])
    ys = _dot(os_ref[...], ws_ref[...])
    yh = _dot(oh_ref[...], wh_ref[...])
    o_ref[...] = _layer_norm(r2, g2_ref[...], b2_ref[...]).reshape(o_ref.shape)
    h1 = _layer_norm(r1, g1_ref[...], b1_ref[...])
    hb = h1.astype(BF16)
    a = _dot(hb, wg_ref[...])
    u = _dot(hb, wu_ref[...])
    mixed = _sigmoid(gm_ref[...].astype(F32)) * ym
    mixed += _sigmoid(gs_ref[...].astype(F32)) * ys
    mixed += _sigmoid(gh_ref[...].astype(F32)) * yh
    r1_ref[...] = alpha * h_ref[...] + _dot(mixed.astype(BF16), wo_ref[...])
    r2_ref[...] = alpha * h1 + _dot((a * _sigmoid(a) * u).astype(BF16), wd_ref[...])


def _back(om, oh, zall, h, wm, ws, wh, wo, g1, b1, wg, wu, wd, g2, b2, nb, alpha, final, li):
    r, d = h.shape
    bw = om.shape[1]
    dff = wg.shape[-1]
    rb = T_BLK * nb
    skip = 1 if final else 0
    n_blk = r // rb - skip
    lag = 2
    row = lambda width, j=0: pl.BlockSpec((rb, width), lambda i: (jnp.minimum(i, n_blk - 1) + skip, j))
    if final:
        out_shape = jax.ShapeDtypeStruct((nb, n_blk * T_BLK, d), F32)
        out_spec = pl.BlockSpec((nb, T_BLK, d), lambda i: (0, jnp.maximum(i - lag, 0), 0))
    else:
        out_shape = jax.ShapeDtypeStruct((r, d), F32)
        out_spec = pl.BlockSpec((rb, d), lambda i: (jnp.maximum(i - lag, 0), 0))
    return pl.pallas_call(
        functools.partial(_back_kernel, alpha=alpha),
        out_shape=out_shape,
        grid=(n_blk + lag,),
        in_specs=[row(bw), row(bw, OFF_Y // bw), row(bw),
                  row(d, OFF_G // d), row(d, OFF_G // d + 1), row(d, OFF_G // d + 2), row(d),
                  _const_spec((bw, d), li), _const_spec((bw, d), li), _const_spec((bw, d), li),
                  _const_spec((d, d), li), _const_spec((1, d), li), _const_spec((1, d), li),
                  _const_spec((d, dff), li), _const_spec((d, dff), li), _const_spec((dff, d), li),
                  _const_spec((1, d), li), _const_spec((1, d), li)],
        out_specs=out_spec,
        scratch_shapes=[pltpu.VMEM((rb, d), F32), pltpu.VMEM((rb, d), F32)],
        compiler_params=_cparams(1, "arbitrary"),
        name="merge_ffn",
    )(om, zall, oh, zall, zall, zall, h, wm, ws, wh, wo, g1, b1, wg, wu, wd, g2, b2)


def _permute_w_in_kernel(w_ref, o_ref):
    lat = MLA_Q_RANK + MLA_KV_RANK
    s5_0 = lat + MLA_ROPE
    hg_0 = s5_0 + S5_WIDTH
    wide = ZHG_W + ZG_W
    o_ref[:, 0:S5_WIDTH] = w_ref[:, s5_0:hg_0].astype(BF16)
    o_ref[:, S5_WIDTH:ZA_W] = w_ref[:, 0:lat].astype(BF16)
    o_ref[:, ZA_W:ZA_W + wide] = w_ref[:, hg_0:hg_0 + wide].astype(BF16)
    o_ref[:, ZA_W + wide:Z_W] = jnp.zeros((o_ref.shape[0], ZKR_W), BF16)
    o_ref[:, ZA_W + wide + MLA_NOPE:ZA_W + wide + MLA_NOPE + MLA_ROPE] = w_ref[:, lat:s5_0].astype(BF16)


def _permute_w_in(w):
    depth, d, d_in = w.shape
    rows = 128
    return pl.pallas_call(
        _permute_w_in_kernel,
        out_shape=jax.ShapeDtypeStruct((depth, d, Z_W), BF16),
        grid=(depth, d // rows),
        in_specs=[pl.BlockSpec((None, rows, d_in), lambda l, i: (l, i, 0))],
        out_specs=pl.BlockSpec((None, rows, Z_W), lambda l, i: (l, i, 0)),
        compiler_params=_cparams(2),
        name="permute_w_in",
    )(w)


def _mla_weights(w_uq, w_ukv):
    rq, rkv = w_uq.shape[0], w_ukv.shape[0]
    zpad = HEAD_PAD - MLA_NOPE - MLA_ROPE
    wq = w_uq.reshape(rq, MLA_HEADS, MLA_NOPE + MLA_ROPE)
    q_nope, q_rope = wq[..., :MLA_NOPE], wq[..., MLA_NOPE:]
    zq = jnp.zeros((rq, MLA_HEADS, zpad), w_uq.dtype)
    wq_p = jnp.concatenate([q_nope, q_rope, zq], axis=-1).reshape(rq, -1)
    wkv = w_ukv.reshape(rkv, MLA_HEADS, MLA_NOPE + MLA_V)
    zk = jnp.zeros((rkv, MLA_HEADS, HEAD_PAD - MLA_NOPE), w_ukv.dtype)
    wk_p = jnp.concatenate([wkv[..., :MLA_NOPE], zk], axis=-1).reshape(rkv, -1)
    wv = wkv[..., MLA_NOPE:].reshape(rkv, -1)
    return [a.astype(BF16) for a in (wq_p, wk_p, wv)]


def _rope_consts():
    half = MLA_ROPE // 2
    inv = ROPE_THETA ** (-(jnp.arange(0, MLA_ROPE, 2, dtype=F32) / MLA_ROPE))
    place = np.zeros((half, HEAD_PAD), np.float32)
    place[np.arange(half), MLA_NOPE + np.arange(half)] = 1.0
    place[np.arange(half), MLA_NOPE + half + np.arange(half)] = 1.0
    base = np.zeros((1, HEAD_PAD), np.float32)
    base[0, :MLA_NOPE] = 1.0
    return inv[:, None], jnp.asarray(place, BF16), jnp.asarray(base)


def _s5_params(lam_re, lam_im, log_dt, b_re, b_im, c_re, c_im):
    lr = jnp.minimum(lam_re.astype(F32), -1e-4)
    li = lam_im.astype(F32)
    dt = jnp.exp(log_dt.astype(F32))[:, None]
    mag = jnp.exp(lr * dt)
    ab_r = mag * jnp.cos(li * dt)
    ab_i = mag * jnp.sin(li * dt)
    den = lr * lr + li * li
    nr = ab_r - 1.0
    coef_r = ((nr * lr + ab_i * li) / den)[..., None]
    coef_i = ((ab_i * lr - nr * li) / den)[..., None]
    bb_r = coef_r * b_re.astype(F32) - coef_i * b_im.astype(F32)
    bb_i = coef_r * b_im.astype(F32) + coef_i * b_re.astype(F32)
    n_slab = S5_WIDTH // S5_SLAB
    gps = S5_SLAB // S5_GROUP
    eye = jnp.eye(gps, dtype=F32)

    def in_mat(bb):
        b4 = bb.reshape(n_slab, gps, S5_STATE, S5_GROUP)
        return jnp.einsum('jgnc,gh->jgchn', b4, eye).reshape(n_slab, S5_SLAB, S5_SLAB_STATE)

    def out_mat(cc):
        c4 = cc.astype(F32).reshape(n_slab, gps, S5_GROUP, S5_STATE)
        return jnp.einsum('jgcn,gh->jgnhc', c4, eye).reshape(n_slab, S5_SLAB_STATE, S5_SLAB)

    bm = jnp.concatenate([in_mat(bb_r), in_mat(bb_i)], axis=2).astype(BF16)
    cm = jnp.concatenate([out_mat(c_re), -out_mat(c_im)], axis=1).astype(BF16)
    ar = ab_r.reshape(n_slab, 1, S5_SLAB_STATE)
    ai = ab_i.reshape(n_slab, 1, S5_SLAB_STATE)
    return bm, cm, ar, ai


def kernel(x, positions, meta_tokens, ln_in_g, ln_in_b, w_in, mla_q_norm, mla_w_uq, mla_kv_norm, mla_w_ukv,
           s5_lam_re, s5_lam_im, s5_log_dt, s5_b_re, s5_b_im, s5_c_re, s5_c_im, s5_d, s5_w_glu,
           hg_lb_logits, hg_out_norm, w_br_mla, w_br_s5, w_br_hg, w_out, ln1_g, ln1_b,
           w_ffn_gate, w_ffn_up, w_ffn_down, ln2_g, ln2_b):
    nb, s, d = x.shape
    depth = w_in.shape[0]
    assert N_META == T_BLK and s % HG_CHUNK == 0 and s % min(ATT_TQ, s) == 0, (nb, s, d)
    assert meta_tokens.shape == (N_META, d) and positions.shape == (nb, s), (meta_tokens.shape, positions.shape)
    nblk = (s + N_META) // T_BLK
    alpha = (2 * depth) ** 0.25
    row2 = lambda a: a.astype(F32)[None, :]
    rows3 = lambda a: a.astype(F32)[:, None, :]
    bf16 = lambda a: a.astype(BF16)

    meta_pos = jnp.broadcast_to(jnp.arange(N_META, dtype=jnp.int32)[None, :], (nb, N_META))
    pos = jnp.concatenate([meta_pos, positions.astype(jnp.int32) + N_META], axis=1)
    pos = pos.reshape(nb, nblk, T_BLK).transpose(1, 0, 2).reshape(nblk, 1, nb * T_BLK)
    rope = _rope_consts()
    p_lb = jax.nn.softmax(hg_lb_logits.astype(F32), axis=0)
    lower_bounds = jnp.cumsum(p_lb, axis=0) - p_lb[0]
    tri = jnp.asarray(np.tril(np.ones((HG_CHUNK, HG_CHUNK), np.float32)), BF16)
    lvl = jnp.asarray(_hgrn_level_matrix())
    perm = jnp.asarray(_time_major_perm(nb), BF16)

    front_p = (_permute_w_in(w_in), pos, rope, rows3(mla_q_norm), rows3(mla_kv_norm),
               *jax.vmap(_mla_weights)(mla_w_uq, mla_w_ukv), perm,
               *jax.vmap(_s5_params)(s5_lam_re, s5_lam_im, s5_log_dt, s5_b_re, s5_b_im, s5_c_re, s5_c_im),
               rows3(s5_d), bf16(s5_w_glu))
    hgrn_p = (lower_bounds[:, None, :], rows3(hg_out_norm), tri, lvl)
    back_p = (bf16(w_br_mla), bf16(w_br_s5), bf16(w_br_hg), bf16(w_out), rows3(ln1_g), rows3(ln1_b),
              bf16(w_ffn_gate), bf16(w_ffn_up), bf16(w_ffn_down), rows3(ln2_g), rows3(ln2_b))

    h = (x, meta_tokens.astype(x.dtype), row2(ln_in_g), row2(ln_in_b))
    for li in range(depth):
        outs = _front(h, *front_p, nb, li)
        if li == 0:
            h, *outs = outs
        zall, = outs
        o_mla = _attention(zall, nb, nblk)
        o_hg = _hgrn(zall, *hgrn_p, nb, nblk, li)
        h = _back(o_mla, o_hg, zall, h, *back_p, nb, alpha, final=li == depth - 1, li=li)
    return h
```
